```python
import jax, jax.numpy as jnp
from jax import lax
import numpy as np

D_MODEL = 2048
BATCH = 8
SEQ = 8192
DEPTH = 1

ATTN_WIDTH = D_MODEL // 2
ATTN_HEAD_DIM = 128
ATTN_HEADS = ATTN_WIDTH // ATTN_HEAD_DIM
DILATED_PATTERNS = ((128, 1), (512, 4), (2048, 16))
ROPE_THETA = 500000.0
ROPE_DIM = ATTN_HEAD_DIM // 4

GLA_VALUE_WIDTH = D_MODEL - ATTN_WIDTH
GLA_KEY_WIDTH = GLA_VALUE_WIDTH // 2
GLA_HEADS = 4
GLA_DK = GLA_KEY_WIDTH // GLA_HEADS
GLA_DV = GLA_VALUE_WIDTH // GLA_HEADS
GLA_GATE_RANK = 16
GLA_GATE_NORMALIZER = 16.0
GLA_CHUNK = 64

IN_SPLITS = (ATTN_WIDTH, ATTN_WIDTH, ATTN_WIDTH,
             GLA_KEY_WIDTH, GLA_KEY_WIDTH, GLA_VALUE_WIDTH, GLA_VALUE_WIDTH,
             GLA_GATE_RANK, GLA_GATE_RANK)
IN_WIDTH = sum(IN_SPLITS)

D_FF = 5632
CONV_WIDTH = 3
EPS = 1e-6

kernel_name = "hymba_gla_dilated_attn_convglu_encoder"


def rmsnorm(x, g):
    xf = x.astype(jnp.float32)
    y = xf * lax.rsqrt(jnp.mean(xf * xf, axis=-1, keepdims=True) + EPS)
    return (y * g.astype(jnp.float32)).astype(x.dtype)


def apply_partial_rope(t):
    S = t.shape[1]
    pos = jnp.arange(S, dtype=jnp.float32)
    inv_freq = ROPE_THETA ** (-jnp.arange(0, ROPE_DIM, 2, dtype=jnp.float32) / ROPE_DIM)
    ang = pos[:, None] * inv_freq[None, :]
    cos = jnp.cos(ang)[None, :, None, :]
    sin = jnp.sin(ang)[None, :, None, :]
    tr = t[..., :ROPE_DIM].astype(jnp.float32)
    x1, x2 = tr[..., :ROPE_DIM // 2], tr[..., ROPE_DIM // 2:]
    rot = jnp.concatenate([x1 * cos - x2 * sin, x2 * cos + x1 * sin], axis=-1)
    return jnp.concatenate([rot.astype(t.dtype), t[..., ROPE_DIM:]], axis=-1)


def to_residues(t, d):
    B, S = t.shape[:2]
    rest = t.shape[2:]
    t = t.reshape((B, S // d, d) + rest)
    return jnp.swapaxes(t, 1, 2).reshape((B * d, S // d) + rest)


def from_residues(t, d, B):
    N, L = t.shape[:2]
    rest = t.shape[2:]
    t = t.reshape((B, d, L) + rest)
    return jnp.swapaxes(t, 1, 2).reshape((B, L * d) + rest)


def banded_attention(q, k, v, n_side):
    N, L, H, Dh = q.shape
    blk = n_side
    nb = -(-L // blk)
    Lp = nb * blk
    pad = Lp - L
    qb = jnp.pad(q, ((0, 0), (0, pad), (0, 0), (0, 0))).reshape(N, nb, blk, H, Dh)
    kp = jnp.pad(k, ((0, 0), (blk, pad + blk), (0, 0), (0, 0)))
    vp = jnp.pad(v, ((0, 0), (blk, pad + blk), (0, 0), (0, 0)))

    def key_blocks(t):
        return jnp.concatenate(
            [t[:, i * blk:i * blk + Lp].reshape(N, nb, blk, H, Dh) for i in range(3)], axis=2)

    kb, vb = key_blocks(kp), key_blocks(vp)
    s = jnp.einsum('nbqhd,nbkhd->nbhqk', qb, kb).astype(jnp.float32) * (Dh ** -0.5)
    qpos = jnp.arange(nb)[:, None] * blk + jnp.arange(blk)[None, :]
    kpos = jnp.arange(nb)[:, None] * blk - blk + jnp.arange(3 * blk)[None, :]
    qp_, kp_ = qpos[:, :, None], kpos[:, None, :]
    mask = (jnp.abs(kp_ - qp_) <= n_side) & (kp_ >= 0) & ((kp_ < L) | (kp_ == qp_))
    s = jnp.where(mask[None, :, None], s, -jnp.inf)
    m = jnp.max(s, axis=-1, keepdims=True)
    p = jnp.exp(s - m)
    den = jnp.sum(p, axis=-1, keepdims=True)
    lse = (m + jnp.log(den))[..., 0]
    p = p / den
    o = jnp.einsum('nbhqk,nbkhd->nbqhd', p.astype(vb.dtype), vb).reshape(N, Lp, H, Dh)[:, :L]
    lse = jnp.swapaxes(lse, 2, 3).reshape(N, Lp, H)[:, :L]
    return o, lse


def dilated_attention(q, k, v):
    B = q.shape[0]
    outs, lses = [], []
    for window, d in DILATED_PATTERNS:
        n_side = (window // 2) // d
        o, lse = banded_attention(to_residues(q, d), to_residues(k, d), to_residues(v, d), n_side)
        outs.append(from_residues(o, d, B).astype(jnp.float32))
        lses.append(from_residues(lse, d, B))
    w = jax.nn.softmax(jnp.stack(lses), axis=0)
    o = jnp.einsum('pbsh,pbshd->bshd', w, jnp.stack(outs))
    return o.astype(q.dtype)


def gla_chunked(q, k, v, g):
    B, S, H, K = q.shape
    V = v.shape[-1]
    C = GLA_CHUNK
    n = S // C
    q, k, v, g = (t.reshape(B, n, C, H, t.shape[-1]) for t in (q, k, v, g))
    b = jnp.cumsum(g, axis=2)
    b_ref = b[:, :, C // 2 - 1:C // 2]
    b_last = b[:, :, -1]
    a = jnp.einsum('bnihk,bnjhk->bnhij', q * jnp.exp(b - b_ref), k * jnp.exp(b_ref - b))
    a = jnp.where(jnp.tril(jnp.ones((C, C), dtype=bool)), a, 0.0)
    o_intra = jnp.einsum('bnhij,bnjhv->bnihv', a, v)
    q_in = q * jnp.exp(b)
    k_st = k * jnp.exp(b_last[:, :, None] - b)
    dec = jnp.exp(b_last)

    def step(state, xs):
        qc, kc, vc, dc = xs
        o = jnp.einsum('bihk,bhkv->bihv', qc, state)
        state = dc[..., None] * state + jnp.einsum('bjhk,bjhv->bhkv', kc, vc)
        return state, o

    xs = tuple(jnp.moveaxis(t, 1, 0) for t in (q_in, k_st, v, dec))
    _, o_inter = lax.scan(step, jnp.zeros((B, H, K, V), jnp.float32), xs)
    o = o_intra + jnp.moveaxis(o_inter, 0, 1)
    return o.reshape(B, S, H, V)


def hybrid_mixer(h, w_in, gf_up, gf_b, gb_up, gb_b, gla_norm_g, attn_norm_g, w_out):
    B, S, _ = h.shape
    f32 = jnp.float32
    proj = h @ w_in
    split_at = np.cumsum(IN_SPLITS)[:-1].tolist()
    aq, ak, av, gq, gk, gv, gr, zf, zb = jnp.split(proj, split_at, axis=-1)

    def heads(t, nh):
        return t.reshape(B, S, nh, -1)

    aq = apply_partial_rope(heads(aq, ATTN_HEADS))
    ak = apply_partial_rope(heads(ak, ATTN_HEADS))
    ao = dilated_attention(aq, ak, heads(av, ATTN_HEADS)).reshape(B, S, ATTN_WIDTH)
    ao = rmsnorm(ao, attn_norm_g)

    q = heads(gq, GLA_HEADS).astype(f32) * (GLA_DK ** -0.5)
    k = heads(gk, GLA_HEADS).astype(f32)
    v = heads(gv, GLA_HEADS).astype(f32)
    log_gf = jax.nn.log_sigmoid((zf @ gf_up + gf_b).astype(f32)) / GLA_GATE_NORMALIZER
    log_gb = jax.nn.log_sigmoid((zb @ gb_up + gb_b).astype(f32)) / GLA_GATE_NORMALIZER
    flip = lambda t: jnp.flip(t, axis=1)
    o_f = gla_chunked(q, k, v, heads(log_gf, GLA_HEADS))
    o_b = flip(gla_chunked(flip(q), flip(k), flip(v), flip(heads(log_gb, GLA_HEADS))))
    go = rmsnorm(o_f + o_b, gla_norm_g).astype(h.dtype)
    go = (go * jax.nn.silu(heads(gr, GLA_HEADS))).reshape(B, S, GLA_VALUE_WIDTH)

    return jnp.concatenate([ao, go], axis=-1) @ w_out


def conv_glu_ffn(h, w_gate, w_up, conv_w, conv_b, w_down):
    gate = h @ w_gate
    gate = lax.conv_general_dilated(
        gate, conv_w[:, None, :].astype(gate.dtype), window_strides=(1,),
        padding=((CONV_WIDTH // 2, CONV_WIDTH // 2),),
        dimension_numbers=('NWC', 'WIO', 'NWC'), feature_group_count=D_FF) + conv_b
    return (jax.nn.silu(gate) * (h @ w_up)) @ w_down


def _fwd_setup_inputs(seed: int = 0) -> dict:
    key = jax.random.key(seed)
    ks = jax.random.split(key, 20)
    nrm = lambda k, shape, scale: jax.random.normal(k, shape, jnp.float32) * scale
    gain = lambda k, shape: 1.0 + 0.02 * jax.random.normal(k, shape, jnp.float32)
    L = DEPTH
    return {
        "x": jax.random.normal(ks[0], (BATCH, SEQ, D_MODEL), jnp.float32),
        "norm1_g": gain(ks[1], (L, D_MODEL)),
        "w_in": nrm(ks[2], (L, D_MODEL, IN_WIDTH), D_MODEL ** -0.5),
        "gf_up": nrm(ks[3], (L, GLA_GATE_RANK, GLA_KEY_WIDTH), GLA_GATE_RANK ** -0.5),
        "gf_b": nrm(ks[4], (L, GLA_KEY_WIDTH), 0.1),
        "gb_up": nrm(ks[5], (L, GLA_GATE_RANK, GLA_KEY_WIDTH), GLA_GATE_RANK ** -0.5),
        "gb_b": nrm(ks[6], (L, GLA_KEY_WIDTH), 0.1),
        "gla_norm_g": gain(ks[7], (L, GLA_DV)),
        "attn_norm_g": gain(ks[8], (L, ATTN_WIDTH)),
        "w_out": nrm(ks[9], (L, D_MODEL, D_MODEL), D_MODEL ** -0.5),
        "norm2_g": gain(ks[10], (L, D_MODEL)),
        "w_gate": nrm(ks[11], (L, D_MODEL, D_FF), D_MODEL ** -0.5),
        "w_up": nrm(ks[12], (L, D_MODEL, D_FF), D_MODEL ** -0.5),
        "conv_w": nrm(ks[13], (L, CONV_WIDTH, D_FF), CONV_WIDTH ** -0.5),
        "conv_b": nrm(ks[14], (L, D_FF), 0.02),
        "w_down": nrm(ks[15], (L, D_FF, D_MODEL), D_FF ** -0.5),
        "final_norm_g": gain(ks[16], (D_MODEL,)),
    }


def _fwd_reference(x, norm1_g, w_in, gf_up, gf_b, gb_up, gb_b, gla_norm_g, attn_norm_g, w_out,
              norm2_g, w_gate, w_up, conv_w, conv_b, w_down, final_norm_g):
    h = x
    for l in range(DEPTH):
        h = h + hybrid_mixer(rmsnorm(h, norm1_g[l]), w_in[l], gf_up[l], gf_b[l], gb_up[l], gb_b[l],
                             gla_norm_g[l], attn_norm_g[l], w_out[l])
        h = h + conv_glu_ffn(rmsnorm(h, norm2_g[l]), w_gate[l], w_up[l], conv_w[l], conv_b[l], w_down[l])
    return rmsnorm(h, final_norm_g)


import jax as _jax
import jax.numpy as _jnp

TWIN_FORMAT = 'train_step'
FWD_PARAMS = ['x', 'norm1_g', 'w_in', 'gf_up', 'gf_b', 'gb_up', 'gb_b', 'gla_norm_g', 'attn_norm_g', 'w_out', 'norm2_g', 'w_gate', 'w_up', 'conv_w', 'conv_b', 'w_down', 'final_norm_g']
TWIN_WEIGHTS = ['norm1_g', 'w_in', 'gf_up', 'gf_b', 'gb_up', 'gb_b', 'gla_norm_g', 'attn_norm_g', 'w_out', 'norm2_g', 'w_gate', 'w_up', 'conv_w', 'conv_b', 'w_down', 'final_norm_g']
TWIN_DIFF_INPUT = 'x'
TWIN_INPUTS = ['x', 'norm1_g', 'w_in', 'gf_up', 'gf_b', 'gb_up', 'gb_b', 'gla_norm_g', 'attn_norm_g', 'w_out', 'norm2_g', 'w_gate', 'w_up', 'conv_w', 'conv_b', 'w_down', 'final_norm_g', 'loss_target', 'm_norm1_g', 'm_w_in', 'm_gf_up', 'm_gf_b', 'm_gb_up', 'm_gb_b', 'm_gla_norm_g', 'm_attn_norm_g', 'm_w_out', 'm_norm2_g', 'm_w_gate', 'm_w_up', 'm_conv_w', 'm_conv_b', 'm_w_down', 'm_final_norm_g', 'v_norm1_g', 'v_w_in', 'v_gf_up', 'v_gf_b', 'v_gb_up', 'v_gb_b', 'v_gla_norm_g', 'v_attn_norm_g', 'v_w_out', 'v_norm2_g', 'v_w_gate', 'v_w_up', 'v_conv_w', 'v_conv_b', 'v_w_down', 'v_final_norm_g']
TWIN_OUTPUTS = ['loss', 'grad_x', 'grad_norm1_g', 'grad_w_in', 'grad_gf_up', 'grad_gf_b', 'grad_gb_up', 'grad_gb_b', 'grad_gla_norm_g', 'grad_attn_norm_g', 'grad_w_out', 'grad_norm2_g', 'grad_w_gate', 'grad_w_up', 'grad_conv_w', 'grad_conv_b', 'grad_w_down', 'grad_final_norm_g', 'delta_norm1_g', 'delta_w_in', 'delta_gf_up', 'delta_gf_b', 'delta_gb_up', 'delta_gb_b', 'delta_gla_norm_g', 'delta_attn_norm_g', 'delta_w_out', 'delta_norm2_g', 'delta_w_gate', 'delta_w_up', 'delta_conv_w', 'delta_conv_b', 'delta_w_down', 'delta_final_norm_g', 'new_m_norm1_g', 'new_m_w_in', 'new_m_gf_up', 'new_m_gf_b', 'new_m_gb_up', 'new_m_gb_b', 'new_m_gla_norm_g', 'new_m_attn_norm_g', 'new_m_w_out', 'new_m_norm2_g', 'new_m_w_gate', 'new_m_w_up', 'new_m_conv_w', 'new_m_conv_b', 'new_m_w_down', 'new_m_final_norm_g', 'new_v_norm1_g', 'new_v_w_in', 'new_v_gf_up', 'new_v_gf_b', 'new_v_gb_up', 'new_v_gb_b', 'new_v_gla_norm_g', 'new_v_attn_norm_g', 'new_v_w_out', 'new_v_norm2_g', 'new_v_w_gate', 'new_v_w_up', 'new_v_conv_w', 'new_v_conv_b', 'new_v_w_down', 'new_v_final_norm_g']
TWIN_LEAF_KINDS = {'loss': 'loss', 'grad_x': 'grad_x', 'grad_norm1_g': 'grad_w', 'grad_w_in': 'grad_w', 'grad_gf_up': 'grad_w', 'grad_gf_b': 'grad_w', 'grad_gb_up': 'grad_w', 'grad_gb_b': 'grad_w', 'grad_gla_norm_g': 'grad_w', 'grad_attn_norm_g': 'grad_w', 'grad_w_out': 'grad_w', 'grad_norm2_g': 'grad_w', 'grad_w_gate': 'grad_w', 'grad_w_up': 'grad_w', 'grad_conv_w': 'grad_w', 'grad_conv_b': 'grad_w', 'grad_w_down': 'grad_w', 'grad_final_norm_g': 'grad_w', 'delta_norm1_g': 'delta_w', 'delta_w_in': 'delta_w', 'delta_gf_up': 'delta_w', 'delta_gf_b': 'delta_w', 'delta_gb_up': 'delta_w', 'delta_gb_b': 'delta_w', 'delta_gla_norm_g': 'delta_w', 'delta_attn_norm_g': 'delta_w', 'delta_w_out': 'delta_w', 'delta_norm2_g': 'delta_w', 'delta_w_gate': 'delta_w', 'delta_w_up': 'delta_w', 'delta_conv_w': 'delta_w', 'delta_conv_b': 'delta_w', 'delta_w_down': 'delta_w', 'delta_final_norm_g': 'delta_w', 'new_m_norm1_g': 'new_m', 'new_m_w_in': 'new_m', 'new_m_gf_up': 'new_m', 'new_m_gf_b': 'new_m', 'new_m_gb_up': 'new_m', 'new_m_gb_b': 'new_m', 'new_m_gla_norm_g': 'new_m', 'new_m_attn_norm_g': 'new_m', 'new_m_w_out': 'new_m', 'new_m_norm2_g': 'new_m', 'new_m_w_gate': 'new_m', 'new_m_w_up': 'new_m', 'new_m_conv_w': 'new_m', 'new_m_conv_b': 'new_m', 'new_m_w_down': 'new_m', 'new_m_final_norm_g': 'new_m', 'new_v_norm1_g': 'new_v', 'new_v_w_in': 'new_v', 'new_v_gf_up': 'new_v', 'new_v_gf_b': 'new_v', 'new_v_gb_up': 'new_v', 'new_v_gb_b': 'new_v', 'new_v_gla_norm_g': 'new_v', 'new_v_attn_norm_g': 'new_v', 'new_v_w_out': 'new_v', 'new_v_norm2_g': 'new_v', 'new_v_w_gate': 'new_v', 'new_v_w_up': 'new_v', 'new_v_conv_w': 'new_v', 'new_v_conv_b': 'new_v', 'new_v_w_down': 'new_v', 'new_v_final_norm_g': 'new_v'}


def _forward(args):
    return _fwd_reference(*[args[k] for k in FWD_PARAMS])


def _output_shape():
    def fwd():
        inp = _fwd_setup_inputs(0)
        return _fwd_reference(*[inp[k] for k in FWD_PARAMS])
    out = _jax.eval_shape(fwd)
    return out.shape, out.dtype

N_MICROBATCH = 1
ADAM_LR = 0.001
ADAM_B1 = 0.9
ADAM_B2 = 0.999
ADAM_EPS = 1e-08
ADAM_WD = 0.01
ADAM_STEP = 10
PER_EXAMPLE_BATCH_AXIS = {'x': 0, 'loss_target': 0}
SHARED_INPUTS = []
_WEIGHT_DTYPES = {'norm1_g': _jnp.float32, 'w_in': _jnp.float32, 'gf_up': _jnp.float32, 'gf_b': _jnp.float32, 'gb_up': _jnp.float32, 'gb_b': _jnp.float32, 'gla_norm_g': _jnp.float32, 'attn_norm_g': _jnp.float32, 'w_out': _jnp.float32, 'norm2_g': _jnp.float32, 'w_gate': _jnp.float32, 'w_up': _jnp.float32, 'conv_w': _jnp.float32, 'conv_b': _jnp.float32, 'w_down': _jnp.float32, 'final_norm_g': _jnp.float32}
MOMENT_SCALE = {'norm1_g': 1.611328e-01, 'w_in': 9.139200e-02, 'gf_up': 7.111936e-03, 'gf_b': 2.956191e-02, 'gb_up': 7.296341e-03, 'gb_b': 2.862860e-02, 'gla_norm_g': 1.505805e-01, 'attn_norm_g': 1.125418e-01, 'w_out': 8.761607e-02, 'norm2_g': 8.396251e-02, 'w_gate': 3.380017e-02, 'w_up': 3.270016e-02, 'conv_w': 3.340223e-02, 'conv_b': 3.160114e-02, 'w_down': 5.436649e-02, 'final_norm_g': 3.194635e+01}


def _to_microbatches(a, axis):
    t = _jnp.moveaxis(a, axis, 0)
    t = t.reshape((N_MICROBATCH, t.shape[0] // N_MICROBATCH) + t.shape[1:])
    return _jnp.moveaxis(t, 1, axis + 1)


def setup_inputs(seed: int = 0) -> dict:
    inp = _fwd_setup_inputs(seed)
    key = _jax.random.fold_in(_jax.random.key(seed), 7919)
    shape, _ = _output_shape()
    out = dict(inp)
    out["loss_target"] = _jax.random.normal(_jax.random.fold_in(key, 0), shape, _jnp.float32)
    for i, name in enumerate(TWIN_WEIGHTS):
        w = inp[name].astype(_jnp.float32)
        if MOMENT_SCALE is None:
            s = _jnp.sqrt(_jnp.mean(_jnp.square(w)) + 1e-30)
        else:
            s = MOMENT_SCALE[name]
        km, kv = _jax.random.split(_jax.random.fold_in(key, i + 1))
        out[name] = w
        out["m_" + name] = s * _jax.random.normal(km, w.shape, _jnp.float32)
        out["v_" + name] = (s * s) * _jax.random.uniform(kv, w.shape, _jnp.float32, 0.5, 1.5)
    if N_MICROBATCH > 1:
        for name, axis in PER_EXAMPLE_BATCH_AXIS.items():
            out[name] = _to_microbatches(out[name], axis)
    return {'x': out['x'], 'norm1_g': out['norm1_g'], 'w_in': out['w_in'], 'gf_up': out['gf_up'], 'gf_b': out['gf_b'], 'gb_up': out['gb_up'], 'gb_b': out['gb_b'], 'gla_norm_g': out['gla_norm_g'], 'attn_norm_g': out['attn_norm_g'], 'w_out': out['w_out'], 'norm2_g': out['norm2_g'], 'w_gate': out['w_gate'], 'w_up': out['w_up'], 'conv_w': out['conv_w'], 'conv_b': out['conv_b'], 'w_down': out['w_down'], 'final_norm_g': out['final_norm_g'], 'loss_target': out['loss_target'], 'm_norm1_g': out['m_norm1_g'], 'm_w_in': out['m_w_in'], 'm_gf_up': out['m_gf_up'], 'm_gf_b': out['m_gf_b'], 'm_gb_up': out['m_gb_up'], 'm_gb_b': out['m_gb_b'], 'm_gla_norm_g': out['m_gla_norm_g'], 'm_attn_norm_g': out['m_attn_norm_g'], 'm_w_out': out['m_w_out'], 'm_norm2_g': out['m_norm2_g'], 'm_w_gate': out['m_w_gate'], 'm_w_up': out['m_w_up'], 'm_conv_w': out['m_conv_w'], 'm_conv_b': out['m_conv_b'], 'm_w_down': out['m_w_down'], 'm_final_norm_g': out['m_final_norm_g'], 'v_norm1_g': out['v_norm1_g'], 'v_w_in': out['v_w_in'], 'v_gf_up': out['v_gf_up'], 'v_gf_b': out['v_gf_b'], 'v_gb_up': out['v_gb_up'], 'v_gb_b': out['v_gb_b'], 'v_gla_norm_g': out['v_gla_norm_g'], 'v_attn_norm_g': out['v_attn_norm_g'], 'v_w_out': out['v_w_out'], 'v_norm2_g': out['v_norm2_g'], 'v_w_gate': out['v_w_gate'], 'v_w_up': out['v_w_up'], 'v_conv_w': out['v_conv_w'], 'v_conv_b': out['v_conv_b'], 'v_w_down': out['v_w_down'], 'v_final_norm_g': out['v_final_norm_g']}


def _loss(weights, diff, rest, loss_target):
    with _jax.named_scope("forward"):
        args = {**rest, TWIN_DIFF_INPUT: diff, **{k: w.astype(_WEIGHT_DTYPES[k]) for k, w in weights.items()}}
        y = _forward(args)
    with _jax.named_scope("loss_head"):
        err = _jnp.square(y.astype(_jnp.float32) - loss_target)
        return 0.5 * _jnp.sum(_jnp.mean(err, axis=-1)) if err.ndim else 0.5 * err


def _adamw(w, g, m, v):
    m = ADAM_B1 * m + (1.0 - ADAM_B1) * g
    v = ADAM_B2 * v + (1.0 - ADAM_B2) * _jnp.square(g)
    m_hat = m / (1.0 - ADAM_B1 ** ADAM_STEP)
    v_hat = v / (1.0 - ADAM_B2 ** ADAM_STEP)
    delta = -ADAM_LR * (m_hat / (_jnp.sqrt(v_hat) + ADAM_EPS) + ADAM_WD * w)
    return delta, m, v


def reference(x, norm1_g, w_in, gf_up, gf_b, gb_up, gb_b, gla_norm_g, attn_norm_g, w_out, norm2_g, w_gate, w_up, conv_w, conv_b, w_down, final_norm_g, loss_target, m_norm1_g, m_w_in, m_gf_up, m_gf_b, m_gb_up, m_gb_b, m_gla_norm_g, m_attn_norm_g, m_w_out, m_norm2_g, m_w_gate, m_w_up, m_conv_w, m_conv_b, m_w_down, m_final_norm_g, v_norm1_g, v_w_in, v_gf_up, v_gf_b, v_gb_up, v_gb_b, v_gla_norm_g, v_attn_norm_g, v_w_out, v_norm2_g, v_w_gate, v_w_up, v_conv_w, v_conv_b, v_w_down, v_final_norm_g):
    given = dict(x=x, norm1_g=norm1_g, w_in=w_in, gf_up=gf_up, gf_b=gf_b, gb_up=gb_up, gb_b=gb_b, gla_norm_g=gla_norm_g, attn_norm_g=attn_norm_g, w_out=w_out, norm2_g=norm2_g, w_gate=w_gate, w_up=w_up, conv_w=conv_w, conv_b=conv_b, w_down=w_down, final_norm_g=final_norm_g, loss_target=loss_target, m_norm1_g=m_norm1_g, m_w_in=m_w_in, m_gf_up=m_gf_up, m_gf_b=m_gf_b, m_gb_up=m_gb_up, m_gb_b=m_gb_b, m_gla_norm_g=m_gla_norm_g, m_attn_norm_g=m_attn_norm_g, m_w_out=m_w_out, m_norm2_g=m_norm2_g, m_w_gate=m_w_gate, m_w_up=m_w_up, m_conv_w=m_conv_w, m_conv_b=m_conv_b, m_w_down=m_w_down, m_final_norm_g=m_final_norm_g, v_norm1_g=v_norm1_g, v_w_in=v_w_in, v_gf_up=v_gf_up, v_gf_b=v_gf_b, v_gb_up=v_gb_up, v_gb_b=v_gb_b, v_gla_norm_g=v_gla_norm_g, v_attn_norm_g=v_attn_norm_g, v_w_out=v_w_out, v_norm2_g=v_norm2_g, v_w_gate=v_w_gate, v_w_up=v_w_up, v_conv_w=v_conv_w, v_conv_b=v_conv_b, v_w_down=v_w_down, v_final_norm_g=v_final_norm_g)
    weights = {n: given[n] for n in TWIN_WEIGHTS}
    shared = {n: given[n] for n in SHARED_INPUTS}
    per_example = {n: given[n] for n in ['x']}
    grad_fn = _jax.value_and_grad(_loss, argnums=(0, 1))

    def one_microbatch(ex, loss_target):
        ex = dict(ex)
        diff = ex.pop(TWIN_DIFF_INPUT)
        return grad_fn(weights, diff, {**shared, **ex}, loss_target)

    if N_MICROBATCH == 1:
        loss, (grad_w, grad_x) = one_microbatch(per_example, given["loss_target"])
    else:
        def body(carry, xs):
            loss_sum, grad_sum = carry
            l_k, (gw_k, gx_k) = one_microbatch(xs[0], xs[1])
            with _jax.named_scope("update"):
                return (loss_sum + l_k, _jax.tree.map(_jnp.add, grad_sum, gw_k)), gx_k

        init = (_jnp.zeros((), _jnp.float32), _jax.tree.map(_jnp.zeros_like, weights))
        (loss, grad_w), grad_x = _jax.lax.scan(body, init, (per_example, given["loss_target"]))
    with _jax.named_scope("update"):
        delta_w, new_m, new_v = {}, {}, {}
        for n in TWIN_WEIGHTS:
            delta_w[n], new_m[n], new_v[n] = _adamw(weights[n], grad_w[n], given["m_" + n], given["v_" + n])
    return (loss, grad_x, *[grad_w[n] for n in TWIN_WEIGHTS], *[delta_w[n] for n in TWIN_WEIGHTS],
            *[new_m[n] for n in TWIN_WEIGHTS], *[new_v[n] for n in TWIN_WEIGHTS])
```

```python
import functools

import numpy as np
import jax
import jax.numpy as jnp
from jax import lax
from jax.experimental import pallas as pl
from jax.experimental.pallas import tpu as pltpu

F32 = jnp.float32
BF16 = jnp.bfloat16

D_MODEL = 2048
ATTN_W = 1024
ATTN_HEADS = 8
HEAD_DIM = 128
ROPE_DIM = 32
ROPE_THETA = 500000.0
DILATIONS = (1, 4, 16)
N_SIDE = 64
GLA_KW = 512
GLA_VW = 1024
GLA_HEADS = 4
GLA_DK = 128
GLA_DV = 256
GLA_RANK = 16
GLA_GATE_NORM = 16.0
GLA_CHUNK = 64
IN_WIDTH = 6176
IN_PAD = 6272
D_FF = 5632
EPS = 1e-6
N_DEV = 8

OFF_AQ, OFF_AK, OFF_AV = 0, 1024, 2048
OFF_GQ, OFF_GK, OFF_GV, OFF_GR, OFF_Z = 3072, 3584, 4096, 5120, 6144

ADAM_LR, ADAM_B1, ADAM_B2, ADAM_EPS, ADAM_WD, ADAM_STEP = 0.001, 0.9, 0.999, 1e-08, 0.01, 10

LANES = 128
SUBLANES = 8
VMEM_LIMIT = 48 * 1024 * 1024
ROW_BLOCK = 256
ATTN_BLOCK = 128
GLA_CHUNKS_PER_STEP = 4
NEG = -1e30
MESH_ID = pl.DeviceIdType.MESH


def _params(sem):
    return pltpu.CompilerParams(dimension_semantics=sem, vmem_limit_bytes=VMEM_LIMIT)


def _dot(a, b):
    return lax.dot_general(a, b, (((1,), (0,)), ((), ())), preferred_element_type=F32)


def _dot_nt(a, b):
    return lax.dot_general(a, b, (((1,), (1,)), ((), ())), preferred_element_type=F32)


def _dot_tn(a, b):
    return lax.dot_general(a, b, (((0,), (0,)), ((), ())), preferred_element_type=F32)


def _sigmoid(x):
    return 1.0 / (1.0 + jnp.exp(-x))


def _matmul(pairs, mode, out_dtype, tm, tn, tk, name, res=None):
    a0, b0 = pairs[0]
    if mode == "nn":
        (m, kdim), n = a0.shape, b0.shape[1]
    elif mode == "nt":
        (m, kdim), n = a0.shape, b0.shape[0]
    else:
        (kdim, m), n = a0.shape, b0.shape[1]
    assert m % tm == 0 and n % tn == 0 and kdim % tk == 0, (name, m, n, kdim)
    nk = kdim // tk
    npairs = len(pairs)
    steps = nk * npairs
    dot = {"nn": _dot, "nt": _dot_nt, "tn": _dot_tn}[mode]

    def kidx(p):
        return lambda k: jnp.clip(k - p * nk, 0, nk - 1)

    in_specs, args = [], []
    for p, (a, b) in enumerate(pairs):
        kk = kidx(p)
        if mode == "nn":
            in_specs += [pl.BlockSpec((tm, tk), lambda i, j, k, kk=kk: (i, kk(k))),
                         pl.BlockSpec((tk, tn), lambda i, j, k, kk=kk: (kk(k), j))]
        elif mode == "nt":
            in_specs += [pl.BlockSpec((tm, tk), lambda i, j, k, kk=kk: (i, kk(k))),
                         pl.BlockSpec((tn, tk), lambda i, j, k, kk=kk: (j, kk(k)))]
        else:
            in_specs += [pl.BlockSpec((tk, tm), lambda i, j, k, kk=kk: (kk(k), i)),
                         pl.BlockSpec((tk, tn), lambda i, j, k, kk=kk: (kk(k), j))]
        args += [a, b]
    if res is not None:
        in_specs.append(pl.BlockSpec((tm, tn), lambda i, j, k: (i, j)))
        args.append(res)

    def body(*refs):
        ab = refs[:2 * npairs]
        res_ref = refs[2 * npairs] if res is not None else None
        o_ref = refs[2 * npairs + (1 if res is not None else 0)]

        def finish(acc):
            if res_ref is not None:
                acc = acc + res_ref[...]
            o_ref[...] = acc.astype(out_dtype)

        if steps == 1:
            finish(dot(ab[0][...], ab[1][...]))
            return
        acc_ref = refs[-1]
        k = pl.program_id(2)

        @pl.when(k == 0)
        def _():
            acc_ref[...] = jnp.zeros_like(acc_ref)

        for p in range(npairs):
            @pl.when((k >= p * nk) & (k < (p + 1) * nk))
            def _(p=p):
                acc_ref[...] += dot(ab[2 * p][...], ab[2 * p + 1][...])

        @pl.when(k == steps - 1)
        def _():
            finish(acc_ref[...])

    return pl.pallas_call(
        body, name=name,
        grid=(m // tm, n // tn, steps),
        in_specs=in_specs,
        out_specs=pl.BlockSpec((tm, tn), lambda i, j, k: (i, j)),
        out_shape=jax.ShapeDtypeStruct((m, n), out_dtype),
        scratch_shapes=[] if steps == 1 else [pltpu.VMEM((tm, tn), F32)],
        compiler_params=_params(("parallel", "parallel", "arbitrary")),
    )(*args)


def _rms_fwd(x, g, name):
    s, d = x.shape

    def body(x_ref, g_ref, o_ref):
        xv = x_ref[...]
        r = lax.rsqrt(jnp.mean(xv * xv, axis=-1, keepdims=True) + EPS)
        o_ref[...] = (xv * r * g_ref[...]).astype(BF16)

    return pl.pallas_call(
        body, name=name, grid=(s // ROW_BLOCK,),
        in_specs=[pl.BlockSpec((ROW_BLOCK, d), lambda i: (i, 0)), pl.BlockSpec((1, d), lambda i: (0, 0))],
        out_specs=pl.BlockSpec((ROW_BLOCK, d), lambda i: (i, 0)),
        out_shape=jax.ShapeDtypeStruct((s, d), BF16),
        compiler_params=_params(("parallel",)),
    )(x, g)


def _rms_bwd(dn, x, g, dres, name):
    s, d = x.shape

    def body(dn_ref, x_ref, g_ref, dres_ref, dx_ref, dxb_ref, gg_ref):
        i = pl.program_id(0)
        xv, dnv = x_ref[...], dn_ref[...]
        r = lax.rsqrt(jnp.mean(xv * xv, axis=-1, keepdims=True) + EPS)
        dng = dnv * g_ref[...]
        c = jnp.mean(dng * xv, axis=-1, keepdims=True)
        dx = dres_ref[...] + r * dng - xv * (r * r * r * c)
        dx_ref[...] = dx
        dxb_ref[...] = dx.astype(BF16)

        @pl.when(i == 0)
        def _():
            gg_ref[...] = jnp.zeros_like(gg_ref)

        gg_ref[...] += jnp.sum(dnv * xv * r, axis=0, keepdims=True)

    row = pl.BlockSpec((ROW_BLOCK, d), lambda i: (i, 0))
    vec = pl.BlockSpec((1, d), lambda i: (0, 0))
    return pl.pallas_call(
        body, name=name, grid=(s // ROW_BLOCK,),
        in_specs=[row, row, vec, row],
        out_specs=[row, row, vec],
        out_shape=[jax.ShapeDtypeStruct((s, d), F32), jax.ShapeDtypeStruct((s, d), BF16),
                   jax.ShapeDtypeStruct((1, d), F32)],
        compiler_params=_params(("arbitrary",)),
    )(dn, x, g, dres)


def _final_loss(h2, target, g, name="final_loss"):
    s, d = h2.shape

    def body(h_ref, t_ref, g_ref, dh_ref, dhb_ref, loss_ref, gg_ref):
        i = pl.program_id(0)
        hv, gv = h_ref[...], g_ref[...]
        r = lax.rsqrt(jnp.mean(hv * hv, axis=-1, keepdims=True) + EPS)
        e = hv * r * gv - t_ref[...]
        dy = e * (1.0 / d)
        dyg = dy * gv
        c = jnp.mean(dyg * hv, axis=-1, keepdims=True)
        dh = r * dyg - hv * (r * r * r * c)
        dh_ref[...] = dh
        dhb_ref[...] = dh.astype(BF16)

        @pl.when(i == 0)
        def _():
            gg_ref[...] = jnp.zeros_like(gg_ref)
            loss_ref[...] = jnp.zeros_like(loss_ref)

        gg_ref[...] += jnp.sum(dy * hv * r, axis=0, keepdims=True)
        loss_ref[...] += jnp.sum(jnp.sum(e * e, axis=-1, keepdims=True), axis=0, keepdims=True) * (0.5 / d)

    row = pl.BlockSpec((ROW_BLOCK, d), lambda i: (i, 0))
    vec = pl.BlockSpec((1, d), lambda i: (0, 0))
    return pl.pallas_call(
        body, name=name, grid=(s // ROW_BLOCK,),
        in_specs=[row, row, vec],
        out_specs=[row, row, pl.BlockSpec((SUBLANES, LANES), lambda i: (0, 0)), vec],
        out_shape=[jax.ShapeDtypeStruct((s, d), F32), jax.ShapeDtypeStruct((s, d), BF16),
                   jax.ShapeDtypeStruct((SUBLANES, LANES), F32), jax.ShapeDtypeStruct((1, d), F32)],
        compiler_params=_params(("arbitrary",)),
    )(h2, target, g)


def _rope_tables(s):
    pos = jnp.arange(s, dtype=F32)
    inv_freq = ROPE_THETA ** (-jnp.arange(0, ROPE_DIM, 2, dtype=F32) / ROPE_DIM)
    ang = pos[:, None] * inv_freq[None, :]
    cos, sin = jnp.cos(ang), jnp.sin(ang)
    half = ROPE_DIM // 2
    rest = HEAD_DIM - ROPE_DIM
    c = jnp.concatenate([cos, cos, jnp.ones((s, rest), F32)], axis=1)
    sm = jnp.concatenate([-sin, jnp.zeros((s, half + rest), F32)], axis=1)
    sp = jnp.concatenate([jnp.zeros((s, half), F32), sin, jnp.zeros((s, rest), F32)], axis=1)
    return c, sm, sp


def _rope_fwd(proj, tables, name="rope_fwd"):
    s = proj.shape[0]
    half = ROPE_DIM // 2

    def body(p_ref, c_ref, sm_ref, sp_ref, q_ref, k_ref, v_ref):
        c, sm, sp = c_ref[...], sm_ref[...], sp_ref[...]
        for off, o_ref in ((OFF_AQ, q_ref), (OFF_AK, k_ref)):
            for h in range(ATTN_HEADS):
                t = p_ref[:, off + h * HEAD_DIM: off + (h + 1) * HEAD_DIM]
                o = t * c + pltpu.roll(t, HEAD_DIM - half, 1) * sm + pltpu.roll(t, half, 1) * sp
                o_ref[:, h * HEAD_DIM:(h + 1) * HEAD_DIM] = o.astype(BF16)
        v_ref[...] = p_ref[:, OFF_AV:OFF_AV + ATTN_W].astype(BF16)

    tab = pl.BlockSpec((ROW_BLOCK, HEAD_DIM), lambda i: (i, 0))
    out = pl.BlockSpec((ROW_BLOCK, ATTN_W), lambda i: (i, 0))
    return pl.pallas_call(
        body, name=name, grid=(s // ROW_BLOCK,),
        in_specs=[pl.BlockSpec((ROW_BLOCK, 3 * ATTN_W), lambda i: (i, 0)), tab, tab, tab],
        out_specs=[out, out, out],
        out_shape=[jax.ShapeDtypeStruct((s, ATTN_W), BF16)] * 3,
        compiler_params=_params(("parallel",)),
    )(proj, *tables)


def _rope_bwd(dqs, dks, dvs, tables, name="rope_bwd"):
    s = dqs[0].shape[0]
    half = ROPE_DIM // 2
    nbr = len(dqs)

    def body(*refs):
        dq_refs, dk_refs, dv_refs = refs[:nbr], refs[nbr:2 * nbr], refs[2 * nbr:3 * nbr]
        c_ref, sm_ref, sp_ref, o_ref = refs[3 * nbr:]
        c, sm, sp = c_ref[...], sm_ref[...], sp_ref[...]
        for off, grp in ((OFF_AQ, dq_refs), (OFF_AK, dk_refs)):
            for h in range(ATTN_HEADS):
                sl = slice(h * HEAD_DIM, (h + 1) * HEAD_DIM)
                t = grp[0][:, sl]
                for r in grp[1:]:
                    t = t + r[:, sl]
                o = t * c + pltpu.roll(t * sm, half, 1) + pltpu.roll(t * sp, HEAD_DIM - half, 1)
                o_ref[:, off + h * HEAD_DIM: off + (h + 1) * HEAD_DIM] = o.astype(BF16)
        t = dv_refs[0][...]
        for r in dv_refs[1:]:
            t = t + r[...]
        o_ref[:, OFF_AV:OFF_AV + ATTN_W] = t.astype(BF16)

    tab = pl.BlockSpec((ROW_BLOCK, HEAD_DIM), lambda i: (i, 0))
    blk = pl.BlockSpec((ROW_BLOCK, ATTN_W), lambda i: (i, 0))
    return pl.pallas_call(
        body, name=name, grid=(s // ROW_BLOCK,),
        in_specs=[blk] * (3 * nbr) + [tab, tab, tab],
        out_specs=pl.BlockSpec((ROW_BLOCK, 3 * ATTN_W), lambda i: (i, 0)),
        out_shape=jax.ShapeDtypeStruct((s, 3 * ATTN_W), BF16),
        compiler_params=_params(("parallel",)),
    )(*dqs, *dks, *dvs, *tables)


def _band_valid(qpos, kpos, length):
    return (jnp.abs(kpos - qpos) <= N_SIDE) & (kpos >= 0) & (kpos < length) & (qpos >= 0) & (qpos < length)


def _attn_fwd(q, k, v, dil, name):
    s = q.shape[0]
    length = s // dil
    qb = ATTN_BLOCK
    nb = length // qb
    scale = HEAD_DIM ** -0.5
    qv, kv, vv = (t.reshape(length, dil * ATTN_W) for t in (q, k, v))

    def body(q_ref, kp_ref, kc_ref, kn_ref, vp_ref, vc_ref, vn_ref, o_ref, lse_ref):
        j = pl.program_id(1)
        row = lax.broadcasted_iota(jnp.int32, (qb, qb), 0)
        col = lax.broadcasted_iota(jnp.int32, (qb, qb), 1)
        lane = lax.broadcasted_iota(jnp.int32, (qb, LANES), 1)
        qpos = j * qb + row
        valids = [_band_valid(qpos, (j + bi - 1) * qb + col, length) for bi in range(3)]
        lse_acc = jnp.zeros((qb, LANES), F32)
        for h in range(ATTN_HEADS):
            sl = slice(h * HEAD_DIM, (h + 1) * HEAD_DIM)
            qh = q_ref[:, sl]
            ss = [jnp.where(valids[bi], _dot_nt(qh, kr[:, sl]) * scale, NEG)
                  for bi, kr in enumerate((kp_ref, kc_ref, kn_ref))]
            m = jnp.maximum(jnp.maximum(jnp.max(ss[0], axis=-1, keepdims=True),
                                        jnp.max(ss[1], axis=-1, keepdims=True)),
                            jnp.max(ss[2], axis=-1, keepdims=True))
            ps = [jnp.exp(sv - m) for sv in ss]
            den = (jnp.sum(ps[0], axis=-1, keepdims=True) + jnp.sum(ps[1], axis=-1, keepdims=True)
                   + jnp.sum(ps[2], axis=-1, keepdims=True))
            acc = (_dot(ps[0].astype(BF16), vp_ref[:, sl]) + _dot(ps[1].astype(BF16), vc_ref[:, sl])
                   + _dot(ps[2].astype(BF16), vn_ref[:, sl]))
            o_ref[:, sl] = acc / den
            lse_acc = jnp.where(lane == h, m + jnp.log(den), lse_acc)
        lse_ref[...] = lse_acc

    cur = pl.BlockSpec((qb, ATTN_W), lambda r, j: (j, r))
    prev = pl.BlockSpec((qb, ATTN_W), lambda r, j: (jnp.maximum(j - 1, 0), r))
    nxt = pl.BlockSpec((qb, ATTN_W), lambda r, j: (jnp.minimum(j + 1, nb - 1), r))
    o, lse = pl.pallas_call(
        body, name=name, grid=(dil, nb),
        in_specs=[cur, prev, cur, nxt, prev, cur, nxt],
        out_specs=[cur, pl.BlockSpec((qb, LANES), lambda r, j: (j, r))],
        out_shape=[jax.ShapeDtypeStruct((length, dil * ATTN_W), F32),
                   jax.ShapeDtypeStruct((length, dil * LANES), F32)],
        compiler_params=_params(("parallel", "parallel")),
    )(qv, kv, kv, kv, vv, vv, vv)
    return o.reshape(s, ATTN_W), lse.reshape(s, LANES)


def _attn_combine(outs, lses, g, name="attn_combine"):
    s = outs[0].shape[0]
    nbr = len(outs)

    def body(*refs):
        o_refs, l_refs = refs[:nbr], refs[nbr:2 * nbr]
        g_ref, o_ref, lse_ref, n_ref = refs[2 * nbr:]
        ls = [r[...] for r in l_refs]
        m = ls[0]
        for l in ls[1:]:
            m = jnp.maximum(m, l)
        es = [jnp.exp(l - m) for l in ls]
        z = es[0]
        for e in es[1:]:
            z = z + e
        lse_ref[...] = m + jnp.log(z)
        ws = [e / z for e in es]
        ssq = jnp.zeros((ROW_BLOCK, 1), F32)
        for h in range(ATTN_HEADS):
            sl = slice(h * HEAD_DIM, (h + 1) * HEAD_DIM)
            acc = ws[0][:, h:h + 1] * o_refs[0][:, sl]
            for w, r in zip(ws[1:], o_refs[1:]):
                acc = acc + w[:, h:h + 1] * r[:, sl]
            o_ref[:, sl] = acc
            ssq = ssq + jnp.sum(acc * acc, axis=-1, keepdims=True)
        r = lax.rsqrt(ssq * (1.0 / ATTN_W) + EPS)
        n_ref[...] = (o_ref[...] * r * g_ref[...]).astype(BF16)

    blk = pl.BlockSpec((ROW_BLOCK, ATTN_W), lambda i: (i, 0))
    lblk = pl.BlockSpec((ROW_BLOCK, LANES), lambda i: (i, 0))
    return pl.pallas_call(
        body, name=name, grid=(s // ROW_BLOCK,),
        in_specs=[blk] * nbr + [lblk] * nbr + [pl.BlockSpec((1, ATTN_W), lambda i: (0, 0))],
        out_specs=[blk, lblk, blk],
        out_shape=[jax.ShapeDtypeStruct((s, ATTN_W), F32), jax.ShapeDtypeStruct((s, LANES), F32),
                   jax.ShapeDtypeStruct((s, ATTN_W), BF16)],
        compiler_params=_params(("parallel",)),
    )(*outs, *lses, g)


def _attn_prebwd(dcat, o, g, name="attn_prebwd"):
    s = o.shape[0]

    def body(dy_ref, o_ref, g_ref, do_ref, delta_ref, gg_ref):
        i = pl.program_id(0)
        dy, ov = dy_ref[...], o_ref[...]
        r = lax.rsqrt(jnp.mean(ov * ov, axis=-1, keepdims=True) + EPS)
        dyg = dy * g_ref[...]
        c = jnp.mean(dyg * ov, axis=-1, keepdims=True)
        do = r * dyg - ov * (r * r * r * c)
        do_ref[...] = do.astype(BF16)
        prod = do * ov
        lane = lax.broadcasted_iota(jnp.int32, (ROW_BLOCK, LANES), 1)
        acc = jnp.zeros((ROW_BLOCK, LANES), F32)
        for h in range(ATTN_HEADS):
            acc = jnp.where(lane == h, jnp.sum(prod[:, h * HEAD_DIM:(h + 1) * HEAD_DIM], axis=-1, keepdims=True), acc)
        delta_ref[...] = acc

        @pl.when(i == 0)
        def _():
            gg_ref[...] = jnp.zeros_like(gg_ref)

        gg_ref[...] += jnp.sum(dy * ov * r, axis=0, keepdims=True)

    blk = pl.BlockSpec((ROW_BLOCK, ATTN_W), lambda i: (i, 0))
    vec = pl.BlockSpec((1, ATTN_W), lambda i: (0, 0))
    return pl.pallas_call(
        body, name=name, grid=(s // ROW_BLOCK,),
        in_specs=[blk, blk, vec],
        out_specs=[blk, pl.BlockSpec((ROW_BLOCK, LANES), lambda i: (i, 0)), vec],
        out_shape=[jax.ShapeDtypeStruct((s, ATTN_W), BF16), jax.ShapeDtypeStruct((s, LANES), F32),
                   jax.ShapeDtypeStruct((1, ATTN_W), F32)],
        compiler_params=_params(("arbitrary",)),
    )(dcat, o, g)


def _attn_bwd(q, k, v, do, lse, delta, dil, name):
    s = q.shape[0]
    length = s // dil
    qb = ATTN_BLOCK
    nb = length // qb
    scale = HEAD_DIM ** -0.5
    qv, kv, vv, dov = (t.reshape(length, dil * ATTN_W) for t in (q, k, v, do))
    lv, dv_ = (t.reshape(length, dil * LANES) for t in (lse, delta))

    def body(qp, qc, qn, kp, kc, kn, vp, vc, vn, dop, doc, don, lp, lc, ln, dp, dc, dn, dq_ref, dk_ref, dv_ref):
        j = pl.program_id(1)
        row = lax.broadcasted_iota(jnp.int32, (qb, qb), 0)
        col = lax.broadcasted_iota(jnp.int32, (qb, qb), 1)
        pos_q = [(j + bi - 1) * qb + row for bi in range(3)]
        pos_k = [(j + bi - 1) * qb + col for bi in range(3)]
        valid_a = [_band_valid(pos_q[1], pos_k[bi], length) for bi in range(3)]
        valid_b = [_band_valid(pos_q[bi], pos_k[1], length) for bi in range(3)]
        for h in range(ATTN_HEADS):
            sl = slice(h * HEAD_DIM, (h + 1) * HEAD_DIM)
            hc = slice(h, h + 1)
            qh, kh, vh, doh = qc[:, sl], kc[:, sl], vc[:, sl], doc[:, sl]
            lse_c, del_c = lc[:, hc], dc[:, hc]
            dq = jnp.zeros((qb, HEAD_DIM), F32)
            dk = jnp.zeros((qb, HEAD_DIM), F32)
            dvh = jnp.zeros((qb, HEAD_DIM), F32)
            for bi, (kr, vr) in enumerate(((kp, vp), (kc, vc), (kn, vn))):
                kb, vb = kr[:, sl], vr[:, sl]
                p = jnp.where(valid_a[bi], jnp.exp(_dot_nt(qh, kb) * scale - lse_c), 0.0)
                ds = p * (_dot_nt(doh, vb) - del_c)
                dq = dq + _dot(ds.astype(BF16), kb)
            for bi, (qr, dor, lr, dr) in enumerate(((qp, dop, lp, dp), (qc, doc, lc, dc), (qn, don, ln, dn))):
                qx, dox = qr[:, sl], dor[:, sl]
                p = jnp.where(valid_b[bi], jnp.exp(_dot_nt(qx, kh) * scale - lr[:, hc]), 0.0)
                dvh = dvh + _dot_tn(p.astype(BF16), dox)
                ds = p * (_dot_nt(dox, vh) - dr[:, hc])
                dk = dk + _dot_tn(ds.astype(BF16), qx)
            dq_ref[:, sl] = dq * scale
            dk_ref[:, sl] = dk * scale
            dv_ref[:, sl] = dvh

    def specs(width):
        cur = pl.BlockSpec((qb, width), lambda r, j: (j, r))
        prev = pl.BlockSpec((qb, width), lambda r, j: (jnp.maximum(j - 1, 0), r))
        nxt = pl.BlockSpec((qb, width), lambda r, j: (jnp.minimum(j + 1, nb - 1), r))
        return [prev, cur, nxt]

    wide, narrow = specs(ATTN_W), specs(LANES)
    outs = pl.pallas_call(
        body, name=name, grid=(dil, nb),
        in_specs=wide * 4 + narrow * 2,
        out_specs=[wide[1]] * 3,
        out_shape=[jax.ShapeDtypeStruct((length, dil * ATTN_W), F32)] * 3,
        compiler_params=_params(("parallel", "parallel")),
    )(qv, qv, qv, kv, kv, kv, vv, vv, vv, dov, dov, dov, lv, lv, lv, dv_, dv_, dv_)
    return tuple(t.reshape(s, ATTN_W) for t in outs)


def _gate_matrices(gf_up, gb_up):
    pad = LANES - 2 * GLA_RANK
    uf = jnp.concatenate([gf_up, jnp.zeros((GLA_RANK + pad, GLA_KW), gf_up.dtype)], axis=0)
    ub = jnp.concatenate([jnp.zeros((GLA_RANK, GLA_KW), gb_up.dtype), gb_up, jnp.zeros((pad, GLA_KW), gb_up.dtype)], axis=0)
    return uf.astype(BF16), ub.astype(BF16)


def _log_sigmoid(x):
    return jnp.minimum(x, 0.0) - jnp.log(1.0 + jnp.exp(-jnp.abs(x)))


def _gla_gates(proj, uf, ub, gf_b, gb_b, name="gla_gates"):
    s = proj.shape[0]

    def body(z_ref, uf_ref, ub_ref, bf_ref, bb_ref, gf_ref, gb_ref):
        z = z_ref[...].astype(BF16)
        gf_ref[...] = _log_sigmoid(_dot(z, uf_ref[...]) + bf_ref[...]) * (1.0 / GLA_GATE_NORM)
        gb_ref[...] = _log_sigmoid(_dot(z, ub_ref[...]) + bb_ref[...]) * (1.0 / GLA_GATE_NORM)

    mat = pl.BlockSpec((LANES, GLA_KW), lambda i: (0, 0))
    vec = pl.BlockSpec((1, GLA_KW), lambda i: (0, 0))
    out = pl.BlockSpec((ROW_BLOCK, GLA_KW), lambda i: (i, 0))
    return pl.pallas_call(
        body, name=name, grid=(s // ROW_BLOCK,),
        in_specs=[pl.BlockSpec((ROW_BLOCK, LANES), lambda i: (i, OFF_Z // LANES)), mat, mat, vec, vec],
        out_specs=[out, out],
        out_shape=[jax.ShapeDtypeStruct((s, GLA_KW), F32)] * 2,
        compiler_params=_params(("parallel",)),
    )(proj, uf, ub, gf_b, gb_b)


def _gla_gates_bwd(dgf, dgb, proj, uf, ub, gf_b, gb_b, name="gla_gates_bwd"):
    s = proj.shape[0]

    def body(dgf_ref, dgb_ref, z_ref, uf_ref, ub_ref, bf_ref, bb_ref, dz_ref, guf_ref, gub_ref, gbf_ref, gbb_ref):
        i = pl.program_id(0)
        z = z_ref[...].astype(BF16)
        uf_, ub_ = uf_ref[...], ub_ref[...]
        dpf = dgf_ref[...] * (1.0 / GLA_GATE_NORM) * _sigmoid(-(_dot(z, uf_) + bf_ref[...]))
        dpb = dgb_ref[...] * (1.0 / GLA_GATE_NORM) * _sigmoid(-(_dot(z, ub_) + bb_ref[...]))
        dpf_b, dpb_b = dpf.astype(BF16), dpb.astype(BF16)
        dz_ref[...] = (_dot_nt(dpf_b, uf_) + _dot_nt(dpb_b, ub_)).astype(BF16)

        @pl.when(i == 0)
        def _():
            for r in (guf_ref, gub_ref, gbf_ref, gbb_ref):
                r[...] = jnp.zeros_like(r)

        guf_ref[...] += _dot_tn(z, dpf_b)
        gub_ref[...] += _dot_tn(z, dpb_b)
        gbf_ref[...] += jnp.sum(dpf, axis=0, keepdims=True)
        gbb_ref[...] += jnp.sum(dpb, axis=0, keepdims=True)

    mat = pl.BlockSpec((LANES, GLA_KW), lambda i: (0, 0))
    vec = pl.BlockSpec((1, GLA_KW), lambda i: (0, 0))
    blk = pl.BlockSpec((ROW_BLOCK, GLA_KW), lambda i: (i, 0))
    return pl.pallas_call(
        body, name=name, grid=(s // ROW_BLOCK,),
        in_specs=[blk, blk, pl.BlockSpec((ROW_BLOCK, LANES), lambda i: (i, OFF_Z // LANES)), mat, mat, vec, vec],
        out_specs=[pl.BlockSpec((ROW_BLOCK, LANES), lambda i: (i, 0)), mat, mat, vec, vec],
        out_shape=[jax.ShapeDtypeStruct((s, LANES), BF16), jax.ShapeDtypeStruct((LANES, GLA_KW), F32),
                   jax.ShapeDtypeStruct((LANES, GLA_KW), F32), jax.ShapeDtypeStruct((1, GLA_KW), F32),
                   jax.ShapeDtypeStruct((1, GLA_KW), F32)],
        compiler_params=_params(("arbitrary",)),
    )(dgf, dgb, proj, uf, ub, gf_b, gb_b)


def _split3(x):
    x1 = x.astype(BF16)
    r1 = x - x1.astype(F32)
    x2 = r1.astype(BF16)
    x3 = (r1 - x2.astype(F32)).astype(BF16)
    return x1, x2, x3


def _dot_exact(mask_bf, x):
    x1, x2, x3 = _split3(x)
    return _dot(mask_bf, x1) + _dot(mask_bf, x2) + _dot(mask_bf, x3)


def _chunk_terms(q_ref, k_ref, g_ref, rs, reverse):
    c = GLA_CHUNK
    row = lax.broadcasted_iota(jnp.int32, (c, c), 0)
    col = lax.broadcasted_iota(jnp.int32, (c, c), 1)
    allowed = (col >= row) if reverse else (col <= row)
    seen_by = (col <= row) if reverse else (col >= row)
    mid, last = (c // 2, 0) if reverse else (c // 2 - 1, c - 1)
    q = q_ref[rs, :] * (GLA_DK ** -0.5)
    k = k_ref[rs, :]
    b = _dot_exact(jnp.where(allowed, 1.0, 0.0).astype(BF16), g_ref[rs, :])
    bref, blast = b[mid:mid + 1, :], b[last:last + 1, :]
    e_q, e_k, e_in, e_st = jnp.exp(b - bref), jnp.exp(bref - b), jnp.exp(b), jnp.exp(blast - b)
    return dict(allowed=allowed, seen_by=seen_by, last=last, q=q, k=k, e_q=e_q, e_k=e_k, e_in=e_in, e_st=e_st,
                dec=jnp.exp(blast), qe=q * e_q, ke=k * e_k, qin=q * e_in, kst=k * e_st)


def _gla_blockspecs(s, reverse_order):
    cb = GLA_CHUNKS_PER_STEP
    rows = cb * GLA_CHUNK
    nsteps = s // rows

    def rb(n):
        return (nsteps - 1 - n) if reverse_order else n

    qspec = pl.BlockSpec((rows, GLA_DK), lambda h, n: (rb(n), OFF_GQ // GLA_DK + h))
    kspec = pl.BlockSpec((rows, GLA_DK), lambda h, n: (rb(n), OFF_GK // GLA_DK + h))
    vspec = pl.BlockSpec((rows, GLA_DV), lambda h, n: (rb(n), OFF_GV // GLA_DV + h))
    gspec = pl.BlockSpec((rows, GLA_DK), lambda h, n: (rb(n), h))
    ospec = pl.BlockSpec((rows, GLA_DV), lambda h, n: (rb(n), h))
    sspec = pl.BlockSpec((1, cb, GLA_DV, GLA_DK), lambda h, n: (h, rb(n), 0, 0))
    return cb, rows, nsteps, qspec, kspec, vspec, gspec, ospec, sspec


def _gla_fwd(proj, g, reverse, name):
    s = proj.shape[0]
    cb, rows, nsteps, qspec, kspec, vspec, gspec, ospec, sspec = _gla_blockspecs(s, reverse)

    def body(q_ref, k_ref, v_ref, g_ref, o_ref, st_ref, state):
        @pl.when(pl.program_id(1) == 0)
        def _():
            state[...] = jnp.zeros_like(state)

        for c in (reversed(range(cb)) if reverse else range(cb)):
            rs = slice(c * GLA_CHUNK, (c + 1) * GLA_CHUNK)
            t = _chunk_terms(q_ref, k_ref, g_ref, rs, reverse)
            v = v_ref[rs, :].astype(BF16)
            a = jnp.where(t["allowed"], _dot_nt(t["qe"].astype(BF16), t["ke"].astype(BF16)), 0.0)
            st = state[...]
            st_ref[0, c] = st
            o_ref[rs, :] = _dot(a.astype(BF16), v) + _dot_nt(t["qin"].astype(BF16), st.astype(BF16))
            state[...] = st * t["dec"] + _dot_tn(v, t["kst"].astype(BF16))

    return pl.pallas_call(
        body, name=name, grid=(GLA_HEADS, nsteps),
        in_specs=[qspec, kspec, vspec, gspec],
        out_specs=[ospec, sspec],
        out_shape=[jax.ShapeDtypeStruct((s, GLA_VW), F32),
                   jax.ShapeDtypeStruct((GLA_HEADS, s // GLA_CHUNK, GLA_DV, GLA_DK), F32)],
        scratch_shapes=[pltpu.VMEM((GLA_DV, GLA_DK), F32)],
        compiler_params=_params(("parallel", "arbitrary")),
    )(proj, proj, proj, g)


def _gla_bwd(proj, g, do, states, reverse, name):
    s = proj.shape[0]
    cb, rows, nsteps, qspec, kspec, vspec, gspec, ospec, sspec = _gla_blockspecs(s, not reverse)

    def body(q_ref, k_ref, v_ref, g_ref, do_ref, sp_ref, dq_ref, dk_ref, dv_ref, dg_ref, dstate):
        @pl.when(pl.program_id(1) == 0)
        def _():
            dstate[...] = jnp.zeros_like(dstate)

        for c in (range(cb) if reverse else reversed(range(cb))):
            rs = slice(c * GLA_CHUNK, (c + 1) * GLA_CHUNK)
            t = _chunk_terms(q_ref, k_ref, g_ref, rs, reverse)
            v = v_ref[rs, :].astype(BF16)
            do = do_ref[rs, :]
            qe_b, ke_b = t["qe"].astype(BF16), t["ke"].astype(BF16)
            qin_b, kst_b = t["qin"].astype(BF16), t["kst"].astype(BF16)
            a = jnp.where(t["allowed"], _dot_nt(qe_b, ke_b), 0.0)
            da = jnp.where(t["allowed"], _dot_nt(do, v), 0.0).astype(BF16)
            dqe = _dot(da, ke_b)
            dke = _dot_tn(da, qe_b)
            sp = sp_ref[0, c]
            ds = dstate[...]
            ds_b = ds.astype(BF16)
            dqin = _dot(do, sp.astype(BF16))
            dkst = _dot(v, ds_b)
            dv_ref[rs, :] = _dot_tn(a.astype(BF16), do) + _dot_nt(kst_b, ds_b)
            ddec = jnp.sum(sp * ds, axis=0, keepdims=True)
            dstate[...] = ds * t["dec"] + _dot_tn(do, qin_b)
            dq_ref[rs, :] = (dqe * t["e_q"] + dqin * t["e_in"]) * (GLA_DK ** -0.5)
            dk_ref[rs, :] = dke * t["e_k"] + dkst * t["e_st"]
            kk = dkst * t["kst"]
            db = dqe * t["qe"] - dke * t["ke"] + dqin * t["qin"] - kk
            extra = jnp.sum(kk, axis=0, keepdims=True) + ddec * t["dec"]
            rowi = lax.broadcasted_iota(jnp.int32, (GLA_CHUNK, GLA_DK), 0)
            db = db + jnp.where(rowi == t["last"], extra, 0.0)
            dg_ref[rs, :] = _dot_exact(jnp.where(t["seen_by"], 1.0, 0.0).astype(BF16), db)

    return pl.pallas_call(
        body, name=name, grid=(GLA_HEADS, nsteps),
        in_specs=[qspec, kspec, vspec, gspec, ospec, sspec],
        out_specs=[gspec, gspec, ospec, gspec],
        out_shape=[jax.ShapeDtypeStruct((s, GLA_KW), F32), jax.ShapeDtypeStruct((s, GLA_KW), F32),
                   jax.ShapeDtypeStruct((s, GLA_VW), F32), jax.ShapeDtypeStruct((s, GLA_KW), F32)],
        scratch_shapes=[pltpu.VMEM((GLA_DV, GLA_DK), F32)],
        compiler_params=_params(("parallel", "arbitrary")),
    )(proj, proj, proj, g, do, states)


def _gla_post(o_f, o_b, proj, g, name="gla_post"):
    s = o_f.shape[0]

    def body(of_ref, ob_ref, gr_ref, g_ref, o_ref):
        gv = g_ref[...]
        for h in range(GLA_HEADS):
            sl = slice(h * GLA_DV, (h + 1) * GLA_DV)
            osum = of_ref[:, sl] + ob_ref[:, sl]
            r = lax.rsqrt(jnp.mean(osum * osum, axis=-1, keepdims=True) + EPS)
            gr = gr_ref[:, sl]
            o_ref[:, sl] = (osum * r * gv * (gr * _sigmoid(gr))).astype(BF16)

    blk = pl.BlockSpec((ROW_BLOCK, GLA_VW), lambda i: (i, 0))
    return pl.pallas_call(
        body, name=name, grid=(s // ROW_BLOCK,),
        in_specs=[blk, blk, pl.BlockSpec((ROW_BLOCK, GLA_VW), lambda i: (i, OFF_GR // GLA_VW)),
                  pl.BlockSpec((1, GLA_DV), lambda i: (0, 0))],
        out_specs=blk,
        out_shape=jax.ShapeDtypeStruct((s, GLA_VW), BF16),
        compiler_params=_params(("parallel",)),
    )(o_f, o_b, proj, g)


def _gla_post_bwd(dcat, o_f, o_b, proj, g, name="gla_post_bwd"):
    s = o_f.shape[0]

    def body(dy_ref, of_ref, ob_ref, gr_ref, g_ref, do_ref, dgr_ref, gg_ref):
        i = pl.program_id(0)
        gv = g_ref[...]
        gg = jnp.zeros((1, GLA_DV), F32)
        for h in range(GLA_HEADS):
            sl = slice(h * GLA_DV, (h + 1) * GLA_DV)
            osum = of_ref[:, sl] + ob_ref[:, sl]
            r = lax.rsqrt(jnp.mean(osum * osum, axis=-1, keepdims=True) + EPS)
            gr, dy = gr_ref[:, sl], dy_ref[:, sl]
            sg = _sigmoid(gr)
            dgr_ref[:, sl] = (dy * (osum * r * gv) * (sg * (1.0 + gr * (1.0 - sg)))).astype(BF16)
            dn = dy * (gr * sg)
            dng = dn * gv
            c = jnp.mean(dng * osum, axis=-1, keepdims=True)
            do_ref[:, sl] = (r * dng - osum * (r * r * r * c)).astype(BF16)
            gg = gg + jnp.sum(dn * osum * r, axis=0, keepdims=True)

        @pl.when(i == 0)
        def _():
            gg_ref[...] = jnp.zeros_like(gg_ref)

        gg_ref[...] += gg

    blk = pl.BlockSpec((ROW_BLOCK, GLA_VW), lambda i: (i, 0))
    vec = pl.BlockSpec((1, GLA_DV), lambda i: (0, 0))
    return pl.pallas_call(
        body, name=name, grid=(s // ROW_BLOCK,),
        in_specs=[pl.BlockSpec((ROW_BLOCK, GLA_VW), lambda i: (i, 1)), blk, blk,
                  pl.BlockSpec((ROW_BLOCK, GLA_VW), lambda i: (i, OFF_GR // GLA_VW)), vec],
        out_specs=[blk, blk, vec],
        out_shape=[jax.ShapeDtypeStruct((s, GLA_VW), BF16), jax.ShapeDtypeStruct((s, GLA_VW), BF16),
                   jax.ShapeDtypeStruct((1, GLA_DV), F32)],
        compiler_params=_params(("arbitrary",)),
    )(dcat, o_f, o_b, proj, g)


def _assemble_dproj(dpa, dq_f, dq_b, dk_f, dk_b, dv_f, dv_b, dgr, dz, name="assemble_dproj"):
    s = dpa.shape[0]

    def body(dpa_ref, dqf, dqb, dkf, dkb, dvf, dvb, dgr_ref, dz_ref, o_ref):
        o_ref[:, 0:OFF_GQ] = dpa_ref[...]
        o_ref[:, OFF_GQ:OFF_GK] = (dqf[...] + dqb[...]).astype(BF16)
        o_ref[:, OFF_GK:OFF_GV] = (dkf[...] + dkb[...]).astype(BF16)
        o_ref[:, OFF_GV:OFF_GR] = (dvf[...] + dvb[...]).astype(BF16)
        o_ref[:, OFF_GR:OFF_Z] = dgr_ref[...]
        o_ref[:, OFF_Z:IN_PAD] = dz_ref[...]

    def blk(w):
        return pl.BlockSpec((ROW_BLOCK, w), lambda i: (i, 0))

    return pl.pallas_call(
        body, name=name, grid=(s // ROW_BLOCK,),
        in_specs=[blk(3 * ATTN_W)] + [blk(GLA_KW)] * 4 + [blk(GLA_VW)] * 3 + [blk(LANES)],
        out_specs=blk(IN_PAD),
        out_shape=jax.ShapeDtypeStruct((s, IN_PAD), BF16),
        compiler_params=_params(("parallel",)),
    )(dpa, dq_f, dq_b, dk_f, dk_b, dv_f, dv_b, dgr, dz)


CONV_ROWS = 256
CONV_COLS = 1408
HALO = SUBLANES


def _halo_specs(s, tr, tc, col_of):
    per = tr // HALO
    last = s // HALO - 1
    cur = pl.BlockSpec((tr, tc), lambda c, i: (i, col_of(c)))
    prev = pl.BlockSpec((HALO, tc), lambda c, i: (jnp.maximum(i * per - 1, 0), col_of(c)))
    nxt = pl.BlockSpec((HALO, tc), lambda c, i: (jnp.minimum((i + 1) * per, last), col_of(c)))
    return prev, cur, nxt


def _extended(prev_ref, cur_ref, next_ref, i, s, tr):
    x = jnp.concatenate([prev_ref[...], cur_ref[...], next_ref[...]], axis=0)
    idx = i * tr - HALO + lax.broadcasted_iota(jnp.int32, x.shape, 0)
    return jnp.where((idx >= 0) & (idx < s), x, 0.0)


def _conv_glu(gate, up, conv_w, conv_b, name="conv_glu"):
    s, f = gate.shape
    tr, tc = CONV_ROWS, CONV_COLS
    ext = tr + 2 * HALO

    def body(gp, gc, gn, up_ref, w_ref, b_ref, o_ref):
        i = pl.program_id(1)
        ge = _extended(gp, gc, gn, i, s, tr)
        w = w_ref[...]
        conv = (w[0:1] * pltpu.roll(ge, 1, 0) + w[1:2] * ge + w[2:3] * pltpu.roll(ge, ext - 1, 0))[HALO:HALO + tr]
        conv = conv + b_ref[...]
        o_ref[...] = (conv * _sigmoid(conv) * up_ref[...]).astype(BF16)

    prev, cur, nxt = _halo_specs(s, tr, tc, lambda c: c)
    return pl.pallas_call(
        body, name=name, grid=(f // tc, s // tr),
        in_specs=[prev, cur, nxt, cur, pl.BlockSpec((3, tc), lambda c, i: (0, c)), pl.BlockSpec((1, tc), lambda c, i: (0, c))],
        out_specs=cur,
        out_shape=jax.ShapeDtypeStruct((s, f), BF16),
        compiler_params=_params(("parallel", "parallel")),
    )(gate, gate, gate, up, conv_w, conv_b)


def _conv_glu_bwd(dact, gate, up, conv_w, conv_b, name="conv_glu_bwd"):
    s, f = gate.shape
    tr, tc = CONV_ROWS, CONV_COLS
    ext = tr + 2 * HALO

    def body(dp, dc, dn, gp, gc, gn, upp, upc, upn, w_ref, b_ref, dg_ref, du_ref, gw_ref, gb_ref):
        i = pl.program_id(1)
        ge = _extended(gp, gc, gn, i, s, tr)
        ue = _extended(upp, upc, upn, i, s, tr)
        de = _extended(dp, dc, dn, i, s, tr)
        w = w_ref[...]
        g_prev, g_next = pltpu.roll(ge, 1, 0), pltpu.roll(ge, ext - 1, 0)
        conv = w[0:1] * g_prev + w[1:2] * ge + w[2:3] * g_next + b_ref[...]
        sg = _sigmoid(conv)
        du_ref[...] = (de * (conv * sg))[HALO:HALO + tr].astype(BF16)
        dconv = de * ue * (sg * (1.0 + conv * (1.0 - sg)))
        dgate = w[0:1] * pltpu.roll(dconv, ext - 1, 0) + w[1:2] * dconv + w[2:3] * pltpu.roll(dconv, 1, 0)
        dg_ref[...] = dgate[HALO:HALO + tr].astype(BF16)
        inner = slice(HALO, HALO + tr)
        dci = dconv[inner]

        @pl.when(i == 0)
        def _():
            gw_ref[...] = jnp.zeros_like(gw_ref)
            gb_ref[...] = jnp.zeros_like(gb_ref)

        gw_ref[0:1, :] += jnp.sum(dci * g_prev[inner], axis=0, keepdims=True)
        gw_ref[1:2, :] += jnp.sum(dci * ge[inner], axis=0, keepdims=True)
        gw_ref[2:3, :] += jnp.sum(dci * g_next[inner], axis=0, keepdims=True)
        gb_ref[...] += jnp.sum(dci, axis=0, keepdims=True)

    prev, cur, nxt = _halo_specs(s, tr, tc, lambda c: c)
    wspec = pl.BlockSpec((3, tc), lambda c, i: (0, c))
    bspec = pl.BlockSpec((1, tc), lambda c, i: (0, c))
    return pl.pallas_call(
        body, name=name, grid=(f // tc, s // tr),
        in_specs=[prev, cur, nxt] * 3 + [wspec, bspec],
        out_specs=[cur, cur, wspec, bspec],
        out_shape=[jax.ShapeDtypeStruct((s, f), BF16), jax.ShapeDtypeStruct((s, f), BF16),
                   jax.ShapeDtypeStruct((3, f), F32), jax.ShapeDtypeStruct((1, f), F32)],
        compiler_params=_params(("parallel", "arbitrary")),
    )(dact, dact, dact, gate, gate, gate, up, up, up, conv_w, conv_b)


def _local_step(x, target, w):
    s = x.shape[0]
    tables = _rope_tables(s)
    uf, ub = _gate_matrices(w["gf_up"], w["gb_up"])

    n1 = _rms_fwd(x, w["norm1_g"], "norm1")
    proj = _matmul([(n1, w["w_in"])], "nn", F32, 1024, 896, D_MODEL, "in_proj")
    q, k, v = _rope_fwd(proj, tables)
    branches = [_attn_fwd(q, k, v, d, f"attn_fwd_d{d}") for d in DILATIONS]
    o_mix, lse, ao = _attn_combine([b[0] for b in branches], [b[1] for b in branches], w["attn_norm_g"])
    g_f, g_b = _gla_gates(proj, uf, ub, w["gf_b"], w["gb_b"])
    o_f, st_f = _gla_fwd(proj, g_f, False, "gla_fwd_f")
    o_b, st_b = _gla_fwd(proj, g_b, True, "gla_fwd_b")
    go = _gla_post(o_f, o_b, proj, w["gla_norm_g"])
    cat = jnp.concatenate([ao, go], axis=1)
    h1 = _matmul([(cat, w["w_out"])], "nn", F32, 512, 1024, D_MODEL, "out_proj", res=x)
    n2 = _rms_fwd(h1, w["norm2_g"], "norm2")
    gate = _matmul([(n2, w["w_gate"])], "nn", F32, 512, 1408, D_MODEL, "ffn_gate")
    up = _matmul([(n2, w["w_up"])], "nn", F32, 512, 1408, D_MODEL, "ffn_up")
    act = _conv_glu(gate, up, w["conv_w"], w["conv_b"])
    h2 = _matmul([(act, w["w_down"])], "nn", F32, 512, 1024, 1408, "ffn_down", res=h1)
    dh2, dh2_b, loss_acc, g_final = _final_loss(h2, target, w["final_norm_g"])

    dact = _matmul([(dh2_b, w["w_down"])], "nt", F32, 512, 1408, D_MODEL, "d_act")
    g_w_down = _matmul([(act, dh2_b)], "tn", F32, 1408, 1024, 512, "g_w_down")
    dgate, dup, g_conv_w, g_conv_b = _conv_glu_bwd(dact, gate, up, w["conv_w"], w["conv_b"])
    dn2 = _matmul([(dgate, w["w_gate"]), (dup, w["w_up"])], "nt", F32, 512, 1024, 1408, "d_n2")
    g_w_gate = _matmul([(n2, dgate)], "tn", F32, 1024, 1408, 512, "g_w_gate")
    g_w_up = _matmul([(n2, dup)], "tn", F32, 1024, 1408, 512, "g_w_up")
    dh1, dh1_b, g_norm2 = _rms_bwd(dn2, h1, w["norm2_g"], dh2, "norm2_bwd")

    dcat = _matmul([(dh1_b, w["w_out"])], "nt", F32, 512, 1024, D_MODEL, "d_cat")
    g_w_out = _matmul([(cat, dh1_b)], "tn", F32, 1024, 1024, 512, "g_w_out")
    do_attn, delta, g_attn_norm = _attn_prebwd(dcat, o_mix, w["attn_norm_g"])
    grads = [_attn_bwd(q, k, v, do_attn, lse, delta, d, f"attn_bwd_d{d}") for d in DILATIONS]
    dpa = _rope_bwd([g[0] for g in grads], [g[1] for g in grads], [g[2] for g in grads], tables)
    do_gla, dgr, g_gla_norm = _gla_post_bwd(dcat, o_f, o_b, proj, w["gla_norm_g"])
    dq_f, dk_f, dv_f, dg_f = _gla_bwd(proj, g_f, do_gla, st_f, False, "gla_bwd_f")
    dq_b, dk_b, dv_b, dg_b = _gla_bwd(proj, g_b, do_gla, st_b, True, "gla_bwd_b")
    dz, g_uf, g_ub, g_gf_b, g_gb_b = _gla_gates_bwd(dg_f, dg_b, proj, uf, ub, w["gf_b"], w["gb_b"])
    dproj = _assemble_dproj(dpa, dq_f, dq_b, dk_f, dk_b, dv_f, dv_b, dgr, dz)
    dn1 = _matmul([(dproj, w["w_in"])], "nt", F32, 512, 1024, 896, "d_n1")
    g_w_in = _matmul([(n1, dproj)], "tn", F32, 1024, 896, 512, "g_w_in")
    grad_x, _, g_norm1 = _rms_bwd(dn1, x, w["norm1_g"], dh1, "norm1_bwd")

    g = dict(norm1_g=g_norm1, w_in=g_w_in, gf_up=g_uf[:GLA_RANK], gf_b=g_gf_b,
             gb_up=g_ub[GLA_RANK:2 * GLA_RANK], gb_b=g_gb_b, gla_norm_g=g_gla_norm, attn_norm_g=g_attn_norm,
             w_out=g_w_out, norm2_g=g_norm2, w_gate=g_w_gate, w_up=g_w_up, conv_w=g_conv_w, conv_b=g_conv_b,
             w_down=g_w_down, final_norm_g=g_final)
    return loss_acc, grad_x, g


def _me_and_peers():
    x, y, c = lax.axis_index("x"), lax.axis_index("y"), lax.axis_index("c")
    me = 4 * x + 2 * y + c
    peers = []
    for kbits in range(1, N_DEV):
        px, py, pc = x ^ (kbits >> 2 & 1), y ^ (kbits >> 1 & 1), c ^ (kbits & 1)
        peers.append(((px, py, pc), 4 * px + 2 * py + pc))
    return me, peers


def _all_gather_hbm(shards, name):
    n = len(shards)

    def body(*refs):
        in_refs, out_refs = refs[:n], refs[n:2 * n]
        send_sems, recv_sems, local_sems = refs[2 * n:]
        me, peers = _me_and_peers()
        copies = []
        for a in range(n):
            local = pltpu.make_async_copy(in_refs[a], out_refs[a].at[me], local_sems.at[a])
            local.start()
            copies.append(local)
        sends = []
        for a in range(n):
            for kk, (dev, _) in enumerate(peers):
                cp = pltpu.make_async_remote_copy(
                    src_ref=in_refs[a], dst_ref=out_refs[a].at[me],
                    send_sem=send_sems.at[a, kk], recv_sem=recv_sems.at[a, kk],
                    device_id=dev, device_id_type=MESH_ID)
                cp.start()
                sends.append(cp)
        for a in range(n):
            for kk, (dev, idx) in enumerate(peers):
                pltpu.make_async_remote_copy(
                    src_ref=in_refs[a], dst_ref=out_refs[a].at[idx],
                    send_sem=send_sems.at[a, kk], recv_sem=recv_sems.at[a, kk],
                    device_id=dev, device_id_type=MESH_ID).wait_recv()
        for cp in sends:
            cp.wait_send()
        for cp in copies:
            cp.wait()

    any_spec = pl.BlockSpec(memory_space=pl.ANY)
    return pl.pallas_call(
        body, name=name,
        in_specs=[any_spec] * n, out_specs=[any_spec] * n,
        out_shape=[jax.ShapeDtypeStruct((N_DEV,) + t.shape, t.dtype) for t in shards],
        scratch_shapes=[pltpu.SemaphoreType.DMA((n, N_DEV - 1)), pltpu.SemaphoreType.DMA((n, N_DEV - 1)),
                        pltpu.SemaphoreType.DMA((n,))],
    )(*shards)


def _exchange_partials_hbm(partials, name):
    n = len(partials)

    def body(*refs):
        in_refs, out_refs = refs[:n], refs[n:2 * n]
        send_sems, recv_sems, local_sems = refs[2 * n:]
        me, peers = _me_and_peers()
        copies = []
        for a in range(n):
            local = pltpu.make_async_copy(in_refs[a].at[me], out_refs[a].at[me], local_sems.at[a])
            local.start()
            copies.append(local)
        sends = []
        for a in range(n):
            for kk, (dev, idx) in enumerate(peers):
                cp = pltpu.make_async_remote_copy(
                    src_ref=in_refs[a].at[idx], dst_ref=out_refs[a].at[me],
                    send_sem=send_sems.at[a, kk], recv_sem=recv_sems.at[a, kk],
                    device_id=dev, device_id_type=MESH_ID)
                cp.start()
                sends.append(cp)
        for a in range(n):
            for kk, (dev, idx) in enumerate(peers):
                pltpu.make_async_remote_copy(
                    src_ref=in_refs[a].at[idx], dst_ref=out_refs[a].at[idx],
                    send_sem=send_sems.at[a, kk], recv_sem=recv_sems.at[a, kk],
                    device_id=dev, device_id_type=MESH_ID).wait_recv()
        for cp in sends:
            cp.wait_send()
        for cp in copies:
            cp.wait()

    any_spec = pl.BlockSpec(memory_space=pl.ANY)
    return pl.pallas_call(
        body, name=name,
        in_specs=[any_spec] * n, out_specs=[any_spec] * n,
        out_shape=[jax.ShapeDtypeStruct(t.shape, t.dtype) for t in partials],
        scratch_shapes=[pltpu.SemaphoreType.DMA((n, N_DEV - 1)), pltpu.SemaphoreType.DMA((n, N_DEV - 1)),
                        pltpu.SemaphoreType.DMA((n,))],
    )(*partials)


def _all_gather_vmem(vec, name):
    r = vec.shape[0]

    def body(v_ref, o_ref, send_sems, recv_sems):
        me, peers = _me_and_peers()
        o_ref[me] = v_ref[...]
        sends = []
        for kk, (dev, _) in enumerate(peers):
            cp = pltpu.make_async_remote_copy(
                src_ref=v_ref, dst_ref=o_ref.at[me],
                send_sem=send_sems.at[kk], recv_sem=recv_sems.at[kk],
                device_id=dev, device_id_type=MESH_ID)
            cp.start()
            sends.append(cp)
        for kk, (dev, idx) in enumerate(peers):
            pltpu.make_async_remote_copy(
                src_ref=v_ref, dst_ref=o_ref.at[idx],
                send_sem=send_sems.at[kk], recv_sem=recv_sems.at[kk],
                device_id=dev, device_id_type=MESH_ID).wait_recv()
        for cp in sends:
            cp.wait_send()

    return pl.pallas_call(
        body, name=name,
        in_specs=[pl.BlockSpec(memory_space=pltpu.VMEM)],
        out_specs=pl.BlockSpec(memory_space=pltpu.VMEM),
        out_shape=jax.ShapeDtypeStruct((N_DEV, r, LANES), F32),
        scratch_shapes=[pltpu.SemaphoreType.DMA((N_DEV - 1,)), pltpu.SemaphoreType.DMA((N_DEV - 1,))],
        compiler_params=pltpu.CompilerParams(vmem_limit_bytes=VMEM_LIMIT),
    )(vec)


def _adamw_math(w, g, m, v):
    m = ADAM_B1 * m + (1.0 - ADAM_B1) * g
    v = ADAM_B2 * v + (1.0 - ADAM_B2) * (g * g)
    m_hat = m / (1.0 - ADAM_B1 ** ADAM_STEP)
    v_hat = v / (1.0 - ADAM_B2 ** ADAM_STEP)
    delta = -ADAM_LR * (m_hat / (jnp.sqrt(v_hat) + ADAM_EPS) + ADAM_WD * w)
    return delta, m, v


def _adamw_sum(parts, w, m, v, tr, name):
    r, c = w.shape

    def body(p_ref, w_ref, m_ref, v_ref, g_ref, d_ref, nm_ref, nv_ref):
        g = p_ref[0]
        for kk in range(1, N_DEV):
            g = g + p_ref[kk]
        g_ref[...] = g
        d_ref[...], nm_ref[...], nv_ref[...] = _adamw_math(w_ref[...], g, m_ref[...], v_ref[...])

    blk = pl.BlockSpec((tr, c), lambda i: (i, 0))
    return pl.pallas_call(
        body, name=name, grid=(r // tr,),
        in_specs=[pl.BlockSpec((N_DEV, tr, c), lambda i: (0, i, 0)), blk, blk, blk],
        out_specs=[blk] * 4,
        out_shape=[jax.ShapeDtypeStruct((r, c), F32)] * 4,
        compiler_params=_params(("parallel",)),
    )(parts, w, m, v)


_SMALL = ("norm1_g", "gf_b", "gb_b", "gla_norm_g", "attn_norm_g", "norm2_g", "conv_b", "final_norm_g",
          "gf_up", "gb_up", "conv_w")


def _pack(named):
    flat = jnp.concatenate([jnp.ravel(t).astype(F32) for t in named])
    tile = SUBLANES * LANES
    total = -(-flat.shape[0] // tile) * tile
    return jnp.pad(flat, (0, total - flat.shape[0])).reshape(total // LANES, LANES)


def _unpack(packed, shapes):
    flat = packed.reshape(-1)
    out, off = [], 0
    for shp in shapes:
        size = int(np.prod(shp))
        out.append(flat[off:off + size].reshape(shp))
        off += size
    return out


def kernel(x, norm1_g, w_in, gf_up, gf_b, gb_up, gb_b, gla_norm_g, attn_norm_g, w_out, norm2_g, w_gate, w_up, conv_w, conv_b, w_down, final_norm_g, loss_target, m_norm1_g, m_w_in, m_gf_up, m_gf_b, m_gb_up, m_gb_b, m_gla_norm_g, m_attn_norm_g, m_w_out, m_norm2_g, m_w_gate, m_w_up, m_conv_w, m_conv_b, m_w_down, m_final_norm_g, v_norm1_g, v_w_in, v_gf_up, v_gf_b, v_gb_up, v_gb_b, v_gla_norm_g, v_attn_norm_g, v_w_out, v_norm2_g, v_w_gate, v_w_up, v_conv_w, v_conv_b, v_w_down, v_final_norm_g):
    names = ("norm1_g", "w_in", "gf_up", "gf_b", "gb_up", "gb_b", "gla_norm_g", "attn_norm_g", "w_out", "norm2_g",
             "w_gate", "w_up", "conv_w", "conv_b", "w_down", "final_norm_g")
    ws = dict(zip(names, (norm1_g, w_in, gf_up, gf_b, gb_up, gb_b, gla_norm_g, attn_norm_g, w_out, norm2_g,
                          w_gate, w_up, conv_w, conv_b, w_down, final_norm_g)))
    ms = dict(zip(names, (m_norm1_g, m_w_in, m_gf_up, m_gf_b, m_gb_up, m_gb_b, m_gla_norm_g, m_attn_norm_g, m_w_out,
                          m_norm2_g, m_w_gate, m_w_up, m_conv_w, m_conv_b, m_w_down, m_final_norm_g)))
    vs = dict(zip(names, (v_norm1_g, v_w_in, v_gf_up, v_gf_b, v_gb_up, v_gb_b, v_gla_norm_g, v_attn_norm_g, v_w_out,
                          v_norm2_g, v_w_gate, v_w_up, v_conv_w, v_conv_b, v_w_down, v_final_norm_g)))
    me = 4 * lax.axis_index("x") + 2 * lax.axis_index("y") + lax.axis_index("c")
    big = ("w_in", "w_out", "w_gate", "w_up", "w_down")
    col_sharded = ("w_in", "w_gate", "w_up")

    gathered = _all_gather_hbm([ws[n][0].astype(BF16) for n in big], "gather_weights")
    full = {}
    for n, t in zip(big, gathered):
        if n in col_sharded:
            full[n] = jnp.transpose(t, (1, 0, 2)).reshape(t.shape[1], N_DEV * t.shape[2])
        else:
            full[n] = t.reshape(N_DEV * t.shape[1], t.shape[2])
    full["w_in"] = jnp.pad(full["w_in"], ((0, 0), (0, IN_PAD - IN_WIDTH)))
    small_sharded = ("gf_up", "gb_up", "conv_w")
    sm = _all_gather_vmem(_pack([ws[n][0] for n in small_sharded]), "gather_small")
    shard_shapes = [ws[n][0].shape for n in small_sharded]
    per_dev = [_unpack(sm[d], shard_shapes) for d in range(N_DEV)]
    for i, n in enumerate(small_sharded):
        full[n] = jnp.concatenate([per_dev[d][i] for d in range(N_DEV)], axis=1)
    for n in ("norm1_g", "gf_b", "gb_b", "gla_norm_g", "attn_norm_g", "norm2_g", "conv_b"):
        full[n] = ws[n]
    full["final_norm_g"] = final_norm_g.reshape(1, D_MODEL)

    loss_acc, grad_x, g = _local_step(x[0], loss_target[0], full)

    partials = []
    for n in big:
        t = g[n]
        if n == "w_in":
            t = t[:, :IN_WIDTH]
        if n in col_sharded:
            t = jnp.transpose(t.reshape(t.shape[0], N_DEV, t.shape[1] // N_DEV), (1, 0, 2))
        else:
            t = t.reshape(N_DEV, t.shape[0] // N_DEV, t.shape[1])
        partials.append(t)
    landed = _exchange_partials_hbm(partials, "exchange_grads")
    out = {}
    for n, parts in zip(big, landed):
        out[n] = _adamw_sum(parts, ws[n][0], ms[n][0], vs[n][0], 64, "adamw_" + n)

    small_full_shapes = [g[n].shape for n in _SMALL]
    gsmall = _pack([g[n] for n in _SMALL] + [loss_acc[0:1, 0:1]])
    gathered_small = _all_gather_vmem(gsmall, "gather_small_grads")

    def full_small(d):
        parts = []
        for n in _SMALL:
            t = d[n].reshape(d[n].shape[-2:]) if d[n].ndim == 3 else d[n].reshape(1, -1)
            if n in small_sharded:
                wide = jnp.zeros((t.shape[0], t.shape[1] * N_DEV), F32)
                t = lax.dynamic_update_slice_in_dim(wide, t, me * t.shape[1], axis=1)
            parts.append(t)
        return _pack(parts + [jnp.zeros((1, 1), F32)])

    rows = gsmall.shape[0]
    res_small = _adamw_sum(gathered_small, full_small(ws), full_small(ms), full_small(vs), rows, "adamw_small")
    loss = res_small[0].reshape(-1)[sum(int(np.prod(sh)) for sh in small_full_shapes)]
    unpacked = [_unpack(t, small_full_shapes) for t in res_small]
    for i, n in enumerate(_SMALL):
        vals = [u[i] for u in unpacked]
        if n in small_sharded:
            width = vals[0].shape[1] // N_DEV
            vals = [lax.dynamic_slice_in_dim(t, me * width, width, axis=1) for t in vals]
        out[n] = vals

    result = [loss, grad_x[None]]
    for kind in range(4):
        for n in names:
            result.append(out[n][kind].reshape(ws[n].shape))
    return tuple(result)
```

```python
import functools

import numpy as np
import jax
import jax.numpy as jnp
from jax import lax
from jax.experimental import pallas as pl
from jax.experimental.pallas import tpu as pltpu

F32 = jnp.float32
BF16 = jnp.bfloat16

D_MODEL = 2048
ATTN_W = 1024
ATTN_HEADS = 8
HEAD_DIM = 128
ROPE_DIM = 32
ROPE_THETA = 500000.0
DILATIONS = (1, 4, 16)
N_SIDE = 64
GLA_KW = 512
GLA_VW = 1024
GLA_HEADS = 4
GLA_DK = 128
GLA_DV = 256
GLA_RANK = 16
GLA_GATE_NORM = 16.0
GLA_CHUNK = 64
IN_WIDTH = 6176
IN_PAD = 6272
D_FF = 5632
EPS = 1e-6
N_DEV = 8

OFF_AQ, OFF_AK, OFF_AV = 0, 1024, 2048
OFF_GQ, OFF_GK, OFF_GV, OFF_GR, OFF_Z = 3072, 3584, 4096, 5120, 6144

ADAM_LR, ADAM_B1, ADAM_B2, ADAM_EPS, ADAM_WD, ADAM_STEP = 0.001, 0.9, 0.999, 1e-08, 0.01, 10

LANES = 128
SUBLANES = 8
VMEM_LIMIT = 56 * 1024 * 1024
ROW_BLOCK = 256
ATTN_BLOCK = 128
GLA_CHUNKS_PER_STEP = 4
NEG = -1e30
MESH_ID = pl.DeviceIdType.MESH


def _params(sem):
    return pltpu.CompilerParams(dimension_semantics=sem, vmem_limit_bytes=VMEM_LIMIT)


def _dot(a, b):
    return lax.dot_general(a, b, (((1,), (0,)), ((), ())), preferred_element_type=F32)


def _dot_nt(a, b):
    return lax.dot_general(a, b, (((1,), (1,)), ((), ())), preferred_element_type=F32)


def _dot_tn(a, b):
    return lax.dot_general(a, b, (((0,), (0,)), ((), ())), preferred_element_type=F32)


def _sigmoid(x):
    return 1.0 / (1.0 + jnp.exp(-x))


def _matmul(pairs, mode, out_dtype, tm, tn, tk, name, res=None, deps=()):
    a0, b0 = pairs[0]
    if mode == "nn":
        (m, kdim), n = a0.shape, b0.shape[1]
    elif mode == "nt":
        (m, kdim), n = a0.shape, b0.shape[0]
    else:
        (kdim, m), n = a0.shape, b0.shape[1]
    assert m % tm == 0 and n % tn == 0 and kdim % tk == 0, (name, m, n, kdim)
    nk = kdim // tk
    npairs = len(pairs)
    steps = nk * npairs
    dot = {"nn": _dot, "nt": _dot_nt, "tn": _dot_tn}[mode]

    def kidx(p):
        return lambda k: jnp.clip(k - p * nk, 0, nk - 1)

    in_specs, args = [], []
    for p, (a, b) in enumerate(pairs):
        kk = kidx(p)
        if mode == "nn":
            in_specs += [pl.BlockSpec((tm, tk), lambda i, j, k, kk=kk: (i, kk(k))),
                         pl.BlockSpec((tk, tn), lambda i, j, k, kk=kk: (kk(k), j))]
        elif mode == "nt":
            in_specs += [pl.BlockSpec((tm, tk), lambda i, j, k, kk=kk: (i, kk(k))),
                         pl.BlockSpec((tn, tk), lambda i, j, k, kk=kk: (j, kk(k)))]
        else:
            in_specs += [pl.BlockSpec((tk, tm), lambda i, j, k, kk=kk: (kk(k), i)),
                         pl.BlockSpec((tk, tn), lambda i, j, k, kk=kk: (kk(k), j))]
        args += [a, b]
    if res is not None:
        in_specs.append(pl.BlockSpec((tm, tn), lambda i, j, k: (i, j)))
        args.append(res)
    in_specs += [pl.BlockSpec(memory_space=pl.ANY)] * len(deps)
    args += list(deps)

    def body(*refs):
        ab = refs[:2 * npairs]
        res_ref = refs[2 * npairs] if res is not None else None
        o_ref = refs[2 * npairs + (1 if res is not None else 0) + len(deps)]

        def finish(acc):
            if res_ref is not None:
                acc = acc + res_ref[...]
            o_ref[...] = acc.astype(out_dtype)

        if steps == 1:
            finish(dot(ab[0][...], ab[1][...]))
            return
        acc_ref = refs[-1]
        k = pl.program_id(2)

        @pl.when(k == 0)
        def _():
            acc_ref[...] = jnp.zeros_like(acc_ref)

        for p in range(npairs):
            @pl.when((k >= p * nk) & (k < (p + 1) * nk))
            def _(p=p):
                acc_ref[...] += dot(ab[2 * p][...], ab[2 * p + 1][...])

        @pl.when(k == steps - 1)
        def _():
            finish(acc_ref[...])

    return pl.pallas_call(
        body, name=name,
        grid=(m // tm, n // tn, steps),
        in_specs=in_specs,
        out_specs=pl.BlockSpec((tm, tn), lambda i, j, k: (i, j)),
        out_shape=jax.ShapeDtypeStruct((m, n), out_dtype),
        scratch_shapes=[] if steps == 1 else [pltpu.VMEM((tm, tn), F32)],
        compiler_params=_params(("parallel", "parallel", "arbitrary")),
    )(*args)


def _rms_fwd(x, g, name):
    s, d = x.shape

    def body(x_ref, g_ref, o_ref):
        xv = x_ref[...]
        r = lax.rsqrt(jnp.mean(xv * xv, axis=-1, keepdims=True) + EPS)
        o_ref[...] = (xv * r * g_ref[...]).astype(BF16)

    return pl.pallas_call(
        body, name=name, grid=(s // ROW_BLOCK,),
        in_specs=[pl.BlockSpec((ROW_BLOCK, d), lambda i: (i, 0)), pl.BlockSpec((1, d), lambda i: (0, 0))],
        out_specs=pl.BlockSpec((ROW_BLOCK, d), lambda i: (i, 0)),
        out_shape=jax.ShapeDtypeStruct((s, d), BF16),
        compiler_params=_params(("parallel",)),
    )(x, g)


def _rms_bwd(dn, x, g, dres, name):
    s, d = x.shape

    def body(dn_ref, x_ref, g_ref, dres_ref, dx_ref, dxb_ref, gg_ref):
        i = pl.program_id(0)
        xv, dnv = x_ref[...], dn_ref[...]
        r = lax.rsqrt(jnp.mean(xv * xv, axis=-1, keepdims=True) + EPS)
        dng = dnv * g_ref[...]
        c = jnp.mean(dng * xv, axis=-1, keepdims=True)
        dx = dres_ref[...] + r * dng - xv * (r * r * r * c)
        dx_ref[...] = dx
        dxb_ref[...] = dx.astype(BF16)

        @pl.when(i == 0)
        def _():
            gg_ref[...] = jnp.zeros_like(gg_ref)

        gg_ref[...] += jnp.sum(dnv * xv * r, axis=0, keepdims=True)

    row = pl.BlockSpec((ROW_BLOCK, d), lambda i: (i, 0))
    vec = pl.BlockSpec((1, d), lambda i: (0, 0))
    return pl.pallas_call(
        body, name=name, grid=(s // ROW_BLOCK,),
        in_specs=[row, row, vec, row],
        out_specs=[row, row, vec],
        out_shape=[jax.ShapeDtypeStruct((s, d), F32), jax.ShapeDtypeStruct((s, d), BF16),
                   jax.ShapeDtypeStruct((1, d), F32)],
        compiler_params=_params(("arbitrary",)),
    )(dn, x, g, dres)


def _final_loss(h2, target, g, name="final_loss"):
    s, d = h2.shape

    def body(h_ref, t_ref, g_ref, dh_ref, dhb_ref, loss_ref, gg_ref):
        i = pl.program_id(0)
        hv, gv = h_ref[...], g_ref[...]
        r = lax.rsqrt(jnp.mean(hv * hv, axis=-1, keepdims=True) + EPS)
        e = hv * r * gv - t_ref[...]
        dy = e * (1.0 / d)
        dyg = dy * gv
        c = jnp.mean(dyg * hv, axis=-1, keepdims=True)
        dh = r * dyg - hv * (r * r * r * c)
        dh_ref[...] = dh
        dhb_ref[...] = dh.astype(BF16)

        @pl.when(i == 0)
        def _():
            gg_ref[...] = jnp.zeros_like(gg_ref)
            loss_ref[...] = jnp.zeros_like(loss_ref)

        gg_ref[...] += jnp.sum(dy * hv * r, axis=0, keepdims=True)
        loss_ref[...] += jnp.sum(jnp.sum(e * e, axis=-1, keepdims=True), axis=0, keepdims=True) * (0.5 / d)

    row = pl.BlockSpec((ROW_BLOCK, d), lambda i: (i, 0))
    vec = pl.BlockSpec((1, d), lambda i: (0, 0))
    return pl.pallas_call(
        body, name=name, grid=(s // ROW_BLOCK,),
        in_specs=[row, row, vec],
        out_specs=[row, row, pl.BlockSpec((SUBLANES, LANES), lambda i: (0, 0)), vec],
        out_shape=[jax.ShapeDtypeStruct((s, d), F32), jax.ShapeDtypeStruct((s, d), BF16),
                   jax.ShapeDtypeStruct((SUBLANES, LANES), F32), jax.ShapeDtypeStruct((1, d), F32)],
        compiler_params=_params(("arbitrary",)),
    )(h2, target, g)


def _rope_tables(s):
    pos = jnp.arange(s, dtype=F32)
    inv_freq = ROPE_THETA ** (-jnp.arange(0, ROPE_DIM, 2, dtype=F32) / ROPE_DIM)
    ang = pos[:, None] * inv_freq[None, :]
    cos, sin = jnp.cos(ang), jnp.sin(ang)
    half = ROPE_DIM // 2
    rest = HEAD_DIM - ROPE_DIM
    c = jnp.concatenate([cos, cos, jnp.ones((s, rest), F32)], axis=1)
    sm = jnp.concatenate([-sin, jnp.zeros((s, half + rest), F32)], axis=1)
    sp = jnp.concatenate([jnp.zeros((s, half), F32), sin, jnp.zeros((s, rest), F32)], axis=1)
    return c, sm, sp


def _rope_fwd(proj, tables, name="rope_fwd"):
    s = proj.shape[0]
    half = ROPE_DIM // 2

    def body(p_ref, c_ref, sm_ref, sp_ref, q_ref, k_ref, v_ref):
        c, sm, sp = c_ref[...], sm_ref[...], sp_ref[...]
        for off, o_ref in ((OFF_AQ, q_ref), (OFF_AK, k_ref)):
            for h in range(ATTN_HEADS):
                t = p_ref[:, off + h * HEAD_DIM: off + (h + 1) * HEAD_DIM]
                o = t * c + pltpu.roll(t, HEAD_DIM - half, 1) * sm + pltpu.roll(t, half, 1) * sp
                o_ref[:, h * HEAD_DIM:(h + 1) * HEAD_DIM] = o.astype(BF16)
        v_ref[...] = p_ref[:, OFF_AV:OFF_AV + ATTN_W].astype(BF16)

    tab = pl.BlockSpec((ROW_BLOCK, HEAD_DIM), lambda i: (i, 0))
    out = pl.BlockSpec((ROW_BLOCK, ATTN_W), lambda i: (i, 0))
    return pl.pallas_call(
        body, name=name, grid=(s // ROW_BLOCK,),
        in_specs=[pl.BlockSpec((ROW_BLOCK, 3 * ATTN_W), lambda i: (i, 0)), tab, tab, tab],
        out_specs=[out, out, out],
        out_shape=[jax.ShapeDtypeStruct((s, ATTN_W), BF16)] * 3,
        compiler_params=_params(("parallel",)),
    )(proj, *tables)


def _rope_bwd(dqs, dks, dvs, tables, name="rope_bwd"):
    s = dqs[0].shape[0]
    half = ROPE_DIM // 2
    nbr = len(dqs)

    def body(*refs):
        dq_refs, dk_refs, dv_refs = refs[:nbr], refs[nbr:2 * nbr], refs[2 * nbr:3 * nbr]
        c_ref, sm_ref, sp_ref, o_ref = refs[3 * nbr:]
        c, sm, sp = c_ref[...], sm_ref[...], sp_ref[...]
        for off, grp in ((OFF_AQ, dq_refs), (OFF_AK, dk_refs)):
            for h in range(ATTN_HEADS):
                sl = slice(h * HEAD_DIM, (h + 1) * HEAD_DIM)
                t = grp[0][:, sl]
                for r in grp[1:]:
                    t = t + r[:, sl]
                o = t * c + pltpu.roll(t * sm, half, 1) + pltpu.roll(t * sp, HEAD_DIM - half, 1)
                o_ref[:, off + h * HEAD_DIM: off + (h + 1) * HEAD_DIM] = o.astype(BF16)
        t = dv_refs[0][...]
        for r in dv_refs[1:]:
            t = t + r[...]
        o_ref[:, OFF_AV:OFF_AV + ATTN_W] = t.astype(BF16)

    tab = pl.BlockSpec((ROW_BLOCK, HEAD_DIM), lambda i: (i, 0))
    blk = pl.BlockSpec((ROW_BLOCK, ATTN_W), lambda i: (i, 0))
    return pl.pallas_call(
        body, name=name, grid=(s // ROW_BLOCK,),
        in_specs=[blk] * (3 * nbr) + [tab, tab, tab],
        out_specs=pl.BlockSpec((ROW_BLOCK, 3 * ATTN_W), lambda i: (i, 0)),
        out_shape=jax.ShapeDtypeStruct((s, 3 * ATTN_W), BF16),
        compiler_params=_params(("parallel",)),
    )(*dqs, *dks, *dvs, *tables)


def _band_valid(qpos, kpos, length):
    return (jnp.abs(kpos - qpos) <= N_SIDE) & (kpos >= 0) & (kpos < length) & (qpos >= 0) & (qpos < length)


def _attn_fwd(q, k, v, dil, name):
    s = q.shape[0]
    length = s // dil
    qb = ATTN_BLOCK
    nb = length // qb
    scale = HEAD_DIM ** -0.5
    qv, kv, vv = (t.reshape(length, dil * ATTN_W) for t in (q, k, v))

    def body(q_ref, kp_ref, kc_ref, kn_ref, vp_ref, vc_ref, vn_ref, o_ref, lse_ref):
        j = pl.program_id(1)
        row = lax.broadcasted_iota(jnp.int32, (qb, qb), 0)
        col = lax.broadcasted_iota(jnp.int32, (qb, qb), 1)
        lane = lax.broadcasted_iota(jnp.int32, (qb, LANES), 1)
        qpos = j * qb + row
        valids = [_band_valid(qpos, (j + bi - 1) * qb + col, length) for bi in range(3)]
        lse_acc = jnp.zeros((qb, LANES), F32)
        for h in range(ATTN_HEADS):
            sl = slice(h * HEAD_DIM, (h + 1) * HEAD_DIM)
            qh = q_ref[:, sl]
            ss = [jnp.where(valids[bi], _dot_nt(qh, kr[:, sl]) * scale, NEG)
                  for bi, kr in enumerate((kp_ref, kc_ref, kn_ref))]
            m = jnp.maximum(jnp.maximum(jnp.max(ss[0], axis=-1, keepdims=True),
                                        jnp.max(ss[1], axis=-1, keepdims=True)),
                            jnp.max(ss[2], axis=-1, keepdims=True))
            ps = [jnp.exp(sv - m) for sv in ss]
            den = (jnp.sum(ps[0], axis=-1, keepdims=True) + jnp.sum(ps[1], axis=-1, keepdims=True)
                   + jnp.sum(ps[2], axis=-1, keepdims=True))
            acc = (_dot(ps[0].astype(BF16), vp_ref[:, sl]) + _dot(ps[1].astype(BF16), vc_ref[:, sl])
                   + _dot(ps[2].astype(BF16), vn_ref[:, sl]))
            o_ref[:, sl] = acc / den
            lse_acc = jnp.where(lane == h, m + jnp.log(den), lse_acc)
        lse_ref[...] = lse_acc

    cur = pl.BlockSpec((qb, ATTN_W), lambda r, j: (j, r))
    prev = pl.BlockSpec((qb, ATTN_W), lambda r, j: (jnp.maximum(j - 1, 0), r))
    nxt = pl.BlockSpec((qb, ATTN_W), lambda r, j: (jnp.minimum(j + 1, nb - 1), r))
    o, lse = pl.pallas_call(
        body, name=name, grid=(dil, nb),
        in_specs=[cur, prev, cur, nxt, prev, cur, nxt],
        out_specs=[cur, pl.BlockSpec((qb, LANES), lambda r, j: (j, r))],
        out_shape=[jax.ShapeDtypeStruct((length, dil * ATTN_W), F32),
                   jax.ShapeDtypeStruct((length, dil * LANES), F32)],
        compiler_params=_params(("parallel", "parallel")),
    )(qv, kv, kv, kv, vv, vv, vv)
    return o.reshape(s, ATTN_W), lse.reshape(s, LANES)


def _attn_combine(outs, lses, g, name="attn_combine"):
    s = outs[0].shape[0]
    nbr = len(outs)

    def body(*refs):
        o_refs, l_refs = refs[:nbr], refs[nbr:2 * nbr]
        g_ref, o_ref, lse_ref, n_ref = refs[2 * nbr:]
        ls = [r[...] for r in l_refs]
        m = ls[0]
        for l in ls[1:]:
            m = jnp.maximum(m, l)
        es = [jnp.exp(l - m) for l in ls]
        z = es[0]
        for e in es[1:]:
            z = z + e
        lse_ref[...] = m + jnp.log(z)
        ws = [e / z for e in es]
        ssq = jnp.zeros((ROW_BLOCK, 1), F32)
        for h in range(ATTN_HEADS):
            sl = slice(h * HEAD_DIM, (h + 1) * HEAD_DIM)
            acc = ws[0][:, h:h + 1] * o_refs[0][:, sl]
            for w, r in zip(ws[1:], o_refs[1:]):
                acc = acc + w[:, h:h + 1] * r[:, sl]
            o_ref[:, sl] = acc
            ssq = ssq + jnp.sum(acc * acc, axis=-1, keepdims=True)
        r = lax.rsqrt(ssq * (1.0 / ATTN_W) + EPS)
        n_ref[...] = (o_ref[...] * r * g_ref[...]).astype(BF16)

    blk = pl.BlockSpec((ROW_BLOCK, ATTN_W), lambda i: (i, 0))
    lblk = pl.BlockSpec((ROW_BLOCK, LANES), lambda i: (i, 0))
    return pl.pallas_call(
        body, name=name, grid=(s // ROW_BLOCK,),
        in_specs=[blk] * nbr + [lblk] * nbr + [pl.BlockSpec((1, ATTN_W), lambda i: (0, 0))],
        out_specs=[blk, lblk, blk],
        out_shape=[jax.ShapeDtypeStruct((s, ATTN_W), F32), jax.ShapeDtypeStruct((s, LANES), F32),
                   jax.ShapeDtypeStruct((s, ATTN_W), BF16)],
        compiler_params=_params(("parallel",)),
    )(*outs, *lses, g)


def _attn_prebwd(dcat, o, g, name="attn_prebwd"):
    s = o.shape[0]

    def body(dy_ref, o_ref, g_ref, do_ref, delta_ref, gg_ref):
        i = pl.program_id(0)
        dy, ov = dy_ref[...], o_ref[...]
        r = lax.rsqrt(jnp.mean(ov * ov, axis=-1, keepdims=True) + EPS)
        dyg = dy * g_ref[...]
        c = jnp.mean(dyg * ov, axis=-1, keepdims=True)
        do = r * dyg - ov * (r * r * r * c)
        do_ref[...] = do.astype(BF16)
        prod = do * ov
        lane = lax.broadcasted_iota(jnp.int32, (ROW_BLOCK, LANES), 1)
        acc = jnp.zeros((ROW_BLOCK, LANES), F32)
        for h in range(ATTN_HEADS):
            acc = jnp.where(lane == h, jnp.sum(prod[:, h * HEAD_DIM:(h + 1) * HEAD_DIM], axis=-1, keepdims=True), acc)
        delta_ref[...] = acc

        @pl.when(i == 0)
        def _():
            gg_ref[...] = jnp.zeros_like(gg_ref)

        gg_ref[...] += jnp.sum(dy * ov * r, axis=0, keepdims=True)

    blk = pl.BlockSpec((ROW_BLOCK, ATTN_W), lambda i: (i, 0))
    vec = pl.BlockSpec((1, ATTN_W), lambda i: (0, 0))
    return pl.pallas_call(
        body, name=name, grid=(s // ROW_BLOCK,),
        in_specs=[blk, blk, vec],
        out_specs=[blk, pl.BlockSpec((ROW_BLOCK, LANES), lambda i: (i, 0)), vec],
        out_shape=[jax.ShapeDtypeStruct((s, ATTN_W), BF16), jax.ShapeDtypeStruct((s, LANES), F32),
                   jax.ShapeDtypeStruct((1, ATTN_W), F32)],
        compiler_params=_params(("arbitrary",)),
    )(dcat, o, g)


def _attn_bwd(q, k, v, do, lse, delta, dil, name):
    s = q.shape[0]
    length = s // dil
    qb = ATTN_BLOCK
    nb = length // qb
    scale = HEAD_DIM ** -0.5
    qv, kv, vv, dov = (t.reshape(length, dil * ATTN_W) for t in (q, k, v, do))
    lv, dv_ = (t.reshape(length, dil * LANES) for t in (lse, delta))

    def body(qp, qc, qn, kp, kc, kn, vp, vc, vn, dop, doc, don, lp, lc, ln, dp, dc, dn, dq_ref, dk_ref, dv_ref):
        j = pl.program_id(1)
        row = lax.broadcasted_iota(jnp.int32, (qb, qb), 0)
        col = lax.broadcasted_iota(jnp.int32, (qb, qb), 1)
        pos_q = [(j + bi - 1) * qb + row for bi in range(3)]
        pos_k = [(j + bi - 1) * qb + col for bi in range(3)]
        valid_a = [_band_valid(pos_q[1], pos_k[bi], length) for bi in range(3)]
        valid_b = [_band_valid(pos_q[bi], pos_k[1], length) for bi in range(3)]
        for h in range(ATTN_HEADS):
            sl = slice(h * HEAD_DIM, (h + 1) * HEAD_DIM)
            hc = slice(h, h + 1)
            qh, kh, vh, doh = qc[:, sl], kc[:, sl], vc[:, sl], doc[:, sl]
            lse_c, del_c = lc[:, hc], dc[:, hc]
            dq = jnp.zeros((qb, HEAD_DIM), F32)
            dk = jnp.zeros((qb, HEAD_DIM), F32)
            dvh = jnp.zeros((qb, HEAD_DIM), F32)
            for bi, (kr, vr) in enumerate(((kp, vp), (kc, vc), (kn, vn))):
                kb, vb = kr[:, sl], vr[:, sl]
                p = jnp.where(valid_a[bi], jnp.exp(_dot_nt(qh, kb) * scale - lse_c), 0.0)
                ds = p * (_dot_nt(doh, vb) - del_c)
                dq = dq + _dot(ds.astype(BF16), kb)
            for bi, (qr, dor, lr, dr) in enumerate(((qp, dop, lp, dp), (qc, doc, lc, dc), (qn, don, ln, dn))):
                qx, dox = qr[:, sl], dor[:, sl]
                p = jnp.where(valid_b[bi], jnp.exp(_dot_nt(qx, kh) * scale - lr[:, hc]), 0.0)
                dvh = dvh + _dot_tn(p.astype(BF16), dox)
                ds = p * (_dot_nt(dox, vh) - dr[:, hc])
                dk = dk + _dot_tn(ds.astype(BF16), qx)
            dq_ref[:, sl] = dq * scale
            dk_ref[:, sl] = dk * scale
            dv_ref[:, sl] = dvh

    def specs(width):
        cur = pl.BlockSpec((qb, width), lambda r, j: (j, r))
        prev = pl.BlockSpec((qb, width), lambda r, j: (jnp.maximum(j - 1, 0), r))
        nxt = pl.BlockSpec((qb, width), lambda r, j: (jnp.minimum(j + 1, nb - 1), r))
        return [prev, cur, nxt]

    wide, narrow = specs(ATTN_W), specs(LANES)
    outs = pl.pallas_call(
        body, name=name, grid=(dil, nb),
        in_specs=wide * 4 + narrow * 2,
        out_specs=[wide[1]] * 3,
        out_shape=[jax.ShapeDtypeStruct((length, dil * ATTN_W), F32)] * 3,
        compiler_params=_params(("parallel", "parallel")),
    )(qv, qv, qv, kv, kv, kv, vv, vv, vv, dov, dov, dov, lv, lv, lv, dv_, dv_, dv_)
    return tuple(t.reshape(s, ATTN_W) for t in outs)


def _gate_matrices(gf_up, gb_up):
    pad = LANES - 2 * GLA_RANK
    uf = jnp.concatenate([gf_up, jnp.zeros((GLA_RANK + pad, GLA_KW), gf_up.dtype)], axis=0)
    ub = jnp.concatenate([jnp.zeros((GLA_RANK, GLA_KW), gb_up.dtype), gb_up, jnp.zeros((pad, GLA_KW), gb_up.dtype)], axis=0)
    return uf.astype(BF16), ub.astype(BF16)


def _log_sigmoid(x):
    return jnp.minimum(x, 0.0) - jnp.log(1.0 + jnp.exp(-jnp.abs(x)))


def _gla_gates(proj, uf, ub, gf_b, gb_b, name="gla_gates"):
    s = proj.shape[0]

    def body(z_ref, uf_ref, ub_ref, bf_ref, bb_ref, gf_ref, gb_ref):
        z = z_ref[...].astype(BF16)
        gf_ref[...] = _log_sigmoid(_dot(z, uf_ref[...]) + bf_ref[...]) * (1.0 / GLA_GATE_NORM)
        gb_ref[...] = _log_sigmoid(_dot(z, ub_ref[...]) + bb_ref[...]) * (1.0 / GLA_GATE_NORM)

    mat = pl.BlockSpec((LANES, GLA_KW), lambda i: (0, 0))
    vec = pl.BlockSpec((1, GLA_KW), lambda i: (0, 0))
    out = pl.BlockSpec((ROW_BLOCK, GLA_KW), lambda i: (i, 0))
    return pl.pallas_call(
        body, name=name, grid=(s // ROW_BLOCK,),
        in_specs=[pl.BlockSpec((ROW_BLOCK, LANES), lambda i: (i, OFF_Z // LANES)), mat, mat, vec, vec],
        out_specs=[out, out],
        out_shape=[jax.ShapeDtypeStruct((s, GLA_KW), F32)] * 2,
        compiler_params=_params(("parallel",)),
    )(proj, uf, ub, gf_b, gb_b)


def _gla_gates_bwd(dgf, dgb, proj, uf, ub, gf_b, gb_b, name="gla_gates_bwd"):
    s = proj.shape[0]

    def body(dgf_ref, dgb_ref, z_ref, uf_ref, ub_ref, bf_ref, bb_ref, dz_ref, guf_ref, gub_ref, gbf_ref, gbb_ref):
        i = pl.program_id(0)
        z = z_ref[...].astype(BF16)
        uf_, ub_ = uf_ref[...], ub_ref[...]
        dpf = dgf_ref[...] * (1.0 / GLA_GATE_NORM) * _sigmoid(-(_dot(z, uf_) + bf_ref[...]))
        dpb = dgb_ref[...] * (1.0 / GLA_GATE_NORM) * _sigmoid(-(_dot(z, ub_) + bb_ref[...]))
        dpf_b, dpb_b = dpf.astype(BF16), dpb.astype(BF16)
        dz_ref[...] = (_dot_nt(dpf_b, uf_) + _dot_nt(dpb_b, ub_)).astype(BF16)

        @pl.when(i == 0)
        def _():
            for r in (guf_ref, gub_ref, gbf_ref, gbb_ref):
                r[...] = jnp.zeros_like(r)

        guf_ref[...] += _dot_tn(z, dpf_b)
        gub_ref[...] += _dot_tn(z, dpb_b)
        gbf_ref[...] += jnp.sum(dpf, axis=0, keepdims=True)
        gbb_ref[...] += jnp.sum(dpb, axis=0, keepdims=True)

    mat = pl.BlockSpec((LANES, GLA_KW), lambda i: (0, 0))
    vec = pl.BlockSpec((1, GLA_KW), lambda i: (0, 0))
    blk = pl.BlockSpec((ROW_BLOCK, GLA_KW), lambda i: (i, 0))
    return pl.pallas_call(
        body, name=name, grid=(s // ROW_BLOCK,),
        in_specs=[blk, blk, pl.BlockSpec((ROW_BLOCK, LANES), lambda i: (i, OFF_Z // LANES)), mat, mat, vec, vec],
        out_specs=[pl.BlockSpec((ROW_BLOCK, LANES), lambda i: (i, 0)), mat, mat, vec, vec],
        out_shape=[jax.ShapeDtypeStruct((s, LANES), BF16), jax.ShapeDtypeStruct((LANES, GLA_KW), F32),
                   jax.ShapeDtypeStruct((LANES, GLA_KW), F32), jax.ShapeDtypeStruct((1, GLA_KW), F32),
                   jax.ShapeDtypeStruct((1, GLA_KW), F32)],
        compiler_params=_params(("arbitrary",)),
    )(dgf, dgb, proj, uf, ub, gf_b, gb_b)


def _split3(x):
    x1 = x.astype(BF16)
    r1 = x - x1.astype(F32)
    x2 = r1.astype(BF16)
    x3 = (r1 - x2.astype(F32)).astype(BF16)
    return x1, x2, x3


def _dot_exact(mask_bf, x):
    x1, x2, x3 = _split3(x)
    return _dot(mask_bf, x1) + _dot(mask_bf, x2) + _dot(mask_bf, x3)


def _chunk_terms(q_ref, k_ref, g_ref, rs, reverse):
    c = GLA_CHUNK
    row = lax.broadcasted_iota(jnp.int32, (c, c), 0)
    col = lax.broadcasted_iota(jnp.int32, (c, c), 1)
    allowed = (col >= row) if reverse else (col <= row)
    seen_by = (col <= row) if reverse else (col >= row)
    mid, last = (c // 2, 0) if reverse else (c // 2 - 1, c - 1)
    q = q_ref[rs, :] * (GLA_DK ** -0.5)
    k = k_ref[rs, :]
    b = _dot_exact(jnp.where(allowed, 1.0, 0.0).astype(BF16), g_ref[rs, :])
    bref, blast = b[mid:mid + 1, :], b[last:last + 1, :]
    e_q, e_k, e_in, e_st = jnp.exp(b - bref), jnp.exp(bref - b), jnp.exp(b), jnp.exp(blast - b)
    return dict(allowed=allowed, seen_by=seen_by, last=last, q=q, k=k, e_q=e_q, e_k=e_k, e_in=e_in, e_st=e_st,
                dec=jnp.exp(blast), qe=q * e_q, ke=k * e_k, qin=q * e_in, kst=k * e_st)


def _gla_blockspecs(s, reverse_order):
    cb = GLA_CHUNKS_PER_STEP
    rows = cb * GLA_CHUNK
    nsteps = s // rows

    def rb(n):
        return (nsteps - 1 - n) if reverse_order else n

    qspec = pl.BlockSpec((rows, GLA_DK), lambda h, n: (rb(n), OFF_GQ // GLA_DK + h))
    kspec = pl.BlockSpec((rows, GLA_DK), lambda h, n: (rb(n), OFF_GK // GLA_DK + h))
    vspec = pl.BlockSpec((rows, GLA_DV), lambda h, n: (rb(n), OFF_GV // GLA_DV + h))
    gspec = pl.BlockSpec((rows, GLA_DK), lambda h, n: (rb(n), h))
    ospec = pl.BlockSpec((rows, GLA_DV), lambda h, n: (rb(n), h))
    sspec = pl.BlockSpec((1, cb, GLA_DV, GLA_DK), lambda h, n: (h, rb(n), 0, 0))
    return cb, rows, nsteps, qspec, kspec, vspec, gspec, ospec, sspec


def _gla_fwd(proj, g, reverse, name):
    s = proj.shape[0]
    cb, rows, nsteps, qspec, kspec, vspec, gspec, ospec, sspec = _gla_blockspecs(s, reverse)

    def body(q_ref, k_ref, v_ref, g_ref, o_ref, st_ref, state):
        @pl.when(pl.program_id(1) == 0)
        def _():
            state[...] = jnp.zeros_like(state)

        for c in (reversed(range(cb)) if reverse else range(cb)):
            rs = slice(c * GLA_CHUNK, (c + 1) * GLA_CHUNK)
            t = _chunk_terms(q_ref, k_ref, g_ref, rs, reverse)
            v = v_ref[rs, :].astype(BF16)
            a = jnp.where(t["allowed"], _dot_nt(t["qe"].astype(BF16), t["ke"].astype(BF16)), 0.0)
            st = state[...]
            st_ref[0, c] = st
            o_ref[rs, :] = _dot(a.astype(BF16), v) + _dot_nt(t["qin"].astype(BF16), st.astype(BF16))
            state[...] = st * t["dec"] + _dot_tn(v, t["kst"].astype(BF16))

    return pl.pallas_call(
        body, name=name, grid=(GLA_HEADS, nsteps),
        in_specs=[qspec, kspec, vspec, gspec],
        out_specs=[ospec, sspec],
        out_shape=[jax.ShapeDtypeStruct((s, GLA_VW), F32),
                   jax.ShapeDtypeStruct((GLA_HEADS, s // GLA_CHUNK, GLA_DV, GLA_DK), F32)],
        scratch_shapes=[pltpu.VMEM((GLA_DV, GLA_DK), F32)],
        compiler_params=_params(("parallel", "arbitrary")),
    )(proj, proj, proj, g)


def _gla_bwd(proj, g, do, states, reverse, name):
    s = proj.shape[0]
    cb, rows, nsteps, qspec, kspec, vspec, gspec, ospec, sspec = _gla_blockspecs(s, not reverse)

    def body(q_ref, k_ref, v_ref, g_ref, do_ref, sp_ref, dq_ref, dk_ref, dv_ref, dg_ref, dstate):
        @pl.when(pl.program_id(1) == 0)
        def _():
            dstate[...] = jnp.zeros_like(dstate)

        for c in (range(cb) if reverse else reversed(range(cb))):
            rs = slice(c * GLA_CHUNK, (c + 1) * GLA_CHUNK)
            t = _chunk_terms(q_ref, k_ref, g_ref, rs, reverse)
            v = v_ref[rs, :].astype(BF16)
            do = do_ref[rs, :]
            qe_b, ke_b = t["qe"].astype(BF16), t["ke"].astype(BF16)
            qin_b, kst_b = t["qin"].astype(BF16), t["kst"].astype(BF16)
            a = jnp.where(t["allowed"], _dot_nt(qe_b, ke_b), 0.0)
            da = jnp.where(t["allowed"], _dot_nt(do, v), 0.0).astype(BF16)
            dqe = _dot(da, ke_b)
            dke = _dot_tn(da, qe_b)
            sp = sp_ref[0, c]
            ds = dstate[...]
            ds_b = ds.astype(BF16)
            dqin = _dot(do, sp.astype(BF16))
            dkst = _dot(v, ds_b)
            dv_ref[rs, :] = _dot_tn(a.astype(BF16), do) + _dot_nt(kst_b, ds_b)
            ddec = jnp.sum(sp * ds, axis=0, keepdims=True)
            dstate[...] = ds * t["dec"] + _dot_tn(do, qin_b)
            dq_ref[rs, :] = (dqe * t["e_q"] + dqin * t["e_in"]) * (GLA_DK ** -0.5)
            dk_ref[rs, :] = dke * t["e_k"] + dkst * t["e_st"]
            kk = dkst * t["kst"]
            db = dqe * t["qe"] - dke * t["ke"] + dqin * t["qin"] - kk
            extra = jnp.sum(kk, axis=0, keepdims=True) + ddec * t["dec"]
            rowi = lax.broadcasted_iota(jnp.int32, (GLA_CHUNK, GLA_DK), 0)
            db = db + jnp.where(rowi == t["last"], extra, 0.0)
            dg_ref[rs, :] = _dot_exact(jnp.where(t["seen_by"], 1.0, 0.0).astype(BF16), db)

    return pl.pallas_call(
        body, name=name, grid=(GLA_HEADS, nsteps),
        in_specs=[qspec, kspec, vspec, gspec, ospec, sspec],
        out_specs=[gspec, gspec, ospec, gspec],
        out_shape=[jax.ShapeDtypeStruct((s, GLA_KW), F32), jax.ShapeDtypeStruct((s, GLA_KW), F32),
                   jax.ShapeDtypeStruct((s, GLA_VW), F32), jax.ShapeDtypeStruct((s, GLA_KW), F32)],
        scratch_shapes=[pltpu.VMEM((GLA_DV, GLA_DK), F32)],
        compiler_params=_params(("parallel", "arbitrary")),
    )(proj, proj, proj, g, do, states)


def _gla_post(o_f, o_b, proj, g, name="gla_post"):
    s = o_f.shape[0]

    def body(of_ref, ob_ref, gr_ref, g_ref, o_ref):
        gv = g_ref[...]
        for h in range(GLA_HEADS):
            sl = slice(h * GLA_DV, (h + 1) * GLA_DV)
            osum = of_ref[:, sl] + ob_ref[:, sl]
            r = lax.rsqrt(jnp.mean(osum * osum, axis=-1, keepdims=True) + EPS)
            gr = gr_ref[:, sl]
            o_ref[:, sl] = (osum * r * gv * (gr * _sigmoid(gr))).astype(BF16)

    blk = pl.BlockSpec((ROW_BLOCK, GLA_VW), lambda i: (i, 0))
    return pl.pallas_call(
        body, name=name, grid=(s // ROW_BLOCK,),
        in_specs=[blk, blk, pl.BlockSpec((ROW_BLOCK, GLA_VW), lambda i: (i, OFF_GR // GLA_VW)),
                  pl.BlockSpec((1, GLA_DV), lambda i: (0, 0))],
        out_specs=blk,
        out_shape=jax.ShapeDtypeStruct((s, GLA_VW), BF16),
        compiler_params=_params(("parallel",)),
    )(o_f, o_b, proj, g)


def _gla_post_bwd(dcat, o_f, o_b, proj, g, name="gla_post_bwd"):
    s = o_f.shape[0]

    def body(dy_ref, of_ref, ob_ref, gr_ref, g_ref, do_ref, dgr_ref, gg_ref):
        i = pl.program_id(0)
        gv = g_ref[...]
        gg = jnp.zeros((1, GLA_DV), F32)
        for h in range(GLA_HEADS):
            sl = slice(h * GLA_DV, (h + 1) * GLA_DV)
            osum = of_ref[:, sl] + ob_ref[:, sl]
            r = lax.rsqrt(jnp.mean(osum * osum, axis=-1, keepdims=True) + EPS)
            gr, dy = gr_ref[:, sl], dy_ref[:, sl]
            sg = _sigmoid(gr)
            dgr_ref[:, sl] = (dy * (osum * r * gv) * (sg * (1.0 + gr * (1.0 - sg)))).astype(BF16)
            dn = dy * (gr * sg)
            dng = dn * gv
            c = jnp.mean(dng * osum, axis=-1, keepdims=True)
            do_ref[:, sl] = (r * dng - osum * (r * r * r * c)).astype(BF16)
            gg = gg + jnp.sum(dn * osum * r, axis=0, keepdims=True)

        @pl.when(i == 0)
        def _():
            gg_ref[...] = jnp.zeros_like(gg_ref)

        gg_ref[...] += gg

    blk = pl.BlockSpec((ROW_BLOCK, GLA_VW), lambda i: (i, 0))
    vec = pl.BlockSpec((1, GLA_DV), lambda i: (0, 0))
    return pl.pallas_call(
        body, name=name, grid=(s // ROW_BLOCK,),
        in_specs=[pl.BlockSpec((ROW_BLOCK, GLA_VW), lambda i: (i, 1)), blk, blk,
                  pl.BlockSpec((ROW_BLOCK, GLA_VW), lambda i: (i, OFF_GR // GLA_VW)), vec],
        out_specs=[blk, blk, vec],
        out_shape=[jax.ShapeDtypeStruct((s, GLA_VW), BF16), jax.ShapeDtypeStruct((s, GLA_VW), BF16),
                   jax.ShapeDtypeStruct((1, GLA_DV), F32)],
        compiler_params=_params(("arbitrary",)),
    )(dcat, o_f, o_b, proj, g)


def _assemble_dproj(dpa, dq_f, dq_b, dk_f, dk_b, dv_f, dv_b, dgr, dz, name="assemble_dproj"):
    s = dpa.shape[0]

    def body(dpa_ref, dqf, dqb, dkf, dkb, dvf, dvb, dgr_ref, dz_ref, o_ref):
        o_ref[:, 0:OFF_GQ] = dpa_ref[...]
        o_ref[:, OFF_GQ:OFF_GK] = (dqf[...] + dqb[...]).astype(BF16)
        o_ref[:, OFF_GK:OFF_GV] = (dkf[...] + dkb[...]).astype(BF16)
        o_ref[:, OFF_GV:OFF_GR] = (dvf[...] + dvb[...]).astype(BF16)
        o_ref[:, OFF_GR:OFF_Z] = dgr_ref[...]
        o_ref[:, OFF_Z:IN_PAD] = dz_ref[...]

    def blk(w):
        return pl.BlockSpec((ROW_BLOCK, w), lambda i: (i, 0))

    return pl.pallas_call(
        body, name=name, grid=(s // ROW_BLOCK,),
        in_specs=[blk(3 * ATTN_W)] + [blk(GLA_KW)] * 4 + [blk(GLA_VW)] * 3 + [blk(LANES)],
        out_specs=blk(IN_PAD),
        out_shape=jax.ShapeDtypeStruct((s, IN_PAD), BF16),
        compiler_params=_params(("parallel",)),
    )(dpa, dq_f, dq_b, dk_f, dk_b, dv_f, dv_b, dgr, dz)


CONV_ROWS = 256
CONV_COLS = 1408
HALO = SUBLANES


def _halo_specs(s, tr, tc, col_of):
    per = tr // HALO
    last = s // HALO - 1
    cur = pl.BlockSpec((tr, tc), lambda c, i: (i, col_of(c)))
    prev = pl.BlockSpec((HALO, tc), lambda c, i: (jnp.maximum(i * per - 1, 0), col_of(c)))
    nxt = pl.BlockSpec((HALO, tc), lambda c, i: (jnp.minimum((i + 1) * per, last), col_of(c)))
    return prev, cur, nxt


def _extended(prev_ref, cur_ref, next_ref, i, s, tr):
    x = jnp.concatenate([prev_ref[...], cur_ref[...], next_ref[...]], axis=0)
    idx = i * tr - HALO + lax.broadcasted_iota(jnp.int32, x.shape, 0)
    return jnp.where((idx >= 0) & (idx < s), x, 0.0)


def _conv_glu(gate, up, conv_w, conv_b, name="conv_glu"):
    s, f = gate.shape
    tr, tc = CONV_ROWS, CONV_COLS
    ext = tr + 2 * HALO

    def body(gp, gc, gn, up_ref, w_ref, b_ref, o_ref):
        i = pl.program_id(1)
        ge = _extended(gp, gc, gn, i, s, tr)
        w = w_ref[...]
        conv = (w[0:1] * pltpu.roll(ge, 1, 0) + w[1:2] * ge + w[2:3] * pltpu.roll(ge, ext - 1, 0))[HALO:HALO + tr]
        conv = conv + b_ref[...]
        o_ref[...] = (conv * _sigmoid(conv) * up_ref[...]).astype(BF16)

    prev, cur, nxt = _halo_specs(s, tr, tc, lambda c: c)
    return pl.pallas_call(
        body, name=name, grid=(f // tc, s // tr),
        in_specs=[prev, cur, nxt, cur, pl.BlockSpec((3, tc), lambda c, i: (0, c)), pl.BlockSpec((1, tc), lambda c, i: (0, c))],
        out_specs=cur,
        out_shape=jax.ShapeDtypeStruct((s, f), BF16),
        compiler_params=_params(("parallel", "parallel")),
    )(gate, gate, gate, up, conv_w, conv_b)


def _conv_glu_bwd(dact, gate, up, conv_w, conv_b, name="conv_glu_bwd"):
    s, f = gate.shape
    tr, tc = CONV_ROWS, CONV_COLS
    ext = tr + 2 * HALO

    def body(dp, dc, dn, gp, gc, gn, upp, upc, upn, w_ref, b_ref, dg_ref, du_ref, gw_ref, gb_ref):
        i = pl.program_id(1)
        ge = _extended(gp, gc, gn, i, s, tr)
        ue = _extended(upp, upc, upn, i, s, tr)
        de = _extended(dp, dc, dn, i, s, tr)
        w = w_ref[...]
        g_prev, g_next = pltpu.roll(ge, 1, 0), pltpu.roll(ge, ext - 1, 0)
        conv = w[0:1] * g_prev + w[1:2] * ge + w[2:3] * g_next + b_ref[...]
        sg = _sigmoid(conv)
        du_ref[...] = (de * (conv * sg))[HALO:HALO + tr].astype(BF16)
        dconv = de * ue * (sg * (1.0 + conv * (1.0 - sg)))
        dgate = w[0:1] * pltpu.roll(dconv, ext - 1, 0) + w[1:2] * dconv + w[2:3] * pltpu.roll(dconv, 1, 0)
        dg_ref[...] = dgate[HALO:HALO + tr].astype(BF16)
        inner = slice(HALO, HALO + tr)
        dci = dconv[inner]

        @pl.when(i == 0)
        def _():
            gw_ref[...] = jnp.zeros_like(gw_ref)
            gb_ref[...] = jnp.zeros_like(gb_ref)

        gw_ref[0:1, :] += jnp.sum(dci * g_prev[inner], axis=0, keepdims=True)
        gw_ref[1:2, :] += jnp.sum(dci * ge[inner], axis=0, keepdims=True)
        gw_ref[2:3, :] += jnp.sum(dci * g_next[inner], axis=0, keepdims=True)
        gb_ref[...] += jnp.sum(dci, axis=0, keepdims=True)

    prev, cur, nxt = _halo_specs(s, tr, tc, lambda c: c)
    wspec = pl.BlockSpec((3, tc), lambda c, i: (0, c))
    bspec = pl.BlockSpec((1, tc), lambda c, i: (0, c))
    return pl.pallas_call(
        body, name=name, grid=(f // tc, s // tr),
        in_specs=[prev, cur, nxt] * 3 + [wspec, bspec],
        out_specs=[cur, cur, wspec, bspec],
        out_shape=[jax.ShapeDtypeStruct((s, f), BF16), jax.ShapeDtypeStruct((s, f), BF16),
                   jax.ShapeDtypeStruct((3, f), F32), jax.ShapeDtypeStruct((1, f), F32)],
        compiler_params=_params(("parallel", "arbitrary")),
    )(dact, dact, dact, gate, gate, gate, up, up, up, conv_w, conv_b)


def _local_step(x, target, w, late_weights=None, grad_sink=None, first_dep=()):
    s = x.shape[0]
    tables = _rope_tables(s)
    uf, ub = _gate_matrices(w["gf_up"], w["gb_up"])
    if grad_sink is None:
        grad_sink = lambda names, grads: ()

    n1 = _rms_fwd(x, w["norm1_g"], "norm1")
    proj = _matmul([(n1, w["w_in"])], "nn", F32, 1024, 896, D_MODEL, "in_proj", deps=first_dep)
    q, k, v = _rope_fwd(proj, tables)
    branches = [_attn_fwd(q, k, v, d, f"attn_fwd_d{d}") for d in DILATIONS]
    o_mix, lse, ao = _attn_combine([b[0] for b in branches], [b[1] for b in branches], w["attn_norm_g"])
    g_f, g_b = _gla_gates(proj, uf, ub, w["gf_b"], w["gb_b"])
    o_f, st_f = _gla_fwd(proj, g_f, False, "gla_fwd_f")
    o_b, st_b = _gla_fwd(proj, g_b, True, "gla_fwd_b")
    go = _gla_post(o_f, o_b, proj, w["gla_norm_g"])
    cat = jnp.concatenate([ao, go], axis=1)
    if late_weights is not None:
        w = {**w, **late_weights(cat)}
    h1 = _matmul([(cat, w["w_out"])], "nn", F32, 512, 1024, D_MODEL, "out_proj", res=x)
    n2 = _rms_fwd(h1, w["norm2_g"], "norm2")
    gate = _matmul([(n2, w["w_gate"])], "nn", F32, 512, 1408, D_MODEL, "ffn_gate")
    up = _matmul([(n2, w["w_up"])], "nn", F32, 512, 1408, D_MODEL, "ffn_up")
    act = _conv_glu(gate, up, w["conv_w"], w["conv_b"])
    h2 = _matmul([(act, w["w_down"])], "nn", F32, 512, 1024, 2816, "ffn_down", res=h1)
    dh2, dh2_b, loss_acc, g_final = _final_loss(h2, target, w["final_norm_g"])

    dact = _matmul([(dh2_b, w["w_down"])], "nt", F32, 512, 1408, D_MODEL, "d_act")
    g_w_down = _matmul([(act, dh2_b)], "tn", F32, 1408, 1024, 2048, "g_w_down")
    dep = grad_sink(["w_down"], [g_w_down])
    dgate, dup, g_conv_w, g_conv_b = _conv_glu_bwd(dact, gate, up, w["conv_w"], w["conv_b"])
    g_w_gate = _matmul([(n2, dgate)], "tn", F32, 1024, 1408, 2048, "g_w_gate", deps=dep)
    g_w_up = _matmul([(n2, dup)], "tn", F32, 1024, 1408, 2048, "g_w_up")
    dep = grad_sink(["w_gate", "w_up"], [g_w_gate, g_w_up])
    dn2 = _matmul([(dgate, w["w_gate"]), (dup, w["w_up"])], "nt", F32, 512, 1024, 2816, "d_n2", deps=dep)
    dh1, dh1_b, g_norm2 = _rms_bwd(dn2, h1, w["norm2_g"], dh2, "norm2_bwd")

    g_w_out = _matmul([(cat, dh1_b)], "tn", F32, 1024, 1024, 2048, "g_w_out")
    dep = grad_sink(["w_out"], [g_w_out])
    dcat = _matmul([(dh1_b, w["w_out"])], "nt", F32, 512, 1024, D_MODEL, "d_cat", deps=dep)
    do_attn, delta, g_attn_norm = _attn_prebwd(dcat, o_mix, w["attn_norm_g"])
    grads = [_attn_bwd(q, k, v, do_attn, lse, delta, d, f"attn_bwd_d{d}") for d in DILATIONS]
    dpa = _rope_bwd([g[0] for g in grads], [g[1] for g in grads], [g[2] for g in grads], tables)
    do_gla, dgr, g_gla_norm = _gla_post_bwd(dcat, o_f, o_b, proj, w["gla_norm_g"])
    dq_f, dk_f, dv_f, dg_f = _gla_bwd(proj, g_f, do_gla, st_f, False, "gla_bwd_f")
    dq_b, dk_b, dv_b, dg_b = _gla_bwd(proj, g_b, do_gla, st_b, True, "gla_bwd_b")
    dz, g_uf, g_ub, g_gf_b, g_gb_b = _gla_gates_bwd(dg_f, dg_b, proj, uf, ub, w["gf_b"], w["gb_b"])
    dproj = _assemble_dproj(dpa, dq_f, dq_b, dk_f, dk_b, dv_f, dv_b, dgr, dz)
    g_w_in = _matmul([(n1, dproj)], "tn", F32, 1024, 896, 2048, "g_w_in")
    dep = grad_sink(["w_in"], [g_w_in])
    dn1 = _matmul([(dproj, w["w_in"])], "nt", F32, 512, 1024, IN_PAD, "d_n1", deps=dep)
    grad_x, _, g_norm1 = _rms_bwd(dn1, x, w["norm1_g"], dh1, "norm1_bwd")

    g = dict(norm1_g=g_norm1, w_in=g_w_in, gf_up=g_uf[:GLA_RANK], gf_b=g_gf_b,
             gb_up=g_ub[GLA_RANK:2 * GLA_RANK], gb_b=g_gb_b, gla_norm_g=g_gla_norm, attn_norm_g=g_attn_norm,
             w_out=g_w_out, norm2_g=g_norm2, w_gate=g_w_gate, w_up=g_w_up, conv_w=g_conv_w, conv_b=g_conv_b,
             w_down=g_w_down, final_norm_g=g_final)
    return loss_acc, grad_x, g


def _me_and_peers():
    x, y, c = lax.axis_index("x"), lax.axis_index("y"), lax.axis_index("c")
    me = 4 * x + 2 * y + c
    peers = []
    for kbits in range(1, N_DEV):
        px, py, pc = x ^ (kbits >> 2 & 1), y ^ (kbits >> 1 & 1), c ^ (kbits & 1)
        peers.append(((px, py, pc), 4 * px + 2 * py + pc))
    return me, peers


_HBM = pl.BlockSpec(memory_space=pltpu.HBM)
_SEM = pl.BlockSpec(memory_space=pltpu.SEMAPHORE)
_ANY = pl.BlockSpec(memory_space=pl.ANY)
_EFFECT = pltpu.SideEffectType.DATAFLOW_SIDE_EFFECTING


def _exchange_copies(src_refs, land_refs, send_sems, recv_sems, scatter):
    me, peers = _me_and_peers()
    out = []
    for a, (src, land) in enumerate(zip(src_refs, land_refs)):
        for kk, (dev, idx) in enumerate(peers):
            out.append(pltpu.make_async_remote_copy(
                src_ref=src.at[idx] if scatter else src, dst_ref=land.at[me],
                send_sem=send_sems.at[a * (N_DEV - 1) + kk], recv_sem=recv_sems.at[a * (N_DEV - 1) + kk],
                device_id=dev, device_id_type=MESH_ID))
    return out


def _exchange_start(srcs, lands, scatter, name, deps=()):
    n, nd = len(srcs), len(deps)

    def body(*refs):
        src_refs, land_refs = refs[:n], refs[n:2 * n]
        send_sems, recv_sems = refs[2 * n + nd:2 * n + nd + 2]
        token = refs[-1]
        for cp in _exchange_copies(src_refs, land_refs, send_sems, recv_sems, scatter):
            cp.start()
        token[...] = jnp.zeros_like(token)

    outs = pl.pallas_call(
        body, name=name,
        in_specs=[_HBM] * (2 * n) + [_ANY] * nd,
        out_specs=[_SEM, _SEM] + [_HBM] * (2 * n) + [pl.BlockSpec(memory_space=pltpu.VMEM)],
        out_shape=[pltpu.SemaphoreType.DMA((n * (N_DEV - 1),)), pltpu.SemaphoreType.DMA((n * (N_DEV - 1),))]
        + [pltpu.HBM(t.shape, t.dtype) for t in srcs] + [pltpu.HBM(t.shape, t.dtype) for t in lands]
        + [jax.ShapeDtypeStruct((SUBLANES, LANES), F32)],
        input_output_aliases={i: 2 + i for i in range(2 * n)},
        compiler_params=pltpu.CompilerParams(has_side_effects=_EFFECT),
    )(*[pltpu.with_memory_space_constraint(t, pltpu.HBM) for t in list(srcs) + list(lands)], *deps)
    send_sems, recv_sems = outs[0], outs[1]
    return dict(send=send_sems, recv=recv_sems, srcs=outs[2:2 + n], lands=outs[2 + n:2 + 2 * n],
                scatter=scatter, token=outs[-1])


def _exchange_wait(started, name, after):
    n = len(started["srcs"])
    scatter = started["scatter"]

    def body(*refs):
        src_refs, land_refs = refs[:n], refs[n:2 * n]
        send_sems, recv_sems = refs[2 * n], refs[2 * n + 1]
        for cp in _exchange_copies(src_refs, land_refs, send_sems, recv_sems, scatter):
            cp.wait_send()
            cp.wait_recv()

    outs = pl.pallas_call(
        body, name=name,
        in_specs=[_HBM] * (2 * n) + [_SEM, _SEM, _ANY],
        out_specs=[_HBM] * (2 * n),
        out_shape=[pltpu.HBM(t.shape, t.dtype) for t in started["srcs"]]
        + [pltpu.HBM(t.shape, t.dtype) for t in started["lands"]],
        input_output_aliases={i: i for i in range(2 * n)},
        compiler_params=pltpu.CompilerParams(has_side_effects=_EFFECT),
    )(*started["srcs"], *started["lands"], started["send"], started["recv"], after)
    return outs[:n], outs[n:]


def _all_gather_vmem(vec, name):
    r = vec.shape[0]

    def body(v_ref, o_ref, send_sems, recv_sems):
        me, peers = _me_and_peers()
        o_ref[me] = v_ref[...]
        sends = []
        for kk, (dev, _) in enumerate(peers):
            cp = pltpu.make_async_remote_copy(
                src_ref=v_ref, dst_ref=o_ref.at[me],
                send_sem=send_sems.at[kk], recv_sem=recv_sems.at[kk],
                device_id=dev, device_id_type=MESH_ID)
            cp.start()
            sends.append(cp)
        for kk, (dev, idx) in enumerate(peers):
            pltpu.make_async_remote_copy(
                src_ref=v_ref, dst_ref=o_ref.at[idx],
                send_sem=send_sems.at[kk], recv_sem=recv_sems.at[kk],
                device_id=dev, device_id_type=MESH_ID).wait_recv()
        for cp in sends:
            cp.wait_send()

    return pl.pallas_call(
        body, name=name,
        in_specs=[pl.BlockSpec(memory_space=pltpu.VMEM)],
        out_specs=pl.BlockSpec(memory_space=pltpu.VMEM),
        out_shape=jax.ShapeDtypeStruct((N_DEV, r, LANES), F32),
        scratch_shapes=[pltpu.SemaphoreType.DMA((N_DEV - 1,)), pltpu.SemaphoreType.DMA((N_DEV - 1,))],
        compiler_params=pltpu.CompilerParams(vmem_limit_bytes=VMEM_LIMIT),
    )(vec)


def _adamw_math(w, g, m, v):
    m = ADAM_B1 * m + (1.0 - ADAM_B1) * g
    v = ADAM_B2 * v + (1.0 - ADAM_B2) * (g * g)
    m_hat = m / (1.0 - ADAM_B1 ** ADAM_STEP)
    v_hat = v / (1.0 - ADAM_B2 ** ADAM_STEP)
    delta = -ADAM_LR * (m_hat / (jnp.sqrt(v_hat) + ADAM_EPS) + ADAM_WD * w)
    return delta, m, v


def _adamw_sum(parts, w, m, v, tr, name, own=None, me=None):
    r, c = w.shape

    def body(*refs):
        if own is None:
            p_ref, w_ref, m_ref, v_ref, g_ref, d_ref, nm_ref, nv_ref = refs
            terms = [p_ref[kk] for kk in range(N_DEV)]
        else:
            me_ref, p_ref, own_ref, w_ref, m_ref, v_ref, g_ref, d_ref, nm_ref, nv_ref = refs
            terms = [jnp.where(me_ref[0] == kk, own_ref[0], p_ref[kk]) for kk in range(N_DEV)]
        g = terms[0]
        for t in terms[1:]:
            g = g + t
        g_ref[...] = g
        d_ref[...], nm_ref[...], nv_ref[...] = _adamw_math(w_ref[...], g, m_ref[...], v_ref[...])

    out_shape = [jax.ShapeDtypeStruct((r, c), F32)] * 4
    if own is None:
        blk = pl.BlockSpec((tr, c), lambda i: (i, 0))
        return pl.pallas_call(
            body, name=name, grid=(r // tr,),
            in_specs=[pl.BlockSpec((N_DEV, tr, c), lambda i: (0, i, 0)), blk, blk, blk],
            out_specs=[blk] * 4, out_shape=out_shape,
            compiler_params=_params(("parallel",)),
        )(parts, w, m, v)
    blk = pl.BlockSpec((tr, c), lambda i, me_ref: (i, 0))
    return pl.pallas_call(
        body, name=name,
        grid_spec=pltpu.PrefetchScalarGridSpec(
            num_scalar_prefetch=1, grid=(r // tr,),
            in_specs=[pl.BlockSpec((N_DEV, tr, c), lambda i, me_ref: (0, i, 0)),
                      pl.BlockSpec((1, tr, c), lambda i, me_ref: (me_ref[0], i, 0)), blk, blk, blk],
            out_specs=[blk] * 4),
        out_shape=out_shape,
        compiler_params=_params(("parallel",)),
    )(jnp.reshape(me, (1,)).astype(jnp.int32), parts, own, w, m, v)


_SMALL = ("norm1_g", "gf_b", "gb_b", "gla_norm_g", "attn_norm_g", "norm2_g", "conv_b", "final_norm_g",
          "gf_up", "gb_up", "conv_w")


def _pack(named):
    flat = jnp.concatenate([jnp.ravel(t).astype(F32) for t in named])
    tile = SUBLANES * LANES
    total = -(-flat.shape[0] // tile) * tile
    return jnp.pad(flat, (0, total - flat.shape[0])).reshape(total // LANES, LANES)


def _unpack(packed, shapes):
    flat = packed.reshape(-1)
    out, off = [], 0
    for shp in shapes:
        size = int(np.prod(shp))
        out.append(flat[off:off + size].reshape(shp))
        off += size
    return out


def kernel(x, norm1_g, w_in, gf_up, gf_b, gb_up, gb_b, gla_norm_g, attn_norm_g, w_out, norm2_g, w_gate, w_up, conv_w, conv_b, w_down, final_norm_g, loss_target, m_norm1_g, m_w_in, m_gf_up, m_gf_b, m_gb_up, m_gb_b, m_gla_norm_g, m_attn_norm_g, m_w_out, m_norm2_g, m_w_gate, m_w_up, m_conv_w, m_conv_b, m_w_down, m_final_norm_g, v_norm1_g, v_w_in, v_gf_up, v_gf_b, v_gb_up, v_gb_b, v_gla_norm_g, v_attn_norm_g, v_w_out, v_norm2_g, v_w_gate, v_w_up, v_conv_w, v_conv_b, v_w_down, v_final_norm_g):
    names = ("norm1_g", "w_in", "gf_up", "gf_b", "gb_up", "gb_b", "gla_norm_g", "attn_norm_g", "w_out", "norm2_g",
             "w_gate", "w_up", "conv_w", "conv_b", "w_down", "final_norm_g")
    ws = dict(zip(names, (norm1_g, w_in, gf_up, gf_b, gb_up, gb_b, gla_norm_g, attn_norm_g, w_out, norm2_g,
                          w_gate, w_up, conv_w, conv_b, w_down, final_norm_g)))
    ms = dict(zip(names, (m_norm1_g, m_w_in, m_gf_up, m_gf_b, m_gb_up, m_gb_b, m_gla_norm_g, m_attn_norm_g, m_w_out,
                          m_norm2_g, m_w_gate, m_w_up, m_conv_w, m_conv_b, m_w_down, m_final_norm_g)))
    vs = dict(zip(names, (v_norm1_g, v_w_in, v_gf_up, v_gf_b, v_gb_up, v_gb_b, v_gla_norm_g, v_attn_norm_g, v_w_out,
                          v_norm2_g, v_w_gate, v_w_up, v_conv_w, v_conv_b, v_w_down, v_final_norm_g)))
    me = 4 * lax.axis_index("x") + 2 * lax.axis_index("y") + lax.axis_index("c")
    big = ("w_in", "w_out", "w_gate", "w_up", "w_down")
    col_sharded = ("w_in", "w_gate", "w_up")

    def gather_start(group, name, deps=()):
        shards = [ws[n][0].astype(BF16) for n in group]
        lands = [lax.empty((N_DEV,) + t.shape, BF16) for t in shards]
        return _exchange_start(shards, lands, False, name, deps)

    def gather_finish(group, started, name, after):
        full = {}
        for n, own, t in zip(group, *_exchange_wait(started, name, after)):
            t = lax.dynamic_update_slice(t, own[None], (me, 0, 0))
            if n in col_sharded:
                full[n] = jnp.transpose(t, (1, 0, 2)).reshape(t.shape[1], N_DEV * t.shape[2])
            else:
                full[n] = t.reshape(N_DEV * t.shape[1], t.shape[2])
        return full

    started_a = gather_start(("w_in",), "gather_w_in_start")
    full = gather_finish(("w_in",), started_a, "gather_w_in_wait", started_a["token"])
    full["w_in"] = jnp.pad(full["w_in"], ((0, 0), (0, IN_PAD - IN_WIDTH)))
    late = ("w_out", "w_gate", "w_up", "w_down")
    started_b = gather_start(late, "gather_late_start", deps=(full["w_in"],))

    def late_weights(after):
        return gather_finish(late, started_b, "gather_late_wait", after)

    small_sharded = ("gf_up", "gb_up", "conv_w")
    sm = _all_gather_vmem(_pack([ws[n][0] for n in small_sharded]), "gather_small")
    shard_shapes = [ws[n][0].shape for n in small_sharded]
    per_dev = [_unpack(sm[d], shard_shapes) for d in range(N_DEV)]
    for i, n in enumerate(small_sharded):
        full[n] = jnp.concatenate([per_dev[d][i] for d in range(N_DEV)], axis=1)
    for n in ("norm1_g", "gf_b", "gb_b", "gla_norm_g", "attn_norm_g", "norm2_g", "conv_b"):
        full[n] = ws[n]
    full["final_norm_g"] = final_norm_g.reshape(1, D_MODEL)

    in_flight = []

    def grad_sink(group, grads):
        partials = []
        for n, t in zip(group, grads):
            if n == "w_in":
                t = t[:, :IN_WIDTH]
            if n in col_sharded:
                t = jnp.transpose(t.reshape(t.shape[0], N_DEV, t.shape[1] // N_DEV), (1, 0, 2))
            else:
                t = t.reshape(N_DEV, t.shape[0] // N_DEV, t.shape[1])
            partials.append(t)
        lands = [lax.empty(t.shape, F32) for t in partials]
        started = _exchange_start(partials, lands, True, "exchange_" + "_".join(group) + "_start")
        in_flight.append((group, started))
        return (started["token"],)

    loss_acc, grad_x, g = _local_step(x[0], loss_target[0], full, late_weights, grad_sink,
                                      first_dep=(started_b["token"],))

    out = {}
    for group, started in in_flight:
        sent, landed = _exchange_wait(started, "exchange_" + "_".join(group) + "_wait", grad_x)
        for n, parts, own in zip(group, landed, sent):
            out[n] = _adamw_sum(parts, ws[n][0], ms[n][0], vs[n][0], 64, "adamw_" + n, own=own, me=me)

    small_full_shapes = [g[n].shape for n in _SMALL]
    gsmall = _pack([g[n] for n in _SMALL] + [loss_acc[0:1, 0:1]])
    gathered_small = _all_gather_vmem(gsmall, "gather_small_grads")

    def full_small(d):
        parts = []
        for n in _SMALL:
            t = d[n].reshape(d[n].shape[-2:]) if d[n].ndim == 3 else d[n].reshape(1, -1)
            if n in small_sharded:
                wide = jnp.zeros((t.shape[0], t.shape[1] * N_DEV), F32)
                t = lax.dynamic_update_slice_in_dim(wide, t, me * t.shape[1], axis=1)
            parts.append(t)
        return _pack(parts + [jnp.zeros((1, 1), F32)])

    rows = gsmall.shape[0]
    res_small = _adamw_sum(gathered_small, full_small(ws), full_small(ms), full_small(vs), rows, "adamw_small")
    loss = res_small[0].reshape(-1)[sum(int(np.prod(sh)) for sh in small_full_shapes)]
    unpacked = [_unpack(t, small_full_shapes) for t in res_small]
    for i, n in enumerate(_SMALL):
        vals = [u[i] for u in unpacked]
        if n in small_sharded:
            width = vals[0].shape[1] // N_DEV
            vals = [lax.dynamic_slice_in_dim(t, me * width, width, axis=1) for t in vals]
        out[n] = vals

    result = [loss, grad_x[None]]
    for kind in range(4):
        for n in names:
            result.append(out[n][kind].reshape(ws[n].shape))
    return tuple(result)
```

```python
import functools

import numpy as np
import jax
import jax.numpy as jnp
from jax import lax
from jax.experimental import pallas as pl
from jax.experimental.pallas import tpu as pltpu

F32 = jnp.float32
BF16 = jnp.bfloat16

D_MODEL = 2048
ATTN_W = 1024
ATTN_HEADS = 8
HEAD_DIM = 128
ROPE_DIM = 32
ROPE_THETA = 500000.0
DILATIONS = (1, 4, 16)
N_SIDE = 64
GLA_KW = 512
GLA_VW = 1024
GLA_HEADS = 4
GLA_DK = 128
GLA_DV = 256
GLA_RANK = 16
GLA_GATE_NORM = 16.0
GLA_CHUNK = 64
IN_WIDTH = 6176
IN_PAD = 6272
D_FF = 5632
EPS = 1e-6
N_DEV = 8

OFF_AQ, OFF_AK, OFF_AV = 0, 1024, 2048
OFF_GQ, OFF_GK, OFF_GV, OFF_GR, OFF_Z = 3072, 3584, 4096, 5120, 6144

ADAM_LR, ADAM_B1, ADAM_B2, ADAM_EPS, ADAM_WD, ADAM_STEP = 0.001, 0.9, 0.999, 1e-08, 0.01, 10

LANES = 128
SUBLANES = 8
VMEM_LIMIT = 56 * 1024 * 1024
ROW_BLOCK = 256
ATTN_BLOCK = 128
GLA_CHUNKS_PER_STEP = 4
NEG = -1e30
MESH_ID = pl.DeviceIdType.MESH


def _params(sem):
    return pltpu.CompilerParams(dimension_semantics=sem, vmem_limit_bytes=VMEM_LIMIT)


def _dot(a, b):
    return lax.dot_general(a, b, (((1,), (0,)), ((), ())), preferred_element_type=F32)


def _dot_nt(a, b):
    return lax.dot_general(a, b, (((1,), (1,)), ((), ())), preferred_element_type=F32)


def _dot_tn(a, b):
    return lax.dot_general(a, b, (((0,), (0,)), ((), ())), preferred_element_type=F32)


def _sigmoid(x):
    return 1.0 / (1.0 + jnp.exp(-x))


def _matmul(pairs, mode, out_dtype, tm, tn, tk, name, res=None, deps=()):
    a0, b0 = pairs[0]
    if mode == "nn":
        (m, kdim), n = a0.shape, b0.shape[1]
    elif mode == "nt":
        (m, kdim), n = a0.shape, b0.shape[0]
    else:
        (kdim, m), n = a0.shape, b0.shape[1]
    assert m % tm == 0 and n % tn == 0 and kdim % tk == 0, (name, m, n, kdim)
    nk = kdim // tk
    npairs = len(pairs)
    steps = nk * npairs
    dot = {"nn": _dot, "nt": _dot_nt, "tn": _dot_tn}[mode]

    def kidx(p):
        return lambda k: jnp.clip(k - p * nk, 0, nk - 1)

    in_specs, args = [], []
    for p, (a, b) in enumerate(pairs):
        kk = kidx(p)
        if mode == "nn":
            in_specs += [pl.BlockSpec((tm, tk), lambda i, j, k, kk=kk: (i, kk(k))),
                         pl.BlockSpec((tk, tn), lambda i, j, k, kk=kk: (kk(k), j))]
        elif mode == "nt":
            in_specs += [pl.BlockSpec((tm, tk), lambda i, j, k, kk=kk: (i, kk(k))),
                         pl.BlockSpec((tn, tk), lambda i, j, k, kk=kk: (j, kk(k)))]
        else:
            in_specs += [pl.BlockSpec((tk, tm), lambda i, j, k, kk=kk: (kk(k), i)),
                         pl.BlockSpec((tk, tn), lambda i, j, k, kk=kk: (kk(k), j))]
        args += [a, b]
    if res is not None:
        in_specs.append(pl.BlockSpec((tm, tn), lambda i, j, k: (i, j)))
        args.append(res)
    in_specs += [pl.BlockSpec(memory_space=pl.ANY)] * len(deps)
    args += list(deps)

    def body(*refs):
        ab = refs[:2 * npairs]
        res_ref = refs[2 * npairs] if res is not None else None
        o_ref = refs[2 * npairs + (1 if res is not None else 0) + len(deps)]

        def finish(acc):
            if res_ref is not None:
                acc = acc + res_ref[...]
            o_ref[...] = acc.astype(out_dtype)

        if steps == 1:
            finish(dot(ab[0][...], ab[1][...]))
            return
        acc_ref = refs[-1]
        k = pl.program_id(2)

        @pl.when(k == 0)
        def _():
            acc_ref[...] = jnp.zeros_like(acc_ref)

        for p in range(npairs):
            @pl.when((k >= p * nk) & (k < (p + 1) * nk))
            def _(p=p):
                acc_ref[...] += dot(ab[2 * p][...], ab[2 * p + 1][...])

        @pl.when(k == steps - 1)
        def _():
            finish(acc_ref[...])

    return pl.pallas_call(
        body, name=name,
        grid=(m // tm, n // tn, steps),
        in_specs=in_specs,
        out_specs=pl.BlockSpec((tm, tn), lambda i, j, k: (i, j)),
        out_shape=jax.ShapeDtypeStruct((m, n), out_dtype),
        scratch_shapes=[] if steps == 1 else [pltpu.VMEM((tm, tn), F32)],
        compiler_params=_params(("parallel", "parallel", "arbitrary")),
    )(*args)


def _rms_fwd(x, g, name):
    s, d = x.shape

    def body(x_ref, g_ref, o_ref):
        xv = x_ref[...]
        r = lax.rsqrt(jnp.mean(xv * xv, axis=-1, keepdims=True) + EPS)
        o_ref[...] = (xv * r * g_ref[...]).astype(BF16)

    return pl.pallas_call(
        body, name=name, grid=(s // ROW_BLOCK,),
        in_specs=[pl.BlockSpec((ROW_BLOCK, d), lambda i: (i, 0)), pl.BlockSpec((1, d), lambda i: (0, 0))],
        out_specs=pl.BlockSpec((ROW_BLOCK, d), lambda i: (i, 0)),
        out_shape=jax.ShapeDtypeStruct((s, d), BF16),
        compiler_params=_params(("parallel",)),
    )(x, g)


def _rms_bwd(dn, x, g, dres, name):
    s, d = x.shape

    def body(dn_ref, x_ref, g_ref, dres_ref, dx_ref, dxb_ref, gg_ref):
        i = pl.program_id(0)
        xv, dnv = x_ref[...], dn_ref[...]
        r = lax.rsqrt(jnp.mean(xv * xv, axis=-1, keepdims=True) + EPS)
        dng = dnv * g_ref[...]
        c = jnp.mean(dng * xv, axis=-1, keepdims=True)
        dx = dres_ref[...] + r * dng - xv * (r * r * r * c)
        dx_ref[...] = dx
        dxb_ref[...] = dx.astype(BF16)

        @pl.when(i == 0)
        def _():
            gg_ref[...] = jnp.zeros_like(gg_ref)

        gg_ref[...] += jnp.sum(dnv * xv * r, axis=0, keepdims=True)

    row = pl.BlockSpec((ROW_BLOCK, d), lambda i: (i, 0))
    vec = pl.BlockSpec((1, d), lambda i: (0, 0))
    return pl.pallas_call(
        body, name=name, grid=(s // ROW_BLOCK,),
        in_specs=[row, row, vec, row],
        out_specs=[row, row, vec],
        out_shape=[jax.ShapeDtypeStruct((s, d), F32), jax.ShapeDtypeStruct((s, d), BF16),
                   jax.ShapeDtypeStruct((1, d), F32)],
        compiler_params=_params(("arbitrary",)),
    )(dn, x, g, dres)


def _final_loss(h2, target, g, name="final_loss"):
    s, d = h2.shape

    def body(h_ref, t_ref, g_ref, dh_ref, dhb_ref, loss_ref, gg_ref):
        i = pl.program_id(0)
        hv, gv = h_ref[...], g_ref[...]
        r = lax.rsqrt(jnp.mean(hv * hv, axis=-1, keepdims=True) + EPS)
        e = hv * r * gv - t_ref[...]
        dy = e * (1.0 / d)
        dyg = dy * gv
        c = jnp.mean(dyg * hv, axis=-1, keepdims=True)
        dh = r * dyg - hv * (r * r * r * c)
        dh_ref[...] = dh
        dhb_ref[...] = dh.astype(BF16)

        @pl.when(i == 0)
        def _():
            gg_ref[...] = jnp.zeros_like(gg_ref)
            loss_ref[...] = jnp.zeros_like(loss_ref)

        gg_ref[...] += jnp.sum(dy * hv * r, axis=0, keepdims=True)
        loss_ref[...] += jnp.sum(jnp.sum(e * e, axis=-1, keepdims=True), axis=0, keepdims=True) * (0.5 / d)

    row = pl.BlockSpec((ROW_BLOCK, d), lambda i: (i, 0))
    vec = pl.BlockSpec((1, d), lambda i: (0, 0))
    return pl.pallas_call(
        body, name=name, grid=(s // ROW_BLOCK,),
        in_specs=[row, row, vec],
        out_specs=[row, row, pl.BlockSpec((SUBLANES, LANES), lambda i: (0, 0)), vec],
        out_shape=[jax.ShapeDtypeStruct((s, d), F32), jax.ShapeDtypeStruct((s, d), BF16),
                   jax.ShapeDtypeStruct((SUBLANES, LANES), F32), jax.ShapeDtypeStruct((1, d), F32)],
        compiler_params=_params(("arbitrary",)),
    )(h2, target, g)


def _rope_tables(s):
    pos = jnp.arange(s, dtype=F32)
    inv_freq = ROPE_THETA ** (-jnp.arange(0, ROPE_DIM, 2, dtype=F32) / ROPE_DIM)
    ang = pos[:, None] * inv_freq[None, :]
    cos, sin = jnp.cos(ang), jnp.sin(ang)
    half = ROPE_DIM // 2
    rest = HEAD_DIM - ROPE_DIM
    c = jnp.concatenate([cos, cos, jnp.ones((s, rest), F32)], axis=1)
    sm = jnp.concatenate([-sin, jnp.zeros((s, half + rest), F32)], axis=1)
    sp = jnp.concatenate([jnp.zeros((s, half), F32), sin, jnp.zeros((s, rest), F32)], axis=1)
    return c, sm, sp


def _res_shape(s, groups, dil, dtype):
    return jax.ShapeDtypeStruct((s // dil, dil * groups * LANES), dtype)


def _res_spec(groups, dil):
    return pl.BlockSpec((ROW_BLOCK // dil, dil * groups * LANES), lambda i: (i, 0))


def _to_residues(scr, o_ref, dil):
    groups, rows = scr.shape[0], ROW_BLOCK // dil
    for r in range(dil):
        for h in range(groups):
            piece = scr[h] if dil == 1 else scr.at[h][pl.ds(r, rows, stride=dil), :]
            o_ref[:, (r * groups + h) * LANES:(r * groups + h + 1) * LANES] = piece.astype(o_ref.dtype)


def _from_residues(i_ref, scr, dil):
    groups, rows = scr.shape[0], ROW_BLOCK // dil
    for r in range(dil):
        for h in range(groups):
            piece = i_ref[:, (r * groups + h) * LANES:(r * groups + h + 1) * LANES].astype(F32)
            if dil == 1:
                scr[h] = piece
            else:
                scr.at[h][pl.ds(r, rows, stride=dil), :] = piece


def _rope_fwd(proj, tables, name="rope_fwd"):
    s = proj.shape[0]
    half = ROPE_DIM // 2
    nd = len(DILATIONS)

    def body(p_ref, c_ref, sm_ref, sp_ref, *rest):
        outs, scr = rest[:3 * nd], rest[3 * nd]
        c, sm, sp = c_ref[...], sm_ref[...], sp_ref[...]
        for gi, off in enumerate((OFF_AQ, OFF_AK, OFF_AV)):
            for h in range(ATTN_HEADS):
                t = p_ref[:, off + h * HEAD_DIM: off + (h + 1) * HEAD_DIM]
                if off != OFF_AV:
                    t = t * c + pltpu.roll(t, HEAD_DIM - half, 1) * sm + pltpu.roll(t, half, 1) * sp
                scr[h] = t
            for di, dil in enumerate(DILATIONS):
                _to_residues(scr, outs[3 * di + gi], dil)

    tab = pl.BlockSpec((ROW_BLOCK, HEAD_DIM), lambda i: (i, 0))
    outs = pl.pallas_call(
        body, name=name, grid=(s // ROW_BLOCK,),
        in_specs=[pl.BlockSpec((ROW_BLOCK, 3 * ATTN_W), lambda i: (i, 0)), tab, tab, tab],
        out_specs=[_res_spec(ATTN_HEADS, d) for d in DILATIONS for _ in range(3)],
        out_shape=[_res_shape(s, ATTN_HEADS, d, BF16) for d in DILATIONS for _ in range(3)],
        scratch_shapes=[pltpu.VMEM((ATTN_HEADS, ROW_BLOCK, LANES), F32)],
        compiler_params=_params(("parallel",)),
    )(proj, *tables)
    return [tuple(outs[3 * di:3 * di + 3]) for di in range(nd)]


def _rope_bwd(grads, tables, name="rope_bwd"):
    s = grads[0][0].shape[0] * DILATIONS[0]
    half = ROPE_DIM // 2
    nd = len(DILATIONS)

    def body(*refs):
        ins = refs[:3 * nd]
        c_ref, sm_ref, sp_ref, o_ref = refs[3 * nd:3 * nd + 4]
        scrs = refs[3 * nd + 4:]
        c, sm, sp = c_ref[...], sm_ref[...], sp_ref[...]
        for gi, off in enumerate((OFF_AQ, OFF_AK, OFF_AV)):
            for di, dil in enumerate(DILATIONS):
                _from_residues(ins[3 * di + gi], scrs[di], dil)
            for h in range(ATTN_HEADS):
                t = scrs[0][h]
                for scr in scrs[1:]:
                    t = t + scr[h]
                if off != OFF_AV:
                    t = t * c + pltpu.roll(t * sm, half, 1) + pltpu.roll(t * sp, HEAD_DIM - half, 1)
                o_ref[:, off + h * HEAD_DIM: off + (h + 1) * HEAD_DIM] = t.astype(BF16)

    tab = pl.BlockSpec((ROW_BLOCK, HEAD_DIM), lambda i: (i, 0))
    return pl.pallas_call(
        body, name=name, grid=(s // ROW_BLOCK,),
        in_specs=[_res_spec(ATTN_HEADS, d) for d in DILATIONS for _ in range(3)] + [tab, tab, tab],
        out_specs=pl.BlockSpec((ROW_BLOCK, 3 * ATTN_W), lambda i: (i, 0)),
        out_shape=jax.ShapeDtypeStruct((s, 3 * ATTN_W), BF16),
        scratch_shapes=[pltpu.VMEM((ATTN_HEADS, ROW_BLOCK, LANES), F32) for _ in DILATIONS],
        compiler_params=_params(("parallel",)),
    )(*[t for g in grads for t in g], *tables)


def _window_specs(nb, width):
    qb, hb = ATTN_BLOCK, N_SIDE
    cur = pl.BlockSpec((qb, width), lambda r, j: (j, r))
    prev = pl.BlockSpec((hb, width), lambda r, j: (jnp.maximum(2 * j - 1, 0), r))
    nxt = pl.BlockSpec((hb, width), lambda r, j: (jnp.minimum(2 * j + 2, 2 * nb - 1), r))
    return prev, cur, nxt


def _band_masks(j, length):
    qb, hb = ATTN_BLOCK, N_SIDE
    row = lax.broadcasted_iota(jnp.int32, (qb, qb), 0)
    col = lax.broadcasted_iota(jnp.int32, (qb, qb), 1)

    def edge_pos(i):
        return j * qb - hb + i + jnp.where(i >= hb, qb, 0)

    def ok(a, b, outside):
        return (jnp.abs(a - b) <= N_SIDE) & (outside >= 0) & (outside < length)

    cur = jnp.abs(row - col) <= N_SIDE
    edge_k = ok(j * qb + row, edge_pos(col), edge_pos(col))
    edge_q = ok(edge_pos(row), j * qb + col, edge_pos(row))
    return cur, edge_k, edge_q


def _edge(prev_ref, next_ref, sl):
    return jnp.concatenate([prev_ref[:, sl], next_ref[:, sl]], axis=0)


def _attn_fwd(q, k, v, dil, name):
    length = q.shape[0]
    qb = ATTN_BLOCK
    nb = length // qb
    scale = HEAD_DIM ** -0.5

    def body(q_ref, kp_ref, kc_ref, kn_ref, vp_ref, vc_ref, vn_ref, o_ref, lse_ref):
        valid_c, valid_e, _ = _band_masks(pl.program_id(1), length)
        lane = lax.broadcasted_iota(jnp.int32, (qb, LANES), 1)
        lse_acc = jnp.zeros((qb, LANES), F32)
        for h in range(ATTN_HEADS):
            sl = slice(h * HEAD_DIM, (h + 1) * HEAD_DIM)
            qh = q_ref[:, sl]
            s_c = jnp.where(valid_c, _dot_nt(qh, kc_ref[:, sl]) * scale, NEG)
            s_e = jnp.where(valid_e, _dot_nt(qh, _edge(kp_ref, kn_ref, sl)) * scale, NEG)
            m = jnp.max(jnp.maximum(s_c, s_e), axis=-1, keepdims=True)
            p_c, p_e = jnp.exp(s_c - m), jnp.exp(s_e - m)
            den = jnp.sum(p_c + p_e, axis=-1, keepdims=True)
            acc = _dot(p_c.astype(BF16), vc_ref[:, sl]) + _dot(p_e.astype(BF16), _edge(vp_ref, vn_ref, sl))
            o_ref[:, sl] = acc / den
            lse_acc = jnp.where(lane == h, m + jnp.log(den), lse_acc)
        lse_ref[...] = lse_acc

    prev, cur, nxt = _window_specs(nb, ATTN_W)
    return pl.pallas_call(
        body, name=name, grid=(dil, nb),
        in_specs=[cur, prev, cur, nxt, prev, cur, nxt],
        out_specs=[cur, pl.BlockSpec((qb, LANES), lambda r, j: (j, r))],
        out_shape=[jax.ShapeDtypeStruct((length, dil * ATTN_W), F32),
                   jax.ShapeDtypeStruct((length, dil * LANES), F32)],
        compiler_params=_params(("parallel", "parallel")),
    )(q, k, k, k, v, v, v)


def _attn_combine(outs, lses, g, name="attn_combine"):
    s = outs[0].shape[0] * DILATIONS[0]
    nd = len(DILATIONS)

    def body(*refs):
        o_refs, l_refs = refs[:nd], refs[nd:2 * nd]
        g_ref, o_ref, n_ref = refs[2 * nd:2 * nd + 3]
        lse_outs = refs[2 * nd + 3:3 * nd + 3]
        o_scr, l_scr = refs[3 * nd + 3:4 * nd + 3], refs[4 * nd + 3:5 * nd + 3]
        for di, dil in enumerate(DILATIONS):
            _from_residues(o_refs[di], o_scr[di], dil)
            _from_residues(l_refs[di], l_scr[di], dil)
        ls = [scr[0] for scr in l_scr]
        m = ls[0]
        for l in ls[1:]:
            m = jnp.maximum(m, l)
        es = [jnp.exp(l - m) for l in ls]
        z = es[0]
        for e in es[1:]:
            z = z + e
        ws = [e / z for e in es]
        l_scr[0][0] = m + jnp.log(z)
        for di, dil in enumerate(DILATIONS):
            _to_residues(l_scr[0], lse_outs[di], dil)
        ssq = jnp.zeros((ROW_BLOCK, 1), F32)
        for h in range(ATTN_HEADS):
            sl = slice(h * HEAD_DIM, (h + 1) * HEAD_DIM)
            acc = ws[0][:, h:h + 1] * o_scr[0][h]
            for w, scr in zip(ws[1:], o_scr[1:]):
                acc = acc + w[:, h:h + 1] * scr[h]
            o_ref[:, sl] = acc
            ssq = ssq + jnp.sum(acc * acc, axis=-1, keepdims=True)
        r = lax.rsqrt(ssq * (1.0 / ATTN_W) + EPS)
        n_ref[...] = (o_ref[...] * r * g_ref[...]).astype(BF16)

    blk = pl.BlockSpec((ROW_BLOCK, ATTN_W), lambda i: (i, 0))
    outs_ = pl.pallas_call(
        body, name=name, grid=(s // ROW_BLOCK,),
        in_specs=[_res_spec(ATTN_HEADS, d) for d in DILATIONS] + [_res_spec(1, d) for d in DILATIONS]
        + [pl.BlockSpec((1, ATTN_W), lambda i: (0, 0))],
        out_specs=[blk, blk] + [_res_spec(1, d) for d in DILATIONS],
        out_shape=[jax.ShapeDtypeStruct((s, ATTN_W), F32), jax.ShapeDtypeStruct((s, ATTN_W), BF16)]
        + [_res_shape(s, 1, d, F32) for d in DILATIONS],
        scratch_shapes=[pltpu.VMEM((ATTN_HEADS, ROW_BLOCK, LANES), F32) for _ in DILATIONS]
        + [pltpu.VMEM((1, ROW_BLOCK, LANES), F32) for _ in DILATIONS],
        compiler_params=_params(("parallel",)),
    )(*outs, *lses, g)
    return outs_[0], outs_[1], list(outs_[2:])


def _attn_prebwd(dcat, o, g, name="attn_prebwd"):
    s = o.shape[0]
    nd = len(DILATIONS)

    def body(dy_ref, o_ref, g_ref, *rest):
        do_outs, delta_outs, gg_ref = rest[:nd], rest[nd:2 * nd], rest[2 * nd]
        do_scr, delta_scr = rest[2 * nd + 1], rest[2 * nd + 2]
        i = pl.program_id(0)
        dy, ov = dy_ref[...], o_ref[...]
        r = lax.rsqrt(jnp.mean(ov * ov, axis=-1, keepdims=True) + EPS)
        dyg = dy * g_ref[...]
        c = jnp.mean(dyg * ov, axis=-1, keepdims=True)
        do = r * dyg - ov * (r * r * r * c)
        prod = do * ov
        lane = lax.broadcasted_iota(jnp.int32, (ROW_BLOCK, LANES), 1)
        acc = jnp.zeros((ROW_BLOCK, LANES), F32)
        for h in range(ATTN_HEADS):
            sl = slice(h * HEAD_DIM, (h + 1) * HEAD_DIM)
            do_scr[h] = do[:, sl]
            acc = jnp.where(lane == h, jnp.sum(prod[:, sl], axis=-1, keepdims=True), acc)
        delta_scr[0] = acc
        for di, dil in enumerate(DILATIONS):
            _to_residues(do_scr, do_outs[di], dil)
            _to_residues(delta_scr, delta_outs[di], dil)

        @pl.when(i == 0)
        def _():
            gg_ref[...] = jnp.zeros_like(gg_ref)

        gg_ref[...] += jnp.sum(dy * ov * r, axis=0, keepdims=True)

    blk = pl.BlockSpec((ROW_BLOCK, ATTN_W), lambda i: (i, 0))
    vec = pl.BlockSpec((1, ATTN_W), lambda i: (0, 0))
    outs = pl.pallas_call(
        body, name=name, grid=(s // ROW_BLOCK,),
        in_specs=[blk, blk, vec],
        out_specs=[_res_spec(ATTN_HEADS, d) for d in DILATIONS] + [_res_spec(1, d) for d in DILATIONS] + [vec],
        out_shape=[_res_shape(s, ATTN_HEADS, d, BF16) for d in DILATIONS]
        + [_res_shape(s, 1, d, F32) for d in DILATIONS] + [jax.ShapeDtypeStruct((1, ATTN_W), F32)],
        scratch_shapes=[pltpu.VMEM((ATTN_HEADS, ROW_BLOCK, LANES), F32), pltpu.VMEM((1, ROW_BLOCK, LANES), F32)],
        compiler_params=_params(("arbitrary",)),
    )(dcat, o, g)
    return list(outs[:nd]), list(outs[nd:2 * nd]), outs[2 * nd]


def _attn_bwd(q, k, v, do, lse, delta, dil, name):
    length = q.shape[0]
    qb = ATTN_BLOCK
    nb = length // qb
    scale = HEAD_DIM ** -0.5

    def body(qp, qc, qn, kp, kc, kn, vp, vc, vn, dop, doc, don, lp, lc, ln, dp, dc, dn, dq_ref, dk_ref, dv_ref):
        valid_c, valid_ek, valid_eq = _band_masks(pl.program_id(1), length)
        everything = slice(None)
        lse_e, del_e = _edge(lp, ln, everything), _edge(dp, dn, everything)
        for h in range(ATTN_HEADS):
            sl = slice(h * HEAD_DIM, (h + 1) * HEAD_DIM)
            hc = slice(h, h + 1)
            q_c, k_c, v_c, do_c = qc[:, sl], kc[:, sl], vc[:, sl], doc[:, sl]
            q_e, k_e, v_e, do_e = _edge(qp, qn, sl), _edge(kp, kn, sl), _edge(vp, vn, sl), _edge(dop, don, sl)
            lse_c, del_c = lc[:, hc], dc[:, hc]
            p = jnp.where(valid_c, jnp.exp(_dot_nt(q_c, k_c) * scale - lse_c), 0.0)
            ds = (p * (_dot_nt(do_c, v_c) - del_c)).astype(BF16)
            dq = _dot(ds, k_c)
            dk = _dot_tn(ds, q_c)
            dvh = _dot_tn(p.astype(BF16), do_c)
            p = jnp.where(valid_ek, jnp.exp(_dot_nt(q_c, k_e) * scale - lse_c), 0.0)
            ds = (p * (_dot_nt(do_c, v_e) - del_c)).astype(BF16)
            dq = dq + _dot(ds, k_e)
            p = jnp.where(valid_eq, jnp.exp(_dot_nt(q_e, k_c) * scale - lse_e[:, hc]), 0.0)
            ds = (p * (_dot_nt(do_e, v_c) - del_e[:, hc])).astype(BF16)
            dk = dk + _dot_tn(ds, q_e)
            dvh = dvh + _dot_tn(p.astype(BF16), do_e)
            dq_ref[:, sl] = dq * scale
            dk_ref[:, sl] = dk * scale
            dv_ref[:, sl] = dvh

    wide, narrow = list(_window_specs(nb, ATTN_W)), list(_window_specs(nb, LANES))
    return tuple(pl.pallas_call(
        body, name=name, grid=(dil, nb),
        in_specs=wide * 4 + narrow * 2,
        out_specs=[wide[1]] * 3,
        out_shape=[jax.ShapeDtypeStruct((length, dil * ATTN_W), F32)] * 3,
        compiler_params=_params(("parallel", "parallel")),
    )(q, q, q, k, k, k, v, v, v, do, do, do, lse, lse, lse, delta, delta, delta))


def _gate_matrices(gf_up, gb_up):
    pad = LANES - 2 * GLA_RANK
    uf = jnp.concatenate([gf_up, jnp.zeros((GLA_RANK + pad, GLA_KW), gf_up.dtype)], axis=0)
    ub = jnp.concatenate([jnp.zeros((GLA_RANK, GLA_KW), gb_up.dtype), gb_up, jnp.zeros((pad, GLA_KW), gb_up.dtype)], axis=0)
    return uf.astype(BF16), ub.astype(BF16)


def _log_sigmoid(x):
    return jnp.minimum(x, 0.0) - jnp.log(1.0 + jnp.exp(-jnp.abs(x)))


def _gla_gates(proj, uf, ub, gf_b, gb_b, name="gla_gates"):
    s = proj.shape[0]

    def body(z_ref, uf_ref, ub_ref, bf_ref, bb_ref, gf_ref, gb_ref):
        z = z_ref[...].astype(BF16)
        gf_ref[...] = _log_sigmoid(_dot(z, uf_ref[...]) + bf_ref[...]) * (1.0 / GLA_GATE_NORM)
        gb_ref[...] = _log_sigmoid(_dot(z, ub_ref[...]) + bb_ref[...]) * (1.0 / GLA_GATE_NORM)

    mat = pl.BlockSpec((LANES, GLA_KW), lambda i: (0, 0))
    vec = pl.BlockSpec((1, GLA_KW), lambda i: (0, 0))
    out = pl.BlockSpec((ROW_BLOCK, GLA_KW), lambda i: (i, 0))
    return pl.pallas_call(
        body, name=name, grid=(s // ROW_BLOCK,),
        in_specs=[pl.BlockSpec((ROW_BLOCK, LANES), lambda i: (i, OFF_Z // LANES)), mat, mat, vec, vec],
        out_specs=[out, out],
        out_shape=[jax.ShapeDtypeStruct((s, GLA_KW), F32)] * 2,
        compiler_params=_params(("parallel",)),
    )(proj, uf, ub, gf_b, gb_b)


def _gla_gates_bwd(dgf, dgb, proj, uf, ub, gf_b, gb_b, name="gla_gates_bwd"):
    s = proj.shape[0]

    def body(dgf_ref, dgb_ref, z_ref, uf_ref, ub_ref, bf_ref, bb_ref, dz_ref, guf_ref, gub_ref, gbf_ref, gbb_ref):
        i = pl.program_id(0)
        z = z_ref[...].astype(BF16)
        uf_, ub_ = uf_ref[...], ub_ref[...]
        dpf = dgf_ref[...] * (1.0 / GLA_GATE_NORM) * _sigmoid(-(_dot(z, uf_) + bf_ref[...]))
        dpb = dgb_ref[...] * (1.0 / GLA_GATE_NORM) * _sigmoid(-(_dot(z, ub_) + bb_ref[...]))
        dpf_b, dpb_b = dpf.astype(BF16), dpb.astype(BF16)
        dz_ref[...] = (_dot_nt(dpf_b, uf_) + _dot_nt(dpb_b, ub_)).astype(BF16)

        @pl.when(i == 0)
        def _():
            for r in (guf_ref, gub_ref, gbf_ref, gbb_ref):
                r[...] = jnp.zeros_like(r)

        guf_ref[...] += _dot_tn(z, dpf_b)
        gub_ref[...] += _dot_tn(z, dpb_b)
        gbf_ref[...] += jnp.sum(dpf, axis=0, keepdims=True)
        gbb_ref[...] += jnp.sum(dpb, axis=0, keepdims=True)

    mat = pl.BlockSpec((LANES, GLA_KW), lambda i: (0, 0))
    vec = pl.BlockSpec((1, GLA_KW), lambda i: (0, 0))
    blk = pl.BlockSpec((ROW_BLOCK, GLA_KW), lambda i: (i, 0))
    return pl.pallas_call(
        body, name=name, grid=(s // ROW_BLOCK,),
        in_specs=[blk, blk, pl.BlockSpec((ROW_BLOCK, LANES), lambda i: (i, OFF_Z // LANES)), mat, mat, vec, vec],
        out_specs=[pl.BlockSpec((ROW_BLOCK, LANES), lambda i: (i, 0)), mat, mat, vec, vec],
        out_shape=[jax.ShapeDtypeStruct((s, LANES), BF16), jax.ShapeDtypeStruct((LANES, GLA_KW), F32),
                   jax.ShapeDtypeStruct((LANES, GLA_KW), F32), jax.ShapeDtypeStruct((1, GLA_KW), F32),
                   jax.ShapeDtypeStruct((1, GLA_KW), F32)],
        compiler_params=_params(("arbitrary",)),
    )(dgf, dgb, proj, uf, ub, gf_b, gb_b)


def _split3(x):
    x1 = x.astype(BF16)
    r1 = x - x1.astype(F32)
    x2 = r1.astype(BF16)
    x3 = (r1 - x2.astype(F32)).astype(BF16)
    return x1, x2, x3


def _dot_exact(mask_bf, x):
    x1, x2, x3 = _split3(x)
    return _dot(mask_bf, x1) + _dot(mask_bf, x2) + _dot(mask_bf, x3)


def _chunk_terms(q_ref, k_ref, g_ref, rs, reverse):
    c = GLA_CHUNK
    row = lax.broadcasted_iota(jnp.int32, (c, c), 0)
    col = lax.broadcasted_iota(jnp.int32, (c, c), 1)
    allowed = (col >= row) if reverse else (col <= row)
    seen_by = (col <= row) if reverse else (col >= row)
    mid, last = (c // 2, 0) if reverse else (c // 2 - 1, c - 1)
    q = q_ref[rs, :] * (GLA_DK ** -0.5)
    k = k_ref[rs, :]
    b = _dot_exact(jnp.where(allowed, 1.0, 0.0).astype(BF16), g_ref[rs, :])
    bref, blast = b[mid:mid + 1, :], b[last:last + 1, :]
    e_q, e_k, e_in, e_st = jnp.exp(b - bref), jnp.exp(bref - b), jnp.exp(b), jnp.exp(blast - b)
    return dict(allowed=allowed, seen_by=seen_by, last=last, q=q, k=k, e_q=e_q, e_k=e_k, e_in=e_in, e_st=e_st,
                dec=jnp.exp(blast), qe=q * e_q, ke=k * e_k, qin=q * e_in, kst=k * e_st)


def _gla_blockspecs(s, reverse_order):
    cb = GLA_CHUNKS_PER_STEP
    rows = cb * GLA_CHUNK
    nsteps = s // rows

    def rb(n):
        return (nsteps - 1 - n) if reverse_order else n

    qspec = pl.BlockSpec((rows, GLA_DK), lambda h, n: (rb(n), OFF_GQ // GLA_DK + h))
    kspec = pl.BlockSpec((rows, GLA_DK), lambda h, n: (rb(n), OFF_GK // GLA_DK + h))
    vspec = pl.BlockSpec((rows, GLA_DV), lambda h, n: (rb(n), OFF_GV // GLA_DV + h))
    gspec = pl.BlockSpec((rows, GLA_DK), lambda h, n: (rb(n), h))
    ospec = pl.BlockSpec((rows, GLA_DV), lambda h, n: (rb(n), h))
    sspec = pl.BlockSpec((1, cb, GLA_DV, GLA_DK), lambda h, n: (h, rb(n), 0, 0))
    return cb, rows, nsteps, qspec, kspec, vspec, gspec, ospec, sspec


def _gla_fwd(proj, g, reverse, name):
    s = proj.shape[0]
    cb, rows, nsteps, qspec, kspec, vspec, gspec, ospec, sspec = _gla_blockspecs(s, reverse)

    def body(q_ref, k_ref, v_ref, g_ref, o_ref, st_ref, state):
        @pl.when(pl.program_id(1) == 0)
        def _():
            state[...] = jnp.zeros_like(state)

        for c in (reversed(range(cb)) if reverse else range(cb)):
            rs = slice(c * GLA_CHUNK, (c + 1) * GLA_CHUNK)
            t = _chunk_terms(q_ref, k_ref, g_ref, rs, reverse)
            v = v_ref[rs, :].astype(BF16)
            a = jnp.where(t["allowed"], _dot_nt(t["qe"].astype(BF16), t["ke"].astype(BF16)), 0.0)
            st = state[...]
            st_ref[0, c] = st
            o_ref[rs, :] = _dot(a.astype(BF16), v) + _dot_nt(t["qin"].astype(BF16), st.astype(BF16))
            state[...] = st * t["dec"] + _dot_tn(v, t["kst"].astype(BF16))

    return pl.pallas_call(
        body, name=name, grid=(GLA_HEADS, nsteps),
        in_specs=[qspec, kspec, vspec, gspec],
        out_specs=[ospec, sspec],
        out_shape=[jax.ShapeDtypeStruct((s, GLA_VW), F32),
                   jax.ShapeDtypeStruct((GLA_HEADS, s // GLA_CHUNK, GLA_DV, GLA_DK), F32)],
        scratch_shapes=[pltpu.VMEM((GLA_DV, GLA_DK), F32)],
        compiler_params=_params(("parallel", "arbitrary")),
    )(proj, proj, proj, g)


def _gla_bwd(proj, g, do, states, reverse, name):
    s = proj.shape[0]
    cb, rows, nsteps, qspec, kspec, vspec, gspec, ospec, sspec = _gla_blockspecs(s, not reverse)

    def body(q_ref, k_ref, v_ref, g_ref, do_ref, sp_ref, dq_ref, dk_ref, dv_ref, dg_ref, dstate):
        @pl.when(pl.program_id(1) == 0)
        def _():
            dstate[...] = jnp.zeros_like(dstate)

        for c in (range(cb) if reverse else reversed(range(cb))):
            rs = slice(c * GLA_CHUNK, (c + 1) * GLA_CHUNK)
            t = _chunk_terms(q_ref, k_ref, g_ref, rs, reverse)
            v = v_ref[rs, :].astype(BF16)
            do = do_ref[rs, :]
            qe_b, ke_b = t["qe"].astype(BF16), t["ke"].astype(BF16)
            qin_b, kst_b = t["qin"].astype(BF16), t["kst"].astype(BF16)
            a = jnp.where(t["allowed"], _dot_nt(qe_b, ke_b), 0.0)
            da = jnp.where(t["allowed"], _dot_nt(do, v), 0.0).astype(BF16)
            dqe = _dot(da, ke_b)
            dke = _dot_tn(da, qe_b)
            sp = sp_ref[0, c]
            ds = dstate[...]
            ds_b = ds.astype(BF16)
            dqin = _dot(do, sp.astype(BF16))
            dkst = _dot(v, ds_b)
            dv_ref[rs, :] = _dot_tn(a.astype(BF16), do) + _dot_nt(kst_b, ds_b)
            ddec = jnp.sum(sp * ds, axis=0, keepdims=True)
            dstate[...] = ds * t["dec"] + _dot_tn(do, qin_b)
            dq_ref[rs, :] = (dqe * t["e_q"] + dqin * t["e_in"]) * (GLA_DK ** -0.5)
            dk_ref[rs, :] = dke * t["e_k"] + dkst * t["e_st"]
            kk = dkst * t["kst"]
            db = dqe * t["qe"] - dke * t["ke"] + dqin * t["qin"] - kk
            extra = jnp.sum(kk, axis=0, keepdims=True) + ddec * t["dec"]
            rowi = lax.broadcasted_iota(jnp.int32, (GLA_CHUNK, GLA_DK), 0)
            db = db + jnp.where(rowi == t["last"], extra, 0.0)
            dg_ref[rs, :] = _dot_exact(jnp.where(t["seen_by"], 1.0, 0.0).astype(BF16), db)

    return pl.pallas_call(
        body, name=name, grid=(GLA_HEADS, nsteps),
        in_specs=[qspec, kspec, vspec, gspec, ospec, sspec],
        out_specs=[gspec, gspec, ospec, gspec],
        out_shape=[jax.ShapeDtypeStruct((s, GLA_KW), F32), jax.ShapeDtypeStruct((s, GLA_KW), F32),
                   jax.ShapeDtypeStruct((s, GLA_VW), F32), jax.ShapeDtypeStruct((s, GLA_KW), F32)],
        scratch_shapes=[pltpu.VMEM((GLA_DV, GLA_DK), F32)],
        compiler_params=_params(("parallel", "arbitrary")),
    )(proj, proj, proj, g, do, states)


def _gla_post(o_f, o_b, proj, g, name="gla_post"):
    s = o_f.shape[0]

    def body(of_ref, ob_ref, gr_ref, g_ref, o_ref):
        gv = g_ref[...]
        for h in range(GLA_HEADS):
            sl = slice(h * GLA_DV, (h + 1) * GLA_DV)
            osum = of_ref[:, sl] + ob_ref[:, sl]
            r = lax.rsqrt(jnp.mean(osum * osum, axis=-1, keepdims=True) + EPS)
            gr = gr_ref[:, sl]
            o_ref[:, sl] = (osum * r * gv * (gr * _sigmoid(gr))).astype(BF16)

    blk = pl.BlockSpec((ROW_BLOCK, GLA_VW), lambda i: (i, 0))
    return pl.pallas_call(
        body, name=name, grid=(s // ROW_BLOCK,),
        in_specs=[blk, blk, pl.BlockSpec((ROW_BLOCK, GLA_VW), lambda i: (i, OFF_GR // GLA_VW)),
                  pl.BlockSpec((1, GLA_DV), lambda i: (0, 0))],
        out_specs=blk,
        out_shape=jax.ShapeDtypeStruct((s, GLA_VW), BF16),
        compiler_params=_params(("parallel",)),
    )(o_f, o_b, proj, g)


def _gla_post_bwd(dcat, o_f, o_b, proj, g, name="gla_post_bwd"):
    s = o_f.shape[0]

    def body(dy_ref, of_ref, ob_ref, gr_ref, g_ref, do_ref, dgr_ref, gg_ref):
        i = pl.program_id(0)
        gv = g_ref[...]
        gg = jnp.zeros((1, GLA_DV), F32)
        for h in range(GLA_HEADS):
            sl = slice(h * GLA_DV, (h + 1) * GLA_DV)
            osum = of_ref[:, sl] + ob_ref[:, sl]
            r = lax.rsqrt(jnp.mean(osum * osum, axis=-1, keepdims=True) + EPS)
            gr, dy = gr_ref[:, sl], dy_ref[:, sl]
            sg = _sigmoid(gr)
            dgr_ref[:, sl] = (dy * (osum * r * gv) * (sg * (1.0 + gr * (1.0 - sg)))).astype(BF16)
            dn = dy * (gr * sg)
            dng = dn * gv
            c = jnp.mean(dng * osum, axis=-1, keepdims=True)
            do_ref[:, sl] = (r * dng - osum * (r * r * r * c)).astype(BF16)
            gg = gg + jnp.sum(dn * osum * r, axis=0, keepdims=True)

        @pl.when(i == 0)
        def _():
            gg_ref[...] = jnp.zeros_like(gg_ref)

        gg_ref[...] += gg

    blk = pl.BlockSpec((ROW_BLOCK, GLA_VW), lambda i: (i, 0))
    vec = pl.BlockSpec((1, GLA_DV), lambda i: (0, 0))
    return pl.pallas_call(
        body, name=name, grid=(s // ROW_BLOCK,),
        in_specs=[pl.BlockSpec((ROW_BLOCK, GLA_VW), lambda i: (i, 1)), blk, blk,
                  pl.BlockSpec((ROW_BLOCK, GLA_VW), lambda i: (i, OFF_GR // GLA_VW)), vec],
        out_specs=[blk, blk, vec],
        out_shape=[jax.ShapeDtypeStruct((s, GLA_VW), BF16), jax.ShapeDtypeStruct((s, GLA_VW), BF16),
                   jax.ShapeDtypeStruct((1, GLA_DV), F32)],
        compiler_params=_params(("arbitrary",)),
    )(dcat, o_f, o_b, proj, g)


def _assemble_dproj(dpa, dq_f, dq_b, dk_f, dk_b, dv_f, dv_b, dgr, dz, name="assemble_dproj"):
    s = dpa.shape[0]

    def body(dpa_ref, dqf, dqb, dkf, dkb, dvf, dvb, dgr_ref, dz_ref, o_ref):
        o_ref[:, 0:OFF_GQ] = dpa_ref[...]
        o_ref[:, OFF_GQ:OFF_GK] = (dqf[...] + dqb[...]).astype(BF16)
        o_ref[:, OFF_GK:OFF_GV] = (dkf[...] + dkb[...]).astype(BF16)
        o_ref[:, OFF_GV:OFF_GR] = (dvf[...] + dvb[...]).astype(BF16)
        o_ref[:, OFF_GR:OFF_Z] = dgr_ref[...]
        o_ref[:, OFF_Z:IN_PAD] = dz_ref[...]

    def blk(w):
        return pl.BlockSpec((ROW_BLOCK, w), lambda i: (i, 0))

    return pl.pallas_call(
        body, name=name, grid=(s // ROW_BLOCK,),
        in_specs=[blk(3 * ATTN_W)] + [blk(GLA_KW)] * 4 + [blk(GLA_VW)] * 3 + [blk(LANES)],
        out_specs=blk(IN_PAD),
        out_shape=jax.ShapeDtypeStruct((s, IN_PAD), BF16),
        compiler_params=_params(("parallel",)),
    )(dpa, dq_f, dq_b, dk_f, dk_b, dv_f, dv_b, dgr, dz)


CONV_ROWS = 256
CONV_COLS = 1408
HALO = SUBLANES


def _halo_specs(s, tr, tc, col_of):
    per = tr // HALO
    last = s // HALO - 1
    cur = pl.BlockSpec((tr, tc), lambda c, i: (i, col_of(c)))
    prev = pl.BlockSpec((HALO, tc), lambda c, i: (jnp.maximum(i * per - 1, 0), col_of(c)))
    nxt = pl.BlockSpec((HALO, tc), lambda c, i: (jnp.minimum((i + 1) * per, last), col_of(c)))
    return prev, cur, nxt


def _extended(prev_ref, cur_ref, next_ref, i, s, tr):
    x = jnp.concatenate([prev_ref[...], cur_ref[...], next_ref[...]], axis=0)
    idx = i * tr - HALO + lax.broadcasted_iota(jnp.int32, x.shape, 0)
    return jnp.where((idx >= 0) & (idx < s), x, 0.0)


def _conv_glu(gate, up, conv_w, conv_b, name="conv_glu"):
    s, f = gate.shape
    tr, tc = CONV_ROWS, CONV_COLS
    ext = tr + 2 * HALO

    def body(gp, gc, gn, up_ref, w_ref, b_ref, o_ref):
        i = pl.program_id(1)
        ge = _extended(gp, gc, gn, i, s, tr)
        w = w_ref[...]
        conv = (w[0:1] * pltpu.roll(ge, 1, 0) + w[1:2] * ge + w[2:3] * pltpu.roll(ge, ext - 1, 0))[HALO:HALO + tr]
        conv = conv + b_ref[...]
        o_ref[...] = (conv * _sigmoid(conv) * up_ref[...]).astype(BF16)

    prev, cur, nxt = _halo_specs(s, tr, tc, lambda c: c)
    return pl.pallas_call(
        body, name=name, grid=(f // tc, s // tr),
        in_specs=[prev, cur, nxt, cur, pl.BlockSpec((3, tc), lambda c, i: (0, c)), pl.BlockSpec((1, tc), lambda c, i: (0, c))],
        out_specs=cur,
        out_shape=jax.ShapeDtypeStruct((s, f), BF16),
        compiler_params=_params(("parallel", "parallel")),
    )(gate, gate, gate, up, conv_w, conv_b)


def _conv_glu_bwd(dact, gate, up, conv_w, conv_b, name="conv_glu_bwd"):
    s, f = gate.shape
    tr, tc = CONV_ROWS, CONV_COLS
    ext = tr + 2 * HALO

    def body(dp, dc, dn, gp, gc, gn, upp, upc, upn, w_ref, b_ref, dg_ref, du_ref, gw_ref, gb_ref):
        i = pl.program_id(1)
        ge = _extended(gp, gc, gn, i, s, tr)
        ue = _extended(upp, upc, upn, i, s, tr)
        de = _extended(dp, dc, dn, i, s, tr)
        w = w_ref[...]
        g_prev, g_next = pltpu.roll(ge, 1, 0), pltpu.roll(ge, ext - 1, 0)
        conv = w[0:1] * g_prev + w[1:2] * ge + w[2:3] * g_next + b_ref[...]
        sg = _sigmoid(conv)
        du_ref[...] = (de * (conv * sg))[HALO:HALO + tr].astype(BF16)
        dconv = de * ue * (sg * (1.0 + conv * (1.0 - sg)))
        dgate = w[0:1] * pltpu.roll(dconv, ext - 1, 0) + w[1:2] * dconv + w[2:3] * pltpu.roll(dconv, 1, 0)
        dg_ref[...] = dgate[HALO:HALO + tr].astype(BF16)
        inner = slice(HALO, HALO + tr)
        dci = dconv[inner]

        @pl.when(i == 0)
        def _():
            gw_ref[...] = jnp.zeros_like(gw_ref)
            gb_ref[...] = jnp.zeros_like(gb_ref)

        gw_ref[0:1, :] += jnp.sum(dci * g_prev[inner], axis=0, keepdims=True)
        gw_ref[1:2, :] += jnp.sum(dci * ge[inner], axis=0, keepdims=True)
        gw_ref[2:3, :] += jnp.sum(dci * g_next[inner], axis=0, keepdims=True)
        gb_ref[...] += jnp.sum(dci, axis=0, keepdims=True)

    prev, cur, nxt = _halo_specs(s, tr, tc, lambda c: c)
    wspec = pl.BlockSpec((3, tc), lambda c, i: (0, c))
    bspec = pl.BlockSpec((1, tc), lambda c, i: (0, c))
    return pl.pallas_call(
        body, name=name, grid=(f // tc, s // tr),
        in_specs=[prev, cur, nxt] * 3 + [wspec, bspec],
        out_specs=[cur, cur, wspec, bspec],
        out_shape=[jax.ShapeDtypeStruct((s, f), BF16), jax.ShapeDtypeStruct((s, f), BF16),
                   jax.ShapeDtypeStruct((3, f), F32), jax.ShapeDtypeStruct((1, f), F32)],
        compiler_params=_params(("parallel", "arbitrary")),
    )(dact, dact, dact, gate, gate, gate, up, up, up, conv_w, conv_b)


def _local_step(x, target, w, late_weights=None, grad_sink=None, first_dep=()):
    s = x.shape[0]
    tables = _rope_tables(s)
    uf, ub = _gate_matrices(w["gf_up"], w["gb_up"])
    if grad_sink is None:
        grad_sink = lambda names, grads: ()

    n1 = _rms_fwd(x, w["norm1_g"], "norm1")
    proj = _matmul([(n1, w["w_in"])], "nn", F32, 1024, 896, D_MODEL, "in_proj", deps=first_dep)
    qkv = _rope_fwd(proj, tables)
    branches = [_attn_fwd(*qkv[di], d, f"attn_fwd_d{d}") for di, d in enumerate(DILATIONS)]
    o_mix, ao, lse = _attn_combine([b[0] for b in branches], [b[1] for b in branches], w["attn_norm_g"])
    g_f, g_b = _gla_gates(proj, uf, ub, w["gf_b"], w["gb_b"])
    o_f, st_f = _gla_fwd(proj, g_f, False, "gla_fwd_f")
    o_b, st_b = _gla_fwd(proj, g_b, True, "gla_fwd_b")
    go = _gla_post(o_f, o_b, proj, w["gla_norm_g"])
    cat = jnp.concatenate([ao, go], axis=1)
    if late_weights is not None:
        w = {**w, **late_weights(cat)}
    h1 = _matmul([(cat, w["w_out"])], "nn", F32, 512, 1024, D_MODEL, "out_proj", res=x)
    n2 = _rms_fwd(h1, w["norm2_g"], "norm2")
    gate = _matmul([(n2, w["w_gate"])], "nn", F32, 512, 1408, D_MODEL, "ffn_gate")
    up = _matmul([(n2, w["w_up"])], "nn", F32, 512, 1408, D_MODEL, "ffn_up")
    act = _conv_glu(gate, up, w["conv_w"], w["conv_b"])
    h2 = _matmul([(act, w["w_down"])], "nn", F32, 1024, 1024, 1408, "ffn_down", res=h1)
    dh2, dh2_b, loss_acc, g_final = _final_loss(h2, target, w["final_norm_g"])

    dact = _matmul([(dh2_b, w["w_down"])], "nt", F32, 512, 1408, D_MODEL, "d_act")
    g_w_down = _matmul([(act, dh2_b)], "tn", F32, 1408, 1024, 2048, "g_w_down")
    dep = grad_sink(["w_down"], [g_w_down])
    dgate, dup, g_conv_w, g_conv_b = _conv_glu_bwd(dact, gate, up, w["conv_w"], w["conv_b"])
    g_w_gate = _matmul([(n2, dgate)], "tn", F32, 1024, 1408, 2048, "g_w_gate", deps=dep)
    g_w_up = _matmul([(n2, dup)], "tn", F32, 1024, 1408, 2048, "g_w_up")
    dep = grad_sink(["w_gate", "w_up"], [g_w_gate, g_w_up])
    dn2 = _matmul([(dgate, w["w_gate"]), (dup, w["w_up"])], "nt", F32, 1024, 1024, 1408, "d_n2", deps=dep)
    dh1, dh1_b, g_norm2 = _rms_bwd(dn2, h1, w["norm2_g"], dh2, "norm2_bwd")

    g_w_out = _matmul([(cat, dh1_b)], "tn", F32, 1024, 1024, 2048, "g_w_out")
    dep = grad_sink(["w_out"], [g_w_out])
    dcat = _matmul([(dh1_b, w["w_out"])], "nt", F32, 512, 1024, D_MODEL, "d_cat", deps=dep)
    do_attn, delta, g_attn_norm = _attn_prebwd(dcat, o_mix, w["attn_norm_g"])
    grads = [_attn_bwd(*qkv[di], do_attn[di], lse[di], delta[di], d, f"attn_bwd_d{d}")
             for di, d in enumerate(DILATIONS)]
    dpa = _rope_bwd(grads, tables)
    do_gla, dgr, g_gla_norm = _gla_post_bwd(dcat, o_f, o_b, proj, w["gla_norm_g"])
    dq_f, dk_f, dv_f, dg_f = _gla_bwd(proj, g_f, do_gla, st_f, False, "gla_bwd_f")
    dq_b, dk_b, dv_b, dg_b = _gla_bwd(proj, g_b, do_gla, st_b, True, "gla_bwd_b")
    dz, g_uf, g_ub, g_gf_b, g_gb_b = _gla_gates_bwd(dg_f, dg_b, proj, uf, ub, w["gf_b"], w["gb_b"])
    dproj = _assemble_dproj(dpa, dq_f, dq_b, dk_f, dk_b, dv_f, dv_b, dgr, dz)
    g_w_in = _matmul([(n1, dproj)], "tn", F32, 1024, 896, 2048, "g_w_in")
    dep = grad_sink(["w_in"], [g_w_in])
    dn1 = _matmul([(dproj, w["w_in"])], "nt", F32, 512, 1024, IN_PAD, "d_n1", deps=dep)
    grad_x, _, g_norm1 = _rms_bwd(dn1, x, w["norm1_g"], dh1, "norm1_bwd")

    g = dict(norm1_g=g_norm1, w_in=g_w_in, gf_up=g_uf[:GLA_RANK], gf_b=g_gf_b,
             gb_up=g_ub[GLA_RANK:2 * GLA_RANK], gb_b=g_gb_b, gla_norm_g=g_gla_norm, attn_norm_g=g_attn_norm,
             w_out=g_w_out, norm2_g=g_norm2, w_gate=g_w_gate, w_up=g_w_up, conv_w=g_conv_w, conv_b=g_conv_b,
             w_down=g_w_down, final_norm_g=g_final)
    return loss_acc, grad_x, g


def _me_and_peers():
    x, y, c = lax.axis_index("x"), lax.axis_index("y"), lax.axis_index("c")
    me = 4 * x + 2 * y + c
    peers = []
    for kbits in range(1, N_DEV):
        px, py, pc = x ^ (kbits >> 2 & 1), y ^ (kbits >> 1 & 1), c ^ (kbits & 1)
        peers.append(((px, py, pc), 4 * px + 2 * py + pc))
    return me, peers


_HBM = pl.BlockSpec(memory_space=pltpu.HBM)
_SEM = pl.BlockSpec(memory_space=pltpu.SEMAPHORE)
_ANY = pl.BlockSpec(memory_space=pl.ANY)
_EFFECT = pltpu.SideEffectType.DATAFLOW_SIDE_EFFECTING


def _exchange_copies(src_refs, land_refs, send_sems, recv_sems, scatter):
    me, peers = _me_and_peers()
    out = []
    for a, (src, land) in enumerate(zip(src_refs, land_refs)):
        for kk, (dev, idx) in enumerate(peers):
            out.append(pltpu.make_async_remote_copy(
                src_ref=src.at[idx] if scatter else src, dst_ref=land.at[me],
                send_sem=send_sems.at[a * (N_DEV - 1) + kk], recv_sem=recv_sems.at[a * (N_DEV - 1) + kk],
                device_id=dev, device_id_type=MESH_ID))
    return out


def _exchange_start(srcs, lands, scatter, name, deps=()):
    n, nd = len(srcs), len(deps)

    def body(*refs):
        src_refs, land_refs = refs[:n], refs[n:2 * n]
        send_sems, recv_sems = refs[2 * n + nd:2 * n + nd + 2]
        token = refs[-1]
        for cp in _exchange_copies(src_refs, land_refs, send_sems, recv_sems, scatter):
            cp.start()
        token[...] = jnp.zeros_like(token)

    outs = pl.pallas_call(
        body, name=name,
        in_specs=[_HBM] * (2 * n) + [_ANY] * nd,
        out_specs=[_SEM, _SEM] + [_HBM] * (2 * n) + [pl.BlockSpec(memory_space=pltpu.VMEM)],
        out_shape=[pltpu.SemaphoreType.DMA((n * (N_DEV - 1),)), pltpu.SemaphoreType.DMA((n * (N_DEV - 1),))]
        + [pltpu.HBM(t.shape, t.dtype) for t in srcs] + [pltpu.HBM(t.shape, t.dtype) for t in lands]
        + [jax.ShapeDtypeStruct((SUBLANES, LANES), F32)],
        input_output_aliases={i: 2 + i for i in range(2 * n)},
        compiler_params=pltpu.CompilerParams(has_side_effects=_EFFECT),
    )(*[pltpu.with_memory_space_constraint(t, pltpu.HBM) for t in list(srcs) + list(lands)], *deps)
    send_sems, recv_sems = outs[0], outs[1]
    return dict(send=send_sems, recv=recv_sems, srcs=outs[2:2 + n], lands=outs[2 + n:2 + 2 * n],
                scatter=scatter, token=outs[-1])


def _exchange_wait(started, name, after):
    n = len(started["srcs"])
    scatter = started["scatter"]

    def body(*refs):
        src_refs, land_refs = refs[:n], refs[n:2 * n]
        send_sems, recv_sems = refs[2 * n], refs[2 * n + 1]
        for cp in _exchange_copies(src_refs, land_refs, send_sems, recv_sems, scatter):
            cp.wait_send()
            cp.wait_recv()

    outs = pl.pallas_call(
        body, name=name,
        in_specs=[_HBM] * (2 * n) + [_SEM, _SEM, _ANY],
        out_specs=[_HBM] * (2 * n),
        out_shape=[pltpu.HBM(t.shape, t.dtype) for t in started["srcs"]]
        + [pltpu.HBM(t.shape, t.dtype) for t in started["lands"]],
        input_output_aliases={i: i for i in range(2 * n)},
        compiler_params=pltpu.CompilerParams(has_side_effects=_EFFECT),
    )(*started["srcs"], *started["lands"], started["send"], started["recv"], after)
    return outs[:n], outs[n:]


def _all_gather_vmem(vec, name):
    r = vec.shape[0]

    def body(v_ref, o_ref, send_sems, recv_sems):
        me, peers = _me_and_peers()
        o_ref[me] = v_ref[...]
        sends = []
        for kk, (dev, _) in enumerate(peers):
            cp = pltpu.make_async_remote_copy(
                src_ref=v_ref, dst_ref=o_ref.at[me],
                send_sem=send_sems.at[kk], recv_sem=recv_sems.at[kk],
                device_id=dev, device_id_type=MESH_ID)
            cp.start()
            sends.append(cp)
        for kk, (dev, idx) in enumerate(peers):
            pltpu.make_async_remote_copy(
                src_ref=v_ref, dst_ref=o_ref.at[idx],
                send_sem=send_sems.at[kk], recv_sem=recv_sems.at[kk],
                device_id=dev, device_id_type=MESH_ID).wait_recv()
        for cp in sends:
            cp.wait_send()

    return pl.pallas_call(
        body, name=name,
        in_specs=[pl.BlockSpec(memory_space=pltpu.VMEM)],
        out_specs=pl.BlockSpec(memory_space=pltpu.VMEM),
        out_shape=jax.ShapeDtypeStruct((N_DEV, r, LANES), F32),
        scratch_shapes=[pltpu.SemaphoreType.DMA((N_DEV - 1,)), pltpu.SemaphoreType.DMA((N_DEV - 1,))],
        compiler_params=pltpu.CompilerParams(vmem_limit_bytes=VMEM_LIMIT),
    )(vec)


def _adamw_math(w, g, m, v):
    m = ADAM_B1 * m + (1.0 - ADAM_B1) * g
    v = ADAM_B2 * v + (1.0 - ADAM_B2) * (g * g)
    m_hat = m / (1.0 - ADAM_B1 ** ADAM_STEP)
    v_hat = v / (1.0 - ADAM_B2 ** ADAM_STEP)
    delta = -ADAM_LR * (m_hat / (jnp.sqrt(v_hat) + ADAM_EPS) + ADAM_WD * w)
    return delta, m, v


def _adamw_sum(parts, w, m, v, tr, name, own=None, me=None):
    r, c = w.shape

    def body(*refs):
        if own is None:
            p_ref, w_ref, m_ref, v_ref, g_ref, d_ref, nm_ref, nv_ref = refs
            terms = [p_ref[kk] for kk in range(N_DEV)]
        else:
            me_ref, p_ref, own_ref, w_ref, m_ref, v_ref, g_ref, d_ref, nm_ref, nv_ref = refs
            terms = [jnp.where(me_ref[0] == kk, own_ref[0], p_ref[kk]) for kk in range(N_DEV)]
        g = terms[0]
        for t in terms[1:]:
            g = g + t
        g_ref[...] = g
        d_ref[...], nm_ref[...], nv_ref[...] = _adamw_math(w_ref[...], g, m_ref[...], v_ref[...])

    out_shape = [jax.ShapeDtypeStruct((r, c), F32)] * 4
    if own is None:
        blk = pl.BlockSpec((tr, c), lambda i: (i, 0))
        return pl.pallas_call(
            body, name=name, grid=(r // tr,),
            in_specs=[pl.BlockSpec((N_DEV, tr, c), lambda i: (0, i, 0)), blk, blk, blk],
            out_specs=[blk] * 4, out_shape=out_shape,
            compiler_params=_params(("parallel",)),
        )(parts, w, m, v)
    blk = pl.BlockSpec((tr, c), lambda i, me_ref: (i, 0))
    return pl.pallas_call(
        body, name=name,
        grid_spec=pltpu.PrefetchScalarGridSpec(
            num_scalar_prefetch=1, grid=(r // tr,),
            in_specs=[pl.BlockSpec((N_DEV, tr, c), lambda i, me_ref: (0, i, 0)),
                      pl.BlockSpec((1, tr, c), lambda i, me_ref: (me_ref[0], i, 0)), blk, blk, blk],
            out_specs=[blk] * 4),
        out_shape=out_shape,
        compiler_params=_params(("parallel",)),
    )(jnp.reshape(me, (1,)).astype(jnp.int32), parts, own, w, m, v)


_SMALL = ("norm1_g", "gf_b", "gb_b", "gla_norm_g", "attn_norm_g", "norm2_g", "conv_b", "final_norm_g",
          "gf_up", "gb_up", "conv_w")


def _pack(named):
    flat = jnp.concatenate([jnp.ravel(t).astype(F32) for t in named])
    tile = SUBLANES * LANES
    total = -(-flat.shape[0] // tile) * tile
    return jnp.pad(flat, (0, total - flat.shape[0])).reshape(total // LANES, LANES)


def _unpack(packed, shapes):
    flat = packed.reshape(-1)
    out, off = [], 0
    for shp in shapes:
        size = int(np.prod(shp))
        out.append(flat[off:off + size].reshape(shp))
        off += size
    return out


def kernel(x, norm1_g, w_in, gf_up, gf_b, gb_up, gb_b, gla_norm_g, attn_norm_g, w_out, norm2_g, w_gate, w_up, conv_w, conv_b, w_down, final_norm_g, loss_target, m_norm1_g, m_w_in, m_gf_up, m_gf_b, m_gb_up, m_gb_b, m_gla_norm_g, m_attn_norm_g, m_w_out, m_norm2_g, m_w_gate, m_w_up, m_conv_w, m_conv_b, m_w_down, m_final_norm_g, v_norm1_g, v_w_in, v_gf_up, v_gf_b, v_gb_up, v_gb_b, v_gla_norm_g, v_attn_norm_g, v_w_out, v_norm2_g, v_w_gate, v_w_up, v_conv_w, v_conv_b, v_w_down, v_final_norm_g):
    names = ("norm1_g", "w_in", "gf_up", "gf_b", "gb_up", "gb_b", "gla_norm_g", "attn_norm_g", "w_out", "norm2_g",
             "w_gate", "w_up", "conv_w", "conv_b", "w_down", "final_norm_g")
    ws = dict(zip(names, (norm1_g, w_in, gf_up, gf_b, gb_up, gb_b, gla_norm_g, attn_norm_g, w_out, norm2_g,
                          w_gate, w_up, conv_w, conv_b, w_down, final_norm_g)))
    ms = dict(zip(names, (m_norm1_g, m_w_in, m_gf_up, m_gf_b, m_gb_up, m_gb_b, m_gla_norm_g, m_attn_norm_g, m_w_out,
                          m_norm2_g, m_w_gate, m_w_up, m_conv_w, m_conv_b, m_w_down, m_final_norm_g)))
    vs = dict(zip(names, (v_norm1_g, v_w_in, v_gf_up, v_gf_b, v_gb_up, v_gb_b, v_gla_norm_g, v_attn_norm_g, v_w_out,
                          v_norm2_g, v_w_gate, v_w_up, v_conv_w, v_conv_b, v_w_down, v_final_norm_g)))
    me = 4 * lax.axis_index("x") + 2 * lax.axis_index("y") + lax.axis_index("c")
    big = ("w_in", "w_out", "w_gate", "w_up", "w_down")
    col_sharded = ("w_in", "w_gate", "w_up")

    def gather_start(group, name, deps=()):
        shards = [ws[n][0].astype(BF16) for n in group]
        lands = [lax.empty((N_DEV,) + t.shape, BF16) for t in shards]
        return _exchange_start(shards, lands, False, name, deps)

    def gather_finish(group, started, name, after):
        full = {}
        for n, own, t in zip(group, *_exchange_wait(started, name, after)):
            t = lax.dynamic_update_slice(t, own[None], (me, 0, 0))
            if n in col_sharded:
                full[n] = jnp.transpose(t, (1, 0, 2)).reshape(t.shape[1], N_DEV * t.shape[2])
            else:
                full[n] = t.reshape(N_DEV * t.shape[1], t.shape[2])
        return full

    started_a = gather_start(("w_in",), "gather_w_in_start")
    full = gather_finish(("w_in",), started_a, "gather_w_in_wait", started_a["token"])
    full["w_in"] = jnp.pad(full["w_in"], ((0, 0), (0, IN_PAD - IN_WIDTH)))
    late = ("w_out", "w_gate", "w_up", "w_down")
    started_b = gather_start(late, "gather_late_start", deps=(full["w_in"],))

    def late_weights(after):
        return gather_finish(late, started_b, "gather_late_wait", after)

    small_sharded = ("gf_up", "gb_up", "conv_w")
    sm = _all_gather_vmem(_pack([ws[n][0] for n in small_sharded]), "gather_small")
    shard_shapes = [ws[n][0].shape for n in small_sharded]
    per_dev = [_unpack(sm[d], shard_shapes) for d in range(N_DEV)]
    for i, n in enumerate(small_sharded):
        full[n] = jnp.concatenate([per_dev[d][i] for d in range(N_DEV)], axis=1)
    for n in ("norm1_g", "gf_b", "gb_b", "gla_norm_g", "attn_norm_g", "norm2_g", "conv_b"):
        full[n] = ws[n]
    full["final_norm_g"] = final_norm_g.reshape(1, D_MODEL)

    in_flight = []

    def grad_sink(group, grads):
        partials = []
        for n, t in zip(group, grads):
            if n == "w_in":
                t = t[:, :IN_WIDTH]
            if n in col_sharded:
                t = jnp.transpose(t.reshape(t.shape[0], N_DEV, t.shape[1] // N_DEV), (1, 0, 2))
            else:
                t = t.reshape(N_DEV, t.shape[0] // N_DEV, t.shape[1])
            partials.append(t)
        lands = [lax.empty(t.shape, F32) for t in partials]
        started = _exchange_start(partials, lands, True, "exchange_" + "_".join(group) + "_start")
        in_flight.append((group, started))
        return (started["token"],)

    loss_acc, grad_x, g = _local_step(x[0], loss_target[0], full, late_weights, grad_sink,
                                      first_dep=(started_b["token"],))

    out = {}
    for group, started in in_flight:
        sent, landed = _exchange_wait(started, "exchange_" + "_".join(group) + "_wait", grad_x)
        for n, parts, own in zip(group, landed, sent):
            out[n] = _adamw_sum(parts, ws[n][0], ms[n][0], vs[n][0], 64, "adamw_" + n, own=own, me=me)

    small_full_shapes = [g[n].shape for n in _SMALL]
    gsmall = _pack([g[n] for n in _SMALL] + [loss_acc[0:1, 0:1]])
    gathered_small = _all_gather_vmem(gsmall, "gather_small_grads")

    def full_small(d):
        parts = []
        for n in _SMALL:
            t = d[n].reshape(d[n].shape[-2:]) if d[n].ndim == 3 else d[n].reshape(1, -1)
            if n in small_sharded:
                wide = jnp.zeros((t.shape[0], t.shape[1] * N_DEV), F32)
                t = lax.dynamic_update_slice_in_dim(wide, t, me * t.shape[1], axis=1)
            parts.append(t)
        return _pack(parts + [jnp.zeros((1, 1), F32)])

    rows = gsmall.shape[0]
    res_small = _adamw_sum(gathered_small, full_small(ws), full_small(ms), full_small(vs), rows, "adamw_small")
    loss = res_small[0].reshape(-1)[sum(int(np.prod(sh)) for sh in small_full_shapes)]
    unpacked = [_unpack(t, small_full_shapes) for t in res_small]
    for i, n in enumerate(_SMALL):
        vals = [u[i] for u in unpacked]
        if n in small_sharded:
            width = vals[0].shape[1] // N_DEV
            vals = [lax.dynamic_slice_in_dim(t, me * width, width, axis=1) for t in vals]
        out[n] = vals

    result = [loss, grad_x[None]]
    for kind in range(4):
        for n in names:
            result.append(out[n][kind].reshape(ws[n].shape))
    return tuple(result)
```

```python
import functools

import numpy as np
import jax
import jax.numpy as jnp
from jax import lax
from jax.experimental import pallas as pl
from jax.experimental.pallas import tpu as pltpu

F32 = jnp.float32
BF16 = jnp.bfloat16

D_MODEL = 2048
ATTN_W = 1024
ATTN_HEADS = 8
HEAD_DIM = 128
ROPE_DIM = 32
ROPE_THETA = 500000.0
DILATIONS = (1, 4, 16)
N_SIDE = 64
GLA_KW = 512
GLA_VW = 1024
GLA_HEADS = 4
GLA_DK = 128
GLA_DV = 256
GLA_RANK = 16
GLA_GATE_NORM = 16.0
GLA_CHUNK = 64
IN_WIDTH = 6176
IN_PAD = 6272
D_FF = 5632
EPS = 1e-6
N_DEV = 8

OFF_AQ, OFF_AK, OFF_AV = 0, 1024, 2048
OFF_GQ, OFF_GK, OFF_GV, OFF_GR, OFF_Z = 3072, 3584, 4096, 5120, 6144

ADAM_LR, ADAM_B1, ADAM_B2, ADAM_EPS, ADAM_WD, ADAM_STEP = 0.001, 0.9, 0.999, 1e-08, 0.01, 10

LANES = 128
SUBLANES = 8
VMEM_LIMIT = 56 * 1024 * 1024
ROW_BLOCK = 256
ATTN_BLOCK = 128
GLA_CHUNKS_PER_STEP = 4
NEG = -1e30
MESH_ID = pl.DeviceIdType.MESH


def _params(sem):
    return pltpu.CompilerParams(dimension_semantics=sem, vmem_limit_bytes=VMEM_LIMIT)


def _dot(a, b):
    return lax.dot_general(a, b, (((1,), (0,)), ((), ())), preferred_element_type=F32)


def _dot_nt(a, b):
    return lax.dot_general(a, b, (((1,), (1,)), ((), ())), preferred_element_type=F32)


def _dot_tn(a, b):
    return lax.dot_general(a, b, (((0,), (0,)), ((), ())), preferred_element_type=F32)


def _sigmoid(x):
    return 1.0 / (1.0 + jnp.exp(-x))


def _matmul(pairs, mode, out_dtype, tm, tn, tk, name, res=None, deps=()):
    a0, b0 = pairs[0]
    if mode == "nn":
        (m, kdim), n = a0.shape, b0.shape[1]
    elif mode == "nt":
        (m, kdim), n = a0.shape, b0.shape[0]
    else:
        (kdim, m), n = a0.shape, b0.shape[1]
    assert m % tm == 0 and n % tn == 0 and kdim % tk == 0, (name, m, n, kdim)
    nk = kdim // tk
    npairs = len(pairs)
    steps = nk * npairs
    dot = {"nn": _dot, "nt": _dot_nt, "tn": _dot_tn}[mode]

    def kidx(p):
        return lambda k: jnp.clip(k - p * nk, 0, nk - 1)

    in_specs, args = [], []
    for p, (a, b) in enumerate(pairs):
        kk = kidx(p)
        if mode == "nn":
            in_specs += [pl.BlockSpec((tm, tk), lambda i, j, k, kk=kk: (i, kk(k))),
                         pl.BlockSpec((tk, tn), lambda i, j, k, kk=kk: (kk(k), j))]
        elif mode == "nt":
            in_specs += [pl.BlockSpec((tm, tk), lambda i, j, k, kk=kk: (i, kk(k))),
                         pl.BlockSpec((tn, tk), lambda i, j, k, kk=kk: (j, kk(k)))]
        else:
            in_specs += [pl.BlockSpec((tk, tm), lambda i, j, k, kk=kk: (kk(k), i)),
                         pl.BlockSpec((tk, tn), lambda i, j, k, kk=kk: (kk(k), j))]
        args += [a, b]
    if res is not None:
        in_specs.append(pl.BlockSpec((tm, tn), lambda i, j, k: (i, j)))
        args.append(res)
    in_specs += [pl.BlockSpec(memory_space=pl.ANY)] * len(deps)
    args += list(deps)

    def body(*refs):
        ab = refs[:2 * npairs]
        res_ref = refs[2 * npairs] if res is not None else None
        o_ref = refs[2 * npairs + (1 if res is not None else 0) + len(deps)]

        def finish(acc):
            if res_ref is not None:
                acc = acc + res_ref[...]
            o_ref[...] = acc.astype(out_dtype)

        if steps == 1:
            finish(dot(ab[0][...], ab[1][...]))
            return
        acc_ref = refs[-1]
        k = pl.program_id(2)

        @pl.when(k == 0)
        def _():
            acc_ref[...] = jnp.zeros_like(acc_ref)

        for p in range(npairs):
            @pl.when((k >= p * nk) & (k < (p + 1) * nk))
            def _(p=p):
                acc_ref[...] += dot(ab[2 * p][...], ab[2 * p + 1][...])

        @pl.when(k == steps - 1)
        def _():
            finish(acc_ref[...])

    return pl.pallas_call(
        body, name=name,
        grid=(m // tm, n // tn, steps),
        in_specs=in_specs,
        out_specs=pl.BlockSpec((tm, tn), lambda i, j, k: (i, j)),
        out_shape=jax.ShapeDtypeStruct((m, n), out_dtype),
        scratch_shapes=[] if steps == 1 else [pltpu.VMEM((tm, tn), F32)],
        compiler_params=_params(("parallel", "parallel", "arbitrary")),
    )(*args)


def _rms_fwd(x, g, name):
    s, d = x.shape

    def body(x_ref, g_ref, o_ref):
        xv = x_ref[...]
        r = lax.rsqrt(jnp.mean(xv * xv, axis=-1, keepdims=True) + EPS)
        o_ref[...] = (xv * r * g_ref[...]).astype(BF16)

    return pl.pallas_call(
        body, name=name, grid=(s // ROW_BLOCK,),
        in_specs=[pl.BlockSpec((ROW_BLOCK, d), lambda i: (i, 0)), pl.BlockSpec((1, d), lambda i: (0, 0))],
        out_specs=pl.BlockSpec((ROW_BLOCK, d), lambda i: (i, 0)),
        out_shape=jax.ShapeDtypeStruct((s, d), BF16),
        compiler_params=_params(("parallel",)),
    )(x, g)


def _rms_bwd(dn, x, g, dres, name):
    s, d = x.shape

    def body(dn_ref, x_ref, g_ref, dres_ref, dx_ref, dxb_ref, gg_ref):
        i = pl.program_id(0)
        xv, dnv = x_ref[...], dn_ref[...]
        r = lax.rsqrt(jnp.mean(xv * xv, axis=-1, keepdims=True) + EPS)
        dng = dnv * g_ref[...]
        c = jnp.mean(dng * xv, axis=-1, keepdims=True)
        dx = dres_ref[...] + r * dng - xv * (r * r * r * c)
        dx_ref[...] = dx
        dxb_ref[...] = dx.astype(BF16)

        @pl.when(i == 0)
        def _():
            gg_ref[...] = jnp.zeros_like(gg_ref)

        gg_ref[...] += jnp.sum(dnv * xv * r, axis=0, keepdims=True)

    row = pl.BlockSpec((ROW_BLOCK, d), lambda i: (i, 0))
    vec = pl.BlockSpec((1, d), lambda i: (0, 0))
    return pl.pallas_call(
        body, name=name, grid=(s // ROW_BLOCK,),
        in_specs=[row, row, vec, row],
        out_specs=[row, row, vec],
        out_shape=[jax.ShapeDtypeStruct((s, d), F32), jax.ShapeDtypeStruct((s, d), BF16),
                   jax.ShapeDtypeStruct((1, d), F32)],
        compiler_params=_params(("arbitrary",)),
    )(dn, x, g, dres)


def _final_loss(h2, target, g, name="final_loss"):
    s, d = h2.shape

    def body(h_ref, t_ref, g_ref, dh_ref, dhb_ref, loss_ref, gg_ref):
        i = pl.program_id(0)
        hv, gv = h_ref[...], g_ref[...]
        r = lax.rsqrt(jnp.mean(hv * hv, axis=-1, keepdims=True) + EPS)
        e = hv * r * gv - t_ref[...]
        dy = e * (1.0 / d)
        dyg = dy * gv
        c = jnp.mean(dyg * hv, axis=-1, keepdims=True)
        dh = r * dyg - hv * (r * r * r * c)
        dh_ref[...] = dh
        dhb_ref[...] = dh.astype(BF16)

        @pl.when(i == 0)
        def _():
            gg_ref[...] = jnp.zeros_like(gg_ref)
            loss_ref[...] = jnp.zeros_like(loss_ref)

        gg_ref[...] += jnp.sum(dy * hv * r, axis=0, keepdims=True)
        loss_ref[...] += jnp.sum(jnp.sum(e * e, axis=-1, keepdims=True), axis=0, keepdims=True) * (0.5 / d)

    row = pl.BlockSpec((ROW_BLOCK, d), lambda i: (i, 0))
    vec = pl.BlockSpec((1, d), lambda i: (0, 0))
    return pl.pallas_call(
        body, name=name, grid=(s // ROW_BLOCK,),
        in_specs=[row, row, vec],
        out_specs=[row, row, pl.BlockSpec((SUBLANES, LANES), lambda i: (0, 0)), vec],
        out_shape=[jax.ShapeDtypeStruct((s, d), F32), jax.ShapeDtypeStruct((s, d), BF16),
                   jax.ShapeDtypeStruct((SUBLANES, LANES), F32), jax.ShapeDtypeStruct((1, d), F32)],
        compiler_params=_params(("arbitrary",)),
    )(h2, target, g)


def _rope_tables(s):
    pos = jnp.arange(s, dtype=F32)
    inv_freq = ROPE_THETA ** (-jnp.arange(0, ROPE_DIM, 2, dtype=F32) / ROPE_DIM)
    ang = pos[:, None] * inv_freq[None, :]
    cos, sin = jnp.cos(ang), jnp.sin(ang)
    half = ROPE_DIM // 2
    rest = HEAD_DIM - ROPE_DIM
    c = jnp.concatenate([cos, cos, jnp.ones((s, rest), F32)], axis=1)
    sm = jnp.concatenate([-sin, jnp.zeros((s, half + rest), F32)], axis=1)
    sp = jnp.concatenate([jnp.zeros((s, half), F32), sin, jnp.zeros((s, rest), F32)], axis=1)
    return c, sm, sp


def _res_shape(s, groups, dil, dtype):
    return jax.ShapeDtypeStruct((s // dil, dil * groups * LANES), dtype)


def _res_spec(groups, dil):
    return pl.BlockSpec((ROW_BLOCK // dil, dil * groups * LANES), lambda i: (i, 0))


def _to_residues(scr, o_ref, dil):
    groups, rows = scr.shape[0], ROW_BLOCK // dil
    for r in range(dil):
        for h in range(groups):
            piece = scr[h] if dil == 1 else scr.at[h][pl.ds(r, rows, stride=dil), :]
            o_ref[:, (r * groups + h) * LANES:(r * groups + h + 1) * LANES] = piece.astype(o_ref.dtype)


def _from_residues(i_ref, scr, dil):
    groups, rows = scr.shape[0], ROW_BLOCK // dil
    for r in range(dil):
        for h in range(groups):
            piece = i_ref[:, (r * groups + h) * LANES:(r * groups + h + 1) * LANES].astype(F32)
            if dil == 1:
                scr[h] = piece
            else:
                scr.at[h][pl.ds(r, rows, stride=dil), :] = piece


def _rope_fwd(proj, tables, name="rope_fwd"):
    s = proj.shape[0]
    half = ROPE_DIM // 2
    nd = len(DILATIONS)

    def body(p_ref, c_ref, sm_ref, sp_ref, *rest):
        outs, scr = rest[:3 * nd], rest[3 * nd]
        c, sm, sp = c_ref[...], sm_ref[...], sp_ref[...]
        for gi, off in enumerate((OFF_AQ, OFF_AK, OFF_AV)):
            for h in range(ATTN_HEADS):
                t = p_ref[:, off + h * HEAD_DIM: off + (h + 1) * HEAD_DIM]
                if off != OFF_AV:
                    t = t * c + pltpu.roll(t, HEAD_DIM - half, 1) * sm + pltpu.roll(t, half, 1) * sp
                scr[h] = t
            for di, dil in enumerate(DILATIONS):
                _to_residues(scr, outs[3 * di + gi], dil)

    tab = pl.BlockSpec((ROW_BLOCK, HEAD_DIM), lambda i: (i, 0))
    outs = pl.pallas_call(
        body, name=name, grid=(s // ROW_BLOCK,),
        in_specs=[pl.BlockSpec((ROW_BLOCK, 3 * ATTN_W), lambda i: (i, 0)), tab, tab, tab],
        out_specs=[_res_spec(ATTN_HEADS, d) for d in DILATIONS for _ in range(3)],
        out_shape=[_res_shape(s, ATTN_HEADS, d, BF16) for d in DILATIONS for _ in range(3)],
        scratch_shapes=[pltpu.VMEM((ATTN_HEADS, ROW_BLOCK, LANES), F32)],
        compiler_params=_params(("parallel",)),
    )(proj, *tables)
    return [tuple(outs[3 * di:3 * di + 3]) for di in range(nd)]


def _rope_bwd(grads, tables, name="rope_bwd"):
    s = grads[0][0].shape[0] * DILATIONS[0]
    half = ROPE_DIM // 2
    nd = len(DILATIONS)

    def body(*refs):
        ins = refs[:3 * nd]
        c_ref, sm_ref, sp_ref, o_ref = refs[3 * nd:3 * nd + 4]
        scrs = refs[3 * nd + 4:]
        c, sm, sp = c_ref[...], sm_ref[...], sp_ref[...]
        for gi, off in enumerate((OFF_AQ, OFF_AK, OFF_AV)):
            for di, dil in enumerate(DILATIONS):
                _from_residues(ins[3 * di + gi], scrs[di], dil)
            for h in range(ATTN_HEADS):
                t = scrs[0][h]
                for scr in scrs[1:]:
                    t = t + scr[h]
                if off != OFF_AV:
                    t = t * c + pltpu.roll(t * sm, half, 1) + pltpu.roll(t * sp, HEAD_DIM - half, 1)
                o_ref[:, off + h * HEAD_DIM: off + (h + 1) * HEAD_DIM] = t.astype(BF16)

    tab = pl.BlockSpec((ROW_BLOCK, HEAD_DIM), lambda i: (i, 0))
    return pl.pallas_call(
        body, name=name, grid=(s // ROW_BLOCK,),
        in_specs=[_res_spec(ATTN_HEADS, d) for d in DILATIONS for _ in range(3)] + [tab, tab, tab],
        out_specs=pl.BlockSpec((ROW_BLOCK, 3 * ATTN_W), lambda i: (i, 0)),
        out_shape=jax.ShapeDtypeStruct((s, 3 * ATTN_W), BF16),
        scratch_shapes=[pltpu.VMEM((ATTN_HEADS, ROW_BLOCK, LANES), F32) for _ in DILATIONS],
        compiler_params=_params(("parallel",)),
    )(*[t for g in grads for t in g], *tables)


def _window_specs(nb, width):
    qb, hb = ATTN_BLOCK, N_SIDE
    cur = pl.BlockSpec((qb, width), lambda r, j: (j, r))
    prev = pl.BlockSpec((hb, width), lambda r, j: (jnp.maximum(2 * j - 1, 0), r))
    nxt = pl.BlockSpec((hb, width), lambda r, j: (jnp.minimum(2 * j + 2, 2 * nb - 1), r))
    return prev, cur, nxt


def _band_masks(j, length):
    qb, hb = ATTN_BLOCK, N_SIDE
    row = lax.broadcasted_iota(jnp.int32, (qb, qb), 0)
    col = lax.broadcasted_iota(jnp.int32, (qb, qb), 1)

    def edge_pos(i):
        return j * qb - hb + i + jnp.where(i >= hb, qb, 0)

    def ok(a, b, outside):
        return (jnp.abs(a - b) <= N_SIDE) & (outside >= 0) & (outside < length)

    cur = jnp.abs(row - col) <= N_SIDE
    edge_k = ok(j * qb + row, edge_pos(col), edge_pos(col))
    edge_q = ok(edge_pos(row), j * qb + col, edge_pos(row))
    return cur, edge_k, edge_q


def _edge(prev_ref, next_ref, sl):
    return jnp.concatenate([prev_ref[:, sl], next_ref[:, sl]], axis=0)


def _attn_fwd(q, k, v, dil, name):
    length = q.shape[0]
    qb = ATTN_BLOCK
    nb = length // qb
    scale = HEAD_DIM ** -0.5

    def body(q_ref, kp_ref, kc_ref, kn_ref, vp_ref, vc_ref, vn_ref, o_ref, lse_ref):
        valid_c, valid_e, _ = _band_masks(pl.program_id(1), length)
        lane = lax.broadcasted_iota(jnp.int32, (qb, LANES), 1)
        lse_acc = jnp.zeros((qb, LANES), F32)
        for h in range(ATTN_HEADS):
            sl = slice(h * HEAD_DIM, (h + 1) * HEAD_DIM)
            qh = q_ref[:, sl]
            s_c = jnp.where(valid_c, _dot_nt(qh, kc_ref[:, sl]) * scale, NEG)
            s_e = jnp.where(valid_e, _dot_nt(qh, _edge(kp_ref, kn_ref, sl)) * scale, NEG)
            m = jnp.max(jnp.maximum(s_c, s_e), axis=-1, keepdims=True)
            p_c, p_e = jnp.exp(s_c - m), jnp.exp(s_e - m)
            den = jnp.sum(p_c + p_e, axis=-1, keepdims=True)
            acc = _dot(p_c.astype(BF16), vc_ref[:, sl]) + _dot(p_e.astype(BF16), _edge(vp_ref, vn_ref, sl))
            o_ref[:, sl] = acc / den
            lse_acc = jnp.where(lane == h, m + jnp.log(den), lse_acc)
        lse_ref[...] = lse_acc

    prev, cur, nxt = _window_specs(nb, ATTN_W)
    return pl.pallas_call(
        body, name=name, grid=(dil, nb),
        in_specs=[cur, prev, cur, nxt, prev, cur, nxt],
        out_specs=[cur, pl.BlockSpec((qb, LANES), lambda r, j: (j, r))],
        out_shape=[jax.ShapeDtypeStruct((length, dil * ATTN_W), F32),
                   jax.ShapeDtypeStruct((length, dil * LANES), F32)],
        compiler_params=_params(("parallel", "parallel")),
    )(q, k, k, k, v, v, v)


def _attn_combine(outs, lses, g, name="attn_combine"):
    s = outs[0].shape[0] * DILATIONS[0]
    nd = len(DILATIONS)

    def body(*refs):
        o_refs, l_refs = refs[:nd], refs[nd:2 * nd]
        g_ref, o_ref, n_ref = refs[2 * nd:2 * nd + 3]
        lse_outs = refs[2 * nd + 3:3 * nd + 3]
        o_scr, l_scr = refs[3 * nd + 3:4 * nd + 3], refs[4 * nd + 3:5 * nd + 3]
        for di, dil in enumerate(DILATIONS):
            _from_residues(o_refs[di], o_scr[di], dil)
            _from_residues(l_refs[di], l_scr[di], dil)
        ls = [scr[0] for scr in l_scr]
        m = ls[0]
        for l in ls[1:]:
            m = jnp.maximum(m, l)
        es = [jnp.exp(l - m) for l in ls]
        z = es[0]
        for e in es[1:]:
            z = z + e
        ws = [e / z for e in es]
        l_scr[0][0] = m + jnp.log(z)
        for di, dil in enumerate(DILATIONS):
            _to_residues(l_scr[0], lse_outs[di], dil)
        ssq = jnp.zeros((ROW_BLOCK, 1), F32)
        for h in range(ATTN_HEADS):
            sl = slice(h * HEAD_DIM, (h + 1) * HEAD_DIM)
            acc = ws[0][:, h:h + 1] * o_scr[0][h]
            for w, scr in zip(ws[1:], o_scr[1:]):
                acc = acc + w[:, h:h + 1] * scr[h]
            o_ref[:, sl] = acc
            ssq = ssq + jnp.sum(acc * acc, axis=-1, keepdims=True)
        r = lax.rsqrt(ssq * (1.0 / ATTN_W) + EPS)
        n_ref[...] = (o_ref[...] * r * g_ref[...]).astype(BF16)

    blk = pl.BlockSpec((ROW_BLOCK, ATTN_W), lambda i: (i, 0))
    outs_ = pl.pallas_call(
        body, name=name, grid=(s // ROW_BLOCK,),
        in_specs=[_res_spec(ATTN_HEADS, d) for d in DILATIONS] + [_res_spec(1, d) for d in DILATIONS]
        + [pl.BlockSpec((1, ATTN_W), lambda i: (0, 0))],
        out_specs=[blk, blk] + [_res_spec(1, d) for d in DILATIONS],
        out_shape=[jax.ShapeDtypeStruct((s, ATTN_W), F32), jax.ShapeDtypeStruct((s, ATTN_W), BF16)]
        + [_res_shape(s, 1, d, F32) for d in DILATIONS],
        scratch_shapes=[pltpu.VMEM((ATTN_HEADS, ROW_BLOCK, LANES), F32) for _ in DILATIONS]
        + [pltpu.VMEM((1, ROW_BLOCK, LANES), F32) for _ in DILATIONS],
        compiler_params=_params(("parallel",)),
    )(*outs, *lses, g)
    return outs_[0], outs_[1], list(outs_[2:])


def _attn_prebwd(dcat, o, g, name="attn_prebwd"):
    s = o.shape[0]
    nd = len(DILATIONS)

    def body(dy_ref, o_ref, g_ref, *rest):
        do_outs, delta_outs, gg_ref = rest[:nd], rest[nd:2 * nd], rest[2 * nd]
        do_scr, delta_scr = rest[2 * nd + 1], rest[2 * nd + 2]
        i = pl.program_id(0)
        dy, ov = dy_ref[...], o_ref[...]
        r = lax.rsqrt(jnp.mean(ov * ov, axis=-1, keepdims=True) + EPS)
        dyg = dy * g_ref[...]
        c = jnp.mean(dyg * ov, axis=-1, keepdims=True)
        do = r * dyg - ov * (r * r * r * c)
        prod = do * ov
        lane = lax.broadcasted_iota(jnp.int32, (ROW_BLOCK, LANES), 1)
        acc = jnp.zeros((ROW_BLOCK, LANES), F32)
        for h in range(ATTN_HEADS):
            sl = slice(h * HEAD_DIM, (h + 1) * HEAD_DIM)
            do_scr[h] = do[:, sl]
            acc = jnp.where(lane == h, jnp.sum(prod[:, sl], axis=-1, keepdims=True), acc)
        delta_scr[0] = acc
        for di, dil in enumerate(DILATIONS):
            _to_residues(do_scr, do_outs[di], dil)
            _to_residues(delta_scr, delta_outs[di], dil)

        @pl.when(i == 0)
        def _():
            gg_ref[...] = jnp.zeros_like(gg_ref)

        gg_ref[...] += jnp.sum(dy * ov * r, axis=0, keepdims=True)

    blk = pl.BlockSpec((ROW_BLOCK, ATTN_W), lambda i: (i, 0))
    vec = pl.BlockSpec((1, ATTN_W), lambda i: (0, 0))
    outs = pl.pallas_call(
        body, name=name, grid=(s // ROW_BLOCK,),
        in_specs=[blk, blk, vec],
        out_specs=[_res_spec(ATTN_HEADS, d) for d in DILATIONS] + [_res_spec(1, d) for d in DILATIONS] + [vec],
        out_shape=[_res_shape(s, ATTN_HEADS, d, BF16) for d in DILATIONS]
        + [_res_shape(s, 1, d, F32) for d in DILATIONS] + [jax.ShapeDtypeStruct((1, ATTN_W), F32)],
        scratch_shapes=[pltpu.VMEM((ATTN_HEADS, ROW_BLOCK, LANES), F32), pltpu.VMEM((1, ROW_BLOCK, LANES), F32)],
        compiler_params=_params(("arbitrary",)),
    )(dcat, o, g)
    return list(outs[:nd]), list(outs[nd:2 * nd]), outs[2 * nd]


def _attn_bwd(q, k, v, do, lse, delta, dil, name):
    length = q.shape[0]
    qb = ATTN_BLOCK
    nb = length // qb
    scale = HEAD_DIM ** -0.5

    def body(qp, qc, qn, kp, kc, kn, vp, vc, vn, dop, doc, don, lp, lc, ln, dp, dc, dn, dq_ref, dk_ref, dv_ref):
        valid_c, valid_ek, valid_eq = _band_masks(pl.program_id(1), length)
        everything = slice(None)
        lse_e, del_e = _edge(lp, ln, everything), _edge(dp, dn, everything)
        for h in range(ATTN_HEADS):
            sl = slice(h * HEAD_DIM, (h + 1) * HEAD_DIM)
            hc = slice(h, h + 1)
            q_c, k_c, v_c, do_c = qc[:, sl], kc[:, sl], vc[:, sl], doc[:, sl]
            q_e, k_e, v_e, do_e = _edge(qp, qn, sl), _edge(kp, kn, sl), _edge(vp, vn, sl), _edge(dop, don, sl)
            lse_c, del_c = lc[:, hc], dc[:, hc]
            p = jnp.where(valid_c, jnp.exp(_dot_nt(q_c, k_c) * scale - lse_c), 0.0)
            ds = (p * (_dot_nt(do_c, v_c) - del_c)).astype(BF16)
            dq = _dot(ds, k_c)
            dk = _dot_tn(ds, q_c)
            dvh = _dot_tn(p.astype(BF16), do_c)
            p = jnp.where(valid_ek, jnp.exp(_dot_nt(q_c, k_e) * scale - lse_c), 0.0)
            ds = (p * (_dot_nt(do_c, v_e) - del_c)).astype(BF16)
            dq = dq + _dot(ds, k_e)
            p = jnp.where(valid_eq, jnp.exp(_dot_nt(q_e, k_c) * scale - lse_e[:, hc]), 0.0)
            ds = (p * (_dot_nt(do_e, v_c) - del_e[:, hc])).astype(BF16)
            dk = dk + _dot_tn(ds, q_e)
            dvh = dvh + _dot_tn(p.astype(BF16), do_e)
            dq_ref[:, sl] = dq * scale
            dk_ref[:, sl] = dk * scale
            dv_ref[:, sl] = dvh

    wide, narrow = list(_window_specs(nb, ATTN_W)), list(_window_specs(nb, LANES))
    return tuple(pl.pallas_call(
        body, name=name, grid=(dil, nb),
        in_specs=wide * 4 + narrow * 2,
        out_specs=[wide[1]] * 3,
        out_shape=[jax.ShapeDtypeStruct((length, dil * ATTN_W), F32)] * 3,
        compiler_params=_params(("parallel", "parallel")),
    )(q, q, q, k, k, k, v, v, v, do, do, do, lse, lse, lse, delta, delta, delta))


def _gate_matrices(gf_up, gb_up):
    pad = LANES - 2 * GLA_RANK
    uf = jnp.concatenate([gf_up, jnp.zeros((GLA_RANK + pad, GLA_KW), gf_up.dtype)], axis=0)
    ub = jnp.concatenate([jnp.zeros((GLA_RANK, GLA_KW), gb_up.dtype), gb_up, jnp.zeros((pad, GLA_KW), gb_up.dtype)], axis=0)
    return uf.astype(BF16), ub.astype(BF16)


def _log_sigmoid(x):
    return jnp.minimum(x, 0.0) - jnp.log(1.0 + jnp.exp(-jnp.abs(x)))


def _gla_gates(proj, uf, ub, gf_b, gb_b, name="gla_gates"):
    s = proj.shape[0]

    def body(z_ref, uf_ref, ub_ref, bf_ref, bb_ref, gf_ref, gb_ref):
        z = z_ref[...].astype(BF16)
        gf_ref[...] = _log_sigmoid(_dot(z, uf_ref[...]) + bf_ref[...]) * (1.0 / GLA_GATE_NORM)
        gb_ref[...] = _log_sigmoid(_dot(z, ub_ref[...]) + bb_ref[...]) * (1.0 / GLA_GATE_NORM)

    mat = pl.BlockSpec((LANES, GLA_KW), lambda i: (0, 0))
    vec = pl.BlockSpec((1, GLA_KW), lambda i: (0, 0))
    out = pl.BlockSpec((ROW_BLOCK, GLA_KW), lambda i: (i, 0))
    return pl.pallas_call(
        body, name=name, grid=(s // ROW_BLOCK,),
        in_specs=[pl.BlockSpec((ROW_BLOCK, LANES), lambda i: (i, OFF_Z // LANES)), mat, mat, vec, vec],
        out_specs=[out, out],
        out_shape=[jax.ShapeDtypeStruct((s, GLA_KW), F32)] * 2,
        compiler_params=_params(("parallel",)),
    )(proj, uf, ub, gf_b, gb_b)


def _gla_gates_bwd(dgf, dgb, proj, uf, ub, gf_b, gb_b, name="gla_gates_bwd"):
    s = proj.shape[0]

    def body(dgf_ref, dgb_ref, z_ref, uf_ref, ub_ref, bf_ref, bb_ref, dz_ref, guf_ref, gub_ref, gbf_ref, gbb_ref):
        i = pl.program_id(0)
        z = z_ref[...].astype(BF16)
        uf_, ub_ = uf_ref[...], ub_ref[...]
        dpf = dgf_ref[...] * (1.0 / GLA_GATE_NORM) * _sigmoid(-(_dot(z, uf_) + bf_ref[...]))
        dpb = dgb_ref[...] * (1.0 / GLA_GATE_NORM) * _sigmoid(-(_dot(z, ub_) + bb_ref[...]))
        dpf_b, dpb_b = dpf.astype(BF16), dpb.astype(BF16)
        dz_ref[...] = (_dot_nt(dpf_b, uf_) + _dot_nt(dpb_b, ub_)).astype(BF16)

        @pl.when(i == 0)
        def _():
            for r in (guf_ref, gub_ref, gbf_ref, gbb_ref):
                r[...] = jnp.zeros_like(r)

        guf_ref[...] += _dot_tn(z, dpf_b)
        gub_ref[...] += _dot_tn(z, dpb_b)
        gbf_ref[...] += jnp.sum(dpf, axis=0, keepdims=True)
        gbb_ref[...] += jnp.sum(dpb, axis=0, keepdims=True)

    mat = pl.BlockSpec((LANES, GLA_KW), lambda i: (0, 0))
    vec = pl.BlockSpec((1, GLA_KW), lambda i: (0, 0))
    blk = pl.BlockSpec((ROW_BLOCK, GLA_KW), lambda i: (i, 0))
    return pl.pallas_call(
        body, name=name, grid=(s // ROW_BLOCK,),
        in_specs=[blk, blk, pl.BlockSpec((ROW_BLOCK, LANES), lambda i: (i, OFF_Z // LANES)), mat, mat, vec, vec],
        out_specs=[pl.BlockSpec((ROW_BLOCK, LANES), lambda i: (i, 0)), mat, mat, vec, vec],
        out_shape=[jax.ShapeDtypeStruct((s, LANES), BF16), jax.ShapeDtypeStruct((LANES, GLA_KW), F32),
                   jax.ShapeDtypeStruct((LANES, GLA_KW), F32), jax.ShapeDtypeStruct((1, GLA_KW), F32),
                   jax.ShapeDtypeStruct((1, GLA_KW), F32)],
        compiler_params=_params(("arbitrary",)),
    )(dgf, dgb, proj, uf, ub, gf_b, gb_b)


def _split3(x):
    x1 = x.astype(BF16)
    r1 = x - x1.astype(F32)
    x2 = r1.astype(BF16)
    x3 = (r1 - x2.astype(F32)).astype(BF16)
    return x1, x2, x3


def _dot_exact(mask_bf, x):
    x1, x2, x3 = _split3(x)
    return _dot(mask_bf, x1) + _dot(mask_bf, x2) + _dot(mask_bf, x3)


def _chunk_terms(q_ref, k_ref, g_ref, rs, reverse):
    c = GLA_CHUNK
    row = lax.broadcasted_iota(jnp.int32, (c, c), 0)
    col = lax.broadcasted_iota(jnp.int32, (c, c), 1)
    allowed = (col >= row) if reverse else (col <= row)
    seen_by = (col <= row) if reverse else (col >= row)
    mid, last = (c // 2, 0) if reverse else (c // 2 - 1, c - 1)
    q = q_ref[rs, :] * (GLA_DK ** -0.5)
    k = k_ref[rs, :]
    b = _dot_exact(jnp.where(allowed, 1.0, 0.0).astype(BF16), g_ref[rs, :])
    bref, blast = b[mid:mid + 1, :], b[last:last + 1, :]
    e_q, e_k, e_in, e_st = jnp.exp(b - bref), jnp.exp(bref - b), jnp.exp(b), jnp.exp(blast - b)
    return dict(allowed=allowed, seen_by=seen_by, last=last, q=q, k=k, e_q=e_q, e_k=e_k, e_in=e_in, e_st=e_st,
                dec=jnp.exp(blast), qe=q * e_q, ke=k * e_k, qin=q * e_in, kst=k * e_st)


def _gla_blockspecs(s, reverse_order):
    cb = GLA_CHUNKS_PER_STEP
    rows = cb * GLA_CHUNK
    nsteps = s // rows

    def rb(n):
        return (nsteps - 1 - n) if reverse_order else n

    qspec = pl.BlockSpec((rows, GLA_DK), lambda h, n: (rb(n), OFF_GQ // GLA_DK + h))
    kspec = pl.BlockSpec((rows, GLA_DK), lambda h, n: (rb(n), OFF_GK // GLA_DK + h))
    vspec = pl.BlockSpec((rows, GLA_DV), lambda h, n: (rb(n), OFF_GV // GLA_DV + h))
    gspec = pl.BlockSpec((rows, GLA_DK), lambda h, n: (rb(n), h))
    ospec = pl.BlockSpec((rows, GLA_DV), lambda h, n: (rb(n), h))
    sspec = pl.BlockSpec((1, cb, GLA_DV, GLA_DK), lambda h, n: (h, rb(n), 0, 0))
    return cb, rows, nsteps, qspec, kspec, vspec, gspec, ospec, sspec


def _gla_fwd(proj, g, reverse, name):
    s = proj.shape[0]
    cb, rows, nsteps, qspec, kspec, vspec, gspec, ospec, sspec = _gla_blockspecs(s, reverse)

    def body(q_ref, k_ref, v_ref, g_ref, o_ref, st_ref, state):
        @pl.when(pl.program_id(1) == 0)
        def _():
            state[...] = jnp.zeros_like(state)

        for c in (reversed(range(cb)) if reverse else range(cb)):
            rs = slice(c * GLA_CHUNK, (c + 1) * GLA_CHUNK)
            t = _chunk_terms(q_ref, k_ref, g_ref, rs, reverse)
            v = v_ref[rs, :].astype(BF16)
            a = jnp.where(t["allowed"], _dot_nt(t["qe"].astype(BF16), t["ke"].astype(BF16)), 0.0)
            st = state[...]
            st_ref[0, c] = st
            o_ref[rs, :] = _dot(a.astype(BF16), v) + _dot_nt(t["qin"].astype(BF16), st.astype(BF16))
            state[...] = st * t["dec"] + _dot_tn(v, t["kst"].astype(BF16))

    return pl.pallas_call(
        body, name=name, grid=(GLA_HEADS, nsteps),
        in_specs=[qspec, kspec, vspec, gspec],
        out_specs=[ospec, sspec],
        out_shape=[jax.ShapeDtypeStruct((s, GLA_VW), F32),
                   jax.ShapeDtypeStruct((GLA_HEADS, s // GLA_CHUNK, GLA_DV, GLA_DK), F32)],
        scratch_shapes=[pltpu.VMEM((GLA_DV, GLA_DK), F32)],
        compiler_params=_params(("parallel", "arbitrary")),
    )(proj, proj, proj, g)


def _gla_bwd(proj, g, do, states, reverse, name):
    s = proj.shape[0]
    cb, rows, nsteps, qspec, kspec, vspec, gspec, ospec, sspec = _gla_blockspecs(s, not reverse)

    def body(q_ref, k_ref, v_ref, g_ref, do_ref, sp_ref, dq_ref, dk_ref, dv_ref, dg_ref, dstate):
        @pl.when(pl.program_id(1) == 0)
        def _():
            dstate[...] = jnp.zeros_like(dstate)

        for c in (range(cb) if reverse else reversed(range(cb))):
            rs = slice(c * GLA_CHUNK, (c + 1) * GLA_CHUNK)
            t = _chunk_terms(q_ref, k_ref, g_ref, rs, reverse)
            v = v_ref[rs, :].astype(BF16)
            do = do_ref[rs, :]
            qe_b, ke_b = t["qe"].astype(BF16), t["ke"].astype(BF16)
            qin_b, kst_b = t["qin"].astype(BF16), t["kst"].astype(BF16)
            a = jnp.where(t["allowed"], _dot_nt(qe_b, ke_b), 0.0)
            da = jnp.where(t["allowed"], _dot_nt(do, v), 0.0).astype(BF16)
            dqe = _dot(da, ke_b)
            dke = _dot_tn(da, qe_b)
            sp = sp_ref[0, c]
            ds = dstate[...]
            ds_b = ds.astype(BF16)
            dqin = _dot(do, sp.astype(BF16))
            dkst = _dot(v, ds_b)
            dv_ref[rs, :] = _dot_tn(a.astype(BF16), do) + _dot_nt(kst_b, ds_b)
            ddec = jnp.sum(sp * ds, axis=0, keepdims=True)
            dstate[...] = ds * t["dec"] + _dot_tn(do, qin_b)
            dq_ref[rs, :] = (dqe * t["e_q"] + dqin * t["e_in"]) * (GLA_DK ** -0.5)
            dk_ref[rs, :] = dke * t["e_k"] + dkst * t["e_st"]
            kk = dkst * t["kst"]
            db = dqe * t["qe"] - dke * t["ke"] + dqin * t["qin"] - kk
            extra = jnp.sum(kk, axis=0, keepdims=True) + ddec * t["dec"]
            rowi = lax.broadcasted_iota(jnp.int32, (GLA_CHUNK, GLA_DK), 0)
            db = db + jnp.where(rowi == t["last"], extra, 0.0)
            dg_ref[rs, :] = _dot_exact(jnp.where(t["seen_by"], 1.0, 0.0).astype(BF16), db)

    return pl.pallas_call(
        body, name=name, grid=(GLA_HEADS, nsteps),
        in_specs=[qspec, kspec, vspec, gspec, ospec, sspec],
        out_specs=[gspec, gspec, ospec, gspec],
        out_shape=[jax.ShapeDtypeStruct((s, GLA_KW), F32), jax.ShapeDtypeStruct((s, GLA_KW), F32),
                   jax.ShapeDtypeStruct((s, GLA_VW), F32), jax.ShapeDtypeStruct((s, GLA_KW), F32)],
        scratch_shapes=[pltpu.VMEM((GLA_DV, GLA_DK), F32)],
        compiler_params=_params(("parallel", "arbitrary")),
    )(proj, proj, proj, g, do, states)


def _gla_post(o_f, o_b, proj, g, name="gla_post"):
    s = o_f.shape[0]

    def body(of_ref, ob_ref, gr_ref, g_ref, o_ref):
        gv = g_ref[...]
        for h in range(GLA_HEADS):
            sl = slice(h * GLA_DV, (h + 1) * GLA_DV)
            osum = of_ref[:, sl] + ob_ref[:, sl]
            r = lax.rsqrt(jnp.mean(osum * osum, axis=-1, keepdims=True) + EPS)
            gr = gr_ref[:, sl]
            o_ref[:, sl] = (osum * r * gv * (gr * _sigmoid(gr))).astype(BF16)

    blk = pl.BlockSpec((ROW_BLOCK, GLA_VW), lambda i: (i, 0))
    return pl.pallas_call(
        body, name=name, grid=(s // ROW_BLOCK,),
        in_specs=[blk, blk, pl.BlockSpec((ROW_BLOCK, GLA_VW), lambda i: (i, OFF_GR // GLA_VW)),
                  pl.BlockSpec((1, GLA_DV), lambda i: (0, 0))],
        out_specs=blk,
        out_shape=jax.ShapeDtypeStruct((s, GLA_VW), BF16),
        compiler_params=_params(("parallel",)),
    )(o_f, o_b, proj, g)


def _gla_post_bwd(dcat, o_f, o_b, proj, g, name="gla_post_bwd"):
    s = o_f.shape[0]

    def body(dy_ref, of_ref, ob_ref, gr_ref, g_ref, do_ref, dgr_ref, gg_ref):
        i = pl.program_id(0)
        gv = g_ref[...]
        gg = jnp.zeros((1, GLA_DV), F32)
        for h in range(GLA_HEADS):
            sl = slice(h * GLA_DV, (h + 1) * GLA_DV)
            osum = of_ref[:, sl] + ob_ref[:, sl]
            r = lax.rsqrt(jnp.mean(osum * osum, axis=-1, keepdims=True) + EPS)
            gr, dy = gr_ref[:, sl], dy_ref[:, sl]
            sg = _sigmoid(gr)
            dgr_ref[:, sl] = (dy * (osum * r * gv) * (sg * (1.0 + gr * (1.0 - sg)))).astype(BF16)
            dn = dy * (gr * sg)
            dng = dn * gv
            c = jnp.mean(dng * osum, axis=-1, keepdims=True)
            do_ref[:, sl] = (r * dng - osum * (r * r * r * c)).astype(BF16)
            gg = gg + jnp.sum(dn * osum * r, axis=0, keepdims=True)

        @pl.when(i == 0)
        def _():
            gg_ref[...] = jnp.zeros_like(gg_ref)

        gg_ref[...] += gg

    blk = pl.BlockSpec((ROW_BLOCK, GLA_VW), lambda i: (i, 0))
    vec = pl.BlockSpec((1, GLA_DV), lambda i: (0, 0))
    return pl.pallas_call(
        body, name=name, grid=(s // ROW_BLOCK,),
        in_specs=[pl.BlockSpec((ROW_BLOCK, GLA_VW), lambda i: (i, 1)), blk, blk,
                  pl.BlockSpec((ROW_BLOCK, GLA_VW), lambda i: (i, OFF_GR // GLA_VW)), vec],
        out_specs=[blk, blk, vec],
        out_shape=[jax.ShapeDtypeStruct((s, GLA_VW), BF16), jax.ShapeDtypeStruct((s, GLA_VW), BF16),
                   jax.ShapeDtypeStruct((1, GLA_DV), F32)],
        compiler_params=_params(("arbitrary",)),
    )(dcat, o_f, o_b, proj, g)


def _assemble_dproj(dpa, dq_f, dq_b, dk_f, dk_b, dv_f, dv_b, dgr, dz, name="assemble_dproj"):
    s = dpa.shape[0]

    def body(dpa_ref, dqf, dqb, dkf, dkb, dvf, dvb, dgr_ref, dz_ref, o_ref):
        o_ref[:, 0:OFF_GQ] = dpa_ref[...]
        o_ref[:, OFF_GQ:OFF_GK] = (dqf[...] + dqb[...]).astype(BF16)
        o_ref[:, OFF_GK:OFF_GV] = (dkf[...] + dkb[...]).astype(BF16)
        o_ref[:, OFF_GV:OFF_GR] = (dvf[...] + dvb[...]).astype(BF16)
        o_ref[:, OFF_GR:OFF_Z] = dgr_ref[...]
        o_ref[:, OFF_Z:IN_PAD] = dz_ref[...]

    def blk(w):
        return pl.BlockSpec((ROW_BLOCK, w), lambda i: (i, 0))

    return pl.pallas_call(
        body, name=name, grid=(s // ROW_BLOCK,),
        in_specs=[blk(3 * ATTN_W)] + [blk(GLA_KW)] * 4 + [blk(GLA_VW)] * 3 + [blk(LANES)],
        out_specs=blk(IN_PAD),
        out_shape=jax.ShapeDtypeStruct((s, IN_PAD), BF16),
        compiler_params=_params(("parallel",)),
    )(dpa, dq_f, dq_b, dk_f, dk_b, dv_f, dv_b, dgr, dz)


CONV_ROWS = 256
CONV_COLS = 1408
HALO = SUBLANES


def _halo_specs(s, tr, tc, col_of):
    per = tr // HALO
    last = s // HALO - 1
    cur = pl.BlockSpec((tr, tc), lambda c, i: (i, col_of(c)))
    prev = pl.BlockSpec((HALO, tc), lambda c, i: (jnp.maximum(i * per - 1, 0), col_of(c)))
    nxt = pl.BlockSpec((HALO, tc), lambda c, i: (jnp.minimum((i + 1) * per, last), col_of(c)))
    return prev, cur, nxt


def _extended(prev_ref, cur_ref, next_ref, i, s, tr):
    x = jnp.concatenate([prev_ref[...], cur_ref[...], next_ref[...]], axis=0)
    idx = i * tr - HALO + lax.broadcasted_iota(jnp.int32, x.shape, 0)
    return jnp.where((idx >= 0) & (idx < s), x, 0.0)


def _conv_glu(gate, up, conv_w, conv_b, name="conv_glu"):
    s, f = gate.shape
    tr, tc = CONV_ROWS, CONV_COLS
    ext = tr + 2 * HALO

    def body(gp, gc, gn, up_ref, w_ref, b_ref, o_ref):
        i = pl.program_id(1)
        ge = _extended(gp, gc, gn, i, s, tr)
        w = w_ref[...]
        conv = (w[0:1] * pltpu.roll(ge, 1, 0) + w[1:2] * ge + w[2:3] * pltpu.roll(ge, ext - 1, 0))[HALO:HALO + tr]
        conv = conv + b_ref[...]
        o_ref[...] = (conv * _sigmoid(conv) * up_ref[...]).astype(BF16)

    prev, cur, nxt = _halo_specs(s, tr, tc, lambda c: c)
    return pl.pallas_call(
        body, name=name, grid=(f // tc, s // tr),
        in_specs=[prev, cur, nxt, cur, pl.BlockSpec((3, tc), lambda c, i: (0, c)), pl.BlockSpec((1, tc), lambda c, i: (0, c))],
        out_specs=cur,
        out_shape=jax.ShapeDtypeStruct((s, f), BF16),
        compiler_params=_params(("parallel", "parallel")),
    )(gate, gate, gate, up, conv_w, conv_b)


def _conv_glu_bwd(dact, gate, up, conv_w, conv_b, name="conv_glu_bwd"):
    s, f = gate.shape
    tr, tc = CONV_ROWS, CONV_COLS
    ext = tr + 2 * HALO

    def body(dp, dc, dn, gp, gc, gn, upp, upc, upn, w_ref, b_ref, dg_ref, du_ref, gw_ref, gb_ref):
        i = pl.program_id(1)
        ge = _extended(gp, gc, gn, i, s, tr)
        ue = _extended(upp, upc, upn, i, s, tr)
        de = _extended(dp, dc, dn, i, s, tr)
        w = w_ref[...]
        g_prev, g_next = pltpu.roll(ge, 1, 0), pltpu.roll(ge, ext - 1, 0)
        conv = w[0:1] * g_prev + w[1:2] * ge + w[2:3] * g_next + b_ref[...]
        sg = _sigmoid(conv)
        du_ref[...] = (de * (conv * sg))[HALO:HALO + tr].astype(BF16)
        dconv = de * ue * (sg * (1.0 + conv * (1.0 - sg)))
        dgate = w[0:1] * pltpu.roll(dconv, ext - 1, 0) + w[1:2] * dconv + w[2:3] * pltpu.roll(dconv, 1, 0)
        dg_ref[...] = dgate[HALO:HALO + tr].astype(BF16)
        inner = slice(HALO, HALO + tr)
        dci = dconv[inner]

        @pl.when(i == 0)
        def _():
            gw_ref[...] = jnp.zeros_like(gw_ref)
            gb_ref[...] = jnp.zeros_like(gb_ref)

        gw_ref[0:1, :] += jnp.sum(dci * g_prev[inner], axis=0, keepdims=True)
        gw_ref[1:2, :] += jnp.sum(dci * ge[inner], axis=0, keepdims=True)
        gw_ref[2:3, :] += jnp.sum(dci * g_next[inner], axis=0, keepdims=True)
        gb_ref[...] += jnp.sum(dci, axis=0, keepdims=True)

    prev, cur, nxt = _halo_specs(s, tr, tc, lambda c: c)
    wspec = pl.BlockSpec((3, tc), lambda c, i: (0, c))
    bspec = pl.BlockSpec((1, tc), lambda c, i: (0, c))
    return pl.pallas_call(
        body, name=name, grid=(f // tc, s // tr),
        in_specs=[prev, cur, nxt] * 3 + [wspec, bspec],
        out_specs=[cur, cur, wspec, bspec],
        out_shape=[jax.ShapeDtypeStruct((s, f), BF16), jax.ShapeDtypeStruct((s, f), BF16),
                   jax.ShapeDtypeStruct((3, f), F32), jax.ShapeDtypeStruct((1, f), F32)],
        compiler_params=_params(("parallel", "arbitrary")),
    )(dact, dact, dact, gate, gate, gate, up, up, up, conv_w, conv_b)


def _local_step(x, target, w, late_weights=None, grad_sink=None, first_dep=()):
    s = x.shape[0]
    tables = _rope_tables(s)
    uf, ub = _gate_matrices(w["gf_up"], w["gb_up"])
    if grad_sink is None:
        grad_sink = lambda names, grads: ()

    n1 = _rms_fwd(x, w["norm1_g"], "norm1")
    proj = _matmul([(n1, w["w_in"])], "nn", F32, 1024, 896, D_MODEL, "in_proj", deps=first_dep)
    qkv = _rope_fwd(proj, tables)
    branches = [_attn_fwd(*qkv[di], d, f"attn_fwd_d{d}") for di, d in enumerate(DILATIONS)]
    o_mix, ao, lse = _attn_combine([b[0] for b in branches], [b[1] for b in branches], w["attn_norm_g"])
    g_f, g_b = _gla_gates(proj, uf, ub, w["gf_b"], w["gb_b"])
    o_f, st_f = _gla_fwd(proj, g_f, False, "gla_fwd_f")
    o_b, st_b = _gla_fwd(proj, g_b, True, "gla_fwd_b")
    go = _gla_post(o_f, o_b, proj, w["gla_norm_g"])
    cat = jnp.concatenate([ao, go], axis=1)
    if late_weights is not None:
        w = {**w, **late_weights(cat)}
    h1 = _matmul([(cat, w["w_out"])], "nn", F32, 512, 1024, D_MODEL, "out_proj", res=x)
    n2 = _rms_fwd(h1, w["norm2_g"], "norm2")
    gate = _matmul([(n2, w["w_gate"])], "nn", F32, 512, 1408, D_MODEL, "ffn_gate")
    up = _matmul([(n2, w["w_up"])], "nn", F32, 512, 1408, D_MODEL, "ffn_up")
    act = _conv_glu(gate, up, w["conv_w"], w["conv_b"])
    h2 = _matmul([(act, w["w_down"])], "nn", F32, 1024, 1024, 1408, "ffn_down", res=h1)
    dh2, dh2_b, loss_acc, g_final = _final_loss(h2, target, w["final_norm_g"])

    dact = _matmul([(dh2_b, w["w_down"])], "nt", F32, 512, 1408, D_MODEL, "d_act")
    g_w_down = _matmul([(act, dh2_b)], "tn", F32, 1408, 1024, 2048, "g_w_down")
    dep = grad_sink(["w_down"], [g_w_down])
    dgate, dup, g_conv_w, g_conv_b = _conv_glu_bwd(dact, gate, up, w["conv_w"], w["conv_b"])
    g_w_gate = _matmul([(n2, dgate)], "tn", F32, 1024, 1408, 2048, "g_w_gate", deps=dep)
    g_w_up = _matmul([(n2, dup)], "tn", F32, 1024, 1408, 2048, "g_w_up")
    dep = grad_sink(["w_gate", "w_up"], [g_w_gate, g_w_up])
    dn2 = _matmul([(dgate, w["w_gate"]), (dup, w["w_up"])], "nt", F32, 1024, 1024, 1408, "d_n2", deps=dep)
    dh1, dh1_b, g_norm2 = _rms_bwd(dn2, h1, w["norm2_g"], dh2, "norm2_bwd")

    g_w_out = _matmul([(cat, dh1_b)], "tn", F32, 1024, 1024, 2048, "g_w_out")
    dep = grad_sink(["w_out"], [g_w_out])
    dcat = _matmul([(dh1_b, w["w_out"])], "nt", F32, 512, 1024, D_MODEL, "d_cat", deps=dep)
    do_attn, delta, g_attn_norm = _attn_prebwd(dcat, o_mix, w["attn_norm_g"])
    grads = [_attn_bwd(*qkv[di], do_attn[di], lse[di], delta[di], d, f"attn_bwd_d{d}")
             for di, d in enumerate(DILATIONS)]
    dpa = _rope_bwd(grads, tables)
    do_gla, dgr, g_gla_norm = _gla_post_bwd(dcat, o_f, o_b, proj, w["gla_norm_g"])
    dq_f, dk_f, dv_f, dg_f = _gla_bwd(proj, g_f, do_gla, st_f, False, "gla_bwd_f")
    dq_b, dk_b, dv_b, dg_b = _gla_bwd(proj, g_b, do_gla, st_b, True, "gla_bwd_b")
    dz, g_uf, g_ub, g_gf_b, g_gb_b = _gla_gates_bwd(dg_f, dg_b, proj, uf, ub, w["gf_b"], w["gb_b"])
    dproj = _assemble_dproj(dpa, dq_f, dq_b, dk_f, dk_b, dv_f, dv_b, dgr, dz)
    g_w_in = _matmul([(n1, dproj)], "tn", F32, 1024, 896, 2048, "g_w_in")
    dep = grad_sink(["w_in"], [g_w_in])
    dn1 = _matmul([(dproj, w["w_in"])], "nt", F32, 512, 1024, IN_PAD, "d_n1", deps=dep)
    grad_x, _, g_norm1 = _rms_bwd(dn1, x, w["norm1_g"], dh1, "norm1_bwd")

    g = dict(norm1_g=g_norm1, w_in=g_w_in, gf_up=g_uf[:GLA_RANK], gf_b=g_gf_b,
             gb_up=g_ub[GLA_RANK:2 * GLA_RANK], gb_b=g_gb_b, gla_norm_g=g_gla_norm, attn_norm_g=g_attn_norm,
             w_out=g_w_out, norm2_g=g_norm2, w_gate=g_w_gate, w_up=g_w_up, conv_w=g_conv_w, conv_b=g_conv_b,
             w_down=g_w_down, final_norm_g=g_final)
    return loss_acc, grad_x, g


def _me_and_peers():
    x, y, c = lax.axis_index("x"), lax.axis_index("y"), lax.axis_index("c")
    me = 4 * x + 2 * y + c
    peers = []
    for kbits in range(1, N_DEV):
        px, py, pc = x ^ (kbits >> 2 & 1), y ^ (kbits >> 1 & 1), c ^ (kbits & 1)
        peers.append(((px, py, pc), 4 * px + 2 * py + pc))
    return me, peers


_HBM = pl.BlockSpec(memory_space=pltpu.HBM)
_SEM = pl.BlockSpec(memory_space=pltpu.SEMAPHORE)
_ANY = pl.BlockSpec(memory_space=pl.ANY)
_EFFECT = pltpu.SideEffectType.DATAFLOW_SIDE_EFFECTING


def _exchange_copies(src_refs, land_refs, send_sems, recv_sems, scatter):
    me, peers = _me_and_peers()
    out = []
    for a, (src, land) in enumerate(zip(src_refs, land_refs)):
        for kk, (dev, idx) in enumerate(peers):
            out.append(pltpu.make_async_remote_copy(
                src_ref=src.at[idx] if scatter else src, dst_ref=land.at[me],
                send_sem=send_sems.at[a * (N_DEV - 1) + kk], recv_sem=recv_sems.at[a * (N_DEV - 1) + kk],
                device_id=dev, device_id_type=MESH_ID))
    return out


def _exchange_start(srcs, lands, scatter, name, deps=()):
    n, nd = len(srcs), len(deps)

    def body(*refs):
        src_refs, land_refs = refs[:n], refs[n:2 * n]
        send_sems, recv_sems = refs[2 * n + nd:2 * n + nd + 2]
        token = refs[-1]
        for cp in _exchange_copies(src_refs, land_refs, send_sems, recv_sems, scatter):
            cp.start()
        token[...] = jnp.zeros_like(token)

    outs = pl.pallas_call(
        body, name=name,
        in_specs=[_HBM] * (2 * n) + [_ANY] * nd,
        out_specs=[_SEM, _SEM] + [_HBM] * (2 * n) + [pl.BlockSpec(memory_space=pltpu.VMEM)],
        out_shape=[pltpu.SemaphoreType.DMA((n * (N_DEV - 1),)), pltpu.SemaphoreType.DMA((n * (N_DEV - 1),))]
        + [pltpu.HBM(t.shape, t.dtype) for t in srcs] + [pltpu.HBM(t.shape, t.dtype) for t in lands]
        + [jax.ShapeDtypeStruct((SUBLANES, LANES), F32)],
        input_output_aliases={i: 2 + i for i in range(2 * n)},
        compiler_params=pltpu.CompilerParams(has_side_effects=_EFFECT),
    )(*[pltpu.with_memory_space_constraint(t, pltpu.HBM) for t in list(srcs) + list(lands)], *deps)
    send_sems, recv_sems = outs[0], outs[1]
    return dict(send=send_sems, recv=recv_sems, srcs=outs[2:2 + n], lands=outs[2 + n:2 + 2 * n],
                scatter=scatter, token=outs[-1])


def _exchange_wait(started, name, after):
    n = len(started["srcs"])
    scatter = started["scatter"]

    def body(*refs):
        src_refs, land_refs = refs[:n], refs[n:2 * n]
        send_sems, recv_sems = refs[2 * n], refs[2 * n + 1]
        for cp in _exchange_copies(src_refs, land_refs, send_sems, recv_sems, scatter):
            cp.wait_send()
            cp.wait_recv()

    outs = pl.pallas_call(
        body, name=name,
        in_specs=[_HBM] * (2 * n) + [_SEM, _SEM, _ANY],
        out_specs=[_HBM] * (2 * n),
        out_shape=[pltpu.HBM(t.shape, t.dtype) for t in started["srcs"]]
        + [pltpu.HBM(t.shape, t.dtype) for t in started["lands"]],
        input_output_aliases={i: i for i in range(2 * n)},
        compiler_params=pltpu.CompilerParams(has_side_effects=_EFFECT),
    )(*started["srcs"], *started["lands"], started["send"], started["recv"], after)
    return outs[:n], outs[n:]


def _all_gather_two_level(shard, name):
    def body(x_ref, out_ref, send_sems, recv_sems, local_sem):
        x, y, c = lax.axis_index("x"), lax.axis_index("y"), lax.axis_index("c")
        me, sibling = (x, y, c), (x, y, 1 - c)
        chips = [(1 - x, y), (x, 1 - y), (1 - x, 1 - y)]

        def slot(px, py, pc):
            return out_ref.at[4 * px + 2 * py + pc]

        def copy(k, block, to, src=None):
            return pltpu.make_async_remote_copy(
                src_ref=slot(*block) if src is None else src, dst_ref=slot(*block),
                send_sem=send_sems.at[k], recv_sem=recv_sems.at[k], device_id=to, device_id_type=MESH_ID)

        mine = pltpu.make_async_copy(x_ref, slot(*me), local_sem)
        mine.start()
        first = [copy(0, me, sibling, src=x_ref)]
        first += [copy(1 + j, me, (*chip, c), src=x_ref) for j, chip in enumerate(chips)]
        for cp in first:
            cp.start()
        passed = [copy(4 + j, (*chip, c), sibling) for j, chip in enumerate(chips)]
        for j, chip in enumerate(chips):
            copy(1 + j, (*chip, c), me).wait_recv()
            passed[j].start()
        copy(0, sibling, me).wait_recv()
        for j, chip in enumerate(chips):
            copy(4 + j, (*chip, 1 - c), me).wait_recv()
        for cp in first + passed:
            cp.wait_send()
        mine.wait()

    return pl.pallas_call(
        body, name=name,
        in_specs=[_ANY], out_specs=_ANY,
        out_shape=jax.ShapeDtypeStruct((N_DEV,) + shard.shape, shard.dtype),
        scratch_shapes=[pltpu.SemaphoreType.DMA((N_DEV - 1,)), pltpu.SemaphoreType.DMA((N_DEV - 1,)),
                        pltpu.SemaphoreType.DMA],
    )(shard)


def _all_gather_vmem(vec, name):
    r = vec.shape[0]

    def body(v_ref, o_ref, send_sems, recv_sems):
        me, peers = _me_and_peers()
        o_ref[me] = v_ref[...]
        sends = []
        for kk, (dev, _) in enumerate(peers):
            cp = pltpu.make_async_remote_copy(
                src_ref=v_ref, dst_ref=o_ref.at[me],
                send_sem=send_sems.at[kk], recv_sem=recv_sems.at[kk],
                device_id=dev, device_id_type=MESH_ID)
            cp.start()
            sends.append(cp)
        for kk, (dev, idx) in enumerate(peers):
            pltpu.make_async_remote_copy(
                src_ref=v_ref, dst_ref=o_ref.at[idx],
                send_sem=send_sems.at[kk], recv_sem=recv_sems.at[kk],
                device_id=dev, device_id_type=MESH_ID).wait_recv()
        for cp in sends:
            cp.wait_send()

    return pl.pallas_call(
        body, name=name,
        in_specs=[pl.BlockSpec(memory_space=pltpu.VMEM)],
        out_specs=pl.BlockSpec(memory_space=pltpu.VMEM),
        out_shape=jax.ShapeDtypeStruct((N_DEV, r, LANES), F32),
        scratch_shapes=[pltpu.SemaphoreType.DMA((N_DEV - 1,)), pltpu.SemaphoreType.DMA((N_DEV - 1,))],
        compiler_params=pltpu.CompilerParams(vmem_limit_bytes=VMEM_LIMIT),
    )(vec)


def _adamw_math(w, g, m, v):
    m = ADAM_B1 * m + (1.0 - ADAM_B1) * g
    v = ADAM_B2 * v + (1.0 - ADAM_B2) * (g * g)
    m_hat = m / (1.0 - ADAM_B1 ** ADAM_STEP)
    v_hat = v / (1.0 - ADAM_B2 ** ADAM_STEP)
    delta = -ADAM_LR * (m_hat / (jnp.sqrt(v_hat) + ADAM_EPS) + ADAM_WD * w)
    return delta, m, v


def _adamw_sum(parts, w, m, v, tr, name, own=None, me=None):
    r, c = w.shape

    def body(*refs):
        if own is None:
            p_ref, w_ref, m_ref, v_ref, g_ref, d_ref, nm_ref, nv_ref = refs
            terms = [p_ref[kk] for kk in range(N_DEV)]
        else:
            me_ref, p_ref, own_ref, w_ref, m_ref, v_ref, g_ref, d_ref, nm_ref, nv_ref = refs
            terms = [jnp.where(me_ref[0] == kk, own_ref[0], p_ref[kk]).astype(F32) for kk in range(N_DEV)]
        g = terms[0]
        for t in terms[1:]:
            g = g + t
        g_ref[...] = g
        d_ref[...], nm_ref[...], nv_ref[...] = _adamw_math(w_ref[...], g, m_ref[...], v_ref[...])

    out_shape = [jax.ShapeDtypeStruct((r, c), F32)] * 4
    if own is None:
        blk = pl.BlockSpec((tr, c), lambda i: (i, 0))
        return pl.pallas_call(
            body, name=name, grid=(r // tr,),
            in_specs=[pl.BlockSpec((N_DEV, tr, c), lambda i: (0, i, 0)), blk, blk, blk],
            out_specs=[blk] * 4, out_shape=out_shape,
            compiler_params=_params(("parallel",)),
        )(parts, w, m, v)
    blk = pl.BlockSpec((tr, c), lambda i, me_ref: (i, 0))
    return pl.pallas_call(
        body, name=name,
        grid_spec=pltpu.PrefetchScalarGridSpec(
            num_scalar_prefetch=1, grid=(r // tr,),
            in_specs=[pl.BlockSpec((N_DEV, tr, c), lambda i, me_ref: (0, i, 0)),
                      pl.BlockSpec((1, tr, c), lambda i, me_ref: (me_ref[0], i, 0)), blk, blk, blk],
            out_specs=[blk] * 4),
        out_shape=out_shape,
        compiler_params=_params(("parallel",)),
    )(jnp.reshape(me, (1,)).astype(jnp.int32), parts, own, w, m, v)


_SMALL = ("norm1_g", "gf_b", "gb_b", "gla_norm_g", "attn_norm_g", "norm2_g", "conv_b", "final_norm_g",
          "gf_up", "gb_up", "conv_w")


def _pack(named):
    flat = jnp.concatenate([jnp.ravel(t).astype(F32) for t in named])
    tile = SUBLANES * LANES
    total = -(-flat.shape[0] // tile) * tile
    return jnp.pad(flat, (0, total - flat.shape[0])).reshape(total // LANES, LANES)


def _unpack(packed, shapes):
    flat = packed.reshape(-1)
    out, off = [], 0
    for shp in shapes:
        size = int(np.prod(shp))
        out.append(flat[off:off + size].reshape(shp))
        off += size
    return out


def kernel(x, norm1_g, w_in, gf_up, gf_b, gb_up, gb_b, gla_norm_g, attn_norm_g, w_out, norm2_g, w_gate, w_up, conv_w, conv_b, w_down, final_norm_g, loss_target, m_norm1_g, m_w_in, m_gf_up, m_gf_b, m_gb_up, m_gb_b, m_gla_norm_g, m_attn_norm_g, m_w_out, m_norm2_g, m_w_gate, m_w_up, m_conv_w, m_conv_b, m_w_down, m_final_norm_g, v_norm1_g, v_w_in, v_gf_up, v_gf_b, v_gb_up, v_gb_b, v_gla_norm_g, v_attn_norm_g, v_w_out, v_norm2_g, v_w_gate, v_w_up, v_conv_w, v_conv_b, v_w_down, v_final_norm_g):
    names = ("norm1_g", "w_in", "gf_up", "gf_b", "gb_up", "gb_b", "gla_norm_g", "attn_norm_g", "w_out", "norm2_g",
             "w_gate", "w_up", "conv_w", "conv_b", "w_down", "final_norm_g")
    ws = dict(zip(names, (norm1_g, w_in, gf_up, gf_b, gb_up, gb_b, gla_norm_g, attn_norm_g, w_out, norm2_g,
                          w_gate, w_up, conv_w, conv_b, w_down, final_norm_g)))
    ms = dict(zip(names, (m_norm1_g, m_w_in, m_gf_up, m_gf_b, m_gb_up, m_gb_b, m_gla_norm_g, m_attn_norm_g, m_w_out,
                          m_norm2_g, m_w_gate, m_w_up, m_conv_w, m_conv_b, m_w_down, m_final_norm_g)))
    vs = dict(zip(names, (v_norm1_g, v_w_in, v_gf_up, v_gf_b, v_gb_up, v_gb_b, v_gla_norm_g, v_attn_norm_g, v_w_out,
                          v_norm2_g, v_w_gate, v_w_up, v_conv_w, v_conv_b, v_w_down, v_final_norm_g)))
    me = 4 * lax.axis_index("x") + 2 * lax.axis_index("y") + lax.axis_index("c")
    big = ("w_in", "w_out", "w_gate", "w_up", "w_down")
    col_sharded = ("w_in", "w_gate", "w_up")

    def gather_start(group, name, deps=()):
        shards = [ws[n][0].astype(BF16) for n in group]
        lands = [lax.empty((N_DEV,) + t.shape, BF16) for t in shards]
        return _exchange_start(shards, lands, False, name, deps)

    def gather_finish(group, started, name, after):
        full = {}
        for n, own, t in zip(group, *_exchange_wait(started, name, after)):
            t = lax.dynamic_update_slice(t, own[None], (me, 0, 0))
            if n in col_sharded:
                full[n] = jnp.transpose(t, (1, 0, 2)).reshape(t.shape[1], N_DEV * t.shape[2])
            else:
                full[n] = t.reshape(N_DEV * t.shape[1], t.shape[2])
        return full

    w_in_all = _all_gather_two_level(ws["w_in"][0].astype(BF16), "gather_w_in")
    full = {"w_in": jnp.pad(jnp.transpose(w_in_all, (1, 0, 2)).reshape(D_MODEL, IN_WIDTH),
                            ((0, 0), (0, IN_PAD - IN_WIDTH)))}
    late = ("w_out", "w_gate", "w_up", "w_down")
    started_b = gather_start(late, "gather_late_start", deps=(full["w_in"],))

    def late_weights(after):
        return gather_finish(late, started_b, "gather_late_wait", after)

    small_sharded = ("gf_up", "gb_up", "conv_w")
    sm = _all_gather_vmem(_pack([ws[n][0] for n in small_sharded]), "gather_small")
    shard_shapes = [ws[n][0].shape for n in small_sharded]
    per_dev = [_unpack(sm[d], shard_shapes) for d in range(N_DEV)]
    for i, n in enumerate(small_sharded):
        full[n] = jnp.concatenate([per_dev[d][i] for d in range(N_DEV)], axis=1)
    for n in ("norm1_g", "gf_b", "gb_b", "gla_norm_g", "attn_norm_g", "norm2_g", "conv_b"):
        full[n] = ws[n]
    full["final_norm_g"] = final_norm_g.reshape(1, D_MODEL)

    in_flight = []

    def grad_sink(group, grads):
        partials = []
        for n, t in zip(group, grads):
            if n == "w_in":
                t = t[:, :IN_WIDTH].astype(BF16)
            if n in col_sharded:
                t = jnp.transpose(t.reshape(t.shape[0], N_DEV, t.shape[1] // N_DEV), (1, 0, 2))
            else:
                t = t.reshape(N_DEV, t.shape[0] // N_DEV, t.shape[1])
            partials.append(t)
        lands = [lax.empty(t.shape, t.dtype) for t in partials]
        started = _exchange_start(partials, lands, True, "exchange_" + "_".join(group) + "_start")
        in_flight.append((group, started))
        return (started["token"],)

    loss_acc, grad_x, g = _local_step(x[0], loss_target[0], full, late_weights, grad_sink,
                                      first_dep=(started_b["token"],))

    out = {}
    for group, started in in_flight:
        sent, landed = _exchange_wait(started, "exchange_" + "_".join(group) + "_wait", grad_x)
        for n, parts, own in zip(group, landed, sent):
            out[n] = _adamw_sum(parts, ws[n][0], ms[n][0], vs[n][0], 64, "adamw_" + n, own=own, me=me)

    small_full_shapes = [g[n].shape for n in _SMALL]
    gsmall = _pack([g[n] for n in _SMALL] + [loss_acc[0:1, 0:1]])
    gathered_small = _all_gather_vmem(gsmall, "gather_small_grads")

    def full_small(d):
        parts = []
        for n in _SMALL:
            t = d[n].reshape(d[n].shape[-2:]) if d[n].ndim == 3 else d[n].reshape(1, -1)
            if n in small_sharded:
                wide = jnp.zeros((t.shape[0], t.shape[1] * N_DEV), F32)
                t = lax.dynamic_update_slice_in_dim(wide, t, me * t.shape[1], axis=1)
            parts.append(t)
        return _pack(parts + [jnp.zeros((1, 1), F32)])

    rows = gsmall.shape[0]
    res_small = _adamw_sum(gathered_small, full_small(ws), full_small(ms), full_small(vs), rows, "adamw_small")
    loss = res_small[0].reshape(-1)[sum(int(np.prod(sh)) for sh in small_full_shapes)]
    unpacked = [_unpack(t, small_full_shapes) for t in res_small]
    for i, n in enumerate(_SMALL):
        vals = [u[i] for u in unpacked]
        if n in small_sharded:
            width = vals[0].shape[1] // N_DEV
            vals = [lax.dynamic_slice_in_dim(t, me * width, width, axis=1) for t in vals]
        out[n] = vals

    result = [loss, grad_x[None]]
    for kind in range(4):
        for n in names:
            result.append(out[n][kind].reshape(ws[n].shape))
    return tuple(result)
```

```python
import functools

import numpy as np
import jax
import jax.numpy as jnp
from jax import lax
from jax.experimental import pallas as pl
from jax.experimental.pallas import tpu as pltpu

F32 = jnp.float32
BF16 = jnp.bfloat16

D_MODEL = 2048
ATTN_W = 1024
ATTN_HEADS = 8
HEAD_DIM = 128
ROPE_DIM = 32
ROPE_THETA = 500000.0
DILATIONS = (1, 4, 16)
N_SIDE = 64
GLA_KW = 512
GLA_VW = 1024
GLA_HEADS = 4
GLA_DK = 128
GLA_DV = 256
GLA_RANK = 16
GLA_GATE_NORM = 16.0
GLA_CHUNK = 64
IN_WIDTH = 6176
IN_PAD = 6272
D_FF = 5632
EPS = 1e-6
N_DEV = 8

OFF_AQ, OFF_AK, OFF_AV = 0, 1024, 2048
OFF_GQ, OFF_GK, OFF_GV, OFF_GR, OFF_Z = 3072, 3584, 4096, 5120, 6144

ADAM_LR, ADAM_B1, ADAM_B2, ADAM_EPS, ADAM_WD, ADAM_STEP = 0.001, 0.9, 0.999, 1e-08, 0.01, 10

LANES = 128
SUBLANES = 8
VMEM_LIMIT = 56 * 1024 * 1024
ROW_BLOCK = 256
ATTN_BLOCK = 128
GLA_CHUNKS_PER_STEP = 4
NEG = -1e30
MESH_ID = pl.DeviceIdType.MESH


def _params(sem):
    return pltpu.CompilerParams(dimension_semantics=sem, vmem_limit_bytes=VMEM_LIMIT)


def _dot(a, b):
    return lax.dot_general(a, b, (((1,), (0,)), ((), ())), preferred_element_type=F32)


def _dot_nt(a, b):
    return lax.dot_general(a, b, (((1,), (1,)), ((), ())), preferred_element_type=F32)


def _dot_tn(a, b):
    return lax.dot_general(a, b, (((0,), (0,)), ((), ())), preferred_element_type=F32)


def _sigmoid(x):
    return 1.0 / (1.0 + jnp.exp(-x))


def _matmul(pairs, mode, out_dtype, tm, tn, tk, name, res=None, deps=()):
    a0, b0 = pairs[0]
    if mode == "nn":
        (m, kdim), n = a0.shape, b0.shape[1]
    elif mode == "nt":
        (m, kdim), n = a0.shape, b0.shape[0]
    else:
        (kdim, m), n = a0.shape, b0.shape[1]
    assert m % tm == 0 and n % tn == 0 and kdim % tk == 0, (name, m, n, kdim)
    nk = kdim // tk
    npairs = len(pairs)
    steps = nk * npairs
    dot = {"nn": _dot, "nt": _dot_nt, "tn": _dot_tn}[mode]

    def kidx(p):
        return lambda k: jnp.clip(k - p * nk, 0, nk - 1)

    in_specs, args = [], []
    for p, (a, b) in enumerate(pairs):
        kk = kidx(p)
        if mode == "nn":
            in_specs += [pl.BlockSpec((tm, tk), lambda i, j, k, kk=kk: (i, kk(k))),
                         pl.BlockSpec((tk, tn), lambda i, j, k, kk=kk: (kk(k), j))]
        elif mode == "nt":
            in_specs += [pl.BlockSpec((tm, tk), lambda i, j, k, kk=kk: (i, kk(k))),
                         pl.BlockSpec((tn, tk), lambda i, j, k, kk=kk: (j, kk(k)))]
        else:
            in_specs += [pl.BlockSpec((tk, tm), lambda i, j, k, kk=kk: (kk(k), i)),
                         pl.BlockSpec((tk, tn), lambda i, j, k, kk=kk: (kk(k), j))]
        args += [a, b]
    if res is not None:
        in_specs.append(pl.BlockSpec((tm, tn), lambda i, j, k: (i, j)))
        args.append(res)
    in_specs += [pl.BlockSpec(memory_space=pl.ANY)] * len(deps)
    args += list(deps)

    def body(*refs):
        ab = refs[:2 * npairs]
        res_ref = refs[2 * npairs] if res is not None else None
        o_ref = refs[2 * npairs + (1 if res is not None else 0) + len(deps)]

        def finish(acc):
            if res_ref is not None:
                acc = acc + res_ref[...]
            o_ref[...] = acc.astype(out_dtype)

        if steps == 1:
            finish(dot(ab[0][...], ab[1][...]))
            return
        acc_ref = refs[-1]
        k = pl.program_id(2)

        @pl.when(k == 0)
        def _():
            acc_ref[...] = jnp.zeros_like(acc_ref)

        for p in range(npairs):
            @pl.when((k >= p * nk) & (k < (p + 1) * nk))
            def _(p=p):
                acc_ref[...] += dot(ab[2 * p][...], ab[2 * p + 1][...])

        @pl.when(k == steps - 1)
        def _():
            finish(acc_ref[...])

    return pl.pallas_call(
        body, name=name,
        grid=(m // tm, n // tn, steps),
        in_specs=in_specs,
        out_specs=pl.BlockSpec((tm, tn), lambda i, j, k: (i, j)),
        out_shape=jax.ShapeDtypeStruct((m, n), out_dtype),
        scratch_shapes=[] if steps == 1 else [pltpu.VMEM((tm, tn), F32)],
        compiler_params=_params(("parallel", "parallel", "arbitrary")),
    )(*args)


def _rms_fwd(x, g, name):
    s, d = x.shape

    def body(x_ref, g_ref, o_ref):
        xv = x_ref[...]
        r = lax.rsqrt(jnp.mean(xv * xv, axis=-1, keepdims=True) + EPS)
        o_ref[...] = (xv * r * g_ref[...]).astype(BF16)

    return pl.pallas_call(
        body, name=name, grid=(s // ROW_BLOCK,),
        in_specs=[pl.BlockSpec((ROW_BLOCK, d), lambda i: (i, 0)), pl.BlockSpec((1, d), lambda i: (0, 0))],
        out_specs=pl.BlockSpec((ROW_BLOCK, d), lambda i: (i, 0)),
        out_shape=jax.ShapeDtypeStruct((s, d), BF16),
        compiler_params=_params(("parallel",)),
    )(x, g)


def _rms_bwd(dn, x, g, dres, name):
    s, d = x.shape

    def body(dn_ref, x_ref, g_ref, dres_ref, dx_ref, dxb_ref, gg_ref):
        i = pl.program_id(0)
        xv, dnv = x_ref[...], dn_ref[...]
        r = lax.rsqrt(jnp.mean(xv * xv, axis=-1, keepdims=True) + EPS)
        dng = dnv * g_ref[...]
        c = jnp.mean(dng * xv, axis=-1, keepdims=True)
        dx = dres_ref[...] + r * dng - xv * (r * r * r * c)
        dx_ref[...] = dx
        dxb_ref[...] = dx.astype(BF16)

        @pl.when(i == 0)
        def _():
            gg_ref[...] = jnp.zeros_like(gg_ref)

        gg_ref[...] += jnp.sum(dnv * xv * r, axis=0, keepdims=True)

    row = pl.BlockSpec((ROW_BLOCK, d), lambda i: (i, 0))
    vec = pl.BlockSpec((1, d), lambda i: (0, 0))
    return pl.pallas_call(
        body, name=name, grid=(s // ROW_BLOCK,),
        in_specs=[row, row, vec, row],
        out_specs=[row, row, vec],
        out_shape=[jax.ShapeDtypeStruct((s, d), F32), jax.ShapeDtypeStruct((s, d), BF16),
                   jax.ShapeDtypeStruct((1, d), F32)],
        compiler_params=_params(("arbitrary",)),
    )(dn, x, g, dres)


def _final_loss(h2, target, g, name="final_loss"):
    s, d = h2.shape

    def body(h_ref, t_ref, g_ref, dh_ref, dhb_ref, loss_ref, gg_ref):
        i = pl.program_id(0)
        hv, gv = h_ref[...], g_ref[...]
        r = lax.rsqrt(jnp.mean(hv * hv, axis=-1, keepdims=True) + EPS)
        e = hv * r * gv - t_ref[...]
        dy = e * (1.0 / d)
        dyg = dy * gv
        c = jnp.mean(dyg * hv, axis=-1, keepdims=True)
        dh = r * dyg - hv * (r * r * r * c)
        dh_ref[...] = dh
        dhb_ref[...] = dh.astype(BF16)

        @pl.when(i == 0)
        def _():
            gg_ref[...] = jnp.zeros_like(gg_ref)
            loss_ref[...] = jnp.zeros_like(loss_ref)

        gg_ref[...] += jnp.sum(dy * hv * r, axis=0, keepdims=True)
        loss_ref[...] += jnp.sum(jnp.sum(e * e, axis=-1, keepdims=True), axis=0, keepdims=True) * (0.5 / d)

    row = pl.BlockSpec((ROW_BLOCK, d), lambda i: (i, 0))
    vec = pl.BlockSpec((1, d), lambda i: (0, 0))
    return pl.pallas_call(
        body, name=name, grid=(s // ROW_BLOCK,),
        in_specs=[row, row, vec],
        out_specs=[row, row, pl.BlockSpec((SUBLANES, LANES), lambda i: (0, 0)), vec],
        out_shape=[jax.ShapeDtypeStruct((s, d), F32), jax.ShapeDtypeStruct((s, d), BF16),
                   jax.ShapeDtypeStruct((SUBLANES, LANES), F32), jax.ShapeDtypeStruct((1, d), F32)],
        compiler_params=_params(("arbitrary",)),
    )(h2, target, g)


def _rope_tables(s):
    pos = jnp.arange(s, dtype=F32)
    inv_freq = ROPE_THETA ** (-jnp.arange(0, ROPE_DIM, 2, dtype=F32) / ROPE_DIM)
    ang = pos[:, None] * inv_freq[None, :]
    cos, sin = jnp.cos(ang), jnp.sin(ang)
    half = ROPE_DIM // 2
    rest = HEAD_DIM - ROPE_DIM
    c = jnp.concatenate([cos, cos, jnp.ones((s, rest), F32)], axis=1)
    sm = jnp.concatenate([-sin, jnp.zeros((s, half + rest), F32)], axis=1)
    sp = jnp.concatenate([jnp.zeros((s, half), F32), sin, jnp.zeros((s, rest), F32)], axis=1)
    return c, sm, sp


def _res_shape(s, groups, dil, dtype):
    return jax.ShapeDtypeStruct((s // dil, dil * groups * LANES), dtype)


def _res_spec(groups, dil):
    return pl.BlockSpec((ROW_BLOCK // dil, dil * groups * LANES), lambda i: (i, 0))


def _to_residues(scr, o_ref, dil):
    groups, rows = scr.shape[0], ROW_BLOCK // dil
    for r in range(dil):
        for h in range(groups):
            piece = scr[h] if dil == 1 else scr.at[h][pl.ds(r, rows, stride=dil), :]
            o_ref[:, (r * groups + h) * LANES:(r * groups + h + 1) * LANES] = piece.astype(o_ref.dtype)


def _from_residues(i_ref, scr, dil):
    groups, rows = scr.shape[0], ROW_BLOCK // dil
    for r in range(dil):
        for h in range(groups):
            piece = i_ref[:, (r * groups + h) * LANES:(r * groups + h + 1) * LANES].astype(F32)
            if dil == 1:
                scr[h] = piece
            else:
                scr.at[h][pl.ds(r, rows, stride=dil), :] = piece


def _rope_fwd(proj, tables, name="rope_fwd"):
    s = proj.shape[0]
    half = ROPE_DIM // 2
    nd = len(DILATIONS)

    def body(p_ref, c_ref, sm_ref, sp_ref, *rest):
        outs, scr = rest[:3 * nd], rest[3 * nd]
        c, sm, sp = c_ref[...], sm_ref[...], sp_ref[...]
        for gi, off in enumerate((OFF_AQ, OFF_AK, OFF_AV)):
            for h in range(ATTN_HEADS):
                t = p_ref[:, off + h * HEAD_DIM: off + (h + 1) * HEAD_DIM]
                if off != OFF_AV:
                    t = t * c + pltpu.roll(t, HEAD_DIM - half, 1) * sm + pltpu.roll(t, half, 1) * sp
                scr[h] = t
            for di, dil in enumerate(DILATIONS):
                _to_residues(scr, outs[3 * di + gi], dil)

    tab = pl.BlockSpec((ROW_BLOCK, HEAD_DIM), lambda i: (i, 0))
    outs = pl.pallas_call(
        body, name=name, grid=(s // ROW_BLOCK,),
        in_specs=[pl.BlockSpec((ROW_BLOCK, 3 * ATTN_W), lambda i: (i, 0)), tab, tab, tab],
        out_specs=[_res_spec(ATTN_HEADS, d) for d in DILATIONS for _ in range(3)],
        out_shape=[_res_shape(s, ATTN_HEADS, d, BF16) for d in DILATIONS for _ in range(3)],
        scratch_shapes=[pltpu.VMEM((ATTN_HEADS, ROW_BLOCK, LANES), F32)],
        compiler_params=_params(("parallel",)),
    )(proj, *tables)
    return [tuple(outs[3 * di:3 * di + 3]) for di in range(nd)]


def _rope_bwd(grads, tables, name="rope_bwd"):
    s = grads[0][0].shape[0] * DILATIONS[0]
    half = ROPE_DIM // 2
    nd = len(DILATIONS)

    def body(*refs):
        ins = refs[:3 * nd]
        c_ref, sm_ref, sp_ref, o_ref = refs[3 * nd:3 * nd + 4]
        scrs = refs[3 * nd + 4:]
        c, sm, sp = c_ref[...], sm_ref[...], sp_ref[...]
        for gi, off in enumerate((OFF_AQ, OFF_AK, OFF_AV)):
            for di, dil in enumerate(DILATIONS):
                _from_residues(ins[3 * di + gi], scrs[di], dil)
            for h in range(ATTN_HEADS):
                t = scrs[0][h]
                for scr in scrs[1:]:
                    t = t + scr[h]
                if off != OFF_AV:
                    t = t * c + pltpu.roll(t * sm, half, 1) + pltpu.roll(t * sp, HEAD_DIM - half, 1)
                o_ref[:, off + h * HEAD_DIM: off + (h + 1) * HEAD_DIM] = t.astype(BF16)

    tab = pl.BlockSpec((ROW_BLOCK, HEAD_DIM), lambda i: (i, 0))
    return pl.pallas_call(
        body, name=name, grid=(s // ROW_BLOCK,),
        in_specs=[_res_spec(ATTN_HEADS, d) for d in DILATIONS for _ in range(3)] + [tab, tab, tab],
        out_specs=pl.BlockSpec((ROW_BLOCK, 3 * ATTN_W), lambda i: (i, 0)),
        out_shape=jax.ShapeDtypeStruct((s, 3 * ATTN_W), BF16),
        scratch_shapes=[pltpu.VMEM((ATTN_HEADS, ROW_BLOCK, LANES), F32) for _ in DILATIONS],
        compiler_params=_params(("parallel",)),
    )(*[t for g in grads for t in g], *tables)


def _window_specs(nb, width):
    qb, hb = ATTN_BLOCK, N_SIDE
    cur = pl.BlockSpec((qb, width), lambda r, j: (j, r))
    prev = pl.BlockSpec((hb, width), lambda r, j: (jnp.maximum(2 * j - 1, 0), r))
    nxt = pl.BlockSpec((hb, width), lambda r, j: (jnp.minimum(2 * j + 2, 2 * nb - 1), r))
    return prev, cur, nxt


def _band_masks(j, length):
    qb, hb = ATTN_BLOCK, N_SIDE
    row = lax.broadcasted_iota(jnp.int32, (qb, qb), 0)
    col = lax.broadcasted_iota(jnp.int32, (qb, qb), 1)

    def edge_pos(i):
        return j * qb - hb + i + jnp.where(i >= hb, qb, 0)

    def ok(a, b, outside):
        return (jnp.abs(a - b) <= N_SIDE) & (outside >= 0) & (outside < length)

    cur = jnp.abs(row - col) <= N_SIDE
    edge_k = ok(j * qb + row, edge_pos(col), edge_pos(col))
    edge_q = ok(edge_pos(row), j * qb + col, edge_pos(row))
    return cur, edge_k, edge_q


def _edge(prev_ref, next_ref, sl):
    return jnp.concatenate([prev_ref[:, sl], next_ref[:, sl]], axis=0)


def _attn_fwd(q, k, v, dil, name):
    length = q.shape[0]
    qb = ATTN_BLOCK
    nb = length // qb
    scale = HEAD_DIM ** -0.5

    def body(q_ref, kp_ref, kc_ref, kn_ref, vp_ref, vc_ref, vn_ref, o_ref, lse_ref):
        valid_c, valid_e, _ = _band_masks(pl.program_id(1), length)
        lane = lax.broadcasted_iota(jnp.int32, (qb, LANES), 1)
        lse_acc = jnp.zeros((qb, LANES), F32)
        heads = [slice(h * HEAD_DIM, (h + 1) * HEAD_DIM) for h in range(ATTN_HEADS)]
        scores = [(_dot_nt(q_ref[:, sl], kc_ref[:, sl]), _dot_nt(q_ref[:, sl], _edge(kp_ref, kn_ref, sl)))
                  for sl in heads]
        probs = []
        for h, (s_c, s_e) in enumerate(scores):
            s_c = jnp.where(valid_c, s_c * scale, NEG)
            s_e = jnp.where(valid_e, s_e * scale, NEG)
            m = jnp.max(jnp.maximum(s_c, s_e), axis=-1, keepdims=True)
            p_c, p_e = jnp.exp(s_c - m), jnp.exp(s_e - m)
            den = jnp.sum(p_c + p_e, axis=-1, keepdims=True)
            probs.append((p_c.astype(BF16), p_e.astype(BF16), 1.0 / den))
            lse_acc = jnp.where(lane == h, m + jnp.log(den), lse_acc)
        for sl, (p_c, p_e, inv) in zip(heads, probs):
            o_ref[:, sl] = (_dot(p_c, vc_ref[:, sl]) + _dot(p_e, _edge(vp_ref, vn_ref, sl))) * inv
        lse_ref[...] = lse_acc

    prev, cur, nxt = _window_specs(nb, ATTN_W)
    return pl.pallas_call(
        body, name=name, grid=(dil, nb),
        in_specs=[cur, prev, cur, nxt, prev, cur, nxt],
        out_specs=[cur, pl.BlockSpec((qb, LANES), lambda r, j: (j, r))],
        out_shape=[jax.ShapeDtypeStruct((length, dil * ATTN_W), F32),
                   jax.ShapeDtypeStruct((length, dil * LANES), F32)],
        compiler_params=_params(("parallel", "parallel")),
    )(q, k, k, k, v, v, v)


def _attn_combine(outs, lses, g, name="attn_combine"):
    s = outs[0].shape[0] * DILATIONS[0]
    nd = len(DILATIONS)

    def body(*refs):
        o_refs, l_refs = refs[:nd], refs[nd:2 * nd]
        g_ref, o_ref, n_ref = refs[2 * nd:2 * nd + 3]
        lse_outs = refs[2 * nd + 3:3 * nd + 3]
        o_scr, l_scr = refs[3 * nd + 3:4 * nd + 3], refs[4 * nd + 3:5 * nd + 3]
        for di, dil in enumerate(DILATIONS):
            _from_residues(o_refs[di], o_scr[di], dil)
            _from_residues(l_refs[di], l_scr[di], dil)
        ls = [scr[0] for scr in l_scr]
        m = ls[0]
        for l in ls[1:]:
            m = jnp.maximum(m, l)
        es = [jnp.exp(l - m) for l in ls]
        z = es[0]
        for e in es[1:]:
            z = z + e
        ws = [e / z for e in es]
        l_scr[0][0] = m + jnp.log(z)
        for di, dil in enumerate(DILATIONS):
            _to_residues(l_scr[0], lse_outs[di], dil)
        ssq = jnp.zeros((ROW_BLOCK, 1), F32)
        for h in range(ATTN_HEADS):
            sl = slice(h * HEAD_DIM, (h + 1) * HEAD_DIM)
            acc = ws[0][:, h:h + 1] * o_scr[0][h]
            for w, scr in zip(ws[1:], o_scr[1:]):
                acc = acc + w[:, h:h + 1] * scr[h]
            o_ref[:, sl] = acc
            ssq = ssq + jnp.sum(acc * acc, axis=-1, keepdims=True)
        r = lax.rsqrt(ssq * (1.0 / ATTN_W) + EPS)
        n_ref[...] = (o_ref[...] * r * g_ref[...]).astype(BF16)

    blk = pl.BlockSpec((ROW_BLOCK, ATTN_W), lambda i: (i, 0))
    outs_ = pl.pallas_call(
        body, name=name, grid=(s // ROW_BLOCK,),
        in_specs=[_res_spec(ATTN_HEADS, d) for d in DILATIONS] + [_res_spec(1, d) for d in DILATIONS]
        + [pl.BlockSpec((1, ATTN_W), lambda i: (0, 0))],
        out_specs=[blk, blk] + [_res_spec(1, d) for d in DILATIONS],
        out_shape=[jax.ShapeDtypeStruct((s, ATTN_W), F32), jax.ShapeDtypeStruct((s, ATTN_W), BF16)]
        + [_res_shape(s, 1, d, F32) for d in DILATIONS],
        scratch_shapes=[pltpu.VMEM((ATTN_HEADS, ROW_BLOCK, LANES), F32) for _ in DILATIONS]
        + [pltpu.VMEM((1, ROW_BLOCK, LANES), F32) for _ in DILATIONS],
        compiler_params=_params(("parallel",)),
    )(*outs, *lses, g)
    return outs_[0], outs_[1], list(outs_[2:])


def _attn_prebwd(dcat, o, g, name="attn_prebwd"):
    s = o.shape[0]
    nd = len(DILATIONS)

    def body(dy_ref, o_ref, g_ref, *rest):
        do_outs, delta_outs, gg_ref = rest[:nd], rest[nd:2 * nd], rest[2 * nd]
        do_scr, delta_scr = rest[2 * nd + 1], rest[2 * nd + 2]
        i = pl.program_id(0)
        dy, ov = dy_ref[...], o_ref[...]
        r = lax.rsqrt(jnp.mean(ov * ov, axis=-1, keepdims=True) + EPS)
        dyg = dy * g_ref[...]
        c = jnp.mean(dyg * ov, axis=-1, keepdims=True)
        do = r * dyg - ov * (r * r * r * c)
        prod = do * ov
        lane = lax.broadcasted_iota(jnp.int32, (ROW_BLOCK, LANES), 1)
        acc = jnp.zeros((ROW_BLOCK, LANES), F32)
        for h in range(ATTN_HEADS):
            sl = slice(h * HEAD_DIM, (h + 1) * HEAD_DIM)
            do_scr[h] = do[:, sl]
            acc = jnp.where(lane == h, jnp.sum(prod[:, sl], axis=-1, keepdims=True), acc)
        delta_scr[0] = acc
        for di, dil in enumerate(DILATIONS):
            _to_residues(do_scr, do_outs[di], dil)
            _to_residues(delta_scr, delta_outs[di], dil)

        @pl.when(i == 0)
        def _():
            gg_ref[...] = jnp.zeros_like(gg_ref)

        gg_ref[...] += jnp.sum(dy * ov * r, axis=0, keepdims=True)

    blk = pl.BlockSpec((ROW_BLOCK, ATTN_W), lambda i: (i, 0))
    vec = pl.BlockSpec((1, ATTN_W), lambda i: (0, 0))
    outs = pl.pallas_call(
        body, name=name, grid=(s // ROW_BLOCK,),
        in_specs=[blk, blk, vec],
        out_specs=[_res_spec(ATTN_HEADS, d) for d in DILATIONS] + [_res_spec(1, d) for d in DILATIONS] + [vec],
        out_shape=[_res_shape(s, ATTN_HEADS, d, BF16) for d in DILATIONS]
        + [_res_shape(s, 1, d, F32) for d in DILATIONS] + [jax.ShapeDtypeStruct((1, ATTN_W), F32)],
        scratch_shapes=[pltpu.VMEM((ATTN_HEADS, ROW_BLOCK, LANES), F32), pltpu.VMEM((1, ROW_BLOCK, LANES), F32)],
        compiler_params=_params(("arbitrary",)),
    )(dcat, o, g)
    return list(outs[:nd]), list(outs[nd:2 * nd]), outs[2 * nd]


def _attn_bwd(q, k, v, do, lse, delta, dil, name):
    length = q.shape[0]
    qb = ATTN_BLOCK
    nb = length // qb
    scale = HEAD_DIM ** -0.5

    def body(qp, qc, qn, kp, kc, kn, vp, vc, vn, dop, doc, don, lp, lc, ln, dp, dc, dn, dq_ref, dk_ref, dv_ref):
        valid_c, valid_ek, valid_eq = _band_masks(pl.program_id(1), length)
        everything = slice(None)
        lse_e, del_e = _edge(lp, ln, everything), _edge(dp, dn, everything)
        heads = [slice(h * HEAD_DIM, (h + 1) * HEAD_DIM) for h in range(ATTN_HEADS)]
        prods = []
        for sl in heads:
            q_c, k_c, v_c, do_c = qc[:, sl], kc[:, sl], vc[:, sl], doc[:, sl]
            q_e, k_e, v_e, do_e = _edge(qp, qn, sl), _edge(kp, kn, sl), _edge(vp, vn, sl), _edge(dop, don, sl)
            prods.append((_dot_nt(q_c, k_c), _dot_nt(do_c, v_c), _dot_nt(q_c, k_e), _dot_nt(do_c, v_e),
                          _dot_nt(q_e, k_c), _dot_nt(do_e, v_c)))
        parts = []
        for h, (s_cc, dp_cc, s_ek, dp_ek, s_eq, dp_eq) in enumerate(prods):
            hc = slice(h, h + 1)
            lse_c, del_c = lc[:, hc], dc[:, hc]
            p_cc = jnp.where(valid_c, jnp.exp(s_cc * scale - lse_c), 0.0)
            ds_cc = (p_cc * (dp_cc - del_c)).astype(BF16)
            p_ek = jnp.where(valid_ek, jnp.exp(s_ek * scale - lse_c), 0.0)
            ds_ek = (p_ek * (dp_ek - del_c)).astype(BF16)
            p_eq = jnp.where(valid_eq, jnp.exp(s_eq * scale - lse_e[:, hc]), 0.0)
            ds_eq = (p_eq * (dp_eq - del_e[:, hc])).astype(BF16)
            parts.append((p_cc.astype(BF16), ds_cc, ds_ek, p_eq.astype(BF16), ds_eq))
        for sl, (p_cc, ds_cc, ds_ek, p_eq, ds_eq) in zip(heads, parts):
            q_c, k_c, do_c = qc[:, sl], kc[:, sl], doc[:, sl]
            q_e, k_e, do_e = _edge(qp, qn, sl), _edge(kp, kn, sl), _edge(dop, don, sl)
            dq_ref[:, sl] = (_dot(ds_cc, k_c) + _dot(ds_ek, k_e)) * scale
            dk_ref[:, sl] = (_dot_tn(ds_cc, q_c) + _dot_tn(ds_eq, q_e)) * scale
            dv_ref[:, sl] = _dot_tn(p_cc, do_c) + _dot_tn(p_eq, do_e)

    wide, narrow = list(_window_specs(nb, ATTN_W)), list(_window_specs(nb, LANES))
    return tuple(pl.pallas_call(
        body, name=name, grid=(dil, nb),
        in_specs=wide * 4 + narrow * 2,
        out_specs=[wide[1]] * 3,
        out_shape=[jax.ShapeDtypeStruct((length, dil * ATTN_W), F32)] * 3,
        compiler_params=_params(("parallel", "parallel")),
    )(q, q, q, k, k, k, v, v, v, do, do, do, lse, lse, lse, delta, delta, delta))


def _gate_matrices(gf_up, gb_up):
    pad = LANES - 2 * GLA_RANK
    uf = jnp.concatenate([gf_up, jnp.zeros((GLA_RANK + pad, GLA_KW), gf_up.dtype)], axis=0)
    ub = jnp.concatenate([jnp.zeros((GLA_RANK, GLA_KW), gb_up.dtype), gb_up, jnp.zeros((pad, GLA_KW), gb_up.dtype)], axis=0)
    return uf.astype(BF16), ub.astype(BF16)


def _log_sigmoid(x):
    return jnp.minimum(x, 0.0) - jnp.log(1.0 + jnp.exp(-jnp.abs(x)))


def _gla_gates(proj, uf, ub, gf_b, gb_b, name="gla_gates"):
    s = proj.shape[0]

    def body(z_ref, uf_ref, ub_ref, bf_ref, bb_ref, gf_ref, gb_ref):
        z = z_ref[...].astype(BF16)
        gf_ref[...] = _log_sigmoid(_dot(z, uf_ref[...]) + bf_ref[...]) * (1.0 / GLA_GATE_NORM)
        gb_ref[...] = _log_sigmoid(_dot(z, ub_ref[...]) + bb_ref[...]) * (1.0 / GLA_GATE_NORM)

    mat = pl.BlockSpec((LANES, GLA_KW), lambda i: (0, 0))
    vec = pl.BlockSpec((1, GLA_KW), lambda i: (0, 0))
    out = pl.BlockSpec((ROW_BLOCK, GLA_KW), lambda i: (i, 0))
    return pl.pallas_call(
        body, name=name, grid=(s // ROW_BLOCK,),
        in_specs=[pl.BlockSpec((ROW_BLOCK, LANES), lambda i: (i, OFF_Z // LANES)), mat, mat, vec, vec],
        out_specs=[out, out],
        out_shape=[jax.ShapeDtypeStruct((s, GLA_KW), F32)] * 2,
        compiler_params=_params(("parallel",)),
    )(proj, uf, ub, gf_b, gb_b)


def _gla_gates_bwd(dgf, dgb, proj, uf, ub, gf_b, gb_b, name="gla_gates_bwd"):
    s = proj.shape[0]

    def body(dgf_ref, dgb_ref, z_ref, uf_ref, ub_ref, bf_ref, bb_ref, dz_ref, guf_ref, gub_ref, gbf_ref, gbb_ref):
        i = pl.program_id(0)
        z = z_ref[...].astype(BF16)
        uf_, ub_ = uf_ref[...], ub_ref[...]
        dpf = dgf_ref[...] * (1.0 / GLA_GATE_NORM) * _sigmoid(-(_dot(z, uf_) + bf_ref[...]))
        dpb = dgb_ref[...] * (1.0 / GLA_GATE_NORM) * _sigmoid(-(_dot(z, ub_) + bb_ref[...]))
        dpf_b, dpb_b = dpf.astype(BF16), dpb.astype(BF16)
        dz_ref[...] = (_dot_nt(dpf_b, uf_) + _dot_nt(dpb_b, ub_)).astype(BF16)

        @pl.when(i == 0)
        def _():
            for r in (guf_ref, gub_ref, gbf_ref, gbb_ref):
                r[...] = jnp.zeros_like(r)

        guf_ref[...] += _dot_tn(z, dpf_b)
        gub_ref[...] += _dot_tn(z, dpb_b)
        gbf_ref[...] += jnp.sum(dpf, axis=0, keepdims=True)
        gbb_ref[...] += jnp.sum(dpb, axis=0, keepdims=True)

    mat = pl.BlockSpec((LANES, GLA_KW), lambda i: (0, 0))
    vec = pl.BlockSpec((1, GLA_KW), lambda i: (0, 0))
    blk = pl.BlockSpec((ROW_BLOCK, GLA_KW), lambda i: (i, 0))
    return pl.pallas_call(
        body, name=name, grid=(s // ROW_BLOCK,),
        in_specs=[blk, blk, pl.BlockSpec((ROW_BLOCK, LANES), lambda i: (i, OFF_Z // LANES)), mat, mat, vec, vec],
        out_specs=[pl.BlockSpec((ROW_BLOCK, LANES), lambda i: (i, 0)), mat, mat, vec, vec],
        out_shape=[jax.ShapeDtypeStruct((s, LANES), BF16), jax.ShapeDtypeStruct((LANES, GLA_KW), F32),
                   jax.ShapeDtypeStruct((LANES, GLA_KW), F32), jax.ShapeDtypeStruct((1, GLA_KW), F32),
                   jax.ShapeDtypeStruct((1, GLA_KW), F32)],
        compiler_params=_params(("arbitrary",)),
    )(dgf, dgb, proj, uf, ub, gf_b, gb_b)


def _split3(x):
    x1 = x.astype(BF16)
    r1 = x - x1.astype(F32)
    x2 = r1.astype(BF16)
    x3 = (r1 - x2.astype(F32)).astype(BF16)
    return x1, x2, x3


def _dot_exact(mask_bf, x):
    x1, x2, x3 = _split3(x)
    return _dot(mask_bf, x1) + _dot(mask_bf, x2) + _dot(mask_bf, x3)


def _chunk_masks(reverse):
    c = GLA_CHUNK
    row = lax.broadcasted_iota(jnp.int32, (c, c), 0)
    col = lax.broadcasted_iota(jnp.int32, (c, c), 1)
    allowed = (col >= row) if reverse else (col <= row)
    seen_by = (col <= row) if reverse else (col >= row)
    return allowed, seen_by


def _chunk_terms(q_ref, k_ref, g_ref, rs, hs, allowed, reverse):
    c = GLA_CHUNK
    mid, last = (c // 2, 0) if reverse else (c // 2 - 1, c - 1)
    q = q_ref[rs, hs] * (GLA_DK ** -0.5)
    k = k_ref[rs, hs]
    b = _dot_exact(jnp.where(allowed, 1.0, 0.0).astype(BF16), g_ref[rs, hs])
    bref, blast = b[mid:mid + 1, :], b[last:last + 1, :]
    e_q, e_k, e_in, e_st = jnp.exp(b - bref), jnp.exp(bref - b), jnp.exp(b), jnp.exp(blast - b)
    return dict(last=last, e_q=e_q, e_k=e_k, e_in=e_in, e_st=e_st,
                dec=jnp.exp(blast), qe=q * e_q, ke=k * e_k, qin=q * e_in, kst=k * e_st)


def _gla_blockspecs(s, reverse_order):
    cb = GLA_CHUNKS_PER_STEP
    rows = cb * GLA_CHUNK
    nsteps = s // rows

    def rb(n):
        return (nsteps - 1 - n) if reverse_order else n

    qspec = pl.BlockSpec((rows, GLA_KW), lambda n: (rb(n), OFF_GQ // GLA_KW))
    kspec = pl.BlockSpec((rows, GLA_KW), lambda n: (rb(n), OFF_GK // GLA_KW))
    vspec = pl.BlockSpec((rows, GLA_VW), lambda n: (rb(n), OFF_GV // GLA_VW))
    gspec = pl.BlockSpec((rows, GLA_KW), lambda n: (rb(n), 0))
    ospec = pl.BlockSpec((rows, GLA_VW), lambda n: (rb(n), 0))
    sspec = pl.BlockSpec((GLA_HEADS, cb, GLA_DV, GLA_DK), lambda n: (0, rb(n), 0, 0))
    return cb, rows, nsteps, qspec, kspec, vspec, gspec, ospec, sspec


def _gla_units(cb, order_reversed):
    chunks = list(reversed(range(cb))) if order_reversed else list(range(cb))
    return [(c, h, slice(c * GLA_CHUNK, (c + 1) * GLA_CHUNK), slice(h * GLA_DK, (h + 1) * GLA_DK),
             slice(h * GLA_DV, (h + 1) * GLA_DV)) for c in chunks for h in range(GLA_HEADS)]


def _gla_fwd(proj, g, reverse, name):
    s = proj.shape[0]
    cb, rows, nsteps, qspec, kspec, vspec, gspec, ospec, sspec = _gla_blockspecs(s, reverse)

    def body(q_ref, k_ref, v_ref, g_ref, o_ref, st_ref, state):
        @pl.when(pl.program_id(0) == 0)
        def _():
            state[...] = jnp.zeros_like(state)

        allowed, _ = _chunk_masks(reverse)
        units = _gla_units(cb, reverse)
        terms = [_chunk_terms(q_ref, k_ref, g_ref, rs, hs, allowed, reverse) for _, _, rs, hs, _ in units]
        vals = [v_ref[rs, vs].astype(BF16) for _, _, rs, _, vs in units]
        raw = [(_dot_nt(t["qe"].astype(BF16), t["ke"].astype(BF16)), _dot_tn(v, t["kst"].astype(BF16)))
               for t, v in zip(terms, vals)]
        intra = [_dot(jnp.where(allowed, a, 0.0).astype(BF16), v) for (a, _), v in zip(raw, vals)]
        st = [state[h] for h in range(GLA_HEADS)]
        for (c, h, rs, _, vs), t, (_, kv), o_in in zip(units, terms, raw, intra):
            st_ref[h, c] = st[h]
            o_ref[rs, vs] = o_in + _dot_nt(t["qin"].astype(BF16), st[h].astype(BF16))
            st[h] = st[h] * t["dec"] + kv
        for h in range(GLA_HEADS):
            state[h] = st[h]

    return pl.pallas_call(
        body, name=name, grid=(nsteps,),
        in_specs=[qspec, kspec, vspec, gspec],
        out_specs=[ospec, sspec],
        out_shape=[jax.ShapeDtypeStruct((s, GLA_VW), F32),
                   jax.ShapeDtypeStruct((GLA_HEADS, s // GLA_CHUNK, GLA_DV, GLA_DK), F32)],
        scratch_shapes=[pltpu.VMEM((GLA_HEADS, GLA_DV, GLA_DK), F32)],
        compiler_params=_params(("arbitrary",)),
    )(proj, proj, proj, g)


def _gla_bwd(proj, g, do, states, reverse, name):
    s = proj.shape[0]
    cb, rows, nsteps, qspec, kspec, vspec, gspec, ospec, sspec = _gla_blockspecs(s, not reverse)

    def body(q_ref, k_ref, v_ref, g_ref, do_ref, sp_ref, dq_ref, dk_ref, dv_ref, dg_ref, dstate):
        @pl.when(pl.program_id(0) == 0)
        def _():
            dstate[...] = jnp.zeros_like(dstate)

        allowed, seen_by = _chunk_masks(reverse)
        units = _gla_units(cb, not reverse)
        terms = [_chunk_terms(q_ref, k_ref, g_ref, rs, hs, allowed, reverse) for _, _, rs, hs, _ in units]
        vals = [v_ref[rs, vs].astype(BF16) for _, _, rs, _, vs in units]
        dos = [do_ref[rs, vs] for _, _, rs, _, vs in units]
        prevs = [sp_ref[h, c] for c, h, _, _, _ in units]
        raw = [(_dot_nt(t["qe"].astype(BF16), t["ke"].astype(BF16)), _dot_nt(do, v),
                _dot(do, sp.astype(BF16)), _dot_tn(do, t["qin"].astype(BF16)))
               for t, v, do, sp in zip(terms, vals, dos, prevs)]
        inner = []
        for t, do, (a, da, _, _) in zip(terms, dos, raw):
            da = jnp.where(allowed, da, 0.0).astype(BF16)
            inner.append((_dot(da, t["ke"].astype(BF16)), _dot_tn(da, t["qe"].astype(BF16)),
                          _dot_tn(jnp.where(allowed, a, 0.0).astype(BF16), do)))
        ds = [dstate[h] for h in range(GLA_HEADS)]
        outer = []
        for (c, h, _, _, _), t, v, sp, (_, _, _, inc) in zip(units, terms, vals, prevs, raw):
            ds_b = ds[h].astype(BF16)
            outer.append((_dot(v, ds_b), _dot_nt(t["kst"].astype(BF16), ds_b),
                          jnp.sum(sp * ds[h], axis=0, keepdims=True)))
            ds[h] = ds[h] * t["dec"] + inc
        for h in range(GLA_HEADS):
            dstate[h] = ds[h]
        seen_bf = jnp.where(seen_by, 1.0, 0.0).astype(BF16)
        rowi = lax.broadcasted_iota(jnp.int32, (GLA_CHUNK, GLA_DK), 0)
        for (c, h, rs, hs, vs), t, (_, _, dqin, _), (dqe, dke, dv_in), (dkst, dv_out, ddec) in zip(
                units, terms, raw, inner, outer):
            dv_ref[rs, vs] = dv_in + dv_out
            dq_ref[rs, hs] = (dqe * t["e_q"] + dqin * t["e_in"]) * (GLA_DK ** -0.5)
            dk_ref[rs, hs] = dke * t["e_k"] + dkst * t["e_st"]
            kk = dkst * t["kst"]
            db = dqe * t["qe"] - dke * t["ke"] + dqin * t["qin"] - kk
            extra = jnp.sum(kk, axis=0, keepdims=True) + ddec * t["dec"]
            db = db + jnp.where(rowi == t["last"], extra, 0.0)
            dg_ref[rs, hs] = _dot_exact(seen_bf, db)

    return pl.pallas_call(
        body, name=name, grid=(nsteps,),
        in_specs=[qspec, kspec, vspec, gspec, ospec, sspec],
        out_specs=[gspec, gspec, ospec, gspec],
        out_shape=[jax.ShapeDtypeStruct((s, GLA_KW), F32), jax.ShapeDtypeStruct((s, GLA_KW), F32),
                   jax.ShapeDtypeStruct((s, GLA_VW), F32), jax.ShapeDtypeStruct((s, GLA_KW), F32)],
        scratch_shapes=[pltpu.VMEM((GLA_HEADS, GLA_DV, GLA_DK), F32)],
        compiler_params=_params(("arbitrary",)),
    )(proj, proj, proj, g, do, states)


def _gla_post(o_f, o_b, proj, g, name="gla_post"):
    s = o_f.shape[0]

    def body(of_ref, ob_ref, gr_ref, g_ref, o_ref):
        gv = g_ref[...]
        for h in range(GLA_HEADS):
            sl = slice(h * GLA_DV, (h + 1) * GLA_DV)
            osum = of_ref[:, sl] + ob_ref[:, sl]
            r = lax.rsqrt(jnp.mean(osum * osum, axis=-1, keepdims=True) + EPS)
            gr = gr_ref[:, sl]
            o_ref[:, sl] = (osum * r * gv * (gr * _sigmoid(gr))).astype(BF16)

    blk = pl.BlockSpec((ROW_BLOCK, GLA_VW), lambda i: (i, 0))
    return pl.pallas_call(
        body, name=name, grid=(s // ROW_BLOCK,),
        in_specs=[blk, blk, pl.BlockSpec((ROW_BLOCK, GLA_VW), lambda i: (i, OFF_GR // GLA_VW)),
                  pl.BlockSpec((1, GLA_DV), lambda i: (0, 0))],
        out_specs=blk,
        out_shape=jax.ShapeDtypeStruct((s, GLA_VW), BF16),
        compiler_params=_params(("parallel",)),
    )(o_f, o_b, proj, g)


def _gla_post_bwd(dcat, o_f, o_b, proj, g, name="gla_post_bwd"):
    s = o_f.shape[0]

    def body(dy_ref, of_ref, ob_ref, gr_ref, g_ref, do_ref, dgr_ref, gg_ref):
        i = pl.program_id(0)
        gv = g_ref[...]
        gg = jnp.zeros((1, GLA_DV), F32)
        for h in range(GLA_HEADS):
            sl = slice(h * GLA_DV, (h + 1) * GLA_DV)
            osum = of_ref[:, sl] + ob_ref[:, sl]
            r = lax.rsqrt(jnp.mean(osum * osum, axis=-1, keepdims=True) + EPS)
            gr, dy = gr_ref[:, sl], dy_ref[:, sl]
            sg = _sigmoid(gr)
            dgr_ref[:, sl] = (dy * (osum * r * gv) * (sg * (1.0 + gr * (1.0 - sg)))).astype(BF16)
            dn = dy * (gr * sg)
            dng = dn * gv
            c = jnp.mean(dng * osum, axis=-1, keepdims=True)
            do_ref[:, sl] = (r * dng - osum * (r * r * r * c)).astype(BF16)
            gg = gg + jnp.sum(dn * osum * r, axis=0, keepdims=True)

        @pl.when(i == 0)
        def _():
            gg_ref[...] = jnp.zeros_like(gg_ref)

        gg_ref[...] += gg

    blk = pl.BlockSpec((ROW_BLOCK, GLA_VW), lambda i: (i, 0))
    vec = pl.BlockSpec((1, GLA_DV), lambda i: (0, 0))
    return pl.pallas_call(
        body, name=name, grid=(s // ROW_BLOCK,),
        in_specs=[pl.BlockSpec((ROW_BLOCK, GLA_VW), lambda i: (i, 1)), blk, blk,
                  pl.BlockSpec((ROW_BLOCK, GLA_VW), lambda i: (i, OFF_GR // GLA_VW)), vec],
        out_specs=[blk, blk, vec],
        out_shape=[jax.ShapeDtypeStruct((s, GLA_VW), BF16), jax.ShapeDtypeStruct((s, GLA_VW), BF16),
                   jax.ShapeDtypeStruct((1, GLA_DV), F32)],
        compiler_params=_params(("arbitrary",)),
    )(dcat, o_f, o_b, proj, g)


def _assemble_dproj(dpa, dq_f, dq_b, dk_f, dk_b, dv_f, dv_b, dgr, dz, name="assemble_dproj"):
    s = dpa.shape[0]

    def body(dpa_ref, dqf, dqb, dkf, dkb, dvf, dvb, dgr_ref, dz_ref, o_ref):
        o_ref[:, 0:OFF_GQ] = dpa_ref[...]
        o_ref[:, OFF_GQ:OFF_GK] = (dqf[...] + dqb[...]).astype(BF16)
        o_ref[:, OFF_GK:OFF_GV] = (dkf[...] + dkb[...]).astype(BF16)
        o_ref[:, OFF_GV:OFF_GR] = (dvf[...] + dvb[...]).astype(BF16)
        o_ref[:, OFF_GR:OFF_Z] = dgr_ref[...]
        o_ref[:, OFF_Z:IN_PAD] = dz_ref[...]

    def blk(w):
        return pl.BlockSpec((ROW_BLOCK, w), lambda i: (i, 0))

    return pl.pallas_call(
        body, name=name, grid=(s // ROW_BLOCK,),
        in_specs=[blk(3 * ATTN_W)] + [blk(GLA_KW)] * 4 + [blk(GLA_VW)] * 3 + [blk(LANES)],
        out_specs=blk(IN_PAD),
        out_shape=jax.ShapeDtypeStruct((s, IN_PAD), BF16),
        compiler_params=_params(("parallel",)),
    )(dpa, dq_f, dq_b, dk_f, dk_b, dv_f, dv_b, dgr, dz)


CONV_ROWS = 256
CONV_COLS = 1408
HALO = SUBLANES


def _halo_specs(s, tr, tc, col_of):
    per = tr // HALO
    last = s // HALO - 1
    cur = pl.BlockSpec((tr, tc), lambda c, i: (i, col_of(c)))
    prev = pl.BlockSpec((HALO, tc), lambda c, i: (jnp.maximum(i * per - 1, 0), col_of(c)))
    nxt = pl.BlockSpec((HALO, tc), lambda c, i: (jnp.minimum((i + 1) * per, last), col_of(c)))
    return prev, cur, nxt


def _extended(prev_ref, cur_ref, next_ref, i, s, tr):
    x = jnp.concatenate([prev_ref[...], cur_ref[...], next_ref[...]], axis=0)
    idx = i * tr - HALO + lax.broadcasted_iota(jnp.int32, x.shape, 0)
    return jnp.where((idx >= 0) & (idx < s), x, 0.0)


def _conv_glu(gate, up, conv_w, conv_b, name="conv_glu"):
    s, f = gate.shape
    tr, tc = CONV_ROWS, CONV_COLS
    ext = tr + 2 * HALO

    def body(gp, gc, gn, up_ref, w_ref, b_ref, o_ref):
        i = pl.program_id(1)
        ge = _extended(gp, gc, gn, i, s, tr)
        w = w_ref[...]
        conv = (w[0:1] * pltpu.roll(ge, 1, 0) + w[1:2] * ge + w[2:3] * pltpu.roll(ge, ext - 1, 0))[HALO:HALO + tr]
        conv = conv + b_ref[...]
        o_ref[...] = (conv * _sigmoid(conv) * up_ref[...]).astype(BF16)

    prev, cur, nxt = _halo_specs(s, tr, tc, lambda c: c)
    return pl.pallas_call(
        body, name=name, grid=(f // tc, s // tr),
        in_specs=[prev, cur, nxt, cur, pl.BlockSpec((3, tc), lambda c, i: (0, c)), pl.BlockSpec((1, tc), lambda c, i: (0, c))],
        out_specs=cur,
        out_shape=jax.ShapeDtypeStruct((s, f), BF16),
        compiler_params=_params(("parallel", "parallel")),
    )(gate, gate, gate, up, conv_w, conv_b)


def _conv_glu_bwd(dact, gate, up, conv_w, conv_b, name="conv_glu_bwd"):
    s, f = gate.shape
    tr, tc = CONV_ROWS, CONV_COLS
    ext = tr + 2 * HALO

    def body(dp, dc, dn, gp, gc, gn, upp, upc, upn, w_ref, b_ref, dg_ref, du_ref, gw_ref, gb_ref):
        i = pl.program_id(1)
        ge = _extended(gp, gc, gn, i, s, tr)
        ue = _extended(upp, upc, upn, i, s, tr)
        de = _extended(dp, dc, dn, i, s, tr)
        w = w_ref[...]
        g_prev, g_next = pltpu.roll(ge, 1, 0), pltpu.roll(ge, ext - 1, 0)
        conv = w[0:1] * g_prev + w[1:2] * ge + w[2:3] * g_next + b_ref[...]
        sg = _sigmoid(conv)
        du_ref[...] = (de * (conv * sg))[HALO:HALO + tr].astype(BF16)
        dconv = de * ue * (sg * (1.0 + conv * (1.0 - sg)))
        dgate = w[0:1] * pltpu.roll(dconv, ext - 1, 0) + w[1:2] * dconv + w[2:3] * pltpu.roll(dconv, 1, 0)
        dg_ref[...] = dgate[HALO:HALO + tr].astype(BF16)
        inner = slice(HALO, HALO + tr)
        dci = dconv[inner]

        @pl.when(i == 0)
        def _():
            gw_ref[...] = jnp.zeros_like(gw_ref)
            gb_ref[...] = jnp.zeros_like(gb_ref)

        gw_ref[0:1, :] += jnp.sum(dci * g_prev[inner], axis=0, keepdims=True)
        gw_ref[1:2, :] += jnp.sum(dci * ge[inner], axis=0, keepdims=True)
        gw_ref[2:3, :] += jnp.sum(dci * g_next[inner], axis=0, keepdims=True)
        gb_ref[...] += jnp.sum(dci, axis=0, keepdims=True)

    prev, cur, nxt = _halo_specs(s, tr, tc, lambda c: c)
    wspec = pl.BlockSpec((3, tc), lambda c, i: (0, c))
    bspec = pl.BlockSpec((1, tc), lambda c, i: (0, c))
    return pl.pallas_call(
        body, name=name, grid=(f // tc, s // tr),
        in_specs=[prev, cur, nxt] * 3 + [wspec, bspec],
        out_specs=[cur, cur, wspec, bspec],
        out_shape=[jax.ShapeDtypeStruct((s, f), BF16), jax.ShapeDtypeStruct((s, f), BF16),
                   jax.ShapeDtypeStruct((3, f), F32), jax.ShapeDtypeStruct((1, f), F32)],
        compiler_params=_params(("parallel", "arbitrary")),
    )(dact, dact, dact, gate, gate, gate, up, up, up, conv_w, conv_b)


def _local_step(x, target, w, late_weights=None, grad_sink=None, first_dep=()):
    s = x.shape[0]
    tables = _rope_tables(s)
    uf, ub = _gate_matrices(w["gf_up"], w["gb_up"])
    if grad_sink is None:
        grad_sink = lambda names, grads: ()

    n1 = _rms_fwd(x, w["norm1_g"], "norm1")
    proj = _matmul([(n1, w["w_in"])], "nn", F32, 1024, 896, D_MODEL, "in_proj", deps=first_dep)
    qkv = _rope_fwd(proj, tables)
    branches = [_attn_fwd(*qkv[di], d, f"attn_fwd_d{d}") for di, d in enumerate(DILATIONS)]
    o_mix, ao, lse = _attn_combine([b[0] for b in branches], [b[1] for b in branches], w["attn_norm_g"])
    g_f, g_b = _gla_gates(proj, uf, ub, w["gf_b"], w["gb_b"])
    o_f, st_f = _gla_fwd(proj, g_f, False, "gla_fwd_f")
    o_b, st_b = _gla_fwd(proj, g_b, True, "gla_fwd_b")
    go = _gla_post(o_f, o_b, proj, w["gla_norm_g"])
    cat = jnp.concatenate([ao, go], axis=1)
    if late_weights is not None:
        w = {**w, **late_weights(cat)}
    h1 = _matmul([(cat, w["w_out"])], "nn", F32, 512, 1024, D_MODEL, "out_proj", res=x)
    n2 = _rms_fwd(h1, w["norm2_g"], "norm2")
    gate = _matmul([(n2, w["w_gate"])], "nn", F32, 512, 1408, D_MODEL, "ffn_gate")
    up = _matmul([(n2, w["w_up"])], "nn", F32, 512, 1408, D_MODEL, "ffn_up")
    act = _conv_glu(gate, up, w["conv_w"], w["conv_b"])
    h2 = _matmul([(act, w["w_down"])], "nn", F32, 1024, 1024, 1408, "ffn_down", res=h1)
    dh2, dh2_b, loss_acc, g_final = _final_loss(h2, target, w["final_norm_g"])

    dact = _matmul([(dh2_b, w["w_down"])], "nt", F32, 512, 1408, D_MODEL, "d_act")
    g_w_down = _matmul([(act, dh2_b)], "tn", F32, 1408, 1024, 2048, "g_w_down")
    dep = grad_sink(["w_down"], [g_w_down])
    dgate, dup, g_conv_w, g_conv_b = _conv_glu_bwd(dact, gate, up, w["conv_w"], w["conv_b"])
    g_w_gate = _matmul([(n2, dgate)], "tn", F32, 1024, 1408, 2048, "g_w_gate", deps=dep)
    g_w_up = _matmul([(n2, dup)], "tn", F32, 1024, 1408, 2048, "g_w_up")
    dep = grad_sink(["w_gate", "w_up"], [g_w_gate, g_w_up])
    dn2 = _matmul([(dgate, w["w_gate"]), (dup, w["w_up"])], "nt", F32, 1024, 1024, 1408, "d_n2", deps=dep)
    dh1, dh1_b, g_norm2 = _rms_bwd(dn2, h1, w["norm2_g"], dh2, "norm2_bwd")

    g_w_out = _matmul([(cat, dh1_b)], "tn", F32, 1024, 1024, 2048, "g_w_out")
    dep = grad_sink(["w_out"], [g_w_out])
    dcat = _matmul([(dh1_b, w["w_out"])], "nt", F32, 512, 1024, D_MODEL, "d_cat", deps=dep)
    do_attn, delta, g_attn_norm = _attn_prebwd(dcat, o_mix, w["attn_norm_g"])
    grads = [_attn_bwd(*qkv[di], do_attn[di], lse[di], delta[di], d, f"attn_bwd_d{d}")
             for di, d in enumerate(DILATIONS)]
    dpa = _rope_bwd(grads, tables)
    do_gla, dgr, g_gla_norm = _gla_post_bwd(dcat, o_f, o_b, proj, w["gla_norm_g"])
    dq_f, dk_f, dv_f, dg_f = _gla_bwd(proj, g_f, do_gla, st_f, False, "gla_bwd_f")
    dq_b, dk_b, dv_b, dg_b = _gla_bwd(proj, g_b, do_gla, st_b, True, "gla_bwd_b")
    dz, g_uf, g_ub, g_gf_b, g_gb_b = _gla_gates_bwd(dg_f, dg_b, proj, uf, ub, w["gf_b"], w["gb_b"])
    dproj = _assemble_dproj(dpa, dq_f, dq_b, dk_f, dk_b, dv_f, dv_b, dgr, dz)
    g_w_in = _matmul([(n1, dproj)], "tn", F32, 1024, 896, 2048, "g_w_in")
    dep = grad_sink(["w_in"], [g_w_in])
    dn1 = _matmul([(dproj, w["w_in"])], "nt", F32, 512, 1024, IN_PAD, "d_n1", deps=dep)
    grad_x, _, g_norm1 = _rms_bwd(dn1, x, w["norm1_g"], dh1, "norm1_bwd")

    g = dict(norm1_g=g_norm1, w_in=g_w_in, gf_up=g_uf[:GLA_RANK], gf_b=g_gf_b,
             gb_up=g_ub[GLA_RANK:2 * GLA_RANK], gb_b=g_gb_b, gla_norm_g=g_gla_norm, attn_norm_g=g_attn_norm,
             w_out=g_w_out, norm2_g=g_norm2, w_gate=g_w_gate, w_up=g_w_up, conv_w=g_conv_w, conv_b=g_conv_b,
             w_down=g_w_down, final_norm_g=g_final)
    return loss_acc, grad_x, g


def _me_and_peers():
    x, y, c = lax.axis_index("x"), lax.axis_index("y"), lax.axis_index("c")
    me = 4 * x + 2 * y + c
    peers = []
    for kbits in range(1, N_DEV):
        px, py, pc = x ^ (kbits >> 2 & 1), y ^ (kbits >> 1 & 1), c ^ (kbits & 1)
        peers.append(((px, py, pc), 4 * px + 2 * py + pc))
    return me, peers


_HBM = pl.BlockSpec(memory_space=pltpu.HBM)
_SEM = pl.BlockSpec(memory_space=pltpu.SEMAPHORE)
_ANY = pl.BlockSpec(memory_space=pl.ANY)
_EFFECT = pltpu.SideEffectType.DATAFLOW_SIDE_EFFECTING


def _exchange_copies(src_refs, land_refs, send_sems, recv_sems, scatter):
    me, peers = _me_and_peers()
    out = []
    for a, (src, land) in enumerate(zip(src_refs, land_refs)):
        for kk, (dev, idx) in enumerate(peers):
            out.append(pltpu.make_async_remote_copy(
                src_ref=src.at[idx] if scatter else src, dst_ref=land.at[me],
                send_sem=send_sems.at[a * (N_DEV - 1) + kk], recv_sem=recv_sems.at[a * (N_DEV - 1) + kk],
                device_id=dev, device_id_type=MESH_ID))
    return out


def _exchange_start(srcs, lands, scatter, name, deps=()):
    n, nd = len(srcs), len(deps)

    def body(*refs):
        src_refs, land_refs = refs[:n], refs[n:2 * n]
        send_sems, recv_sems = refs[2 * n + nd:2 * n + nd + 2]
        token = refs[-1]
        for cp in _exchange_copies(src_refs, land_refs, send_sems, recv_sems, scatter):
            cp.start()
        token[...] = jnp.zeros_like(token)

    outs = pl.pallas_call(
        body, name=name,
        in_specs=[_HBM] * (2 * n) + [_ANY] * nd,
        out_specs=[_SEM, _SEM] + [_HBM] * (2 * n) + [pl.BlockSpec(memory_space=pltpu.VMEM)],
        out_shape=[pltpu.SemaphoreType.DMA((n * (N_DEV - 1),)), pltpu.SemaphoreType.DMA((n * (N_DEV - 1),))]
        + [pltpu.HBM(t.shape, t.dtype) for t in srcs] + [pltpu.HBM(t.shape, t.dtype) for t in lands]
        + [jax.ShapeDtypeStruct((SUBLANES, LANES), F32)],
        input_output_aliases={i: 2 + i for i in range(2 * n)},
        compiler_params=pltpu.CompilerParams(has_side_effects=_EFFECT),
    )(*[pltpu.with_memory_space_constraint(t, pltpu.HBM) for t in list(srcs) + list(lands)], *deps)
    send_sems, recv_sems = outs[0], outs[1]
    return dict(send=send_sems, recv=recv_sems, srcs=outs[2:2 + n], lands=outs[2 + n:2 + 2 * n],
                scatter=scatter, token=outs[-1])


def _exchange_wait(started, name, after):
    n = len(started["srcs"])
    scatter = started["scatter"]

    def body(*refs):
        src_refs, land_refs = refs[:n], refs[n:2 * n]
        send_sems, recv_sems = refs[2 * n], refs[2 * n + 1]
        for cp in _exchange_copies(src_refs, land_refs, send_sems, recv_sems, scatter):
            cp.wait_send()
            cp.wait_recv()

    outs = pl.pallas_call(
        body, name=name,
        in_specs=[_HBM] * (2 * n) + [_SEM, _SEM, _ANY],
        out_specs=[_HBM] * (2 * n),
        out_shape=[pltpu.HBM(t.shape, t.dtype) for t in started["srcs"]]
        + [pltpu.HBM(t.shape, t.dtype) for t in started["lands"]],
        input_output_aliases={i: i for i in range(2 * n)},
        compiler_params=pltpu.CompilerParams(has_side_effects=_EFFECT),
    )(*started["srcs"], *started["lands"], started["send"], started["recv"], after)
    return outs[:n], outs[n:]


def _all_gather_two_level(shard, name):
    def body(x_ref, out_ref, send_sems, recv_sems, local_sem):
        x, y, c = lax.axis_index("x"), lax.axis_index("y"), lax.axis_index("c")
        me, sibling = (x, y, c), (x, y, 1 - c)
        chips = [(1 - x, y), (x, 1 - y), (1 - x, 1 - y)]

        def slot(px, py, pc):
            return out_ref.at[4 * px + 2 * py + pc]

        def copy(k, block, to, src=None):
            return pltpu.make_async_remote_copy(
                src_ref=slot(*block) if src is None else src, dst_ref=slot(*block),
                send_sem=send_sems.at[k], recv_sem=recv_sems.at[k], device_id=to, device_id_type=MESH_ID)

        mine = pltpu.make_async_copy(x_ref, slot(*me), local_sem)
        mine.start()
        first = [copy(0, me, sibling, src=x_ref)]
        first += [copy(1 + j, me, (*chip, c), src=x_ref) for j, chip in enumerate(chips)]
        for cp in first:
            cp.start()
        passed = [copy(4 + j, (*chip, c), sibling) for j, chip in enumerate(chips)]
        for j, chip in enumerate(chips):
            copy(1 + j, (*chip, c), me).wait_recv()
            passed[j].start()
        copy(0, sibling, me).wait_recv()
        for j, chip in enumerate(chips):
            copy(4 + j, (*chip, 1 - c), me).wait_recv()
        for cp in first + passed:
            cp.wait_send()
        mine.wait()

    return pl.pallas_call(
        body, name=name,
        in_specs=[_ANY], out_specs=_ANY,
        out_shape=jax.ShapeDtypeStruct((N_DEV,) + shard.shape, shard.dtype),
        scratch_shapes=[pltpu.SemaphoreType.DMA((N_DEV - 1,)), pltpu.SemaphoreType.DMA((N_DEV - 1,)),
                        pltpu.SemaphoreType.DMA],
    )(shard)


def _all_gather_vmem(vec, name):
    r = vec.shape[0]

    def body(v_ref, o_ref, send_sems, recv_sems):
        me, peers = _me_and_peers()
        o_ref[me] = v_ref[...]
        sends = []
        for kk, (dev, _) in enumerate(peers):
            cp = pltpu.make_async_remote_copy(
                src_ref=v_ref, dst_ref=o_ref.at[me],
                send_sem=send_sems.at[kk], recv_sem=recv_sems.at[kk],
                device_id=dev, device_id_type=MESH_ID)
            cp.start()
            sends.append(cp)
        for kk, (dev, idx) in enumerate(peers):
            pltpu.make_async_remote_copy(
                src_ref=v_ref, dst_ref=o_ref.at[idx],
                send_sem=send_sems.at[kk], recv_sem=recv_sems.at[kk],
                device_id=dev, device_id_type=MESH_ID).wait_recv()
        for cp in sends:
            cp.wait_send()

    return pl.pallas_call(
        body, name=name,
        in_specs=[pl.BlockSpec(memory_space=pltpu.VMEM)],
        out_specs=pl.BlockSpec(memory_space=pltpu.VMEM),
        out_shape=jax.ShapeDtypeStruct((N_DEV, r, LANES), F32),
        scratch_shapes=[pltpu.SemaphoreType.DMA((N_DEV - 1,)), pltpu.SemaphoreType.DMA((N_DEV - 1,))],
        compiler_params=pltpu.CompilerParams(vmem_limit_bytes=VMEM_LIMIT),
    )(vec)


def _adamw_math(w, g, m, v):
    m = ADAM_B1 * m + (1.0 - ADAM_B1) * g
    v = ADAM_B2 * v + (1.0 - ADAM_B2) * (g * g)
    m_hat = m / (1.0 - ADAM_B1 ** ADAM_STEP)
    v_hat = v / (1.0 - ADAM_B2 ** ADAM_STEP)
    delta = -ADAM_LR * (m_hat / (jnp.sqrt(v_hat) + ADAM_EPS) + ADAM_WD * w)
    return delta, m, v


def _adamw_sum(parts, w, m, v, tr, name, own=None, me=None):
    r, c = w.shape

    def body(*refs):
        if own is None:
            p_ref, w_ref, m_ref, v_ref, g_ref, d_ref, nm_ref, nv_ref = refs
            terms = [p_ref[kk] for kk in range(N_DEV)]
        else:
            me_ref, p_ref, own_ref, w_ref, m_ref, v_ref, g_ref, d_ref, nm_ref, nv_ref = refs
            terms = [jnp.where(me_ref[0] == kk, own_ref[0], p_ref[kk]).astype(F32) for kk in range(N_DEV)]
        g = terms[0]
        for t in terms[1:]:
            g = g + t
        g_ref[...] = g
        d_ref[...], nm_ref[...], nv_ref[...] = _adamw_math(w_ref[...], g, m_ref[...], v_ref[...])

    out_shape = [jax.ShapeDtypeStruct((r, c), F32)] * 4
    if own is None:
        blk = pl.BlockSpec((tr, c), lambda i: (i, 0))
        return pl.pallas_call(
            body, name=name, grid=(r // tr,),
            in_specs=[pl.BlockSpec((N_DEV, tr, c), lambda i: (0, i, 0)), blk, blk, blk],
            out_specs=[blk] * 4, out_shape=out_shape,
            compiler_params=_params(("parallel",)),
        )(parts, w, m, v)
    blk = pl.BlockSpec((tr, c), lambda i, me_ref: (i, 0))
    return pl.pallas_call(
        body, name=name,
        grid_spec=pltpu.PrefetchScalarGridSpec(
            num_scalar_prefetch=1, grid=(r // tr,),
            in_specs=[pl.BlockSpec((N_DEV, tr, c), lambda i, me_ref: (0, i, 0)),
                      pl.BlockSpec((1, tr, c), lambda i, me_ref: (me_ref[0], i, 0)), blk, blk, blk],
            out_specs=[blk] * 4),
        out_shape=out_shape,
        compiler_params=_params(("parallel",)),
    )(jnp.reshape(me, (1,)).astype(jnp.int32), parts, own, w, m, v)


_SMALL = ("norm1_g", "gf_b", "gb_b", "gla_norm_g", "attn_norm_g", "norm2_g", "conv_b", "final_norm_g",
          "gf_up", "gb_up", "conv_w")


def _pack(named):
    flat = jnp.concatenate([jnp.ravel(t).astype(F32) for t in named])
    tile = SUBLANES * LANES
    total = -(-flat.shape[0] // tile) * tile
    return jnp.pad(flat, (0, total - flat.shape[0])).reshape(total // LANES, LANES)


def _unpack(packed, shapes):
    flat = packed.reshape(-1)
    out, off = [], 0
    for shp in shapes:
        size = int(np.prod(shp))
        out.append(flat[off:off + size].reshape(shp))
        off += size
    return out


def kernel(x, norm1_g, w_in, gf_up, gf_b, gb_up, gb_b, gla_norm_g, attn_norm_g, w_out, norm2_g, w_gate, w_up, conv_w, conv_b, w_down, final_norm_g, loss_target, m_norm1_g, m_w_in, m_gf_up, m_gf_b, m_gb_up, m_gb_b, m_gla_norm_g, m_attn_norm_g, m_w_out, m_norm2_g, m_w_gate, m_w_up, m_conv_w, m_conv_b, m_w_down, m_final_norm_g, v_norm1_g, v_w_in, v_gf_up, v_gf_b, v_gb_up, v_gb_b, v_gla_norm_g, v_attn_norm_g, v_w_out, v_norm2_g, v_w_gate, v_w_up, v_conv_w, v_conv_b, v_w_down, v_final_norm_g):
    names = ("norm1_g", "w_in", "gf_up", "gf_b", "gb_up", "gb_b", "gla_norm_g", "attn_norm_g", "w_out", "norm2_g",
             "w_gate", "w_up", "conv_w", "conv_b", "w_down", "final_norm_g")
    ws = dict(zip(names, (norm1_g, w_in, gf_up, gf_b, gb_up, gb_b, gla_norm_g, attn_norm_g, w_out, norm2_g,
                          w_gate, w_up, conv_w, conv_b, w_down, final_norm_g)))
    ms = dict(zip(names, (m_norm1_g, m_w_in, m_gf_up, m_gf_b, m_gb_up, m_gb_b, m_gla_norm_g, m_attn_norm_g, m_w_out,
                          m_norm2_g, m_w_gate, m_w_up, m_conv_w, m_conv_b, m_w_down, m_final_norm_g)))
    vs = dict(zip(names, (v_norm1_g, v_w_in, v_gf_up, v_gf_b, v_gb_up, v_gb_b, v_gla_norm_g, v_attn_norm_g, v_w_out,
                          v_norm2_g, v_w_gate, v_w_up, v_conv_w, v_conv_b, v_w_down, v_final_norm_g)))
    me = 4 * lax.axis_index("x") + 2 * lax.axis_index("y") + lax.axis_index("c")
    big = ("w_in", "w_out", "w_gate", "w_up", "w_down")
    col_sharded = ("w_in", "w_gate", "w_up")

    def gather_start(group, name, deps=()):
        shards = [ws[n][0].astype(BF16) for n in group]
        lands = [lax.empty((N_DEV,) + t.shape, BF16) for t in shards]
        return _exchange_start(shards, lands, False, name, deps)

    def gather_finish(group, started, name, after):
        full = {}
        for n, own, t in zip(group, *_exchange_wait(started, name, after)):
            t = lax.dynamic_update_slice(t, own[None], (me, 0, 0))
            if n in col_sharded:
                full[n] = jnp.transpose(t, (1, 0, 2)).reshape(t.shape[1], N_DEV * t.shape[2])
            else:
                full[n] = t.reshape(N_DEV * t.shape[1], t.shape[2])
        return full

    w_in_all = _all_gather_two_level(ws["w_in"][0].astype(BF16), "gather_w_in")
    full = {"w_in": jnp.pad(jnp.transpose(w_in_all, (1, 0, 2)).reshape(D_MODEL, IN_WIDTH),
                            ((0, 0), (0, IN_PAD - IN_WIDTH)))}
    late = ("w_out", "w_gate", "w_up", "w_down")
    started_b = gather_start(late, "gather_late_start", deps=(full["w_in"],))

    def late_weights(after):
        return gather_finish(late, started_b, "gather_late_wait", after)

    small_sharded = ("gf_up", "gb_up", "conv_w")
    sm = _all_gather_vmem(_pack([ws[n][0] for n in small_sharded]), "gather_small")
    shard_shapes = [ws[n][0].shape for n in small_sharded]
    per_dev = [_unpack(sm[d], shard_shapes) for d in range(N_DEV)]
    for i, n in enumerate(small_sharded):
        full[n] = jnp.concatenate([per_dev[d][i] for d in range(N_DEV)], axis=1)
    for n in ("norm1_g", "gf_b", "gb_b", "gla_norm_g", "attn_norm_g", "norm2_g", "conv_b"):
        full[n] = ws[n]
    full["final_norm_g"] = final_norm_g.reshape(1, D_MODEL)

    in_flight = []

    def grad_sink(group, grads):
        partials = []
        for n, t in zip(group, grads):
            if n == "w_in":
                t = t[:, :IN_WIDTH].astype(BF16)
            if n in col_sharded:
                t = jnp.transpose(t.reshape(t.shape[0], N_DEV, t.shape[1] // N_DEV), (1, 0, 2))
            else:
                t = t.reshape(N_DEV, t.shape[0] // N_DEV, t.shape[1])
            partials.append(t)
        lands = [lax.empty(t.shape, t.dtype) for t in partials]
        started = _exchange_start(partials, lands, True, "exchange_" + "_".join(group) + "_start")
        in_flight.append((group, started))
        return (started["token"],)

    loss_acc, grad_x, g = _local_step(x[0], loss_target[0], full, late_weights, grad_sink,
                                      first_dep=(started_b["token"],))

    out = {}
    for group, started in in_flight:
        sent, landed = _exchange_wait(started, "exchange_" + "_".join(group) + "_wait", grad_x)
        for n, parts, own in zip(group, landed, sent):
            out[n] = _adamw_sum(parts, ws[n][0], ms[n][0], vs[n][0], 64, "adamw_" + n, own=own, me=me)

    small_full_shapes = [g[n].shape for n in _SMALL]
    gsmall = _pack([g[n] for n in _SMALL] + [loss_acc[0:1, 0:1]])
    gathered_small = _all_gather_vmem(gsmall, "gather_small_grads")

    def full_small(d):
        parts = []
        for n in _SMALL:
            t = d[n].reshape(d[n].shape[-2:]) if d[n].ndim == 3 else d[n].reshape(1, -1)
            if n in small_sharded:
                wide = jnp.zeros((t.shape[0], t.shape[1] * N_DEV), F32)
                t = lax.dynamic_update_slice_in_dim(wide, t, me * t.shape[1], axis=1)
            parts.append(t)
        return _pack(parts + [jnp.zeros((1, 1), F32)])

    rows = gsmall.shape[0]
    res_small = _adamw_sum(gathered_small, full_small(ws), full_small(ms), full_small(vs), rows, "adamw_small")
    loss = res_small[0].reshape(-1)[sum(int(np.prod(sh)) for sh in small_full_shapes)]
    unpacked = [_unpack(t, small_full_shapes) for t in res_small]
    for i, n in enumerate(_SMALL):
        vals = [u[i] for u in unpacked]
        if n in small_sharded:
            width = vals[0].shape[1] // N_DEV
            vals = [lax.dynamic_slice_in_dim(t, me * width, width, axis=1) for t in vals]
        out[n] = vals

    result = [loss, grad_x[None]]
    for kind in range(4):
        for n in names:
            result.append(out[n][kind].reshape(ws[n].shape))
    return tuple(result)
```

```python
import functools

import numpy as np
import jax
import jax.numpy as jnp
from jax import lax
from jax.experimental import pallas as pl
from jax.experimental.pallas import tpu as pltpu

F32 = jnp.float32
BF16 = jnp.bfloat16

D_MODEL = 2048
ATTN_W = 1024
ATTN_HEADS = 8
HEAD_DIM = 128
ROPE_DIM = 32
ROPE_THETA = 500000.0
DILATIONS = (1, 4, 16)
N_SIDE = 64
GLA_KW = 512
GLA_VW = 1024
GLA_HEADS = 4
GLA_DK = 128
GLA_DV = 256
GLA_RANK = 16
GLA_GATE_NORM = 16.0
GLA_CHUNK = 64
IN_WIDTH = 6176
IN_PAD = 6400
D_FF = 5632
EPS = 1e-6
N_DEV = 8

OFF_AQ, OFF_AK, OFF_AV = 0, 1024, 2048
OFF_GQ, OFF_GK, OFF_GV, OFF_GR, OFF_Z = 3072, 3584, 4096, 5120, 6144

ADAM_LR, ADAM_B1, ADAM_B2, ADAM_EPS, ADAM_WD, ADAM_STEP = 0.001, 0.9, 0.999, 1e-08, 0.01, 10

LANES = 128
SUBLANES = 8
VMEM_LIMIT = 56 * 1024 * 1024
ROW_BLOCK = 256
ATTN_BLOCK = 128
GLA_CHUNKS_PER_STEP = 4
NEG = -1e30
MESH_ID = pl.DeviceIdType.MESH


def _params(sem):
    return pltpu.CompilerParams(dimension_semantics=sem, vmem_limit_bytes=VMEM_LIMIT)


def _dot(a, b):
    return lax.dot_general(a, b, (((1,), (0,)), ((), ())), preferred_element_type=F32)


def _dot_nt(a, b):
    return lax.dot_general(a, b, (((1,), (1,)), ((), ())), preferred_element_type=F32)


def _dot_tn(a, b):
    return lax.dot_general(a, b, (((0,), (0,)), ((), ())), preferred_element_type=F32)


def _sigmoid(x):
    return 1.0 / (1.0 + jnp.exp(-x))


def _matmul(pairs, mode, out_dtype, tm, tn, tk, name, res=None, deps=()):
    a0, b0 = pairs[0]
    if mode == "nn":
        (m, kdim), n = a0.shape, b0.shape[1]
    elif mode == "nt":
        (m, kdim), n = a0.shape, b0.shape[0]
    else:
        (kdim, m), n = a0.shape, b0.shape[1]
    assert m % tm == 0 and n % tn == 0 and kdim % tk == 0, (name, m, n, kdim)
    nk = kdim // tk
    npairs = len(pairs)
    steps = nk * npairs
    dot = {"nn": _dot, "nt": _dot_nt, "tn": _dot_tn}[mode]

    def kidx(p):
        return lambda k: jnp.clip(k - p * nk, 0, nk - 1)

    in_specs, args = [], []
    for p, (a, b) in enumerate(pairs):
        kk = kidx(p)
        if mode == "nn":
            in_specs += [pl.BlockSpec((tm, tk), lambda i, j, k, kk=kk: (i, kk(k))),
                         pl.BlockSpec((tk, tn), lambda i, j, k, kk=kk: (kk(k), j))]
        elif mode == "nt":
            in_specs += [pl.BlockSpec((tm, tk), lambda i, j, k, kk=kk: (i, kk(k))),
                         pl.BlockSpec((tn, tk), lambda i, j, k, kk=kk: (j, kk(k)))]
        else:
            in_specs += [pl.BlockSpec((tk, tm), lambda i, j, k, kk=kk: (kk(k), i)),
                         pl.BlockSpec((tk, tn), lambda i, j, k, kk=kk: (kk(k), j))]
        args += [a, b]
    if res is not None:
        in_specs.append(pl.BlockSpec((tm, tn), lambda i, j, k: (i, j)))
        args.append(res)
    in_specs += [pl.BlockSpec(memory_space=pl.ANY)] * len(deps)
    args += list(deps)

    def body(*refs):
        ab = refs[:2 * npairs]
        res_ref = refs[2 * npairs] if res is not None else None
        o_ref = refs[2 * npairs + (1 if res is not None else 0) + len(deps)]

        def finish(acc):
            if res_ref is not None:
                acc = acc + res_ref[...]
            o_ref[...] = acc.astype(out_dtype)

        if steps == 1:
            finish(dot(ab[0][...], ab[1][...]))
            return
        acc_ref = refs[-1]
        k = pl.program_id(2)

        @pl.when(k == 0)
        def _():
            acc_ref[...] = jnp.zeros_like(acc_ref)

        for p in range(npairs):
            @pl.when((k >= p * nk) & (k < (p + 1) * nk))
            def _(p=p):
                acc_ref[...] += dot(ab[2 * p][...], ab[2 * p + 1][...])

        @pl.when(k == steps - 1)
        def _():
            finish(acc_ref[...])

    return pl.pallas_call(
        body, name=name,
        grid=(m // tm, n // tn, steps),
        in_specs=in_specs,
        out_specs=pl.BlockSpec((tm, tn), lambda i, j, k: (i, j)),
        out_shape=jax.ShapeDtypeStruct((m, n), out_dtype),
        scratch_shapes=[] if steps == 1 else [pltpu.VMEM((tm, tn), F32)],
        compiler_params=_params(("parallel", "parallel", "arbitrary")),
    )(*args)


def _rms_fwd(x, g, name):
    s, d = x.shape

    def body(x_ref, g_ref, o_ref):
        xv = x_ref[...]
        r = lax.rsqrt(jnp.mean(xv * xv, axis=-1, keepdims=True) + EPS)
        o_ref[...] = (xv * r * g_ref[...]).astype(BF16)

    return pl.pallas_call(
        body, name=name, grid=(s // ROW_BLOCK,),
        in_specs=[pl.BlockSpec((ROW_BLOCK, d), lambda i: (i, 0)), pl.BlockSpec((1, d), lambda i: (0, 0))],
        out_specs=pl.BlockSpec((ROW_BLOCK, d), lambda i: (i, 0)),
        out_shape=jax.ShapeDtypeStruct((s, d), BF16),
        compiler_params=_params(("parallel",)),
    )(x, g)


def _rms_bwd(dn, x, g, dres, name):
    s, d = x.shape

    def body(dn_ref, x_ref, g_ref, dres_ref, dx_ref, dxb_ref, gg_ref):
        i = pl.program_id(0)
        xv, dnv = x_ref[...], dn_ref[...]
        r = lax.rsqrt(jnp.mean(xv * xv, axis=-1, keepdims=True) + EPS)
        dng = dnv * g_ref[...]
        c = jnp.mean(dng * xv, axis=-1, keepdims=True)
        dx = dres_ref[...] + r * dng - xv * (r * r * r * c)
        dx_ref[...] = dx
        dxb_ref[...] = dx.astype(BF16)

        @pl.when(i == 0)
        def _():
            gg_ref[...] = jnp.zeros_like(gg_ref)

        gg_ref[...] += jnp.sum(dnv * xv * r, axis=0, keepdims=True)

    row = pl.BlockSpec((ROW_BLOCK, d), lambda i: (i, 0))
    vec = pl.BlockSpec((1, d), lambda i: (0, 0))
    return pl.pallas_call(
        body, name=name, grid=(s // ROW_BLOCK,),
        in_specs=[row, row, vec, row],
        out_specs=[row, row, vec],
        out_shape=[jax.ShapeDtypeStruct((s, d), F32), jax.ShapeDtypeStruct((s, d), BF16),
                   jax.ShapeDtypeStruct((1, d), F32)],
        compiler_params=_params(("arbitrary",)),
    )(dn, x, g, dres)


def _final_loss(h2, target, g, name="final_loss"):
    s, d = h2.shape

    def body(h_ref, t_ref, g_ref, dh_ref, dhb_ref, loss_ref, gg_ref):
        i = pl.program_id(0)
        hv, gv = h_ref[...], g_ref[...]
        r = lax.rsqrt(jnp.mean(hv * hv, axis=-1, keepdims=True) + EPS)
        e = hv * r * gv - t_ref[...]
        dy = e * (1.0 / d)
        dyg = dy * gv
        c = jnp.mean(dyg * hv, axis=-1, keepdims=True)
        dh = r * dyg - hv * (r * r * r * c)
        dh_ref[...] = dh
        dhb_ref[...] = dh.astype(BF16)

        @pl.when(i == 0)
        def _():
            gg_ref[...] = jnp.zeros_like(gg_ref)
            loss_ref[...] = jnp.zeros_like(loss_ref)

        gg_ref[...] += jnp.sum(dy * hv * r, axis=0, keepdims=True)
        loss_ref[...] += jnp.sum(jnp.sum(e * e, axis=-1, keepdims=True), axis=0, keepdims=True) * (0.5 / d)

    row = pl.BlockSpec((ROW_BLOCK, d), lambda i: (i, 0))
    vec = pl.BlockSpec((1, d), lambda i: (0, 0))
    return pl.pallas_call(
        body, name=name, grid=(s // ROW_BLOCK,),
        in_specs=[row, row, vec],
        out_specs=[row, row, pl.BlockSpec((SUBLANES, LANES), lambda i: (0, 0)), vec],
        out_shape=[jax.ShapeDtypeStruct((s, d), F32), jax.ShapeDtypeStruct((s, d), BF16),
                   jax.ShapeDtypeStruct((SUBLANES, LANES), F32), jax.ShapeDtypeStruct((1, d), F32)],
        compiler_params=_params(("arbitrary",)),
    )(h2, target, g)


def _rope_tables(s):
    pos = jnp.arange(s, dtype=F32)
    inv_freq = ROPE_THETA ** (-jnp.arange(0, ROPE_DIM, 2, dtype=F32) / ROPE_DIM)
    ang = pos[:, None] * inv_freq[None, :]
    cos, sin = jnp.cos(ang), jnp.sin(ang)
    half = ROPE_DIM // 2
    rest = HEAD_DIM - ROPE_DIM
    c = jnp.concatenate([cos, cos, jnp.ones((s, rest), F32)], axis=1)
    sm = jnp.concatenate([-sin, jnp.zeros((s, half + rest), F32)], axis=1)
    sp = jnp.concatenate([jnp.zeros((s, half), F32), sin, jnp.zeros((s, rest), F32)], axis=1)
    return c, sm, sp


def _res_shape(s, groups, dil, dtype):
    return jax.ShapeDtypeStruct((s // dil, dil * groups * LANES), dtype)


def _res_spec(groups, dil):
    return pl.BlockSpec((ROW_BLOCK // dil, dil * groups * LANES), lambda i: (i, 0))


def _to_residues(scr, o_ref, dil):
    groups, rows = scr.shape[0], ROW_BLOCK // dil
    for r in range(dil):
        for h in range(groups):
            piece = scr[h] if dil == 1 else scr.at[h][pl.ds(r, rows, stride=dil), :]
            o_ref[:, (r * groups + h) * LANES:(r * groups + h + 1) * LANES] = piece.astype(o_ref.dtype)


def _from_residues(i_ref, scr, dil):
    groups, rows = scr.shape[0], ROW_BLOCK // dil
    for r in range(dil):
        for h in range(groups):
            piece = i_ref[:, (r * groups + h) * LANES:(r * groups + h + 1) * LANES].astype(F32)
            if dil == 1:
                scr[h] = piece
            else:
                scr.at[h][pl.ds(r, rows, stride=dil), :] = piece


def _rope_fwd(proj, tables, name="rope_fwd"):
    s = proj.shape[0]
    half = ROPE_DIM // 2
    nd = len(DILATIONS)

    def body(p_ref, c_ref, sm_ref, sp_ref, *rest):
        outs, scr = rest[:3 * nd], rest[3 * nd]
        c, sm, sp = c_ref[...], sm_ref[...], sp_ref[...]
        for gi, off in enumerate((OFF_AQ, OFF_AK, OFF_AV)):
            for h in range(ATTN_HEADS):
                t = p_ref[:, off + h * HEAD_DIM: off + (h + 1) * HEAD_DIM]
                if off != OFF_AV:
                    t = t * c + pltpu.roll(t, HEAD_DIM - half, 1) * sm + pltpu.roll(t, half, 1) * sp
                scr[h] = t
            for di, dil in enumerate(DILATIONS):
                _to_residues(scr, outs[3 * di + gi], dil)

    tab = pl.BlockSpec((ROW_BLOCK, HEAD_DIM), lambda i: (i, 0))
    outs = pl.pallas_call(
        body, name=name, grid=(s // ROW_BLOCK,),
        in_specs=[pl.BlockSpec((ROW_BLOCK, 3 * ATTN_W), lambda i: (i, 0)), tab, tab, tab],
        out_specs=[_res_spec(ATTN_HEADS, d) for d in DILATIONS for _ in range(3)],
        out_shape=[_res_shape(s, ATTN_HEADS, d, BF16) for d in DILATIONS for _ in range(3)],
        scratch_shapes=[pltpu.VMEM((ATTN_HEADS, ROW_BLOCK, LANES), F32)],
        compiler_params=_params(("parallel",)),
    )(proj, *tables)
    return [tuple(outs[3 * di:3 * di + 3]) for di in range(nd)]


def _rope_bwd(grads, tables, name="rope_bwd"):
    s = grads[0][0].shape[0] * DILATIONS[0]
    half = ROPE_DIM // 2
    nd = len(DILATIONS)

    def body(*refs):
        ins = refs[:3 * nd]
        c_ref, sm_ref, sp_ref, o_ref = refs[3 * nd:3 * nd + 4]
        scrs = refs[3 * nd + 4:]
        c, sm, sp = c_ref[...], sm_ref[...], sp_ref[...]
        for gi, off in enumerate((OFF_AQ, OFF_AK, OFF_AV)):
            for di, dil in enumerate(DILATIONS):
                _from_residues(ins[3 * di + gi], scrs[di], dil)
            for h in range(ATTN_HEADS):
                t = scrs[0][h]
                for scr in scrs[1:]:
                    t = t + scr[h]
                if off != OFF_AV:
                    t = t * c + pltpu.roll(t * sm, half, 1) + pltpu.roll(t * sp, HEAD_DIM - half, 1)
                o_ref[:, off + h * HEAD_DIM: off + (h + 1) * HEAD_DIM] = t.astype(BF16)

    tab = pl.BlockSpec((ROW_BLOCK, HEAD_DIM), lambda i: (i, 0))
    return pl.pallas_call(
        body, name=name, grid=(s // ROW_BLOCK,),
        in_specs=[_res_spec(ATTN_HEADS, d) for d in DILATIONS for _ in range(3)] + [tab, tab, tab],
        out_specs=pl.BlockSpec((ROW_BLOCK, 3 * ATTN_W), lambda i: (i, 0)),
        out_shape=jax.ShapeDtypeStruct((s, IN_PAD), BF16),
        scratch_shapes=[pltpu.VMEM((ATTN_HEADS, ROW_BLOCK, LANES), F32) for _ in DILATIONS],
        compiler_params=_params(("parallel",)),
    )(*[t for g in grads for t in g], *tables)


def _window_specs(nb, width):
    qb, hb = ATTN_BLOCK, N_SIDE
    cur = pl.BlockSpec((qb, width), lambda r, j: (j, r))
    prev = pl.BlockSpec((hb, width), lambda r, j: (jnp.maximum(2 * j - 1, 0), r))
    nxt = pl.BlockSpec((hb, width), lambda r, j: (jnp.minimum(2 * j + 2, 2 * nb - 1), r))
    return prev, cur, nxt


def _band_masks(j, length):
    qb, hb = ATTN_BLOCK, N_SIDE
    row = lax.broadcasted_iota(jnp.int32, (qb, qb), 0)
    col = lax.broadcasted_iota(jnp.int32, (qb, qb), 1)

    def edge_pos(i):
        return j * qb - hb + i + jnp.where(i >= hb, qb, 0)

    def ok(a, b, outside):
        return (jnp.abs(a - b) <= N_SIDE) & (outside >= 0) & (outside < length)

    cur = jnp.abs(row - col) <= N_SIDE
    edge_k = ok(j * qb + row, edge_pos(col), edge_pos(col))
    edge_q = ok(edge_pos(row), j * qb + col, edge_pos(row))
    return cur, edge_k, edge_q


def _edge(prev_ref, next_ref, sl):
    return jnp.concatenate([prev_ref[:, sl], next_ref[:, sl]], axis=0)


def _attn_fwd(q, k, v, dil, name):
    length = q.shape[0]
    qb = ATTN_BLOCK
    nb = length // qb
    scale = HEAD_DIM ** -0.5

    def body(q_ref, kp_ref, kc_ref, kn_ref, vp_ref, vc_ref, vn_ref, o_ref, lse_ref):
        valid_c, valid_e, _ = _band_masks(pl.program_id(1), length)
        lane = lax.broadcasted_iota(jnp.int32, (qb, LANES), 1)
        lse_acc = jnp.zeros((qb, LANES), F32)
        heads = [slice(h * HEAD_DIM, (h + 1) * HEAD_DIM) for h in range(ATTN_HEADS)]
        scores = [(_dot_nt(q_ref[:, sl], kc_ref[:, sl]), _dot_nt(q_ref[:, sl], _edge(kp_ref, kn_ref, sl)))
                  for sl in heads]
        probs = []
        for h, (s_c, s_e) in enumerate(scores):
            s_c = jnp.where(valid_c, s_c * scale, NEG)
            s_e = jnp.where(valid_e, s_e * scale, NEG)
            m = jnp.max(jnp.maximum(s_c, s_e), axis=-1, keepdims=True)
            p_c, p_e = jnp.exp(s_c - m), jnp.exp(s_e - m)
            den = jnp.sum(p_c + p_e, axis=-1, keepdims=True)
            probs.append((p_c.astype(BF16), p_e.astype(BF16), 1.0 / den))
            lse_acc = jnp.where(lane == h, m + jnp.log(den), lse_acc)
        for sl, (p_c, p_e, inv) in zip(heads, probs):
            o_ref[:, sl] = (_dot(p_c, vc_ref[:, sl]) + _dot(p_e, _edge(vp_ref, vn_ref, sl))) * inv
        lse_ref[...] = lse_acc

    prev, cur, nxt = _window_specs(nb, ATTN_W)
    return pl.pallas_call(
        body, name=name, grid=(dil, nb),
        in_specs=[cur, prev, cur, nxt, prev, cur, nxt],
        out_specs=[cur, pl.BlockSpec((qb, LANES), lambda r, j: (j, r))],
        out_shape=[jax.ShapeDtypeStruct((length, dil * ATTN_W), F32),
                   jax.ShapeDtypeStruct((length, dil * LANES), F32)],
        compiler_params=_params(("parallel", "parallel")),
    )(q, k, k, k, v, v, v)


def _attn_combine(outs, lses, g, name="attn_combine"):
    s = outs[0].shape[0] * DILATIONS[0]
    nd = len(DILATIONS)

    def body(*refs):
        o_refs, l_refs = refs[:nd], refs[nd:2 * nd]
        g_ref, o_ref, n_ref = refs[2 * nd:2 * nd + 3]
        lse_outs = refs[2 * nd + 3:3 * nd + 3]
        o_scr, l_scr = refs[3 * nd + 3:4 * nd + 3], refs[4 * nd + 3:5 * nd + 3]
        for di, dil in enumerate(DILATIONS):
            _from_residues(o_refs[di], o_scr[di], dil)
            _from_residues(l_refs[di], l_scr[di], dil)
        ls = [scr[0] for scr in l_scr]
        m = ls[0]
        for l in ls[1:]:
            m = jnp.maximum(m, l)
        es = [jnp.exp(l - m) for l in ls]
        z = es[0]
        for e in es[1:]:
            z = z + e
        ws = [e / z for e in es]
        l_scr[0][0] = m + jnp.log(z)
        for di, dil in enumerate(DILATIONS):
            _to_residues(l_scr[0], lse_outs[di], dil)
        ssq = jnp.zeros((ROW_BLOCK, 1), F32)
        for h in range(ATTN_HEADS):
            sl = slice(h * HEAD_DIM, (h + 1) * HEAD_DIM)
            acc = ws[0][:, h:h + 1] * o_scr[0][h]
            for w, scr in zip(ws[1:], o_scr[1:]):
                acc = acc + w[:, h:h + 1] * scr[h]
            o_ref[:, sl] = acc
            ssq = ssq + jnp.sum(acc * acc, axis=-1, keepdims=True)
        r = lax.rsqrt(ssq * (1.0 / ATTN_W) + EPS)
        n_ref[...] = (o_ref[...] * r * g_ref[...]).astype(BF16)

    blk = pl.BlockSpec((ROW_BLOCK, ATTN_W), lambda i: (i, 0))
    outs_ = pl.pallas_call(
        body, name=name, grid=(s // ROW_BLOCK,),
        in_specs=[_res_spec(ATTN_HEADS, d) for d in DILATIONS] + [_res_spec(1, d) for d in DILATIONS]
        + [pl.BlockSpec((1, ATTN_W), lambda i: (0, 0))],
        out_specs=[blk, blk] + [_res_spec(1, d) for d in DILATIONS],
        out_shape=[jax.ShapeDtypeStruct((s, ATTN_W), F32), jax.ShapeDtypeStruct((s, ATTN_W), BF16)]
        + [_res_shape(s, 1, d, F32) for d in DILATIONS],
        scratch_shapes=[pltpu.VMEM((ATTN_HEADS, ROW_BLOCK, LANES), F32) for _ in DILATIONS]
        + [pltpu.VMEM((1, ROW_BLOCK, LANES), F32) for _ in DILATIONS],
        compiler_params=_params(("parallel",)),
    )(*outs, *lses, g)
    return outs_[0], outs_[1], list(outs_[2:])


def _attn_prebwd(dcat, o, g, name="attn_prebwd"):
    s = o.shape[0]
    nd = len(DILATIONS)

    def body(dy_ref, o_ref, g_ref, *rest):
        do_outs, delta_outs, gg_ref = rest[:nd], rest[nd:2 * nd], rest[2 * nd]
        do_scr, delta_scr = rest[2 * nd + 1], rest[2 * nd + 2]
        i = pl.program_id(0)
        dy, ov = dy_ref[...], o_ref[...]
        r = lax.rsqrt(jnp.mean(ov * ov, axis=-1, keepdims=True) + EPS)
        dyg = dy * g_ref[...]
        c = jnp.mean(dyg * ov, axis=-1, keepdims=True)
        do = r * dyg - ov * (r * r * r * c)
        prod = do * ov
        lane = lax.broadcasted_iota(jnp.int32, (ROW_BLOCK, LANES), 1)
        acc = jnp.zeros((ROW_BLOCK, LANES), F32)
        for h in range(ATTN_HEADS):
            sl = slice(h * HEAD_DIM, (h + 1) * HEAD_DIM)
            do_scr[h] = do[:, sl]
            acc = jnp.where(lane == h, jnp.sum(prod[:, sl], axis=-1, keepdims=True), acc)
        delta_scr[0] = acc
        for di, dil in enumerate(DILATIONS):
            _to_residues(do_scr, do_outs[di], dil)
            _to_residues(delta_scr, delta_outs[di], dil)

        @pl.when(i == 0)
        def _():
            gg_ref[...] = jnp.zeros_like(gg_ref)

        gg_ref[...] += jnp.sum(dy * ov * r, axis=0, keepdims=True)

    blk = pl.BlockSpec((ROW_BLOCK, ATTN_W), lambda i: (i, 0))
    vec = pl.BlockSpec((1, ATTN_W), lambda i: (0, 0))
    outs = pl.pallas_call(
        body, name=name, grid=(s // ROW_BLOCK,),
        in_specs=[blk, blk, vec],
        out_specs=[_res_spec(ATTN_HEADS, d) for d in DILATIONS] + [_res_spec(1, d) for d in DILATIONS] + [vec],
        out_shape=[_res_shape(s, ATTN_HEADS, d, BF16) for d in DILATIONS]
        + [_res_shape(s, 1, d, F32) for d in DILATIONS] + [jax.ShapeDtypeStruct((1, ATTN_W), F32)],
        scratch_shapes=[pltpu.VMEM((ATTN_HEADS, ROW_BLOCK, LANES), F32), pltpu.VMEM((1, ROW_BLOCK, LANES), F32)],
        compiler_params=_params(("arbitrary",)),
    )(dcat, o, g)
    return list(outs[:nd]), list(outs[nd:2 * nd]), outs[2 * nd]


def _attn_bwd(q, k, v, do, lse, delta, dil, name):
    length = q.shape[0]
    qb = ATTN_BLOCK
    nb = length // qb
    scale = HEAD_DIM ** -0.5

    def body(qp, qc, qn, kp, kc, kn, vp, vc, vn, dop, doc, don, lp, lc, ln, dp, dc, dn, dq_ref, dk_ref, dv_ref):
        valid_c, valid_ek, valid_eq = _band_masks(pl.program_id(1), length)
        everything = slice(None)
        lse_e, del_e = _edge(lp, ln, everything), _edge(dp, dn, everything)
        heads = [slice(h * HEAD_DIM, (h + 1) * HEAD_DIM) for h in range(ATTN_HEADS)]
        prods = []
        for sl in heads:
            q_c, k_c, v_c, do_c = qc[:, sl], kc[:, sl], vc[:, sl], doc[:, sl]
            q_e, k_e, v_e, do_e = _edge(qp, qn, sl), _edge(kp, kn, sl), _edge(vp, vn, sl), _edge(dop, don, sl)
            prods.append((_dot_nt(q_c, k_c), _dot_nt(do_c, v_c), _dot_nt(q_c, k_e), _dot_nt(do_c, v_e),
                          _dot_nt(q_e, k_c), _dot_nt(do_e, v_c)))
        parts = []
        for h, (s_cc, dp_cc, s_ek, dp_ek, s_eq, dp_eq) in enumerate(prods):
            hc = slice(h, h + 1)
            lse_c, del_c = lc[:, hc], dc[:, hc]
            p_cc = jnp.where(valid_c, jnp.exp(s_cc * scale - lse_c), 0.0)
            ds_cc = (p_cc * (dp_cc - del_c)).astype(BF16)
            p_ek = jnp.where(valid_ek, jnp.exp(s_ek * scale - lse_c), 0.0)
            ds_ek = (p_ek * (dp_ek - del_c)).astype(BF16)
            p_eq = jnp.where(valid_eq, jnp.exp(s_eq * scale - lse_e[:, hc]), 0.0)
            ds_eq = (p_eq * (dp_eq - del_e[:, hc])).astype(BF16)
            parts.append((p_cc.astype(BF16), ds_cc, ds_ek, p_eq.astype(BF16), ds_eq))
        for sl, (p_cc, ds_cc, ds_ek, p_eq, ds_eq) in zip(heads, parts):
            q_c, k_c, do_c = qc[:, sl], kc[:, sl], doc[:, sl]
            q_e, k_e, do_e = _edge(qp, qn, sl), _edge(kp, kn, sl), _edge(dop, don, sl)
            dq_ref[:, sl] = (_dot(ds_cc, k_c) + _dot(ds_ek, k_e)) * scale
            dk_ref[:, sl] = (_dot_tn(ds_cc, q_c) + _dot_tn(ds_eq, q_e)) * scale
            dv_ref[:, sl] = _dot_tn(p_cc, do_c) + _dot_tn(p_eq, do_e)

    wide, narrow = list(_window_specs(nb, ATTN_W)), list(_window_specs(nb, LANES))
    return tuple(pl.pallas_call(
        body, name=name, grid=(dil, nb),
        in_specs=wide * 4 + narrow * 2,
        out_specs=[wide[1]] * 3,
        out_shape=[jax.ShapeDtypeStruct((length, dil * ATTN_W), F32)] * 3,
        compiler_params=_params(("parallel", "parallel")),
    )(q, q, q, k, k, k, v, v, v, do, do, do, lse, lse, lse, delta, delta, delta))


def _gate_matrices(gf_up, gb_up):
    pad = LANES - 2 * GLA_RANK
    uf = jnp.concatenate([gf_up, jnp.zeros((GLA_RANK + pad, GLA_KW), gf_up.dtype)], axis=0)
    ub = jnp.concatenate([jnp.zeros((GLA_RANK, GLA_KW), gb_up.dtype), gb_up, jnp.zeros((pad, GLA_KW), gb_up.dtype)], axis=0)
    return uf.astype(BF16), ub.astype(BF16)


def _log_sigmoid(x):
    return jnp.minimum(x, 0.0) - jnp.log(1.0 + jnp.exp(-jnp.abs(x)))


def _gla_gates(proj, uf, ub, gf_b, gb_b, name="gla_gates"):
    s = proj.shape[0]

    def body(z_ref, uf_ref, ub_ref, bf_ref, bb_ref, gf_ref, gb_ref):
        z = z_ref[...].astype(BF16)
        gf_ref[...] = _log_sigmoid(_dot(z, uf_ref[...]) + bf_ref[...]) * (1.0 / GLA_GATE_NORM)
        gb_ref[...] = _log_sigmoid(_dot(z, ub_ref[...]) + bb_ref[...]) * (1.0 / GLA_GATE_NORM)

    mat = pl.BlockSpec((LANES, GLA_KW), lambda i: (0, 0))
    vec = pl.BlockSpec((1, GLA_KW), lambda i: (0, 0))
    out = pl.BlockSpec((ROW_BLOCK, GLA_KW), lambda i: (i, 0))
    return pl.pallas_call(
        body, name=name, grid=(s // ROW_BLOCK,),
        in_specs=[pl.BlockSpec((ROW_BLOCK, LANES), lambda i: (i, OFF_Z // LANES)), mat, mat, vec, vec],
        out_specs=[out, out],
        out_shape=[jax.ShapeDtypeStruct((s, GLA_KW), F32)] * 2,
        compiler_params=_params(("parallel",)),
    )(proj, uf, ub, gf_b, gb_b)


def _gla_gates_bwd(dgf, dgb, proj, uf, ub, gf_b, gb_b, dproj, name="gla_gates_bwd"):
    s = proj.shape[0]
    tail = IN_PAD - OFF_Z

    def body(dgf_ref, dgb_ref, z_ref, uf_ref, ub_ref, bf_ref, bb_ref, _, dz_ref, guf_ref, gub_ref, gbf_ref, gbb_ref):
        i = pl.program_id(0)
        z = z_ref[...].astype(BF16)
        uf_, ub_ = uf_ref[...], ub_ref[...]
        dpf = dgf_ref[...] * (1.0 / GLA_GATE_NORM) * _sigmoid(-(_dot(z, uf_) + bf_ref[...]))
        dpb = dgb_ref[...] * (1.0 / GLA_GATE_NORM) * _sigmoid(-(_dot(z, ub_) + bb_ref[...]))
        dpf_b, dpb_b = dpf.astype(BF16), dpb.astype(BF16)
        dz_ref[:, 0:LANES] = (_dot_nt(dpf_b, uf_) + _dot_nt(dpb_b, ub_)).astype(BF16)
        dz_ref[:, LANES:tail] = jnp.zeros((ROW_BLOCK, tail - LANES), BF16)

        @pl.when(i == 0)
        def _():
            for r in (guf_ref, gub_ref, gbf_ref, gbb_ref):
                r[...] = jnp.zeros_like(r)

        guf_ref[...] += _dot_tn(z, dpf_b)
        gub_ref[...] += _dot_tn(z, dpb_b)
        gbf_ref[...] += jnp.sum(dpf, axis=0, keepdims=True)
        gbb_ref[...] += jnp.sum(dpb, axis=0, keepdims=True)

    mat = pl.BlockSpec((LANES, GLA_KW), lambda i: (0, 0))
    vec = pl.BlockSpec((1, GLA_KW), lambda i: (0, 0))
    blk = pl.BlockSpec((ROW_BLOCK, GLA_KW), lambda i: (i, 0))
    return pl.pallas_call(
        body, name=name, grid=(s // ROW_BLOCK,),
        in_specs=[blk, blk, pl.BlockSpec((ROW_BLOCK, LANES), lambda i: (i, OFF_Z // LANES)), mat, mat, vec, vec,
                  pl.BlockSpec(memory_space=pl.ANY)],
        out_specs=[pl.BlockSpec((ROW_BLOCK, tail), lambda i: (i, OFF_Z // tail)), mat, mat, vec, vec],
        out_shape=[jax.ShapeDtypeStruct(dproj.shape, dproj.dtype), jax.ShapeDtypeStruct((LANES, GLA_KW), F32),
                   jax.ShapeDtypeStruct((LANES, GLA_KW), F32), jax.ShapeDtypeStruct((1, GLA_KW), F32),
                   jax.ShapeDtypeStruct((1, GLA_KW), F32)],
        input_output_aliases={7: 0},
        compiler_params=_params(("arbitrary",)),
    )(dgf, dgb, proj, uf, ub, gf_b, gb_b, dproj)


def _split3(x):
    x1 = x.astype(BF16)
    r1 = x - x1.astype(F32)
    x2 = r1.astype(BF16)
    x3 = (r1 - x2.astype(F32)).astype(BF16)
    return x1, x2, x3


def _dot_exact(mask_bf, x):
    x1, x2, x3 = _split3(x)
    return _dot(mask_bf, x1) + _dot(mask_bf, x2) + _dot(mask_bf, x3)


def _chunk_masks(reverse):
    c = GLA_CHUNK
    row = lax.broadcasted_iota(jnp.int32, (c, c), 0)
    col = lax.broadcasted_iota(jnp.int32, (c, c), 1)
    allowed = (col >= row) if reverse else (col <= row)
    seen_by = (col <= row) if reverse else (col >= row)
    return allowed, seen_by


def _chunk_terms(q_ref, k_ref, g_ref, rs, hs, allowed, reverse):
    c = GLA_CHUNK
    mid, last = (c // 2, 0) if reverse else (c // 2 - 1, c - 1)
    q = q_ref[rs, hs] * (GLA_DK ** -0.5)
    k = k_ref[rs, hs]
    b = _dot_exact(jnp.where(allowed, 1.0, 0.0).astype(BF16), g_ref[rs, hs])
    bref, blast = b[mid:mid + 1, :], b[last:last + 1, :]
    e_q, e_k, e_in, e_st = jnp.exp(b - bref), jnp.exp(bref - b), jnp.exp(b), jnp.exp(blast - b)
    return dict(last=last, e_q=e_q, e_k=e_k, e_in=e_in, e_st=e_st,
                dec=jnp.exp(blast), qe=q * e_q, ke=k * e_k, qin=q * e_in, kst=k * e_st)


def _gla_blockspecs(s, reverse_order):
    cb = GLA_CHUNKS_PER_STEP
    rows = cb * GLA_CHUNK
    nsteps = s // rows

    def rb(n):
        return (nsteps - 1 - n) if reverse_order else n

    qspec = pl.BlockSpec((rows, GLA_KW), lambda n: (rb(n), OFF_GQ // GLA_KW))
    kspec = pl.BlockSpec((rows, GLA_KW), lambda n: (rb(n), OFF_GK // GLA_KW))
    vspec = pl.BlockSpec((rows, GLA_VW), lambda n: (rb(n), OFF_GV // GLA_VW))
    gspec = pl.BlockSpec((rows, GLA_KW), lambda n: (rb(n), 0))
    ospec = pl.BlockSpec((rows, GLA_VW), lambda n: (rb(n), 0))
    sspec = pl.BlockSpec((GLA_HEADS, cb, GLA_DV, GLA_DK), lambda n: (0, rb(n), 0, 0))
    return cb, rows, nsteps, qspec, kspec, vspec, gspec, ospec, sspec


def _gla_units(cb, order_reversed):
    chunks = list(reversed(range(cb))) if order_reversed else list(range(cb))
    return [(c, h, slice(c * GLA_CHUNK, (c + 1) * GLA_CHUNK), slice(h * GLA_DK, (h + 1) * GLA_DK),
             slice(h * GLA_DV, (h + 1) * GLA_DV)) for c in chunks for h in range(GLA_HEADS)]


def _gla_fwd(proj, g, reverse, name):
    s = proj.shape[0]
    cb, rows, nsteps, qspec, kspec, vspec, gspec, ospec, sspec = _gla_blockspecs(s, reverse)

    def body(q_ref, k_ref, v_ref, g_ref, o_ref, st_ref, state):
        @pl.when(pl.program_id(0) == 0)
        def _():
            state[...] = jnp.zeros_like(state)

        allowed, _ = _chunk_masks(reverse)
        units = _gla_units(cb, reverse)
        terms = [_chunk_terms(q_ref, k_ref, g_ref, rs, hs, allowed, reverse) for _, _, rs, hs, _ in units]
        vals = [v_ref[rs, vs].astype(BF16) for _, _, rs, _, vs in units]
        raw = [(_dot_nt(t["qe"].astype(BF16), t["ke"].astype(BF16)), _dot_tn(v, t["kst"].astype(BF16)))
               for t, v in zip(terms, vals)]
        intra = [_dot(jnp.where(allowed, a, 0.0).astype(BF16), v) for (a, _), v in zip(raw, vals)]
        st = [state[h] for h in range(GLA_HEADS)]
        for (c, h, rs, _, vs), t, (_, kv), o_in in zip(units, terms, raw, intra):
            st_ref[h, c] = st[h]
            o_ref[rs, vs] = o_in + _dot_nt(t["qin"].astype(BF16), st[h].astype(BF16))
            st[h] = st[h] * t["dec"] + kv
        for h in range(GLA_HEADS):
            state[h] = st[h]

    return pl.pallas_call(
        body, name=name, grid=(nsteps,),
        in_specs=[qspec, kspec, vspec, gspec],
        out_specs=[ospec, sspec],
        out_shape=[jax.ShapeDtypeStruct((s, GLA_VW), F32),
                   jax.ShapeDtypeStruct((GLA_HEADS, s // GLA_CHUNK, GLA_DV, GLA_DK), F32)],
        scratch_shapes=[pltpu.VMEM((GLA_HEADS, GLA_DV, GLA_DK), F32)],
        compiler_params=_params(("arbitrary",)),
    )(proj, proj, proj, g)


def _gla_bwd(proj, g, do, states, reverse, name, merge=None):
    s = proj.shape[0]
    cb, rows, nsteps, qspec, kspec, vspec, gspec, ospec, sspec = _gla_blockspecs(s, not reverse)
    gla_cols = OFF_Z - OFF_GQ

    def body(q_ref, k_ref, v_ref, g_ref, do_ref, sp_ref, *rest):
        if merge is None:
            dq_ref, dk_ref, dv_ref, dg_ref, dstate = rest
        else:
            dq_o, dk_o, dv_o, dgr_ref, _, dp_ref, dg_ref, dstate = rest
        @pl.when(pl.program_id(0) == 0)
        def _():
            dstate[...] = jnp.zeros_like(dstate)

        allowed, seen_by = _chunk_masks(reverse)
        units = _gla_units(cb, not reverse)
        terms = [_chunk_terms(q_ref, k_ref, g_ref, rs, hs, allowed, reverse) for _, _, rs, hs, _ in units]
        vals = [v_ref[rs, vs].astype(BF16) for _, _, rs, _, vs in units]
        dos = [do_ref[rs, vs] for _, _, rs, _, vs in units]
        prevs = [sp_ref[h, c] for c, h, _, _, _ in units]
        raw = [(_dot_nt(t["qe"].astype(BF16), t["ke"].astype(BF16)), _dot_nt(do, v),
                _dot(do, sp.astype(BF16)), _dot_tn(do, t["qin"].astype(BF16)))
               for t, v, do, sp in zip(terms, vals, dos, prevs)]
        inner = []
        for t, do, (a, da, _, _) in zip(terms, dos, raw):
            da = jnp.where(allowed, da, 0.0).astype(BF16)
            inner.append((_dot(da, t["ke"].astype(BF16)), _dot_tn(da, t["qe"].astype(BF16)),
                          _dot_tn(jnp.where(allowed, a, 0.0).astype(BF16), do)))
        ds = [dstate[h] for h in range(GLA_HEADS)]
        outer = []
        for (c, h, _, _, _), t, v, sp, (_, _, _, inc) in zip(units, terms, vals, prevs, raw):
            ds_b = ds[h].astype(BF16)
            outer.append((_dot(v, ds_b), _dot_nt(t["kst"].astype(BF16), ds_b),
                          jnp.sum(sp * ds[h], axis=0, keepdims=True)))
            ds[h] = ds[h] * t["dec"] + inc
        for h in range(GLA_HEADS):
            dstate[h] = ds[h]
        seen_bf = jnp.where(seen_by, 1.0, 0.0).astype(BF16)
        rowi = lax.broadcasted_iota(jnp.int32, (GLA_CHUNK, GLA_DK), 0)
        for (c, h, rs, hs, vs), t, (_, _, dqin, _), (dqe, dke, dv_in), (dkst, dv_out, ddec) in zip(
                units, terms, raw, inner, outer):
            dq = (dqe * t["e_q"] + dqin * t["e_in"]) * (GLA_DK ** -0.5)
            dk = dke * t["e_k"] + dkst * t["e_st"]
            if merge is None:
                dq_ref[rs, hs], dk_ref[rs, hs], dv_ref[rs, vs] = dq, dk, dv_in + dv_out
            else:
                lo = OFF_GK - OFF_GQ + h * GLA_DK
                dp_ref[rs, hs] = (dq + dq_o[rs, hs]).astype(BF16)
                dp_ref[rs, lo:lo + GLA_DK] = (dk + dk_o[rs, hs]).astype(BF16)
                lo = OFF_GV - OFF_GQ + h * GLA_DV
                dp_ref[rs, lo:lo + GLA_DV] = (dv_in + dv_out + dv_o[rs, vs]).astype(BF16)
            kk = dkst * t["kst"]
            db = dqe * t["qe"] - dke * t["ke"] + dqin * t["qin"] - kk
            extra = jnp.sum(kk, axis=0, keepdims=True) + ddec * t["dec"]
            db = db + jnp.where(rowi == t["last"], extra, 0.0)
            dg_ref[rs, hs] = _dot_exact(seen_bf, db)
        if merge is not None:
            dp_ref[:, OFF_GR - OFF_GQ:gla_cols] = dgr_ref[...]

    scratch = [pltpu.VMEM((GLA_HEADS, GLA_DV, GLA_DK), F32)]
    if merge is None:
        return pl.pallas_call(
            body, name=name, grid=(nsteps,),
            in_specs=[qspec, kspec, vspec, gspec, ospec, sspec],
            out_specs=[gspec, gspec, ospec, gspec],
            out_shape=[jax.ShapeDtypeStruct((s, GLA_KW), F32), jax.ShapeDtypeStruct((s, GLA_KW), F32),
                       jax.ShapeDtypeStruct((s, GLA_VW), F32), jax.ShapeDtypeStruct((s, GLA_KW), F32)],
            scratch_shapes=scratch,
            compiler_params=_params(("arbitrary",)),
        )(proj, proj, proj, g, do, states)
    dproj = merge[4]
    block = gspec.index_map
    return pl.pallas_call(
        body, name=name, grid=(nsteps,),
        in_specs=[qspec, kspec, vspec, gspec, ospec, sspec, gspec, gspec, ospec, ospec, _ANY],
        out_specs=[pl.BlockSpec((rows, gla_cols), lambda n: (block(n)[0], OFF_GQ // gla_cols)), gspec],
        out_shape=[jax.ShapeDtypeStruct(dproj.shape, dproj.dtype), jax.ShapeDtypeStruct((s, GLA_KW), F32)],
        input_output_aliases={10: 0},
        scratch_shapes=scratch,
        compiler_params=_params(("arbitrary",)),
    )(proj, proj, proj, g, do, states, *merge)


def _gla_post(o_f, o_b, proj, g, name="gla_post"):
    s = o_f.shape[0]

    def body(of_ref, ob_ref, gr_ref, g_ref, o_ref):
        gv = g_ref[...]
        for h in range(GLA_HEADS):
            sl = slice(h * GLA_DV, (h + 1) * GLA_DV)
            osum = of_ref[:, sl] + ob_ref[:, sl]
            r = lax.rsqrt(jnp.mean(osum * osum, axis=-1, keepdims=True) + EPS)
            gr = gr_ref[:, sl]
            o_ref[:, sl] = (osum * r * gv * (gr * _sigmoid(gr))).astype(BF16)

    blk = pl.BlockSpec((ROW_BLOCK, GLA_VW), lambda i: (i, 0))
    return pl.pallas_call(
        body, name=name, grid=(s // ROW_BLOCK,),
        in_specs=[blk, blk, pl.BlockSpec((ROW_BLOCK, GLA_VW), lambda i: (i, OFF_GR // GLA_VW)),
                  pl.BlockSpec((1, GLA_DV), lambda i: (0, 0))],
        out_specs=blk,
        out_shape=jax.ShapeDtypeStruct((s, GLA_VW), BF16),
        compiler_params=_params(("parallel",)),
    )(o_f, o_b, proj, g)


def _gla_post_bwd(dcat, o_f, o_b, proj, g, name="gla_post_bwd"):
    s = o_f.shape[0]

    def body(dy_ref, of_ref, ob_ref, gr_ref, g_ref, do_ref, dgr_ref, gg_ref):
        i = pl.program_id(0)
        gv = g_ref[...]
        gg = jnp.zeros((1, GLA_DV), F32)
        for h in range(GLA_HEADS):
            sl = slice(h * GLA_DV, (h + 1) * GLA_DV)
            osum = of_ref[:, sl] + ob_ref[:, sl]
            r = lax.rsqrt(jnp.mean(osum * osum, axis=-1, keepdims=True) + EPS)
            gr, dy = gr_ref[:, sl], dy_ref[:, sl]
            sg = _sigmoid(gr)
            dgr_ref[:, sl] = (dy * (osum * r * gv) * (sg * (1.0 + gr * (1.0 - sg)))).astype(BF16)
            dn = dy * (gr * sg)
            dng = dn * gv
            c = jnp.mean(dng * osum, axis=-1, keepdims=True)
            do_ref[:, sl] = (r * dng - osum * (r * r * r * c)).astype(BF16)
            gg = gg + jnp.sum(dn * osum * r, axis=0, keepdims=True)

        @pl.when(i == 0)
        def _():
            gg_ref[...] = jnp.zeros_like(gg_ref)

        gg_ref[...] += gg

    blk = pl.BlockSpec((ROW_BLOCK, GLA_VW), lambda i: (i, 0))
    vec = pl.BlockSpec((1, GLA_DV), lambda i: (0, 0))
    return pl.pallas_call(
        body, name=name, grid=(s // ROW_BLOCK,),
        in_specs=[pl.BlockSpec((ROW_BLOCK, GLA_VW), lambda i: (i, 1)), blk, blk,
                  pl.BlockSpec((ROW_BLOCK, GLA_VW), lambda i: (i, OFF_GR // GLA_VW)), vec],
        out_specs=[blk, blk, vec],
        out_shape=[jax.ShapeDtypeStruct((s, GLA_VW), BF16), jax.ShapeDtypeStruct((s, GLA_VW), BF16),
                   jax.ShapeDtypeStruct((1, GLA_DV), F32)],
        compiler_params=_params(("arbitrary",)),
    )(dcat, o_f, o_b, proj, g)


CONV_ROWS = 256
CONV_COLS = 1408
HALO = 16


def _halo_specs(s, tr, tc, col_of):
    per = tr // HALO
    last = s // HALO - 1
    cur = pl.BlockSpec((tr, tc), lambda c, i: (i, col_of(c)))
    prev = pl.BlockSpec((HALO, tc), lambda c, i: (jnp.maximum(i * per - 1, 0), col_of(c)))
    nxt = pl.BlockSpec((HALO, tc), lambda c, i: (jnp.minimum((i + 1) * per, last), col_of(c)))
    return prev, cur, nxt


def _extended(prev_ref, cur_ref, next_ref, i, s, tr):
    x = jnp.concatenate([prev_ref[...], cur_ref[...], next_ref[...]], axis=0).astype(F32)
    idx = i * tr - HALO + lax.broadcasted_iota(jnp.int32, x.shape, 0)
    return jnp.where((idx >= 0) & (idx < s), x, 0.0)


def _conv_glu(gate, up, conv_w, conv_b, name="conv_glu"):
    s, f = gate.shape
    tr, tc = CONV_ROWS, CONV_COLS
    ext = tr + 2 * HALO

    def body(gp, gc, gn, up_ref, w_ref, b_ref, o_ref):
        i = pl.program_id(1)
        ge = _extended(gp, gc, gn, i, s, tr)
        w = w_ref[...]
        conv = (w[0:1] * pltpu.roll(ge, 1, 0) + w[1:2] * ge + w[2:3] * pltpu.roll(ge, ext - 1, 0))[HALO:HALO + tr]
        conv = conv + b_ref[...]
        o_ref[...] = (conv * _sigmoid(conv) * up_ref[...]).astype(BF16)

    prev, cur, nxt = _halo_specs(s, tr, tc, lambda c: c)
    return pl.pallas_call(
        body, name=name, grid=(f // tc, s // tr),
        in_specs=[prev, cur, nxt, cur, pl.BlockSpec((3, tc), lambda c, i: (0, c)), pl.BlockSpec((1, tc), lambda c, i: (0, c))],
        out_specs=cur,
        out_shape=jax.ShapeDtypeStruct((s, f), BF16),
        compiler_params=_params(("parallel", "parallel")),
    )(gate, gate, gate, up, conv_w, conv_b)


def _conv_glu_bwd(dact, gate, up, conv_w, conv_b, name="conv_glu_bwd"):
    s, f = gate.shape
    tr, tc = CONV_ROWS, CONV_COLS
    ext = tr + 2 * HALO

    def body(dp, dc, dn, gp, gc, gn, upp, upc, upn, w_ref, b_ref, dg_ref, du_ref, gw_ref, gb_ref):
        i = pl.program_id(1)
        ge = _extended(gp, gc, gn, i, s, tr)
        ue = _extended(upp, upc, upn, i, s, tr)
        de = _extended(dp, dc, dn, i, s, tr)
        w = w_ref[...]
        g_prev, g_next = pltpu.roll(ge, 1, 0), pltpu.roll(ge, ext - 1, 0)
        conv = w[0:1] * g_prev + w[1:2] * ge + w[2:3] * g_next + b_ref[...]
        sg = _sigmoid(conv)
        du_ref[...] = (de * (conv * sg))[HALO:HALO + tr].astype(BF16)
        dconv = de * ue * (sg * (1.0 + conv * (1.0 - sg)))
        dgate = w[0:1] * pltpu.roll(dconv, ext - 1, 0) + w[1:2] * dconv + w[2:3] * pltpu.roll(dconv, 1, 0)
        dg_ref[...] = dgate[HALO:HALO + tr].astype(BF16)
        inner = slice(HALO, HALO + tr)
        dci = dconv[inner]

        @pl.when(i == 0)
        def _():
            gw_ref[...] = jnp.zeros_like(gw_ref)
            gb_ref[...] = jnp.zeros_like(gb_ref)

        gw_ref[0:1, :] += jnp.sum(dci * g_prev[inner], axis=0, keepdims=True)
        gw_ref[1:2, :] += jnp.sum(dci * ge[inner], axis=0, keepdims=True)
        gw_ref[2:3, :] += jnp.sum(dci * g_next[inner], axis=0, keepdims=True)
        gb_ref[...] += jnp.sum(dci, axis=0, keepdims=True)

    prev, cur, nxt = _halo_specs(s, tr, tc, lambda c: c)
    wspec = pl.BlockSpec((3, tc), lambda c, i: (0, c))
    bspec = pl.BlockSpec((1, tc), lambda c, i: (0, c))
    return pl.pallas_call(
        body, name=name, grid=(f // tc, s // tr),
        in_specs=[prev, cur, nxt] * 3 + [wspec, bspec],
        out_specs=[cur, cur, wspec, bspec],
        out_shape=[jax.ShapeDtypeStruct((s, f), BF16), jax.ShapeDtypeStruct((s, f), BF16),
                   jax.ShapeDtypeStruct((3, f), F32), jax.ShapeDtypeStruct((1, f), F32)],
        compiler_params=_params(("parallel", "arbitrary")),
    )(dact, dact, dact, gate, gate, gate, up, up, up, conv_w, conv_b)


def _local_step(x, target, w, late_weights=None, grad_sink=None, first_dep=()):
    s = x.shape[0]
    tables = _rope_tables(s)
    uf, ub = _gate_matrices(w["gf_up"], w["gb_up"])
    if grad_sink is None:
        grad_sink = lambda names, grads: ()

    n1 = _rms_fwd(x, w["norm1_g"], "norm1")
    proj = _matmul([(n1, w["w_in"])], "nn", F32, 1024, 1280, D_MODEL, "in_proj", deps=first_dep)
    qkv = _rope_fwd(proj, tables)
    branches = [_attn_fwd(*qkv[di], d, f"attn_fwd_d{d}") for di, d in enumerate(DILATIONS)]
    o_mix, ao, lse = _attn_combine([b[0] for b in branches], [b[1] for b in branches], w["attn_norm_g"])
    g_f, g_b = _gla_gates(proj, uf, ub, w["gf_b"], w["gb_b"])
    o_f, st_f = _gla_fwd(proj, g_f, False, "gla_fwd_f")
    o_b, st_b = _gla_fwd(proj, g_b, True, "gla_fwd_b")
    go = _gla_post(o_f, o_b, proj, w["gla_norm_g"])
    cat = jnp.concatenate([ao, go], axis=1)
    if late_weights is not None:
        w = {**w, **late_weights(cat)}
    h1 = _matmul([(cat, w["w_out"])], "nn", F32, 512, 1024, D_MODEL, "out_proj", res=x)
    n2 = _rms_fwd(h1, w["norm2_g"], "norm2")
    gate = _matmul([(n2, w["w_gate"])], "nn", F32, 1024, 512, D_MODEL, "ffn_gate")
    up = _matmul([(n2, w["w_up"])], "nn", BF16, 1024, 512, D_MODEL, "ffn_up")
    act = _conv_glu(gate, up, w["conv_w"], w["conv_b"])
    h2 = _matmul([(act, w["w_down"])], "nn", F32, 1024, 1024, 2816, "ffn_down", res=h1)
    dh2, dh2_b, loss_acc, g_final = _final_loss(h2, target, w["final_norm_g"])

    dact = _matmul([(dh2_b, w["w_down"])], "nt", BF16, 1024, 512, D_MODEL, "d_act")
    g_w_down = _matmul([(act, dh2_b)], "tn", F32, 1408, 1024, 2048, "g_w_down")
    dep = grad_sink(["w_down"], [g_w_down])
    dgate, dup, g_conv_w, g_conv_b = _conv_glu_bwd(dact, gate, up, w["conv_w"], w["conv_b"])
    g_w_gate = _matmul([(n2, dgate)], "tn", F32, 2048, 512, 2048, "g_w_gate", deps=dep)
    g_w_up = _matmul([(n2, dup)], "tn", F32, 2048, 512, 2048, "g_w_up")
    dep = grad_sink(["w_gate", "w_up"], [g_w_gate, g_w_up])
    dn2 = _matmul([(dgate, w["w_gate"]), (dup, w["w_up"])], "nt", F32, 1024, 1024, 512, "d_n2", deps=dep)
    dh1, dh1_b, g_norm2 = _rms_bwd(dn2, h1, w["norm2_g"], dh2, "norm2_bwd")

    g_w_out = _matmul([(cat, dh1_b)], "tn", F32, 1024, 1024, 2048, "g_w_out")
    dep = grad_sink(["w_out"], [g_w_out])
    dcat = _matmul([(dh1_b, w["w_out"])], "nt", F32, 512, 1024, D_MODEL, "d_cat", deps=dep)
    do_attn, delta, g_attn_norm = _attn_prebwd(dcat, o_mix, w["attn_norm_g"])
    grads = [_attn_bwd(*qkv[di], do_attn[di], lse[di], delta[di], d, f"attn_bwd_d{d}")
             for di, d in enumerate(DILATIONS)]
    dproj = _rope_bwd(grads, tables)
    do_gla, dgr, g_gla_norm = _gla_post_bwd(dcat, o_f, o_b, proj, w["gla_norm_g"])
    dq_f, dk_f, dv_f, dg_f = _gla_bwd(proj, g_f, do_gla, st_f, False, "gla_bwd_f")
    dproj, dg_b = _gla_bwd(proj, g_b, do_gla, st_b, True, "gla_bwd_b", merge=(dq_f, dk_f, dv_f, dgr, dproj))
    dproj, g_uf, g_ub, g_gf_b, g_gb_b = _gla_gates_bwd(dg_f, dg_b, proj, uf, ub, w["gf_b"], w["gb_b"], dproj)
    g_w_in = _matmul([(n1, dproj)], "tn", F32, 1024, 1280, 2048, "g_w_in")
    dep = grad_sink(["w_in"], [g_w_in])
    dn1 = _matmul([(dproj, w["w_in"])], "nt", F32, 1024, 2048, 1280, "d_n1", deps=dep)
    grad_x, _, g_norm1 = _rms_bwd(dn1, x, w["norm1_g"], dh1, "norm1_bwd")

    g = dict(norm1_g=g_norm1, w_in=g_w_in, gf_up=g_uf[:GLA_RANK], gf_b=g_gf_b,
             gb_up=g_ub[GLA_RANK:2 * GLA_RANK], gb_b=g_gb_b, gla_norm_g=g_gla_norm, attn_norm_g=g_attn_norm,
             w_out=g_w_out, norm2_g=g_norm2, w_gate=g_w_gate, w_up=g_w_up, conv_w=g_conv_w, conv_b=g_conv_b,
             w_down=g_w_down, final_norm_g=g_final)
    return loss_acc, grad_x, g


def _me_and_peers():
    x, y, c = lax.axis_index("x"), lax.axis_index("y"), lax.axis_index("c")
    me = 4 * x + 2 * y + c
    peers = []
    for kbits in range(1, N_DEV):
        px, py, pc = x ^ (kbits >> 2 & 1), y ^ (kbits >> 1 & 1), c ^ (kbits & 1)
        peers.append(((px, py, pc), 4 * px + 2 * py + pc))
    return me, peers


_HBM = pl.BlockSpec(memory_space=pltpu.HBM)
_SEM = pl.BlockSpec(memory_space=pltpu.SEMAPHORE)
_ANY = pl.BlockSpec(memory_space=pl.ANY)
_EFFECT = pltpu.SideEffectType.DATAFLOW_SIDE_EFFECTING


def _exchange_copies(src_refs, land_refs, send_sems, recv_sems, scatter):
    me, peers = _me_and_peers()
    out = []
    for a, (src, land) in enumerate(zip(src_refs, land_refs)):
        for kk, (dev, idx) in enumerate(peers):
            out.append(pltpu.make_async_remote_copy(
                src_ref=src.at[idx] if scatter else src, dst_ref=land.at[me],
                send_sem=send_sems.at[a * (N_DEV - 1) + kk], recv_sem=recv_sems.at[a * (N_DEV - 1) + kk],
                device_id=dev, device_id_type=MESH_ID))
    return out


def _exchange_start(srcs, lands, scatter, name, deps=()):
    n, nd = len(srcs), len(deps)

    def body(*refs):
        src_refs, land_refs = refs[:n], refs[n:2 * n]
        send_sems, recv_sems = refs[2 * n + nd:2 * n + nd + 2]
        token = refs[-1]
        for cp in _exchange_copies(src_refs, land_refs, send_sems, recv_sems, scatter):
            cp.start()
        token[...] = jnp.zeros_like(token)

    outs = pl.pallas_call(
        body, name=name,
        in_specs=[_HBM] * (2 * n) + [_ANY] * nd,
        out_specs=[_SEM, _SEM] + [_HBM] * (2 * n) + [pl.BlockSpec(memory_space=pltpu.VMEM)],
        out_shape=[pltpu.SemaphoreType.DMA((n * (N_DEV - 1),)), pltpu.SemaphoreType.DMA((n * (N_DEV - 1),))]
        + [pltpu.HBM(t.shape, t.dtype) for t in srcs] + [pltpu.HBM(t.shape, t.dtype) for t in lands]
        + [jax.ShapeDtypeStruct((SUBLANES, LANES), F32)],
        input_output_aliases={i: 2 + i for i in range(2 * n)},
        compiler_params=pltpu.CompilerParams(has_side_effects=_EFFECT),
    )(*[pltpu.with_memory_space_constraint(t, pltpu.HBM) for t in list(srcs) + list(lands)], *deps)
    send_sems, recv_sems = outs[0], outs[1]
    return dict(send=send_sems, recv=recv_sems, srcs=outs[2:2 + n], lands=outs[2 + n:2 + 2 * n],
                scatter=scatter, token=outs[-1])


def _exchange_wait(started, name, after):
    n = len(started["srcs"])
    scatter = started["scatter"]

    def body(*refs):
        src_refs, land_refs = refs[:n], refs[n:2 * n]
        send_sems, recv_sems = refs[2 * n], refs[2 * n + 1]
        for cp in _exchange_copies(src_refs, land_refs, send_sems, recv_sems, scatter):
            cp.wait_send()
            cp.wait_recv()

    outs = pl.pallas_call(
        body, name=name,
        in_specs=[_HBM] * (2 * n) + [_SEM, _SEM, _ANY],
        out_specs=[_HBM] * (2 * n),
        out_shape=[pltpu.HBM(t.shape, t.dtype) for t in started["srcs"]]
        + [pltpu.HBM(t.shape, t.dtype) for t in started["lands"]],
        input_output_aliases={i: i for i in range(2 * n)},
        compiler_params=pltpu.CompilerParams(has_side_effects=_EFFECT),
    )(*started["srcs"], *started["lands"], started["send"], started["recv"], after)
    return outs[:n], outs[n:]


def _all_gather_two_level(shard, name):
    def body(x_ref, out_ref, send_sems, recv_sems, local_sem):
        x, y, c = lax.axis_index("x"), lax.axis_index("y"), lax.axis_index("c")
        me, sibling = (x, y, c), (x, y, 1 - c)
        chips = [(1 - x, y), (x, 1 - y), (1 - x, 1 - y)]

        def slot(px, py, pc):
            return out_ref.at[4 * px + 2 * py + pc]

        def copy(k, block, to, src=None):
            return pltpu.make_async_remote_copy(
                src_ref=slot(*block) if src is None else src, dst_ref=slot(*block),
                send_sem=send_sems.at[k], recv_sem=recv_sems.at[k], device_id=to, device_id_type=MESH_ID)

        mine = pltpu.make_async_copy(x_ref, slot(*me), local_sem)
        mine.start()
        first = [copy(0, me, sibling, src=x_ref)]
        first += [copy(1 + j, me, (*chip, c), src=x_ref) for j, chip in enumerate(chips)]
        for cp in first:
            cp.start()
        passed = [copy(4 + j, (*chip, c), sibling) for j, chip in enumerate(chips)]
        for j, chip in enumerate(chips):
            copy(1 + j, (*chip, c), me).wait_recv()
            passed[j].start()
        copy(0, sibling, me).wait_recv()
        for j, chip in enumerate(chips):
            copy(4 + j, (*chip, 1 - c), me).wait_recv()
        for cp in first + passed:
            cp.wait_send()
        mine.wait()

    return pl.pallas_call(
        body, name=name,
        in_specs=[_ANY], out_specs=_ANY,
        out_shape=jax.ShapeDtypeStruct((N_DEV,) + shard.shape, shard.dtype),
        scratch_shapes=[pltpu.SemaphoreType.DMA((N_DEV - 1,)), pltpu.SemaphoreType.DMA((N_DEV - 1,)),
                        pltpu.SemaphoreType.DMA],
    )(shard)


def _all_gather_vmem(vec, name):
    r = vec.shape[0]

    def body(v_ref, o_ref, send_sems, recv_sems):
        me, peers = _me_and_peers()
        o_ref[me] = v_ref[...]
        sends = []
        for kk, (dev, _) in enumerate(peers):
            cp = pltpu.make_async_remote_copy(
                src_ref=v_ref, dst_ref=o_ref.at[me],
                send_sem=send_sems.at[kk], recv_sem=recv_sems.at[kk],
                device_id=dev, device_id_type=MESH_ID)
            cp.start()
            sends.append(cp)
        for kk, (dev, idx) in enumerate(peers):
            pltpu.make_async_remote_copy(
                src_ref=v_ref, dst_ref=o_ref.at[idx],
                send_sem=send_sems.at[kk], recv_sem=recv_sems.at[kk],
                device_id=dev, device_id_type=MESH_ID).wait_recv()
        for cp in sends:
            cp.wait_send()

    return pl.pallas_call(
        body, name=name,
        in_specs=[pl.BlockSpec(memory_space=pltpu.VMEM)],
        out_specs=pl.BlockSpec(memory_space=pltpu.VMEM),
        out_shape=jax.ShapeDtypeStruct((N_DEV, r, LANES), F32),
        scratch_shapes=[pltpu.SemaphoreType.DMA((N_DEV - 1,)), pltpu.SemaphoreType.DMA((N_DEV - 1,))],
        compiler_params=pltpu.CompilerParams(vmem_limit_bytes=VMEM_LIMIT),
    )(vec)


def _adamw_math(w, g, m, v):
    m = ADAM_B1 * m + (1.0 - ADAM_B1) * g
    v = ADAM_B2 * v + (1.0 - ADAM_B2) * (g * g)
    m_hat = m / (1.0 - ADAM_B1 ** ADAM_STEP)
    v_hat = v / (1.0 - ADAM_B2 ** ADAM_STEP)
    delta = -ADAM_LR * (m_hat / (jnp.sqrt(v_hat) + ADAM_EPS) + ADAM_WD * w)
    return delta, m, v


def _adamw_sum(parts, w, m, v, tr, name, own=None, me=None):
    r, c = w.shape

    def body(*refs):
        if own is None:
            p_ref, w_ref, m_ref, v_ref, g_ref, d_ref, nm_ref, nv_ref = refs
            terms = [p_ref[kk] for kk in range(N_DEV)]
        else:
            me_ref, p_ref, own_ref, w_ref, m_ref, v_ref, g_ref, d_ref, nm_ref, nv_ref = refs
            terms = [jnp.where(me_ref[0] == kk, own_ref[0], p_ref[kk]).astype(F32) for kk in range(N_DEV)]
        g = terms[0]
        for t in terms[1:]:
            g = g + t
        g_ref[...] = g
        d_ref[...], nm_ref[...], nv_ref[...] = _adamw_math(w_ref[...], g, m_ref[...], v_ref[...])

    out_shape = [jax.ShapeDtypeStruct((r, c), F32)] * 4
    if own is None:
        blk = pl.BlockSpec((tr, c), lambda i: (i, 0))
        return pl.pallas_call(
            body, name=name, grid=(r // tr,),
            in_specs=[pl.BlockSpec((N_DEV, tr, c), lambda i: (0, i, 0)), blk, blk, blk],
            out_specs=[blk] * 4, out_shape=out_shape,
            compiler_params=_params(("parallel",)),
        )(parts, w, m, v)
    blk = pl.BlockSpec((tr, c), lambda i, me_ref: (i, 0))
    return pl.pallas_call(
        body, name=name,
        grid_spec=pltpu.PrefetchScalarGridSpec(
            num_scalar_prefetch=1, grid=(r // tr,),
            in_specs=[pl.BlockSpec((N_DEV, tr, c), lambda i, me_ref: (0, i, 0)),
                      pl.BlockSpec((1, tr, c), lambda i, me_ref: (me_ref[0], i, 0)), blk, blk, blk],
            out_specs=[blk] * 4),
        out_shape=out_shape,
        compiler_params=_params(("parallel",)),
    )(jnp.reshape(me, (1,)).astype(jnp.int32), parts, own, w, m, v)


_SMALL = ("norm1_g", "gf_b", "gb_b", "gla_norm_g", "attn_norm_g", "norm2_g", "conv_b", "final_norm_g",
          "gf_up", "gb_up", "conv_w")


def _pack(named):
    flat = jnp.concatenate([jnp.ravel(t).astype(F32) for t in named])
    tile = SUBLANES * LANES
    total = -(-flat.shape[0] // tile) * tile
    return jnp.pad(flat, (0, total - flat.shape[0])).reshape(total // LANES, LANES)


def _unpack(packed, shapes):
    flat = packed.reshape(-1)
    out, off = [], 0
    for shp in shapes:
        size = int(np.prod(shp))
        out.append(flat[off:off + size].reshape(shp))
        off += size
    return out


def kernel(x, norm1_g, w_in, gf_up, gf_b, gb_up, gb_b, gla_norm_g, attn_norm_g, w_out, norm2_g, w_gate, w_up, conv_w, conv_b, w_down, final_norm_g, loss_target, m_norm1_g, m_w_in, m_gf_up, m_gf_b, m_gb_up, m_gb_b, m_gla_norm_g, m_attn_norm_g, m_w_out, m_norm2_g, m_w_gate, m_w_up, m_conv_w, m_conv_b, m_w_down, m_final_norm_g, v_norm1_g, v_w_in, v_gf_up, v_gf_b, v_gb_up, v_gb_b, v_gla_norm_g, v_attn_norm_g, v_w_out, v_norm2_g, v_w_gate, v_w_up, v_conv_w, v_conv_b, v_w_down, v_final_norm_g):
    names = ("norm1_g", "w_in", "gf_up", "gf_b", "gb_up", "gb_b", "gla_norm_g", "attn_norm_g", "w_out", "norm2_g",
             "w_gate", "w_up", "conv_w", "conv_b", "w_down", "final_norm_g")
    ws = dict(zip(names, (norm1_g, w_in, gf_up, gf_b, gb_up, gb_b, gla_norm_g, attn_norm_g, w_out, norm2_g,
                          w_gate, w_up, conv_w, conv_b, w_down, final_norm_g)))
    ms = dict(zip(names, (m_norm1_g, m_w_in, m_gf_up, m_gf_b, m_gb_up, m_gb_b, m_gla_norm_g, m_attn_norm_g, m_w_out,
                          m_norm2_g, m_w_gate, m_w_up, m_conv_w, m_conv_b, m_w_down, m_final_norm_g)))
    vs = dict(zip(names, (v_norm1_g, v_w_in, v_gf_up, v_gf_b, v_gb_up, v_gb_b, v_gla_norm_g, v_attn_norm_g, v_w_out,
                          v_norm2_g, v_w_gate, v_w_up, v_conv_w, v_conv_b, v_w_down, v_final_norm_g)))
    me = 4 * lax.axis_index("x") + 2 * lax.axis_index("y") + lax.axis_index("c")
    big = ("w_in", "w_out", "w_gate", "w_up", "w_down")
    col_sharded = ("w_in", "w_gate", "w_up")

    def gather_start(group, name, deps=()):
        shards = [ws[n][0].astype(BF16) for n in group]
        lands = [lax.empty((N_DEV,) + t.shape, BF16) for t in shards]
        return _exchange_start(shards, lands, False, name, deps)

    def gather_finish(group, started, name, after):
        full = {}
        for n, own, t in zip(group, *_exchange_wait(started, name, after)):
            t = lax.dynamic_update_slice(t, own[None], (me, 0, 0))
            if n in col_sharded:
                full[n] = jnp.transpose(t, (1, 0, 2)).reshape(t.shape[1], N_DEV * t.shape[2])
            else:
                full[n] = t.reshape(N_DEV * t.shape[1], t.shape[2])
        return full

    w_in_all = _all_gather_two_level(ws["w_in"][0].astype(BF16), "gather_w_in")
    full = {"w_in": jnp.pad(jnp.transpose(w_in_all, (1, 0, 2)).reshape(D_MODEL, IN_WIDTH),
                            ((0, 0), (0, IN_PAD - IN_WIDTH)))}
    late = ("w_out", "w_gate", "w_up", "w_down")
    started_b = gather_start(late, "gather_late_start", deps=(full["w_in"],))

    def late_weights(after):
        return gather_finish(late, started_b, "gather_late_wait", after)

    small_sharded = ("gf_up", "gb_up", "conv_w")
    sm = _all_gather_vmem(_pack([ws[n][0] for n in small_sharded]), "gather_small")
    shard_shapes = [ws[n][0].shape for n in small_sharded]
    per_dev = [_unpack(sm[d], shard_shapes) for d in range(N_DEV)]
    for i, n in enumerate(small_sharded):
        full[n] = jnp.concatenate([per_dev[d][i] for d in range(N_DEV)], axis=1)
    for n in ("norm1_g", "gf_b", "gb_b", "gla_norm_g", "attn_norm_g", "norm2_g", "conv_b"):
        full[n] = ws[n]
    full["final_norm_g"] = final_norm_g.reshape(1, D_MODEL)

    in_flight = []

    def grad_sink(group, grads):
        partials = []
        for n, t in zip(group, grads):
            if n == "w_in":
                t = t[:, :IN_WIDTH].astype(BF16)
            if n in col_sharded:
                t = jnp.transpose(t.reshape(t.shape[0], N_DEV, t.shape[1] // N_DEV), (1, 0, 2))
            else:
                t = t.reshape(N_DEV, t.shape[0] // N_DEV, t.shape[1])
            partials.append(t)
        lands = [lax.empty(t.shape, t.dtype) for t in partials]
        started = _exchange_start(partials, lands, True, "exchange_" + "_".join(group) + "_start")
        in_flight.append((group, started))
        return (started["token"],)

    loss_acc, grad_x, g = _local_step(x[0], loss_target[0], full, late_weights, grad_sink,
                                      first_dep=(started_b["token"],))

    out = {}
    for group, started in in_flight:
        sent, landed = _exchange_wait(started, "exchange_" + "_".join(group) + "_wait", grad_x)
        for n, parts, own in zip(group, landed, sent):
            out[n] = _adamw_sum(parts, ws[n][0], ms[n][0], vs[n][0], 64, "adamw_" + n, own=own, me=me)

    small_full_shapes = [g[n].shape for n in _SMALL]
    gsmall = _pack([g[n] for n in _SMALL] + [loss_acc[0:1, 0:1]])
    gathered_small = _all_gather_vmem(gsmall, "gather_small_grads")

    def full_small(d):
        parts = []
        for n in _SMALL:
            t = d[n].reshape(d[n].shape[-2:]) if d[n].ndim == 3 else d[n].reshape(1, -1)
            if n in small_sharded:
                wide = jnp.zeros((t.shape[0], t.shape[1] * N_DEV), F32)
                t = lax.dynamic_update_slice_in_dim(wide, t, me * t.shape[1], axis=1)
            parts.append(t)
        return _pack(parts + [jnp.zeros((1, 1), F32)])

    rows = gsmall.shape[0]
    res_small = _adamw_sum(gathered_small, full_small(ws), full_small(ms), full_small(vs), rows, "adamw_small")
    loss = res_small[0].reshape(-1)[sum(int(np.prod(sh)) for sh in small_full_shapes)]
    unpacked = [_unpack(t, small_full_shapes) for t in res_small]
    for i, n in enumerate(_SMALL):
        vals = [u[i] for u in unpacked]
        if n in small_sharded:
            width = vals[0].shape[1] // N_DEV
            vals = [lax.dynamic_slice_in_dim(t, me * width, width, axis=1) for t in vals]
        out[n] = vals

    result = [loss, grad_x[None]]
    for kind in range(4):
        for n in names:
            result.append(out[n][kind].reshape(ws[n].shape))
    return tuple(result)
```

```python
import functools

import numpy as np
import jax
import jax.numpy as jnp
from jax import lax
from jax.experimental import pallas as pl
from jax.experimental.pallas import tpu as pltpu

F32 = jnp.float32
BF16 = jnp.bfloat16

D_MODEL = 2048
ATTN_W = 1024
ATTN_HEADS = 8
HEAD_DIM = 128
ROPE_DIM = 32
ROPE_THETA = 500000.0
DILATIONS = (1, 4, 16)
N_SIDE = 64
GLA_KW = 512
GLA_VW = 1024
GLA_HEADS = 4
GLA_DK = 128
GLA_DV = 256
GLA_RANK = 16
GLA_GATE_NORM = 16.0
GLA_CHUNK = 64
IN_WIDTH = 6176
IN_PAD = 6400
D_FF = 5632
EPS = 1e-6
N_DEV = 8

OFF_AQ, OFF_AK, OFF_AV = 0, 1024, 2048
OFF_GQ, OFF_GK, OFF_GV, OFF_GR, OFF_Z = 3072, 3584, 4096, 5120, 6144

ADAM_LR, ADAM_B1, ADAM_B2, ADAM_EPS, ADAM_WD, ADAM_STEP = 0.001, 0.9, 0.999, 1e-08, 0.01, 10

LANES = 128
SUBLANES = 8
VMEM_LIMIT = 56 * 1024 * 1024
ROW_BLOCK = 256
ATTN_BLOCK = 128
GLA_CHUNKS_PER_STEP = 4
NEG = -1e30
MESH_ID = pl.DeviceIdType.MESH


def _params(sem):
    return pltpu.CompilerParams(dimension_semantics=sem, vmem_limit_bytes=VMEM_LIMIT)


def _dot(a, b):
    return lax.dot_general(a, b, (((1,), (0,)), ((), ())), preferred_element_type=F32)


def _dot_nt(a, b):
    return lax.dot_general(a, b, (((1,), (1,)), ((), ())), preferred_element_type=F32)


def _dot_tn(a, b):
    return lax.dot_general(a, b, (((0,), (0,)), ((), ())), preferred_element_type=F32)


def _sigmoid(x):
    return 0.5 * jnp.tanh(0.5 * x) + 0.5


def _matmul(pairs, mode, out_dtype, tm, tn, tk, name, res=None, deps=()):
    a0, b0 = pairs[0]
    if mode == "nn":
        (m, kdim), n = a0.shape, b0.shape[1]
    elif mode == "nt":
        (m, kdim), n = a0.shape, b0.shape[0]
    else:
        (kdim, m), n = a0.shape, b0.shape[1]
    assert m % tm == 0 and n % tn == 0 and kdim % tk == 0, (name, m, n, kdim)
    nk = kdim // tk
    npairs = len(pairs)
    steps = nk * npairs
    dot = {"nn": _dot, "nt": _dot_nt, "tn": _dot_tn}[mode]

    def kidx(p):
        return lambda k: jnp.clip(k - p * nk, 0, nk - 1)

    in_specs, args = [], []
    for p, (a, b) in enumerate(pairs):
        kk = kidx(p)
        if mode == "nn":
            in_specs += [pl.BlockSpec((tm, tk), lambda i, j, k, kk=kk: (i, kk(k))),
                         pl.BlockSpec((tk, tn), lambda i, j, k, kk=kk: (kk(k), j))]
        elif mode == "nt":
            in_specs += [pl.BlockSpec((tm, tk), lambda i, j, k, kk=kk: (i, kk(k))),
                         pl.BlockSpec((tn, tk), lambda i, j, k, kk=kk: (j, kk(k)))]
        else:
            in_specs += [pl.BlockSpec((tk, tm), lambda i, j, k, kk=kk: (kk(k), i)),
                         pl.BlockSpec((tk, tn), lambda i, j, k, kk=kk: (kk(k), j))]
        args += [a, b]
    if res is not None:
        in_specs.append(pl.BlockSpec((tm, tn), lambda i, j, k: (i, j)))
        args.append(res)
    in_specs += [pl.BlockSpec(memory_space=pl.ANY)] * len(deps)
    args += list(deps)

    def body(*refs):
        ab = refs[:2 * npairs]
        res_ref = refs[2 * npairs] if res is not None else None
        o_ref = refs[2 * npairs + (1 if res is not None else 0) + len(deps)]

        def finish(acc):
            if res_ref is not None:
                acc = acc + res_ref[...]
            o_ref[...] = acc.astype(out_dtype)

        if steps == 1:
            finish(dot(ab[0][...], ab[1][...]))
            return
        acc_ref = refs[-1]
        k = pl.program_id(2)

        @pl.when(k == 0)
        def _():
            acc_ref[...] = jnp.zeros_like(acc_ref)

        for p in range(npairs):
            @pl.when((k >= p * nk) & (k < (p + 1) * nk))
            def _(p=p):
                acc_ref[...] += dot(ab[2 * p][...], ab[2 * p + 1][...])

        @pl.when(k == steps - 1)
        def _():
            finish(acc_ref[...])

    return pl.pallas_call(
        body, name=name,
        grid=(m // tm, n // tn, steps),
        in_specs=in_specs,
        out_specs=pl.BlockSpec((tm, tn), lambda i, j, k: (i, j)),
        out_shape=jax.ShapeDtypeStruct((m, n), out_dtype),
        scratch_shapes=[] if steps == 1 else [pltpu.VMEM((tm, tn), F32)],
        compiler_params=_params(("parallel", "parallel", "arbitrary")),
    )(*args)


def _rms_fwd(x, g, name):
    s, d = x.shape

    def body(x_ref, g_ref, o_ref):
        xv = x_ref[...]
        r = lax.rsqrt(jnp.mean(xv * xv, axis=-1, keepdims=True) + EPS)
        o_ref[...] = (xv * r * g_ref[...]).astype(BF16)

    return pl.pallas_call(
        body, name=name, grid=(s // ROW_BLOCK,),
        in_specs=[pl.BlockSpec((ROW_BLOCK, d), lambda i: (i, 0)), pl.BlockSpec((1, d), lambda i: (0, 0))],
        out_specs=pl.BlockSpec((ROW_BLOCK, d), lambda i: (i, 0)),
        out_shape=jax.ShapeDtypeStruct((s, d), BF16),
        compiler_params=_params(("parallel",)),
    )(x, g)


def _rms_bwd(dn, x, g, dres, name):
    s, d = x.shape

    def body(dn_ref, x_ref, g_ref, dres_ref, dx_ref, dxb_ref, gg_ref):
        i = pl.program_id(0)
        xv, dnv = x_ref[...], dn_ref[...]
        r = lax.rsqrt(jnp.mean(xv * xv, axis=-1, keepdims=True) + EPS)
        dng = dnv * g_ref[...]
        c = jnp.mean(dng * xv, axis=-1, keepdims=True)
        dx = dres_ref[...] + r * dng - xv * (r * r * r * c)
        dx_ref[...] = dx
        dxb_ref[...] = dx.astype(BF16)

        @pl.when(i == 0)
        def _():
            gg_ref[...] = jnp.zeros_like(gg_ref)

        gg_ref[...] += jnp.sum(dnv * xv * r, axis=0, keepdims=True)

    row = pl.BlockSpec((ROW_BLOCK, d), lambda i: (i, 0))
    vec = pl.BlockSpec((1, d), lambda i: (0, 0))
    return pl.pallas_call(
        body, name=name, grid=(s // ROW_BLOCK,),
        in_specs=[row, row, vec, row],
        out_specs=[row, row, vec],
        out_shape=[jax.ShapeDtypeStruct((s, d), F32), jax.ShapeDtypeStruct((s, d), BF16),
                   jax.ShapeDtypeStruct((1, d), F32)],
        compiler_params=_params(("arbitrary",)),
    )(dn, x, g, dres)


def _final_loss(h2, target, g, name="final_loss"):
    s, d = h2.shape

    def body(h_ref, t_ref, g_ref, dh_ref, dhb_ref, loss_ref, gg_ref):
        i = pl.program_id(0)
        hv, gv = h_ref[...], g_ref[...]
        r = lax.rsqrt(jnp.mean(hv * hv, axis=-1, keepdims=True) + EPS)
        e = hv * r * gv - t_ref[...]
        dy = e * (1.0 / d)
        dyg = dy * gv
        c = jnp.mean(dyg * hv, axis=-1, keepdims=True)
        dh = r * dyg - hv * (r * r * r * c)
        dh_ref[...] = dh
        dhb_ref[...] = dh.astype(BF16)

        @pl.when(i == 0)
        def _():
            gg_ref[...] = jnp.zeros_like(gg_ref)
            loss_ref[...] = jnp.zeros_like(loss_ref)

        gg_ref[...] += jnp.sum(dy * hv * r, axis=0, keepdims=True)
        loss_ref[...] += jnp.sum(jnp.sum(e * e, axis=-1, keepdims=True), axis=0, keepdims=True) * (0.5 / d)

    row = pl.BlockSpec((ROW_BLOCK, d), lambda i: (i, 0))
    vec = pl.BlockSpec((1, d), lambda i: (0, 0))
    return pl.pallas_call(
        body, name=name, grid=(s // ROW_BLOCK,),
        in_specs=[row, row, vec],
        out_specs=[row, row, pl.BlockSpec((SUBLANES, LANES), lambda i: (0, 0)), vec],
        out_shape=[jax.ShapeDtypeStruct((s, d), F32), jax.ShapeDtypeStruct((s, d), BF16),
                   jax.ShapeDtypeStruct((SUBLANES, LANES), F32), jax.ShapeDtypeStruct((1, d), F32)],
        compiler_params=_params(("arbitrary",)),
    )(h2, target, g)


def _rope_tables(s):
    pos = jnp.arange(s, dtype=F32)
    inv_freq = ROPE_THETA ** (-jnp.arange(0, ROPE_DIM, 2, dtype=F32) / ROPE_DIM)
    ang = pos[:, None] * inv_freq[None, :]
    cos, sin = jnp.cos(ang), jnp.sin(ang)
    half = ROPE_DIM // 2
    rest = HEAD_DIM - ROPE_DIM
    c = jnp.concatenate([cos, cos, jnp.ones((s, rest), F32)], axis=1)
    sm = jnp.concatenate([-sin, jnp.zeros((s, half + rest), F32)], axis=1)
    sp = jnp.concatenate([jnp.zeros((s, half), F32), sin, jnp.zeros((s, rest), F32)], axis=1)
    return c, sm, sp


def _res_shape(s, groups, dil, dtype):
    return jax.ShapeDtypeStruct((s // dil, dil * groups * LANES), dtype)


def _res_spec(groups, dil):
    return pl.BlockSpec((ROW_BLOCK // dil, dil * groups * LANES), lambda i: (i, 0))


def _to_residues(scr, o_ref, dil):
    groups, rows = scr.shape[0], ROW_BLOCK // dil
    for r in range(dil):
        for h in range(groups):
            piece = scr[h] if dil == 1 else scr.at[h][pl.ds(r, rows, stride=dil), :]
            o_ref[:, (r * groups + h) * LANES:(r * groups + h + 1) * LANES] = piece.astype(o_ref.dtype)


def _from_residues(i_ref, scr, dil):
    groups, rows = scr.shape[0], ROW_BLOCK // dil
    for r in range(dil):
        for h in range(groups):
            piece = i_ref[:, (r * groups + h) * LANES:(r * groups + h + 1) * LANES].astype(F32)
            if dil == 1:
                scr[h] = piece
            else:
                scr.at[h][pl.ds(r, rows, stride=dil), :] = piece


def _rope_fwd(proj, tables, name="rope_fwd"):
    s = proj.shape[0]
    half = ROPE_DIM // 2
    nd = len(DILATIONS)

    def body(p_ref, c_ref, sm_ref, sp_ref, *rest):
        outs, scr = rest[:3 * nd], rest[3 * nd]
        c, sm, sp = c_ref[...], sm_ref[...], sp_ref[...]
        for gi, off in enumerate((OFF_AQ, OFF_AK, OFF_AV)):
            for h in range(ATTN_HEADS):
                t = p_ref[:, off + h * HEAD_DIM: off + (h + 1) * HEAD_DIM]
                if off != OFF_AV:
                    t = t * c + pltpu.roll(t, HEAD_DIM - half, 1) * sm + pltpu.roll(t, half, 1) * sp
                scr[h] = t
            for di, dil in enumerate(DILATIONS):
                _to_residues(scr, outs[3 * di + gi], dil)

    tab = pl.BlockSpec((ROW_BLOCK, HEAD_DIM), lambda i: (i, 0))
    outs = pl.pallas_call(
        body, name=name, grid=(s // ROW_BLOCK,),
        in_specs=[pl.BlockSpec((ROW_BLOCK, 3 * ATTN_W), lambda i: (i, 0)), tab, tab, tab],
        out_specs=[_res_spec(ATTN_HEADS, d) for d in DILATIONS for _ in range(3)],
        out_shape=[_res_shape(s, ATTN_HEADS, d, BF16) for d in DILATIONS for _ in range(3)],
        scratch_shapes=[pltpu.VMEM((ATTN_HEADS, ROW_BLOCK, LANES), F32)],
        compiler_params=_params(("parallel",)),
    )(proj, *tables)
    return [tuple(outs[3 * di:3 * di + 3]) for di in range(nd)]


def _rope_bwd(grads, tables, name="rope_bwd"):
    s = grads[0][0].shape[0] * DILATIONS[0]
    half = ROPE_DIM // 2
    nd = len(DILATIONS)

    def body(*refs):
        ins = refs[:3 * nd]
        c_ref, sm_ref, sp_ref, o_ref = refs[3 * nd:3 * nd + 4]
        scrs = refs[3 * nd + 4:]
        c, sm, sp = c_ref[...], sm_ref[...], sp_ref[...]
        for gi, off in enumerate((OFF_AQ, OFF_AK, OFF_AV)):
            for di, dil in enumerate(DILATIONS):
                _from_residues(ins[3 * di + gi], scrs[di], dil)
            for h in range(ATTN_HEADS):
                t = scrs[0][h]
                for scr in scrs[1:]:
                    t = t + scr[h]
                if off != OFF_AV:
                    t = t * c + pltpu.roll(t * sm, half, 1) + pltpu.roll(t * sp, HEAD_DIM - half, 1)
                o_ref[:, off + h * HEAD_DIM: off + (h + 1) * HEAD_DIM] = t.astype(BF16)

    tab = pl.BlockSpec((ROW_BLOCK, HEAD_DIM), lambda i: (i, 0))
    return pl.pallas_call(
        body, name=name, grid=(s // ROW_BLOCK,),
        in_specs=[_res_spec(ATTN_HEADS, d) for d in DILATIONS for _ in range(3)] + [tab, tab, tab],
        out_specs=pl.BlockSpec((ROW_BLOCK, 3 * ATTN_W), lambda i: (i, 0)),
        out_shape=jax.ShapeDtypeStruct((s, IN_PAD), BF16),
        scratch_shapes=[pltpu.VMEM((ATTN_HEADS, ROW_BLOCK, LANES), F32) for _ in DILATIONS],
        compiler_params=_params(("parallel",)),
    )(*[t for g in grads for t in g], *tables)


def _window_specs(nb, width):
    qb, hb = ATTN_BLOCK, N_SIDE
    cur = pl.BlockSpec((qb, width), lambda r, j: (j, r))
    prev = pl.BlockSpec((hb, width), lambda r, j: (jnp.maximum(2 * j - 1, 0), r))
    nxt = pl.BlockSpec((hb, width), lambda r, j: (jnp.minimum(2 * j + 2, 2 * nb - 1), r))
    return prev, cur, nxt


def _band_masks(j, length):
    qb, hb = ATTN_BLOCK, N_SIDE
    row = lax.broadcasted_iota(jnp.int32, (qb, qb), 0)
    col = lax.broadcasted_iota(jnp.int32, (qb, qb), 1)

    def edge_pos(i):
        return j * qb - hb + i + jnp.where(i >= hb, qb, 0)

    def ok(a, b, outside):
        return (jnp.abs(a - b) <= N_SIDE) & (outside >= 0) & (outside < length)

    cur = jnp.abs(row - col) <= N_SIDE
    edge_k = ok(j * qb + row, edge_pos(col), edge_pos(col))
    edge_q = ok(edge_pos(row), j * qb + col, edge_pos(row))
    return cur, edge_k, edge_q


def _edge(prev_ref, next_ref, sl):
    return jnp.concatenate([prev_ref[:, sl], next_ref[:, sl]], axis=0)


def _attn_fwd(q, k, v, dil, name):
    length = q.shape[0]
    qb = ATTN_BLOCK
    nb = length // qb
    scale = HEAD_DIM ** -0.5

    def body(q_ref, kp_ref, kc_ref, kn_ref, vp_ref, vc_ref, vn_ref, o_ref, lse_ref):
        valid_c, valid_e, _ = _band_masks(pl.program_id(1), length)
        lane = lax.broadcasted_iota(jnp.int32, (qb, LANES), 1)
        lse_acc = jnp.zeros((qb, LANES), F32)
        heads = [slice(h * HEAD_DIM, (h + 1) * HEAD_DIM) for h in range(ATTN_HEADS)]
        scores = [(_dot_nt(q_ref[:, sl], kc_ref[:, sl]), _dot_nt(q_ref[:, sl], _edge(kp_ref, kn_ref, sl)))
                  for sl in heads]
        probs = []
        for h, (s_c, s_e) in enumerate(scores):
            s_c = jnp.where(valid_c, s_c * scale, NEG)
            s_e = jnp.where(valid_e, s_e * scale, NEG)
            m = jnp.max(jnp.maximum(s_c, s_e), axis=-1, keepdims=True)
            p_c, p_e = jnp.exp(s_c - m), jnp.exp(s_e - m)
            den = jnp.sum(p_c + p_e, axis=-1, keepdims=True)
            probs.append((p_c.astype(BF16), p_e.astype(BF16), 1.0 / den))
            lse_acc = jnp.where(lane == h, m + jnp.log(den), lse_acc)
        for sl, (p_c, p_e, inv) in zip(heads, probs):
            o_ref[:, sl] = (_dot(p_c, vc_ref[:, sl]) + _dot(p_e, _edge(vp_ref, vn_ref, sl))) * inv
        lse_ref[...] = lse_acc

    prev, cur, nxt = _window_specs(nb, ATTN_W)
    return pl.pallas_call(
        body, name=name, grid=(dil, nb),
        in_specs=[cur, prev, cur, nxt, prev, cur, nxt],
        out_specs=[cur, pl.BlockSpec((qb, LANES), lambda r, j: (j, r))],
        out_shape=[jax.ShapeDtypeStruct((length, dil * ATTN_W), F32),
                   jax.ShapeDtypeStruct((length, dil * LANES), F32)],
        compiler_params=_params(("parallel", "parallel")),
    )(q, k, k, k, v, v, v)


def _attn_combine(outs, lses, g, name="attn_combine"):
    s = outs[0].shape[0] * DILATIONS[0]
    nd = len(DILATIONS)

    def body(*refs):
        o_refs, l_refs = refs[:nd], refs[nd:2 * nd]
        g_ref, o_ref, n_ref = refs[2 * nd:2 * nd + 3]
        lse_outs = refs[2 * nd + 3:3 * nd + 3]
        o_scr, l_scr = refs[3 * nd + 3:4 * nd + 3], refs[4 * nd + 3:5 * nd + 3]
        for di, dil in enumerate(DILATIONS):
            _from_residues(o_refs[di], o_scr[di], dil)
            _from_residues(l_refs[di], l_scr[di], dil)
        ls = [scr[0] for scr in l_scr]
        m = ls[0]
        for l in ls[1:]:
            m = jnp.maximum(m, l)
        es = [jnp.exp(l - m) for l in ls]
        z = es[0]
        for e in es[1:]:
            z = z + e
        ws = [e / z for e in es]
        l_scr[0][0] = m + jnp.log(z)
        for di, dil in enumerate(DILATIONS):
            _to_residues(l_scr[0], lse_outs[di], dil)
        ssq = jnp.zeros((ROW_BLOCK, 1), F32)
        for h in range(ATTN_HEADS):
            sl = slice(h * HEAD_DIM, (h + 1) * HEAD_DIM)
            acc = ws[0][:, h:h + 1] * o_scr[0][h]
            for w, scr in zip(ws[1:], o_scr[1:]):
                acc = acc + w[:, h:h + 1] * scr[h]
            o_ref[:, sl] = acc
            ssq = ssq + jnp.sum(acc * acc, axis=-1, keepdims=True)
        r = lax.rsqrt(ssq * (1.0 / ATTN_W) + EPS)
        n_ref[...] = (o_ref[...] * r * g_ref[...]).astype(BF16)

    blk = pl.BlockSpec((ROW_BLOCK, ATTN_W), lambda i: (i, 0))
    outs_ = pl.pallas_call(
        body, name=name, grid=(s // ROW_BLOCK,),
        in_specs=[_res_spec(ATTN_HEADS, d) for d in DILATIONS] + [_res_spec(1, d) for d in DILATIONS]
        + [pl.BlockSpec((1, ATTN_W), lambda i: (0, 0))],
        out_specs=[blk, blk] + [_res_spec(1, d) for d in DILATIONS],
        out_shape=[jax.ShapeDtypeStruct((s, ATTN_W), F32), jax.ShapeDtypeStruct((s, ATTN_W), BF16)]
        + [_res_shape(s, 1, d, F32) for d in DILATIONS],
        scratch_shapes=[pltpu.VMEM((ATTN_HEADS, ROW_BLOCK, LANES), F32) for _ in DILATIONS]
        + [pltpu.VMEM((1, ROW_BLOCK, LANES), F32) for _ in DILATIONS],
        compiler_params=_params(("parallel",)),
    )(*outs, *lses, g)
    return outs_[0], outs_[1], list(outs_[2:])


def _attn_prebwd(dcat, o, g, name="attn_prebwd"):
    s = o.shape[0]
    nd = len(DILATIONS)

    def body(dy_ref, o_ref, g_ref, *rest):
        do_outs, delta_outs, gg_ref = rest[:nd], rest[nd:2 * nd], rest[2 * nd]
        do_scr, delta_scr = rest[2 * nd + 1], rest[2 * nd + 2]
        i = pl.program_id(0)
        dy, ov = dy_ref[...], o_ref[...]
        r = lax.rsqrt(jnp.mean(ov * ov, axis=-1, keepdims=True) + EPS)
        dyg = dy * g_ref[...]
        c = jnp.mean(dyg * ov, axis=-1, keepdims=True)
        do = r * dyg - ov * (r * r * r * c)
        prod = do * ov
        lane = lax.broadcasted_iota(jnp.int32, (ROW_BLOCK, LANES), 1)
        acc = jnp.zeros((ROW_BLOCK, LANES), F32)
        for h in range(ATTN_HEADS):
            sl = slice(h * HEAD_DIM, (h + 1) * HEAD_DIM)
            do_scr[h] = do[:, sl]
            acc = jnp.where(lane == h, jnp.sum(prod[:, sl], axis=-1, keepdims=True), acc)
        delta_scr[0] = acc
        for di, dil in enumerate(DILATIONS):
            _to_residues(do_scr, do_outs[di], dil)
            _to_residues(delta_scr, delta_outs[di], dil)

        @pl.when(i == 0)
        def _():
            gg_ref[...] = jnp.zeros_like(gg_ref)

        gg_ref[...] += jnp.sum(dy * ov * r, axis=0, keepdims=True)

    blk = pl.BlockSpec((ROW_BLOCK, ATTN_W), lambda i: (i, 0))
    vec = pl.BlockSpec((1, ATTN_W), lambda i: (0, 0))
    outs = pl.pallas_call(
        body, name=name, grid=(s // ROW_BLOCK,),
        in_specs=[blk, blk, vec],
        out_specs=[_res_spec(ATTN_HEADS, d) for d in DILATIONS] + [_res_spec(1, d) for d in DILATIONS] + [vec],
        out_shape=[_res_shape(s, ATTN_HEADS, d, BF16) for d in DILATIONS]
        + [_res_shape(s, 1, d, F32) for d in DILATIONS] + [jax.ShapeDtypeStruct((1, ATTN_W), F32)],
        scratch_shapes=[pltpu.VMEM((ATTN_HEADS, ROW_BLOCK, LANES), F32), pltpu.VMEM((1, ROW_BLOCK, LANES), F32)],
        compiler_params=_params(("arbitrary",)),
    )(dcat, o, g)
    return list(outs[:nd]), list(outs[nd:2 * nd]), outs[2 * nd]


def _attn_bwd(q, k, v, do, lse, delta, dil, name):
    length = q.shape[0]
    qb = ATTN_BLOCK
    nb = length // qb
    scale = HEAD_DIM ** -0.5

    def body(qp, qc, qn, kp, kc, kn, vp, vc, vn, dop, doc, don, lp, lc, ln, dp, dc, dn, dq_ref, dk_ref, dv_ref):
        valid_c, valid_ek, valid_eq = _band_masks(pl.program_id(1), length)
        everything = slice(None)
        lse_e, del_e = _edge(lp, ln, everything), _edge(dp, dn, everything)
        heads = [slice(h * HEAD_DIM, (h + 1) * HEAD_DIM) for h in range(ATTN_HEADS)]
        prods = []
        for sl in heads:
            q_c, k_c, v_c, do_c = qc[:, sl], kc[:, sl], vc[:, sl], doc[:, sl]
            q_e, k_e, v_e, do_e = _edge(qp, qn, sl), _edge(kp, kn, sl), _edge(vp, vn, sl), _edge(dop, don, sl)
            prods.append((_dot_nt(q_c, k_c), _dot_nt(do_c, v_c), _dot_nt(q_c, k_e), _dot_nt(do_c, v_e),
                          _dot_nt(q_e, k_c), _dot_nt(do_e, v_c)))
        parts = []
        for h, (s_cc, dp_cc, s_ek, dp_ek, s_eq, dp_eq) in enumerate(prods):
            hc = slice(h, h + 1)
            lse_c, del_c = lc[:, hc], dc[:, hc]
            p_cc = jnp.where(valid_c, jnp.exp(s_cc * scale - lse_c), 0.0)
            ds_cc = (p_cc * (dp_cc - del_c)).astype(BF16)
            p_ek = jnp.where(valid_ek, jnp.exp(s_ek * scale - lse_c), 0.0)
            ds_ek = (p_ek * (dp_ek - del_c)).astype(BF16)
            p_eq = jnp.where(valid_eq, jnp.exp(s_eq * scale - lse_e[:, hc]), 0.0)
            ds_eq = (p_eq * (dp_eq - del_e[:, hc])).astype(BF16)
            parts.append((p_cc.astype(BF16), ds_cc, ds_ek, p_eq.astype(BF16), ds_eq))
        for sl, (p_cc, ds_cc, ds_ek, p_eq, ds_eq) in zip(heads, parts):
            q_c, k_c, do_c = qc[:, sl], kc[:, sl], doc[:, sl]
            q_e, k_e, do_e = _edge(qp, qn, sl), _edge(kp, kn, sl), _edge(dop, don, sl)
            dq_ref[:, sl] = ((_dot(ds_cc, k_c) + _dot(ds_ek, k_e)) * scale).astype(BF16)
            dk_ref[:, sl] = ((_dot_tn(ds_cc, q_c) + _dot_tn(ds_eq, q_e)) * scale).astype(BF16)
            dv_ref[:, sl] = (_dot_tn(p_cc, do_c) + _dot_tn(p_eq, do_e)).astype(BF16)

    wide, narrow = list(_window_specs(nb, ATTN_W)), list(_window_specs(nb, LANES))
    return tuple(pl.pallas_call(
        body, name=name, grid=(dil, nb),
        in_specs=wide * 4 + narrow * 2,
        out_specs=[wide[1]] * 3,
        out_shape=[jax.ShapeDtypeStruct((length, dil * ATTN_W), BF16)] * 3,
        compiler_params=_params(("parallel", "parallel")),
    )(q, q, q, k, k, k, v, v, v, do, do, do, lse, lse, lse, delta, delta, delta))


def _gate_matrices(gf_up, gb_up):
    pad = LANES - 2 * GLA_RANK
    uf = jnp.concatenate([gf_up, jnp.zeros((GLA_RANK + pad, GLA_KW), gf_up.dtype)], axis=0)
    ub = jnp.concatenate([jnp.zeros((GLA_RANK, GLA_KW), gb_up.dtype), gb_up, jnp.zeros((pad, GLA_KW), gb_up.dtype)], axis=0)
    return uf.astype(BF16), ub.astype(BF16)


def _log_sigmoid(x):
    return jnp.minimum(x, 0.0) - jnp.log(1.0 + jnp.exp(-jnp.abs(x)))


def _gla_gates(proj, uf, ub, gf_b, gb_b, name="gla_gates"):
    s = proj.shape[0]

    def body(z_ref, uf_ref, ub_ref, bf_ref, bb_ref, gf_ref, gb_ref):
        z = z_ref[...].astype(BF16)
        gf_ref[...] = _log_sigmoid(_dot(z, uf_ref[...]) + bf_ref[...]) * (1.0 / GLA_GATE_NORM)
        gb_ref[...] = _log_sigmoid(_dot(z, ub_ref[...]) + bb_ref[...]) * (1.0 / GLA_GATE_NORM)

    mat = pl.BlockSpec((LANES, GLA_KW), lambda i: (0, 0))
    vec = pl.BlockSpec((1, GLA_KW), lambda i: (0, 0))
    out = pl.BlockSpec((ROW_BLOCK, GLA_KW), lambda i: (i, 0))
    return pl.pallas_call(
        body, name=name, grid=(s // ROW_BLOCK,),
        in_specs=[pl.BlockSpec((ROW_BLOCK, LANES), lambda i: (i, OFF_Z // LANES)), mat, mat, vec, vec],
        out_specs=[out, out],
        out_shape=[jax.ShapeDtypeStruct((s, GLA_KW), F32)] * 2,
        compiler_params=_params(("parallel",)),
    )(proj, uf, ub, gf_b, gb_b)


def _gla_gates_bwd(dgf, dgb, proj, uf, ub, gf_b, gb_b, dproj, name="gla_gates_bwd"):
    s = proj.shape[0]
    tail = IN_PAD - OFF_Z

    def body(dgf_ref, dgb_ref, z_ref, uf_ref, ub_ref, bf_ref, bb_ref, _, dz_ref, guf_ref, gub_ref, gbf_ref, gbb_ref):
        i = pl.program_id(0)
        z = z_ref[...].astype(BF16)
        uf_, ub_ = uf_ref[...], ub_ref[...]
        dpf = dgf_ref[...] * (1.0 / GLA_GATE_NORM) * _sigmoid(-(_dot(z, uf_) + bf_ref[...]))
        dpb = dgb_ref[...] * (1.0 / GLA_GATE_NORM) * _sigmoid(-(_dot(z, ub_) + bb_ref[...]))
        dpf_b, dpb_b = dpf.astype(BF16), dpb.astype(BF16)
        dz_ref[:, 0:LANES] = (_dot_nt(dpf_b, uf_) + _dot_nt(dpb_b, ub_)).astype(BF16)
        dz_ref[:, LANES:tail] = jnp.zeros((ROW_BLOCK, tail - LANES), BF16)

        @pl.when(i == 0)
        def _():
            for r in (guf_ref, gub_ref, gbf_ref, gbb_ref):
                r[...] = jnp.zeros_like(r)

        guf_ref[...] += _dot_tn(z, dpf_b)
        gub_ref[...] += _dot_tn(z, dpb_b)
        gbf_ref[...] += jnp.sum(dpf, axis=0, keepdims=True)
        gbb_ref[...] += jnp.sum(dpb, axis=0, keepdims=True)

    mat = pl.BlockSpec((LANES, GLA_KW), lambda i: (0, 0))
    vec = pl.BlockSpec((1, GLA_KW), lambda i: (0, 0))
    blk = pl.BlockSpec((ROW_BLOCK, GLA_KW), lambda i: (i, 0))
    return pl.pallas_call(
        body, name=name, grid=(s // ROW_BLOCK,),
        in_specs=[blk, blk, pl.BlockSpec((ROW_BLOCK, LANES), lambda i: (i, OFF_Z // LANES)), mat, mat, vec, vec,
                  pl.BlockSpec(memory_space=pl.ANY)],
        out_specs=[pl.BlockSpec((ROW_BLOCK, tail), lambda i: (i, OFF_Z // tail)), mat, mat, vec, vec],
        out_shape=[jax.ShapeDtypeStruct(dproj.shape, dproj.dtype), jax.ShapeDtypeStruct((LANES, GLA_KW), F32),
                   jax.ShapeDtypeStruct((LANES, GLA_KW), F32), jax.ShapeDtypeStruct((1, GLA_KW), F32),
                   jax.ShapeDtypeStruct((1, GLA_KW), F32)],
        input_output_aliases={7: 0},
        compiler_params=_params(("arbitrary",)),
    )(dgf, dgb, proj, uf, ub, gf_b, gb_b, dproj)


def _split3(x):
    x1 = x.astype(BF16)
    r1 = x - x1.astype(F32)
    x2 = r1.astype(BF16)
    x3 = (r1 - x2.astype(F32)).astype(BF16)
    return x1, x2, x3


def _dot_exact(mask_bf, x):
    x1, x2, x3 = _split3(x)
    return _dot(mask_bf, x1) + _dot(mask_bf, x2) + _dot(mask_bf, x3)


def _chunk_masks(reverse):
    c = GLA_CHUNK
    row = lax.broadcasted_iota(jnp.int32, (c, c), 0)
    col = lax.broadcasted_iota(jnp.int32, (c, c), 1)
    allowed = (col >= row) if reverse else (col <= row)
    seen_by = (col <= row) if reverse else (col >= row)
    return allowed, seen_by


def _chunk_terms(q_ref, k_ref, g_ref, rs, hs, allowed, reverse):
    c = GLA_CHUNK
    mid, last = (c // 2, 0) if reverse else (c // 2 - 1, c - 1)
    q = q_ref[rs, hs] * (GLA_DK ** -0.5)
    k = k_ref[rs, hs]
    b = _dot_exact(jnp.where(allowed, 1.0, 0.0).astype(BF16), g_ref[rs, hs])
    bref, blast = b[mid:mid + 1, :], b[last:last + 1, :]
    e_q, e_k, e_in, e_st = jnp.exp(b - bref), jnp.exp(bref - b), jnp.exp(b), jnp.exp(blast - b)
    return dict(last=last, e_q=e_q, e_k=e_k, e_in=e_in, e_st=e_st,
                dec=jnp.exp(blast), qe=q * e_q, ke=k * e_k, qin=q * e_in, kst=k * e_st)


def _gla_blockspecs(s, reverse_order):
    cb = GLA_CHUNKS_PER_STEP
    rows = cb * GLA_CHUNK
    nsteps = s // rows

    def rb(n):
        return (nsteps - 1 - n) if reverse_order else n

    qspec = pl.BlockSpec((rows, GLA_KW), lambda n: (rb(n), OFF_GQ // GLA_KW))
    kspec = pl.BlockSpec((rows, GLA_KW), lambda n: (rb(n), OFF_GK // GLA_KW))
    vspec = pl.BlockSpec((rows, GLA_VW), lambda n: (rb(n), OFF_GV // GLA_VW))
    gspec = pl.BlockSpec((rows, GLA_KW), lambda n: (rb(n), 0))
    ospec = pl.BlockSpec((rows, GLA_VW), lambda n: (rb(n), 0))
    sspec = pl.BlockSpec((GLA_HEADS, cb, GLA_DV, GLA_DK), lambda n: (0, rb(n), 0, 0))
    return cb, rows, nsteps, qspec, kspec, vspec, gspec, ospec, sspec


def _gla_units(cb, order_reversed):
    chunks = list(reversed(range(cb))) if order_reversed else list(range(cb))
    return [(c, h, slice(c * GLA_CHUNK, (c + 1) * GLA_CHUNK), slice(h * GLA_DK, (h + 1) * GLA_DK),
             slice(h * GLA_DV, (h + 1) * GLA_DV)) for c in chunks for h in range(GLA_HEADS)]


def _gla_fwd(proj, g, reverse, name):
    s = proj.shape[0]
    cb, rows, nsteps, qspec, kspec, vspec, gspec, ospec, sspec = _gla_blockspecs(s, reverse)

    def body(q_ref, k_ref, v_ref, g_ref, o_ref, st_ref, state):
        @pl.when(pl.program_id(0) == 0)
        def _():
            state[...] = jnp.zeros_like(state)

        allowed, _ = _chunk_masks(reverse)
        units = _gla_units(cb, reverse)
        terms = [_chunk_terms(q_ref, k_ref, g_ref, rs, hs, allowed, reverse) for _, _, rs, hs, _ in units]
        vals = [v_ref[rs, vs].astype(BF16) for _, _, rs, _, vs in units]
        raw = [(_dot_nt(t["qe"].astype(BF16), t["ke"].astype(BF16)), _dot_tn(v, t["kst"].astype(BF16)))
               for t, v in zip(terms, vals)]
        intra = [_dot(jnp.where(allowed, a, 0.0).astype(BF16), v) for (a, _), v in zip(raw, vals)]
        st = [state[h] for h in range(GLA_HEADS)]
        for (c, h, rs, _, vs), t, (_, kv), o_in in zip(units, terms, raw, intra):
            st_ref[h, c] = st[h]
            o_ref[rs, vs] = o_in + _dot_nt(t["qin"].astype(BF16), st[h].astype(BF16))
            st[h] = st[h] * t["dec"] + kv
        for h in range(GLA_HEADS):
            state[h] = st[h]

    return pl.pallas_call(
        body, name=name, grid=(nsteps,),
        in_specs=[qspec, kspec, vspec, gspec],
        out_specs=[ospec, sspec],
        out_shape=[jax.ShapeDtypeStruct((s, GLA_VW), F32),
                   jax.ShapeDtypeStruct((GLA_HEADS, s // GLA_CHUNK, GLA_DV, GLA_DK), F32)],
        scratch_shapes=[pltpu.VMEM((GLA_HEADS, GLA_DV, GLA_DK), F32)],
        compiler_params=_params(("arbitrary",)),
    )(proj, proj, proj, g)


def _gla_bwd(proj, g, do, states, reverse, name, merge=None):
    s = proj.shape[0]
    cb, rows, nsteps, qspec, kspec, vspec, gspec, ospec, sspec = _gla_blockspecs(s, not reverse)
    gla_cols = OFF_Z - OFF_GQ

    def body(q_ref, k_ref, v_ref, g_ref, do_ref, sp_ref, *rest):
        if merge is None:
            dq_ref, dk_ref, dv_ref, dg_ref, dstate = rest
        else:
            dq_o, dk_o, dv_o, dgr_ref, _, dp_ref, dg_ref, dstate = rest
        @pl.when(pl.program_id(0) == 0)
        def _():
            dstate[...] = jnp.zeros_like(dstate)

        allowed, seen_by = _chunk_masks(reverse)
        units = _gla_units(cb, not reverse)
        terms = [_chunk_terms(q_ref, k_ref, g_ref, rs, hs, allowed, reverse) for _, _, rs, hs, _ in units]
        vals = [v_ref[rs, vs].astype(BF16) for _, _, rs, _, vs in units]
        dos = [do_ref[rs, vs] for _, _, rs, _, vs in units]
        prevs = [sp_ref[h, c] for c, h, _, _, _ in units]
        raw = [(_dot_nt(t["qe"].astype(BF16), t["ke"].astype(BF16)), _dot_nt(do, v),
                _dot(do, sp.astype(BF16)), _dot_tn(do, t["qin"].astype(BF16)))
               for t, v, do, sp in zip(terms, vals, dos, prevs)]
        inner = []
        for t, do, (a, da, _, _) in zip(terms, dos, raw):
            da = jnp.where(allowed, da, 0.0).astype(BF16)
            inner.append((_dot(da, t["ke"].astype(BF16)), _dot_tn(da, t["qe"].astype(BF16)),
                          _dot_tn(jnp.where(allowed, a, 0.0).astype(BF16), do)))
        ds = [dstate[h] for h in range(GLA_HEADS)]
        outer = []
        for (c, h, _, _, _), t, v, sp, (_, _, _, inc) in zip(units, terms, vals, prevs, raw):
            ds_b = ds[h].astype(BF16)
            outer.append((_dot(v, ds_b), _dot_nt(t["kst"].astype(BF16), ds_b),
                          jnp.sum(sp * ds[h], axis=0, keepdims=True)))
            ds[h] = ds[h] * t["dec"] + inc
        for h in range(GLA_HEADS):
            dstate[h] = ds[h]
        seen_bf = jnp.where(seen_by, 1.0, 0.0).astype(BF16)
        rowi = lax.broadcasted_iota(jnp.int32, (GLA_CHUNK, GLA_DK), 0)
        for (c, h, rs, hs, vs), t, (_, _, dqin, _), (dqe, dke, dv_in), (dkst, dv_out, ddec) in zip(
                units, terms, raw, inner, outer):
            dq = (dqe * t["e_q"] + dqin * t["e_in"]) * (GLA_DK ** -0.5)
            dk = dke * t["e_k"] + dkst * t["e_st"]
            if merge is None:
                dq_ref[rs, hs], dk_ref[rs, hs], dv_ref[rs, vs] = dq, dk, dv_in + dv_out
            else:
                lo = OFF_GK - OFF_GQ + h * GLA_DK
                dp_ref[rs, hs] = (dq + dq_o[rs, hs]).astype(BF16)
                dp_ref[rs, lo:lo + GLA_DK] = (dk + dk_o[rs, hs]).astype(BF16)
                lo = OFF_GV - OFF_GQ + h * GLA_DV
                dp_ref[rs, lo:lo + GLA_DV] = (dv_in + dv_out + dv_o[rs, vs]).astype(BF16)
            kk = dkst * t["kst"]
            db = dqe * t["qe"] - dke * t["ke"] + dqin * t["qin"] - kk
            extra = jnp.sum(kk, axis=0, keepdims=True) + ddec * t["dec"]
            db = db + jnp.where(rowi == t["last"], extra, 0.0)
            dg_ref[rs, hs] = _dot_exact(seen_bf, db)
        if merge is not None:
            dp_ref[:, OFF_GR - OFF_GQ:gla_cols] = dgr_ref[...]

    scratch = [pltpu.VMEM((GLA_HEADS, GLA_DV, GLA_DK), F32)]
    if merge is None:
        return pl.pallas_call(
            body, name=name, grid=(nsteps,),
            in_specs=[qspec, kspec, vspec, gspec, ospec, sspec],
            out_specs=[gspec, gspec, ospec, gspec],
            out_shape=[jax.ShapeDtypeStruct((s, GLA_KW), F32), jax.ShapeDtypeStruct((s, GLA_KW), F32),
                       jax.ShapeDtypeStruct((s, GLA_VW), F32), jax.ShapeDtypeStruct((s, GLA_KW), F32)],
            scratch_shapes=scratch,
            compiler_params=_params(("arbitrary",)),
        )(proj, proj, proj, g, do, states)
    dproj = merge[4]
    block = gspec.index_map
    return pl.pallas_call(
        body, name=name, grid=(nsteps,),
        in_specs=[qspec, kspec, vspec, gspec, ospec, sspec, gspec, gspec, ospec, ospec, _ANY],
        out_specs=[pl.BlockSpec((rows, gla_cols), lambda n: (block(n)[0], OFF_GQ // gla_cols)), gspec],
        out_shape=[jax.ShapeDtypeStruct(dproj.shape, dproj.dtype), jax.ShapeDtypeStruct((s, GLA_KW), F32)],
        input_output_aliases={10: 0},
        scratch_shapes=scratch,
        compiler_params=_params(("arbitrary",)),
    )(proj, proj, proj, g, do, states, *merge)


def _gla_post(o_f, o_b, proj, g, name="gla_post"):
    s = o_f.shape[0]

    def body(of_ref, ob_ref, gr_ref, g_ref, o_ref):
        gv = g_ref[...]
        for h in range(GLA_HEADS):
            sl = slice(h * GLA_DV, (h + 1) * GLA_DV)
            osum = of_ref[:, sl] + ob_ref[:, sl]
            r = lax.rsqrt(jnp.mean(osum * osum, axis=-1, keepdims=True) + EPS)
            gr = gr_ref[:, sl]
            o_ref[:, sl] = (osum * r * gv * (gr * _sigmoid(gr))).astype(BF16)

    blk = pl.BlockSpec((ROW_BLOCK, GLA_VW), lambda i: (i, 0))
    return pl.pallas_call(
        body, name=name, grid=(s // ROW_BLOCK,),
        in_specs=[blk, blk, pl.BlockSpec((ROW_BLOCK, GLA_VW), lambda i: (i, OFF_GR // GLA_VW)),
                  pl.BlockSpec((1, GLA_DV), lambda i: (0, 0))],
        out_specs=blk,
        out_shape=jax.ShapeDtypeStruct((s, GLA_VW), BF16),
        compiler_params=_params(("parallel",)),
    )(o_f, o_b, proj, g)


def _gla_post_bwd(dcat, o_f, o_b, proj, g, name="gla_post_bwd"):
    s = o_f.shape[0]

    def body(dy_ref, of_ref, ob_ref, gr_ref, g_ref, do_ref, dgr_ref, gg_ref):
        i = pl.program_id(0)
        gv = g_ref[...]
        gg = jnp.zeros((1, GLA_DV), F32)
        for h in range(GLA_HEADS):
            sl = slice(h * GLA_DV, (h + 1) * GLA_DV)
            osum = of_ref[:, sl] + ob_ref[:, sl]
            r = lax.rsqrt(jnp.mean(osum * osum, axis=-1, keepdims=True) + EPS)
            gr, dy = gr_ref[:, sl], dy_ref[:, sl]
            sg = _sigmoid(gr)
            dgr_ref[:, sl] = (dy * (osum * r * gv) * (sg * (1.0 + gr * (1.0 - sg)))).astype(BF16)
            dn = dy * (gr * sg)
            dng = dn * gv
            c = jnp.mean(dng * osum, axis=-1, keepdims=True)
            do_ref[:, sl] = (r * dng - osum * (r * r * r * c)).astype(BF16)
            gg = gg + jnp.sum(dn * osum * r, axis=0, keepdims=True)

        @pl.when(i == 0)
        def _():
            gg_ref[...] = jnp.zeros_like(gg_ref)

        gg_ref[...] += gg

    blk = pl.BlockSpec((ROW_BLOCK, GLA_VW), lambda i: (i, 0))
    vec = pl.BlockSpec((1, GLA_DV), lambda i: (0, 0))
    return pl.pallas_call(
        body, name=name, grid=(s // ROW_BLOCK,),
        in_specs=[pl.BlockSpec((ROW_BLOCK, GLA_VW), lambda i: (i, 1)), blk, blk,
                  pl.BlockSpec((ROW_BLOCK, GLA_VW), lambda i: (i, OFF_GR // GLA_VW)), vec],
        out_specs=[blk, blk, vec],
        out_shape=[jax.ShapeDtypeStruct((s, GLA_VW), BF16), jax.ShapeDtypeStruct((s, GLA_VW), BF16),
                   jax.ShapeDtypeStruct((1, GLA_DV), F32)],
        compiler_params=_params(("arbitrary",)),
    )(dcat, o_f, o_b, proj, g)


CONV_ROWS = 256
CONV_COLS = 1408
HALO = 16


def _halo_specs(s, tr, tc, col_of):
    per = tr // HALO
    last = s // HALO - 1
    cur = pl.BlockSpec((tr, tc), lambda c, i: (i, col_of(c)))
    prev = pl.BlockSpec((HALO, tc), lambda c, i: (jnp.maximum(i * per - 1, 0), col_of(c)))
    nxt = pl.BlockSpec((HALO, tc), lambda c, i: (jnp.minimum((i + 1) * per, last), col_of(c)))
    return prev, cur, nxt


def _extended(prev_ref, cur_ref, next_ref, i, s, tr):
    first, last = i == 0, i == s // tr - 1
    prev = jnp.where(first, 0.0, prev_ref[...].astype(F32))
    nxt = jnp.where(last, 0.0, next_ref[...].astype(F32))
    return jnp.concatenate([prev, cur_ref[...].astype(F32), nxt], axis=0)


def _conv_glu(gate, up, conv_w, conv_b, name="conv_glu"):
    s, f = gate.shape
    tr, tc = CONV_ROWS, CONV_COLS
    ext = tr + 2 * HALO

    def body(gp, gc, gn, up_ref, w_ref, b_ref, o_ref):
        i = pl.program_id(1)
        ge = _extended(gp, gc, gn, i, s, tr)
        w = w_ref[...]
        conv = (w[0:1] * pltpu.roll(ge, 1, 0) + w[1:2] * ge + w[2:3] * pltpu.roll(ge, ext - 1, 0))[HALO:HALO + tr]
        conv = conv + b_ref[...]
        o_ref[...] = (conv * _sigmoid(conv) * up_ref[...]).astype(BF16)

    prev, cur, nxt = _halo_specs(s, tr, tc, lambda c: c)
    return pl.pallas_call(
        body, name=name, grid=(f // tc, s // tr),
        in_specs=[prev, cur, nxt, cur, pl.BlockSpec((3, tc), lambda c, i: (0, c)), pl.BlockSpec((1, tc), lambda c, i: (0, c))],
        out_specs=cur,
        out_shape=jax.ShapeDtypeStruct((s, f), BF16),
        compiler_params=_params(("parallel", "parallel")),
    )(gate, gate, gate, up, conv_w, conv_b)


def _conv_glu_bwd(dact, gate, up, conv_w, conv_b, name="conv_glu_bwd"):
    s, f = gate.shape
    tr, tc = CONV_ROWS, CONV_COLS
    ext = tr + 2 * HALO

    def body(dp, dc, dn, gp, gc, gn, upp, upc, upn, w_ref, b_ref, dg_ref, du_ref, gw_ref, gb_ref):
        i = pl.program_id(1)
        ge = _extended(gp, gc, gn, i, s, tr)
        ue = _extended(upp, upc, upn, i, s, tr)
        de = _extended(dp, dc, dn, i, s, tr)
        w = w_ref[...]
        g_prev, g_next = pltpu.roll(ge, 1, 0), pltpu.roll(ge, ext - 1, 0)
        conv = w[0:1] * g_prev + w[1:2] * ge + w[2:3] * g_next + b_ref[...]
        sg = _sigmoid(conv)
        du_ref[...] = (de * (conv * sg))[HALO:HALO + tr].astype(BF16)
        dconv = de * ue * (sg * (1.0 + conv * (1.0 - sg)))
        dgate = w[0:1] * pltpu.roll(dconv, ext - 1, 0) + w[1:2] * dconv + w[2:3] * pltpu.roll(dconv, 1, 0)
        dg_ref[...] = dgate[HALO:HALO + tr].astype(BF16)
        inner = slice(HALO, HALO + tr)
        dci = dconv[inner]

        @pl.when(i == 0)
        def _():
            gw_ref[...] = jnp.zeros_like(gw_ref)
            gb_ref[...] = jnp.zeros_like(gb_ref)

        gw_ref[0:1, :] += jnp.sum(dci * g_prev[inner], axis=0, keepdims=True)
        gw_ref[1:2, :] += jnp.sum(dci * ge[inner], axis=0, keepdims=True)
        gw_ref[2:3, :] += jnp.sum(dci * g_next[inner], axis=0, keepdims=True)
        gb_ref[...] += jnp.sum(dci, axis=0, keepdims=True)

    prev, cur, nxt = _halo_specs(s, tr, tc, lambda c: c)
    wspec = pl.BlockSpec((3, tc), lambda c, i: (0, c))
    bspec = pl.BlockSpec((1, tc), lambda c, i: (0, c))
    return pl.pallas_call(
        body, name=name, grid=(f // tc, s // tr),
        in_specs=[prev, cur, nxt] * 3 + [wspec, bspec],
        out_specs=[cur, cur, wspec, bspec],
        out_shape=[jax.ShapeDtypeStruct((s, f), BF16), jax.ShapeDtypeStruct((s, f), BF16),
                   jax.ShapeDtypeStruct((3, f), F32), jax.ShapeDtypeStruct((1, f), F32)],
        compiler_params=_params(("parallel", "arbitrary")),
    )(dact, dact, dact, gate, gate, gate, up, up, up, conv_w, conv_b)


def _local_step(x, target, w, late_weights=None, grad_sink=None, first_dep=()):
    s = x.shape[0]
    tables = _rope_tables(s)
    uf, ub = _gate_matrices(w["gf_up"], w["gb_up"])
    if grad_sink is None:
        grad_sink = lambda names, grads: ()

    n1 = _rms_fwd(x, w["norm1_g"], "norm1")
    proj = _matmul([(n1, w["w_in"])], "nn", F32, 1024, 1280, D_MODEL, "in_proj", deps=first_dep)
    qkv = _rope_fwd(proj, tables)
    branches = [_attn_fwd(*qkv[di], d, f"attn_fwd_d{d}") for di, d in enumerate(DILATIONS)]
    o_mix, ao, lse = _attn_combine([b[0] for b in branches], [b[1] for b in branches], w["attn_norm_g"])
    g_f, g_b = _gla_gates(proj, uf, ub, w["gf_b"], w["gb_b"])
    o_f, st_f = _gla_fwd(proj, g_f, False, "gla_fwd_f")
    o_b, st_b = _gla_fwd(proj, g_b, True, "gla_fwd_b")
    go = _gla_post(o_f, o_b, proj, w["gla_norm_g"])
    cat = jnp.concatenate([ao, go], axis=1)
    if late_weights is not None:
        w = {**w, **late_weights(cat)}
    h1 = _matmul([(cat, w["w_out"])], "nn", F32, 512, 1024, D_MODEL, "out_proj", res=x)
    n2 = _rms_fwd(h1, w["norm2_g"], "norm2")
    gate = _matmul([(n2, w["w_gate"])], "nn", F32, 1024, 512, D_MODEL, "ffn_gate")
    up = _matmul([(n2, w["w_up"])], "nn", BF16, 1024, 512, D_MODEL, "ffn_up")
    act = _conv_glu(gate, up, w["conv_w"], w["conv_b"])
    h2 = _matmul([(act, w["w_down"])], "nn", F32, 1024, 1024, 2816, "ffn_down", res=h1)
    dh2, dh2_b, loss_acc, g_final = _final_loss(h2, target, w["final_norm_g"])

    dact = _matmul([(dh2_b, w["w_down"])], "nt", BF16, 1024, 512, D_MODEL, "d_act")
    g_w_down = _matmul([(act, dh2_b)], "tn", F32, 1408, 1024, 2048, "g_w_down")
    dep = grad_sink(["w_down"], [g_w_down])
    dgate, dup, g_conv_w, g_conv_b = _conv_glu_bwd(dact, gate, up, w["conv_w"], w["conv_b"])
    g_w_gate = _matmul([(n2, dgate)], "tn", F32, 2048, 512, 2048, "g_w_gate", deps=dep)
    g_w_up = _matmul([(n2, dup)], "tn", F32, 2048, 512, 2048, "g_w_up")
    dep = grad_sink(["w_gate", "w_up"], [g_w_gate, g_w_up])
    dn2 = _matmul([(dgate, w["w_gate"])], "nt", F32, 1024, 1024, 2816, "d_n2_gate", deps=dep)
    dn2 = _matmul([(dup, w["w_up"])], "nt", F32, 1024, 1024, 2816, "d_n2_up", res=dn2)
    dh1, dh1_b, g_norm2 = _rms_bwd(dn2, h1, w["norm2_g"], dh2, "norm2_bwd")

    g_w_out = _matmul([(cat, dh1_b)], "tn", F32, 1024, 1024, 2048, "g_w_out")
    dep = grad_sink(["w_out"], [g_w_out])
    dcat = _matmul([(dh1_b, w["w_out"])], "nt", F32, 512, 1024, D_MODEL, "d_cat", deps=dep)
    do_attn, delta, g_attn_norm = _attn_prebwd(dcat, o_mix, w["attn_norm_g"])
    grads = [_attn_bwd(*qkv[di], do_attn[di], lse[di], delta[di], d, f"attn_bwd_d{d}")
             for di, d in enumerate(DILATIONS)]
    dproj = _rope_bwd(grads, tables)
    do_gla, dgr, g_gla_norm = _gla_post_bwd(dcat, o_f, o_b, proj, w["gla_norm_g"])
    dq_f, dk_f, dv_f, dg_f = _gla_bwd(proj, g_f, do_gla, st_f, False, "gla_bwd_f")
    dproj, dg_b = _gla_bwd(proj, g_b, do_gla, st_b, True, "gla_bwd_b", merge=(dq_f, dk_f, dv_f, dgr, dproj))
    dproj, g_uf, g_ub, g_gf_b, g_gb_b = _gla_gates_bwd(dg_f, dg_b, proj, uf, ub, w["gf_b"], w["gb_b"], dproj)
    g_w_in = _matmul([(n1, dproj)], "tn", F32, 1024, 1280, 2048, "g_w_in")
    dep = grad_sink(["w_in"], [g_w_in])
    dn1 = _matmul([(dproj, w["w_in"])], "nt", F32, 1024, 2048, 1280, "d_n1", deps=dep)
    grad_x, _, g_norm1 = _rms_bwd(dn1, x, w["norm1_g"], dh1, "norm1_bwd")

    g = dict(norm1_g=g_norm1, w_in=g_w_in, gf_up=g_uf[:GLA_RANK], gf_b=g_gf_b,
             gb_up=g_ub[GLA_RANK:2 * GLA_RANK], gb_b=g_gb_b, gla_norm_g=g_gla_norm, attn_norm_g=g_attn_norm,
             w_out=g_w_out, norm2_g=g_norm2, w_gate=g_w_gate, w_up=g_w_up, conv_w=g_conv_w, conv_b=g_conv_b,
             w_down=g_w_down, final_norm_g=g_final)
    return loss_acc, grad_x, g


def _me_and_peers():
    x, y, c = lax.axis_index("x"), lax.axis_index("y"), lax.axis_index("c")
    me = 4 * x + 2 * y + c
    peers = []
    for kbits in range(1, N_DEV):
        px, py, pc = x ^ (kbits >> 2 & 1), y ^ (kbits >> 1 & 1), c ^ (kbits & 1)
        peers.append(((px, py, pc), 4 * px + 2 * py + pc))
    return me, peers


_HBM = pl.BlockSpec(memory_space=pltpu.HBM)
_SEM = pl.BlockSpec(memory_space=pltpu.SEMAPHORE)
_ANY = pl.BlockSpec(memory_space=pl.ANY)
_EFFECT = pltpu.SideEffectType.DATAFLOW_SIDE_EFFECTING


def _exchange_copies(src_refs, land_refs, send_sems, recv_sems, scatter):
    me, peers = _me_and_peers()
    out = []
    for a, (src, land) in enumerate(zip(src_refs, land_refs)):
        for kk, (dev, idx) in enumerate(peers):
            out.append(pltpu.make_async_remote_copy(
                src_ref=src.at[idx] if scatter else src, dst_ref=land.at[me],
                send_sem=send_sems.at[a * (N_DEV - 1) + kk], recv_sem=recv_sems.at[a * (N_DEV - 1) + kk],
                device_id=dev, device_id_type=MESH_ID))
    return out


def _exchange_start(srcs, lands, scatter, name, deps=()):
    n, nd = len(srcs), len(deps)

    def body(*refs):
        src_refs, land_refs = refs[:n], refs[n:2 * n]
        send_sems, recv_sems = refs[2 * n + nd:2 * n + nd + 2]
        token = refs[-1]
        for cp in _exchange_copies(src_refs, land_refs, send_sems, recv_sems, scatter):
            cp.start()
        token[...] = jnp.zeros_like(token)

    outs = pl.pallas_call(
        body, name=name,
        in_specs=[_HBM] * (2 * n) + [_ANY] * nd,
        out_specs=[_SEM, _SEM] + [_HBM] * (2 * n) + [pl.BlockSpec(memory_space=pltpu.VMEM)],
        out_shape=[pltpu.SemaphoreType.DMA((n * (N_DEV - 1),)), pltpu.SemaphoreType.DMA((n * (N_DEV - 1),))]
        + [pltpu.HBM(t.shape, t.dtype) for t in srcs] + [pltpu.HBM(t.shape, t.dtype) for t in lands]
        + [jax.ShapeDtypeStruct((SUBLANES, LANES), F32)],
        input_output_aliases={i: 2 + i for i in range(2 * n)},
        compiler_params=pltpu.CompilerParams(has_side_effects=_EFFECT),
    )(*[pltpu.with_memory_space_constraint(t, pltpu.HBM) for t in list(srcs) + list(lands)], *deps)
    send_sems, recv_sems = outs[0], outs[1]
    return dict(send=send_sems, recv=recv_sems, srcs=outs[2:2 + n], lands=outs[2 + n:2 + 2 * n],
                scatter=scatter, token=outs[-1])


def _exchange_wait(started, name, after):
    n = len(started["srcs"])
    scatter = started["scatter"]

    def body(*refs):
        src_refs, land_refs = refs[:n], refs[n:2 * n]
        send_sems, recv_sems = refs[2 * n], refs[2 * n + 1]
        for cp in _exchange_copies(src_refs, land_refs, send_sems, recv_sems, scatter):
            cp.wait_send()
            cp.wait_recv()

    outs = pl.pallas_call(
        body, name=name,
        in_specs=[_HBM] * (2 * n) + [_SEM, _SEM, _ANY],
        out_specs=[_HBM] * (2 * n),
        out_shape=[pltpu.HBM(t.shape, t.dtype) for t in started["srcs"]]
        + [pltpu.HBM(t.shape, t.dtype) for t in started["lands"]],
        input_output_aliases={i: i for i in range(2 * n)},
        compiler_params=pltpu.CompilerParams(has_side_effects=_EFFECT),
    )(*started["srcs"], *started["lands"], started["send"], started["recv"], after)
    return outs[:n], outs[n:]


def _all_gather_two_level(shard, name):
    def body(x_ref, out_ref, send_sems, recv_sems, local_sem):
        x, y, c = lax.axis_index("x"), lax.axis_index("y"), lax.axis_index("c")
        me, sibling = (x, y, c), (x, y, 1 - c)
        chips = [(1 - x, y), (x, 1 - y), (1 - x, 1 - y)]

        def slot(px, py, pc):
            return out_ref.at[4 * px + 2 * py + pc]

        def copy(k, block, to, src=None):
            return pltpu.make_async_remote_copy(
                src_ref=slot(*block) if src is None else src, dst_ref=slot(*block),
                send_sem=send_sems.at[k], recv_sem=recv_sems.at[k], device_id=to, device_id_type=MESH_ID)

        mine = pltpu.make_async_copy(x_ref, slot(*me), local_sem)
        mine.start()
        first = [copy(0, me, sibling, src=x_ref)]
        first += [copy(1 + j, me, (*chip, c), src=x_ref) for j, chip in enumerate(chips)]
        for cp in first:
            cp.start()
        passed = [copy(4 + j, (*chip, c), sibling) for j, chip in enumerate(chips)]
        for j, chip in enumerate(chips):
            copy(1 + j, (*chip, c), me).wait_recv()
            passed[j].start()
        copy(0, sibling, me).wait_recv()
        for j, chip in enumerate(chips):
            copy(4 + j, (*chip, 1 - c), me).wait_recv()
        for cp in first + passed:
            cp.wait_send()
        mine.wait()

    return pl.pallas_call(
        body, name=name,
        in_specs=[_ANY], out_specs=_ANY,
        out_shape=jax.ShapeDtypeStruct((N_DEV,) + shard.shape, shard.dtype),
        scratch_shapes=[pltpu.SemaphoreType.DMA((N_DEV - 1,)), pltpu.SemaphoreType.DMA((N_DEV - 1,)),
                        pltpu.SemaphoreType.DMA],
    )(shard)


def _all_gather_vmem(vec, name):
    r = vec.shape[0]

    def body(v_ref, o_ref, send_sems, recv_sems):
        me, peers = _me_and_peers()
        o_ref[me] = v_ref[...]
        sends = []
        for kk, (dev, _) in enumerate(peers):
            cp = pltpu.make_async_remote_copy(
                src_ref=v_ref, dst_ref=o_ref.at[me],
                send_sem=send_sems.at[kk], recv_sem=recv_sems.at[kk],
                device_id=dev, device_id_type=MESH_ID)
            cp.start()
            sends.append(cp)
        for kk, (dev, idx) in enumerate(peers):
            pltpu.make_async_remote_copy(
                src_ref=v_ref, dst_ref=o_ref.at[idx],
                send_sem=send_sems.at[kk], recv_sem=recv_sems.at[kk],
                device_id=dev, device_id_type=MESH_ID).wait_recv()
        for cp in sends:
            cp.wait_send()

    return pl.pallas_call(
        body, name=name,
        in_specs=[pl.BlockSpec(memory_space=pltpu.VMEM)],
        out_specs=pl.BlockSpec(memory_space=pltpu.VMEM),
        out_shape=jax.ShapeDtypeStruct((N_DEV, r, LANES), F32),
        scratch_shapes=[pltpu.SemaphoreType.DMA((N_DEV - 1,)), pltpu.SemaphoreType.DMA((N_DEV - 1,))],
        compiler_params=pltpu.CompilerParams(vmem_limit_bytes=VMEM_LIMIT),
    )(vec)


def _adamw_math(w, g, m, v):
    m = ADAM_B1 * m + (1.0 - ADAM_B1) * g
    v = ADAM_B2 * v + (1.0 - ADAM_B2) * (g * g)
    m_hat = m / (1.0 - ADAM_B1 ** ADAM_STEP)
    v_hat = v / (1.0 - ADAM_B2 ** ADAM_STEP)
    delta = -ADAM_LR * (m_hat / (jnp.sqrt(v_hat) + ADAM_EPS) + ADAM_WD * w)
    return delta, m, v


def _adamw_sum(parts, w, m, v, tr, name, own=None, me=None):
    r, c = w.shape

    def body(*refs):
        if own is None:
            p_ref, w_ref, m_ref, v_ref, g_ref, d_ref, nm_ref, nv_ref = refs
            terms = [p_ref[kk] for kk in range(N_DEV)]
        else:
            me_ref, p_ref, own_ref, w_ref, m_ref, v_ref, g_ref, d_ref, nm_ref, nv_ref = refs
            terms = [jnp.where(me_ref[0] == kk, own_ref[0], p_ref[kk]).astype(F32) for kk in range(N_DEV)]
        g = terms[0]
        for t in terms[1:]:
            g = g + t
        g_ref[...] = g
        d_ref[...], nm_ref[...], nv_ref[...] = _adamw_math(w_ref[...], g, m_ref[...], v_ref[...])

    out_shape = [jax.ShapeDtypeStruct((r, c), F32)] * 4
    if own is None:
        blk = pl.BlockSpec((tr, c), lambda i: (i, 0))
        return pl.pallas_call(
            body, name=name, grid=(r // tr,),
            in_specs=[pl.BlockSpec((N_DEV, tr, c), lambda i: (0, i, 0)), blk, blk, blk],
            out_specs=[blk] * 4, out_shape=out_shape,
            compiler_params=_params(("parallel",)),
        )(parts, w, m, v)
    blk = pl.BlockSpec((tr, c), lambda i, me_ref: (i, 0))
    return pl.pallas_call(
        body, name=name,
        grid_spec=pltpu.PrefetchScalarGridSpec(
            num_scalar_prefetch=1, grid=(r // tr,),
            in_specs=[pl.BlockSpec((N_DEV, tr, c), lambda i, me_ref: (0, i, 0)),
                      pl.BlockSpec((1, tr, c), lambda i, me_ref: (me_ref[0], i, 0)), blk, blk, blk],
            out_specs=[blk] * 4),
        out_shape=out_shape,
        compiler_params=_params(("parallel",)),
    )(jnp.reshape(me, (1,)).astype(jnp.int32), parts, own, w, m, v)


_SMALL = ("norm1_g", "gf_b", "gb_b", "gla_norm_g", "attn_norm_g", "norm2_g", "conv_b", "final_norm_g",
          "gf_up", "gb_up", "conv_w")


def _pack(named):
    flat = jnp.concatenate([jnp.ravel(t).astype(F32) for t in named])
    tile = SUBLANES * LANES
    total = -(-flat.shape[0] // tile) * tile
    return jnp.pad(flat, (0, total - flat.shape[0])).reshape(total // LANES, LANES)


def _unpack(packed, shapes):
    flat = packed.reshape(-1)
    out, off = [], 0
    for shp in shapes:
        size = int(np.prod(shp))
        out.append(flat[off:off + size].reshape(shp))
        off += size
    return out


def kernel(x, norm1_g, w_in, gf_up, gf_b, gb_up, gb_b, gla_norm_g, attn_norm_g, w_out, norm2_g, w_gate, w_up, conv_w, conv_b, w_down, final_norm_g, loss_target, m_norm1_g, m_w_in, m_gf_up, m_gf_b, m_gb_up, m_gb_b, m_gla_norm_g, m_attn_norm_g, m_w_out, m_norm2_g, m_w_gate, m_w_up, m_conv_w, m_conv_b, m_w_down, m_final_norm_g, v_norm1_g, v_w_in, v_gf_up, v_gf_b, v_gb_up, v_gb_b, v_gla_norm_g, v_attn_norm_g, v_w_out, v_norm2_g, v_w_gate, v_w_up, v_conv_w, v_conv_b, v_w_down, v_final_norm_g):
    names = ("norm1_g", "w_in", "gf_up", "gf_b", "gb_up", "gb_b", "gla_norm_g", "attn_norm_g", "w_out", "norm2_g",
             "w_gate", "w_up", "conv_w", "conv_b", "w_down", "final_norm_g")
    ws = dict(zip(names, (norm1_g, w_in, gf_up, gf_b, gb_up, gb_b, gla_norm_g, attn_norm_g, w_out, norm2_g,
                          w_gate, w_up, conv_w, conv_b, w_down, final_norm_g)))
    ms = dict(zip(names, (m_norm1_g, m_w_in, m_gf_up, m_gf_b, m_gb_up, m_gb_b, m_gla_norm_g, m_attn_norm_g, m_w_out,
                          m_norm2_g, m_w_gate, m_w_up, m_conv_w, m_conv_b, m_w_down, m_final_norm_g)))
    vs = dict(zip(names, (v_norm1_g, v_w_in, v_gf_up, v_gf_b, v_gb_up, v_gb_b, v_gla_norm_g, v_attn_norm_g, v_w_out,
                          v_norm2_g, v_w_gate, v_w_up, v_conv_w, v_conv_b, v_w_down, v_final_norm_g)))
    me = 4 * lax.axis_index("x") + 2 * lax.axis_index("y") + lax.axis_index("c")
    big = ("w_in", "w_out", "w_gate", "w_up", "w_down")
    col_sharded = ("w_in", "w_gate", "w_up")

    def gather_start(group, name, deps=()):
        shards = [ws[n][0].astype(BF16) for n in group]
        lands = [lax.empty((N_DEV,) + t.shape, BF16) for t in shards]
        return _exchange_start(shards, lands, False, name, deps)

    def gather_finish(group, started, name, after):
        full = {}
        for n, own, t in zip(group, *_exchange_wait(started, name, after)):
            t = lax.dynamic_update_slice(t, own[None], (me, 0, 0))
            if n in col_sharded:
                full[n] = jnp.transpose(t, (1, 0, 2)).reshape(t.shape[1], N_DEV * t.shape[2])
            else:
                full[n] = t.reshape(N_DEV * t.shape[1], t.shape[2])
        return full

    w_in_all = _all_gather_two_level(ws["w_in"][0].astype(BF16), "gather_w_in")
    full = {"w_in": jnp.pad(jnp.transpose(w_in_all, (1, 0, 2)).reshape(D_MODEL, IN_WIDTH),
                            ((0, 0), (0, IN_PAD - IN_WIDTH)))}
    late = ("w_out", "w_gate", "w_up", "w_down")
    started_b = gather_start(late, "gather_late_start", deps=(full["w_in"],))

    def late_weights(after):
        return gather_finish(late, started_b, "gather_late_wait", after)

    small_sharded = ("gf_up", "gb_up", "conv_w")
    sm = _all_gather_vmem(_pack([ws[n][0] for n in small_sharded]), "gather_small")
    shard_shapes = [ws[n][0].shape for n in small_sharded]
    per_dev = [_unpack(sm[d], shard_shapes) for d in range(N_DEV)]
    for i, n in enumerate(small_sharded):
        full[n] = jnp.concatenate([per_dev[d][i] for d in range(N_DEV)], axis=1)
    for n in ("norm1_g", "gf_b", "gb_b", "gla_norm_g", "attn_norm_g", "norm2_g", "conv_b"):
        full[n] = ws[n]
    full["final_norm_g"] = final_norm_g.reshape(1, D_MODEL)

    in_flight = []

    def grad_sink(group, grads):
        partials = []
        for n, t in zip(group, grads):
            if n == "w_in":
                t = t[:, :IN_WIDTH].astype(BF16)
            if n in col_sharded:
                t = jnp.transpose(t.reshape(t.shape[0], N_DEV, t.shape[1] // N_DEV), (1, 0, 2))
            else:
                t = t.reshape(N_DEV, t.shape[0] // N_DEV, t.shape[1])
            partials.append(t)
        lands = [lax.empty(t.shape, t.dtype) for t in partials]
        started = _exchange_start(partials, lands, True, "exchange_" + "_".join(group) + "_start")
        in_flight.append((group, started))
        return (started["token"],)

    loss_acc, grad_x, g = _local_step(x[0], loss_target[0], full, late_weights, grad_sink,
                                      first_dep=(started_b["token"],))

    out = {}
    for group, started in in_flight:
        sent, landed = _exchange_wait(started, "exchange_" + "_".join(group) + "_wait", grad_x)
        for n, parts, own in zip(group, landed, sent):
            out[n] = _adamw_sum(parts, ws[n][0], ms[n][0], vs[n][0], 64, "adamw_" + n, own=own, me=me)

    small_full_shapes = [g[n].shape for n in _SMALL]
    gsmall = _pack([g[n] for n in _SMALL] + [loss_acc[0:1, 0:1]])
    gathered_small = _all_gather_vmem(gsmall, "gather_small_grads")

    def full_small(d):
        parts = []
        for n in _SMALL:
            t = d[n].reshape(d[n].shape[-2:]) if d[n].ndim == 3 else d[n].reshape(1, -1)
            if n in small_sharded:
                wide = jnp.zeros((t.shape[0], t.shape[1] * N_DEV), F32)
                t = lax.dynamic_update_slice_in_dim(wide, t, me * t.shape[1], axis=1)
            parts.append(t)
        return _pack(parts + [jnp.zeros((1, 1), F32)])

    rows = gsmall.shape[0]
    res_small = _adamw_sum(gathered_small, full_small(ws), full_small(ms), full_small(vs), rows, "adamw_small")
    loss = res_small[0].reshape(-1)[sum(int(np.prod(sh)) for sh in small_full_shapes)]
    unpacked = [_unpack(t, small_full_shapes) for t in res_small]
    for i, n in enumerate(_SMALL):
        vals = [u[i] for u in unpacked]
        if n in small_sharded:
            width = vals[0].shape[1] // N_DEV
            vals = [lax.dynamic_slice_in_dim(t, me * width, width, axis=1) for t in vals]
        out[n] = vals

    result = [loss, grad_x[None]]
    for kind in range(4):
        for n in names:
            result.append(out[n][kind].reshape(ws[n].shape))
    return tuple(result)
```

```python
import functools

import numpy as np
import jax
import jax.numpy as jnp
from jax import lax
from jax.experimental import pallas as pl
from jax.experimental.pallas import tpu as pltpu

F32 = jnp.float32
BF16 = jnp.bfloat16

D_MODEL = 2048
ATTN_W = 1024
ATTN_HEADS = 8
HEAD_DIM = 128
ROPE_DIM = 32
ROPE_THETA = 500000.0
DILATIONS = (1, 4, 16)
N_SIDE = 64
GLA_KW = 512
GLA_VW = 1024
GLA_HEADS = 4
GLA_DK = 128
GLA_DV = 256
GLA_RANK = 16
GLA_GATE_NORM = 16.0
GLA_CHUNK = 64
IN_WIDTH = 6176
IN_PAD = 6400
D_FF = 5632
EPS = 1e-6
N_DEV = 8

OFF_AQ, OFF_AK, OFF_AV = 0, 1024, 2048
OFF_GQ, OFF_GK, OFF_GV, OFF_GR, OFF_Z = 3072, 3584, 4096, 5120, 6144

ADAM_LR, ADAM_B1, ADAM_B2, ADAM_EPS, ADAM_WD, ADAM_STEP = 0.001, 0.9, 0.999, 1e-08, 0.01, 10

LANES = 128
SUBLANES = 8
VMEM_LIMIT = 56 * 1024 * 1024
ROW_BLOCK = 256
ATTN_BLOCK = 128
GLA_CHUNKS_PER_STEP = 4
NEG = -1e30
MESH_ID = pl.DeviceIdType.MESH


def _params(sem):
    return pltpu.CompilerParams(dimension_semantics=sem, vmem_limit_bytes=VMEM_LIMIT)


def _dot(a, b):
    return lax.dot_general(a, b, (((1,), (0,)), ((), ())), preferred_element_type=F32)


def _dot_nt(a, b):
    return lax.dot_general(a, b, (((1,), (1,)), ((), ())), preferred_element_type=F32)


def _dot_tn(a, b):
    return lax.dot_general(a, b, (((0,), (0,)), ((), ())), preferred_element_type=F32)


def _sigmoid(x):
    return 0.5 * jnp.tanh(0.5 * x) + 0.5


def _matmul(pairs, mode, out_dtype, tm, tn, tk, name, res=None, deps=()):
    a0, b0 = pairs[0]
    if mode == "nn":
        (m, kdim), n = a0.shape, b0.shape[1]
    elif mode == "nt":
        (m, kdim), n = a0.shape, b0.shape[0]
    else:
        (kdim, m), n = a0.shape, b0.shape[1]
    assert m % tm == 0 and n % tn == 0 and kdim % tk == 0, (name, m, n, kdim)
    nk = kdim // tk
    npairs = len(pairs)
    steps = nk * npairs
    dot = {"nn": _dot, "nt": _dot_nt, "tn": _dot_tn}[mode]

    def kidx(p):
        return lambda k: jnp.clip(k - p * nk, 0, nk - 1)

    in_specs, args = [], []
    for p, (a, b) in enumerate(pairs):
        kk = kidx(p)
        if mode == "nn":
            in_specs += [pl.BlockSpec((tm, tk), lambda i, j, k, kk=kk: (i, kk(k))),
                         pl.BlockSpec((tk, tn), lambda i, j, k, kk=kk: (kk(k), j))]
        elif mode == "nt":
            in_specs += [pl.BlockSpec((tm, tk), lambda i, j, k, kk=kk: (i, kk(k))),
                         pl.BlockSpec((tn, tk), lambda i, j, k, kk=kk: (j, kk(k)))]
        else:
            in_specs += [pl.BlockSpec((tk, tm), lambda i, j, k, kk=kk: (kk(k), i)),
                         pl.BlockSpec((tk, tn), lambda i, j, k, kk=kk: (kk(k), j))]
        args += [a, b]
    if res is not None:
        in_specs.append(pl.BlockSpec((tm, tn), lambda i, j, k: (i, j)))
        args.append(res)
    in_specs += [pl.BlockSpec(memory_space=pl.ANY)] * len(deps)
    args += list(deps)

    def body(*refs):
        ab = refs[:2 * npairs]
        res_ref = refs[2 * npairs] if res is not None else None
        o_ref = refs[2 * npairs + (1 if res is not None else 0) + len(deps)]

        def finish(acc):
            if res_ref is not None:
                acc = acc + res_ref[...]
            o_ref[...] = acc.astype(out_dtype)

        if steps == 1:
            finish(dot(ab[0][...], ab[1][...]))
            return
        acc_ref = refs[-1]
        k = pl.program_id(2)

        @pl.when(k == 0)
        def _():
            acc_ref[...] = jnp.zeros_like(acc_ref)

        for p in range(npairs):
            @pl.when((k >= p * nk) & (k < (p + 1) * nk))
            def _(p=p):
                acc_ref[...] += dot(ab[2 * p][...], ab[2 * p + 1][...])

        @pl.when(k == steps - 1)
        def _():
            finish(acc_ref[...])

    return pl.pallas_call(
        body, name=name,
        grid=(m // tm, n // tn, steps),
        in_specs=in_specs,
        out_specs=pl.BlockSpec((tm, tn), lambda i, j, k: (i, j)),
        out_shape=jax.ShapeDtypeStruct((m, n), out_dtype),
        scratch_shapes=[] if steps == 1 else [pltpu.VMEM((tm, tn), F32)],
        compiler_params=_params(("parallel", "parallel", "arbitrary")),
    )(*args)


def _rms_fwd(x, g, name):
    s, d = x.shape

    def body(x_ref, g_ref, o_ref):
        xv = x_ref[...]
        r = lax.rsqrt(jnp.mean(xv * xv, axis=-1, keepdims=True) + EPS)
        o_ref[...] = (xv * r * g_ref[...]).astype(BF16)

    return pl.pallas_call(
        body, name=name, grid=(s // ROW_BLOCK,),
        in_specs=[pl.BlockSpec((ROW_BLOCK, d), lambda i: (i, 0)), pl.BlockSpec((1, d), lambda i: (0, 0))],
        out_specs=pl.BlockSpec((ROW_BLOCK, d), lambda i: (i, 0)),
        out_shape=jax.ShapeDtypeStruct((s, d), BF16),
        compiler_params=_params(("parallel",)),
    )(x, g)


def _rms_bwd(dn, x, g, dres, name):
    s, d = x.shape

    def body(dn_ref, x_ref, g_ref, dres_ref, dx_ref, dxb_ref, gg_ref):
        i = pl.program_id(0)
        xv, dnv = x_ref[...], dn_ref[...]
        r = lax.rsqrt(jnp.mean(xv * xv, axis=-1, keepdims=True) + EPS)
        dng = dnv * g_ref[...]
        c = jnp.mean(dng * xv, axis=-1, keepdims=True)
        dx = dres_ref[...] + r * dng - xv * (r * r * r * c)
        dx_ref[...] = dx
        dxb_ref[...] = dx.astype(BF16)

        @pl.when(i == 0)
        def _():
            gg_ref[...] = jnp.zeros_like(gg_ref)

        gg_ref[...] += jnp.sum(dnv * xv * r, axis=0, keepdims=True)

    row = pl.BlockSpec((ROW_BLOCK, d), lambda i: (i, 0))
    vec = pl.BlockSpec((1, d), lambda i: (0, 0))
    return pl.pallas_call(
        body, name=name, grid=(s // ROW_BLOCK,),
        in_specs=[row, row, vec, row],
        out_specs=[row, row, vec],
        out_shape=[jax.ShapeDtypeStruct((s, d), F32), jax.ShapeDtypeStruct((s, d), BF16),
                   jax.ShapeDtypeStruct((1, d), F32)],
        compiler_params=_params(("arbitrary",)),
    )(dn, x, g, dres)


def _final_loss(h2, target, g, name="final_loss"):
    s, d = h2.shape

    def body(h_ref, t_ref, g_ref, dh_ref, dhb_ref, loss_ref, gg_ref):
        i = pl.program_id(0)
        hv, gv = h_ref[...], g_ref[...]
        r = lax.rsqrt(jnp.mean(hv * hv, axis=-1, keepdims=True) + EPS)
        e = hv * r * gv - t_ref[...]
        dy = e * (1.0 / d)
        dyg = dy * gv
        c = jnp.mean(dyg * hv, axis=-1, keepdims=True)
        dh = r * dyg - hv * (r * r * r * c)
        dh_ref[...] = dh
        dhb_ref[...] = dh.astype(BF16)

        @pl.when(i == 0)
        def _():
            gg_ref[...] = jnp.zeros_like(gg_ref)
            loss_ref[...] = jnp.zeros_like(loss_ref)

        gg_ref[...] += jnp.sum(dy * hv * r, axis=0, keepdims=True)
        loss_ref[...] += jnp.sum(jnp.sum(e * e, axis=-1, keepdims=True), axis=0, keepdims=True) * (0.5 / d)

    row = pl.BlockSpec((ROW_BLOCK, d), lambda i: (i, 0))
    vec = pl.BlockSpec((1, d), lambda i: (0, 0))
    return pl.pallas_call(
        body, name=name, grid=(s // ROW_BLOCK,),
        in_specs=[row, row, vec],
        out_specs=[row, row, pl.BlockSpec((SUBLANES, LANES), lambda i: (0, 0)), vec],
        out_shape=[jax.ShapeDtypeStruct((s, d), F32), jax.ShapeDtypeStruct((s, d), BF16),
                   jax.ShapeDtypeStruct((SUBLANES, LANES), F32), jax.ShapeDtypeStruct((1, d), F32)],
        compiler_params=_params(("arbitrary",)),
    )(h2, target, g)


def _rope_tables(s):
    pos = jnp.arange(s, dtype=F32)
    inv_freq = ROPE_THETA ** (-jnp.arange(0, ROPE_DIM, 2, dtype=F32) / ROPE_DIM)
    ang = pos[:, None] * inv_freq[None, :]
    cos, sin = jnp.cos(ang), jnp.sin(ang)
    half = ROPE_DIM // 2
    rest = HEAD_DIM - ROPE_DIM
    c = jnp.concatenate([cos, cos, jnp.ones((s, rest), F32)], axis=1)
    sm = jnp.concatenate([-sin, jnp.zeros((s, half + rest), F32)], axis=1)
    sp = jnp.concatenate([jnp.zeros((s, half), F32), sin, jnp.zeros((s, rest), F32)], axis=1)
    return c, sm, sp


def _res_shape(s, groups, dil, dtype):
    return jax.ShapeDtypeStruct((s // dil, dil * groups * LANES), dtype)


def _res_spec(groups, dil):
    return pl.BlockSpec((ROW_BLOCK // dil, dil * groups * LANES), lambda i: (i, 0))


def _to_residues(scr, o_ref, dil):
    groups, rows = scr.shape[0], ROW_BLOCK // dil
    for r in range(dil):
        for h in range(groups):
            piece = scr[h] if dil == 1 else scr.at[h][pl.ds(r, rows, stride=dil), :]
            o_ref[:, (r * groups + h) * LANES:(r * groups + h + 1) * LANES] = piece.astype(o_ref.dtype)


def _from_residues(i_ref, scr, dil):
    groups, rows = scr.shape[0], ROW_BLOCK // dil
    for r in range(dil):
        for h in range(groups):
            piece = i_ref[:, (r * groups + h) * LANES:(r * groups + h + 1) * LANES].astype(F32)
            if dil == 1:
                scr[h] = piece
            else:
                scr.at[h][pl.ds(r, rows, stride=dil), :] = piece


def _rope_fwd(proj, tables, name="rope_fwd"):
    s = proj.shape[0]
    half = ROPE_DIM // 2
    nd = len(DILATIONS)

    def body(p_ref, c_ref, sm_ref, sp_ref, *rest):
        outs, scr = rest[:3 * nd], rest[3 * nd]
        c, sm, sp = c_ref[...], sm_ref[...], sp_ref[...]
        for gi, off in enumerate((OFF_AQ, OFF_AK, OFF_AV)):
            for h in range(ATTN_HEADS):
                t = p_ref[:, off + h * HEAD_DIM: off + (h + 1) * HEAD_DIM]
                if off != OFF_AV:
                    t = t * c + pltpu.roll(t, HEAD_DIM - half, 1) * sm + pltpu.roll(t, half, 1) * sp
                scr[h] = t
            for di, dil in enumerate(DILATIONS):
                _to_residues(scr, outs[3 * di + gi], dil)

    tab = pl.BlockSpec((ROW_BLOCK, HEAD_DIM), lambda i: (i, 0))
    outs = pl.pallas_call(
        body, name=name, grid=(s // ROW_BLOCK,),
        in_specs=[pl.BlockSpec((ROW_BLOCK, 3 * ATTN_W), lambda i: (i, 0)), tab, tab, tab],
        out_specs=[_res_spec(ATTN_HEADS, d) for d in DILATIONS for _ in range(3)],
        out_shape=[_res_shape(s, ATTN_HEADS, d, BF16) for d in DILATIONS for _ in range(3)],
        scratch_shapes=[pltpu.VMEM((ATTN_HEADS, ROW_BLOCK, LANES), F32)],
        compiler_params=_params(("parallel",)),
    )(proj, *tables)
    return [tuple(outs[3 * di:3 * di + 3]) for di in range(nd)]


def _rope_bwd(grads, tables, name="rope_bwd"):
    s = grads[0][0].shape[0] * DILATIONS[0]
    half = ROPE_DIM // 2
    nd = len(DILATIONS)

    def body(*refs):
        ins = refs[:3 * nd]
        c_ref, sm_ref, sp_ref, o_ref = refs[3 * nd:3 * nd + 4]
        scrs = refs[3 * nd + 4:]
        c, sm, sp = c_ref[...], sm_ref[...], sp_ref[...]
        for gi, off in enumerate((OFF_AQ, OFF_AK, OFF_AV)):
            for di, dil in enumerate(DILATIONS):
                _from_residues(ins[3 * di + gi], scrs[di], dil)
            for h in range(ATTN_HEADS):
                t = scrs[0][h]
                for scr in scrs[1:]:
                    t = t + scr[h]
                if off != OFF_AV:
                    t = t * c + pltpu.roll(t * sm, half, 1) + pltpu.roll(t * sp, HEAD_DIM - half, 1)
                o_ref[:, off + h * HEAD_DIM: off + (h + 1) * HEAD_DIM] = t.astype(BF16)

    tab = pl.BlockSpec((ROW_BLOCK, HEAD_DIM), lambda i: (i, 0))
    return pl.pallas_call(
        body, name=name, grid=(s // ROW_BLOCK,),
        in_specs=[_res_spec(ATTN_HEADS, d) for d in DILATIONS for _ in range(3)] + [tab, tab, tab],
        out_specs=pl.BlockSpec((ROW_BLOCK, 3 * ATTN_W), lambda i: (i, 0)),
        out_shape=jax.ShapeDtypeStruct((s, IN_PAD), BF16),
        scratch_shapes=[pltpu.VMEM((ATTN_HEADS, ROW_BLOCK, LANES), F32) for _ in DILATIONS],
        compiler_params=_params(("parallel",)),
    )(*[t for g in grads for t in g], *tables)


def _window_specs(nb, width):
    qb, hb = ATTN_BLOCK, N_SIDE
    cur = pl.BlockSpec((qb, width), lambda r, j: (j, r))
    prev = pl.BlockSpec((hb, width), lambda r, j: (jnp.maximum(2 * j - 1, 0), r))
    nxt = pl.BlockSpec((hb, width), lambda r, j: (jnp.minimum(2 * j + 2, 2 * nb - 1), r))
    return prev, cur, nxt


def _band_masks(j, length):
    qb, hb = ATTN_BLOCK, N_SIDE
    row = lax.broadcasted_iota(jnp.int32, (qb, qb), 0)
    col = lax.broadcasted_iota(jnp.int32, (qb, qb), 1)

    def edge_pos(i):
        return j * qb - hb + i + jnp.where(i >= hb, qb, 0)

    def ok(a, b, outside):
        return (jnp.abs(a - b) <= N_SIDE) & (outside >= 0) & (outside < length)

    cur = jnp.abs(row - col) <= N_SIDE
    edge_k = ok(j * qb + row, edge_pos(col), edge_pos(col))
    edge_q = ok(edge_pos(row), j * qb + col, edge_pos(row))
    return cur, edge_k, edge_q


def _edge(prev_ref, next_ref, sl):
    return jnp.concatenate([prev_ref[:, sl], next_ref[:, sl]], axis=0)


def _attn_fwd(q, k, v, dil, name):
    length = q.shape[0]
    qb = ATTN_BLOCK
    nb = length // qb
    scale = HEAD_DIM ** -0.5

    def body(q_ref, kp_ref, kc_ref, kn_ref, vp_ref, vc_ref, vn_ref, o_ref, lse_ref):
        valid_c, valid_e, _ = _band_masks(pl.program_id(1), length)
        lane = lax.broadcasted_iota(jnp.int32, (qb, LANES), 1)
        lse_acc = jnp.zeros((qb, LANES), F32)
        heads = [slice(h * HEAD_DIM, (h + 1) * HEAD_DIM) for h in range(ATTN_HEADS)]
        scores = [(_dot_nt(q_ref[:, sl], kc_ref[:, sl]), _dot_nt(q_ref[:, sl], _edge(kp_ref, kn_ref, sl)))
                  for sl in heads]
        probs = []
        for h, (s_c, s_e) in enumerate(scores):
            s_c = jnp.where(valid_c, s_c * scale, NEG)
            s_e = jnp.where(valid_e, s_e * scale, NEG)
            m = jnp.max(jnp.maximum(s_c, s_e), axis=-1, keepdims=True)
            p_c, p_e = jnp.exp(s_c - m), jnp.exp(s_e - m)
            den = jnp.sum(p_c + p_e, axis=-1, keepdims=True)
            probs.append((p_c.astype(BF16), p_e.astype(BF16), 1.0 / den))
            lse_acc = jnp.where(lane == h, m + jnp.log(den), lse_acc)
        for sl, (p_c, p_e, inv) in zip(heads, probs):
            o_ref[:, sl] = (_dot(p_c, vc_ref[:, sl]) + _dot(p_e, _edge(vp_ref, vn_ref, sl))) * inv
        lse_ref[...] = lse_acc

    prev, cur, nxt = _window_specs(nb, ATTN_W)
    return pl.pallas_call(
        body, name=name, grid=(dil, nb),
        in_specs=[cur, prev, cur, nxt, prev, cur, nxt],
        out_specs=[cur, pl.BlockSpec((qb, LANES), lambda r, j: (j, r))],
        out_shape=[jax.ShapeDtypeStruct((length, dil * ATTN_W), F32),
                   jax.ShapeDtypeStruct((length, dil * LANES), F32)],
        compiler_params=_params(("parallel", "parallel")),
    )(q, k, k, k, v, v, v)


def _attn_combine(outs, lses, g, name="attn_combine"):
    s = outs[0].shape[0] * DILATIONS[0]
    nd = len(DILATIONS)

    def body(*refs):
        o_refs, l_refs = refs[:nd], refs[nd:2 * nd]
        g_ref, o_ref, n_ref = refs[2 * nd:2 * nd + 3]
        lse_outs = refs[2 * nd + 3:3 * nd + 3]
        o_scr, l_scr = refs[3 * nd + 3:4 * nd + 3], refs[4 * nd + 3:5 * nd + 3]
        for di, dil in enumerate(DILATIONS):
            _from_residues(o_refs[di], o_scr[di], dil)
            _from_residues(l_refs[di], l_scr[di], dil)
        ls = [scr[0] for scr in l_scr]
        m = ls[0]
        for l in ls[1:]:
            m = jnp.maximum(m, l)
        es = [jnp.exp(l - m) for l in ls]
        z = es[0]
        for e in es[1:]:
            z = z + e
        ws = [e / z for e in es]
        l_scr[0][0] = m + jnp.log(z)
        for di, dil in enumerate(DILATIONS):
            _to_residues(l_scr[0], lse_outs[di], dil)
        ssq = jnp.zeros((ROW_BLOCK, 1), F32)
        for h in range(ATTN_HEADS):
            sl = slice(h * HEAD_DIM, (h + 1) * HEAD_DIM)
            acc = ws[0][:, h:h + 1] * o_scr[0][h]
            for w, scr in zip(ws[1:], o_scr[1:]):
                acc = acc + w[:, h:h + 1] * scr[h]
            o_ref[:, sl] = acc
            ssq = ssq + jnp.sum(acc * acc, axis=-1, keepdims=True)
        r = lax.rsqrt(ssq * (1.0 / ATTN_W) + EPS)
        n_ref[...] = (o_ref[...] * r * g_ref[...]).astype(BF16)

    blk = pl.BlockSpec((ROW_BLOCK, ATTN_W), lambda i: (i, 0))
    outs_ = pl.pallas_call(
        body, name=name, grid=(s // ROW_BLOCK,),
        in_specs=[_res_spec(ATTN_HEADS, d) for d in DILATIONS] + [_res_spec(1, d) for d in DILATIONS]
        + [pl.BlockSpec((1, ATTN_W), lambda i: (0, 0))],
        out_specs=[blk, blk] + [_res_spec(1, d) for d in DILATIONS],
        out_shape=[jax.ShapeDtypeStruct((s, ATTN_W), F32), jax.ShapeDtypeStruct((s, D_MODEL), BF16)]
        + [_res_shape(s, 1, d, F32) for d in DILATIONS],
        scratch_shapes=[pltpu.VMEM((ATTN_HEADS, ROW_BLOCK, LANES), F32) for _ in DILATIONS]
        + [pltpu.VMEM((1, ROW_BLOCK, LANES), F32) for _ in DILATIONS],
        compiler_params=_params(("parallel",)),
    )(*outs, *lses, g)
    return outs_[0], outs_[1], list(outs_[2:])


def _attn_prebwd(dcat, o, g, name="attn_prebwd"):
    s = o.shape[0]
    nd = len(DILATIONS)

    def body(dy_ref, o_ref, g_ref, *rest):
        do_outs, delta_outs, gg_ref = rest[:nd], rest[nd:2 * nd], rest[2 * nd]
        do_scr, delta_scr = rest[2 * nd + 1], rest[2 * nd + 2]
        i = pl.program_id(0)
        dy, ov = dy_ref[...], o_ref[...]
        r = lax.rsqrt(jnp.mean(ov * ov, axis=-1, keepdims=True) + EPS)
        dyg = dy * g_ref[...]
        c = jnp.mean(dyg * ov, axis=-1, keepdims=True)
        do = r * dyg - ov * (r * r * r * c)
        prod = do * ov
        lane = lax.broadcasted_iota(jnp.int32, (ROW_BLOCK, LANES), 1)
        acc = jnp.zeros((ROW_BLOCK, LANES), F32)
        for h in range(ATTN_HEADS):
            sl = slice(h * HEAD_DIM, (h + 1) * HEAD_DIM)
            do_scr[h] = do[:, sl]
            acc = jnp.where(lane == h, jnp.sum(prod[:, sl], axis=-1, keepdims=True), acc)
        delta_scr[0] = acc
        for di, dil in enumerate(DILATIONS):
            _to_residues(do_scr, do_outs[di], dil)
            _to_residues(delta_scr, delta_outs[di], dil)

        @pl.when(i == 0)
        def _():
            gg_ref[...] = jnp.zeros_like(gg_ref)

        gg_ref[...] += jnp.sum(dy * ov * r, axis=0, keepdims=True)

    blk = pl.BlockSpec((ROW_BLOCK, ATTN_W), lambda i: (i, 0))
    vec = pl.BlockSpec((1, ATTN_W), lambda i: (0, 0))
    outs = pl.pallas_call(
        body, name=name, grid=(s // ROW_BLOCK,),
        in_specs=[blk, blk, vec],
        out_specs=[_res_spec(ATTN_HEADS, d) for d in DILATIONS] + [_res_spec(1, d) for d in DILATIONS] + [vec],
        out_shape=[_res_shape(s, ATTN_HEADS, d, BF16) for d in DILATIONS]
        + [_res_shape(s, 1, d, F32) for d in DILATIONS] + [jax.ShapeDtypeStruct((1, ATTN_W), F32)],
        scratch_shapes=[pltpu.VMEM((ATTN_HEADS, ROW_BLOCK, LANES), F32), pltpu.VMEM((1, ROW_BLOCK, LANES), F32)],
        compiler_params=_params(("arbitrary",)),
    )(dcat, o, g)
    return list(outs[:nd]), list(outs[nd:2 * nd]), outs[2 * nd]


def _attn_bwd(q, k, v, do, lse, delta, dil, name):
    length = q.shape[0]
    qb = ATTN_BLOCK
    nb = length // qb
    scale = HEAD_DIM ** -0.5

    def body(qp, qc, qn, kp, kc, kn, vp, vc, vn, dop, doc, don, lp, lc, ln, dp, dc, dn, dq_ref, dk_ref, dv_ref):
        valid_c, valid_ek, valid_eq = _band_masks(pl.program_id(1), length)
        everything = slice(None)
        lse_e, del_e = _edge(lp, ln, everything), _edge(dp, dn, everything)
        heads = [slice(h * HEAD_DIM, (h + 1) * HEAD_DIM) for h in range(ATTN_HEADS)]
        prods = []
        for sl in heads:
            q_c, k_c, v_c, do_c = qc[:, sl], kc[:, sl], vc[:, sl], doc[:, sl]
            q_e, k_e, v_e, do_e = _edge(qp, qn, sl), _edge(kp, kn, sl), _edge(vp, vn, sl), _edge(dop, don, sl)
            prods.append((_dot_nt(q_c, k_c), _dot_nt(do_c, v_c), _dot_nt(q_c, k_e), _dot_nt(do_c, v_e),
                          _dot_nt(q_e, k_c), _dot_nt(do_e, v_c)))
        parts = []
        for h, (s_cc, dp_cc, s_ek, dp_ek, s_eq, dp_eq) in enumerate(prods):
            hc = slice(h, h + 1)
            lse_c, del_c = lc[:, hc], dc[:, hc]
            p_cc = jnp.where(valid_c, jnp.exp(s_cc * scale - lse_c), 0.0)
            ds_cc = (p_cc * (dp_cc - del_c)).astype(BF16)
            p_ek = jnp.where(valid_ek, jnp.exp(s_ek * scale - lse_c), 0.0)
            ds_ek = (p_ek * (dp_ek - del_c)).astype(BF16)
            p_eq = jnp.where(valid_eq, jnp.exp(s_eq * scale - lse_e[:, hc]), 0.0)
            ds_eq = (p_eq * (dp_eq - del_e[:, hc])).astype(BF16)
            parts.append((p_cc.astype(BF16), ds_cc, ds_ek, p_eq.astype(BF16), ds_eq))
        for sl, (p_cc, ds_cc, ds_ek, p_eq, ds_eq) in zip(heads, parts):
            q_c, k_c, do_c = qc[:, sl], kc[:, sl], doc[:, sl]
            q_e, k_e, do_e = _edge(qp, qn, sl), _edge(kp, kn, sl), _edge(dop, don, sl)
            dq_ref[:, sl] = ((_dot(ds_cc, k_c) + _dot(ds_ek, k_e)) * scale).astype(BF16)
            dk_ref[:, sl] = ((_dot_tn(ds_cc, q_c) + _dot_tn(ds_eq, q_e)) * scale).astype(BF16)
            dv_ref[:, sl] = (_dot_tn(p_cc, do_c) + _dot_tn(p_eq, do_e)).astype(BF16)

    wide, narrow = list(_window_specs(nb, ATTN_W)), list(_window_specs(nb, LANES))
    return tuple(pl.pallas_call(
        body, name=name, grid=(dil, nb),
        in_specs=wide * 4 + narrow * 2,
        out_specs=[wide[1]] * 3,
        out_shape=[jax.ShapeDtypeStruct((length, dil * ATTN_W), BF16)] * 3,
        compiler_params=_params(("parallel", "parallel")),
    )(q, q, q, k, k, k, v, v, v, do, do, do, lse, lse, lse, delta, delta, delta))


def _gate_matrices(gf_up, gb_up):
    pad = LANES - 2 * GLA_RANK
    uf = jnp.concatenate([gf_up, jnp.zeros((GLA_RANK + pad, GLA_KW), gf_up.dtype)], axis=0)
    ub = jnp.concatenate([jnp.zeros((GLA_RANK, GLA_KW), gb_up.dtype), gb_up, jnp.zeros((pad, GLA_KW), gb_up.dtype)], axis=0)
    return uf.astype(BF16), ub.astype(BF16)


def _log_sigmoid(x):
    return jnp.minimum(x, 0.0) - jnp.log(1.0 + jnp.exp(-jnp.abs(x)))


def _gla_gates(proj, uf, ub, gf_b, gb_b, name="gla_gates"):
    s = proj.shape[0]

    def body(z_ref, uf_ref, ub_ref, bf_ref, bb_ref, gf_ref, gb_ref):
        z = z_ref[...].astype(BF16)
        gf_ref[...] = _log_sigmoid(_dot(z, uf_ref[...]) + bf_ref[...]) * (1.0 / GLA_GATE_NORM)
        gb_ref[...] = _log_sigmoid(_dot(z, ub_ref[...]) + bb_ref[...]) * (1.0 / GLA_GATE_NORM)

    mat = pl.BlockSpec((LANES, GLA_KW), lambda i: (0, 0))
    vec = pl.BlockSpec((1, GLA_KW), lambda i: (0, 0))
    out = pl.BlockSpec((ROW_BLOCK, GLA_KW), lambda i: (i, 0))
    return pl.pallas_call(
        body, name=name, grid=(s // ROW_BLOCK,),
        in_specs=[pl.BlockSpec((ROW_BLOCK, LANES), lambda i: (i, OFF_Z // LANES)), mat, mat, vec, vec],
        out_specs=[out, out],
        out_shape=[jax.ShapeDtypeStruct((s, GLA_KW), F32)] * 2,
        compiler_params=_params(("parallel",)),
    )(proj, uf, ub, gf_b, gb_b)


def _gla_gates_bwd(dgf, dgb, proj, uf, ub, gf_b, gb_b, dproj, name="gla_gates_bwd"):
    s = proj.shape[0]
    tail = IN_PAD - OFF_Z

    def body(dgf_ref, dgb_ref, z_ref, uf_ref, ub_ref, bf_ref, bb_ref, _, dz_ref, guf_ref, gub_ref, gbf_ref, gbb_ref):
        i = pl.program_id(0)
        z = z_ref[...].astype(BF16)
        uf_, ub_ = uf_ref[...], ub_ref[...]
        dpf = dgf_ref[...] * (1.0 / GLA_GATE_NORM) * _sigmoid(-(_dot(z, uf_) + bf_ref[...]))
        dpb = dgb_ref[...] * (1.0 / GLA_GATE_NORM) * _sigmoid(-(_dot(z, ub_) + bb_ref[...]))
        dpf_b, dpb_b = dpf.astype(BF16), dpb.astype(BF16)
        dz_ref[:, 0:LANES] = (_dot_nt(dpf_b, uf_) + _dot_nt(dpb_b, ub_)).astype(BF16)
        dz_ref[:, LANES:tail] = jnp.zeros((ROW_BLOCK, tail - LANES), BF16)

        @pl.when(i == 0)
        def _():
            for r in (guf_ref, gub_ref, gbf_ref, gbb_ref):
                r[...] = jnp.zeros_like(r)

        guf_ref[...] += _dot_tn(z, dpf_b)
        gub_ref[...] += _dot_tn(z, dpb_b)
        gbf_ref[...] += jnp.sum(dpf, axis=0, keepdims=True)
        gbb_ref[...] += jnp.sum(dpb, axis=0, keepdims=True)

    mat = pl.BlockSpec((LANES, GLA_KW), lambda i: (0, 0))
    vec = pl.BlockSpec((1, GLA_KW), lambda i: (0, 0))
    blk = pl.BlockSpec((ROW_BLOCK, GLA_KW), lambda i: (i, 0))
    return pl.pallas_call(
        body, name=name, grid=(s // ROW_BLOCK,),
        in_specs=[blk, blk, pl.BlockSpec((ROW_BLOCK, LANES), lambda i: (i, OFF_Z // LANES)), mat, mat, vec, vec,
                  pl.BlockSpec(memory_space=pl.ANY)],
        out_specs=[pl.BlockSpec((ROW_BLOCK, tail), lambda i: (i, OFF_Z // tail)), mat, mat, vec, vec],
        out_shape=[jax.ShapeDtypeStruct(dproj.shape, dproj.dtype), jax.ShapeDtypeStruct((LANES, GLA_KW), F32),
                   jax.ShapeDtypeStruct((LANES, GLA_KW), F32), jax.ShapeDtypeStruct((1, GLA_KW), F32),
                   jax.ShapeDtypeStruct((1, GLA_KW), F32)],
        input_output_aliases={7: 0},
        compiler_params=_params(("arbitrary",)),
    )(dgf, dgb, proj, uf, ub, gf_b, gb_b, dproj)


def _split3(x):
    x1 = x.astype(BF16)
    r1 = x - x1.astype(F32)
    x2 = r1.astype(BF16)
    x3 = (r1 - x2.astype(F32)).astype(BF16)
    return x1, x2, x3


def _dot_exact(mask_bf, x):
    x1, x2, x3 = _split3(x)
    return _dot(mask_bf, x1) + _dot(mask_bf, x2) + _dot(mask_bf, x3)


def _chunk_masks(reverse):
    c = GLA_CHUNK
    row = lax.broadcasted_iota(jnp.int32, (c, c), 0)
    col = lax.broadcasted_iota(jnp.int32, (c, c), 1)
    allowed = (col >= row) if reverse else (col <= row)
    seen_by = (col <= row) if reverse else (col >= row)
    return allowed, seen_by


def _chunk_terms(q_ref, k_ref, g_ref, rs, hs, allowed, reverse):
    c = GLA_CHUNK
    mid, last = (c // 2, 0) if reverse else (c // 2 - 1, c - 1)
    q = q_ref[rs, hs] * (GLA_DK ** -0.5)
    k = k_ref[rs, hs]
    b = _dot_exact(jnp.where(allowed, 1.0, 0.0).astype(BF16), g_ref[rs, hs])
    bref, blast = b[mid:mid + 1, :], b[last:last + 1, :]
    e_q, e_k, e_in, e_st = jnp.exp(b - bref), jnp.exp(bref - b), jnp.exp(b), jnp.exp(blast - b)
    return dict(last=last, e_q=e_q, e_k=e_k, e_in=e_in, e_st=e_st,
                dec=jnp.exp(blast), qe=q * e_q, ke=k * e_k, qin=q * e_in, kst=k * e_st)


def _gla_blockspecs(s, reverse_order):
    cb = GLA_CHUNKS_PER_STEP
    rows = cb * GLA_CHUNK
    nsteps = s // rows

    def rb(n):
        return (nsteps - 1 - n) if reverse_order else n

    qspec = pl.BlockSpec((rows, GLA_KW), lambda n: (rb(n), OFF_GQ // GLA_KW))
    kspec = pl.BlockSpec((rows, GLA_KW), lambda n: (rb(n), OFF_GK // GLA_KW))
    vspec = pl.BlockSpec((rows, GLA_VW), lambda n: (rb(n), OFF_GV // GLA_VW))
    gspec = pl.BlockSpec((rows, GLA_KW), lambda n: (rb(n), 0))
    ospec = pl.BlockSpec((rows, GLA_VW), lambda n: (rb(n), 0))
    sspec = pl.BlockSpec((GLA_HEADS, cb, GLA_DV, GLA_DK), lambda n: (0, rb(n), 0, 0))
    return cb, rows, nsteps, qspec, kspec, vspec, gspec, ospec, sspec


def _gla_units(cb, order_reversed):
    chunks = list(reversed(range(cb))) if order_reversed else list(range(cb))
    return [(c, h, slice(c * GLA_CHUNK, (c + 1) * GLA_CHUNK), slice(h * GLA_DK, (h + 1) * GLA_DK),
             slice(h * GLA_DV, (h + 1) * GLA_DV)) for c in chunks for h in range(GLA_HEADS)]


def _gla_fwd(proj, g, reverse, name):
    s = proj.shape[0]
    cb, rows, nsteps, qspec, kspec, vspec, gspec, ospec, sspec = _gla_blockspecs(s, reverse)

    def body(q_ref, k_ref, v_ref, g_ref, o_ref, st_ref, state):
        @pl.when(pl.program_id(0) == 0)
        def _():
            state[...] = jnp.zeros_like(state)

        allowed, _ = _chunk_masks(reverse)
        units = _gla_units(cb, reverse)
        terms = [_chunk_terms(q_ref, k_ref, g_ref, rs, hs, allowed, reverse) for _, _, rs, hs, _ in units]
        vals = [v_ref[rs, vs].astype(BF16) for _, _, rs, _, vs in units]
        raw = [(_dot_nt(t["qe"].astype(BF16), t["ke"].astype(BF16)), _dot_tn(v, t["kst"].astype(BF16)))
               for t, v in zip(terms, vals)]
        intra = [_dot(jnp.where(allowed, a, 0.0).astype(BF16), v) for (a, _), v in zip(raw, vals)]
        st = [state[h] for h in range(GLA_HEADS)]
        for (c, h, rs, _, vs), t, (_, kv), o_in in zip(units, terms, raw, intra):
            st_ref[h, c] = st[h]
            o_ref[rs, vs] = o_in + _dot_nt(t["qin"].astype(BF16), st[h].astype(BF16))
            st[h] = st[h] * t["dec"] + kv
        for h in range(GLA_HEADS):
            state[h] = st[h]

    return pl.pallas_call(
        body, name=name, grid=(nsteps,),
        in_specs=[qspec, kspec, vspec, gspec],
        out_specs=[ospec, sspec],
        out_shape=[jax.ShapeDtypeStruct((s, GLA_VW), F32),
                   jax.ShapeDtypeStruct((GLA_HEADS, s // GLA_CHUNK, GLA_DV, GLA_DK), F32)],
        scratch_shapes=[pltpu.VMEM((GLA_HEADS, GLA_DV, GLA_DK), F32)],
        compiler_params=_params(("arbitrary",)),
    )(proj, proj, proj, g)


def _gla_bwd(proj, g, do, states, reverse, name, merge=None):
    s = proj.shape[0]
    cb, rows, nsteps, qspec, kspec, vspec, gspec, ospec, sspec = _gla_blockspecs(s, not reverse)
    gla_cols = OFF_Z - OFF_GQ

    def body(q_ref, k_ref, v_ref, g_ref, do_ref, sp_ref, *rest):
        if merge is None:
            dq_ref, dk_ref, dv_ref, dg_ref, dstate = rest
        else:
            dq_o, dk_o, dv_o, dgr_ref, _, dp_ref, dg_ref, dstate = rest
        @pl.when(pl.program_id(0) == 0)
        def _():
            dstate[...] = jnp.zeros_like(dstate)

        allowed, seen_by = _chunk_masks(reverse)
        units = _gla_units(cb, not reverse)
        terms = [_chunk_terms(q_ref, k_ref, g_ref, rs, hs, allowed, reverse) for _, _, rs, hs, _ in units]
        vals = [v_ref[rs, vs].astype(BF16) for _, _, rs, _, vs in units]
        dos = [do_ref[rs, vs] for _, _, rs, _, vs in units]
        prevs = [sp_ref[h, c] for c, h, _, _, _ in units]
        raw = [(_dot_nt(t["qe"].astype(BF16), t["ke"].astype(BF16)), _dot_nt(do, v),
                _dot(do, sp.astype(BF16)), _dot_tn(do, t["qin"].astype(BF16)))
               for t, v, do, sp in zip(terms, vals, dos, prevs)]
        inner = []
        for t, do, (a, da, _, _) in zip(terms, dos, raw):
            da = jnp.where(allowed, da, 0.0).astype(BF16)
            inner.append((_dot(da, t["ke"].astype(BF16)), _dot_tn(da, t["qe"].astype(BF16)),
                          _dot_tn(jnp.where(allowed, a, 0.0).astype(BF16), do)))
        ds = [dstate[h] for h in range(GLA_HEADS)]
        outer = []
        for (c, h, _, _, _), t, v, sp, (_, _, _, inc) in zip(units, terms, vals, prevs, raw):
            ds_b = ds[h].astype(BF16)
            outer.append((_dot(v, ds_b), _dot_nt(t["kst"].astype(BF16), ds_b),
                          jnp.sum(sp * ds[h], axis=0, keepdims=True)))
            ds[h] = ds[h] * t["dec"] + inc
        for h in range(GLA_HEADS):
            dstate[h] = ds[h]
        seen_bf = jnp.where(seen_by, 1.0, 0.0).astype(BF16)
        rowi = lax.broadcasted_iota(jnp.int32, (GLA_CHUNK, GLA_DK), 0)
        for (c, h, rs, hs, vs), t, (_, _, dqin, _), (dqe, dke, dv_in), (dkst, dv_out, ddec) in zip(
                units, terms, raw, inner, outer):
            dq = (dqe * t["e_q"] + dqin * t["e_in"]) * (GLA_DK ** -0.5)
            dk = dke * t["e_k"] + dkst * t["e_st"]
            if merge is None:
                dq_ref[rs, hs], dk_ref[rs, hs], dv_ref[rs, vs] = dq, dk, dv_in + dv_out
            else:
                lo = OFF_GK - OFF_GQ + h * GLA_DK
                dp_ref[rs, hs] = (dq + dq_o[rs, hs]).astype(BF16)
                dp_ref[rs, lo:lo + GLA_DK] = (dk + dk_o[rs, hs]).astype(BF16)
                lo = OFF_GV - OFF_GQ + h * GLA_DV
                dp_ref[rs, lo:lo + GLA_DV] = (dv_in + dv_out + dv_o[rs, vs]).astype(BF16)
            kk = dkst * t["kst"]
            db = dqe * t["qe"] - dke * t["ke"] + dqin * t["qin"] - kk
            extra = jnp.sum(kk, axis=0, keepdims=True) + ddec * t["dec"]
            db = db + jnp.where(rowi == t["last"], extra, 0.0)
            dg_ref[rs, hs] = _dot_exact(seen_bf, db)
        if merge is not None:
            dp_ref[:, OFF_GR - OFF_GQ:gla_cols] = dgr_ref[...]

    scratch = [pltpu.VMEM((GLA_HEADS, GLA_DV, GLA_DK), F32)]
    if merge is None:
        return pl.pallas_call(
            body, name=name, grid=(nsteps,),
            in_specs=[qspec, kspec, vspec, gspec, ospec, sspec],
            out_specs=[gspec, gspec, ospec, gspec],
            out_shape=[jax.ShapeDtypeStruct((s, GLA_KW), F32), jax.ShapeDtypeStruct((s, GLA_KW), F32),
                       jax.ShapeDtypeStruct((s, GLA_VW), F32), jax.ShapeDtypeStruct((s, GLA_KW), F32)],
            scratch_shapes=scratch,
            compiler_params=_params(("arbitrary",)),
        )(proj, proj, proj, g, do, states)
    dproj = merge[4]
    block = gspec.index_map
    return pl.pallas_call(
        body, name=name, grid=(nsteps,),
        in_specs=[qspec, kspec, vspec, gspec, ospec, sspec, gspec, gspec, ospec, ospec, _ANY],
        out_specs=[pl.BlockSpec((rows, gla_cols), lambda n: (block(n)[0], OFF_GQ // gla_cols)), gspec],
        out_shape=[jax.ShapeDtypeStruct(dproj.shape, dproj.dtype), jax.ShapeDtypeStruct((s, GLA_KW), F32)],
        input_output_aliases={10: 0},
        scratch_shapes=scratch,
        compiler_params=_params(("arbitrary",)),
    )(proj, proj, proj, g, do, states, *merge)


def _gla_post(o_f, o_b, proj, g, cat, name="gla_post"):
    s = o_f.shape[0]

    def body(of_ref, ob_ref, gr_ref, g_ref, _, o_ref):
        gv = g_ref[...]
        for h in range(GLA_HEADS):
            sl = slice(h * GLA_DV, (h + 1) * GLA_DV)
            osum = of_ref[:, sl] + ob_ref[:, sl]
            r = lax.rsqrt(jnp.mean(osum * osum, axis=-1, keepdims=True) + EPS)
            gr = gr_ref[:, sl]
            o_ref[:, sl] = (osum * r * gv * (gr * _sigmoid(gr))).astype(BF16)

    blk = pl.BlockSpec((ROW_BLOCK, GLA_VW), lambda i: (i, 0))
    return pl.pallas_call(
        body, name=name, grid=(s // ROW_BLOCK,),
        in_specs=[blk, blk, pl.BlockSpec((ROW_BLOCK, GLA_VW), lambda i: (i, OFF_GR // GLA_VW)),
                  pl.BlockSpec((1, GLA_DV), lambda i: (0, 0)), pl.BlockSpec(memory_space=pl.ANY)],
        out_specs=pl.BlockSpec((ROW_BLOCK, GLA_VW), lambda i: (i, ATTN_W // GLA_VW)),
        out_shape=jax.ShapeDtypeStruct(cat.shape, cat.dtype),
        input_output_aliases={4: 0},
        compiler_params=_params(("parallel",)),
    )(o_f, o_b, proj, g, cat)


def _gla_post_bwd(dcat, o_f, o_b, proj, g, name="gla_post_bwd"):
    s = o_f.shape[0]

    def body(dy_ref, of_ref, ob_ref, gr_ref, g_ref, do_ref, dgr_ref, gg_ref):
        i = pl.program_id(0)
        gv = g_ref[...]
        gg = jnp.zeros((1, GLA_DV), F32)
        for h in range(GLA_HEADS):
            sl = slice(h * GLA_DV, (h + 1) * GLA_DV)
            osum = of_ref[:, sl] + ob_ref[:, sl]
            r = lax.rsqrt(jnp.mean(osum * osum, axis=-1, keepdims=True) + EPS)
            gr, dy = gr_ref[:, sl], dy_ref[:, sl]
            sg = _sigmoid(gr)
            dgr_ref[:, sl] = (dy * (osum * r * gv) * (sg * (1.0 + gr * (1.0 - sg)))).astype(BF16)
            dn = dy * (gr * sg)
            dng = dn * gv
            c = jnp.mean(dng * osum, axis=-1, keepdims=True)
            do_ref[:, sl] = (r * dng - osum * (r * r * r * c)).astype(BF16)
            gg = gg + jnp.sum(dn * osum * r, axis=0, keepdims=True)

        @pl.when(i == 0)
        def _():
            gg_ref[...] = jnp.zeros_like(gg_ref)

        gg_ref[...] += gg

    blk = pl.BlockSpec((ROW_BLOCK, GLA_VW), lambda i: (i, 0))
    vec = pl.BlockSpec((1, GLA_DV), lambda i: (0, 0))
    return pl.pallas_call(
        body, name=name, grid=(s // ROW_BLOCK,),
        in_specs=[pl.BlockSpec((ROW_BLOCK, GLA_VW), lambda i: (i, 1)), blk, blk,
                  pl.BlockSpec((ROW_BLOCK, GLA_VW), lambda i: (i, OFF_GR // GLA_VW)), vec],
        out_specs=[blk, blk, vec],
        out_shape=[jax.ShapeDtypeStruct((s, GLA_VW), BF16), jax.ShapeDtypeStruct((s, GLA_VW), BF16),
                   jax.ShapeDtypeStruct((1, GLA_DV), F32)],
        compiler_params=_params(("arbitrary",)),
    )(dcat, o_f, o_b, proj, g)


HALO = 16


def _extended(prev_ref, cur_ref, next_ref, i, s, tr):
    first, last = i == 0, i == s // tr - 1
    prev = jnp.where(first, 0.0, prev_ref[...].astype(F32))
    nxt = jnp.where(last, 0.0, next_ref[...].astype(F32))
    return jnp.concatenate([prev, cur_ref[...].astype(F32), nxt], axis=0)


FFN_ROWS = 512
FFN_COLS = 512


def _ffn_in(n2, w_gate, w_up, conv_w, conv_b, name="ffn_in"):
    s, d = n2.shape
    f = w_gate.shape[1]
    tm, tn, edge = FFN_ROWS, FFN_COLS, SUBLANES
    ni = s // tm
    ext = tm + 2 * edge

    def body(a_ref, wg_ref, wu_ref, w_ref, b_ref, gate_ref, up_ref, act_ref, g_tile, u_tile, g_tail):
        i = pl.program_id(1)

        @pl.when(i == 0)
        def _():
            g_tile[...] = jnp.zeros_like(g_tile)
            u_tile[...] = jnp.zeros_like(u_tile)
            g_tail[...] = jnp.zeros_like(g_tail)

        a = a_ref[...]
        g_new = _dot(a, wg_ref[...])
        u_new = _dot(a, wu_ref[...])
        g_old, u_old = g_tile[...], u_tile[...]
        before = jnp.where(i == 1, 0.0, g_tail[...])
        after = jnp.where(i == ni, 0.0, g_new[0:edge])
        ge = jnp.concatenate([before, g_old, after], axis=0)
        w = w_ref[...]
        conv = (w[0:1] * pltpu.roll(ge, 1, 0) + w[1:2] * ge + w[2:3] * pltpu.roll(ge, ext - 1, 0))[edge:edge + tm]
        conv = conv + b_ref[...]
        gate_ref[...] = g_old
        up_ref[...] = u_old
        act_ref[...] = (conv * _sigmoid(conv) * u_old.astype(F32)).astype(BF16)
        g_tail[...] = g_old[tm - edge:tm]
        g_tile[...] = g_new
        u_tile[...] = u_new.astype(BF16)

    lag = pl.BlockSpec((tm, tn), lambda j, i: (jnp.maximum(i - 1, 0), j))
    return pl.pallas_call(
        body, name=name, grid=(f // tn, ni + 1),
        in_specs=[pl.BlockSpec((tm, d), lambda j, i: (jnp.minimum(i, ni - 1), 0)),
                  pl.BlockSpec((d, tn), lambda j, i: (0, j)), pl.BlockSpec((d, tn), lambda j, i: (0, j)),
                  pl.BlockSpec((3, tn), lambda j, i: (0, j)), pl.BlockSpec((1, tn), lambda j, i: (0, j))],
        out_specs=[lag, lag, lag],
        out_shape=[jax.ShapeDtypeStruct((s, f), F32), jax.ShapeDtypeStruct((s, f), BF16),
                   jax.ShapeDtypeStruct((s, f), BF16)],
        scratch_shapes=[pltpu.VMEM((tm, tn), F32), pltpu.VMEM((tm, tn), BF16), pltpu.VMEM((edge, tn), F32)],
        compiler_params=_params(("parallel", "arbitrary")),
    )(n2, w_gate, w_up, conv_w, conv_b)


def _ffn_mid_bwd(dh2, w_down, gate, up, conv_w, conv_b, name="ffn_mid_bwd"):
    s, d = dh2.shape
    f = gate.shape[1]
    tm, tn = FFN_ROWS, FFN_COLS
    ni = s // tm
    ext = tm + 2 * HALO
    per, last_halo = tm // HALO, s // HALO - 1

    def body(a_ref, wd_ref, gp, gc, gn, upp, upc, upn, w_ref, b_ref, dg_ref, du_ref, gw_ref, gb_ref, d_tile, d_tail):
        i = pl.program_id(1)

        @pl.when(i == 0)
        def _():
            d_tile[...] = jnp.zeros_like(d_tile)
            d_tail[...] = jnp.zeros_like(d_tail)
            gw_ref[...] = jnp.zeros_like(gw_ref)
            gb_ref[...] = jnp.zeros_like(gb_ref)

        d_new = _dot_nt(a_ref[...], wd_ref[...])
        d_old = d_tile[...]
        before = jnp.where(i == 1, 0.0, d_tail[...])
        after = jnp.where(i == ni, 0.0, d_new[0:HALO])
        de = jnp.concatenate([before, d_old, after], axis=0)
        ge = _extended(gp, gc, gn, i - 1, s, tm)
        ue = _extended(upp, upc, upn, i - 1, s, tm)
        w = w_ref[...]
        g_prev, g_next = pltpu.roll(ge, 1, 0), pltpu.roll(ge, ext - 1, 0)
        conv = w[0:1] * g_prev + w[1:2] * ge + w[2:3] * g_next + b_ref[...]
        sg = _sigmoid(conv)
        inner = slice(HALO, HALO + tm)
        du_ref[...] = (de * (conv * sg))[inner].astype(BF16)
        dconv = de * ue * (sg * (1.0 + conv * (1.0 - sg)))
        dgate = w[0:1] * pltpu.roll(dconv, ext - 1, 0) + w[1:2] * dconv + w[2:3] * pltpu.roll(dconv, 1, 0)
        dg_ref[...] = dgate[inner].astype(BF16)
        dci = jnp.where(i > 0, dconv[inner], 0.0)
        gw_ref[0:1, :] += jnp.sum(dci * g_prev[inner], axis=0, keepdims=True)
        gw_ref[1:2, :] += jnp.sum(dci * ge[inner], axis=0, keepdims=True)
        gw_ref[2:3, :] += jnp.sum(dci * g_next[inner], axis=0, keepdims=True)
        gb_ref[...] += jnp.sum(dci, axis=0, keepdims=True)
        d_tail[...] = d_old[tm - HALO:tm]
        d_tile[...] = d_new

    def tile(i):
        return jnp.maximum(i - 1, 0)

    cur = pl.BlockSpec((tm, tn), lambda j, i: (tile(i), j))
    prev = pl.BlockSpec((HALO, tn), lambda j, i: (jnp.maximum(tile(i) * per - 1, 0), j))
    nxt = pl.BlockSpec((HALO, tn), lambda j, i: (jnp.minimum((tile(i) + 1) * per, last_halo), j))
    wspec = pl.BlockSpec((3, tn), lambda j, i: (0, j))
    bspec = pl.BlockSpec((1, tn), lambda j, i: (0, j))
    return pl.pallas_call(
        body, name=name, grid=(f // tn, ni + 1),
        in_specs=[pl.BlockSpec((tm, d), lambda j, i: (jnp.minimum(i, ni - 1), 0)),
                  pl.BlockSpec((tn, d), lambda j, i: (j, 0))] + [prev, cur, nxt] * 2 + [wspec, bspec],
        out_specs=[cur, cur, wspec, bspec],
        out_shape=[jax.ShapeDtypeStruct((s, f), BF16), jax.ShapeDtypeStruct((s, f), BF16),
                   jax.ShapeDtypeStruct((3, f), F32), jax.ShapeDtypeStruct((1, f), F32)],
        scratch_shapes=[pltpu.VMEM((tm, tn), F32), pltpu.VMEM((HALO, tn), F32)],
        compiler_params=_params(("parallel", "arbitrary")),
    )(dh2, w_down, gate, gate, gate, up, up, up, conv_w, conv_b)


def _local_step(x, target, w, late_weights=None, grad_sink=None, first_dep=()):
    s = x.shape[0]
    tables = _rope_tables(s)
    uf, ub = _gate_matrices(w["gf_up"], w["gb_up"])
    if grad_sink is None:
        grad_sink = lambda names, grads: ()

    n1 = _rms_fwd(x, w["norm1_g"], "norm1")
    proj = _matmul([(n1, w["w_in"])], "nn", F32, 1024, 1280, D_MODEL, "in_proj", deps=first_dep)
    qkv = _rope_fwd(proj, tables)
    branches = [_attn_fwd(*qkv[di], d, f"attn_fwd_d{d}") for di, d in enumerate(DILATIONS)]
    o_mix, ao, lse = _attn_combine([b[0] for b in branches], [b[1] for b in branches], w["attn_norm_g"])
    g_f, g_b = _gla_gates(proj, uf, ub, w["gf_b"], w["gb_b"])
    o_f, st_f = _gla_fwd(proj, g_f, False, "gla_fwd_f")
    o_b, st_b = _gla_fwd(proj, g_b, True, "gla_fwd_b")
    cat = _gla_post(o_f, o_b, proj, w["gla_norm_g"], ao)
    if late_weights is not None:
        w = {**w, **late_weights(cat)}
    h1 = _matmul([(cat, w["w_out"])], "nn", F32, 512, 1024, D_MODEL, "out_proj", res=x)
    n2 = _rms_fwd(h1, w["norm2_g"], "norm2")
    gate, up, act = _ffn_in(n2, w["w_gate"], w["w_up"], w["conv_w"], w["conv_b"])
    h2 = _matmul([(act, w["w_down"])], "nn", F32, 1024, 1024, 2816, "ffn_down", res=h1)
    dh2, dh2_b, loss_acc, g_final = _final_loss(h2, target, w["final_norm_g"])

    g_w_down = _matmul([(act, dh2_b)], "tn", F32, 1408, 1024, 2048, "g_w_down")
    dep = grad_sink(["w_down"], [g_w_down])
    dgate, dup, g_conv_w, g_conv_b = _ffn_mid_bwd(dh2_b, w["w_down"], gate, up, w["conv_w"], w["conv_b"])
    g_w_gate = _matmul([(n2, dgate)], "tn", F32, 2048, 512, 2048, "g_w_gate", deps=dep)
    g_w_up = _matmul([(n2, dup)], "tn", F32, 2048, 512, 2048, "g_w_up")
    dep = grad_sink(["w_gate", "w_up"], [g_w_gate, g_w_up])
    dn2 = _matmul([(dgate, w["w_gate"])], "nt", F32, 1024, 1024, 2816, "d_n2_gate", deps=dep)
    dn2 = _matmul([(dup, w["w_up"])], "nt", F32, 1024, 1024, 2816, "d_n2_up", res=dn2)
    dh1, dh1_b, g_norm2 = _rms_bwd(dn2, h1, w["norm2_g"], dh2, "norm2_bwd")

    g_w_out = _matmul([(cat, dh1_b)], "tn", F32, 1024, 1024, 2048, "g_w_out")
    dep = grad_sink(["w_out"], [g_w_out])
    dcat = _matmul([(dh1_b, w["w_out"])], "nt", F32, 512, 1024, D_MODEL, "d_cat", deps=dep)
    do_attn, delta, g_attn_norm = _attn_prebwd(dcat, o_mix, w["attn_norm_g"])
    grads = [_attn_bwd(*qkv[di], do_attn[di], lse[di], delta[di], d, f"attn_bwd_d{d}")
             for di, d in enumerate(DILATIONS)]
    dproj = _rope_bwd(grads, tables)
    do_gla, dgr, g_gla_norm = _gla_post_bwd(dcat, o_f, o_b, proj, w["gla_norm_g"])
    dq_f, dk_f, dv_f, dg_f = _gla_bwd(proj, g_f, do_gla, st_f, False, "gla_bwd_f")
    dproj, dg_b = _gla_bwd(proj, g_b, do_gla, st_b, True, "gla_bwd_b", merge=(dq_f, dk_f, dv_f, dgr, dproj))
    dproj, g_uf, g_ub, g_gf_b, g_gb_b = _gla_gates_bwd(dg_f, dg_b, proj, uf, ub, w["gf_b"], w["gb_b"], dproj)
    g_w_in = _matmul([(n1, dproj)], "tn", F32, 1024, 1280, 2048, "g_w_in")
    dep = grad_sink(["w_in"], [g_w_in])
    dn1 = _matmul([(dproj, w["w_in"])], "nt", F32, 1024, 2048, 1280, "d_n1", deps=dep)
    grad_x, _, g_norm1 = _rms_bwd(dn1, x, w["norm1_g"], dh1, "norm1_bwd")

    g = dict(norm1_g=g_norm1, w_in=g_w_in, gf_up=g_uf[:GLA_RANK], gf_b=g_gf_b,
             gb_up=g_ub[GLA_RANK:2 * GLA_RANK], gb_b=g_gb_b, gla_norm_g=g_gla_norm, attn_norm_g=g_attn_norm,
             w_out=g_w_out, norm2_g=g_norm2, w_gate=g_w_gate, w_up=g_w_up, conv_w=g_conv_w, conv_b=g_conv_b,
             w_down=g_w_down, final_norm_g=g_final)
    return loss_acc, grad_x, g


def _me_and_peers():
    x, y, c = lax.axis_index("x"), lax.axis_index("y"), lax.axis_index("c")
    me = 4 * x + 2 * y + c
    peers = []
    for kbits in range(1, N_DEV):
        px, py, pc = x ^ (kbits >> 2 & 1), y ^ (kbits >> 1 & 1), c ^ (kbits & 1)
        peers.append(((px, py, pc), 4 * px + 2 * py + pc))
    return me, peers


_HBM = pl.BlockSpec(memory_space=pltpu.HBM)
_SEM = pl.BlockSpec(memory_space=pltpu.SEMAPHORE)
_ANY = pl.BlockSpec(memory_space=pl.ANY)
_EFFECT = pltpu.SideEffectType.DATAFLOW_SIDE_EFFECTING


def _exchange_copies(src_refs, land_refs, send_sems, recv_sems, scatter):
    me, peers = _me_and_peers()
    out = []
    for a, (src, land) in enumerate(zip(src_refs, land_refs)):
        for kk, (dev, idx) in enumerate(peers):
            out.append(pltpu.make_async_remote_copy(
                src_ref=src.at[idx] if scatter else src, dst_ref=land.at[me],
                send_sem=send_sems.at[a * (N_DEV - 1) + kk], recv_sem=recv_sems.at[a * (N_DEV - 1) + kk],
                device_id=dev, device_id_type=MESH_ID))
    return out


def _exchange_start(srcs, lands, scatter, name, deps=()):
    n, nd = len(srcs), len(deps)

    def body(*refs):
        src_refs, land_refs = refs[:n], refs[n:2 * n]
        send_sems, recv_sems = refs[2 * n + nd:2 * n + nd + 2]
        token = refs[-1]
        for cp in _exchange_copies(src_refs, land_refs, send_sems, recv_sems, scatter):
            cp.start()
        token[...] = jnp.zeros_like(token)

    outs = pl.pallas_call(
        body, name=name,
        in_specs=[_HBM] * (2 * n) + [_ANY] * nd,
        out_specs=[_SEM, _SEM] + [_HBM] * (2 * n) + [pl.BlockSpec(memory_space=pltpu.VMEM)],
        out_shape=[pltpu.SemaphoreType.DMA((n * (N_DEV - 1),)), pltpu.SemaphoreType.DMA((n * (N_DEV - 1),))]
        + [pltpu.HBM(t.shape, t.dtype) for t in srcs] + [pltpu.HBM(t.shape, t.dtype) for t in lands]
        + [jax.ShapeDtypeStruct((SUBLANES, LANES), F32)],
        input_output_aliases={i: 2 + i for i in range(2 * n)},
        compiler_params=pltpu.CompilerParams(has_side_effects=_EFFECT),
    )(*[pltpu.with_memory_space_constraint(t, pltpu.HBM) for t in list(srcs) + list(lands)], *deps)
    send_sems, recv_sems = outs[0], outs[1]
    return dict(send=send_sems, recv=recv_sems, srcs=outs[2:2 + n], lands=outs[2 + n:2 + 2 * n],
                scatter=scatter, token=outs[-1])


def _exchange_wait(started, name, after):
    n = len(started["srcs"])
    scatter = started["scatter"]

    def body(*refs):
        src_refs, land_refs = refs[:n], refs[n:2 * n]
        send_sems, recv_sems = refs[2 * n], refs[2 * n + 1]
        for cp in _exchange_copies(src_refs, land_refs, send_sems, recv_sems, scatter):
            cp.wait_send()
            cp.wait_recv()

    outs = pl.pallas_call(
        body, name=name,
        in_specs=[_HBM] * (2 * n) + [_SEM, _SEM, _ANY],
        out_specs=[_HBM] * (2 * n),
        out_shape=[pltpu.HBM(t.shape, t.dtype) for t in started["srcs"]]
        + [pltpu.HBM(t.shape, t.dtype) for t in started["lands"]],
        input_output_aliases={i: i for i in range(2 * n)},
        compiler_params=pltpu.CompilerParams(has_side_effects=_EFFECT),
    )(*started["srcs"], *started["lands"], started["send"], started["recv"], after)
    return outs[:n], outs[n:]


def _all_gather_two_level(shard, name):
    def body(x_ref, out_ref, send_sems, recv_sems, local_sem):
        x, y, c = lax.axis_index("x"), lax.axis_index("y"), lax.axis_index("c")
        me, sibling = (x, y, c), (x, y, 1 - c)
        chips = [(1 - x, y), (x, 1 - y), (1 - x, 1 - y)]

        def slot(px, py, pc):
            return out_ref.at[4 * px + 2 * py + pc]

        def copy(k, block, to, src=None):
            return pltpu.make_async_remote_copy(
                src_ref=slot(*block) if src is None else src, dst_ref=slot(*block),
                send_sem=send_sems.at[k], recv_sem=recv_sems.at[k], device_id=to, device_id_type=MESH_ID)

        mine = pltpu.make_async_copy(x_ref, slot(*me), local_sem)
        mine.start()
        first = [copy(0, me, sibling, src=x_ref)]
        first += [copy(1 + j, me, (*chip, c), src=x_ref) for j, chip in enumerate(chips)]
        for cp in first:
            cp.start()
        passed = [copy(4 + j, (*chip, c), sibling) for j, chip in enumerate(chips)]
        for j, chip in enumerate(chips):
            copy(1 + j, (*chip, c), me).wait_recv()
            passed[j].start()
        copy(0, sibling, me).wait_recv()
        for j, chip in enumerate(chips):
            copy(4 + j, (*chip, 1 - c), me).wait_recv()
        for cp in first + passed:
            cp.wait_send()
        mine.wait()

    return pl.pallas_call(
        body, name=name,
        in_specs=[_ANY], out_specs=_ANY,
        out_shape=jax.ShapeDtypeStruct((N_DEV,) + shard.shape, shard.dtype),
        scratch_shapes=[pltpu.SemaphoreType.DMA((N_DEV - 1,)), pltpu.SemaphoreType.DMA((N_DEV - 1,)),
                        pltpu.SemaphoreType.DMA],
    )(shard)


def _all_gather_vmem(vec, name):
    r = vec.shape[0]

    def body(v_ref, o_ref, send_sems, recv_sems):
        me, peers = _me_and_peers()
        o_ref[me] = v_ref[...]
        sends = []
        for kk, (dev, _) in enumerate(peers):
            cp = pltpu.make_async_remote_copy(
                src_ref=v_ref, dst_ref=o_ref.at[me],
                send_sem=send_sems.at[kk], recv_sem=recv_sems.at[kk],
                device_id=dev, device_id_type=MESH_ID)
            cp.start()
            sends.append(cp)
        for kk, (dev, idx) in enumerate(peers):
            pltpu.make_async_remote_copy(
                src_ref=v_ref, dst_ref=o_ref.at[idx],
                send_sem=send_sems.at[kk], recv_sem=recv_sems.at[kk],
                device_id=dev, device_id_type=MESH_ID).wait_recv()
        for cp in sends:
            cp.wait_send()

    return pl.pallas_call(
        body, name=name,
        in_specs=[pl.BlockSpec(memory_space=pltpu.VMEM)],
        out_specs=pl.BlockSpec(memory_space=pltpu.VMEM),
        out_shape=jax.ShapeDtypeStruct((N_DEV, r, LANES), F32),
        scratch_shapes=[pltpu.SemaphoreType.DMA((N_DEV - 1,)), pltpu.SemaphoreType.DMA((N_DEV - 1,))],
        compiler_params=pltpu.CompilerParams(vmem_limit_bytes=VMEM_LIMIT),
    )(vec)


def _adamw_math(w, g, m, v):
    m = ADAM_B1 * m + (1.0 - ADAM_B1) * g
    v = ADAM_B2 * v + (1.0 - ADAM_B2) * (g * g)
    m_hat = m / (1.0 - ADAM_B1 ** ADAM_STEP)
    v_hat = v / (1.0 - ADAM_B2 ** ADAM_STEP)
    delta = -ADAM_LR * (m_hat / (jnp.sqrt(v_hat) + ADAM_EPS) + ADAM_WD * w)
    return delta, m, v


def _adamw_sum(parts, w, m, v, tr, name, own=None, me=None):
    r, c = w.shape

    def body(*refs):
        if own is None:
            p_ref, w_ref, m_ref, v_ref, g_ref, d_ref, nm_ref, nv_ref = refs
            terms = [p_ref[kk] for kk in range(N_DEV)]
        else:
            me_ref, p_ref, own_ref, w_ref, m_ref, v_ref, g_ref, d_ref, nm_ref, nv_ref = refs
            terms = [jnp.where(me_ref[0] == kk, own_ref[0], p_ref[kk]).astype(F32) for kk in range(N_DEV)]
        g = terms[0]
        for t in terms[1:]:
            g = g + t
        g_ref[...] = g
        d_ref[...], nm_ref[...], nv_ref[...] = _adamw_math(w_ref[...], g, m_ref[...], v_ref[...])

    out_shape = [jax.ShapeDtypeStruct((r, c), F32)] * 4
    if own is None:
        blk = pl.BlockSpec((tr, c), lambda i: (i, 0))
        return pl.pallas_call(
            body, name=name, grid=(r // tr,),
            in_specs=[pl.BlockSpec((N_DEV, tr, c), lambda i: (0, i, 0)), blk, blk, blk],
            out_specs=[blk] * 4, out_shape=out_shape,
            compiler_params=_params(("parallel",)),
        )(parts, w, m, v)
    blk = pl.BlockSpec((tr, c), lambda i, me_ref: (i, 0))
    return pl.pallas_call(
        body, name=name,
        grid_spec=pltpu.PrefetchScalarGridSpec(
            num_scalar_prefetch=1, grid=(r // tr,),
            in_specs=[pl.BlockSpec((N_DEV, tr, c), lambda i, me_ref: (0, i, 0)),
                      pl.BlockSpec((1, tr, c), lambda i, me_ref: (me_ref[0], i, 0)), blk, blk, blk],
            out_specs=[blk] * 4),
        out_shape=out_shape,
        compiler_params=_params(("parallel",)),
    )(jnp.reshape(me, (1,)).astype(jnp.int32), parts, own, w, m, v)


_SMALL = ("norm1_g", "gf_b", "gb_b", "gla_norm_g", "attn_norm_g", "norm2_g", "conv_b", "final_norm_g",
          "gf_up", "gb_up", "conv_w")


def _pack(named):
    flat = jnp.concatenate([jnp.ravel(t).astype(F32) for t in named])
    tile = SUBLANES * LANES
    total = -(-flat.shape[0] // tile) * tile
    return jnp.pad(flat, (0, total - flat.shape[0])).reshape(total // LANES, LANES)


def _unpack(packed, shapes):
    flat = packed.reshape(-1)
    out, off = [], 0
    for shp in shapes:
        size = int(np.prod(shp))
        out.append(flat[off:off + size].reshape(shp))
        off += size
    return out


def kernel(x, norm1_g, w_in, gf_up, gf_b, gb_up, gb_b, gla_norm_g, attn_norm_g, w_out, norm2_g, w_gate, w_up, conv_w, conv_b, w_down, final_norm_g, loss_target, m_norm1_g, m_w_in, m_gf_up, m_gf_b, m_gb_up, m_gb_b, m_gla_norm_g, m_attn_norm_g, m_w_out, m_norm2_g, m_w_gate, m_w_up, m_conv_w, m_conv_b, m_w_down, m_final_norm_g, v_norm1_g, v_w_in, v_gf_up, v_gf_b, v_gb_up, v_gb_b, v_gla_norm_g, v_attn_norm_g, v_w_out, v_norm2_g, v_w_gate, v_w_up, v_conv_w, v_conv_b, v_w_down, v_final_norm_g):
    names = ("norm1_g", "w_in", "gf_up", "gf_b", "gb_up", "gb_b", "gla_norm_g", "attn_norm_g", "w_out", "norm2_g",
             "w_gate", "w_up", "conv_w", "conv_b", "w_down", "final_norm_g")
    ws = dict(zip(names, (norm1_g, w_in, gf_up, gf_b, gb_up, gb_b, gla_norm_g, attn_norm_g, w_out, norm2_g,
                          w_gate, w_up, conv_w, conv_b, w_down, final_norm_g)))
    ms = dict(zip(names, (m_norm1_g, m_w_in, m_gf_up, m_gf_b, m_gb_up, m_gb_b, m_gla_norm_g, m_attn_norm_g, m_w_out,
                          m_norm2_g, m_w_gate, m_w_up, m_conv_w, m_conv_b, m_w_down, m_final_norm_g)))
    vs = dict(zip(names, (v_norm1_g, v_w_in, v_gf_up, v_gf_b, v_gb_up, v_gb_b, v_gla_norm_g, v_attn_norm_g, v_w_out,
                          v_norm2_g, v_w_gate, v_w_up, v_conv_w, v_conv_b, v_w_down, v_final_norm_g)))
    me = 4 * lax.axis_index("x") + 2 * lax.axis_index("y") + lax.axis_index("c")
    big = ("w_in", "w_out", "w_gate", "w_up", "w_down")
    col_sharded = ("w_in", "w_gate", "w_up")

    def gather_start(group, name, deps=()):
        shards = [ws[n][0].astype(BF16) for n in group]
        lands = [lax.empty((N_DEV,) + t.shape, BF16) for t in shards]
        return _exchange_start(shards, lands, False, name, deps)

    def gather_finish(group, started, name, after):
        full = {}
        for n, own, t in zip(group, *_exchange_wait(started, name, after)):
            t = lax.dynamic_update_slice(t, own[None], (me, 0, 0))
            if n in col_sharded:
                full[n] = jnp.transpose(t, (1, 0, 2)).reshape(t.shape[1], N_DEV * t.shape[2])
            else:
                full[n] = t.reshape(N_DEV * t.shape[1], t.shape[2])
        return full

    w_in_all = _all_gather_two_level(ws["w_in"][0].astype(BF16), "gather_w_in")
    full = {"w_in": jnp.pad(jnp.transpose(w_in_all, (1, 0, 2)).reshape(D_MODEL, IN_WIDTH),
                            ((0, 0), (0, IN_PAD - IN_WIDTH)))}
    late = ("w_out", "w_gate", "w_up", "w_down")
    started_b = gather_start(late, "gather_late_start", deps=(full["w_in"],))

    def late_weights(after):
        return gather_finish(late, started_b, "gather_late_wait", after)

    small_sharded = ("gf_up", "gb_up", "conv_w")
    sm = _all_gather_vmem(_pack([ws[n][0] for n in small_sharded]), "gather_small")
    shard_shapes = [ws[n][0].shape for n in small_sharded]
    per_dev = [_unpack(sm[d], shard_shapes) for d in range(N_DEV)]
    for i, n in enumerate(small_sharded):
        full[n] = jnp.concatenate([per_dev[d][i] for d in range(N_DEV)], axis=1)
    for n in ("norm1_g", "gf_b", "gb_b", "gla_norm_g", "attn_norm_g", "norm2_g", "conv_b"):
        full[n] = ws[n]
    full["final_norm_g"] = final_norm_g.reshape(1, D_MODEL)

    in_flight = []

    def grad_sink(group, grads):
        partials = []
        for n, t in zip(group, grads):
            if n == "w_in":
                t = t[:, :IN_WIDTH].astype(BF16)
            if n in col_sharded:
                t = jnp.transpose(t.reshape(t.shape[0], N_DEV, t.shape[1] // N_DEV), (1, 0, 2))
            else:
                t = t.reshape(N_DEV, t.shape[0] // N_DEV, t.shape[1])
            partials.append(t)
        lands = [lax.empty(t.shape, t.dtype) for t in partials]
        started = _exchange_start(partials, lands, True, "exchange_" + "_".join(group) + "_start")
        in_flight.append((group, started))
        return (started["token"],)

    loss_acc, grad_x, g = _local_step(x[0], loss_target[0], full, late_weights, grad_sink,
                                      first_dep=(started_b["token"],))

    out = {}
    for group, started in in_flight:
        sent, landed = _exchange_wait(started, "exchange_" + "_".join(group) + "_wait", grad_x)
        for n, parts, own in zip(group, landed, sent):
            out[n] = _adamw_sum(parts, ws[n][0], ms[n][0], vs[n][0], 64, "adamw_" + n, own=own, me=me)

    small_full_shapes = [g[n].shape for n in _SMALL]
    gsmall = _pack([g[n] for n in _SMALL] + [loss_acc[0:1, 0:1]])
    gathered_small = _all_gather_vmem(gsmall, "gather_small_grads")

    def full_small(d):
        parts = []
        for n in _SMALL:
            t = d[n].reshape(d[n].shape[-2:]) if d[n].ndim == 3 else d[n].reshape(1, -1)
            if n in small_sharded:
                wide = jnp.zeros((t.shape[0], t.shape[1] * N_DEV), F32)
                t = lax.dynamic_update_slice_in_dim(wide, t, me * t.shape[1], axis=1)
            parts.append(t)
        return _pack(parts + [jnp.zeros((1, 1), F32)])

    rows = gsmall.shape[0]
    res_small = _adamw_sum(gathered_small, full_small(ws), full_small(ms), full_small(vs), rows, "adamw_small")
    loss = res_small[0].reshape(-1)[sum(int(np.prod(sh)) for sh in small_full_shapes)]
    unpacked = [_unpack(t, small_full_shapes) for t in res_small]
    for i, n in enumerate(_SMALL):
        vals = [u[i] for u in unpacked]
        if n in small_sharded:
            width = vals[0].shape[1] // N_DEV
            vals = [lax.dynamic_slice_in_dim(t, me * width, width, axis=1) for t in vals]
        out[n] = vals

    result = [loss, grad_x[None]]
    for kind in range(4):
        for n in names:
            result.append(out[n][kind].reshape(ws[n].shape))
    return tuple(result)
```

```python
import functools

import numpy as np
import jax
import jax.numpy as jnp
from jax import lax
from jax.experimental import pallas as pl
from jax.experimental.pallas import tpu as pltpu

F32 = jnp.float32
BF16 = jnp.bfloat16

D_MODEL = 2048
ATTN_W = 1024
ATTN_HEADS = 8
HEAD_DIM = 128
ROPE_DIM = 32
ROPE_THETA = 500000.0
DILATIONS = (1, 4, 16)
N_SIDE = 64
GLA_KW = 512
GLA_VW = 1024
GLA_HEADS = 4
GLA_DK = 128
GLA_DV = 256
GLA_RANK = 16
GLA_GATE_NORM = 16.0
GLA_CHUNK = 64
IN_WIDTH = 6176
IN_PAD = 6400
D_FF = 5632
EPS = 1e-6
N_DEV = 8

OFF_AQ, OFF_AK, OFF_AV = 0, 1024, 2048
OFF_GQ, OFF_GK, OFF_GV, OFF_GR, OFF_Z = 3072, 3584, 4096, 5120, 6144

ADAM_LR, ADAM_B1, ADAM_B2, ADAM_EPS, ADAM_WD, ADAM_STEP = 0.001, 0.9, 0.999, 1e-08, 0.01, 10

LANES = 128
SUBLANES = 8
VMEM_LIMIT = 56 * 1024 * 1024
ROW_BLOCK = 256
ATTN_BLOCK = 128
GLA_CHUNKS_PER_STEP = 4
NEG = -1e30
MESH_ID = pl.DeviceIdType.MESH


def _params(sem):
    return pltpu.CompilerParams(dimension_semantics=sem, vmem_limit_bytes=VMEM_LIMIT)


def _dot(a, b):
    return lax.dot_general(a, b, (((1,), (0,)), ((), ())), preferred_element_type=F32)


def _dot_nt(a, b):
    return lax.dot_general(a, b, (((1,), (1,)), ((), ())), preferred_element_type=F32)


def _dot_tn(a, b):
    return lax.dot_general(a, b, (((0,), (0,)), ((), ())), preferred_element_type=F32)


def _sigmoid(x):
    return 0.5 * jnp.tanh(0.5 * x) + 0.5


def _matmul(pairs, mode, out_dtype, tm, tn, tk, name, res=None, deps=()):
    a0, b0 = pairs[0]
    if mode == "nn":
        (m, kdim), n = a0.shape, b0.shape[1]
    elif mode == "nt":
        (m, kdim), n = a0.shape, b0.shape[0]
    else:
        (kdim, m), n = a0.shape, b0.shape[1]
    assert m % tm == 0 and n % tn == 0 and kdim % tk == 0, (name, m, n, kdim)
    nk = kdim // tk
    npairs = len(pairs)
    steps = nk * npairs
    dot = {"nn": _dot, "nt": _dot_nt, "tn": _dot_tn}[mode]

    def kidx(p):
        return lambda k: jnp.clip(k - p * nk, 0, nk - 1)

    in_specs, args = [], []
    for p, (a, b) in enumerate(pairs):
        kk = kidx(p)
        if mode == "nn":
            in_specs += [pl.BlockSpec((tm, tk), lambda i, j, k, kk=kk: (i, kk(k))),
                         pl.BlockSpec((tk, tn), lambda i, j, k, kk=kk: (kk(k), j))]
        elif mode == "nt":
            in_specs += [pl.BlockSpec((tm, tk), lambda i, j, k, kk=kk: (i, kk(k))),
                         pl.BlockSpec((tn, tk), lambda i, j, k, kk=kk: (j, kk(k)))]
        else:
            in_specs += [pl.BlockSpec((tk, tm), lambda i, j, k, kk=kk: (kk(k), i)),
                         pl.BlockSpec((tk, tn), lambda i, j, k, kk=kk: (kk(k), j))]
        args += [a, b]
    if res is not None:
        in_specs.append(pl.BlockSpec((tm, tn), lambda i, j, k: (i, j)))
        args.append(res)
    in_specs += [pl.BlockSpec(memory_space=pl.ANY)] * len(deps)
    args += list(deps)

    def body(*refs):
        ab = refs[:2 * npairs]
        res_ref = refs[2 * npairs] if res is not None else None
        o_ref = refs[2 * npairs + (1 if res is not None else 0) + len(deps)]

        def finish(acc):
            if res_ref is not None:
                acc = acc + res_ref[...]
            o_ref[...] = acc.astype(out_dtype)

        if steps == 1:
            finish(dot(ab[0][...], ab[1][...]))
            return
        acc_ref = refs[-1]
        k = pl.program_id(2)

        @pl.when(k == 0)
        def _():
            acc_ref[...] = jnp.zeros_like(acc_ref)

        for p in range(npairs):
            @pl.when((k >= p * nk) & (k < (p + 1) * nk))
            def _(p=p):
                acc_ref[...] += dot(ab[2 * p][...], ab[2 * p + 1][...])

        @pl.when(k == steps - 1)
        def _():
            finish(acc_ref[...])

    return pl.pallas_call(
        body, name=name,
        grid=(m // tm, n // tn, steps),
        in_specs=in_specs,
        out_specs=pl.BlockSpec((tm, tn), lambda i, j, k: (i, j)),
        out_shape=jax.ShapeDtypeStruct((m, n), out_dtype),
        scratch_shapes=[] if steps == 1 else [pltpu.VMEM((tm, tn), F32)],
        compiler_params=_params(("parallel", "parallel", "arbitrary")),
    )(*args)


def _rms_fwd(x, g, name):
    s, d = x.shape

    def body(x_ref, g_ref, o_ref):
        xv = x_ref[...]
        r = lax.rsqrt(jnp.mean(xv * xv, axis=-1, keepdims=True) + EPS)
        o_ref[...] = (xv * r * g_ref[...]).astype(BF16)

    return pl.pallas_call(
        body, name=name, grid=(s // ROW_BLOCK,),
        in_specs=[pl.BlockSpec((ROW_BLOCK, d), lambda i: (i, 0)), pl.BlockSpec((1, d), lambda i: (0, 0))],
        out_specs=pl.BlockSpec((ROW_BLOCK, d), lambda i: (i, 0)),
        out_shape=jax.ShapeDtypeStruct((s, d), BF16),
        compiler_params=_params(("parallel",)),
    )(x, g)


def _rms_bwd(dn, x, g, dres, name):
    s, d = x.shape

    def body(dn_ref, x_ref, g_ref, dres_ref, dx_ref, dxb_ref, gg_ref):
        i = pl.program_id(0)
        xv, dnv = x_ref[...], dn_ref[...]
        r = lax.rsqrt(jnp.mean(xv * xv, axis=-1, keepdims=True) + EPS)
        dng = dnv * g_ref[...]
        c = jnp.mean(dng * xv, axis=-1, keepdims=True)
        dx = dres_ref[...] + r * dng - xv * (r * r * r * c)
        dx_ref[...] = dx
        dxb_ref[...] = dx.astype(BF16)

        @pl.when(i == 0)
        def _():
            gg_ref[...] = jnp.zeros_like(gg_ref)

        gg_ref[...] += jnp.sum(dnv * xv * r, axis=0, keepdims=True)

    row = pl.BlockSpec((ROW_BLOCK, d), lambda i: (i, 0))
    vec = pl.BlockSpec((1, d), lambda i: (0, 0))
    return pl.pallas_call(
        body, name=name, grid=(s // ROW_BLOCK,),
        in_specs=[row, row, vec, row],
        out_specs=[row, row, vec],
        out_shape=[jax.ShapeDtypeStruct((s, d), F32), jax.ShapeDtypeStruct((s, d), BF16),
                   jax.ShapeDtypeStruct((1, d), F32)],
        compiler_params=_params(("arbitrary",)),
    )(dn, x, g, dres)


def _final_loss(h2, target, g, name="final_loss"):
    s, d = h2.shape

    def body(h_ref, t_ref, g_ref, dh_ref, dhb_ref, loss_ref, gg_ref):
        i = pl.program_id(0)
        hv, gv = h_ref[...], g_ref[...]
        r = lax.rsqrt(jnp.mean(hv * hv, axis=-1, keepdims=True) + EPS)
        e = hv * r * gv - t_ref[...]
        dy = e * (1.0 / d)
        dyg = dy * gv
        c = jnp.mean(dyg * hv, axis=-1, keepdims=True)
        dh = r * dyg - hv * (r * r * r * c)
        dh_ref[...] = dh
        dhb_ref[...] = dh.astype(BF16)

        @pl.when(i == 0)
        def _():
            gg_ref[...] = jnp.zeros_like(gg_ref)
            loss_ref[...] = jnp.zeros_like(loss_ref)

        gg_ref[...] += jnp.sum(dy * hv * r, axis=0, keepdims=True)
        loss_ref[...] += jnp.sum(jnp.sum(e * e, axis=-1, keepdims=True), axis=0, keepdims=True) * (0.5 / d)

    row = pl.BlockSpec((ROW_BLOCK, d), lambda i: (i, 0))
    vec = pl.BlockSpec((1, d), lambda i: (0, 0))
    return pl.pallas_call(
        body, name=name, grid=(s // ROW_BLOCK,),
        in_specs=[row, row, vec],
        out_specs=[row, row, pl.BlockSpec((SUBLANES, LANES), lambda i: (0, 0)), vec],
        out_shape=[jax.ShapeDtypeStruct((s, d), F32), jax.ShapeDtypeStruct((s, d), BF16),
                   jax.ShapeDtypeStruct((SUBLANES, LANES), F32), jax.ShapeDtypeStruct((1, d), F32)],
        compiler_params=_params(("arbitrary",)),
    )(h2, target, g)


def _rope_tables(s):
    pos = jnp.arange(s, dtype=F32)
    inv_freq = ROPE_THETA ** (-jnp.arange(0, ROPE_DIM, 2, dtype=F32) / ROPE_DIM)
    ang = pos[:, None] * inv_freq[None, :]
    cos, sin = jnp.cos(ang), jnp.sin(ang)
    half = ROPE_DIM // 2
    rest = HEAD_DIM - ROPE_DIM
    c = jnp.concatenate([cos, cos, jnp.ones((s, rest), F32)], axis=1)
    sm = jnp.concatenate([-sin, jnp.zeros((s, half + rest), F32)], axis=1)
    sp = jnp.concatenate([jnp.zeros((s, half), F32), sin, jnp.zeros((s, rest), F32)], axis=1)
    return c, sm, sp


def _res_shape(s, groups, dil, dtype):
    return jax.ShapeDtypeStruct((s // dil, dil * groups * LANES), dtype)


def _res_spec(groups, dil):
    return pl.BlockSpec((ROW_BLOCK // dil, dil * groups * LANES), lambda i: (i, 0))


def _to_residues(scr, o_ref, dil):
    groups, rows = scr.shape[0], ROW_BLOCK // dil
    for r in range(dil):
        for h in range(groups):
            piece = scr[h] if dil == 1 else scr.at[h][pl.ds(r, rows, stride=dil), :]
            o_ref[:, (r * groups + h) * LANES:(r * groups + h + 1) * LANES] = piece.astype(o_ref.dtype)


def _from_residues(i_ref, scr, dil):
    groups, rows = scr.shape[0], ROW_BLOCK // dil
    for r in range(dil):
        for h in range(groups):
            piece = i_ref[:, (r * groups + h) * LANES:(r * groups + h + 1) * LANES].astype(F32)
            if dil == 1:
                scr[h] = piece
            else:
                scr.at[h][pl.ds(r, rows, stride=dil), :] = piece


def _rope_fwd(proj, tables, name="rope_fwd"):
    s = proj.shape[0]
    half = ROPE_DIM // 2
    nd = len(DILATIONS)

    def body(p_ref, c_ref, sm_ref, sp_ref, *rest):
        outs, scr = rest[:3 * nd], rest[3 * nd]
        c, sm, sp = c_ref[...], sm_ref[...], sp_ref[...]
        for gi, off in enumerate((OFF_AQ, OFF_AK, OFF_AV)):
            for h in range(ATTN_HEADS):
                t = p_ref[:, off + h * HEAD_DIM: off + (h + 1) * HEAD_DIM]
                if off != OFF_AV:
                    t = t * c + pltpu.roll(t, HEAD_DIM - half, 1) * sm + pltpu.roll(t, half, 1) * sp
                scr[h] = t
            for di, dil in enumerate(DILATIONS):
                _to_residues(scr, outs[3 * di + gi], dil)

    tab = pl.BlockSpec((ROW_BLOCK, HEAD_DIM), lambda i: (i, 0))
    outs = pl.pallas_call(
        body, name=name, grid=(s // ROW_BLOCK,),
        in_specs=[pl.BlockSpec((ROW_BLOCK, 3 * ATTN_W), lambda i: (i, 0)), tab, tab, tab],
        out_specs=[_res_spec(ATTN_HEADS, d) for d in DILATIONS for _ in range(3)],
        out_shape=[_res_shape(s, ATTN_HEADS, d, BF16) for d in DILATIONS for _ in range(3)],
        scratch_shapes=[pltpu.VMEM((ATTN_HEADS, ROW_BLOCK, LANES), F32)],
        compiler_params=_params(("parallel",)),
    )(proj, *tables)
    return [tuple(outs[3 * di:3 * di + 3]) for di in range(nd)]


def _rope_bwd(grads, tables, name="rope_bwd"):
    s = grads[0][0].shape[0] * DILATIONS[0]
    half = ROPE_DIM // 2
    nd = len(DILATIONS)

    def body(*refs):
        ins = refs[:3 * nd]
        c_ref, sm_ref, sp_ref, o_ref = refs[3 * nd:3 * nd + 4]
        scrs = refs[3 * nd + 4:]
        c, sm, sp = c_ref[...], sm_ref[...], sp_ref[...]
        for gi, off in enumerate((OFF_AQ, OFF_AK, OFF_AV)):
            for di, dil in enumerate(DILATIONS):
                _from_residues(ins[3 * di + gi], scrs[di], dil)
            for h in range(ATTN_HEADS):
                t = scrs[0][h]
                for scr in scrs[1:]:
                    t = t + scr[h]
                if off != OFF_AV:
                    t = t * c + pltpu.roll(t * sm, half, 1) + pltpu.roll(t * sp, HEAD_DIM - half, 1)
                o_ref[:, off + h * HEAD_DIM: off + (h + 1) * HEAD_DIM] = t.astype(BF16)

    tab = pl.BlockSpec((ROW_BLOCK, HEAD_DIM), lambda i: (i, 0))
    return pl.pallas_call(
        body, name=name, grid=(s // ROW_BLOCK,),
        in_specs=[_res_spec(ATTN_HEADS, d) for d in DILATIONS for _ in range(3)] + [tab, tab, tab],
        out_specs=pl.BlockSpec((ROW_BLOCK, 3 * ATTN_W), lambda i: (i, 0)),
        out_shape=jax.ShapeDtypeStruct((s, IN_PAD), BF16),
        scratch_shapes=[pltpu.VMEM((ATTN_HEADS, ROW_BLOCK, LANES), F32) for _ in DILATIONS],
        compiler_params=_params(("parallel",)),
    )(*[t for g in grads for t in g], *tables)


def _window_specs(nb, width):
    qb, hb = ATTN_BLOCK, N_SIDE
    cur = pl.BlockSpec((qb, width), lambda r, j: (j, r))
    prev = pl.BlockSpec((hb, width), lambda r, j: (jnp.maximum(2 * j - 1, 0), r))
    nxt = pl.BlockSpec((hb, width), lambda r, j: (jnp.minimum(2 * j + 2, 2 * nb - 1), r))
    return prev, cur, nxt


def _band_masks(j, length):
    qb, hb = ATTN_BLOCK, N_SIDE
    row = lax.broadcasted_iota(jnp.int32, (qb, qb), 0)
    col = lax.broadcasted_iota(jnp.int32, (qb, qb), 1)

    def edge_pos(i):
        return j * qb - hb + i + jnp.where(i >= hb, qb, 0)

    def ok(a, b, outside):
        return (jnp.abs(a - b) <= N_SIDE) & (outside >= 0) & (outside < length)

    cur = jnp.abs(row - col) <= N_SIDE
    edge_k = ok(j * qb + row, edge_pos(col), edge_pos(col))
    edge_q = ok(edge_pos(row), j * qb + col, edge_pos(row))
    return cur, edge_k, edge_q


def _edge(prev_ref, next_ref, sl):
    return jnp.concatenate([prev_ref[:, sl], next_ref[:, sl]], axis=0)


def _attn_fwd(q, k, v, dil, name):
    length = q.shape[0]
    qb = ATTN_BLOCK
    nb = length // qb
    scale = HEAD_DIM ** -0.5

    def body(q_ref, kp_ref, kc_ref, kn_ref, vp_ref, vc_ref, vn_ref, o_ref, lse_ref):
        valid_c, valid_e, _ = _band_masks(pl.program_id(1), length)
        lane = lax.broadcasted_iota(jnp.int32, (qb, LANES), 1)
        lse_acc = jnp.zeros((qb, LANES), F32)
        heads = [slice(h * HEAD_DIM, (h + 1) * HEAD_DIM) for h in range(ATTN_HEADS)]
        scores = [(_dot_nt(q_ref[:, sl], kc_ref[:, sl]), _dot_nt(q_ref[:, sl], _edge(kp_ref, kn_ref, sl)))
                  for sl in heads]
        probs = []
        for h, (s_c, s_e) in enumerate(scores):
            s_c = jnp.where(valid_c, s_c * scale, NEG)
            s_e = jnp.where(valid_e, s_e * scale, NEG)
            m = jnp.max(jnp.maximum(s_c, s_e), axis=-1, keepdims=True)
            p_c, p_e = jnp.exp(s_c - m), jnp.exp(s_e - m)
            den = jnp.sum(p_c + p_e, axis=-1, keepdims=True)
            probs.append((p_c.astype(BF16), p_e.astype(BF16), 1.0 / den))
            lse_acc = jnp.where(lane == h, m + jnp.log(den), lse_acc)
        for sl, (p_c, p_e, inv) in zip(heads, probs):
            o_ref[:, sl] = (_dot(p_c, vc_ref[:, sl]) + _dot(p_e, _edge(vp_ref, vn_ref, sl))) * inv
        lse_ref[...] = lse_acc

    prev, cur, nxt = _window_specs(nb, ATTN_W)
    return pl.pallas_call(
        body, name=name, grid=(dil, nb),
        in_specs=[cur, prev, cur, nxt, prev, cur, nxt],
        out_specs=[cur, pl.BlockSpec((qb, LANES), lambda r, j: (j, r))],
        out_shape=[jax.ShapeDtypeStruct((length, dil * ATTN_W), F32),
                   jax.ShapeDtypeStruct((length, dil * LANES), F32)],
        compiler_params=_params(("parallel", "parallel")),
    )(q, k, k, k, v, v, v)


def _attn_combine(outs, lses, g, name="attn_combine"):
    s = outs[0].shape[0] * DILATIONS[0]
    nd = len(DILATIONS)

    def body(*refs):
        o_refs, l_refs = refs[:nd], refs[nd:2 * nd]
        g_ref, o_ref, n_ref = refs[2 * nd:2 * nd + 3]
        lse_outs = refs[2 * nd + 3:3 * nd + 3]
        o_scr, l_scr = refs[3 * nd + 3:4 * nd + 3], refs[4 * nd + 3:5 * nd + 3]
        for di, dil in enumerate(DILATIONS):
            _from_residues(o_refs[di], o_scr[di], dil)
            _from_residues(l_refs[di], l_scr[di], dil)
        ls = [scr[0] for scr in l_scr]
        m = ls[0]
        for l in ls[1:]:
            m = jnp.maximum(m, l)
        es = [jnp.exp(l - m) for l in ls]
        z = es[0]
        for e in es[1:]:
            z = z + e
        ws = [e / z for e in es]
        l_scr[0][0] = m + jnp.log(z)
        for di, dil in enumerate(DILATIONS):
            _to_residues(l_scr[0], lse_outs[di], dil)
        ssq = jnp.zeros((ROW_BLOCK, 1), F32)
        for h in range(ATTN_HEADS):
            sl = slice(h * HEAD_DIM, (h + 1) * HEAD_DIM)
            acc = ws[0][:, h:h + 1] * o_scr[0][h]
            for w, scr in zip(ws[1:], o_scr[1:]):
                acc = acc + w[:, h:h + 1] * scr[h]
            o_ref[:, sl] = acc
            ssq = ssq + jnp.sum(acc * acc, axis=-1, keepdims=True)
        r = lax.rsqrt(ssq * (1.0 / ATTN_W) + EPS)
        n_ref[...] = (o_ref[...] * r * g_ref[...]).astype(BF16)

    blk = pl.BlockSpec((ROW_BLOCK, ATTN_W), lambda i: (i, 0))
    outs_ = pl.pallas_call(
        body, name=name, grid=(s // ROW_BLOCK,),
        in_specs=[_res_spec(ATTN_HEADS, d) for d in DILATIONS] + [_res_spec(1, d) for d in DILATIONS]
        + [pl.BlockSpec((1, ATTN_W), lambda i: (0, 0))],
        out_specs=[blk, blk] + [_res_spec(1, d) for d in DILATIONS],
        out_shape=[jax.ShapeDtypeStruct((s, ATTN_W), F32), jax.ShapeDtypeStruct((s, D_MODEL), BF16)]
        + [_res_shape(s, 1, d, F32) for d in DILATIONS],
        scratch_shapes=[pltpu.VMEM((ATTN_HEADS, ROW_BLOCK, LANES), F32) for _ in DILATIONS]
        + [pltpu.VMEM((1, ROW_BLOCK, LANES), F32) for _ in DILATIONS],
        compiler_params=_params(("parallel",)),
    )(*outs, *lses, g)
    return outs_[0], outs_[1], list(outs_[2:])


def _attn_prebwd(dcat, o, g, name="attn_prebwd"):
    s = o.shape[0]
    nd = len(DILATIONS)

    def body(dy_ref, o_ref, g_ref, *rest):
        do_outs, delta_outs, gg_ref = rest[:nd], rest[nd:2 * nd], rest[2 * nd]
        do_scr, delta_scr = rest[2 * nd + 1], rest[2 * nd + 2]
        i = pl.program_id(0)
        dy, ov = dy_ref[...], o_ref[...]
        r = lax.rsqrt(jnp.mean(ov * ov, axis=-1, keepdims=True) + EPS)
        dyg = dy * g_ref[...]
        c = jnp.mean(dyg * ov, axis=-1, keepdims=True)
        do = r * dyg - ov * (r * r * r * c)
        prod = do * ov
        lane = lax.broadcasted_iota(jnp.int32, (ROW_BLOCK, LANES), 1)
        acc = jnp.zeros((ROW_BLOCK, LANES), F32)
        for h in range(ATTN_HEADS):
            sl = slice(h * HEAD_DIM, (h + 1) * HEAD_DIM)
            do_scr[h] = do[:, sl]
            acc = jnp.where(lane == h, jnp.sum(prod[:, sl], axis=-1, keepdims=True), acc)
        delta_scr[0] = acc
        for di, dil in enumerate(DILATIONS):
            _to_residues(do_scr, do_outs[di], dil)
            _to_residues(delta_scr, delta_outs[di], dil)

        @pl.when(i == 0)
        def _():
            gg_ref[...] = jnp.zeros_like(gg_ref)

        gg_ref[...] += jnp.sum(dy * ov * r, axis=0, keepdims=True)

    blk = pl.BlockSpec((ROW_BLOCK, ATTN_W), lambda i: (i, 0))
    vec = pl.BlockSpec((1, ATTN_W), lambda i: (0, 0))
    outs = pl.pallas_call(
        body, name=name, grid=(s // ROW_BLOCK,),
        in_specs=[blk, blk, vec],
        out_specs=[_res_spec(ATTN_HEADS, d) for d in DILATIONS] + [_res_spec(1, d) for d in DILATIONS] + [vec],
        out_shape=[_res_shape(s, ATTN_HEADS, d, BF16) for d in DILATIONS]
        + [_res_shape(s, 1, d, F32) for d in DILATIONS] + [jax.ShapeDtypeStruct((1, ATTN_W), F32)],
        scratch_shapes=[pltpu.VMEM((ATTN_HEADS, ROW_BLOCK, LANES), F32), pltpu.VMEM((1, ROW_BLOCK, LANES), F32)],
        compiler_params=_params(("arbitrary",)),
    )(dcat, o, g)
    return list(outs[:nd]), list(outs[nd:2 * nd]), outs[2 * nd]


def _attn_bwd(q, k, v, do, lse, delta, dil, name):
    length = q.shape[0]
    qb = ATTN_BLOCK
    nb = length // qb
    scale = HEAD_DIM ** -0.5

    def body(qp, qc, qn, kp, kc, kn, vp, vc, vn, dop, doc, don, lp, lc, ln, dp, dc, dn, dq_ref, dk_ref, dv_ref):
        valid_c, valid_ek, valid_eq = _band_masks(pl.program_id(1), length)
        everything = slice(None)
        lse_e, del_e = _edge(lp, ln, everything), _edge(dp, dn, everything)
        heads = [slice(h * HEAD_DIM, (h + 1) * HEAD_DIM) for h in range(ATTN_HEADS)]
        prods = []
        for sl in heads:
            q_c, k_c, v_c, do_c = qc[:, sl], kc[:, sl], vc[:, sl], doc[:, sl]
            q_e, k_e, v_e, do_e = _edge(qp, qn, sl), _edge(kp, kn, sl), _edge(vp, vn, sl), _edge(dop, don, sl)
            prods.append((_dot_nt(q_c, k_c), _dot_nt(do_c, v_c), _dot_nt(q_c, k_e), _dot_nt(do_c, v_e),
                          _dot_nt(q_e, k_c), _dot_nt(do_e, v_c)))
        parts = []
        for h, (s_cc, dp_cc, s_ek, dp_ek, s_eq, dp_eq) in enumerate(prods):
            hc = slice(h, h + 1)
            lse_c, del_c = lc[:, hc], dc[:, hc]
            p_cc = jnp.where(valid_c, jnp.exp(s_cc * scale - lse_c), 0.0)
            ds_cc = (p_cc * (dp_cc - del_c)).astype(BF16)
            p_ek = jnp.where(valid_ek, jnp.exp(s_ek * scale - lse_c), 0.0)
            ds_ek = (p_ek * (dp_ek - del_c)).astype(BF16)
            p_eq = jnp.where(valid_eq, jnp.exp(s_eq * scale - lse_e[:, hc]), 0.0)
            ds_eq = (p_eq * (dp_eq - del_e[:, hc])).astype(BF16)
            parts.append((p_cc.astype(BF16), ds_cc, ds_ek, p_eq.astype(BF16), ds_eq))
        for sl, (p_cc, ds_cc, ds_ek, p_eq, ds_eq) in zip(heads, parts):
            q_c, k_c, do_c = qc[:, sl], kc[:, sl], doc[:, sl]
            q_e, k_e, do_e = _edge(qp, qn, sl), _edge(kp, kn, sl), _edge(dop, don, sl)
            dq_ref[:, sl] = ((_dot(ds_cc, k_c) + _dot(ds_ek, k_e)) * scale).astype(BF16)
            dk_ref[:, sl] = ((_dot_tn(ds_cc, q_c) + _dot_tn(ds_eq, q_e)) * scale).astype(BF16)
            dv_ref[:, sl] = (_dot_tn(p_cc, do_c) + _dot_tn(p_eq, do_e)).astype(BF16)

    wide, narrow = list(_window_specs(nb, ATTN_W)), list(_window_specs(nb, LANES))
    return tuple(pl.pallas_call(
        body, name=name, grid=(dil, nb),
        in_specs=wide * 4 + narrow * 2,
        out_specs=[wide[1]] * 3,
        out_shape=[jax.ShapeDtypeStruct((length, dil * ATTN_W), BF16)] * 3,
        compiler_params=_params(("parallel", "parallel")),
    )(q, q, q, k, k, k, v, v, v, do, do, do, lse, lse, lse, delta, delta, delta))


def _gate_matrices(gf_up, gb_up):
    pad = LANES - 2 * GLA_RANK
    uf = jnp.concatenate([gf_up, jnp.zeros((GLA_RANK + pad, GLA_KW), gf_up.dtype)], axis=0)
    ub = jnp.concatenate([jnp.zeros((GLA_RANK, GLA_KW), gb_up.dtype), gb_up, jnp.zeros((pad, GLA_KW), gb_up.dtype)], axis=0)
    return uf.astype(BF16), ub.astype(BF16)


def _log_sigmoid(x):
    return jnp.minimum(x, 0.0) - jnp.log(1.0 + jnp.exp(-jnp.abs(x)))


def _gla_gates(proj, uf, ub, gf_b, gb_b, name="gla_gates"):
    s = proj.shape[0]

    def body(z_ref, uf_ref, ub_ref, bf_ref, bb_ref, gf_ref, gb_ref):
        z = z_ref[...].astype(BF16)
        gf_ref[...] = _log_sigmoid(_dot(z, uf_ref[...]) + bf_ref[...]) * (1.0 / GLA_GATE_NORM)
        gb_ref[...] = _log_sigmoid(_dot(z, ub_ref[...]) + bb_ref[...]) * (1.0 / GLA_GATE_NORM)

    mat = pl.BlockSpec((LANES, GLA_KW), lambda i: (0, 0))
    vec = pl.BlockSpec((1, GLA_KW), lambda i: (0, 0))
    out = pl.BlockSpec((ROW_BLOCK, GLA_KW), lambda i: (i, 0))
    return pl.pallas_call(
        body, name=name, grid=(s // ROW_BLOCK,),
        in_specs=[pl.BlockSpec((ROW_BLOCK, LANES), lambda i: (i, OFF_Z // LANES)), mat, mat, vec, vec],
        out_specs=[out, out],
        out_shape=[jax.ShapeDtypeStruct((s, GLA_KW), F32)] * 2,
        compiler_params=_params(("parallel",)),
    )(proj, uf, ub, gf_b, gb_b)


def _gla_gates_bwd(dgf, dgb, proj, uf, ub, gf_b, gb_b, dproj, name="gla_gates_bwd"):
    s = proj.shape[0]
    tail = IN_PAD - OFF_Z

    def body(dgf_ref, dgb_ref, z_ref, uf_ref, ub_ref, bf_ref, bb_ref, _, dz_ref, guf_ref, gub_ref, gbf_ref, gbb_ref):
        i = pl.program_id(0)
        z = z_ref[...].astype(BF16)
        uf_, ub_ = uf_ref[...], ub_ref[...]
        dpf = dgf_ref[...] * (1.0 / GLA_GATE_NORM) * _sigmoid(-(_dot(z, uf_) + bf_ref[...]))
        dpb = dgb_ref[...] * (1.0 / GLA_GATE_NORM) * _sigmoid(-(_dot(z, ub_) + bb_ref[...]))
        dpf_b, dpb_b = dpf.astype(BF16), dpb.astype(BF16)
        dz_ref[:, 0:LANES] = (_dot_nt(dpf_b, uf_) + _dot_nt(dpb_b, ub_)).astype(BF16)
        dz_ref[:, LANES:tail] = jnp.zeros((ROW_BLOCK, tail - LANES), BF16)

        @pl.when(i == 0)
        def _():
            for r in (guf_ref, gub_ref, gbf_ref, gbb_ref):
                r[...] = jnp.zeros_like(r)

        guf_ref[...] += _dot_tn(z, dpf_b)
        gub_ref[...] += _dot_tn(z, dpb_b)
        gbf_ref[...] += jnp.sum(dpf, axis=0, keepdims=True)
        gbb_ref[...] += jnp.sum(dpb, axis=0, keepdims=True)

    mat = pl.BlockSpec((LANES, GLA_KW), lambda i: (0, 0))
    vec = pl.BlockSpec((1, GLA_KW), lambda i: (0, 0))
    blk = pl.BlockSpec((ROW_BLOCK, GLA_KW), lambda i: (i, 0))
    return pl.pallas_call(
        body, name=name, grid=(s // ROW_BLOCK,),
        in_specs=[blk, blk, pl.BlockSpec((ROW_BLOCK, LANES), lambda i: (i, OFF_Z // LANES)), mat, mat, vec, vec,
                  pl.BlockSpec(memory_space=pl.ANY)],
        out_specs=[pl.BlockSpec((ROW_BLOCK, tail), lambda i: (i, OFF_Z // tail)), mat, mat, vec, vec],
        out_shape=[jax.ShapeDtypeStruct(dproj.shape, dproj.dtype), jax.ShapeDtypeStruct((LANES, GLA_KW), F32),
                   jax.ShapeDtypeStruct((LANES, GLA_KW), F32), jax.ShapeDtypeStruct((1, GLA_KW), F32),
                   jax.ShapeDtypeStruct((1, GLA_KW), F32)],
        input_output_aliases={7: 0},
        compiler_params=_params(("arbitrary",)),
    )(dgf, dgb, proj, uf, ub, gf_b, gb_b, dproj)


def _split3(x):
    x1 = x.astype(BF16)
    r1 = x - x1.astype(F32)
    x2 = r1.astype(BF16)
    x3 = (r1 - x2.astype(F32)).astype(BF16)
    return x1, x2, x3


def _dot_exact(mask_bf, x):
    x1, x2, x3 = _split3(x)
    return _dot(mask_bf, x1) + _dot(mask_bf, x2) + _dot(mask_bf, x3)


def _chunk_masks(reverse):
    c = GLA_CHUNK
    row = lax.broadcasted_iota(jnp.int32, (c, c), 0)
    col = lax.broadcasted_iota(jnp.int32, (c, c), 1)
    allowed = (col >= row) if reverse else (col <= row)
    seen_by = (col <= row) if reverse else (col >= row)
    return allowed, seen_by


def _chunk_terms(q_ref, k_ref, g_ref, rs, hs, allowed, reverse):
    c = GLA_CHUNK
    mid, last = (c // 2, 0) if reverse else (c // 2 - 1, c - 1)
    q = q_ref[rs, hs] * (GLA_DK ** -0.5)
    k = k_ref[rs, hs]
    b = _dot_exact(jnp.where(allowed, 1.0, 0.0).astype(BF16), g_ref[rs, hs])
    bref, blast = b[mid:mid + 1, :], b[last:last + 1, :]
    e_q, e_k, e_in, e_st = jnp.exp(b - bref), jnp.exp(bref - b), jnp.exp(b), jnp.exp(blast - b)
    return dict(last=last, e_q=e_q, e_k=e_k, e_in=e_in, e_st=e_st,
                dec=jnp.exp(blast), qe=q * e_q, ke=k * e_k, qin=q * e_in, kst=k * e_st)


def _gla_blockspecs(s, reverse_order):
    cb = GLA_CHUNKS_PER_STEP
    rows = cb * GLA_CHUNK
    nsteps = s // rows

    def rb(n):
        return (nsteps - 1 - n) if reverse_order else n

    qspec = pl.BlockSpec((rows, GLA_KW), lambda n: (rb(n), OFF_GQ // GLA_KW))
    kspec = pl.BlockSpec((rows, GLA_KW), lambda n: (rb(n), OFF_GK // GLA_KW))
    vspec = pl.BlockSpec((rows, GLA_VW), lambda n: (rb(n), OFF_GV // GLA_VW))
    gspec = pl.BlockSpec((rows, GLA_KW), lambda n: (rb(n), 0))
    ospec = pl.BlockSpec((rows, GLA_VW), lambda n: (rb(n), 0))
    sspec = pl.BlockSpec((GLA_HEADS, cb, GLA_DV, GLA_DK), lambda n: (0, rb(n), 0, 0))
    return cb, rows, nsteps, qspec, kspec, vspec, gspec, ospec, sspec


def _gla_units(cb, order_reversed):
    chunks = list(reversed(range(cb))) if order_reversed else list(range(cb))
    return [(c, h, slice(c * GLA_CHUNK, (c + 1) * GLA_CHUNK), slice(h * GLA_DK, (h + 1) * GLA_DK),
             slice(h * GLA_DV, (h + 1) * GLA_DV)) for c in chunks for h in range(GLA_HEADS)]


def _gla_fwd(proj, g, reverse, name):
    s = proj.shape[0]
    cb, rows, nsteps, qspec, kspec, vspec, gspec, ospec, sspec = _gla_blockspecs(s, reverse)

    def body(q_ref, k_ref, v_ref, g_ref, o_ref, st_ref, state):
        @pl.when(pl.program_id(0) == 0)
        def _():
            state[...] = jnp.zeros_like(state)

        allowed, _ = _chunk_masks(reverse)
        units = _gla_units(cb, reverse)
        terms = [_chunk_terms(q_ref, k_ref, g_ref, rs, hs, allowed, reverse) for _, _, rs, hs, _ in units]
        vals = [v_ref[rs, vs].astype(BF16) for _, _, rs, _, vs in units]
        raw = [(_dot_nt(t["qe"].astype(BF16), t["ke"].astype(BF16)), _dot_tn(v, t["kst"].astype(BF16)))
               for t, v in zip(terms, vals)]
        intra = [_dot(jnp.where(allowed, a, 0.0).astype(BF16), v) for (a, _), v in zip(raw, vals)]
        st = [state[h] for h in range(GLA_HEADS)]
        for (c, h, rs, _, vs), t, (_, kv), o_in in zip(units, terms, raw, intra):
            st_ref[h, c] = st[h]
            o_ref[rs, vs] = o_in + _dot_nt(t["qin"].astype(BF16), st[h].astype(BF16))
            st[h] = st[h] * t["dec"] + kv
        for h in range(GLA_HEADS):
            state[h] = st[h]

    return pl.pallas_call(
        body, name=name, grid=(nsteps,),
        in_specs=[qspec, kspec, vspec, gspec],
        out_specs=[ospec, sspec],
        out_shape=[jax.ShapeDtypeStruct((s, GLA_VW), F32),
                   jax.ShapeDtypeStruct((GLA_HEADS, s // GLA_CHUNK, GLA_DV, GLA_DK), F32)],
        scratch_shapes=[pltpu.VMEM((GLA_HEADS, GLA_DV, GLA_DK), F32)],
        compiler_params=_params(("arbitrary",)),
    )(proj, proj, proj, g)


def _gla_bwd(proj, g, do, states, reverse, name, merge=None):
    s = proj.shape[0]
    cb, rows, nsteps, qspec, kspec, vspec, gspec, ospec, sspec = _gla_blockspecs(s, not reverse)
    gla_cols = OFF_Z - OFF_GQ

    def body(q_ref, k_ref, v_ref, g_ref, do_ref, sp_ref, *rest):
        if merge is None:
            dq_ref, dk_ref, dv_ref, dg_ref, dstate = rest
        else:
            dq_o, dk_o, dv_o, dgr_ref, _, dp_ref, dg_ref, dstate = rest
        @pl.when(pl.program_id(0) == 0)
        def _():
            dstate[...] = jnp.zeros_like(dstate)

        allowed, seen_by = _chunk_masks(reverse)
        units = _gla_units(cb, not reverse)
        terms = [_chunk_terms(q_ref, k_ref, g_ref, rs, hs, allowed, reverse) for _, _, rs, hs, _ in units]
        vals = [v_ref[rs, vs].astype(BF16) for _, _, rs, _, vs in units]
        dos = [do_ref[rs, vs] for _, _, rs, _, vs in units]
        prevs = [sp_ref[h, c] for c, h, _, _, _ in units]
        raw = [(_dot_nt(t["qe"].astype(BF16), t["ke"].astype(BF16)), _dot_nt(do, v),
                _dot(do, sp.astype(BF16)), _dot_tn(do, t["qin"].astype(BF16)))
               for t, v, do, sp in zip(terms, vals, dos, prevs)]
        inner = []
        for t, do, (a, da, _, _) in zip(terms, dos, raw):
            da = jnp.where(allowed, da, 0.0).astype(BF16)
            inner.append((_dot(da, t["ke"].astype(BF16)), _dot_tn(da, t["qe"].astype(BF16)),
                          _dot_tn(jnp.where(allowed, a, 0.0).astype(BF16), do)))
        ds = [dstate[h] for h in range(GLA_HEADS)]
        outer = []
        for (c, h, _, _, _), t, v, sp, (_, _, _, inc) in zip(units, terms, vals, prevs, raw):
            ds_b = ds[h].astype(BF16)
            outer.append((_dot(v, ds_b), _dot_nt(t["kst"].astype(BF16), ds_b),
                          jnp.sum(sp * ds[h], axis=0, keepdims=True)))
            ds[h] = ds[h] * t["dec"] + inc
        for h in range(GLA_HEADS):
            dstate[h] = ds[h]
        seen_bf = jnp.where(seen_by, 1.0, 0.0).astype(BF16)
        rowi = lax.broadcasted_iota(jnp.int32, (GLA_CHUNK, GLA_DK), 0)
        for (c, h, rs, hs, vs), t, (_, _, dqin, _), (dqe, dke, dv_in), (dkst, dv_out, ddec) in zip(
                units, terms, raw, inner, outer):
            dq = (dqe * t["e_q"] + dqin * t["e_in"]) * (GLA_DK ** -0.5)
            dk = dke * t["e_k"] + dkst * t["e_st"]
            if merge is None:
                dq_ref[rs, hs], dk_ref[rs, hs], dv_ref[rs, vs] = dq, dk, dv_in + dv_out
            else:
                lo = OFF_GK - OFF_GQ + h * GLA_DK
                dp_ref[rs, hs] = (dq + dq_o[rs, hs]).astype(BF16)
                dp_ref[rs, lo:lo + GLA_DK] = (dk + dk_o[rs, hs]).astype(BF16)
                lo = OFF_GV - OFF_GQ + h * GLA_DV
                dp_ref[rs, lo:lo + GLA_DV] = (dv_in + dv_out + dv_o[rs, vs]).astype(BF16)
            kk = dkst * t["kst"]
            db = dqe * t["qe"] - dke * t["ke"] + dqin * t["qin"] - kk
            extra = jnp.sum(kk, axis=0, keepdims=True) + ddec * t["dec"]
            db = db + jnp.where(rowi == t["last"], extra, 0.0)
            dg_ref[rs, hs] = _dot_exact(seen_bf, db)
        if merge is not None:
            dp_ref[:, OFF_GR - OFF_GQ:gla_cols] = dgr_ref[...]

    scratch = [pltpu.VMEM((GLA_HEADS, GLA_DV, GLA_DK), F32)]
    if merge is None:
        return pl.pallas_call(
            body, name=name, grid=(nsteps,),
            in_specs=[qspec, kspec, vspec, gspec, ospec, sspec],
            out_specs=[gspec, gspec, ospec, gspec],
            out_shape=[jax.ShapeDtypeStruct((s, GLA_KW), F32), jax.ShapeDtypeStruct((s, GLA_KW), F32),
                       jax.ShapeDtypeStruct((s, GLA_VW), F32), jax.ShapeDtypeStruct((s, GLA_KW), F32)],
            scratch_shapes=scratch,
            compiler_params=_params(("arbitrary",)),
        )(proj, proj, proj, g, do, states)
    dproj = merge[4]
    block = gspec.index_map
    return pl.pallas_call(
        body, name=name, grid=(nsteps,),
        in_specs=[qspec, kspec, vspec, gspec, ospec, sspec, gspec, gspec, ospec, ospec, _ANY],
        out_specs=[pl.BlockSpec((rows, gla_cols), lambda n: (block(n)[0], OFF_GQ // gla_cols)), gspec],
        out_shape=[jax.ShapeDtypeStruct(dproj.shape, dproj.dtype), jax.ShapeDtypeStruct((s, GLA_KW), F32)],
        input_output_aliases={10: 0},
        scratch_shapes=scratch,
        compiler_params=_params(("arbitrary",)),
    )(proj, proj, proj, g, do, states, *merge)


def _gla_post(o_f, o_b, proj, g, cat, name="gla_post"):
    s = o_f.shape[0]

    def body(of_ref, ob_ref, gr_ref, g_ref, _, o_ref):
        gv = g_ref[...]
        for h in range(GLA_HEADS):
            sl = slice(h * GLA_DV, (h + 1) * GLA_DV)
            osum = of_ref[:, sl] + ob_ref[:, sl]
            r = lax.rsqrt(jnp.mean(osum * osum, axis=-1, keepdims=True) + EPS)
            gr = gr_ref[:, sl]
            o_ref[:, sl] = (osum * r * gv * (gr * _sigmoid(gr))).astype(BF16)

    blk = pl.BlockSpec((ROW_BLOCK, GLA_VW), lambda i: (i, 0))
    return pl.pallas_call(
        body, name=name, grid=(s // ROW_BLOCK,),
        in_specs=[blk, blk, pl.BlockSpec((ROW_BLOCK, GLA_VW), lambda i: (i, OFF_GR // GLA_VW)),
                  pl.BlockSpec((1, GLA_DV), lambda i: (0, 0)), pl.BlockSpec(memory_space=pl.ANY)],
        out_specs=pl.BlockSpec((ROW_BLOCK, GLA_VW), lambda i: (i, ATTN_W // GLA_VW)),
        out_shape=jax.ShapeDtypeStruct(cat.shape, cat.dtype),
        input_output_aliases={4: 0},
        compiler_params=_params(("parallel",)),
    )(o_f, o_b, proj, g, cat)


def _gla_post_bwd(dcat, o_f, o_b, proj, g, name="gla_post_bwd"):
    s = o_f.shape[0]

    def body(dy_ref, of_ref, ob_ref, gr_ref, g_ref, do_ref, dgr_ref, gg_ref):
        i = pl.program_id(0)
        gv = g_ref[...]
        gg = jnp.zeros((1, GLA_DV), F32)
        for h in range(GLA_HEADS):
            sl = slice(h * GLA_DV, (h + 1) * GLA_DV)
            osum = of_ref[:, sl] + ob_ref[:, sl]
            r = lax.rsqrt(jnp.mean(osum * osum, axis=-1, keepdims=True) + EPS)
            gr, dy = gr_ref[:, sl], dy_ref[:, sl]
            sg = _sigmoid(gr)
            dgr_ref[:, sl] = (dy * (osum * r * gv) * (sg * (1.0 + gr * (1.0 - sg)))).astype(BF16)
            dn = dy * (gr * sg)
            dng = dn * gv
            c = jnp.mean(dng * osum, axis=-1, keepdims=True)
            do_ref[:, sl] = (r * dng - osum * (r * r * r * c)).astype(BF16)
            gg = gg + jnp.sum(dn * osum * r, axis=0, keepdims=True)

        @pl.when(i == 0)
        def _():
            gg_ref[...] = jnp.zeros_like(gg_ref)

        gg_ref[...] += gg

    blk = pl.BlockSpec((ROW_BLOCK, GLA_VW), lambda i: (i, 0))
    vec = pl.BlockSpec((1, GLA_DV), lambda i: (0, 0))
    return pl.pallas_call(
        body, name=name, grid=(s // ROW_BLOCK,),
        in_specs=[pl.BlockSpec((ROW_BLOCK, GLA_VW), lambda i: (i, 1)), blk, blk,
                  pl.BlockSpec((ROW_BLOCK, GLA_VW), lambda i: (i, OFF_GR // GLA_VW)), vec],
        out_specs=[blk, blk, vec],
        out_shape=[jax.ShapeDtypeStruct((s, GLA_VW), BF16), jax.ShapeDtypeStruct((s, GLA_VW), BF16),
                   jax.ShapeDtypeStruct((1, GLA_DV), F32)],
        compiler_params=_params(("arbitrary",)),
    )(dcat, o_f, o_b, proj, g)


HALO = 16


def _extended(prev_ref, cur_ref, next_ref, i, s, tr):
    first, last = i == 0, i == s // tr - 1
    prev = jnp.where(first, 0.0, prev_ref[...].astype(F32))
    nxt = jnp.where(last, 0.0, next_ref[...].astype(F32))
    return jnp.concatenate([prev, cur_ref[...].astype(F32), nxt], axis=0)


FFN_ROWS = 512
FFN_COLS = 512


def _lagged(i, ni, multiply, finish, rotate, init):
    @pl.when(i == 0)
    def _():
        init()

    @pl.when(i < 2)
    def _():
        rotate(multiply())

    @pl.when((i >= 2) & (i < ni))
    def _():
        new = multiply()
        finish()
        rotate(new)

    @pl.when(i >= ni)
    def _():
        finish()
        rotate(None)


def _ffn_in(n2, w_gate, w_up, conv_w, conv_b, name="ffn_in"):
    s, d = n2.shape
    f = w_gate.shape[1]
    tm, tn, edge = FFN_ROWS, FFN_COLS, SUBLANES
    ni = s // tm
    ext = tm + 2 * edge

    def body(a_ref, wg_ref, wu_ref, w_ref, b_ref, gate_ref, up_ref, act_ref, g_near, g_far, u_near, u_far, g_tail):
        i = pl.program_id(1)

        def multiply():
            a = a_ref[...]
            return _dot(a, wg_ref[...]), _dot(a, wu_ref[...]).astype(BF16)

        def finish():
            g_old, u_old = g_far[...], u_far[...]
            before = jnp.where(i == 2, 0.0, g_tail[...])
            after = jnp.where(i == ni + 1, 0.0, g_near[0:edge])
            ge = jnp.concatenate([before, g_old, after], axis=0)
            w = w_ref[...]
            conv = (w[0:1] * pltpu.roll(ge, 1, 0) + w[1:2] * ge + w[2:3] * pltpu.roll(ge, ext - 1, 0))[edge:edge + tm]
            conv = conv + b_ref[...]
            gate_ref[...] = g_old
            up_ref[...] = u_old
            act_ref[...] = (conv * _sigmoid(conv) * u_old.astype(F32)).astype(BF16)

        def rotate(new):
            g_tail[...] = g_far[tm - edge:tm]
            g_far[...] = g_near[...]
            u_far[...] = u_near[...]
            if new is not None:
                g_near[...], u_near[...] = new

        def init():
            for r in (g_near, g_far, u_near, u_far, g_tail):
                r[...] = jnp.zeros_like(r)

        _lagged(i, ni, multiply, finish, rotate, init)

    lag = pl.BlockSpec((tm, tn), lambda j, i: (jnp.maximum(i - 2, 0), j))
    return pl.pallas_call(
        body, name=name, grid=(f // tn, ni + 2),
        in_specs=[pl.BlockSpec((tm, d), lambda j, i: (jnp.minimum(i, ni - 1), 0)),
                  pl.BlockSpec((d, tn), lambda j, i: (0, j)), pl.BlockSpec((d, tn), lambda j, i: (0, j)),
                  pl.BlockSpec((3, tn), lambda j, i: (0, j)), pl.BlockSpec((1, tn), lambda j, i: (0, j))],
        out_specs=[lag, lag, lag],
        out_shape=[jax.ShapeDtypeStruct((s, f), F32), jax.ShapeDtypeStruct((s, f), BF16),
                   jax.ShapeDtypeStruct((s, f), BF16)],
        scratch_shapes=[pltpu.VMEM((tm, tn), F32), pltpu.VMEM((tm, tn), F32), pltpu.VMEM((tm, tn), BF16),
                        pltpu.VMEM((tm, tn), BF16), pltpu.VMEM((edge, tn), F32)],
        compiler_params=_params(("parallel", "arbitrary")),
    )(n2, w_gate, w_up, conv_w, conv_b)


def _ffn_mid_bwd(dh2, w_down, gate, up, conv_w, conv_b, name="ffn_mid_bwd"):
    s, d = dh2.shape
    f = gate.shape[1]
    tm, tn = FFN_ROWS, FFN_COLS
    ni = s // tm
    ext = tm + 2 * HALO
    per, last_halo = tm // HALO, s // HALO - 1

    def body(a_ref, wd_ref, gp, gc, gn, upp, upc, upn, w_ref, b_ref, dg_ref, du_ref, gw_ref, gb_ref,
             d_near, d_far, d_tail):
        i = pl.program_id(1)

        def multiply():
            return _dot_nt(a_ref[...], wd_ref[...])

        def finish():
            before = jnp.where(i == 2, 0.0, d_tail[...])
            after = jnp.where(i == ni + 1, 0.0, d_near[0:HALO])
            de = jnp.concatenate([before, d_far[...], after], axis=0)
            ge = _extended(gp, gc, gn, i - 2, s, tm)
            ue = _extended(upp, upc, upn, i - 2, s, tm)
            w = w_ref[...]
            g_prev, g_next = pltpu.roll(ge, 1, 0), pltpu.roll(ge, ext - 1, 0)
            conv = w[0:1] * g_prev + w[1:2] * ge + w[2:3] * g_next + b_ref[...]
            sg = _sigmoid(conv)
            inner = slice(HALO, HALO + tm)
            du_ref[...] = (de * (conv * sg))[inner].astype(BF16)
            dconv = de * ue * (sg * (1.0 + conv * (1.0 - sg)))
            dgate = w[0:1] * pltpu.roll(dconv, ext - 1, 0) + w[1:2] * dconv + w[2:3] * pltpu.roll(dconv, 1, 0)
            dg_ref[...] = dgate[inner].astype(BF16)
            dci = dconv[inner]
            gw_ref[0:1, :] += jnp.sum(dci * g_prev[inner], axis=0, keepdims=True)
            gw_ref[1:2, :] += jnp.sum(dci * ge[inner], axis=0, keepdims=True)
            gw_ref[2:3, :] += jnp.sum(dci * g_next[inner], axis=0, keepdims=True)
            gb_ref[...] += jnp.sum(dci, axis=0, keepdims=True)

        def rotate(new):
            d_tail[...] = d_far[tm - HALO:tm]
            d_far[...] = d_near[...]
            if new is not None:
                d_near[...] = new

        def init():
            for r in (d_near, d_far, d_tail, gw_ref, gb_ref):
                r[...] = jnp.zeros_like(r)

        _lagged(i, ni, multiply, finish, rotate, init)

    def tile(i):
        return jnp.maximum(i - 2, 0)

    cur = pl.BlockSpec((tm, tn), lambda j, i: (tile(i), j))
    prev = pl.BlockSpec((HALO, tn), lambda j, i: (jnp.maximum(tile(i) * per - 1, 0), j))
    nxt = pl.BlockSpec((HALO, tn), lambda j, i: (jnp.minimum((tile(i) + 1) * per, last_halo), j))
    wspec = pl.BlockSpec((3, tn), lambda j, i: (0, j))
    bspec = pl.BlockSpec((1, tn), lambda j, i: (0, j))
    return pl.pallas_call(
        body, name=name, grid=(f // tn, ni + 2),
        in_specs=[pl.BlockSpec((tm, d), lambda j, i: (jnp.minimum(i, ni - 1), 0)),
                  pl.BlockSpec((tn, d), lambda j, i: (j, 0))] + [prev, cur, nxt] * 2 + [wspec, bspec],
        out_specs=[cur, cur, wspec, bspec],
        out_shape=[jax.ShapeDtypeStruct((s, f), BF16), jax.ShapeDtypeStruct((s, f), BF16),
                   jax.ShapeDtypeStruct((3, f), F32), jax.ShapeDtypeStruct((1, f), F32)],
        scratch_shapes=[pltpu.VMEM((tm, tn), F32), pltpu.VMEM((tm, tn), F32), pltpu.VMEM((HALO, tn), F32)],
        compiler_params=_params(("parallel", "arbitrary")),
    )(dh2, w_down, gate, gate, gate, up, up, up, conv_w, conv_b)


def _local_step(x, target, w, late_weights=None, grad_sink=None, first_dep=()):
    s = x.shape[0]
    tables = _rope_tables(s)
    uf, ub = _gate_matrices(w["gf_up"], w["gb_up"])
    if grad_sink is None:
        grad_sink = lambda names, grads: ()

    n1 = _rms_fwd(x, w["norm1_g"], "norm1")
    proj = _matmul([(n1, w["w_in"])], "nn", F32, 1024, 1280, D_MODEL, "in_proj", deps=first_dep)
    qkv = _rope_fwd(proj, tables)
    branches = [_attn_fwd(*qkv[di], d, f"attn_fwd_d{d}") for di, d in enumerate(DILATIONS)]
    o_mix, ao, lse = _attn_combine([b[0] for b in branches], [b[1] for b in branches], w["attn_norm_g"])
    g_f, g_b = _gla_gates(proj, uf, ub, w["gf_b"], w["gb_b"])
    o_f, st_f = _gla_fwd(proj, g_f, False, "gla_fwd_f")
    o_b, st_b = _gla_fwd(proj, g_b, True, "gla_fwd_b")
    cat = _gla_post(o_f, o_b, proj, w["gla_norm_g"], ao)
    if late_weights is not None:
        w = {**w, **late_weights(cat)}
    h1 = _matmul([(cat, w["w_out"])], "nn", F32, 512, 1024, D_MODEL, "out_proj", res=x)
    n2 = _rms_fwd(h1, w["norm2_g"], "norm2")
    gate, up, act = _ffn_in(n2, w["w_gate"], w["w_up"], w["conv_w"], w["conv_b"])
    h2 = _matmul([(act, w["w_down"])], "nn", F32, 1024, 1024, 2816, "ffn_down", res=h1)
    dh2, dh2_b, loss_acc, g_final = _final_loss(h2, target, w["final_norm_g"])

    g_w_down = _matmul([(act, dh2_b)], "tn", F32, 1408, 1024, 2048, "g_w_down")
    dep = grad_sink(["w_down"], [g_w_down])
    dgate, dup, g_conv_w, g_conv_b = _ffn_mid_bwd(dh2_b, w["w_down"], gate, up, w["conv_w"], w["conv_b"])
    g_w_gate = _matmul([(n2, dgate)], "tn", F32, 2048, 512, 2048, "g_w_gate", deps=dep)
    g_w_up = _matmul([(n2, dup)], "tn", F32, 2048, 512, 2048, "g_w_up")
    dep = grad_sink(["w_gate", "w_up"], [g_w_gate, g_w_up])
    dn2 = _matmul([(dgate, w["w_gate"])], "nt", F32, 1024, 1024, 2816, "d_n2_gate", deps=dep)
    dn2 = _matmul([(dup, w["w_up"])], "nt", F32, 1024, 1024, 2816, "d_n2_up", res=dn2)
    dh1, dh1_b, g_norm2 = _rms_bwd(dn2, h1, w["norm2_g"], dh2, "norm2_bwd")

    g_w_out = _matmul([(cat, dh1_b)], "tn", F32, 1024, 1024, 2048, "g_w_out")
    dep = grad_sink(["w_out"], [g_w_out])
    dcat = _matmul([(dh1_b, w["w_out"])], "nt", F32, 512, 1024, D_MODEL, "d_cat", deps=dep)
    do_attn, delta, g_attn_norm = _attn_prebwd(dcat, o_mix, w["attn_norm_g"])
    grads = [_attn_bwd(*qkv[di], do_attn[di], lse[di], delta[di], d, f"attn_bwd_d{d}")
             for di, d in enumerate(DILATIONS)]
    dproj = _rope_bwd(grads, tables)
    do_gla, dgr, g_gla_norm = _gla_post_bwd(dcat, o_f, o_b, proj, w["gla_norm_g"])
    dq_f, dk_f, dv_f, dg_f = _gla_bwd(proj, g_f, do_gla, st_f, False, "gla_bwd_f")
    dproj, dg_b = _gla_bwd(proj, g_b, do_gla, st_b, True, "gla_bwd_b", merge=(dq_f, dk_f, dv_f, dgr, dproj))
    dproj, g_uf, g_ub, g_gf_b, g_gb_b = _gla_gates_bwd(dg_f, dg_b, proj, uf, ub, w["gf_b"], w["gb_b"], dproj)
    g_w_in = _matmul([(n1, dproj)], "tn", F32, 1024, 1280, 2048, "g_w_in")
    dep = grad_sink(["w_in"], [g_w_in])
    dn1 = _matmul([(dproj, w["w_in"])], "nt", F32, 1024, 2048, 1280, "d_n1", deps=dep)
    grad_x, _, g_norm1 = _rms_bwd(dn1, x, w["norm1_g"], dh1, "norm1_bwd")

    g = dict(norm1_g=g_norm1, w_in=g_w_in, gf_up=g_uf[:GLA_RANK], gf_b=g_gf_b,
             gb_up=g_ub[GLA_RANK:2 * GLA_RANK], gb_b=g_gb_b, gla_norm_g=g_gla_norm, attn_norm_g=g_attn_norm,
             w_out=g_w_out, norm2_g=g_norm2, w_gate=g_w_gate, w_up=g_w_up, conv_w=g_conv_w, conv_b=g_conv_b,
             w_down=g_w_down, final_norm_g=g_final)
    return loss_acc, grad_x, g


def _me_and_peers():
    x, y, c = lax.axis_index("x"), lax.axis_index("y"), lax.axis_index("c")
    me = 4 * x + 2 * y + c
    peers = []
    for kbits in range(1, N_DEV):
        px, py, pc = x ^ (kbits >> 2 & 1), y ^ (kbits >> 1 & 1), c ^ (kbits & 1)
        peers.append(((px, py, pc), 4 * px + 2 * py + pc))
    return me, peers


_HBM = pl.BlockSpec(memory_space=pltpu.HBM)
_SEM = pl.BlockSpec(memory_space=pltpu.SEMAPHORE)
_ANY = pl.BlockSpec(memory_space=pl.ANY)
_EFFECT = pltpu.SideEffectType.DATAFLOW_SIDE_EFFECTING


def _exchange_copies(src_refs, land_refs, send_sems, recv_sems, scatter):
    me, peers = _me_and_peers()
    out = []
    for a, (src, land) in enumerate(zip(src_refs, land_refs)):
        for kk, (dev, idx) in enumerate(peers):
            out.append(pltpu.make_async_remote_copy(
                src_ref=src.at[idx] if scatter else src, dst_ref=land.at[me],
                send_sem=send_sems.at[a * (N_DEV - 1) + kk], recv_sem=recv_sems.at[a * (N_DEV - 1) + kk],
                device_id=dev, device_id_type=MESH_ID))
    return out


def _exchange_start(srcs, lands, scatter, name, deps=()):
    n, nd = len(srcs), len(deps)

    def body(*refs):
        src_refs, land_refs = refs[:n], refs[n:2 * n]
        send_sems, recv_sems = refs[2 * n + nd:2 * n + nd + 2]
        token = refs[-1]
        for cp in _exchange_copies(src_refs, land_refs, send_sems, recv_sems, scatter):
            cp.start()
        token[...] = jnp.zeros_like(token)

    outs = pl.pallas_call(
        body, name=name,
        in_specs=[_HBM] * (2 * n) + [_ANY] * nd,
        out_specs=[_SEM, _SEM] + [_HBM] * (2 * n) + [pl.BlockSpec(memory_space=pltpu.VMEM)],
        out_shape=[pltpu.SemaphoreType.DMA((n * (N_DEV - 1),)), pltpu.SemaphoreType.DMA((n * (N_DEV - 1),))]
        + [pltpu.HBM(t.shape, t.dtype) for t in srcs] + [pltpu.HBM(t.shape, t.dtype) for t in lands]
        + [jax.ShapeDtypeStruct((SUBLANES, LANES), F32)],
        input_output_aliases={i: 2 + i for i in range(2 * n)},
        compiler_params=pltpu.CompilerParams(has_side_effects=_EFFECT),
    )(*[pltpu.with_memory_space_constraint(t, pltpu.HBM) for t in list(srcs) + list(lands)], *deps)
    send_sems, recv_sems = outs[0], outs[1]
    return dict(send=send_sems, recv=recv_sems, srcs=outs[2:2 + n], lands=outs[2 + n:2 + 2 * n],
                scatter=scatter, token=outs[-1])


def _exchange_wait(started, name, after):
    n = len(started["srcs"])
    scatter = started["scatter"]

    def body(*refs):
        src_refs, land_refs = refs[:n], refs[n:2 * n]
        send_sems, recv_sems = refs[2 * n], refs[2 * n + 1]
        for cp in _exchange_copies(src_refs, land_refs, send_sems, recv_sems, scatter):
            cp.wait_send()
            cp.wait_recv()

    outs = pl.pallas_call(
        body, name=name,
        in_specs=[_HBM] * (2 * n) + [_SEM, _SEM, _ANY],
        out_specs=[_HBM] * (2 * n),
        out_shape=[pltpu.HBM(t.shape, t.dtype) for t in started["srcs"]]
        + [pltpu.HBM(t.shape, t.dtype) for t in started["lands"]],
        input_output_aliases={i: i for i in range(2 * n)},
        compiler_params=pltpu.CompilerParams(has_side_effects=_EFFECT),
    )(*started["srcs"], *started["lands"], started["send"], started["recv"], after)
    return outs[:n], outs[n:]


def _all_gather_two_level(shard, name):
    def body(x_ref, out_ref, send_sems, recv_sems, local_sem):
        x, y, c = lax.axis_index("x"), lax.axis_index("y"), lax.axis_index("c")
        me, sibling = (x, y, c), (x, y, 1 - c)
        chips = [(1 - x, y), (x, 1 - y), (1 - x, 1 - y)]

        def slot(px, py, pc):
            return out_ref.at[4 * px + 2 * py + pc]

        def copy(k, block, to, src=None):
            return pltpu.make_async_remote_copy(
                src_ref=slot(*block) if src is None else src, dst_ref=slot(*block),
                send_sem=send_sems.at[k], recv_sem=recv_sems.at[k], device_id=to, device_id_type=MESH_ID)

        mine = pltpu.make_async_copy(x_ref, slot(*me), local_sem)
        mine.start()
        first = [copy(0, me, sibling, src=x_ref)]
        first += [copy(1 + j, me, (*chip, c), src=x_ref) for j, chip in enumerate(chips)]
        for cp in first:
            cp.start()
        passed = [copy(4 + j, (*chip, c), sibling) for j, chip in enumerate(chips)]
        for j, chip in enumerate(chips):
            copy(1 + j, (*chip, c), me).wait_recv()
            passed[j].start()
        copy(0, sibling, me).wait_recv()
        for j, chip in enumerate(chips):
            copy(4 + j, (*chip, 1 - c), me).wait_recv()
        for cp in first + passed:
            cp.wait_send()
        mine.wait()

    return pl.pallas_call(
        body, name=name,
        in_specs=[_ANY], out_specs=_ANY,
        out_shape=jax.ShapeDtypeStruct((N_DEV,) + shard.shape, shard.dtype),
        scratch_shapes=[pltpu.SemaphoreType.DMA((N_DEV - 1,)), pltpu.SemaphoreType.DMA((N_DEV - 1,)),
                        pltpu.SemaphoreType.DMA],
    )(shard)


def _all_gather_vmem(vec, name):
    r = vec.shape[0]

    def body(v_ref, o_ref, send_sems, recv_sems):
        me, peers = _me_and_peers()
        o_ref[me] = v_ref[...]
        sends = []
        for kk, (dev, _) in enumerate(peers):
            cp = pltpu.make_async_remote_copy(
                src_ref=v_ref, dst_ref=o_ref.at[me],
                send_sem=send_sems.at[kk], recv_sem=recv_sems.at[kk],
                device_id=dev, device_id_type=MESH_ID)
            cp.start()
            sends.append(cp)
        for kk, (dev, idx) in enumerate(peers):
            pltpu.make_async_remote_copy(
                src_ref=v_ref, dst_ref=o_ref.at[idx],
                send_sem=send_sems.at[kk], recv_sem=recv_sems.at[kk],
                device_id=dev, device_id_type=MESH_ID).wait_recv()
        for cp in sends:
            cp.wait_send()

    return pl.pallas_call(
        body, name=name,
        in_specs=[pl.BlockSpec(memory_space=pltpu.VMEM)],
        out_specs=pl.BlockSpec(memory_space=pltpu.VMEM),
        out_shape=jax.ShapeDtypeStruct((N_DEV, r, LANES), F32),
        scratch_shapes=[pltpu.SemaphoreType.DMA((N_DEV - 1,)), pltpu.SemaphoreType.DMA((N_DEV - 1,))],
        compiler_params=pltpu.CompilerParams(vmem_limit_bytes=VMEM_LIMIT),
    )(vec)


def _adamw_math(w, g, m, v):
    m = ADAM_B1 * m + (1.0 - ADAM_B1) * g
    v = ADAM_B2 * v + (1.0 - ADAM_B2) * (g * g)
    m_hat = m / (1.0 - ADAM_B1 ** ADAM_STEP)
    v_hat = v / (1.0 - ADAM_B2 ** ADAM_STEP)
    delta = -ADAM_LR * (m_hat / (jnp.sqrt(v_hat) + ADAM_EPS) + ADAM_WD * w)
    return delta, m, v


def _adamw_sum(parts, w, m, v, tr, name, own=None, me=None):
    r, c = w.shape

    def body(*refs):
        if own is None:
            p_ref, w_ref, m_ref, v_ref, g_ref, d_ref, nm_ref, nv_ref = refs
            terms = [p_ref[kk] for kk in range(N_DEV)]
        else:
            me_ref, p_ref, own_ref, w_ref, m_ref, v_ref, g_ref, d_ref, nm_ref, nv_ref = refs
            terms = [jnp.where(me_ref[0] == kk, own_ref[0], p_ref[kk]).astype(F32) for kk in range(N_DEV)]
        g = terms[0]
        for t in terms[1:]:
            g = g + t
        g_ref[...] = g
        d_ref[...], nm_ref[...], nv_ref[...] = _adamw_math(w_ref[...], g, m_ref[...], v_ref[...])

    out_shape = [jax.ShapeDtypeStruct((r, c), F32)] * 4
    if own is None:
        blk = pl.BlockSpec((tr, c), lambda i: (i, 0))
        return pl.pallas_call(
            body, name=name, grid=(r // tr,),
            in_specs=[pl.BlockSpec((N_DEV, tr, c), lambda i: (0, i, 0)), blk, blk, blk],
            out_specs=[blk] * 4, out_shape=out_shape,
            compiler_params=_params(("parallel",)),
        )(parts, w, m, v)
    blk = pl.BlockSpec((tr, c), lambda i, me_ref: (i, 0))
    return pl.pallas_call(
        body, name=name,
        grid_spec=pltpu.PrefetchScalarGridSpec(
            num_scalar_prefetch=1, grid=(r // tr,),
            in_specs=[pl.BlockSpec((N_DEV, tr, c), lambda i, me_ref: (0, i, 0)),
                      pl.BlockSpec((1, tr, c), lambda i, me_ref: (me_ref[0], i, 0)), blk, blk, blk],
            out_specs=[blk] * 4),
        out_shape=out_shape,
        compiler_params=_params(("parallel",)),
    )(jnp.reshape(me, (1,)).astype(jnp.int32), parts, own, w, m, v)


_SMALL = ("norm1_g", "gf_b", "gb_b", "gla_norm_g", "attn_norm_g", "norm2_g", "conv_b", "final_norm_g",
          "gf_up", "gb_up", "conv_w")


def _pack(named):
    flat = jnp.concatenate([jnp.ravel(t).astype(F32) for t in named])
    tile = SUBLANES * LANES
    total = -(-flat.shape[0] // tile) * tile
    return jnp.pad(flat, (0, total - flat.shape[0])).reshape(total // LANES, LANES)


def _unpack(packed, shapes):
    flat = packed.reshape(-1)
    out, off = [], 0
    for shp in shapes:
        size = int(np.prod(shp))
        out.append(flat[off:off + size].reshape(shp))
        off += size
    return out


def kernel(x, norm1_g, w_in, gf_up, gf_b, gb_up, gb_b, gla_norm_g, attn_norm_g, w_out, norm2_g, w_gate, w_up, conv_w, conv_b, w_down, final_norm_g, loss_target, m_norm1_g, m_w_in, m_gf_up, m_gf_b, m_gb_up, m_gb_b, m_gla_norm_g, m_attn_norm_g, m_w_out, m_norm2_g, m_w_gate, m_w_up, m_conv_w, m_conv_b, m_w_down, m_final_norm_g, v_norm1_g, v_w_in, v_gf_up, v_gf_b, v_gb_up, v_gb_b, v_gla_norm_g, v_attn_norm_g, v_w_out, v_norm2_g, v_w_gate, v_w_up, v_conv_w, v_conv_b, v_w_down, v_final_norm_g):
    names = ("norm1_g", "w_in", "gf_up", "gf_b", "gb_up", "gb_b", "gla_norm_g", "attn_norm_g", "w_out", "norm2_g",
             "w_gate", "w_up", "conv_w", "conv_b", "w_down", "final_norm_g")
    ws = dict(zip(names, (norm1_g, w_in, gf_up, gf_b, gb_up, gb_b, gla_norm_g, attn_norm_g, w_out, norm2_g,
                          w_gate, w_up, conv_w, conv_b, w_down, final_norm_g)))
    ms = dict(zip(names, (m_norm1_g, m_w_in, m_gf_up, m_gf_b, m_gb_up, m_gb_b, m_gla_norm_g, m_attn_norm_g, m_w_out,
                          m_norm2_g, m_w_gate, m_w_up, m_conv_w, m_conv_b, m_w_down, m_final_norm_g)))
    vs = dict(zip(names, (v_norm1_g, v_w_in, v_gf_up, v_gf_b, v_gb_up, v_gb_b, v_gla_norm_g, v_attn_norm_g, v_w_out,
                          v_norm2_g, v_w_gate, v_w_up, v_conv_w, v_conv_b, v_w_down, v_final_norm_g)))
    me = 4 * lax.axis_index("x") + 2 * lax.axis_index("y") + lax.axis_index("c")
    big = ("w_in", "w_out", "w_gate", "w_up", "w_down")
    col_sharded = ("w_in", "w_gate", "w_up")

    def gather_start(group, name, deps=()):
        shards = [ws[n][0].astype(BF16) for n in group]
        lands = [lax.empty((N_DEV,) + t.shape, BF16) for t in shards]
        return _exchange_start(shards, lands, False, name, deps)

    def gather_finish(group, started, name, after):
        full = {}
        for n, own, t in zip(group, *_exchange_wait(started, name, after)):
            t = lax.dynamic_update_slice(t, own[None], (me, 0, 0))
            if n in col_sharded:
                full[n] = jnp.transpose(t, (1, 0, 2)).reshape(t.shape[1], N_DEV * t.shape[2])
            else:
                full[n] = t.reshape(N_DEV * t.shape[1], t.shape[2])
        return full

    w_in_all = _all_gather_two_level(ws["w_in"][0].astype(BF16), "gather_w_in")
    full = {"w_in": jnp.pad(jnp.transpose(w_in_all, (1, 0, 2)).reshape(D_MODEL, IN_WIDTH),
                            ((0, 0), (0, IN_PAD - IN_WIDTH)))}
    late = ("w_out", "w_gate", "w_up", "w_down")
    started_b = gather_start(late, "gather_late_start", deps=(full["w_in"],))

    def late_weights(after):
        return gather_finish(late, started_b, "gather_late_wait", after)

    small_sharded = ("gf_up", "gb_up", "conv_w")
    sm = _all_gather_vmem(_pack([ws[n][0] for n in small_sharded]), "gather_small")
    shard_shapes = [ws[n][0].shape for n in small_sharded]
    per_dev = [_unpack(sm[d], shard_shapes) for d in range(N_DEV)]
    for i, n in enumerate(small_sharded):
        full[n] = jnp.concatenate([per_dev[d][i] for d in range(N_DEV)], axis=1)
    for n in ("norm1_g", "gf_b", "gb_b", "gla_norm_g", "attn_norm_g", "norm2_g", "conv_b"):
        full[n] = ws[n]
    full["final_norm_g"] = final_norm_g.reshape(1, D_MODEL)

    in_flight = []

    def grad_sink(group, grads):
        partials = []
        for n, t in zip(group, grads):
            if n == "w_in":
                t = t[:, :IN_WIDTH].astype(BF16)
            if n in col_sharded:
                t = jnp.transpose(t.reshape(t.shape[0], N_DEV, t.shape[1] // N_DEV), (1, 0, 2))
            else:
                t = t.reshape(N_DEV, t.shape[0] // N_DEV, t.shape[1])
            partials.append(t)
        lands = [lax.empty(t.shape, t.dtype) for t in partials]
        started = _exchange_start(partials, lands, True, "exchange_" + "_".join(group) + "_start")
        in_flight.append((group, started))
        return (started["token"],)

    loss_acc, grad_x, g = _local_step(x[0], loss_target[0], full, late_weights, grad_sink,
                                      first_dep=(started_b["token"],))

    out = {}
    for group, started in in_flight:
        sent, landed = _exchange_wait(started, "exchange_" + "_".join(group) + "_wait", grad_x)
        for n, parts, own in zip(group, landed, sent):
            out[n] = _adamw_sum(parts, ws[n][0], ms[n][0], vs[n][0], 64, "adamw_" + n, own=own, me=me)

    small_full_shapes = [g[n].shape for n in _SMALL]
    gsmall = _pack([g[n] for n in _SMALL] + [loss_acc[0:1, 0:1]])
    gathered_small = _all_gather_vmem(gsmall, "gather_small_grads")

    def full_small(d):
        parts = []
        for n in _SMALL:
            t = d[n].reshape(d[n].shape[-2:]) if d[n].ndim == 3 else d[n].reshape(1, -1)
            if n in small_sharded:
                wide = jnp.zeros((t.shape[0], t.shape[1] * N_DEV), F32)
                t = lax.dynamic_update_slice_in_dim(wide, t, me * t.shape[1], axis=1)
            parts.append(t)
        return _pack(parts + [jnp.zeros((1, 1), F32)])

    rows = gsmall.shape[0]
    res_small = _adamw_sum(gathered_small, full_small(ws), full_small(ms), full_small(vs), rows, "adamw_small")
    loss = res_small[0].reshape(-1)[sum(int(np.prod(sh)) for sh in small_full_shapes)]
    unpacked = [_unpack(t, small_full_shapes) for t in res_small]
    for i, n in enumerate(_SMALL):
        vals = [u[i] for u in unpacked]
        if n in small_sharded:
            width = vals[0].shape[1] // N_DEV
            vals = [lax.dynamic_slice_in_dim(t, me * width, width, axis=1) for t in vals]
        out[n] = vals

    result = [loss, grad_x[None]]
    for kind in range(4):
        for n in names:
            result.append(out[n][kind].reshape(ws[n].shape))
    return tuple(result)
```

```python
import functools

import numpy as np
import jax
import jax.numpy as jnp
from jax import lax
from jax.experimental import pallas as pl
from jax.experimental.pallas import tpu as pltpu

F32 = jnp.float32
BF16 = jnp.bfloat16

D_MODEL = 2048
ATTN_W = 1024
ATTN_HEADS = 8
HEAD_DIM = 128
ROPE_DIM = 32
ROPE_THETA = 500000.0
DILATIONS = (1, 4, 16)
N_SIDE = 64
GLA_KW = 512
GLA_VW = 1024
GLA_HEADS = 4
GLA_DK = 128
GLA_DV = 256
GLA_RANK = 16
GLA_GATE_NORM = 16.0
GLA_CHUNK = 64
IN_WIDTH = 6176
IN_PAD = 6400
D_FF = 5632
EPS = 1e-6
N_DEV = 8

OFF_AQ, OFF_AK, OFF_AV = 0, 1024, 2048
OFF_GQ, OFF_GK, OFF_GV, OFF_GR, OFF_Z = 3072, 3584, 4096, 5120, 6144

ADAM_LR, ADAM_B1, ADAM_B2, ADAM_EPS, ADAM_WD, ADAM_STEP = 0.001, 0.9, 0.999, 1e-08, 0.01, 10

LANES = 128
SUBLANES = 8
VMEM_LIMIT = 56 * 1024 * 1024
ROW_BLOCK = 256
ATTN_BLOCK = 128
GLA_CHUNKS_PER_STEP = 4
NEG = -1e30
MESH_ID = pl.DeviceIdType.MESH


def _params(sem):
    return pltpu.CompilerParams(dimension_semantics=sem, vmem_limit_bytes=VMEM_LIMIT)


def _dot(a, b):
    return lax.dot_general(a, b, (((1,), (0,)), ((), ())), preferred_element_type=F32)


def _dot_nt(a, b):
    return lax.dot_general(a, b, (((1,), (1,)), ((), ())), preferred_element_type=F32)


def _dot_tn(a, b):
    return lax.dot_general(a, b, (((0,), (0,)), ((), ())), preferred_element_type=F32)


def _sigmoid(x):
    return 0.5 * jnp.tanh(0.5 * x) + 0.5


def _matmul(pairs, mode, out_dtype, tm, tn, tk, name, res=None, deps=()):
    a0, b0 = pairs[0]
    if mode == "nn":
        (m, kdim), n = a0.shape, b0.shape[1]
    elif mode == "nt":
        (m, kdim), n = a0.shape, b0.shape[0]
    else:
        (kdim, m), n = a0.shape, b0.shape[1]
    assert m % tm == 0 and n % tn == 0 and kdim % tk == 0, (name, m, n, kdim)
    nk = kdim // tk
    npairs = len(pairs)
    steps = nk * npairs
    dot = {"nn": _dot, "nt": _dot_nt, "tn": _dot_tn}[mode]

    def kidx(p):
        return lambda k: jnp.clip(k - p * nk, 0, nk - 1)

    in_specs, args = [], []
    for p, (a, b) in enumerate(pairs):
        kk = kidx(p)
        if mode == "nn":
            in_specs += [pl.BlockSpec((tm, tk), lambda i, j, k, kk=kk: (i, kk(k))),
                         pl.BlockSpec((tk, tn), lambda i, j, k, kk=kk: (kk(k), j))]
        elif mode == "nt":
            in_specs += [pl.BlockSpec((tm, tk), lambda i, j, k, kk=kk: (i, kk(k))),
                         pl.BlockSpec((tn, tk), lambda i, j, k, kk=kk: (j, kk(k)))]
        else:
            in_specs += [pl.BlockSpec((tk, tm), lambda i, j, k, kk=kk: (kk(k), i)),
                         pl.BlockSpec((tk, tn), lambda i, j, k, kk=kk: (kk(k), j))]
        args += [a, b]
    if res is not None:
        in_specs.append(pl.BlockSpec((tm, tn), lambda i, j, k: (i, j)))
        args.append(res)
    in_specs += [pl.BlockSpec(memory_space=pl.ANY)] * len(deps)
    args += list(deps)

    def body(*refs):
        ab = refs[:2 * npairs]
        res_ref = refs[2 * npairs] if res is not None else None
        o_ref = refs[2 * npairs + (1 if res is not None else 0) + len(deps)]

        def finish(acc):
            if res_ref is not None:
                acc = acc + res_ref[...]
            o_ref[...] = acc.astype(out_dtype)

        if steps == 1:
            finish(dot(ab[0][...], ab[1][...]))
            return
        acc_ref = refs[-1]
        k = pl.program_id(2)

        @pl.when(k == 0)
        def _():
            acc_ref[...] = jnp.zeros_like(acc_ref)

        for p in range(npairs):
            @pl.when((k >= p * nk) & (k < (p + 1) * nk))
            def _(p=p):
                acc_ref[...] += dot(ab[2 * p][...], ab[2 * p + 1][...])

        @pl.when(k == steps - 1)
        def _():
            finish(acc_ref[...])

    return pl.pallas_call(
        body, name=name,
        grid=(m // tm, n // tn, steps),
        in_specs=in_specs,
        out_specs=pl.BlockSpec((tm, tn), lambda i, j, k: (i, j)),
        out_shape=jax.ShapeDtypeStruct((m, n), out_dtype),
        scratch_shapes=[] if steps == 1 else [pltpu.VMEM((tm, tn), F32)],
        compiler_params=_params(("parallel", "parallel", "arbitrary")),
    )(*args)


def _rms_fwd(x, g, name):
    s, d = x.shape

    def body(x_ref, g_ref, o_ref):
        xv = x_ref[...]
        r = lax.rsqrt(jnp.mean(xv * xv, axis=-1, keepdims=True) + EPS)
        o_ref[...] = (xv * r * g_ref[...]).astype(BF16)

    return pl.pallas_call(
        body, name=name, grid=(s // ROW_BLOCK,),
        in_specs=[pl.BlockSpec((ROW_BLOCK, d), lambda i: (i, 0)), pl.BlockSpec((1, d), lambda i: (0, 0))],
        out_specs=pl.BlockSpec((ROW_BLOCK, d), lambda i: (i, 0)),
        out_shape=jax.ShapeDtypeStruct((s, d), BF16),
        compiler_params=_params(("parallel",)),
    )(x, g)


def _rms_bwd(dn, x, g, dres, name):
    s, d = x.shape

    def body(dn_ref, x_ref, g_ref, dres_ref, dx_ref, dxb_ref, gg_ref):
        i = pl.program_id(0)
        xv, dnv = x_ref[...], dn_ref[...]
        r = lax.rsqrt(jnp.mean(xv * xv, axis=-1, keepdims=True) + EPS)
        dng = dnv * g_ref[...]
        c = jnp.mean(dng * xv, axis=-1, keepdims=True)
        dx = dres_ref[...] + r * dng - xv * (r * r * r * c)
        dx_ref[...] = dx
        dxb_ref[...] = dx.astype(BF16)

        @pl.when(i == 0)
        def _():
            gg_ref[...] = jnp.zeros_like(gg_ref)

        gg_ref[...] += jnp.sum(dnv * xv * r, axis=0, keepdims=True)

    row = pl.BlockSpec((ROW_BLOCK, d), lambda i: (i, 0))
    vec = pl.BlockSpec((1, d), lambda i: (0, 0))
    return pl.pallas_call(
        body, name=name, grid=(s // ROW_BLOCK,),
        in_specs=[row, row, vec, row],
        out_specs=[row, row, vec],
        out_shape=[jax.ShapeDtypeStruct((s, d), F32), jax.ShapeDtypeStruct((s, d), BF16),
                   jax.ShapeDtypeStruct((1, d), F32)],
        compiler_params=_params(("arbitrary",)),
    )(dn, x, g, dres)


def _final_loss(h2, target, g, name="final_loss"):
    s, d = h2.shape

    def body(h_ref, t_ref, g_ref, dh_ref, dhb_ref, loss_ref, gg_ref):
        i = pl.program_id(0)
        hv, gv = h_ref[...], g_ref[...]
        r = lax.rsqrt(jnp.mean(hv * hv, axis=-1, keepdims=True) + EPS)
        e = hv * r * gv - t_ref[...]
        dy = e * (1.0 / d)
        dyg = dy * gv
        c = jnp.mean(dyg * hv, axis=-1, keepdims=True)
        dh = r * dyg - hv * (r * r * r * c)
        dh_ref[...] = dh
        dhb_ref[...] = dh.astype(BF16)

        @pl.when(i == 0)
        def _():
            gg_ref[...] = jnp.zeros_like(gg_ref)
            loss_ref[...] = jnp.zeros_like(loss_ref)

        gg_ref[...] += jnp.sum(dy * hv * r, axis=0, keepdims=True)
        loss_ref[...] += jnp.sum(jnp.sum(e * e, axis=-1, keepdims=True), axis=0, keepdims=True) * (0.5 / d)

    row = pl.BlockSpec((ROW_BLOCK, d), lambda i: (i, 0))
    vec = pl.BlockSpec((1, d), lambda i: (0, 0))
    return pl.pallas_call(
        body, name=name, grid=(s // ROW_BLOCK,),
        in_specs=[row, row, vec],
        out_specs=[row, row, pl.BlockSpec((SUBLANES, LANES), lambda i: (0, 0)), vec],
        out_shape=[jax.ShapeDtypeStruct((s, d), F32), jax.ShapeDtypeStruct((s, d), BF16),
                   jax.ShapeDtypeStruct((SUBLANES, LANES), F32), jax.ShapeDtypeStruct((1, d), F32)],
        compiler_params=_params(("arbitrary",)),
    )(h2, target, g)


def _rope_tables(s):
    pos = jnp.arange(s, dtype=F32)
    inv_freq = ROPE_THETA ** (-jnp.arange(0, ROPE_DIM, 2, dtype=F32) / ROPE_DIM)
    ang = pos[:, None] * inv_freq[None, :]
    cos, sin = jnp.cos(ang), jnp.sin(ang)
    half = ROPE_DIM // 2
    rest = HEAD_DIM - ROPE_DIM
    c = jnp.concatenate([cos, cos, jnp.ones((s, rest), F32)], axis=1)
    sm = jnp.concatenate([-sin, jnp.zeros((s, half + rest), F32)], axis=1)
    sp = jnp.concatenate([jnp.zeros((s, half), F32), sin, jnp.zeros((s, rest), F32)], axis=1)
    return c, sm, sp


def _res_shape(s, groups, dil, dtype):
    return jax.ShapeDtypeStruct((s // dil, dil * groups * LANES), dtype)


def _res_spec(groups, dil):
    return pl.BlockSpec((ROW_BLOCK // dil, dil * groups * LANES), lambda i: (i, 0))


def _to_residues(scr, o_ref, dil):
    groups, rows = scr.shape[0], ROW_BLOCK // dil
    for r in range(dil):
        for h in range(groups):
            piece = scr[h] if dil == 1 else scr.at[h][pl.ds(r, rows, stride=dil), :]
            o_ref[:, (r * groups + h) * LANES:(r * groups + h + 1) * LANES] = piece.astype(o_ref.dtype)


def _from_residues(i_ref, scr, dil):
    groups, rows = scr.shape[0], ROW_BLOCK // dil
    for r in range(dil):
        for h in range(groups):
            piece = i_ref[:, (r * groups + h) * LANES:(r * groups + h + 1) * LANES].astype(F32)
            if dil == 1:
                scr[h] = piece
            else:
                scr.at[h][pl.ds(r, rows, stride=dil), :] = piece


def _rope_fwd(proj, tables, name="rope_fwd"):
    s = proj.shape[0]
    half = ROPE_DIM // 2
    nd = len(DILATIONS)

    def body(p_ref, c_ref, sm_ref, sp_ref, *rest):
        outs, scr = rest[:3 * nd], rest[3 * nd]
        c, sm, sp = c_ref[...], sm_ref[...], sp_ref[...]
        for gi, off in enumerate((OFF_AQ, OFF_AK, OFF_AV)):
            for h in range(ATTN_HEADS):
                t = p_ref[:, off + h * HEAD_DIM: off + (h + 1) * HEAD_DIM]
                if off != OFF_AV:
                    t = t * c + pltpu.roll(t, HEAD_DIM - half, 1) * sm + pltpu.roll(t, half, 1) * sp
                scr[h] = t
            for di, dil in enumerate(DILATIONS):
                _to_residues(scr, outs[3 * di + gi], dil)

    tab = pl.BlockSpec((ROW_BLOCK, HEAD_DIM), lambda i: (i, 0))
    outs = pl.pallas_call(
        body, name=name, grid=(s // ROW_BLOCK,),
        in_specs=[pl.BlockSpec((ROW_BLOCK, 3 * ATTN_W), lambda i: (i, 0)), tab, tab, tab],
        out_specs=[_res_spec(ATTN_HEADS, d) for d in DILATIONS for _ in range(3)],
        out_shape=[_res_shape(s, ATTN_HEADS, d, BF16) for d in DILATIONS for _ in range(3)],
        scratch_shapes=[pltpu.VMEM((ATTN_HEADS, ROW_BLOCK, LANES), F32)],
        compiler_params=_params(("parallel",)),
    )(proj, *tables)
    return [tuple(outs[3 * di:3 * di + 3]) for di in range(nd)]


def _rope_bwd(grads, tables, name="rope_bwd"):
    s = grads[0][0].shape[0] * DILATIONS[0]
    half = ROPE_DIM // 2
    nd = len(DILATIONS)

    def body(*refs):
        ins = refs[:3 * nd]
        c_ref, sm_ref, sp_ref, o_ref = refs[3 * nd:3 * nd + 4]
        scrs = refs[3 * nd + 4:]
        c, sm, sp = c_ref[...], sm_ref[...], sp_ref[...]
        for gi, off in enumerate((OFF_AQ, OFF_AK, OFF_AV)):
            for di, dil in enumerate(DILATIONS):
                _from_residues(ins[3 * di + gi], scrs[di], dil)
            for h in range(ATTN_HEADS):
                t = scrs[0][h]
                for scr in scrs[1:]:
                    t = t + scr[h]
                if off != OFF_AV:
                    t = t * c + pltpu.roll(t * sm, half, 1) + pltpu.roll(t * sp, HEAD_DIM - half, 1)
                o_ref[:, off + h * HEAD_DIM: off + (h + 1) * HEAD_DIM] = t.astype(BF16)

    tab = pl.BlockSpec((ROW_BLOCK, HEAD_DIM), lambda i: (i, 0))
    return pl.pallas_call(
        body, name=name, grid=(s // ROW_BLOCK,),
        in_specs=[_res_spec(ATTN_HEADS, d) for d in DILATIONS for _ in range(3)] + [tab, tab, tab],
        out_specs=pl.BlockSpec((ROW_BLOCK, 3 * ATTN_W), lambda i: (i, 0)),
        out_shape=jax.ShapeDtypeStruct((s, IN_PAD), BF16),
        scratch_shapes=[pltpu.VMEM((ATTN_HEADS, ROW_BLOCK, LANES), F32) for _ in DILATIONS],
        compiler_params=_params(("parallel",)),
    )(*[t for g in grads for t in g], *tables)


ATTN_GROUP = 2


def _window_specs(nsteps, width):
    rows, hb = ATTN_GROUP * ATTN_BLOCK, N_SIDE
    per = rows // hb
    cur = pl.BlockSpec((rows, width), lambda r, j: (j, r))
    prev = pl.BlockSpec((hb, width), lambda r, j: (jnp.maximum(per * j - 1, 0), r))
    nxt = pl.BlockSpec((hb, width), lambda r, j: (jnp.minimum(per * (j + 1), per * nsteps - 1), r))
    return prev, cur, nxt


def _block(ref, b, sl):
    return ref[b * ATTN_BLOCK:(b + 1) * ATTN_BLOCK, sl]


def _edge(prev_ref, cur_ref, next_ref, b, sl):
    qb, hb = ATTN_BLOCK, N_SIDE
    before = prev_ref[:, sl] if b == 0 else cur_ref[b * qb - hb:b * qb, sl]
    after = next_ref[:, sl] if b == ATTN_GROUP - 1 else cur_ref[(b + 1) * qb:(b + 1) * qb + hb, sl]
    return jnp.concatenate([before, after], axis=0)


def _band_masks(j, length):
    qb, hb = ATTN_BLOCK, N_SIDE
    row = lax.broadcasted_iota(jnp.int32, (qb, qb), 0)
    col = lax.broadcasted_iota(jnp.int32, (qb, qb), 1)

    def edge_pos(i):
        return j * qb - hb + i + jnp.where(i >= hb, qb, 0)

    def ok(a, b, outside):
        return (jnp.abs(a - b) <= N_SIDE) & (outside >= 0) & (outside < length)

    cur = jnp.abs(row - col) <= N_SIDE
    edge_k = ok(j * qb + row, edge_pos(col), edge_pos(col))
    edge_q = ok(edge_pos(row), j * qb + col, edge_pos(row))
    return cur, edge_k, edge_q


def _attn_fwd(q, k, v, dil, name):
    length = q.shape[0]
    qb = ATTN_BLOCK
    nsteps = length // (ATTN_GROUP * qb)
    scale = HEAD_DIM ** -0.5

    def body(q_ref, kp_ref, kc_ref, kn_ref, vp_ref, vc_ref, vn_ref, o_ref, lse_ref):
        masks = [_band_masks(pl.program_id(1) * ATTN_GROUP + b, length) for b in range(ATTN_GROUP)]
        lane = lax.broadcasted_iota(jnp.int32, (qb, LANES), 1)
        units = [(b, h, slice(h * HEAD_DIM, (h + 1) * HEAD_DIM)) for b in range(ATTN_GROUP)
                 for h in range(ATTN_HEADS)]
        scores = [(_dot_nt(_block(q_ref, b, sl), _block(kc_ref, b, sl)),
                   _dot_nt(_block(q_ref, b, sl), _edge(kp_ref, kc_ref, kn_ref, b, sl))) for b, _, sl in units]
        probs = []
        lse_acc = [jnp.zeros((qb, LANES), F32) for _ in range(ATTN_GROUP)]
        for (b, h, _), (s_c, s_e) in zip(units, scores):
            valid_c, valid_e, _ = masks[b]
            s_c = jnp.where(valid_c, s_c * scale, NEG)
            s_e = jnp.where(valid_e, s_e * scale, NEG)
            m = jnp.max(jnp.maximum(s_c, s_e), axis=-1, keepdims=True)
            p_c, p_e = jnp.exp(s_c - m), jnp.exp(s_e - m)
            den = jnp.sum(p_c + p_e, axis=-1, keepdims=True)
            probs.append((p_c.astype(BF16), p_e.astype(BF16), 1.0 / den))
            lse_acc[b] = jnp.where(lane == h, m + jnp.log(den), lse_acc[b])
        for (b, _, sl), (p_c, p_e, inv) in zip(units, probs):
            o_ref[b * qb:(b + 1) * qb, sl] = (_dot(p_c, _block(vc_ref, b, sl))
                                              + _dot(p_e, _edge(vp_ref, vc_ref, vn_ref, b, sl))) * inv
        for b in range(ATTN_GROUP):
            lse_ref[b * qb:(b + 1) * qb, :] = lse_acc[b]

    prev, cur, nxt = _window_specs(nsteps, ATTN_W)
    return pl.pallas_call(
        body, name=name, grid=(dil, nsteps),
        in_specs=[cur, prev, cur, nxt, prev, cur, nxt],
        out_specs=[cur, pl.BlockSpec((ATTN_GROUP * qb, LANES), lambda r, j: (j, r))],
        out_shape=[jax.ShapeDtypeStruct((length, dil * ATTN_W), F32),
                   jax.ShapeDtypeStruct((length, dil * LANES), F32)],
        compiler_params=_params(("parallel", "parallel")),
    )(q, k, k, k, v, v, v)


def _attn_combine(outs, lses, g, name="attn_combine"):
    s = outs[0].shape[0] * DILATIONS[0]
    nd = len(DILATIONS)

    def body(*refs):
        o_refs, l_refs = refs[:nd], refs[nd:2 * nd]
        g_ref, o_ref, n_ref = refs[2 * nd:2 * nd + 3]
        lse_outs = refs[2 * nd + 3:3 * nd + 3]
        o_scr, l_scr = refs[3 * nd + 3:4 * nd + 3], refs[4 * nd + 3:5 * nd + 3]
        for di, dil in enumerate(DILATIONS):
            _from_residues(o_refs[di], o_scr[di], dil)
            _from_residues(l_refs[di], l_scr[di], dil)
        ls = [scr[0] for scr in l_scr]
        m = ls[0]
        for l in ls[1:]:
            m = jnp.maximum(m, l)
        es = [jnp.exp(l - m) for l in ls]
        z = es[0]
        for e in es[1:]:
            z = z + e
        ws = [e / z for e in es]
        l_scr[0][0] = m + jnp.log(z)
        for di, dil in enumerate(DILATIONS):
            _to_residues(l_scr[0], lse_outs[di], dil)
        ssq = jnp.zeros((ROW_BLOCK, 1), F32)
        for h in range(ATTN_HEADS):
            sl = slice(h * HEAD_DIM, (h + 1) * HEAD_DIM)
            acc = ws[0][:, h:h + 1] * o_scr[0][h]
            for w, scr in zip(ws[1:], o_scr[1:]):
                acc = acc + w[:, h:h + 1] * scr[h]
            o_ref[:, sl] = acc
            ssq = ssq + jnp.sum(acc * acc, axis=-1, keepdims=True)
        r = lax.rsqrt(ssq * (1.0 / ATTN_W) + EPS)
        n_ref[...] = (o_ref[...] * r * g_ref[...]).astype(BF16)

    blk = pl.BlockSpec((ROW_BLOCK, ATTN_W), lambda i: (i, 0))
    outs_ = pl.pallas_call(
        body, name=name, grid=(s // ROW_BLOCK,),
        in_specs=[_res_spec(ATTN_HEADS, d) for d in DILATIONS] + [_res_spec(1, d) for d in DILATIONS]
        + [pl.BlockSpec((1, ATTN_W), lambda i: (0, 0))],
        out_specs=[blk, blk] + [_res_spec(1, d) for d in DILATIONS],
        out_shape=[jax.ShapeDtypeStruct((s, ATTN_W), F32), jax.ShapeDtypeStruct((s, D_MODEL), BF16)]
        + [_res_shape(s, 1, d, F32) for d in DILATIONS],
        scratch_shapes=[pltpu.VMEM((ATTN_HEADS, ROW_BLOCK, LANES), F32) for _ in DILATIONS]
        + [pltpu.VMEM((1, ROW_BLOCK, LANES), F32) for _ in DILATIONS],
        compiler_params=_params(("parallel",)),
    )(*outs, *lses, g)
    return outs_[0], outs_[1], list(outs_[2:])


def _attn_prebwd(dcat, o, g, name="attn_prebwd"):
    s = o.shape[0]
    nd = len(DILATIONS)

    def body(dy_ref, o_ref, g_ref, *rest):
        do_outs, delta_outs, gg_ref = rest[:nd], rest[nd:2 * nd], rest[2 * nd]
        do_scr, delta_scr = rest[2 * nd + 1], rest[2 * nd + 2]
        i = pl.program_id(0)
        dy, ov = dy_ref[...], o_ref[...]
        r = lax.rsqrt(jnp.mean(ov * ov, axis=-1, keepdims=True) + EPS)
        dyg = dy * g_ref[...]
        c = jnp.mean(dyg * ov, axis=-1, keepdims=True)
        do = r * dyg - ov * (r * r * r * c)
        prod = do * ov
        lane = lax.broadcasted_iota(jnp.int32, (ROW_BLOCK, LANES), 1)
        acc = jnp.zeros((ROW_BLOCK, LANES), F32)
        for h in range(ATTN_HEADS):
            sl = slice(h * HEAD_DIM, (h + 1) * HEAD_DIM)
            do_scr[h] = do[:, sl]
            acc = jnp.where(lane == h, jnp.sum(prod[:, sl], axis=-1, keepdims=True), acc)
        delta_scr[0] = acc
        for di, dil in enumerate(DILATIONS):
            _to_residues(do_scr, do_outs[di], dil)
            _to_residues(delta_scr, delta_outs[di], dil)

        @pl.when(i == 0)
        def _():
            gg_ref[...] = jnp.zeros_like(gg_ref)

        gg_ref[...] += jnp.sum(dy * ov * r, axis=0, keepdims=True)

    blk = pl.BlockSpec((ROW_BLOCK, ATTN_W), lambda i: (i, 0))
    vec = pl.BlockSpec((1, ATTN_W), lambda i: (0, 0))
    outs = pl.pallas_call(
        body, name=name, grid=(s // ROW_BLOCK,),
        in_specs=[blk, blk, vec],
        out_specs=[_res_spec(ATTN_HEADS, d) for d in DILATIONS] + [_res_spec(1, d) for d in DILATIONS] + [vec],
        out_shape=[_res_shape(s, ATTN_HEADS, d, BF16) for d in DILATIONS]
        + [_res_shape(s, 1, d, F32) for d in DILATIONS] + [jax.ShapeDtypeStruct((1, ATTN_W), F32)],
        scratch_shapes=[pltpu.VMEM((ATTN_HEADS, ROW_BLOCK, LANES), F32), pltpu.VMEM((1, ROW_BLOCK, LANES), F32)],
        compiler_params=_params(("arbitrary",)),
    )(dcat, o, g)
    return list(outs[:nd]), list(outs[nd:2 * nd]), outs[2 * nd]


def _attn_bwd(q, k, v, do, lse, delta, dil, name):
    length = q.shape[0]
    qb = ATTN_BLOCK
    nsteps = length // (ATTN_GROUP * qb)
    scale = HEAD_DIM ** -0.5

    def body(qp, qc, qn, kp, kc, kn, vp, vc, vn, dop, doc, don, lp, lc, ln, dp, dc, dn, dq_ref, dk_ref, dv_ref):
        masks = [_band_masks(pl.program_id(1) * ATTN_GROUP + b, length) for b in range(ATTN_GROUP)]
        everything = slice(None)
        lse_e = [_edge(lp, lc, ln, b, everything) for b in range(ATTN_GROUP)]
        del_e = [_edge(dp, dc, dn, b, everything) for b in range(ATTN_GROUP)]
        units = [(b, h, slice(h * HEAD_DIM, (h + 1) * HEAD_DIM)) for b in range(ATTN_GROUP)
                 for h in range(ATTN_HEADS)]
        prods = []
        for b, _, sl in units:
            q_c, k_c, v_c, do_c = _block(qc, b, sl), _block(kc, b, sl), _block(vc, b, sl), _block(doc, b, sl)
            q_e, k_e = _edge(qp, qc, qn, b, sl), _edge(kp, kc, kn, b, sl)
            v_e, do_e = _edge(vp, vc, vn, b, sl), _edge(dop, doc, don, b, sl)
            prods.append((_dot_nt(q_c, k_c), _dot_nt(do_c, v_c), _dot_nt(q_c, k_e), _dot_nt(do_c, v_e),
                          _dot_nt(q_e, k_c), _dot_nt(do_e, v_c)))
        parts = []
        for (b, h, _), (s_cc, dp_cc, s_ek, dp_ek, s_eq, dp_eq) in zip(units, prods):
            valid_c, valid_ek, valid_eq = masks[b]
            hc = slice(h, h + 1)
            lse_c, del_c = _block(lc, b, hc), _block(dc, b, hc)
            p_cc = jnp.where(valid_c, jnp.exp(s_cc * scale - lse_c), 0.0)
            ds_cc = (p_cc * (dp_cc - del_c)).astype(BF16)
            p_ek = jnp.where(valid_ek, jnp.exp(s_ek * scale - lse_c), 0.0)
            ds_ek = (p_ek * (dp_ek - del_c)).astype(BF16)
            p_eq = jnp.where(valid_eq, jnp.exp(s_eq * scale - lse_e[b][:, hc]), 0.0)
            ds_eq = (p_eq * (dp_eq - del_e[b][:, hc])).astype(BF16)
            parts.append((p_cc.astype(BF16), ds_cc, ds_ek, p_eq.astype(BF16), ds_eq))
        for (b, _, sl), (p_cc, ds_cc, ds_ek, p_eq, ds_eq) in zip(units, parts):
            rows = slice(b * qb, (b + 1) * qb)
            q_c, k_c, do_c = _block(qc, b, sl), _block(kc, b, sl), _block(doc, b, sl)
            q_e, k_e, do_e = _edge(qp, qc, qn, b, sl), _edge(kp, kc, kn, b, sl), _edge(dop, doc, don, b, sl)
            dq_ref[rows, sl] = ((_dot(ds_cc, k_c) + _dot(ds_ek, k_e)) * scale).astype(BF16)
            dk_ref[rows, sl] = ((_dot_tn(ds_cc, q_c) + _dot_tn(ds_eq, q_e)) * scale).astype(BF16)
            dv_ref[rows, sl] = (_dot_tn(p_cc, do_c) + _dot_tn(p_eq, do_e)).astype(BF16)

    wide, narrow = list(_window_specs(nsteps, ATTN_W)), list(_window_specs(nsteps, LANES))
    return tuple(pl.pallas_call(
        body, name=name, grid=(dil, nsteps),
        in_specs=wide * 4 + narrow * 2,
        out_specs=[wide[1]] * 3,
        out_shape=[jax.ShapeDtypeStruct((length, dil * ATTN_W), BF16)] * 3,
        compiler_params=_params(("parallel", "parallel")),
    )(q, q, q, k, k, k, v, v, v, do, do, do, lse, lse, lse, delta, delta, delta))


def _gate_matrices(gf_up, gb_up):
    pad = LANES - 2 * GLA_RANK
    uf = jnp.concatenate([gf_up, jnp.zeros((GLA_RANK + pad, GLA_KW), gf_up.dtype)], axis=0)
    ub = jnp.concatenate([jnp.zeros((GLA_RANK, GLA_KW), gb_up.dtype), gb_up, jnp.zeros((pad, GLA_KW), gb_up.dtype)], axis=0)
    return uf.astype(BF16), ub.astype(BF16)


def _log_sigmoid(x):
    return jnp.minimum(x, 0.0) - jnp.log(1.0 + jnp.exp(-jnp.abs(x)))


def _gla_gates(proj, uf, ub, gf_b, gb_b, name="gla_gates"):
    s = proj.shape[0]

    def body(z_ref, uf_ref, ub_ref, bf_ref, bb_ref, gf_ref, gb_ref):
        z = z_ref[...].astype(BF16)
        gf_ref[...] = _log_sigmoid(_dot(z, uf_ref[...]) + bf_ref[...]) * (1.0 / GLA_GATE_NORM)
        gb_ref[...] = _log_sigmoid(_dot(z, ub_ref[...]) + bb_ref[...]) * (1.0 / GLA_GATE_NORM)

    mat = pl.BlockSpec((LANES, GLA_KW), lambda i: (0, 0))
    vec = pl.BlockSpec((1, GLA_KW), lambda i: (0, 0))
    out = pl.BlockSpec((ROW_BLOCK, GLA_KW), lambda i: (i, 0))
    return pl.pallas_call(
        body, name=name, grid=(s // ROW_BLOCK,),
        in_specs=[pl.BlockSpec((ROW_BLOCK, LANES), lambda i: (i, OFF_Z // LANES)), mat, mat, vec, vec],
        out_specs=[out, out],
        out_shape=[jax.ShapeDtypeStruct((s, GLA_KW), F32)] * 2,
        compiler_params=_params(("parallel",)),
    )(proj, uf, ub, gf_b, gb_b)


def _gla_gates_bwd(dgf, dgb, proj, uf, ub, gf_b, gb_b, dproj, name="gla_gates_bwd"):
    s = proj.shape[0]
    tail = IN_PAD - OFF_Z

    def body(dgf_ref, dgb_ref, z_ref, uf_ref, ub_ref, bf_ref, bb_ref, _, dz_ref, guf_ref, gub_ref, gbf_ref, gbb_ref):
        i = pl.program_id(0)
        z = z_ref[...].astype(BF16)
        uf_, ub_ = uf_ref[...], ub_ref[...]
        dpf = dgf_ref[...] * (1.0 / GLA_GATE_NORM) * _sigmoid(-(_dot(z, uf_) + bf_ref[...]))
        dpb = dgb_ref[...] * (1.0 / GLA_GATE_NORM) * _sigmoid(-(_dot(z, ub_) + bb_ref[...]))
        dpf_b, dpb_b = dpf.astype(BF16), dpb.astype(BF16)
        dz_ref[:, 0:LANES] = (_dot_nt(dpf_b, uf_) + _dot_nt(dpb_b, ub_)).astype(BF16)
        dz_ref[:, LANES:tail] = jnp.zeros((ROW_BLOCK, tail - LANES), BF16)

        @pl.when(i == 0)
        def _():
            for r in (guf_ref, gub_ref, gbf_ref, gbb_ref):
                r[...] = jnp.zeros_like(r)

        guf_ref[...] += _dot_tn(z, dpf_b)
        gub_ref[...] += _dot_tn(z, dpb_b)
        gbf_ref[...] += jnp.sum(dpf, axis=0, keepdims=True)
        gbb_ref[...] += jnp.sum(dpb, axis=0, keepdims=True)

    mat = pl.BlockSpec((LANES, GLA_KW), lambda i: (0, 0))
    vec = pl.BlockSpec((1, GLA_KW), lambda i: (0, 0))
    blk = pl.BlockSpec((ROW_BLOCK, GLA_KW), lambda i: (i, 0))
    return pl.pallas_call(
        body, name=name, grid=(s // ROW_BLOCK,),
        in_specs=[blk, blk, pl.BlockSpec((ROW_BLOCK, LANES), lambda i: (i, OFF_Z // LANES)), mat, mat, vec, vec,
                  pl.BlockSpec(memory_space=pl.ANY)],
        out_specs=[pl.BlockSpec((ROW_BLOCK, tail), lambda i: (i, OFF_Z // tail)), mat, mat, vec, vec],
        out_shape=[jax.ShapeDtypeStruct(dproj.shape, dproj.dtype), jax.ShapeDtypeStruct((LANES, GLA_KW), F32),
                   jax.ShapeDtypeStruct((LANES, GLA_KW), F32), jax.ShapeDtypeStruct((1, GLA_KW), F32),
                   jax.ShapeDtypeStruct((1, GLA_KW), F32)],
        input_output_aliases={7: 0},
        compiler_params=_params(("arbitrary",)),
    )(dgf, dgb, proj, uf, ub, gf_b, gb_b, dproj)


def _split3(x):
    x1 = x.astype(BF16)
    r1 = x - x1.astype(F32)
    x2 = r1.astype(BF16)
    x3 = (r1 - x2.astype(F32)).astype(BF16)
    return x1, x2, x3


def _dot_exact(mask_bf, x):
    x1, x2, x3 = _split3(x)
    return _dot(mask_bf, x1) + _dot(mask_bf, x2) + _dot(mask_bf, x3)


def _chunk_masks(reverse):
    c = GLA_CHUNK
    row = lax.broadcasted_iota(jnp.int32, (c, c), 0)
    col = lax.broadcasted_iota(jnp.int32, (c, c), 1)
    allowed = (col >= row) if reverse else (col <= row)
    seen_by = (col <= row) if reverse else (col >= row)
    return allowed, seen_by


def _chunk_terms(q_ref, k_ref, g_ref, rs, hs, allowed, reverse):
    c = GLA_CHUNK
    mid, last = (c // 2, 0) if reverse else (c // 2 - 1, c - 1)
    q = q_ref[rs, hs] * (GLA_DK ** -0.5)
    k = k_ref[rs, hs]
    b = _dot_exact(jnp.where(allowed, 1.0, 0.0).astype(BF16), g_ref[rs, hs])
    bref, blast = b[mid:mid + 1, :], b[last:last + 1, :]
    e_q, e_k, e_in, e_st = jnp.exp(b - bref), jnp.exp(bref - b), jnp.exp(b), jnp.exp(blast - b)
    return dict(last=last, e_q=e_q, e_k=e_k, e_in=e_in, e_st=e_st,
                dec=jnp.exp(blast), qe=q * e_q, ke=k * e_k, qin=q * e_in, kst=k * e_st)


def _gla_blockspecs(s, reverse_order):
    cb = GLA_CHUNKS_PER_STEP
    rows = cb * GLA_CHUNK
    nsteps = s // rows

    def rb(n):
        return (nsteps - 1 - n) if reverse_order else n

    qspec = pl.BlockSpec((rows, GLA_KW), lambda n: (rb(n), OFF_GQ // GLA_KW))
    kspec = pl.BlockSpec((rows, GLA_KW), lambda n: (rb(n), OFF_GK // GLA_KW))
    vspec = pl.BlockSpec((rows, GLA_VW), lambda n: (rb(n), OFF_GV // GLA_VW))
    gspec = pl.BlockSpec((rows, GLA_KW), lambda n: (rb(n), 0))
    ospec = pl.BlockSpec((rows, GLA_VW), lambda n: (rb(n), 0))
    sspec = pl.BlockSpec((GLA_HEADS, cb, GLA_DV, GLA_DK), lambda n: (0, rb(n), 0, 0))
    return cb, rows, nsteps, qspec, kspec, vspec, gspec, ospec, sspec


def _gla_units(cb, order_reversed):
    chunks = list(reversed(range(cb))) if order_reversed else list(range(cb))
    return [(c, h, slice(c * GLA_CHUNK, (c + 1) * GLA_CHUNK), slice(h * GLA_DK, (h + 1) * GLA_DK),
             slice(h * GLA_DV, (h + 1) * GLA_DV)) for c in chunks for h in range(GLA_HEADS)]


def _gla_fwd(proj, g, reverse, name):
    s = proj.shape[0]
    cb, rows, nsteps, qspec, kspec, vspec, gspec, ospec, sspec = _gla_blockspecs(s, reverse)

    def body(q_ref, k_ref, v_ref, g_ref, o_ref, st_ref, state):
        @pl.when(pl.program_id(0) == 0)
        def _():
            state[...] = jnp.zeros_like(state)

        allowed, _ = _chunk_masks(reverse)
        units = _gla_units(cb, reverse)
        terms = [_chunk_terms(q_ref, k_ref, g_ref, rs, hs, allowed, reverse) for _, _, rs, hs, _ in units]
        vals = [v_ref[rs, vs].astype(BF16) for _, _, rs, _, vs in units]
        raw = [(_dot_nt(t["qe"].astype(BF16), t["ke"].astype(BF16)), _dot_tn(v, t["kst"].astype(BF16)))
               for t, v in zip(terms, vals)]
        intra = [_dot(jnp.where(allowed, a, 0.0).astype(BF16), v) for (a, _), v in zip(raw, vals)]
        st = [state[h] for h in range(GLA_HEADS)]
        for (c, h, rs, _, vs), t, (_, kv), o_in in zip(units, terms, raw, intra):
            st_ref[h, c] = st[h]
            o_ref[rs, vs] = o_in + _dot_nt(t["qin"].astype(BF16), st[h].astype(BF16))
            st[h] = st[h] * t["dec"] + kv
        for h in range(GLA_HEADS):
            state[h] = st[h]

    return pl.pallas_call(
        body, name=name, grid=(nsteps,),
        in_specs=[qspec, kspec, vspec, gspec],
        out_specs=[ospec, sspec],
        out_shape=[jax.ShapeDtypeStruct((s, GLA_VW), F32),
                   jax.ShapeDtypeStruct((GLA_HEADS, s // GLA_CHUNK, GLA_DV, GLA_DK), F32)],
        scratch_shapes=[pltpu.VMEM((GLA_HEADS, GLA_DV, GLA_DK), F32)],
        compiler_params=_params(("arbitrary",)),
    )(proj, proj, proj, g)


def _gla_bwd(proj, g, do, states, reverse, name, merge=None):
    s = proj.shape[0]
    cb, rows, nsteps, qspec, kspec, vspec, gspec, ospec, sspec = _gla_blockspecs(s, not reverse)
    gla_cols = OFF_Z - OFF_GQ

    def body(q_ref, k_ref, v_ref, g_ref, do_ref, sp_ref, *rest):
        if merge is None:
            dq_ref, dk_ref, dv_ref, dg_ref, dstate = rest
        else:
            dq_o, dk_o, dv_o, dgr_ref, _, dp_ref, dg_ref, dstate = rest
        @pl.when(pl.program_id(0) == 0)
        def _():
            dstate[...] = jnp.zeros_like(dstate)

        allowed, seen_by = _chunk_masks(reverse)
        units = _gla_units(cb, not reverse)
        terms = [_chunk_terms(q_ref, k_ref, g_ref, rs, hs, allowed, reverse) for _, _, rs, hs, _ in units]
        vals = [v_ref[rs, vs].astype(BF16) for _, _, rs, _, vs in units]
        dos = [do_ref[rs, vs] for _, _, rs, _, vs in units]
        prevs = [sp_ref[h, c] for c, h, _, _, _ in units]
        raw = [(_dot_nt(t["qe"].astype(BF16), t["ke"].astype(BF16)), _dot_nt(do, v),
                _dot(do, sp.astype(BF16)), _dot_tn(do, t["qin"].astype(BF16)))
               for t, v, do, sp in zip(terms, vals, dos, prevs)]
        inner = []
        for t, do, (a, da, _, _) in zip(terms, dos, raw):
            da = jnp.where(allowed, da, 0.0).astype(BF16)
            inner.append((_dot(da, t["ke"].astype(BF16)), _dot_tn(da, t["qe"].astype(BF16)),
                          _dot_tn(jnp.where(allowed, a, 0.0).astype(BF16), do)))
        ds = [dstate[h] for h in range(GLA_HEADS)]
        outer = []
        for (c, h, _, _, _), t, v, sp, (_, _, _, inc) in zip(units, terms, vals, prevs, raw):
            ds_b = ds[h].astype(BF16)
            outer.append((_dot(v, ds_b), _dot_nt(t["kst"].astype(BF16), ds_b),
                          jnp.sum(sp * ds[h], axis=0, keepdims=True)))
            ds[h] = ds[h] * t["dec"] + inc
        for h in range(GLA_HEADS):
            dstate[h] = ds[h]
        seen_bf = jnp.where(seen_by, 1.0, 0.0).astype(BF16)
        rowi = lax.broadcasted_iota(jnp.int32, (GLA_CHUNK, GLA_DK), 0)
        for (c, h, rs, hs, vs), t, (_, _, dqin, _), (dqe, dke, dv_in), (dkst, dv_out, ddec) in zip(
                units, terms, raw, inner, outer):
            dq = (dqe * t["e_q"] + dqin * t["e_in"]) * (GLA_DK ** -0.5)
            dk = dke * t["e_k"] + dkst * t["e_st"]
            if merge is None:
                dq_ref[rs, hs], dk_ref[rs, hs], dv_ref[rs, vs] = dq, dk, dv_in + dv_out
            else:
                lo = OFF_GK - OFF_GQ + h * GLA_DK
                dp_ref[rs, hs] = (dq + dq_o[rs, hs]).astype(BF16)
                dp_ref[rs, lo:lo + GLA_DK] = (dk + dk_o[rs, hs]).astype(BF16)
                lo = OFF_GV - OFF_GQ + h * GLA_DV
                dp_ref[rs, lo:lo + GLA_DV] = (dv_in + dv_out + dv_o[rs, vs]).astype(BF16)
            kk = dkst * t["kst"]
            db = dqe * t["qe"] - dke * t["ke"] + dqin * t["qin"] - kk
            extra = jnp.sum(kk, axis=0, keepdims=True) + ddec * t["dec"]
            db = db + jnp.where(rowi == t["last"], extra, 0.0)
            dg_ref[rs, hs] = _dot_exact(seen_bf, db)
        if merge is not None:
            dp_ref[:, OFF_GR - OFF_GQ:gla_cols] = dgr_ref[...]

    scratch = [pltpu.VMEM((GLA_HEADS, GLA_DV, GLA_DK), F32)]
    if merge is None:
        return pl.pallas_call(
            body, name=name, grid=(nsteps,),
            in_specs=[qspec, kspec, vspec, gspec, ospec, sspec],
            out_specs=[gspec, gspec, ospec, gspec],
            out_shape=[jax.ShapeDtypeStruct((s, GLA_KW), F32), jax.ShapeDtypeStruct((s, GLA_KW), F32),
                       jax.ShapeDtypeStruct((s, GLA_VW), F32), jax.ShapeDtypeStruct((s, GLA_KW), F32)],
            scratch_shapes=scratch,
            compiler_params=_params(("arbitrary",)),
        )(proj, proj, proj, g, do, states)
    dproj = merge[4]
    block = gspec.index_map
    return pl.pallas_call(
        body, name=name, grid=(nsteps,),
        in_specs=[qspec, kspec, vspec, gspec, ospec, sspec, gspec, gspec, ospec, ospec, _ANY],
        out_specs=[pl.BlockSpec((rows, gla_cols), lambda n: (block(n)[0], OFF_GQ // gla_cols)), gspec],
        out_shape=[jax.ShapeDtypeStruct(dproj.shape, dproj.dtype), jax.ShapeDtypeStruct((s, GLA_KW), F32)],
        input_output_aliases={10: 0},
        scratch_shapes=scratch,
        compiler_params=_params(("arbitrary",)),
    )(proj, proj, proj, g, do, states, *merge)


def _gla_post(o_f, o_b, proj, g, cat, name="gla_post"):
    s = o_f.shape[0]

    def body(of_ref, ob_ref, gr_ref, g_ref, _, o_ref):
        gv = g_ref[...]
        for h in range(GLA_HEADS):
            sl = slice(h * GLA_DV, (h + 1) * GLA_DV)
            osum = of_ref[:, sl] + ob_ref[:, sl]
            r = lax.rsqrt(jnp.mean(osum * osum, axis=-1, keepdims=True) + EPS)
            gr = gr_ref[:, sl]
            o_ref[:, sl] = (osum * r * gv * (gr * _sigmoid(gr))).astype(BF16)

    blk = pl.BlockSpec((ROW_BLOCK, GLA_VW), lambda i: (i, 0))
    return pl.pallas_call(
        body, name=name, grid=(s // ROW_BLOCK,),
        in_specs=[blk, blk, pl.BlockSpec((ROW_BLOCK, GLA_VW), lambda i: (i, OFF_GR // GLA_VW)),
                  pl.BlockSpec((1, GLA_DV), lambda i: (0, 0)), pl.BlockSpec(memory_space=pl.ANY)],
        out_specs=pl.BlockSpec((ROW_BLOCK, GLA_VW), lambda i: (i, ATTN_W // GLA_VW)),
        out_shape=jax.ShapeDtypeStruct(cat.shape, cat.dtype),
        input_output_aliases={4: 0},
        compiler_params=_params(("parallel",)),
    )(o_f, o_b, proj, g, cat)


def _gla_post_bwd(dcat, o_f, o_b, proj, g, name="gla_post_bwd"):
    s = o_f.shape[0]

    def body(dy_ref, of_ref, ob_ref, gr_ref, g_ref, do_ref, dgr_ref, gg_ref):
        i = pl.program_id(0)
        gv = g_ref[...]
        gg = jnp.zeros((1, GLA_DV), F32)
        for h in range(GLA_HEADS):
            sl = slice(h * GLA_DV, (h + 1) * GLA_DV)
            osum = of_ref[:, sl] + ob_ref[:, sl]
            r = lax.rsqrt(jnp.mean(osum * osum, axis=-1, keepdims=True) + EPS)
            gr, dy = gr_ref[:, sl], dy_ref[:, sl]
            sg = _sigmoid(gr)
            dgr_ref[:, sl] = (dy * (osum * r * gv) * (sg * (1.0 + gr * (1.0 - sg)))).astype(BF16)
            dn = dy * (gr * sg)
            dng = dn * gv
            c = jnp.mean(dng * osum, axis=-1, keepdims=True)
            do_ref[:, sl] = (r * dng - osum * (r * r * r * c)).astype(BF16)
            gg = gg + jnp.sum(dn * osum * r, axis=0, keepdims=True)

        @pl.when(i == 0)
        def _():
            gg_ref[...] = jnp.zeros_like(gg_ref)

        gg_ref[...] += gg

    blk = pl.BlockSpec((ROW_BLOCK, GLA_VW), lambda i: (i, 0))
    vec = pl.BlockSpec((1, GLA_DV), lambda i: (0, 0))
    return pl.pallas_call(
        body, name=name, grid=(s // ROW_BLOCK,),
        in_specs=[pl.BlockSpec((ROW_BLOCK, GLA_VW), lambda i: (i, 1)), blk, blk,
                  pl.BlockSpec((ROW_BLOCK, GLA_VW), lambda i: (i, OFF_GR // GLA_VW)), vec],
        out_specs=[blk, blk, vec],
        out_shape=[jax.ShapeDtypeStruct((s, GLA_VW), BF16), jax.ShapeDtypeStruct((s, GLA_VW), BF16),
                   jax.ShapeDtypeStruct((1, GLA_DV), F32)],
        compiler_params=_params(("arbitrary",)),
    )(dcat, o_f, o_b, proj, g)


HALO = 16


def _extended(prev_ref, cur_ref, next_ref, i, s, tr, cs):
    first, last = i == 0, i == s // tr - 1
    prev = jnp.where(first, 0.0, prev_ref[:, cs].astype(F32))
    nxt = jnp.where(last, 0.0, next_ref[:, cs].astype(F32))
    return jnp.concatenate([prev, cur_ref[:, cs].astype(F32), nxt], axis=0)


FFN_ROWS = 512
FFN_COLS = 512


FFN_CHUNK = 256


def _lagged(i, ni, multiply, finish, rotate, init):
    chunks = [slice(c, c + FFN_CHUNK) for c in range(0, FFN_COLS, FFN_CHUNK)]

    @pl.when(i == 0)
    def _():
        init()

    @pl.when(i < 2)
    def _():
        rotate([multiply(cs) for cs in chunks], chunks)

    @pl.when((i >= 2) & (i < ni))
    def _():
        new = []
        for cs in chunks:
            new.append(multiply(cs))
            finish(cs)
        rotate(new, chunks)

    @pl.when(i >= ni)
    def _():
        for cs in chunks:
            finish(cs)
        rotate(None, chunks)


def _ffn_in(n2, w_gate, w_up, conv_w, conv_b, name="ffn_in"):
    s, d = n2.shape
    f = w_gate.shape[1]
    tm, tn, edge = FFN_ROWS, FFN_COLS, SUBLANES
    ni = s // tm
    ext = tm + 2 * edge

    def body(a_ref, wg_ref, wu_ref, w_ref, b_ref, gate_ref, up_ref, act_ref, g_near, g_far, u_near, u_far, g_tail):
        i = pl.program_id(1)

        def multiply(cs):
            a = a_ref[...]
            return _dot(a, wg_ref[:, cs]), _dot(a, wu_ref[:, cs]).astype(BF16)

        def finish(cs):
            g_old, u_old = g_far[:, cs], u_far[:, cs]
            before = jnp.where(i == 2, 0.0, g_tail[:, cs])
            after = jnp.where(i == ni + 1, 0.0, g_near[0:edge, cs])
            ge = jnp.concatenate([before, g_old, after], axis=0)
            w = w_ref[:, cs]
            conv = (w[0:1] * pltpu.roll(ge, 1, 0) + w[1:2] * ge + w[2:3] * pltpu.roll(ge, ext - 1, 0))[edge:edge + tm]
            conv = conv + b_ref[:, cs]
            gate_ref[:, cs] = g_old
            up_ref[:, cs] = u_old
            act_ref[:, cs] = (conv * _sigmoid(conv) * u_old.astype(F32)).astype(BF16)

        def rotate(new, chunks):
            g_tail[...] = g_far[tm - edge:tm]
            g_far[...] = g_near[...]
            u_far[...] = u_near[...]
            if new is not None:
                for cs, (g_new, u_new) in zip(chunks, new):
                    g_near[:, cs], u_near[:, cs] = g_new, u_new

        def init():
            for r in (g_near, g_far, u_near, u_far, g_tail):
                r[...] = jnp.zeros_like(r)

        _lagged(i, ni, multiply, finish, rotate, init)

    lag = pl.BlockSpec((tm, tn), lambda j, i: (jnp.maximum(i - 2, 0), j))
    return pl.pallas_call(
        body, name=name, grid=(f // tn, ni + 2),
        in_specs=[pl.BlockSpec((tm, d), lambda j, i: (jnp.minimum(i, ni - 1), 0)),
                  pl.BlockSpec((d, tn), lambda j, i: (0, j)), pl.BlockSpec((d, tn), lambda j, i: (0, j)),
                  pl.BlockSpec((3, tn), lambda j, i: (0, j)), pl.BlockSpec((1, tn), lambda j, i: (0, j))],
        out_specs=[lag, lag, lag],
        out_shape=[jax.ShapeDtypeStruct((s, f), F32), jax.ShapeDtypeStruct((s, f), BF16),
                   jax.ShapeDtypeStruct((s, f), BF16)],
        scratch_shapes=[pltpu.VMEM((tm, tn), F32), pltpu.VMEM((tm, tn), F32), pltpu.VMEM((tm, tn), BF16),
                        pltpu.VMEM((tm, tn), BF16), pltpu.VMEM((edge, tn), F32)],
        compiler_params=_params(("parallel", "arbitrary")),
    )(n2, w_gate, w_up, conv_w, conv_b)


def _ffn_mid_bwd(dh2, w_down, gate, up, conv_w, conv_b, name="ffn_mid_bwd"):
    s, d = dh2.shape
    f = gate.shape[1]
    tm, tn = FFN_ROWS, FFN_COLS
    ni = s // tm
    ext = tm + 2 * HALO
    per, last_halo = tm // HALO, s // HALO - 1

    def body(a_ref, wd_ref, gp, gc, gn, upp, upc, upn, w_ref, b_ref, dg_ref, du_ref, gw_ref, gb_ref,
             d_near, d_far, d_tail):
        i = pl.program_id(1)

        def multiply(cs):
            return _dot_nt(a_ref[...], wd_ref[cs, :])

        def finish(cs):
            before = jnp.where(i == 2, 0.0, d_tail[:, cs])
            after = jnp.where(i == ni + 1, 0.0, d_near[0:HALO, cs])
            de = jnp.concatenate([before, d_far[:, cs], after], axis=0)
            ge = _extended(gp, gc, gn, i - 2, s, tm, cs)
            ue = _extended(upp, upc, upn, i - 2, s, tm, cs)
            w = w_ref[:, cs]
            g_prev, g_next = pltpu.roll(ge, 1, 0), pltpu.roll(ge, ext - 1, 0)
            conv = w[0:1] * g_prev + w[1:2] * ge + w[2:3] * g_next + b_ref[:, cs]
            sg = _sigmoid(conv)
            inner = slice(HALO, HALO + tm)
            du_ref[:, cs] = (de * (conv * sg))[inner].astype(BF16)
            dconv = de * ue * (sg * (1.0 + conv * (1.0 - sg)))
            dgate = w[0:1] * pltpu.roll(dconv, ext - 1, 0) + w[1:2] * dconv + w[2:3] * pltpu.roll(dconv, 1, 0)
            dg_ref[:, cs] = dgate[inner].astype(BF16)
            dci = dconv[inner]
            gw_ref[0:1, cs] += jnp.sum(dci * g_prev[inner], axis=0, keepdims=True)
            gw_ref[1:2, cs] += jnp.sum(dci * ge[inner], axis=0, keepdims=True)
            gw_ref[2:3, cs] += jnp.sum(dci * g_next[inner], axis=0, keepdims=True)
            gb_ref[:, cs] += jnp.sum(dci, axis=0, keepdims=True)

        def rotate(new, chunks):
            d_tail[...] = d_far[tm - HALO:tm]
            d_far[...] = d_near[...]
            if new is not None:
                for cs, d_new in zip(chunks, new):
                    d_near[:, cs] = d_new

        def init():
            for r in (d_near, d_far, d_tail, gw_ref, gb_ref):
                r[...] = jnp.zeros_like(r)

        _lagged(i, ni, multiply, finish, rotate, init)

    def tile(i):
        return jnp.maximum(i - 2, 0)

    cur = pl.BlockSpec((tm, tn), lambda j, i: (tile(i), j))
    prev = pl.BlockSpec((HALO, tn), lambda j, i: (jnp.maximum(tile(i) * per - 1, 0), j))
    nxt = pl.BlockSpec((HALO, tn), lambda j, i: (jnp.minimum((tile(i) + 1) * per, last_halo), j))
    wspec = pl.BlockSpec((3, tn), lambda j, i: (0, j))
    bspec = pl.BlockSpec((1, tn), lambda j, i: (0, j))
    return pl.pallas_call(
        body, name=name, grid=(f // tn, ni + 2),
        in_specs=[pl.BlockSpec((tm, d), lambda j, i: (jnp.minimum(i, ni - 1), 0)),
                  pl.BlockSpec((tn, d), lambda j, i: (j, 0))] + [prev, cur, nxt] * 2 + [wspec, bspec],
        out_specs=[cur, cur, wspec, bspec],
        out_shape=[jax.ShapeDtypeStruct((s, f), BF16), jax.ShapeDtypeStruct((s, f), BF16),
                   jax.ShapeDtypeStruct((3, f), F32), jax.ShapeDtypeStruct((1, f), F32)],
        scratch_shapes=[pltpu.VMEM((tm, tn), F32), pltpu.VMEM((tm, tn), F32), pltpu.VMEM((HALO, tn), F32)],
        compiler_params=_params(("parallel", "arbitrary")),
    )(dh2, w_down, gate, gate, gate, up, up, up, conv_w, conv_b)


def _local_step(x, target, w, late_weights=None, grad_sink=None, first_dep=()):
    s = x.shape[0]
    tables = _rope_tables(s)
    uf, ub = _gate_matrices(w["gf_up"], w["gb_up"])
    if grad_sink is None:
        grad_sink = lambda names, grads: ()

    n1 = _rms_fwd(x, w["norm1_g"], "norm1")
    proj = _matmul([(n1, w["w_in"])], "nn", F32, 1024, 1280, D_MODEL, "in_proj", deps=first_dep)
    qkv = _rope_fwd(proj, tables)
    branches = [_attn_fwd(*qkv[di], d, f"attn_fwd_d{d}") for di, d in enumerate(DILATIONS)]
    o_mix, ao, lse = _attn_combine([b[0] for b in branches], [b[1] for b in branches], w["attn_norm_g"])
    g_f, g_b = _gla_gates(proj, uf, ub, w["gf_b"], w["gb_b"])
    o_f, st_f = _gla_fwd(proj, g_f, False, "gla_fwd_f")
    o_b, st_b = _gla_fwd(proj, g_b, True, "gla_fwd_b")
    cat = _gla_post(o_f, o_b, proj, w["gla_norm_g"], ao)
    if late_weights is not None:
        w = {**w, **late_weights(cat)}
    h1 = _matmul([(cat, w["w_out"])], "nn", F32, 512, 1024, D_MODEL, "out_proj", res=x)
    n2 = _rms_fwd(h1, w["norm2_g"], "norm2")
    gate, up, act = _ffn_in(n2, w["w_gate"], w["w_up"], w["conv_w"], w["conv_b"])
    h2 = _matmul([(act, w["w_down"])], "nn", F32, 1024, 1024, 2816, "ffn_down", res=h1)
    dh2, dh2_b, loss_acc, g_final = _final_loss(h2, target, w["final_norm_g"])

    g_w_down = _matmul([(act, dh2_b)], "tn", F32, 1408, 1024, 2048, "g_w_down")
    dep = grad_sink(["w_down"], [g_w_down])
    dgate, dup, g_conv_w, g_conv_b = _ffn_mid_bwd(dh2_b, w["w_down"], gate, up, w["conv_w"], w["conv_b"])
    g_w_gate = _matmul([(n2, dgate)], "tn", F32, 2048, 512, 2048, "g_w_gate", deps=dep)
    g_w_up = _matmul([(n2, dup)], "tn", F32, 2048, 512, 2048, "g_w_up")
    dep = grad_sink(["w_gate", "w_up"], [g_w_gate, g_w_up])
    dn2 = _matmul([(dgate, w["w_gate"])], "nt", F32, 1024, 1024, 2816, "d_n2_gate", deps=dep)
    dn2 = _matmul([(dup, w["w_up"])], "nt", F32, 1024, 1024, 2816, "d_n2_up", res=dn2)
    dh1, dh1_b, g_norm2 = _rms_bwd(dn2, h1, w["norm2_g"], dh2, "norm2_bwd")

    g_w_out = _matmul([(cat, dh1_b)], "tn", F32, 1024, 1024, 2048, "g_w_out")
    dep = grad_sink(["w_out"], [g_w_out])
    dcat = _matmul([(dh1_b, w["w_out"])], "nt", F32, 512, 1024, D_MODEL, "d_cat", deps=dep)
    do_attn, delta, g_attn_norm = _attn_prebwd(dcat, o_mix, w["attn_norm_g"])
    grads = [_attn_bwd(*qkv[di], do_attn[di], lse[di], delta[di], d, f"attn_bwd_d{d}")
             for di, d in enumerate(DILATIONS)]
    dproj = _rope_bwd(grads, tables)
    do_gla, dgr, g_gla_norm = _gla_post_bwd(dcat, o_f, o_b, proj, w["gla_norm_g"])
    dq_f, dk_f, dv_f, dg_f = _gla_bwd(proj, g_f, do_gla, st_f, False, "gla_bwd_f")
    dproj, dg_b = _gla_bwd(proj, g_b, do_gla, st_b, True, "gla_bwd_b", merge=(dq_f, dk_f, dv_f, dgr, dproj))
    dproj, g_uf, g_ub, g_gf_b, g_gb_b = _gla_gates_bwd(dg_f, dg_b, proj, uf, ub, w["gf_b"], w["gb_b"], dproj)
    g_w_in = _matmul([(n1, dproj)], "tn", F32, 1024, 1280, 2048, "g_w_in")
    dep = grad_sink(["w_in"], [g_w_in])
    dn1 = _matmul([(dproj, w["w_in"])], "nt", F32, 1024, 2048, 1280, "d_n1", deps=dep)
    grad_x, _, g_norm1 = _rms_bwd(dn1, x, w["norm1_g"], dh1, "norm1_bwd")

    g = dict(norm1_g=g_norm1, w_in=g_w_in, gf_up=g_uf[:GLA_RANK], gf_b=g_gf_b,
             gb_up=g_ub[GLA_RANK:2 * GLA_RANK], gb_b=g_gb_b, gla_norm_g=g_gla_norm, attn_norm_g=g_attn_norm,
             w_out=g_w_out, norm2_g=g_norm2, w_gate=g_w_gate, w_up=g_w_up, conv_w=g_conv_w, conv_b=g_conv_b,
             w_down=g_w_down, final_norm_g=g_final)
    return loss_acc, grad_x, g


def _me_and_peers():
    x, y, c = lax.axis_index("x"), lax.axis_index("y"), lax.axis_index("c")
    me = 4 * x + 2 * y + c
    peers = []
    for kbits in range(1, N_DEV):
        px, py, pc = x ^ (kbits >> 2 & 1), y ^ (kbits >> 1 & 1), c ^ (kbits & 1)
        peers.append(((px, py, pc), 4 * px + 2 * py + pc))
    return me, peers


_HBM = pl.BlockSpec(memory_space=pltpu.HBM)
_SEM = pl.BlockSpec(memory_space=pltpu.SEMAPHORE)
_ANY = pl.BlockSpec(memory_space=pl.ANY)
_EFFECT = pltpu.SideEffectType.DATAFLOW_SIDE_EFFECTING


def _exchange_copies(src_refs, land_refs, send_sems, recv_sems, scatter):
    me, peers = _me_and_peers()
    out = []
    for a, (src, land) in enumerate(zip(src_refs, land_refs)):
        for kk, (dev, idx) in enumerate(peers):
            out.append(pltpu.make_async_remote_copy(
                src_ref=src.at[idx] if scatter else src, dst_ref=land.at[me],
                send_sem=send_sems.at[a * (N_DEV - 1) + kk], recv_sem=recv_sems.at[a * (N_DEV - 1) + kk],
                device_id=dev, device_id_type=MESH_ID))
    return out


def _exchange_start(srcs, lands, scatter, name, deps=()):
    n, nd = len(srcs), len(deps)

    def body(*refs):
        src_refs, land_refs = refs[:n], refs[n:2 * n]
        send_sems, recv_sems = refs[2 * n + nd:2 * n + nd + 2]
        token = refs[-1]
        for cp in _exchange_copies(src_refs, land_refs, send_sems, recv_sems, scatter):
            cp.start()
        token[...] = jnp.zeros_like(token)

    outs = pl.pallas_call(
        body, name=name,
        in_specs=[_HBM] * (2 * n) + [_ANY] * nd,
        out_specs=[_SEM, _SEM] + [_HBM] * (2 * n) + [pl.BlockSpec(memory_space=pltpu.VMEM)],
        out_shape=[pltpu.SemaphoreType.DMA((n * (N_DEV - 1),)), pltpu.SemaphoreType.DMA((n * (N_DEV - 1),))]
        + [pltpu.HBM(t.shape, t.dtype) for t in srcs] + [pltpu.HBM(t.shape, t.dtype) for t in lands]
        + [jax.ShapeDtypeStruct((SUBLANES, LANES), F32)],
        input_output_aliases={i: 2 + i for i in range(2 * n)},
        compiler_params=pltpu.CompilerParams(has_side_effects=_EFFECT),
    )(*[pltpu.with_memory_space_constraint(t, pltpu.HBM) for t in list(srcs) + list(lands)], *deps)
    send_sems, recv_sems = outs[0], outs[1]
    return dict(send=send_sems, recv=recv_sems, srcs=outs[2:2 + n], lands=outs[2 + n:2 + 2 * n],
                scatter=scatter, token=outs[-1])


def _exchange_wait(started, name, after):
    n = len(started["srcs"])
    scatter = started["scatter"]

    def body(*refs):
        src_refs, land_refs = refs[:n], refs[n:2 * n]
        send_sems, recv_sems = refs[2 * n], refs[2 * n + 1]
        for cp in _exchange_copies(src_refs, land_refs, send_sems, recv_sems, scatter):
            cp.wait_send()
            cp.wait_recv()

    outs = pl.pallas_call(
        body, name=name,
        in_specs=[_HBM] * (2 * n) + [_SEM, _SEM, _ANY],
        out_specs=[_HBM] * (2 * n),
        out_shape=[pltpu.HBM(t.shape, t.dtype) for t in started["srcs"]]
        + [pltpu.HBM(t.shape, t.dtype) for t in started["lands"]],
        input_output_aliases={i: i for i in range(2 * n)},
        compiler_params=pltpu.CompilerParams(has_side_effects=_EFFECT),
    )(*started["srcs"], *started["lands"], started["send"], started["recv"], after)
    return outs[:n], outs[n:]


def _all_gather_two_level(shard, name):
    def body(x_ref, out_ref, send_sems, recv_sems, local_sem):
        x, y, c = lax.axis_index("x"), lax.axis_index("y"), lax.axis_index("c")
        me, sibling = (x, y, c), (x, y, 1 - c)
        chips = [(1 - x, y), (x, 1 - y), (1 - x, 1 - y)]

        def slot(px, py, pc):
            return out_ref.at[4 * px + 2 * py + pc]

        def copy(k, block, to, src=None):
            return pltpu.make_async_remote_copy(
                src_ref=slot(*block) if src is None else src, dst_ref=slot(*block),
                send_sem=send_sems.at[k], recv_sem=recv_sems.at[k], device_id=to, device_id_type=MESH_ID)

        mine = pltpu.make_async_copy(x_ref, slot(*me), local_sem)
        mine.start()
        first = [copy(0, me, sibling, src=x_ref)]
        first += [copy(1 + j, me, (*chip, c), src=x_ref) for j, chip in enumerate(chips)]
        for cp in first:
            cp.start()
        passed = [copy(4 + j, (*chip, c), sibling) for j, chip in enumerate(chips)]
        for j, chip in enumerate(chips):
            copy(1 + j, (*chip, c), me).wait_recv()
            passed[j].start()
        copy(0, sibling, me).wait_recv()
        for j, chip in enumerate(chips):
            copy(4 + j, (*chip, 1 - c), me).wait_recv()
        for cp in first + passed:
            cp.wait_send()
        mine.wait()

    return pl.pallas_call(
        body, name=name,
        in_specs=[_ANY], out_specs=_ANY,
        out_shape=jax.ShapeDtypeStruct((N_DEV,) + shard.shape, shard.dtype),
        scratch_shapes=[pltpu.SemaphoreType.DMA((N_DEV - 1,)), pltpu.SemaphoreType.DMA((N_DEV - 1,)),
                        pltpu.SemaphoreType.DMA],
    )(shard)


def _all_gather_vmem(vec, name):
    r = vec.shape[0]

    def body(v_ref, o_ref, send_sems, recv_sems):
        me, peers = _me_and_peers()
        o_ref[me] = v_ref[...]
        sends = []
        for kk, (dev, _) in enumerate(peers):
            cp = pltpu.make_async_remote_copy(
                src_ref=v_ref, dst_ref=o_ref.at[me],
                send_sem=send_sems.at[kk], recv_sem=recv_sems.at[kk],
                device_id=dev, device_id_type=MESH_ID)
            cp.start()
            sends.append(cp)
        for kk, (dev, idx) in enumerate(peers):
            pltpu.make_async_remote_copy(
                src_ref=v_ref, dst_ref=o_ref.at[idx],
                send_sem=send_sems.at[kk], recv_sem=recv_sems.at[kk],
                device_id=dev, device_id_type=MESH_ID).wait_recv()
        for cp in sends:
            cp.wait_send()

    return pl.pallas_call(
        body, name=name,
        in_specs=[pl.BlockSpec(memory_space=pltpu.VMEM)],
        out_specs=pl.BlockSpec(memory_space=pltpu.VMEM),
        out_shape=jax.ShapeDtypeStruct((N_DEV, r, LANES), F32),
        scratch_shapes=[pltpu.SemaphoreType.DMA((N_DEV - 1,)), pltpu.SemaphoreType.DMA((N_DEV - 1,))],
        compiler_params=pltpu.CompilerParams(vmem_limit_bytes=VMEM_LIMIT),
    )(vec)


def _adamw_math(w, g, m, v):
    m = ADAM_B1 * m + (1.0 - ADAM_B1) * g
    v = ADAM_B2 * v + (1.0 - ADAM_B2) * (g * g)
    m_hat = m / (1.0 - ADAM_B1 ** ADAM_STEP)
    v_hat = v / (1.0 - ADAM_B2 ** ADAM_STEP)
    delta = -ADAM_LR * (m_hat / (jnp.sqrt(v_hat) + ADAM_EPS) + ADAM_WD * w)
    return delta, m, v


def _adamw_sum(parts, w, m, v, tr, name, own=None, me=None):
    r, c = w.shape

    def body(*refs):
        if own is None:
            p_ref, w_ref, m_ref, v_ref, g_ref, d_ref, nm_ref, nv_ref = refs
            terms = [p_ref[kk] for kk in range(N_DEV)]
        else:
            me_ref, p_ref, own_ref, w_ref, m_ref, v_ref, g_ref, d_ref, nm_ref, nv_ref = refs
            terms = [jnp.where(me_ref[0] == kk, own_ref[0], p_ref[kk]).astype(F32) for kk in range(N_DEV)]
        g = terms[0]
        for t in terms[1:]:
            g = g + t
        g_ref[...] = g
        d_ref[...], nm_ref[...], nv_ref[...] = _adamw_math(w_ref[...], g, m_ref[...], v_ref[...])

    out_shape = [jax.ShapeDtypeStruct((r, c), F32)] * 4
    if own is None:
        blk = pl.BlockSpec((tr, c), lambda i: (i, 0))
        return pl.pallas_call(
            body, name=name, grid=(r // tr,),
            in_specs=[pl.BlockSpec((N_DEV, tr, c), lambda i: (0, i, 0)), blk, blk, blk],
            out_specs=[blk] * 4, out_shape=out_shape,
            compiler_params=_params(("parallel",)),
        )(parts, w, m, v)
    blk = pl.BlockSpec((tr, c), lambda i, me_ref: (i, 0))
    return pl.pallas_call(
        body, name=name,
        grid_spec=pltpu.PrefetchScalarGridSpec(
            num_scalar_prefetch=1, grid=(r // tr,),
            in_specs=[pl.BlockSpec((N_DEV, tr, c), lambda i, me_ref: (0, i, 0)),
                      pl.BlockSpec((1, tr, c), lambda i, me_ref: (me_ref[0], i, 0)), blk, blk, blk],
            out_specs=[blk] * 4),
        out_shape=out_shape,
        compiler_params=_params(("parallel",)),
    )(jnp.reshape(me, (1,)).astype(jnp.int32), parts, own, w, m, v)


_SMALL = ("norm1_g", "gf_b", "gb_b", "gla_norm_g", "attn_norm_g", "norm2_g", "conv_b", "final_norm_g",
          "gf_up", "gb_up", "conv_w")


def _pack(named):
    flat = jnp.concatenate([jnp.ravel(t).astype(F32) for t in named])
    tile = SUBLANES * LANES
    total = -(-flat.shape[0] // tile) * tile
    return jnp.pad(flat, (0, total - flat.shape[0])).reshape(total // LANES, LANES)


def _unpack(packed, shapes):
    flat = packed.reshape(-1)
    out, off = [], 0
    for shp in shapes:
        size = int(np.prod(shp))
        out.append(flat[off:off + size].reshape(shp))
        off += size
    return out


def kernel(x, norm1_g, w_in, gf_up, gf_b, gb_up, gb_b, gla_norm_g, attn_norm_g, w_out, norm2_g, w_gate, w_up, conv_w, conv_b, w_down, final_norm_g, loss_target, m_norm1_g, m_w_in, m_gf_up, m_gf_b, m_gb_up, m_gb_b, m_gla_norm_g, m_attn_norm_g, m_w_out, m_norm2_g, m_w_gate, m_w_up, m_conv_w, m_conv_b, m_w_down, m_final_norm_g, v_norm1_g, v_w_in, v_gf_up, v_gf_b, v_gb_up, v_gb_b, v_gla_norm_g, v_attn_norm_g, v_w_out, v_norm2_g, v_w_gate, v_w_up, v_conv_w, v_conv_b, v_w_down, v_final_norm_g):
    names = ("norm1_g", "w_in", "gf_up", "gf_b", "gb_up", "gb_b", "gla_norm_g", "attn_norm_g", "w_out", "norm2_g",
             "w_gate", "w_up", "conv_w", "conv_b", "w_down", "final_norm_g")
    ws = dict(zip(names, (norm1_g, w_in, gf_up, gf_b, gb_up, gb_b, gla_norm_g, attn_norm_g, w_out, norm2_g,
                          w_gate, w_up, conv_w, conv_b, w_down, final_norm_g)))
    ms = dict(zip(names, (m_norm1_g, m_w_in, m_gf_up, m_gf_b, m_gb_up, m_gb_b, m_gla_norm_g, m_attn_norm_g, m_w_out,
                          m_norm2_g, m_w_gate, m_w_up, m_conv_w, m_conv_b, m_w_down, m_final_norm_g)))
    vs = dict(zip(names, (v_norm1_g, v_w_in, v_gf_up, v_gf_b, v_gb_up, v_gb_b, v_gla_norm_g, v_attn_norm_g, v_w_out,
                          v_norm2_g, v_w_gate, v_w_up, v_conv_w, v_conv_b, v_w_down, v_final_norm_g)))
    me = 4 * lax.axis_index("x") + 2 * lax.axis_index("y") + lax.axis_index("c")
    big = ("w_in", "w_out", "w_gate", "w_up", "w_down")
    col_sharded = ("w_in", "w_gate", "w_up")

    def gather_start(group, name, deps=()):
        shards = [ws[n][0].astype(BF16) for n in group]
        lands = [lax.empty((N_DEV,) + t.shape, BF16) for t in shards]
        return _exchange_start(shards, lands, False, name, deps)

    def gather_finish(group, started, name, after):
        full = {}
        for n, own, t in zip(group, *_exchange_wait(started, name, after)):
            t = lax.dynamic_update_slice(t, own[None], (me, 0, 0))
            if n in col_sharded:
                full[n] = jnp.transpose(t, (1, 0, 2)).reshape(t.shape[1], N_DEV * t.shape[2])
            else:
                full[n] = t.reshape(N_DEV * t.shape[1], t.shape[2])
        return full

    w_in_all = _all_gather_two_level(ws["w_in"][0].astype(BF16), "gather_w_in")
    full = {"w_in": jnp.pad(jnp.transpose(w_in_all, (1, 0, 2)).reshape(D_MODEL, IN_WIDTH),
                            ((0, 0), (0, IN_PAD - IN_WIDTH)))}
    late = ("w_out", "w_gate", "w_up", "w_down")
    started_b = gather_start(late, "gather_late_start", deps=(full["w_in"],))

    def late_weights(after):
        return gather_finish(late, started_b, "gather_late_wait", after)

    small_sharded = ("gf_up", "gb_up", "conv_w")
    sm = _all_gather_vmem(_pack([ws[n][0] for n in small_sharded]), "gather_small")
    shard_shapes = [ws[n][0].shape for n in small_sharded]
    per_dev = [_unpack(sm[d], shard_shapes) for d in range(N_DEV)]
    for i, n in enumerate(small_sharded):
        full[n] = jnp.concatenate([per_dev[d][i] for d in range(N_DEV)], axis=1)
    for n in ("norm1_g", "gf_b", "gb_b", "gla_norm_g", "attn_norm_g", "norm2_g", "conv_b"):
        full[n] = ws[n]
    full["final_norm_g"] = final_norm_g.reshape(1, D_MODEL)

    in_flight = []

    def grad_sink(group, grads):
        partials = []
        for n, t in zip(group, grads):
            if n == "w_in":
                t = t[:, :IN_WIDTH].astype(BF16)
            if n in col_sharded:
                t = jnp.transpose(t.reshape(t.shape[0], N_DEV, t.shape[1] // N_DEV), (1, 0, 2))
            else:
                t = t.reshape(N_DEV, t.shape[0] // N_DEV, t.shape[1])
            partials.append(t)
        lands = [lax.empty(t.shape, t.dtype) for t in partials]
        started = _exchange_start(partials, lands, True, "exchange_" + "_".join(group) + "_start")
        in_flight.append((group, started))
        return (started["token"],)

    loss_acc, grad_x, g = _local_step(x[0], loss_target[0], full, late_weights, grad_sink,
                                      first_dep=(started_b["token"],))

    out = {}
    for group, started in in_flight:
        sent, landed = _exchange_wait(started, "exchange_" + "_".join(group) + "_wait", grad_x)
        for n, parts, own in zip(group, landed, sent):
            out[n] = _adamw_sum(parts, ws[n][0], ms[n][0], vs[n][0], 64, "adamw_" + n, own=own, me=me)

    small_full_shapes = [g[n].shape for n in _SMALL]
    gsmall = _pack([g[n] for n in _SMALL] + [loss_acc[0:1, 0:1]])
    gathered_small = _all_gather_vmem(gsmall, "gather_small_grads")

    def full_small(d):
        parts = []
        for n in _SMALL:
            t = d[n].reshape(d[n].shape[-2:]) if d[n].ndim == 3 else d[n].reshape(1, -1)
            if n in small_sharded:
                wide = jnp.zeros((t.shape[0], t.shape[1] * N_DEV), F32)
                t = lax.dynamic_update_slice_in_dim(wide, t, me * t.shape[1], axis=1)
            parts.append(t)
        return _pack(parts + [jnp.zeros((1, 1), F32)])

    rows = gsmall.shape[0]
    res_small = _adamw_sum(gathered_small, full_small(ws), full_small(ms), full_small(vs), rows, "adamw_small")
    loss = res_small[0].reshape(-1)[sum(int(np.prod(sh)) for sh in small_full_shapes)]
    unpacked = [_unpack(t, small_full_shapes) for t in res_small]
    for i, n in enumerate(_SMALL):
        vals = [u[i] for u in unpacked]
        if n in small_sharded:
            width = vals[0].shape[1] // N_DEV
            vals = [lax.dynamic_slice_in_dim(t, me * width, width, axis=1) for t in vals]
        out[n] = vals

    result = [loss, grad_x[None]]
    for kind in range(4):
        for n in names:
            result.append(out[n][kind].reshape(ws[n].shape))
    return tuple(result)
```

```python
import functools

import numpy as np
import jax
import jax.numpy as jnp
from jax import lax
from jax.experimental import pallas as pl
from jax.experimental.pallas import tpu as pltpu

F32 = jnp.float32
BF16 = jnp.bfloat16

D_MODEL = 2048
ATTN_W = 1024
ATTN_HEADS = 8
HEAD_DIM = 128
ROPE_DIM = 32
ROPE_THETA = 500000.0
DILATIONS = (1, 4, 16)
N_SIDE = 64
GLA_KW = 512
GLA_VW = 1024
GLA_HEADS = 4
GLA_DK = 128
GLA_DV = 256
GLA_RANK = 16
GLA_GATE_NORM = 16.0
GLA_CHUNK = 64
IN_WIDTH = 6176
IN_PAD = 6400
D_FF = 5632
EPS = 1e-6
N_DEV = 8

OFF_AQ, OFF_AK, OFF_AV = 0, 1024, 2048
OFF_GQ, OFF_GK, OFF_GV, OFF_GR, OFF_Z = 3072, 3584, 4096, 5120, 6144

ADAM_LR, ADAM_B1, ADAM_B2, ADAM_EPS, ADAM_WD, ADAM_STEP = 0.001, 0.9, 0.999, 1e-08, 0.01, 10

LANES = 128
SUBLANES = 8
VMEM_LIMIT = 56 * 1024 * 1024
ROW_BLOCK = 256
ATTN_BLOCK = 128
GLA_CHUNKS_PER_STEP = 4
NEG = -1e30
MESH_ID = pl.DeviceIdType.MESH


def _params(sem):
    return pltpu.CompilerParams(dimension_semantics=sem, vmem_limit_bytes=VMEM_LIMIT)


def _dot(a, b):
    return lax.dot_general(a, b, (((1,), (0,)), ((), ())), preferred_element_type=F32)


def _dot_nt(a, b):
    return lax.dot_general(a, b, (((1,), (1,)), ((), ())), preferred_element_type=F32)


def _dot_tn(a, b):
    return lax.dot_general(a, b, (((0,), (0,)), ((), ())), preferred_element_type=F32)


def _sigmoid(x):
    return 0.5 * jnp.tanh(0.5 * x) + 0.5


def _matmul(pairs, mode, out_dtype, tm, tn, tk, name, res=None, deps=()):
    a0, b0 = pairs[0]
    if mode == "nn":
        (m, kdim), n = a0.shape, b0.shape[1]
    elif mode == "nt":
        (m, kdim), n = a0.shape, b0.shape[0]
    else:
        (kdim, m), n = a0.shape, b0.shape[1]
    assert m % tm == 0 and n % tn == 0 and kdim % tk == 0, (name, m, n, kdim)
    nk = kdim // tk
    npairs = len(pairs)
    steps = nk * npairs
    dot = {"nn": _dot, "nt": _dot_nt, "tn": _dot_tn}[mode]

    def kidx(p):
        return lambda k: jnp.clip(k - p * nk, 0, nk - 1)

    in_specs, args = [], []
    for p, (a, b) in enumerate(pairs):
        kk = kidx(p)
        if mode == "nn":
            in_specs += [pl.BlockSpec((tm, tk), lambda i, j, k, kk=kk: (i, kk(k))),
                         pl.BlockSpec((tk, tn), lambda i, j, k, kk=kk: (kk(k), j))]
        elif mode == "nt":
            in_specs += [pl.BlockSpec((tm, tk), lambda i, j, k, kk=kk: (i, kk(k))),
                         pl.BlockSpec((tn, tk), lambda i, j, k, kk=kk: (j, kk(k)))]
        else:
            in_specs += [pl.BlockSpec((tk, tm), lambda i, j, k, kk=kk: (kk(k), i)),
                         pl.BlockSpec((tk, tn), lambda i, j, k, kk=kk: (kk(k), j))]
        args += [a, b]
    if res is not None:
        in_specs.append(pl.BlockSpec((tm, tn), lambda i, j, k: (i, j)))
        args.append(res)
    in_specs += [pl.BlockSpec(memory_space=pl.ANY)] * len(deps)
    args += list(deps)

    def body(*refs):
        ab = refs[:2 * npairs]
        res_ref = refs[2 * npairs] if res is not None else None
        o_ref = refs[2 * npairs + (1 if res is not None else 0) + len(deps)]

        def finish(acc):
            if res_ref is not None:
                acc = acc + res_ref[...]
            o_ref[...] = acc.astype(out_dtype)

        if steps == 1:
            finish(dot(ab[0][...], ab[1][...]))
            return
        acc_ref = refs[-1]
        k = pl.program_id(2)

        @pl.when(k == 0)
        def _():
            acc_ref[...] = jnp.zeros_like(acc_ref)

        for p in range(npairs):
            @pl.when((k >= p * nk) & (k < (p + 1) * nk))
            def _(p=p):
                acc_ref[...] += dot(ab[2 * p][...], ab[2 * p + 1][...])

        @pl.when(k == steps - 1)
        def _():
            finish(acc_ref[...])

    return pl.pallas_call(
        body, name=name,
        grid=(m // tm, n // tn, steps),
        in_specs=in_specs,
        out_specs=pl.BlockSpec((tm, tn), lambda i, j, k: (i, j)),
        out_shape=jax.ShapeDtypeStruct((m, n), out_dtype),
        scratch_shapes=[] if steps == 1 else [pltpu.VMEM((tm, tn), F32)],
        compiler_params=_params(("parallel", "parallel", "arbitrary")),
    )(*args)


def _rms_fwd(x, g, name):
    s, d = x.shape

    def body(x_ref, g_ref, o_ref):
        xv = x_ref[...]
        r = lax.rsqrt(jnp.mean(xv * xv, axis=-1, keepdims=True) + EPS)
        o_ref[...] = (xv * r * g_ref[...]).astype(BF16)

    return pl.pallas_call(
        body, name=name, grid=(s // ROW_BLOCK,),
        in_specs=[pl.BlockSpec((ROW_BLOCK, d), lambda i: (i, 0)), pl.BlockSpec((1, d), lambda i: (0, 0))],
        out_specs=pl.BlockSpec((ROW_BLOCK, d), lambda i: (i, 0)),
        out_shape=jax.ShapeDtypeStruct((s, d), BF16),
        compiler_params=_params(("parallel",)),
    )(x, g)


def _rms_bwd(dn, x, g, dres, name, bf16_copy=True):
    s, d = x.shape

    def body(dn_ref, x_ref, g_ref, dres_ref, dx_ref, *rest):
        gg_ref = rest[-1]
        i = pl.program_id(0)
        xv, dnv = x_ref[...], dn_ref[...]
        r = lax.rsqrt(jnp.mean(xv * xv, axis=-1, keepdims=True) + EPS)
        dng = dnv * g_ref[...]
        c = jnp.mean(dng * xv, axis=-1, keepdims=True)
        dx = dres_ref[...] + r * dng - xv * (r * r * r * c)
        dx_ref[...] = dx
        if bf16_copy:
            rest[0][...] = dx.astype(BF16)

        @pl.when(i == 0)
        def _():
            gg_ref[...] = jnp.zeros_like(gg_ref)

        gg_ref[...] += jnp.sum(dnv * xv * r, axis=0, keepdims=True)

    row = pl.BlockSpec((ROW_BLOCK, d), lambda i: (i, 0))
    vec = pl.BlockSpec((1, d), lambda i: (0, 0))
    return pl.pallas_call(
        body, name=name, grid=(s // ROW_BLOCK,),
        in_specs=[row, row, vec, row],
        out_specs=[row] + [row] * bf16_copy + [vec],
        out_shape=[jax.ShapeDtypeStruct((s, d), F32)] + [jax.ShapeDtypeStruct((s, d), BF16)] * bf16_copy
        + [jax.ShapeDtypeStruct((1, d), F32)],
        compiler_params=_params(("arbitrary",)),
    )(dn, x, g, dres)


def _final_loss(h2, target, g, name="final_loss"):
    s, d = h2.shape

    def body(h_ref, t_ref, g_ref, dh_ref, dhb_ref, loss_ref, gg_ref):
        i = pl.program_id(0)
        hv, gv = h_ref[...], g_ref[...]
        r = lax.rsqrt(jnp.mean(hv * hv, axis=-1, keepdims=True) + EPS)
        e = hv * r * gv - t_ref[...]
        dy = e * (1.0 / d)
        dyg = dy * gv
        c = jnp.mean(dyg * hv, axis=-1, keepdims=True)
        dh = r * dyg - hv * (r * r * r * c)
        dh_ref[...] = dh
        dhb_ref[...] = dh.astype(BF16)

        @pl.when(i == 0)
        def _():
            gg_ref[...] = jnp.zeros_like(gg_ref)
            loss_ref[...] = jnp.zeros_like(loss_ref)

        gg_ref[...] += jnp.sum(dy * hv * r, axis=0, keepdims=True)
        loss_ref[...] += jnp.sum(jnp.sum(e * e, axis=-1, keepdims=True), axis=0, keepdims=True) * (0.5 / d)

    row = pl.BlockSpec((ROW_BLOCK, d), lambda i: (i, 0))
    vec = pl.BlockSpec((1, d), lambda i: (0, 0))
    return pl.pallas_call(
        body, name=name, grid=(s // ROW_BLOCK,),
        in_specs=[row, row, vec],
        out_specs=[row, row, pl.BlockSpec((SUBLANES, LANES), lambda i: (0, 0)), vec],
        out_shape=[jax.ShapeDtypeStruct((s, d), F32), jax.ShapeDtypeStruct((s, d), BF16),
                   jax.ShapeDtypeStruct((SUBLANES, LANES), F32), jax.ShapeDtypeStruct((1, d), F32)],
        compiler_params=_params(("arbitrary",)),
    )(h2, target, g)


def _rope_tables(s):
    pos = jnp.arange(s, dtype=F32)
    inv_freq = ROPE_THETA ** (-jnp.arange(0, ROPE_DIM, 2, dtype=F32) / ROPE_DIM)
    ang = pos[:, None] * inv_freq[None, :]
    cos, sin = jnp.cos(ang), jnp.sin(ang)
    half = ROPE_DIM // 2
    rest = HEAD_DIM - ROPE_DIM
    c = jnp.concatenate([cos, cos, jnp.ones((s, rest), F32)], axis=1)
    sm = jnp.concatenate([-sin, jnp.zeros((s, half + rest), F32)], axis=1)
    sp = jnp.concatenate([jnp.zeros((s, half), F32), sin, jnp.zeros((s, rest), F32)], axis=1)
    return c, sm, sp


def _res_shape(s, groups, dil, dtype):
    return jax.ShapeDtypeStruct((s // dil, dil * groups * LANES), dtype)


def _res_spec(groups, dil):
    return pl.BlockSpec((ROW_BLOCK // dil, dil * groups * LANES), lambda i: (i, 0))


def _to_residues(scr, o_ref, dil):
    groups, rows = scr.shape[0], ROW_BLOCK // dil
    for r in range(dil):
        for h in range(groups):
            piece = scr[h] if dil == 1 else scr.at[h][pl.ds(r, rows, stride=dil), :]
            o_ref[:, (r * groups + h) * LANES:(r * groups + h + 1) * LANES] = piece.astype(o_ref.dtype)


def _from_residues(i_ref, scr, dil):
    groups, rows = scr.shape[0], ROW_BLOCK // dil
    for r in range(dil):
        for h in range(groups):
            piece = i_ref[:, (r * groups + h) * LANES:(r * groups + h + 1) * LANES].astype(F32)
            if dil == 1:
                scr[h] = piece
            else:
                scr.at[h][pl.ds(r, rows, stride=dil), :] = piece


def _rope_fwd(proj, tables, name="rope_fwd"):
    s = proj.shape[0]
    half = ROPE_DIM // 2
    nd = len(DILATIONS)

    def body(p_ref, c_ref, sm_ref, sp_ref, *rest):
        outs, scr = rest[:3 * nd], rest[3 * nd]
        c, sm, sp = c_ref[...], sm_ref[...], sp_ref[...]
        for gi, off in enumerate((OFF_AQ, OFF_AK, OFF_AV)):
            for h in range(ATTN_HEADS):
                t = p_ref[:, off + h * HEAD_DIM: off + (h + 1) * HEAD_DIM]
                if off != OFF_AV:
                    t = t * c + pltpu.roll(t, HEAD_DIM - half, 1) * sm + pltpu.roll(t, half, 1) * sp
                scr[h] = t
            for di, dil in enumerate(DILATIONS):
                _to_residues(scr, outs[3 * di + gi], dil)

    tab = pl.BlockSpec((ROW_BLOCK, HEAD_DIM), lambda i: (i, 0))
    outs = pl.pallas_call(
        body, name=name, grid=(s // ROW_BLOCK,),
        in_specs=[pl.BlockSpec((ROW_BLOCK, 3 * ATTN_W), lambda i: (i, 0)), tab, tab, tab],
        out_specs=[_res_spec(ATTN_HEADS, d) for d in DILATIONS for _ in range(3)],
        out_shape=[_res_shape(s, ATTN_HEADS, d, BF16) for d in DILATIONS for _ in range(3)],
        scratch_shapes=[pltpu.VMEM((ATTN_HEADS, ROW_BLOCK, LANES), F32)],
        compiler_params=_params(("parallel",)),
    )(proj, *tables)
    return [tuple(outs[3 * di:3 * di + 3]) for di in range(nd)]


def _rope_bwd(grads, tables, name="rope_bwd"):
    s = grads[0][0].shape[0] * DILATIONS[0]
    half = ROPE_DIM // 2
    nd = len(DILATIONS)

    def body(*refs):
        ins = refs[:3 * nd]
        c_ref, sm_ref, sp_ref, o_ref = refs[3 * nd:3 * nd + 4]
        scrs = refs[3 * nd + 4:]
        c, sm, sp = c_ref[...], sm_ref[...], sp_ref[...]
        for gi, off in enumerate((OFF_AQ, OFF_AK, OFF_AV)):
            for di, dil in enumerate(DILATIONS):
                _from_residues(ins[3 * di + gi], scrs[di], dil)
            for h in range(ATTN_HEADS):
                t = scrs[0][h]
                for scr in scrs[1:]:
                    t = t + scr[h]
                if off != OFF_AV:
                    t = t * c + pltpu.roll(t * sm, half, 1) + pltpu.roll(t * sp, HEAD_DIM - half, 1)
                o_ref[:, off + h * HEAD_DIM: off + (h + 1) * HEAD_DIM] = t.astype(BF16)

    tab = pl.BlockSpec((ROW_BLOCK, HEAD_DIM), lambda i: (i, 0))
    return pl.pallas_call(
        body, name=name, grid=(s // ROW_BLOCK,),
        in_specs=[_res_spec(ATTN_HEADS, d) for d in DILATIONS for _ in range(3)] + [tab, tab, tab],
        out_specs=pl.BlockSpec((ROW_BLOCK, 3 * ATTN_W), lambda i: (i, 0)),
        out_shape=jax.ShapeDtypeStruct((s, IN_PAD), BF16),
        scratch_shapes=[pltpu.VMEM((ATTN_HEADS, ROW_BLOCK, LANES), F32) for _ in DILATIONS],
        compiler_params=_params(("parallel",)),
    )(*[t for g in grads for t in g], *tables)


ATTN_GROUP = 2


def _window_specs(nsteps, width):
    rows, hb = ATTN_GROUP * ATTN_BLOCK, N_SIDE
    per = rows // hb
    cur = pl.BlockSpec((rows, width), lambda r, j: (j, r))
    prev = pl.BlockSpec((hb, width), lambda r, j: (jnp.maximum(per * j - 1, 0), r))
    nxt = pl.BlockSpec((hb, width), lambda r, j: (jnp.minimum(per * (j + 1), per * nsteps - 1), r))
    return prev, cur, nxt


def _block(ref, b, sl):
    return ref[b * ATTN_BLOCK:(b + 1) * ATTN_BLOCK, sl]


def _edge(prev_ref, cur_ref, next_ref, b, sl):
    qb, hb = ATTN_BLOCK, N_SIDE
    before = prev_ref[:, sl] if b == 0 else cur_ref[b * qb - hb:b * qb, sl]
    after = next_ref[:, sl] if b == ATTN_GROUP - 1 else cur_ref[(b + 1) * qb:(b + 1) * qb + hb, sl]
    return jnp.concatenate([before, after], axis=0)


def _band_masks(j, length):
    qb, hb = ATTN_BLOCK, N_SIDE
    row = lax.broadcasted_iota(jnp.int32, (qb, qb), 0)
    col = lax.broadcasted_iota(jnp.int32, (qb, qb), 1)

    def edge_pos(i):
        return j * qb - hb + i + jnp.where(i >= hb, qb, 0)

    def ok(a, b, outside):
        return (jnp.abs(a - b) <= N_SIDE) & (outside >= 0) & (outside < length)

    cur = jnp.abs(row - col) <= N_SIDE
    edge_k = ok(j * qb + row, edge_pos(col), edge_pos(col))
    edge_q = ok(edge_pos(row), j * qb + col, edge_pos(row))
    return cur, edge_k, edge_q


def _attn_fwd(q, k, v, dil, name):
    length = q.shape[0]
    qb = ATTN_BLOCK
    nsteps = length // (ATTN_GROUP * qb)
    scale = HEAD_DIM ** -0.5

    def body(q_ref, kp_ref, kc_ref, kn_ref, vp_ref, vc_ref, vn_ref, o_ref, lse_ref):
        masks = [_band_masks(pl.program_id(1) * ATTN_GROUP + b, length) for b in range(ATTN_GROUP)]
        lane = lax.broadcasted_iota(jnp.int32, (qb, LANES), 1)
        units = [(b, h, slice(h * HEAD_DIM, (h + 1) * HEAD_DIM)) for b in range(ATTN_GROUP)
                 for h in range(ATTN_HEADS)]
        scores = [(_dot_nt(_block(q_ref, b, sl), _block(kc_ref, b, sl)),
                   _dot_nt(_block(q_ref, b, sl), _edge(kp_ref, kc_ref, kn_ref, b, sl))) for b, _, sl in units]
        probs = []
        lse_acc = [jnp.zeros((qb, LANES), F32) for _ in range(ATTN_GROUP)]
        for (b, h, _), (s_c, s_e) in zip(units, scores):
            valid_c, valid_e, _ = masks[b]
            s_c = jnp.where(valid_c, s_c * scale, NEG)
            s_e = jnp.where(valid_e, s_e * scale, NEG)
            m = jnp.max(jnp.maximum(s_c, s_e), axis=-1, keepdims=True)
            p_c, p_e = jnp.exp(s_c - m), jnp.exp(s_e - m)
            den = jnp.sum(p_c + p_e, axis=-1, keepdims=True)
            probs.append((p_c.astype(BF16), p_e.astype(BF16), 1.0 / den))
            lse_acc[b] = jnp.where(lane == h, m + jnp.log(den), lse_acc[b])
        for (b, _, sl), (p_c, p_e, inv) in zip(units, probs):
            o_ref[b * qb:(b + 1) * qb, sl] = (_dot(p_c, _block(vc_ref, b, sl))
                                              + _dot(p_e, _edge(vp_ref, vc_ref, vn_ref, b, sl))) * inv
        for b in range(ATTN_GROUP):
            lse_ref[b * qb:(b + 1) * qb, :] = lse_acc[b]

    prev, cur, nxt = _window_specs(nsteps, ATTN_W)
    return pl.pallas_call(
        body, name=name, grid=(dil, nsteps),
        in_specs=[cur, prev, cur, nxt, prev, cur, nxt],
        out_specs=[cur, pl.BlockSpec((ATTN_GROUP * qb, LANES), lambda r, j: (j, r))],
        out_shape=[jax.ShapeDtypeStruct((length, dil * ATTN_W), F32),
                   jax.ShapeDtypeStruct((length, dil * LANES), F32)],
        compiler_params=_params(("parallel", "parallel")),
    )(q, k, k, k, v, v, v)


def _attn_combine(outs, lses, g, name="attn_combine"):
    s = outs[0].shape[0] * DILATIONS[0]
    nd = len(DILATIONS)

    def body(*refs):
        o_refs, l_refs = refs[:nd], refs[nd:2 * nd]
        g_ref, o_ref, n_ref = refs[2 * nd:2 * nd + 3]
        lse_outs = refs[2 * nd + 3:3 * nd + 3]
        o_scr, l_scr = refs[3 * nd + 3:4 * nd + 3], refs[4 * nd + 3:5 * nd + 3]
        for di, dil in enumerate(DILATIONS):
            _from_residues(o_refs[di], o_scr[di], dil)
            _from_residues(l_refs[di], l_scr[di], dil)
        ls = [scr[0] for scr in l_scr]
        m = ls[0]
        for l in ls[1:]:
            m = jnp.maximum(m, l)
        es = [jnp.exp(l - m) for l in ls]
        z = es[0]
        for e in es[1:]:
            z = z + e
        ws = [e / z for e in es]
        l_scr[0][0] = m + jnp.log(z)
        for di, dil in enumerate(DILATIONS):
            _to_residues(l_scr[0], lse_outs[di], dil)
        ssq = jnp.zeros((ROW_BLOCK, 1), F32)
        for h in range(ATTN_HEADS):
            sl = slice(h * HEAD_DIM, (h + 1) * HEAD_DIM)
            acc = ws[0][:, h:h + 1] * o_scr[0][h]
            for w, scr in zip(ws[1:], o_scr[1:]):
                acc = acc + w[:, h:h + 1] * scr[h]
            o_ref[:, sl] = acc
            ssq = ssq + jnp.sum(acc * acc, axis=-1, keepdims=True)
        r = lax.rsqrt(ssq * (1.0 / ATTN_W) + EPS)
        n_ref[...] = (o_ref[...] * r * g_ref[...]).astype(BF16)

    blk = pl.BlockSpec((ROW_BLOCK, ATTN_W), lambda i: (i, 0))
    outs_ = pl.pallas_call(
        body, name=name, grid=(s // ROW_BLOCK,),
        in_specs=[_res_spec(ATTN_HEADS, d) for d in DILATIONS] + [_res_spec(1, d) for d in DILATIONS]
        + [pl.BlockSpec((1, ATTN_W), lambda i: (0, 0))],
        out_specs=[blk, blk] + [_res_spec(1, d) for d in DILATIONS],
        out_shape=[jax.ShapeDtypeStruct((s, ATTN_W), F32), jax.ShapeDtypeStruct((s, D_MODEL), BF16)]
        + [_res_shape(s, 1, d, F32) for d in DILATIONS],
        scratch_shapes=[pltpu.VMEM((ATTN_HEADS, ROW_BLOCK, LANES), F32) for _ in DILATIONS]
        + [pltpu.VMEM((1, ROW_BLOCK, LANES), F32) for _ in DILATIONS],
        compiler_params=_params(("parallel",)),
    )(*outs, *lses, g)
    return outs_[0], outs_[1], list(outs_[2:])


def _attn_prebwd(dcat, o, g, name="attn_prebwd"):
    s = o.shape[0]
    nd = len(DILATIONS)

    def body(dy_ref, o_ref, g_ref, *rest):
        do_outs, delta_outs, gg_ref = rest[:nd], rest[nd:2 * nd], rest[2 * nd]
        do_scr, delta_scr = rest[2 * nd + 1], rest[2 * nd + 2]
        i = pl.program_id(0)
        dy, ov = dy_ref[...], o_ref[...]
        r = lax.rsqrt(jnp.mean(ov * ov, axis=-1, keepdims=True) + EPS)
        dyg = dy * g_ref[...]
        c = jnp.mean(dyg * ov, axis=-1, keepdims=True)
        do = r * dyg - ov * (r * r * r * c)
        prod = do * ov
        lane = lax.broadcasted_iota(jnp.int32, (ROW_BLOCK, LANES), 1)
        acc = jnp.zeros((ROW_BLOCK, LANES), F32)
        for h in range(ATTN_HEADS):
            sl = slice(h * HEAD_DIM, (h + 1) * HEAD_DIM)
            do_scr[h] = do[:, sl]
            acc = jnp.where(lane == h, jnp.sum(prod[:, sl], axis=-1, keepdims=True), acc)
        delta_scr[0] = acc
        for di, dil in enumerate(DILATIONS):
            _to_residues(do_scr, do_outs[di], dil)
            _to_residues(delta_scr, delta_outs[di], dil)

        @pl.when(i == 0)
        def _():
            gg_ref[...] = jnp.zeros_like(gg_ref)

        gg_ref[...] += jnp.sum(dy * ov * r, axis=0, keepdims=True)

    blk = pl.BlockSpec((ROW_BLOCK, ATTN_W), lambda i: (i, 0))
    vec = pl.BlockSpec((1, ATTN_W), lambda i: (0, 0))
    outs = pl.pallas_call(
        body, name=name, grid=(s // ROW_BLOCK,),
        in_specs=[blk, blk, vec],
        out_specs=[_res_spec(ATTN_HEADS, d) for d in DILATIONS] + [_res_spec(1, d) for d in DILATIONS] + [vec],
        out_shape=[_res_shape(s, ATTN_HEADS, d, BF16) for d in DILATIONS]
        + [_res_shape(s, 1, d, F32) for d in DILATIONS] + [jax.ShapeDtypeStruct((1, ATTN_W), F32)],
        scratch_shapes=[pltpu.VMEM((ATTN_HEADS, ROW_BLOCK, LANES), F32), pltpu.VMEM((1, ROW_BLOCK, LANES), F32)],
        compiler_params=_params(("arbitrary",)),
    )(dcat, o, g)
    return list(outs[:nd]), list(outs[nd:2 * nd]), outs[2 * nd]


def _attn_bwd(q, k, v, do, lse, delta, dil, name):
    length = q.shape[0]
    qb = ATTN_BLOCK
    nsteps = length // (ATTN_GROUP * qb)
    scale = HEAD_DIM ** -0.5

    def body(qp, qc, qn, kp, kc, kn, vp, vc, vn, dop, doc, don, lp, lc, ln, dp, dc, dn, dq_ref, dk_ref, dv_ref):
        masks = [_band_masks(pl.program_id(1) * ATTN_GROUP + b, length) for b in range(ATTN_GROUP)]
        everything = slice(None)
        lse_e = [_edge(lp, lc, ln, b, everything) for b in range(ATTN_GROUP)]
        del_e = [_edge(dp, dc, dn, b, everything) for b in range(ATTN_GROUP)]
        units = [(b, h, slice(h * HEAD_DIM, (h + 1) * HEAD_DIM)) for b in range(ATTN_GROUP)
                 for h in range(ATTN_HEADS)]
        prods = []
        for b, _, sl in units:
            q_c, k_c, v_c, do_c = _block(qc, b, sl), _block(kc, b, sl), _block(vc, b, sl), _block(doc, b, sl)
            q_e, k_e = _edge(qp, qc, qn, b, sl), _edge(kp, kc, kn, b, sl)
            v_e, do_e = _edge(vp, vc, vn, b, sl), _edge(dop, doc, don, b, sl)
            prods.append((_dot_nt(q_c, k_c), _dot_nt(do_c, v_c), _dot_nt(q_c, k_e), _dot_nt(do_c, v_e),
                          _dot_nt(q_e, k_c), _dot_nt(do_e, v_c)))
        parts = []
        for (b, h, _), (s_cc, dp_cc, s_ek, dp_ek, s_eq, dp_eq) in zip(units, prods):
            valid_c, valid_ek, valid_eq = masks[b]
            hc = slice(h, h + 1)
            lse_c, del_c = _block(lc, b, hc), _block(dc, b, hc)
            p_cc = jnp.where(valid_c, jnp.exp(s_cc * scale - lse_c), 0.0)
            ds_cc = (p_cc * (dp_cc - del_c)).astype(BF16)
            p_ek = jnp.where(valid_ek, jnp.exp(s_ek * scale - lse_c), 0.0)
            ds_ek = (p_ek * (dp_ek - del_c)).astype(BF16)
            p_eq = jnp.where(valid_eq, jnp.exp(s_eq * scale - lse_e[b][:, hc]), 0.0)
            ds_eq = (p_eq * (dp_eq - del_e[b][:, hc])).astype(BF16)
            parts.append((p_cc.astype(BF16), ds_cc, ds_ek, p_eq.astype(BF16), ds_eq))
        for (b, _, sl), (p_cc, ds_cc, ds_ek, p_eq, ds_eq) in zip(units, parts):
            rows = slice(b * qb, (b + 1) * qb)
            q_c, k_c, do_c = _block(qc, b, sl), _block(kc, b, sl), _block(doc, b, sl)
            q_e, k_e, do_e = _edge(qp, qc, qn, b, sl), _edge(kp, kc, kn, b, sl), _edge(dop, doc, don, b, sl)
            dq_ref[rows, sl] = ((_dot(ds_cc, k_c) + _dot(ds_ek, k_e)) * scale).astype(BF16)
            dk_ref[rows, sl] = ((_dot_tn(ds_cc, q_c) + _dot_tn(ds_eq, q_e)) * scale).astype(BF16)
            dv_ref[rows, sl] = (_dot_tn(p_cc, do_c) + _dot_tn(p_eq, do_e)).astype(BF16)

    wide, narrow = list(_window_specs(nsteps, ATTN_W)), list(_window_specs(nsteps, LANES))
    return tuple(pl.pallas_call(
        body, name=name, grid=(dil, nsteps),
        in_specs=wide * 4 + narrow * 2,
        out_specs=[wide[1]] * 3,
        out_shape=[jax.ShapeDtypeStruct((length, dil * ATTN_W), BF16)] * 3,
        compiler_params=_params(("parallel", "parallel")),
    )(q, q, q, k, k, k, v, v, v, do, do, do, lse, lse, lse, delta, delta, delta))


def _gate_matrices(gf_up, gb_up):
    pad = LANES - 2 * GLA_RANK
    uf = jnp.concatenate([gf_up, jnp.zeros((GLA_RANK + pad, GLA_KW), gf_up.dtype)], axis=0)
    ub = jnp.concatenate([jnp.zeros((GLA_RANK, GLA_KW), gb_up.dtype), gb_up, jnp.zeros((pad, GLA_KW), gb_up.dtype)], axis=0)
    return uf.astype(BF16), ub.astype(BF16)


def _log_sigmoid(x):
    return jnp.minimum(x, 0.0) - jnp.log(1.0 + jnp.exp(-jnp.abs(x)))


def _gla_gates(proj, uf, ub, gf_b, gb_b, name="gla_gates"):
    s = proj.shape[0]

    def body(z_ref, uf_ref, ub_ref, bf_ref, bb_ref, gf_ref, gb_ref):
        z = z_ref[...].astype(BF16)
        gf_ref[...] = _log_sigmoid(_dot(z, uf_ref[...]) + bf_ref[...]) * (1.0 / GLA_GATE_NORM)
        gb_ref[...] = _log_sigmoid(_dot(z, ub_ref[...]) + bb_ref[...]) * (1.0 / GLA_GATE_NORM)

    mat = pl.BlockSpec((LANES, GLA_KW), lambda i: (0, 0))
    vec = pl.BlockSpec((1, GLA_KW), lambda i: (0, 0))
    out = pl.BlockSpec((ROW_BLOCK, GLA_KW), lambda i: (i, 0))
    return pl.pallas_call(
        body, name=name, grid=(s // ROW_BLOCK,),
        in_specs=[pl.BlockSpec((ROW_BLOCK, LANES), lambda i: (i, OFF_Z // LANES)), mat, mat, vec, vec],
        out_specs=[out, out],
        out_shape=[jax.ShapeDtypeStruct((s, GLA_KW), F32)] * 2,
        compiler_params=_params(("parallel",)),
    )(proj, uf, ub, gf_b, gb_b)


def _gla_gates_bwd(dgf, dgb, proj, uf, ub, gf_b, gb_b, dproj, name="gla_gates_bwd"):
    s = proj.shape[0]
    tail = IN_PAD - OFF_Z

    def body(dgf_ref, dgb_ref, z_ref, uf_ref, ub_ref, bf_ref, bb_ref, _, dz_ref, guf_ref, gub_ref, gbf_ref, gbb_ref):
        i = pl.program_id(0)
        z = z_ref[...].astype(BF16)
        uf_, ub_ = uf_ref[...], ub_ref[...]
        dpf = dgf_ref[...] * (1.0 / GLA_GATE_NORM) * _sigmoid(-(_dot(z, uf_) + bf_ref[...]))
        dpb = dgb_ref[...] * (1.0 / GLA_GATE_NORM) * _sigmoid(-(_dot(z, ub_) + bb_ref[...]))
        dpf_b, dpb_b = dpf.astype(BF16), dpb.astype(BF16)
        dz_ref[:, 0:LANES] = (_dot_nt(dpf_b, uf_) + _dot_nt(dpb_b, ub_)).astype(BF16)
        dz_ref[:, LANES:tail] = jnp.zeros((ROW_BLOCK, tail - LANES), BF16)

        @pl.when(i == 0)
        def _():
            for r in (guf_ref, gub_ref, gbf_ref, gbb_ref):
                r[...] = jnp.zeros_like(r)

        guf_ref[...] += _dot_tn(z, dpf_b)
        gub_ref[...] += _dot_tn(z, dpb_b)
        gbf_ref[...] += jnp.sum(dpf, axis=0, keepdims=True)
        gbb_ref[...] += jnp.sum(dpb, axis=0, keepdims=True)

    mat = pl.BlockSpec((LANES, GLA_KW), lambda i: (0, 0))
    vec = pl.BlockSpec((1, GLA_KW), lambda i: (0, 0))
    blk = pl.BlockSpec((ROW_BLOCK, GLA_KW), lambda i: (i, 0))
    return pl.pallas_call(
        body, name=name, grid=(s // ROW_BLOCK,),
        in_specs=[blk, blk, pl.BlockSpec((ROW_BLOCK, LANES), lambda i: (i, OFF_Z // LANES)), mat, mat, vec, vec,
                  pl.BlockSpec(memory_space=pl.ANY)],
        out_specs=[pl.BlockSpec((ROW_BLOCK, tail), lambda i: (i, OFF_Z // tail)), mat, mat, vec, vec],
        out_shape=[jax.ShapeDtypeStruct(dproj.shape, dproj.dtype), jax.ShapeDtypeStruct((LANES, GLA_KW), F32),
                   jax.ShapeDtypeStruct((LANES, GLA_KW), F32), jax.ShapeDtypeStruct((1, GLA_KW), F32),
                   jax.ShapeDtypeStruct((1, GLA_KW), F32)],
        input_output_aliases={7: 0},
        compiler_params=_params(("arbitrary",)),
    )(dgf, dgb, proj, uf, ub, gf_b, gb_b, dproj)


def _split3(x):
    x1 = x.astype(BF16)
    r1 = x - x1.astype(F32)
    x2 = r1.astype(BF16)
    x3 = (r1 - x2.astype(F32)).astype(BF16)
    return x1, x2, x3


def _dot_exact(mask_bf, x):
    x1, x2, x3 = _split3(x)
    return _dot(mask_bf, x1) + _dot(mask_bf, x2) + _dot(mask_bf, x3)


def _chunk_masks(reverse):
    c = GLA_CHUNK
    row = lax.broadcasted_iota(jnp.int32, (c, c), 0)
    col = lax.broadcasted_iota(jnp.int32, (c, c), 1)
    allowed = (col >= row) if reverse else (col <= row)
    seen_by = (col <= row) if reverse else (col >= row)
    return allowed, seen_by


def _chunk_terms(q_ref, k_ref, g_ref, rs, hs, allowed, reverse):
    c = GLA_CHUNK
    mid, last = (c // 2, 0) if reverse else (c // 2 - 1, c - 1)
    q = q_ref[rs, hs] * (GLA_DK ** -0.5)
    k = k_ref[rs, hs]
    b = _dot_exact(jnp.where(allowed, 1.0, 0.0).astype(BF16), g_ref[rs, hs])
    bref, blast = b[mid:mid + 1, :], b[last:last + 1, :]
    e_q, e_k, e_in, e_st = jnp.exp(b - bref), jnp.exp(bref - b), jnp.exp(b), jnp.exp(blast - b)
    return dict(last=last, e_q=e_q, e_k=e_k, e_in=e_in, e_st=e_st,
                dec=jnp.exp(blast), qe=q * e_q, ke=k * e_k, qin=q * e_in, kst=k * e_st)


def _gla_blockspecs(s, reverse_order):
    cb = GLA_CHUNKS_PER_STEP
    rows = cb * GLA_CHUNK
    nsteps = s // rows

    def rb(n):
        return (nsteps - 1 - n) if reverse_order else n

    qspec = pl.BlockSpec((rows, GLA_KW), lambda n: (rb(n), OFF_GQ // GLA_KW))
    kspec = pl.BlockSpec((rows, GLA_KW), lambda n: (rb(n), OFF_GK // GLA_KW))
    vspec = pl.BlockSpec((rows, GLA_VW), lambda n: (rb(n), OFF_GV // GLA_VW))
    gspec = pl.BlockSpec((rows, GLA_KW), lambda n: (rb(n), 0))
    ospec = pl.BlockSpec((rows, GLA_VW), lambda n: (rb(n), 0))
    sspec = pl.BlockSpec((GLA_HEADS, cb, GLA_DV, GLA_DK), lambda n: (0, rb(n), 0, 0))
    return cb, rows, nsteps, qspec, kspec, vspec, gspec, ospec, sspec


def _gla_units(cb, order_reversed):
    chunks = list(reversed(range(cb))) if order_reversed else list(range(cb))
    return [(c, h, slice(c * GLA_CHUNK, (c + 1) * GLA_CHUNK), slice(h * GLA_DK, (h + 1) * GLA_DK),
             slice(h * GLA_DV, (h + 1) * GLA_DV)) for c in chunks for h in range(GLA_HEADS)]


def _gla_fwd(proj, g, reverse, name):
    s = proj.shape[0]
    cb, rows, nsteps, qspec, kspec, vspec, gspec, ospec, sspec = _gla_blockspecs(s, reverse)

    def body(q_ref, k_ref, v_ref, g_ref, o_ref, st_ref, state):
        @pl.when(pl.program_id(0) == 0)
        def _():
            state[...] = jnp.zeros_like(state)

        allowed, _ = _chunk_masks(reverse)
        units = _gla_units(cb, reverse)
        terms = [_chunk_terms(q_ref, k_ref, g_ref, rs, hs, allowed, reverse) for _, _, rs, hs, _ in units]
        vals = [v_ref[rs, vs].astype(BF16) for _, _, rs, _, vs in units]
        raw = [(_dot_nt(t["qe"].astype(BF16), t["ke"].astype(BF16)), _dot_tn(v, t["kst"].astype(BF16)))
               for t, v in zip(terms, vals)]
        intra = [_dot(jnp.where(allowed, a, 0.0).astype(BF16), v) for (a, _), v in zip(raw, vals)]
        st = [state[h] for h in range(GLA_HEADS)]
        for (c, h, rs, _, vs), t, (_, kv), o_in in zip(units, terms, raw, intra):
            st_ref[h, c] = st[h]
            o_ref[rs, vs] = o_in + _dot_nt(t["qin"].astype(BF16), st[h].astype(BF16))
            st[h] = st[h] * t["dec"] + kv
        for h in range(GLA_HEADS):
            state[h] = st[h]

    return pl.pallas_call(
        body, name=name, grid=(nsteps,),
        in_specs=[qspec, kspec, vspec, gspec],
        out_specs=[ospec, sspec],
        out_shape=[jax.ShapeDtypeStruct((s, GLA_VW), F32),
                   jax.ShapeDtypeStruct((GLA_HEADS, s // GLA_CHUNK, GLA_DV, GLA_DK), F32)],
        scratch_shapes=[pltpu.VMEM((GLA_HEADS, GLA_DV, GLA_DK), F32)],
        compiler_params=_params(("arbitrary",)),
    )(proj, proj, proj, g)


def _gla_bwd(proj, g, do, states, reverse, name, merge=None):
    s = proj.shape[0]
    cb, rows, nsteps, qspec, kspec, vspec, gspec, ospec, sspec = _gla_blockspecs(s, not reverse)
    gla_cols = OFF_Z - OFF_GQ

    def body(q_ref, k_ref, v_ref, g_ref, do_ref, sp_ref, *rest):
        if merge is None:
            dq_ref, dk_ref, dv_ref, dg_ref, dstate = rest
        else:
            dq_o, dk_o, dv_o, dgr_ref, _, dp_ref, dg_ref, dstate = rest
        @pl.when(pl.program_id(0) == 0)
        def _():
            dstate[...] = jnp.zeros_like(dstate)

        allowed, seen_by = _chunk_masks(reverse)
        units = _gla_units(cb, not reverse)
        terms = [_chunk_terms(q_ref, k_ref, g_ref, rs, hs, allowed, reverse) for _, _, rs, hs, _ in units]
        vals = [v_ref[rs, vs].astype(BF16) for _, _, rs, _, vs in units]
        dos = [do_ref[rs, vs] for _, _, rs, _, vs in units]
        prevs = [sp_ref[h, c] for c, h, _, _, _ in units]
        raw = [(_dot_nt(t["qe"].astype(BF16), t["ke"].astype(BF16)), _dot_nt(do, v),
                _dot(do, sp.astype(BF16)), _dot_tn(do, t["qin"].astype(BF16)))
               for t, v, do, sp in zip(terms, vals, dos, prevs)]
        inner = []
        for t, do, (a, da, _, _) in zip(terms, dos, raw):
            da = jnp.where(allowed, da, 0.0).astype(BF16)
            inner.append((_dot(da, t["ke"].astype(BF16)), _dot_tn(da, t["qe"].astype(BF16)),
                          _dot_tn(jnp.where(allowed, a, 0.0).astype(BF16), do)))
        ds = [dstate[h] for h in range(GLA_HEADS)]
        outer = []
        for (c, h, _, _, _), t, v, sp, (_, _, _, inc) in zip(units, terms, vals, prevs, raw):
            ds_b = ds[h].astype(BF16)
            outer.append((_dot(v, ds_b), _dot_nt(t["kst"].astype(BF16), ds_b),
                          jnp.sum(sp * ds[h], axis=0, keepdims=True)))
            ds[h] = ds[h] * t["dec"] + inc
        for h in range(GLA_HEADS):
            dstate[h] = ds[h]
        seen_bf = jnp.where(seen_by, 1.0, 0.0).astype(BF16)
        rowi = lax.broadcasted_iota(jnp.int32, (GLA_CHUNK, GLA_DK), 0)
        for (c, h, rs, hs, vs), t, (_, _, dqin, _), (dqe, dke, dv_in), (dkst, dv_out, ddec) in zip(
                units, terms, raw, inner, outer):
            dq = (dqe * t["e_q"] + dqin * t["e_in"]) * (GLA_DK ** -0.5)
            dk = dke * t["e_k"] + dkst * t["e_st"]
            if merge is None:
                dq_ref[rs, hs], dk_ref[rs, hs], dv_ref[rs, vs] = dq, dk, dv_in + dv_out
            else:
                lo = OFF_GK - OFF_GQ + h * GLA_DK
                dp_ref[rs, hs] = (dq + dq_o[rs, hs]).astype(BF16)
                dp_ref[rs, lo:lo + GLA_DK] = (dk + dk_o[rs, hs]).astype(BF16)
                lo = OFF_GV - OFF_GQ + h * GLA_DV
                dp_ref[rs, lo:lo + GLA_DV] = (dv_in + dv_out + dv_o[rs, vs]).astype(BF16)
            kk = dkst * t["kst"]
            db = dqe * t["qe"] - dke * t["ke"] + dqin * t["qin"] - kk
            extra = jnp.sum(kk, axis=0, keepdims=True) + ddec * t["dec"]
            db = db + jnp.where(rowi == t["last"], extra, 0.0)
            dg_ref[rs, hs] = _dot_exact(seen_bf, db)
        if merge is not None:
            dp_ref[:, OFF_GR - OFF_GQ:gla_cols] = dgr_ref[...]

    scratch = [pltpu.VMEM((GLA_HEADS, GLA_DV, GLA_DK), F32)]
    if merge is None:
        return pl.pallas_call(
            body, name=name, grid=(nsteps,),
            in_specs=[qspec, kspec, vspec, gspec, ospec, sspec],
            out_specs=[gspec, gspec, ospec, gspec],
            out_shape=[jax.ShapeDtypeStruct((s, GLA_KW), F32), jax.ShapeDtypeStruct((s, GLA_KW), F32),
                       jax.ShapeDtypeStruct((s, GLA_VW), F32), jax.ShapeDtypeStruct((s, GLA_KW), F32)],
            scratch_shapes=scratch,
            compiler_params=_params(("arbitrary",)),
        )(proj, proj, proj, g, do, states)
    dproj = merge[4]
    block = gspec.index_map
    return pl.pallas_call(
        body, name=name, grid=(nsteps,),
        in_specs=[qspec, kspec, vspec, gspec, ospec, sspec, gspec, gspec, ospec, ospec, _ANY],
        out_specs=[pl.BlockSpec((rows, gla_cols), lambda n: (block(n)[0], OFF_GQ // gla_cols)), gspec],
        out_shape=[jax.ShapeDtypeStruct(dproj.shape, dproj.dtype), jax.ShapeDtypeStruct((s, GLA_KW), F32)],
        input_output_aliases={10: 0},
        scratch_shapes=scratch,
        compiler_params=_params(("arbitrary",)),
    )(proj, proj, proj, g, do, states, *merge)


def _gla_post(o_f, o_b, proj, g, cat, name="gla_post"):
    s = o_f.shape[0]

    def body(of_ref, ob_ref, gr_ref, g_ref, _, o_ref):
        gv = g_ref[...]
        for h in range(GLA_HEADS):
            sl = slice(h * GLA_DV, (h + 1) * GLA_DV)
            osum = of_ref[:, sl] + ob_ref[:, sl]
            r = lax.rsqrt(jnp.mean(osum * osum, axis=-1, keepdims=True) + EPS)
            gr = gr_ref[:, sl]
            o_ref[:, sl] = (osum * r * gv * (gr * _sigmoid(gr))).astype(BF16)

    blk = pl.BlockSpec((ROW_BLOCK, GLA_VW), lambda i: (i, 0))
    return pl.pallas_call(
        body, name=name, grid=(s // ROW_BLOCK,),
        in_specs=[blk, blk, pl.BlockSpec((ROW_BLOCK, GLA_VW), lambda i: (i, OFF_GR // GLA_VW)),
                  pl.BlockSpec((1, GLA_DV), lambda i: (0, 0)), pl.BlockSpec(memory_space=pl.ANY)],
        out_specs=pl.BlockSpec((ROW_BLOCK, GLA_VW), lambda i: (i, ATTN_W // GLA_VW)),
        out_shape=jax.ShapeDtypeStruct(cat.shape, cat.dtype),
        input_output_aliases={4: 0},
        compiler_params=_params(("parallel",)),
    )(o_f, o_b, proj, g, cat)


def _gla_post_bwd(dcat, o_f, o_b, proj, g, name="gla_post_bwd"):
    s = o_f.shape[0]

    def body(dy_ref, of_ref, ob_ref, gr_ref, g_ref, do_ref, dgr_ref, gg_ref):
        i = pl.program_id(0)
        gv = g_ref[...]
        gg = jnp.zeros((1, GLA_DV), F32)
        for h in range(GLA_HEADS):
            sl = slice(h * GLA_DV, (h + 1) * GLA_DV)
            osum = of_ref[:, sl] + ob_ref[:, sl]
            r = lax.rsqrt(jnp.mean(osum * osum, axis=-1, keepdims=True) + EPS)
            gr, dy = gr_ref[:, sl], dy_ref[:, sl]
            sg = _sigmoid(gr)
            dgr_ref[:, sl] = (dy * (osum * r * gv) * (sg * (1.0 + gr * (1.0 - sg)))).astype(BF16)
            dn = dy * (gr * sg)
            dng = dn * gv
            c = jnp.mean(dng * osum, axis=-1, keepdims=True)
            do_ref[:, sl] = (r * dng - osum * (r * r * r * c)).astype(BF16)
            gg = gg + jnp.sum(dn * osum * r, axis=0, keepdims=True)

        @pl.when(i == 0)
        def _():
            gg_ref[...] = jnp.zeros_like(gg_ref)

        gg_ref[...] += gg

    blk = pl.BlockSpec((ROW_BLOCK, GLA_VW), lambda i: (i, 0))
    vec = pl.BlockSpec((1, GLA_DV), lambda i: (0, 0))
    return pl.pallas_call(
        body, name=name, grid=(s // ROW_BLOCK,),
        in_specs=[pl.BlockSpec((ROW_BLOCK, GLA_VW), lambda i: (i, 1)), blk, blk,
                  pl.BlockSpec((ROW_BLOCK, GLA_VW), lambda i: (i, OFF_GR // GLA_VW)), vec],
        out_specs=[blk, blk, vec],
        out_shape=[jax.ShapeDtypeStruct((s, GLA_VW), BF16), jax.ShapeDtypeStruct((s, GLA_VW), BF16),
                   jax.ShapeDtypeStruct((1, GLA_DV), F32)],
        compiler_params=_params(("arbitrary",)),
    )(dcat, o_f, o_b, proj, g)


HALO = 16


def _extended(prev_ref, cur_ref, next_ref, i, s, tr, cs):
    first, last = i == 0, i == s // tr - 1
    prev = jnp.where(first, 0.0, prev_ref[:, cs].astype(F32))
    nxt = jnp.where(last, 0.0, next_ref[:, cs].astype(F32))
    return jnp.concatenate([prev, cur_ref[:, cs].astype(F32), nxt], axis=0)


FFN_ROWS = 512
FFN_COLS = 512


FFN_CHUNK = 256


def _lagged(i, ni, multiply, finish, rotate, init):
    chunks = [slice(c, c + FFN_CHUNK) for c in range(0, FFN_COLS, FFN_CHUNK)]

    @pl.when(i == 0)
    def _():
        init()

    @pl.when(i < 2)
    def _():
        rotate([multiply(cs) for cs in chunks], chunks)

    @pl.when((i >= 2) & (i < ni))
    def _():
        new = []
        for cs in chunks:
            new.append(multiply(cs))
            finish(cs)
        rotate(new, chunks)

    @pl.when(i >= ni)
    def _():
        for cs in chunks:
            finish(cs)
        rotate(None, chunks)


def _ffn_in(n2, w_gate, w_up, conv_w, conv_b, name="ffn_in"):
    s, d = n2.shape
    f = w_gate.shape[1]
    tm, tn, edge = FFN_ROWS, FFN_COLS, SUBLANES
    ni = s // tm
    ext = tm + 2 * edge

    def body(a_ref, wg_ref, wu_ref, w_ref, b_ref, gate_ref, up_ref, act_ref, g_tile, u_tile, g_tail):
        i = pl.program_id(1)

        @pl.when(i == 0)
        def _():
            g_tile[...] = jnp.zeros_like(g_tile)
            u_tile[...] = jnp.zeros_like(u_tile)
            g_tail[...] = jnp.zeros_like(g_tail)

        a = a_ref[...]
        g_new = _dot(a, wg_ref[...])
        u_new = _dot(a, wu_ref[...])
        g_old, u_old = g_tile[...], u_tile[...]
        before = jnp.where(i == 1, 0.0, g_tail[...])
        after = jnp.where(i == ni, 0.0, g_new[0:edge])
        ge = jnp.concatenate([before, g_old, after], axis=0)
        w = w_ref[...]
        conv = (w[0:1] * pltpu.roll(ge, 1, 0) + w[1:2] * ge + w[2:3] * pltpu.roll(ge, ext - 1, 0))[edge:edge + tm]
        conv = conv + b_ref[...]
        gate_ref[...] = g_old
        up_ref[...] = u_old
        act_ref[...] = (conv * _sigmoid(conv) * u_old.astype(F32)).astype(BF16)
        g_tail[...] = g_old[tm - edge:tm]
        g_tile[...] = g_new
        u_tile[...] = u_new.astype(BF16)

    lag = pl.BlockSpec((tm, tn), lambda j, i: (jnp.maximum(i - 1, 0), j))
    return pl.pallas_call(
        body, name=name, grid=(f // tn, ni + 1),
        in_specs=[pl.BlockSpec((tm, d), lambda j, i: (jnp.minimum(i, ni - 1), 0)),
                  pl.BlockSpec((d, tn), lambda j, i: (0, j)), pl.BlockSpec((d, tn), lambda j, i: (0, j)),
                  pl.BlockSpec((3, tn), lambda j, i: (0, j)), pl.BlockSpec((1, tn), lambda j, i: (0, j))],
        out_specs=[lag, lag, lag],
        out_shape=[jax.ShapeDtypeStruct((s, f), F32), jax.ShapeDtypeStruct((s, f), BF16),
                   jax.ShapeDtypeStruct((s, f), BF16)],
        scratch_shapes=[pltpu.VMEM((tm, tn), F32), pltpu.VMEM((tm, tn), BF16), pltpu.VMEM((edge, tn), F32)],
        compiler_params=_params(("parallel", "arbitrary")),
    )(n2, w_gate, w_up, conv_w, conv_b)


def _ffn_mid_bwd(dh2, w_down, gate, up, conv_w, conv_b, name="ffn_mid_bwd"):
    s, d = dh2.shape
    f = gate.shape[1]
    tm, tn = FFN_ROWS, FFN_COLS
    ni = s // tm
    ext = tm + 2 * HALO
    per, last_halo = tm // HALO, s // HALO - 1

    def body(a_ref, wd_ref, gp, gc, gn, upp, upc, upn, w_ref, b_ref, dg_ref, du_ref, gw_ref, gb_ref,
             d_near, d_far, d_tail):
        i = pl.program_id(1)

        def multiply(cs):
            return _dot_nt(a_ref[...], wd_ref[cs, :])

        def finish(cs):
            before = jnp.where(i == 2, 0.0, d_tail[:, cs])
            after = jnp.where(i == ni + 1, 0.0, d_near[0:HALO, cs])
            de = jnp.concatenate([before, d_far[:, cs], after], axis=0)
            ge = _extended(gp, gc, gn, i - 2, s, tm, cs)
            ue = _extended(upp, upc, upn, i - 2, s, tm, cs)
            w = w_ref[:, cs]
            g_prev, g_next = pltpu.roll(ge, 1, 0), pltpu.roll(ge, ext - 1, 0)
            conv = w[0:1] * g_prev + w[1:2] * ge + w[2:3] * g_next + b_ref[:, cs]
            sg = _sigmoid(conv)
            inner = slice(HALO, HALO + tm)
            du_ref[:, cs] = (de * (conv * sg))[inner].astype(BF16)
            dconv = de * ue * (sg * (1.0 + conv * (1.0 - sg)))
            dgate = w[0:1] * pltpu.roll(dconv, ext - 1, 0) + w[1:2] * dconv + w[2:3] * pltpu.roll(dconv, 1, 0)
            dg_ref[:, cs] = dgate[inner].astype(BF16)
            dci = dconv[inner]
            gw_ref[0:1, cs] += jnp.sum(dci * g_prev[inner], axis=0, keepdims=True)
            gw_ref[1:2, cs] += jnp.sum(dci * ge[inner], axis=0, keepdims=True)
            gw_ref[2:3, cs] += jnp.sum(dci * g_next[inner], axis=0, keepdims=True)
            gb_ref[:, cs] += jnp.sum(dci, axis=0, keepdims=True)

        def rotate(new, chunks):
            d_tail[...] = d_far[tm - HALO:tm]
            d_far[...] = d_near[...]
            if new is not None:
                for cs, d_new in zip(chunks, new):
                    d_near[:, cs] = d_new

        def init():
            for r in (d_near, d_far, d_tail, gw_ref, gb_ref):
                r[...] = jnp.zeros_like(r)

        _lagged(i, ni, multiply, finish, rotate, init)

    def tile(i):
        return jnp.maximum(i - 2, 0)

    cur = pl.BlockSpec((tm, tn), lambda j, i: (tile(i), j))
    prev = pl.BlockSpec((HALO, tn), lambda j, i: (jnp.maximum(tile(i) * per - 1, 0), j))
    nxt = pl.BlockSpec((HALO, tn), lambda j, i: (jnp.minimum((tile(i) + 1) * per, last_halo), j))
    wspec = pl.BlockSpec((3, tn), lambda j, i: (0, j))
    bspec = pl.BlockSpec((1, tn), lambda j, i: (0, j))
    return pl.pallas_call(
        body, name=name, grid=(f // tn, ni + 2),
        in_specs=[pl.BlockSpec((tm, d), lambda j, i: (jnp.minimum(i, ni - 1), 0)),
                  pl.BlockSpec((tn, d), lambda j, i: (j, 0))] + [prev, cur, nxt] * 2 + [wspec, bspec],
        out_specs=[cur, cur, wspec, bspec],
        out_shape=[jax.ShapeDtypeStruct((s, f), BF16), jax.ShapeDtypeStruct((s, f), BF16),
                   jax.ShapeDtypeStruct((3, f), F32), jax.ShapeDtypeStruct((1, f), F32)],
        scratch_shapes=[pltpu.VMEM((tm, tn), F32), pltpu.VMEM((tm, tn), F32), pltpu.VMEM((HALO, tn), F32)],
        compiler_params=_params(("parallel", "arbitrary")),
    )(dh2, w_down, gate, gate, gate, up, up, up, conv_w, conv_b)


def _local_step(x, target, w, late_weights=None, grad_sink=None, first_dep=()):
    s = x.shape[0]
    tables = _rope_tables(s)
    uf, ub = _gate_matrices(w["gf_up"], w["gb_up"])
    if grad_sink is None:
        grad_sink = lambda names, grads: ()

    n1 = _rms_fwd(x, w["norm1_g"], "norm1")
    proj = _matmul([(n1, w["w_in"])], "nn", F32, 1024, 1280, D_MODEL, "in_proj", deps=first_dep)
    qkv = _rope_fwd(proj, tables)
    branches = [_attn_fwd(*qkv[di], d, f"attn_fwd_d{d}") for di, d in enumerate(DILATIONS)]
    o_mix, ao, lse = _attn_combine([b[0] for b in branches], [b[1] for b in branches], w["attn_norm_g"])
    g_f, g_b = _gla_gates(proj, uf, ub, w["gf_b"], w["gb_b"])
    o_f, st_f = _gla_fwd(proj, g_f, False, "gla_fwd_f")
    o_b, st_b = _gla_fwd(proj, g_b, True, "gla_fwd_b")
    cat = _gla_post(o_f, o_b, proj, w["gla_norm_g"], ao)
    if late_weights is not None:
        w = {**w, **late_weights(cat)}
    h1 = _matmul([(cat, w["w_out"])], "nn", F32, 512, 1024, D_MODEL, "out_proj", res=x)
    n2 = _rms_fwd(h1, w["norm2_g"], "norm2")
    gate, up, act = _ffn_in(n2, w["w_gate"], w["w_up"], w["conv_w"], w["conv_b"])
    h2 = _matmul([(act, w["w_down"])], "nn", F32, 1024, 1024, 2816, "ffn_down", res=h1)
    dh2, dh2_b, loss_acc, g_final = _final_loss(h2, target, w["final_norm_g"])

    g_w_down = _matmul([(act, dh2_b)], "tn", F32, 1408, 1024, 2048, "g_w_down")
    dep = grad_sink(["w_down"], [g_w_down])
    dgate, dup, g_conv_w, g_conv_b = _ffn_mid_bwd(dh2_b, w["w_down"], gate, up, w["conv_w"], w["conv_b"])
    g_w_gate = _matmul([(n2, dgate)], "tn", F32, 2048, 512, 2048, "g_w_gate", deps=dep)
    g_w_up = _matmul([(n2, dup)], "tn", F32, 2048, 512, 2048, "g_w_up")
    dep = grad_sink(["w_gate", "w_up"], [g_w_gate, g_w_up])
    dn2 = _matmul([(dgate, w["w_gate"])], "nt", F32, 1024, 1024, 2816, "d_n2_gate", deps=dep)
    dn2 = _matmul([(dup, w["w_up"])], "nt", F32, 1024, 1024, 2816, "d_n2_up", res=dn2)
    dh1, dh1_b, g_norm2 = _rms_bwd(dn2, h1, w["norm2_g"], dh2, "norm2_bwd")

    g_w_out = _matmul([(cat, dh1_b)], "tn", F32, 1024, 1024, 2048, "g_w_out")
    dep = grad_sink(["w_out"], [g_w_out])
    dcat = _matmul([(dh1_b, w["w_out"])], "nt", F32, 512, 1024, D_MODEL, "d_cat", deps=dep)
    do_attn, delta, g_attn_norm = _attn_prebwd(dcat, o_mix, w["attn_norm_g"])
    grads = [_attn_bwd(*qkv[di], do_attn[di], lse[di], delta[di], d, f"attn_bwd_d{d}")
             for di, d in enumerate(DILATIONS)]
    dproj = _rope_bwd(grads, tables)
    do_gla, dgr, g_gla_norm = _gla_post_bwd(dcat, o_f, o_b, proj, w["gla_norm_g"])
    dq_f, dk_f, dv_f, dg_f = _gla_bwd(proj, g_f, do_gla, st_f, False, "gla_bwd_f")
    dproj, dg_b = _gla_bwd(proj, g_b, do_gla, st_b, True, "gla_bwd_b", merge=(dq_f, dk_f, dv_f, dgr, dproj))
    dproj, g_uf, g_ub, g_gf_b, g_gb_b = _gla_gates_bwd(dg_f, dg_b, proj, uf, ub, w["gf_b"], w["gb_b"], dproj)
    g_w_in = _matmul([(n1, dproj)], "tn", F32, 1024, 1280, 2048, "g_w_in")
    dep = grad_sink(["w_in"], [g_w_in])
    dn1 = _matmul([(dproj, w["w_in"])], "nt", F32, 1024, 2048, 1280, "d_n1", deps=dep)
    grad_x, g_norm1 = _rms_bwd(dn1, x, w["norm1_g"], dh1, "norm1_bwd", bf16_copy=False)

    g = dict(norm1_g=g_norm1, w_in=g_w_in, gf_up=g_uf[:GLA_RANK], gf_b=g_gf_b,
             gb_up=g_ub[GLA_RANK:2 * GLA_RANK], gb_b=g_gb_b, gla_norm_g=g_gla_norm, attn_norm_g=g_attn_norm,
             w_out=g_w_out, norm2_g=g_norm2, w_gate=g_w_gate, w_up=g_w_up, conv_w=g_conv_w, conv_b=g_conv_b,
             w_down=g_w_down, final_norm_g=g_final)
    return loss_acc, grad_x, g


def _me_and_peers():
    x, y, c = lax.axis_index("x"), lax.axis_index("y"), lax.axis_index("c")
    me = 4 * x + 2 * y + c
    peers = []
    for kbits in range(1, N_DEV):
        px, py, pc = x ^ (kbits >> 2 & 1), y ^ (kbits >> 1 & 1), c ^ (kbits & 1)
        peers.append(((px, py, pc), 4 * px + 2 * py + pc))
    return me, peers


_HBM = pl.BlockSpec(memory_space=pltpu.HBM)
_SEM = pl.BlockSpec(memory_space=pltpu.SEMAPHORE)
_ANY = pl.BlockSpec(memory_space=pl.ANY)
_EFFECT = pltpu.SideEffectType.DATAFLOW_SIDE_EFFECTING


def _exchange_copies(src_refs, land_refs, send_sems, recv_sems, scatter):
    me, peers = _me_and_peers()
    out = []
    for a, (src, land) in enumerate(zip(src_refs, land_refs)):
        for kk, (dev, idx) in enumerate(peers):
            out.append(pltpu.make_async_remote_copy(
                src_ref=src.at[idx] if scatter else src, dst_ref=land.at[me],
                send_sem=send_sems.at[a * (N_DEV - 1) + kk], recv_sem=recv_sems.at[a * (N_DEV - 1) + kk],
                device_id=dev, device_id_type=MESH_ID))
    return out


def _exchange_start(srcs, lands, scatter, name, deps=()):
    n, nd = len(srcs), len(deps)

    def body(*refs):
        src_refs, land_refs = refs[:n], refs[n:2 * n]
        send_sems, recv_sems = refs[2 * n + nd:2 * n + nd + 2]
        token = refs[-1]
        for cp in _exchange_copies(src_refs, land_refs, send_sems, recv_sems, scatter):
            cp.start()
        token[...] = jnp.zeros_like(token)

    outs = pl.pallas_call(
        body, name=name,
        in_specs=[_HBM] * (2 * n) + [_ANY] * nd,
        out_specs=[_SEM, _SEM] + [_HBM] * (2 * n) + [pl.BlockSpec(memory_space=pltpu.VMEM)],
        out_shape=[pltpu.SemaphoreType.DMA((n * (N_DEV - 1),)), pltpu.SemaphoreType.DMA((n * (N_DEV - 1),))]
        + [pltpu.HBM(t.shape, t.dtype) for t in srcs] + [pltpu.HBM(t.shape, t.dtype) for t in lands]
        + [jax.ShapeDtypeStruct((SUBLANES, LANES), F32)],
        input_output_aliases={i: 2 + i for i in range(2 * n)},
        compiler_params=pltpu.CompilerParams(has_side_effects=_EFFECT),
    )(*[pltpu.with_memory_space_constraint(t, pltpu.HBM) for t in list(srcs) + list(lands)], *deps)
    send_sems, recv_sems = outs[0], outs[1]
    return dict(send=send_sems, recv=recv_sems, srcs=outs[2:2 + n], lands=outs[2 + n:2 + 2 * n],
                scatter=scatter, token=outs[-1])


def _exchange_wait(started, name, after):
    n = len(started["srcs"])
    scatter = started["scatter"]

    def body(*refs):
        src_refs, land_refs = refs[:n], refs[n:2 * n]
        send_sems, recv_sems = refs[2 * n], refs[2 * n + 1]
        for cp in _exchange_copies(src_refs, land_refs, send_sems, recv_sems, scatter):
            cp.wait_send()
            cp.wait_recv()

    outs = pl.pallas_call(
        body, name=name,
        in_specs=[_HBM] * (2 * n) + [_SEM, _SEM, _ANY],
        out_specs=[_HBM] * (2 * n),
        out_shape=[pltpu.HBM(t.shape, t.dtype) for t in started["srcs"]]
        + [pltpu.HBM(t.shape, t.dtype) for t in started["lands"]],
        input_output_aliases={i: i for i in range(2 * n)},
        compiler_params=pltpu.CompilerParams(has_side_effects=_EFFECT),
    )(*started["srcs"], *started["lands"], started["send"], started["recv"], after)
    return outs[:n], outs[n:]


def _all_gather_two_level(shard, name):
    def body(x_ref, out_ref, send_sems, recv_sems, local_sem):
        x, y, c = lax.axis_index("x"), lax.axis_index("y"), lax.axis_index("c")
        me, sibling = (x, y, c), (x, y, 1 - c)
        chips = [(1 - x, y), (x, 1 - y), (1 - x, 1 - y)]

        def slot(px, py, pc):
            return out_ref.at[4 * px + 2 * py + pc]

        def copy(k, block, to, src=None):
            return pltpu.make_async_remote_copy(
                src_ref=slot(*block) if src is None else src, dst_ref=slot(*block),
                send_sem=send_sems.at[k], recv_sem=recv_sems.at[k], device_id=to, device_id_type=MESH_ID)

        mine = pltpu.make_async_copy(x_ref, slot(*me), local_sem)
        mine.start()
        first = [copy(0, me, sibling, src=x_ref)]
        first += [copy(1 + j, me, (*chip, c), src=x_ref) for j, chip in enumerate(chips)]
        for cp in first:
            cp.start()
        passed = [copy(4 + j, (*chip, c), sibling) for j, chip in enumerate(chips)]
        for j, chip in enumerate(chips):
            copy(1 + j, (*chip, c), me).wait_recv()
            passed[j].start()
        copy(0, sibling, me).wait_recv()
        for j, chip in enumerate(chips):
            copy(4 + j, (*chip, 1 - c), me).wait_recv()
        for cp in first + passed:
            cp.wait_send()
        mine.wait()

    return pl.pallas_call(
        body, name=name,
        in_specs=[_ANY], out_specs=_ANY,
        out_shape=jax.ShapeDtypeStruct((N_DEV,) + shard.shape, shard.dtype),
        scratch_shapes=[pltpu.SemaphoreType.DMA((N_DEV - 1,)), pltpu.SemaphoreType.DMA((N_DEV - 1,)),
                        pltpu.SemaphoreType.DMA],
    )(shard)


def _all_gather_vmem(vec, name):
    r = vec.shape[0]

    def body(v_ref, o_ref, send_sems, recv_sems):
        me, peers = _me_and_peers()
        o_ref[me] = v_ref[...]
        sends = []
        for kk, (dev, _) in enumerate(peers):
            cp = pltpu.make_async_remote_copy(
                src_ref=v_ref, dst_ref=o_ref.at[me],
                send_sem=send_sems.at[kk], recv_sem=recv_sems.at[kk],
                device_id=dev, device_id_type=MESH_ID)
            cp.start()
            sends.append(cp)
        for kk, (dev, idx) in enumerate(peers):
            pltpu.make_async_remote_copy(
                src_ref=v_ref, dst_ref=o_ref.at[idx],
                send_sem=send_sems.at[kk], recv_sem=recv_sems.at[kk],
                device_id=dev, device_id_type=MESH_ID).wait_recv()
        for cp in sends:
            cp.wait_send()

    return pl.pallas_call(
        body, name=name,
        in_specs=[pl.BlockSpec(memory_space=pltpu.VMEM)],
        out_specs=pl.BlockSpec(memory_space=pltpu.VMEM),
        out_shape=jax.ShapeDtypeStruct((N_DEV, r, LANES), F32),
        scratch_shapes=[pltpu.SemaphoreType.DMA((N_DEV - 1,)), pltpu.SemaphoreType.DMA((N_DEV - 1,))],
        compiler_params=pltpu.CompilerParams(vmem_limit_bytes=VMEM_LIMIT),
    )(vec)


def _adamw_math(w, g, m, v):
    m = ADAM_B1 * m + (1.0 - ADAM_B1) * g
    v = ADAM_B2 * v + (1.0 - ADAM_B2) * (g * g)
    m_hat = m / (1.0 - ADAM_B1 ** ADAM_STEP)
    v_hat = v / (1.0 - ADAM_B2 ** ADAM_STEP)
    delta = -ADAM_LR * (m_hat / (jnp.sqrt(v_hat) + ADAM_EPS) + ADAM_WD * w)
    return delta, m, v


def _adamw_sum(parts, w, m, v, tr, name, own=None, me=None):
    r, c = w.shape

    def body(*refs):
        if own is None:
            p_ref, w_ref, m_ref, v_ref, g_ref, d_ref, nm_ref, nv_ref = refs
            terms = [p_ref[kk] for kk in range(N_DEV)]
        else:
            me_ref, p_ref, own_ref, w_ref, m_ref, v_ref, g_ref, d_ref, nm_ref, nv_ref = refs
            terms = [jnp.where(me_ref[0] == kk, own_ref[0], p_ref[kk]).astype(F32) for kk in range(N_DEV)]
        g = terms[0]
        for t in terms[1:]:
            g = g + t
        g_ref[...] = g
        d_ref[...], nm_ref[...], nv_ref[...] = _adamw_math(w_ref[...], g, m_ref[...], v_ref[...])

    out_shape = [jax.ShapeDtypeStruct((r, c), F32)] * 4
    if own is None:
        blk = pl.BlockSpec((tr, c), lambda i: (i, 0))
        return pl.pallas_call(
            body, name=name, grid=(r // tr,),
            in_specs=[pl.BlockSpec((N_DEV, tr, c), lambda i: (0, i, 0)), blk, blk, blk],
            out_specs=[blk] * 4, out_shape=out_shape,
            compiler_params=_params(("parallel",)),
        )(parts, w, m, v)
    blk = pl.BlockSpec((tr, c), lambda i, me_ref: (i, 0))
    return pl.pallas_call(
        body, name=name,
        grid_spec=pltpu.PrefetchScalarGridSpec(
            num_scalar_prefetch=1, grid=(r // tr,),
            in_specs=[pl.BlockSpec((N_DEV, tr, c), lambda i, me_ref: (0, i, 0)),
                      pl.BlockSpec((1, tr, c), lambda i, me_ref: (me_ref[0], i, 0)), blk, blk, blk],
            out_specs=[blk] * 4),
        out_shape=out_shape,
        compiler_params=_params(("parallel",)),
    )(jnp.reshape(me, (1,)).astype(jnp.int32), parts, own, w, m, v)


_SMALL = ("norm1_g", "gf_b", "gb_b", "gla_norm_g", "attn_norm_g", "norm2_g", "conv_b", "final_norm_g",
          "gf_up", "gb_up", "conv_w")


def _pack(named):
    flat = jnp.concatenate([jnp.ravel(t).astype(F32) for t in named])
    tile = SUBLANES * LANES
    total = -(-flat.shape[0] // tile) * tile
    return jnp.pad(flat, (0, total - flat.shape[0])).reshape(total // LANES, LANES)


def _unpack(packed, shapes):
    flat = packed.reshape(-1)
    out, off = [], 0
    for shp in shapes:
        size = int(np.prod(shp))
        out.append(flat[off:off + size].reshape(shp))
        off += size
    return out


def kernel(x, norm1_g, w_in, gf_up, gf_b, gb_up, gb_b, gla_norm_g, attn_norm_g, w_out, norm2_g, w_gate, w_up, conv_w, conv_b, w_down, final_norm_g, loss_target, m_norm1_g, m_w_in, m_gf_up, m_gf_b, m_gb_up, m_gb_b, m_gla_norm_g, m_attn_norm_g, m_w_out, m_norm2_g, m_w_gate, m_w_up, m_conv_w, m_conv_b, m_w_down, m_final_norm_g, v_norm1_g, v_w_in, v_gf_up, v_gf_b, v_gb_up, v_gb_b, v_gla_norm_g, v_attn_norm_g, v_w_out, v_norm2_g, v_w_gate, v_w_up, v_conv_w, v_conv_b, v_w_down, v_final_norm_g):
    names = ("norm1_g", "w_in", "gf_up", "gf_b", "gb_up", "gb_b", "gla_norm_g", "attn_norm_g", "w_out", "norm2_g",
             "w_gate", "w_up", "conv_w", "conv_b", "w_down", "final_norm_g")
    ws = dict(zip(names, (norm1_g, w_in, gf_up, gf_b, gb_up, gb_b, gla_norm_g, attn_norm_g, w_out, norm2_g,
                          w_gate, w_up, conv_w, conv_b, w_down, final_norm_g)))
    ms = dict(zip(names, (m_norm1_g, m_w_in, m_gf_up, m_gf_b, m_gb_up, m_gb_b, m_gla_norm_g, m_attn_norm_g, m_w_out,
                          m_norm2_g, m_w_gate, m_w_up, m_conv_w, m_conv_b, m_w_down, m_final_norm_g)))
    vs = dict(zip(names, (v_norm1_g, v_w_in, v_gf_up, v_gf_b, v_gb_up, v_gb_b, v_gla_norm_g, v_attn_norm_g, v_w_out,
                          v_norm2_g, v_w_gate, v_w_up, v_conv_w, v_conv_b, v_w_down, v_final_norm_g)))
    me = 4 * lax.axis_index("x") + 2 * lax.axis_index("y") + lax.axis_index("c")
    big = ("w_in", "w_out", "w_gate", "w_up", "w_down")
    col_sharded = ("w_in", "w_gate", "w_up")

    def gather_start(group, name, deps=()):
        shards = [ws[n][0].astype(BF16) for n in group]
        lands = [lax.empty((N_DEV,) + t.shape, BF16) for t in shards]
        return _exchange_start(shards, lands, False, name, deps)

    def gather_finish(group, started, name, after):
        full = {}
        for n, own, t in zip(group, *_exchange_wait(started, name, after)):
            t = lax.dynamic_update_slice(t, own[None], (me, 0, 0))
            if n in col_sharded:
                full[n] = jnp.transpose(t, (1, 0, 2)).reshape(t.shape[1], N_DEV * t.shape[2])
            else:
                full[n] = t.reshape(N_DEV * t.shape[1], t.shape[2])
        return full

    w_in_all = _all_gather_two_level(ws["w_in"][0].astype(BF16), "gather_w_in")
    full = {"w_in": jnp.pad(jnp.transpose(w_in_all, (1, 0, 2)).reshape(D_MODEL, IN_WIDTH),
                            ((0, 0), (0, IN_PAD - IN_WIDTH)))}
    late = ("w_out", "w_gate", "w_up", "w_down")
    started_b = gather_start(late, "gather_late_start", deps=(full["w_in"],))

    def late_weights(after):
        return gather_finish(late, started_b, "gather_late_wait", after)

    small_sharded = ("gf_up", "gb_up", "conv_w")
    sm = _all_gather_vmem(_pack([ws[n][0] for n in small_sharded]), "gather_small")
    shard_shapes = [ws[n][0].shape for n in small_sharded]
    per_dev = [_unpack(sm[d], shard_shapes) for d in range(N_DEV)]
    for i, n in enumerate(small_sharded):
        full[n] = jnp.concatenate([per_dev[d][i] for d in range(N_DEV)], axis=1)
    for n in ("norm1_g", "gf_b", "gb_b", "gla_norm_g", "attn_norm_g", "norm2_g", "conv_b"):
        full[n] = ws[n]
    full["final_norm_g"] = final_norm_g.reshape(1, D_MODEL)

    in_flight = []

    def grad_sink(group, grads):
        partials = []
        for n, t in zip(group, grads):
            if n == "w_in":
                t = t[:, :IN_WIDTH]
            t = t.astype(BF16)
            if n in col_sharded:
                t = jnp.transpose(t.reshape(t.shape[0], N_DEV, t.shape[1] // N_DEV), (1, 0, 2))
            else:
                t = t.reshape(N_DEV, t.shape[0] // N_DEV, t.shape[1])
            partials.append(t)
        lands = [lax.empty(t.shape, t.dtype) for t in partials]
        started = _exchange_start(partials, lands, True, "exchange_" + "_".join(group) + "_start")
        in_flight.append((group, started))
        return (started["token"],)

    loss_acc, grad_x, g = _local_step(x[0], loss_target[0], full, late_weights, grad_sink,
                                      first_dep=(started_b["token"],))

    out = {}
    for group, started in in_flight:
        sent, landed = _exchange_wait(started, "exchange_" + "_".join(group) + "_wait", grad_x)
        for n, parts, own in zip(group, landed, sent):
            out[n] = _adamw_sum(parts, ws[n][0], ms[n][0], vs[n][0], 64, "adamw_" + n, own=own, me=me)

    small_full_shapes = [g[n].shape for n in _SMALL]
    gsmall = _pack([g[n] for n in _SMALL] + [loss_acc[0:1, 0:1]])
    gathered_small = _all_gather_vmem(gsmall, "gather_small_grads")

    def full_small(d):
        parts = []
        for n in _SMALL:
            t = d[n].reshape(d[n].shape[-2:]) if d[n].ndim == 3 else d[n].reshape(1, -1)
            if n in small_sharded:
                wide = jnp.zeros((t.shape[0], t.shape[1] * N_DEV), F32)
                t = lax.dynamic_update_slice_in_dim(wide, t, me * t.shape[1], axis=1)
            parts.append(t)
        return _pack(parts + [jnp.zeros((1, 1), F32)])

    rows = gsmall.shape[0]
    res_small = _adamw_sum(gathered_small, full_small(ws), full_small(ms), full_small(vs), rows, "adamw_small")
    loss = res_small[0].reshape(-1)[sum(int(np.prod(sh)) for sh in small_full_shapes)]
    unpacked = [_unpack(t, small_full_shapes) for t in res_small]
    for i, n in enumerate(_SMALL):
        vals = [u[i] for u in unpacked]
        if n in small_sharded:
            width = vals[0].shape[1] // N_DEV
            vals = [lax.dynamic_slice_in_dim(t, me * width, width, axis=1) for t in vals]
        out[n] = vals

    result = [loss, grad_x[None]]
    for kind in range(4):
        for n in names:
            result.append(out[n][kind].reshape(ws[n].shape))
    return tuple(result)
```

```python
import functools

import numpy as np
import jax
import jax.numpy as jnp
from jax import lax
from jax.experimental import pallas as pl
from jax.experimental.pallas import tpu as pltpu

F32 = jnp.float32
BF16 = jnp.bfloat16

D_MODEL = 2048
ATTN_W = 1024
ATTN_HEADS = 8
HEAD_DIM = 128
ROPE_DIM = 32
ROPE_THETA = 500000.0
DILATIONS = (1, 4, 16)
N_SIDE = 64
GLA_KW = 512
GLA_VW = 1024
GLA_HEADS = 4
GLA_DK = 128
GLA_DV = 256
GLA_RANK = 16
GLA_GATE_NORM = 16.0
GLA_CHUNK = 64
IN_WIDTH = 6176
IN_PAD = 6400
D_FF = 5632
EPS = 1e-6
N_DEV = 8

OFF_AQ, OFF_AK, OFF_AV = 0, 1024, 2048
OFF_GQ, OFF_GK, OFF_GV, OFF_GR, OFF_Z = 3072, 3584, 4096, 5120, 6144

ADAM_LR, ADAM_B1, ADAM_B2, ADAM_EPS, ADAM_WD, ADAM_STEP = 0.001, 0.9, 0.999, 1e-08, 0.01, 10

LANES = 128
SUBLANES = 8
VMEM_LIMIT = 56 * 1024 * 1024
ROW_BLOCK = 256
ATTN_BLOCK = 128
GLA_CHUNKS_PER_STEP = 4
NEG = -1e30
MESH_ID = pl.DeviceIdType.MESH


def _params(sem):
    return pltpu.CompilerParams(dimension_semantics=sem, vmem_limit_bytes=VMEM_LIMIT)


def _dot(a, b):
    return lax.dot_general(a, b, (((1,), (0,)), ((), ())), preferred_element_type=F32)


def _dot_nt(a, b):
    return lax.dot_general(a, b, (((1,), (1,)), ((), ())), preferred_element_type=F32)


def _dot_tn(a, b):
    return lax.dot_general(a, b, (((0,), (0,)), ((), ())), preferred_element_type=F32)


def _sigmoid(x):
    return 0.5 * jnp.tanh(0.5 * x) + 0.5


def _matmul(pairs, mode, out_dtype, tm, tn, tk, name, res=None, deps=()):
    a0, b0 = pairs[0]
    if mode == "nn":
        (m, kdim), n = a0.shape, b0.shape[1]
    elif mode == "nt":
        (m, kdim), n = a0.shape, b0.shape[0]
    else:
        (kdim, m), n = a0.shape, b0.shape[1]
    assert m % tm == 0 and n % tn == 0 and kdim % tk == 0, (name, m, n, kdim)
    nk = kdim // tk
    npairs = len(pairs)
    steps = nk * npairs
    dot = {"nn": _dot, "nt": _dot_nt, "tn": _dot_tn}[mode]

    def kidx(p):
        return lambda k: jnp.clip(k - p * nk, 0, nk - 1)

    in_specs, args = [], []
    for p, (a, b) in enumerate(pairs):
        kk = kidx(p)
        if mode == "nn":
            in_specs += [pl.BlockSpec((tm, tk), lambda i, j, k, kk=kk: (i, kk(k))),
                         pl.BlockSpec((tk, tn), lambda i, j, k, kk=kk: (kk(k), j))]
        elif mode == "nt":
            in_specs += [pl.BlockSpec((tm, tk), lambda i, j, k, kk=kk: (i, kk(k))),
                         pl.BlockSpec((tn, tk), lambda i, j, k, kk=kk: (j, kk(k)))]
        else:
            in_specs += [pl.BlockSpec((tk, tm), lambda i, j, k, kk=kk: (kk(k), i)),
                         pl.BlockSpec((tk, tn), lambda i, j, k, kk=kk: (kk(k), j))]
        args += [a, b]
    if res is not None:
        in_specs.append(pl.BlockSpec((tm, tn), lambda i, j, k: (i, j)))
        args.append(res)
    in_specs += [pl.BlockSpec(memory_space=pl.ANY)] * len(deps)
    args += list(deps)

    def body(*refs):
        ab = refs[:2 * npairs]
        res_ref = refs[2 * npairs] if res is not None else None
        o_ref = refs[2 * npairs + (1 if res is not None else 0) + len(deps)]

        def finish(acc):
            if res_ref is not None:
                acc = acc + res_ref[...]
            o_ref[...] = acc.astype(out_dtype)

        if steps == 1:
            finish(dot(ab[0][...], ab[1][...]))
            return
        acc_ref = refs[-1]
        k = pl.program_id(2)

        @pl.when(k == 0)
        def _():
            acc_ref[...] = jnp.zeros_like(acc_ref)

        for p in range(npairs):
            @pl.when((k >= p * nk) & (k < (p + 1) * nk))
            def _(p=p):
                acc_ref[...] += dot(ab[2 * p][...], ab[2 * p + 1][...])

        @pl.when(k == steps - 1)
        def _():
            finish(acc_ref[...])

    return pl.pallas_call(
        body, name=name,
        grid=(m // tm, n // tn, steps),
        in_specs=in_specs,
        out_specs=pl.BlockSpec((tm, tn), lambda i, j, k: (i, j)),
        out_shape=jax.ShapeDtypeStruct((m, n), out_dtype),
        scratch_shapes=[] if steps == 1 else [pltpu.VMEM((tm, tn), F32)],
        compiler_params=_params(("parallel", "parallel", "arbitrary")),
    )(*args)


def _rms_fwd(x, g, name):
    s, d = x.shape

    def body(x_ref, g_ref, o_ref):
        xv = x_ref[...]
        r = lax.rsqrt(jnp.mean(xv * xv, axis=-1, keepdims=True) + EPS)
        o_ref[...] = (xv * r * g_ref[...]).astype(BF16)

    return pl.pallas_call(
        body, name=name, grid=(s // ROW_BLOCK,),
        in_specs=[pl.BlockSpec((ROW_BLOCK, d), lambda i: (i, 0)), pl.BlockSpec((1, d), lambda i: (0, 0))],
        out_specs=pl.BlockSpec((ROW_BLOCK, d), lambda i: (i, 0)),
        out_shape=jax.ShapeDtypeStruct((s, d), BF16),
        compiler_params=_params(("parallel",)),
    )(x, g)


def _rms_bwd(dn, x, g, dres, name, bf16_copy=True):
    s, d = x.shape

    def body(dn_ref, x_ref, g_ref, dres_ref, dx_ref, *rest):
        gg_ref = rest[-1]
        i = pl.program_id(0)
        xv, dnv = x_ref[...], dn_ref[...]
        r = lax.rsqrt(jnp.mean(xv * xv, axis=-1, keepdims=True) + EPS)
        dng = dnv * g_ref[...]
        c = jnp.mean(dng * xv, axis=-1, keepdims=True)
        dx = dres_ref[...] + r * dng - xv * (r * r * r * c)
        dx_ref[...] = dx
        if bf16_copy:
            rest[0][...] = dx.astype(BF16)

        @pl.when(i == 0)
        def _():
            gg_ref[...] = jnp.zeros_like(gg_ref)

        gg_ref[...] += jnp.sum(dnv * xv * r, axis=0, keepdims=True)

    row = pl.BlockSpec((ROW_BLOCK, d), lambda i: (i, 0))
    vec = pl.BlockSpec((1, d), lambda i: (0, 0))
    return pl.pallas_call(
        body, name=name, grid=(s // ROW_BLOCK,),
        in_specs=[row, row, vec, row],
        out_specs=[row] + [row] * bf16_copy + [vec],
        out_shape=[jax.ShapeDtypeStruct((s, d), F32)] + [jax.ShapeDtypeStruct((s, d), BF16)] * bf16_copy
        + [jax.ShapeDtypeStruct((1, d), F32)],
        compiler_params=_params(("arbitrary",)),
    )(dn, x, g, dres)


def _final_loss(h2, target, g, name="final_loss"):
    s, d = h2.shape

    def body(h_ref, t_ref, g_ref, dh_ref, dhb_ref, loss_ref, gg_ref):
        i = pl.program_id(0)
        hv, gv = h_ref[...], g_ref[...]
        r = lax.rsqrt(jnp.mean(hv * hv, axis=-1, keepdims=True) + EPS)
        e = hv * r * gv - t_ref[...]
        dy = e * (1.0 / d)
        dyg = dy * gv
        c = jnp.mean(dyg * hv, axis=-1, keepdims=True)
        dh = r * dyg - hv * (r * r * r * c)
        dh_ref[...] = dh
        dhb_ref[...] = dh.astype(BF16)

        @pl.when(i == 0)
        def _():
            gg_ref[...] = jnp.zeros_like(gg_ref)
            loss_ref[...] = jnp.zeros_like(loss_ref)

        gg_ref[...] += jnp.sum(dy * hv * r, axis=0, keepdims=True)
        loss_ref[...] += jnp.sum(jnp.sum(e * e, axis=-1, keepdims=True), axis=0, keepdims=True) * (0.5 / d)

    row = pl.BlockSpec((ROW_BLOCK, d), lambda i: (i, 0))
    vec = pl.BlockSpec((1, d), lambda i: (0, 0))
    return pl.pallas_call(
        body, name=name, grid=(s // ROW_BLOCK,),
        in_specs=[row, row, vec],
        out_specs=[row, row, pl.BlockSpec((SUBLANES, LANES), lambda i: (0, 0)), vec],
        out_shape=[jax.ShapeDtypeStruct((s, d), F32), jax.ShapeDtypeStruct((s, d), BF16),
                   jax.ShapeDtypeStruct((SUBLANES, LANES), F32), jax.ShapeDtypeStruct((1, d), F32)],
        compiler_params=_params(("arbitrary",)),
    )(h2, target, g)


def _rope_tables(s):
    pos = jnp.arange(s, dtype=F32)
    inv_freq = ROPE_THETA ** (-jnp.arange(0, ROPE_DIM, 2, dtype=F32) / ROPE_DIM)
    ang = pos[:, None] * inv_freq[None, :]
    cos, sin = jnp.cos(ang), jnp.sin(ang)
    half = ROPE_DIM // 2
    rest = HEAD_DIM - ROPE_DIM
    c = jnp.concatenate([cos, cos, jnp.ones((s, rest), F32)], axis=1)
    sm = jnp.concatenate([-sin, jnp.zeros((s, half + rest), F32)], axis=1)
    sp = jnp.concatenate([jnp.zeros((s, half), F32), sin, jnp.zeros((s, rest), F32)], axis=1)
    return c, sm, sp


def _res_shape(s, groups, dil, dtype):
    return jax.ShapeDtypeStruct((s // dil, dil * groups * LANES), dtype)


def _res_spec(groups, dil):
    return pl.BlockSpec((ROW_BLOCK // dil, dil * groups * LANES), lambda i: (i, 0))


def _to_residues(scr, o_ref, dil):
    groups, rows = scr.shape[0], ROW_BLOCK // dil
    for r in range(dil):
        for h in range(groups):
            piece = scr[h] if dil == 1 else scr.at[h][pl.ds(r, rows, stride=dil), :]
            o_ref[:, (r * groups + h) * LANES:(r * groups + h + 1) * LANES] = piece.astype(o_ref.dtype)


def _from_residues(i_ref, scr, dil):
    groups, rows = scr.shape[0], ROW_BLOCK // dil
    for r in range(dil):
        for h in range(groups):
            piece = i_ref[:, (r * groups + h) * LANES:(r * groups + h + 1) * LANES].astype(F32)
            if dil == 1:
                scr[h] = piece
            else:
                scr.at[h][pl.ds(r, rows, stride=dil), :] = piece


def _rope_fwd(proj, tables, name="rope_fwd"):
    s = proj.shape[0]
    half = ROPE_DIM // 2
    nd = len(DILATIONS)

    def body(p_ref, c_ref, sm_ref, sp_ref, *rest):
        outs, scr = rest[:3 * nd], rest[3 * nd]
        c, sm, sp = c_ref[...], sm_ref[...], sp_ref[...]
        for gi, off in enumerate((OFF_AQ, OFF_AK, OFF_AV)):
            for h in range(ATTN_HEADS):
                t = p_ref[:, off + h * HEAD_DIM: off + (h + 1) * HEAD_DIM]
                if off != OFF_AV:
                    t = t * c + pltpu.roll(t, HEAD_DIM - half, 1) * sm + pltpu.roll(t, half, 1) * sp
                scr[h] = t
            for di, dil in enumerate(DILATIONS):
                _to_residues(scr, outs[3 * di + gi], dil)

    tab = pl.BlockSpec((ROW_BLOCK, HEAD_DIM), lambda i: (i, 0))
    outs = pl.pallas_call(
        body, name=name, grid=(s // ROW_BLOCK,),
        in_specs=[pl.BlockSpec((ROW_BLOCK, 3 * ATTN_W), lambda i: (i, 0)), tab, tab, tab],
        out_specs=[_res_spec(ATTN_HEADS, d) for d in DILATIONS for _ in range(3)],
        out_shape=[_res_shape(s, ATTN_HEADS, d, BF16) for d in DILATIONS for _ in range(3)],
        scratch_shapes=[pltpu.VMEM((ATTN_HEADS, ROW_BLOCK, LANES), F32)],
        compiler_params=_params(("parallel",)),
    )(proj, *tables)
    return [tuple(outs[3 * di:3 * di + 3]) for di in range(nd)]


def _rope_bwd(grads, tables, name="rope_bwd"):
    s = grads[0][0].shape[0] * DILATIONS[0]
    half = ROPE_DIM // 2
    nd = len(DILATIONS)

    def body(*refs):
        ins = refs[:3 * nd]
        c_ref, sm_ref, sp_ref, o_ref = refs[3 * nd:3 * nd + 4]
        scrs = refs[3 * nd + 4:]
        c, sm, sp = c_ref[...], sm_ref[...], sp_ref[...]
        for gi, off in enumerate((OFF_AQ, OFF_AK, OFF_AV)):
            for di, dil in enumerate(DILATIONS):
                _from_residues(ins[3 * di + gi], scrs[di], dil)
            for h in range(ATTN_HEADS):
                t = scrs[0][h]
                for scr in scrs[1:]:
                    t = t + scr[h]
                if off != OFF_AV:
                    t = t * c + pltpu.roll(t * sm, half, 1) + pltpu.roll(t * sp, HEAD_DIM - half, 1)
                o_ref[:, off + h * HEAD_DIM: off + (h + 1) * HEAD_DIM] = t.astype(BF16)

    tab = pl.BlockSpec((ROW_BLOCK, HEAD_DIM), lambda i: (i, 0))
    return pl.pallas_call(
        body, name=name, grid=(s // ROW_BLOCK,),
        in_specs=[_res_spec(ATTN_HEADS, d) for d in DILATIONS for _ in range(3)] + [tab, tab, tab],
        out_specs=pl.BlockSpec((ROW_BLOCK, 3 * ATTN_W), lambda i: (i, 0)),
        out_shape=jax.ShapeDtypeStruct((s, IN_PAD), BF16),
        scratch_shapes=[pltpu.VMEM((ATTN_HEADS, ROW_BLOCK, LANES), F32) for _ in DILATIONS],
        compiler_params=_params(("parallel",)),
    )(*[t for g in grads for t in g], *tables)


ATTN_GROUP = 2


def _window_specs(nsteps, width):
    rows, hb = ATTN_GROUP * ATTN_BLOCK, N_SIDE
    per = rows // hb
    cur = pl.BlockSpec((rows, width), lambda r, j: (j, r))
    prev = pl.BlockSpec((hb, width), lambda r, j: (jnp.maximum(per * j - 1, 0), r))
    nxt = pl.BlockSpec((hb, width), lambda r, j: (jnp.minimum(per * (j + 1), per * nsteps - 1), r))
    return prev, cur, nxt


def _block(ref, b, sl):
    return ref[b * ATTN_BLOCK:(b + 1) * ATTN_BLOCK, sl]


def _edge(prev_ref, cur_ref, next_ref, b, sl):
    qb, hb = ATTN_BLOCK, N_SIDE
    before = prev_ref[:, sl] if b == 0 else cur_ref[b * qb - hb:b * qb, sl]
    after = next_ref[:, sl] if b == ATTN_GROUP - 1 else cur_ref[(b + 1) * qb:(b + 1) * qb + hb, sl]
    return jnp.concatenate([before, after], axis=0)


def _band_masks(j, length):
    qb, hb = ATTN_BLOCK, N_SIDE
    row = lax.broadcasted_iota(jnp.int32, (qb, qb), 0)
    col = lax.broadcasted_iota(jnp.int32, (qb, qb), 1)

    def edge_pos(i):
        return j * qb - hb + i + jnp.where(i >= hb, qb, 0)

    def ok(a, b, outside):
        return (jnp.abs(a - b) <= N_SIDE) & (outside >= 0) & (outside < length)

    cur = jnp.abs(row - col) <= N_SIDE
    edge_k = ok(j * qb + row, edge_pos(col), edge_pos(col))
    edge_q = ok(edge_pos(row), j * qb + col, edge_pos(row))
    return cur, edge_k, edge_q


def _attn_fwd(q, k, v, dil, name):
    length = q.shape[0]
    qb = ATTN_BLOCK
    nsteps = length // (ATTN_GROUP * qb)
    scale = HEAD_DIM ** -0.5

    def body(q_ref, kp_ref, kc_ref, kn_ref, vp_ref, vc_ref, vn_ref, o_ref, lse_ref):
        masks = [_band_masks(pl.program_id(1) * ATTN_GROUP + b, length) for b in range(ATTN_GROUP)]
        lane = lax.broadcasted_iota(jnp.int32, (qb, LANES), 1)
        units = [(b, h, slice(h * HEAD_DIM, (h + 1) * HEAD_DIM)) for b in range(ATTN_GROUP)
                 for h in range(ATTN_HEADS)]
        scores = [(_dot_nt(_block(q_ref, b, sl), _block(kc_ref, b, sl)),
                   _dot_nt(_block(q_ref, b, sl), _edge(kp_ref, kc_ref, kn_ref, b, sl))) for b, _, sl in units]
        probs = []
        lse_acc = [jnp.zeros((qb, LANES), F32) for _ in range(ATTN_GROUP)]
        for (b, h, _), (s_c, s_e) in zip(units, scores):
            valid_c, valid_e, _ = masks[b]
            s_c = jnp.where(valid_c, s_c * scale, NEG)
            s_e = jnp.where(valid_e, s_e * scale, NEG)
            m = jnp.max(jnp.maximum(s_c, s_e), axis=-1, keepdims=True)
            p_c, p_e = jnp.exp(s_c - m), jnp.exp(s_e - m)
            den = jnp.sum(p_c + p_e, axis=-1, keepdims=True)
            probs.append((p_c.astype(BF16), p_e.astype(BF16), 1.0 / den))
            lse_acc[b] = jnp.where(lane == h, m + jnp.log(den), lse_acc[b])
        for (b, _, sl), (p_c, p_e, inv) in zip(units, probs):
            o_ref[b * qb:(b + 1) * qb, sl] = (_dot(p_c, _block(vc_ref, b, sl))
                                              + _dot(p_e, _edge(vp_ref, vc_ref, vn_ref, b, sl))) * inv
        for b in range(ATTN_GROUP):
            lse_ref[b * qb:(b + 1) * qb, :] = lse_acc[b]

    prev, cur, nxt = _window_specs(nsteps, ATTN_W)
    return pl.pallas_call(
        body, name=name, grid=(dil, nsteps),
        in_specs=[cur, prev, cur, nxt, prev, cur, nxt],
        out_specs=[cur, pl.BlockSpec((ATTN_GROUP * qb, LANES), lambda r, j: (j, r))],
        out_shape=[jax.ShapeDtypeStruct((length, dil * ATTN_W), F32),
                   jax.ShapeDtypeStruct((length, dil * LANES), F32)],
        compiler_params=_params(("parallel", "parallel")),
    )(q, k, k, k, v, v, v)


def _attn_combine(outs, lses, g, name="attn_combine"):
    s = outs[0].shape[0] * DILATIONS[0]
    nd = len(DILATIONS)

    def body(*refs):
        o_refs, l_refs = refs[:nd], refs[nd:2 * nd]
        g_ref, o_ref, n_ref = refs[2 * nd:2 * nd + 3]
        lse_outs = refs[2 * nd + 3:3 * nd + 3]
        o_scr, l_scr = refs[3 * nd + 3:4 * nd + 3], refs[4 * nd + 3:5 * nd + 3]
        for di, dil in enumerate(DILATIONS):
            _from_residues(o_refs[di], o_scr[di], dil)
            _from_residues(l_refs[di], l_scr[di], dil)
        ls = [scr[0] for scr in l_scr]
        m = ls[0]
        for l in ls[1:]:
            m = jnp.maximum(m, l)
        es = [jnp.exp(l - m) for l in ls]
        z = es[0]
        for e in es[1:]:
            z = z + e
        ws = [e / z for e in es]
        l_scr[0][0] = m + jnp.log(z)
        for di, dil in enumerate(DILATIONS):
            _to_residues(l_scr[0], lse_outs[di], dil)
        ssq = jnp.zeros((ROW_BLOCK, 1), F32)
        for h in range(ATTN_HEADS):
            sl = slice(h * HEAD_DIM, (h + 1) * HEAD_DIM)
            acc = ws[0][:, h:h + 1] * o_scr[0][h]
            for w, scr in zip(ws[1:], o_scr[1:]):
                acc = acc + w[:, h:h + 1] * scr[h]
            o_ref[:, sl] = acc
            ssq = ssq + jnp.sum(acc * acc, axis=-1, keepdims=True)
        r = lax.rsqrt(ssq * (1.0 / ATTN_W) + EPS)
        n_ref[...] = (o_ref[...] * r * g_ref[...]).astype(BF16)

    blk = pl.BlockSpec((ROW_BLOCK, ATTN_W), lambda i: (i, 0))
    outs_ = pl.pallas_call(
        body, name=name, grid=(s // ROW_BLOCK,),
        in_specs=[_res_spec(ATTN_HEADS, d) for d in DILATIONS] + [_res_spec(1, d) for d in DILATIONS]
        + [pl.BlockSpec((1, ATTN_W), lambda i: (0, 0))],
        out_specs=[blk, blk] + [_res_spec(1, d) for d in DILATIONS],
        out_shape=[jax.ShapeDtypeStruct((s, ATTN_W), F32), jax.ShapeDtypeStruct((s, D_MODEL), BF16)]
        + [_res_shape(s, 1, d, F32) for d in DILATIONS],
        scratch_shapes=[pltpu.VMEM((ATTN_HEADS, ROW_BLOCK, LANES), F32) for _ in DILATIONS]
        + [pltpu.VMEM((1, ROW_BLOCK, LANES), F32) for _ in DILATIONS],
        compiler_params=_params(("parallel",)),
    )(*outs, *lses, g)
    return outs_[0], outs_[1], list(outs_[2:])


def _attn_prebwd(dcat, o, g, name="attn_prebwd"):
    s = o.shape[0]
    nd = len(DILATIONS)

    def body(dy_ref, o_ref, g_ref, *rest):
        do_outs, delta_outs, gg_ref = rest[:nd], rest[nd:2 * nd], rest[2 * nd]
        do_scr, delta_scr = rest[2 * nd + 1], rest[2 * nd + 2]
        i = pl.program_id(0)
        dy, ov = dy_ref[...], o_ref[...]
        r = lax.rsqrt(jnp.mean(ov * ov, axis=-1, keepdims=True) + EPS)
        dyg = dy * g_ref[...]
        c = jnp.mean(dyg * ov, axis=-1, keepdims=True)
        do = r * dyg - ov * (r * r * r * c)
        prod = do * ov
        lane = lax.broadcasted_iota(jnp.int32, (ROW_BLOCK, LANES), 1)
        acc = jnp.zeros((ROW_BLOCK, LANES), F32)
        for h in range(ATTN_HEADS):
            sl = slice(h * HEAD_DIM, (h + 1) * HEAD_DIM)
            do_scr[h] = do[:, sl]
            acc = jnp.where(lane == h, jnp.sum(prod[:, sl], axis=-1, keepdims=True), acc)
        delta_scr[0] = acc
        for di, dil in enumerate(DILATIONS):
            _to_residues(do_scr, do_outs[di], dil)
            _to_residues(delta_scr, delta_outs[di], dil)

        @pl.when(i == 0)
        def _():
            gg_ref[...] = jnp.zeros_like(gg_ref)

        gg_ref[...] += jnp.sum(dy * ov * r, axis=0, keepdims=True)

    blk = pl.BlockSpec((ROW_BLOCK, ATTN_W), lambda i: (i, 0))
    vec = pl.BlockSpec((1, ATTN_W), lambda i: (0, 0))
    outs = pl.pallas_call(
        body, name=name, grid=(s // ROW_BLOCK,),
        in_specs=[blk, blk, vec],
        out_specs=[_res_spec(ATTN_HEADS, d) for d in DILATIONS] + [_res_spec(1, d) for d in DILATIONS] + [vec],
        out_shape=[_res_shape(s, ATTN_HEADS, d, BF16) for d in DILATIONS]
        + [_res_shape(s, 1, d, F32) for d in DILATIONS] + [jax.ShapeDtypeStruct((1, ATTN_W), F32)],
        scratch_shapes=[pltpu.VMEM((ATTN_HEADS, ROW_BLOCK, LANES), F32), pltpu.VMEM((1, ROW_BLOCK, LANES), F32)],
        compiler_params=_params(("arbitrary",)),
    )(dcat, o, g)
    return list(outs[:nd]), list(outs[nd:2 * nd]), outs[2 * nd]


def _attn_bwd(q, k, v, do, lse, delta, dil, name):
    length = q.shape[0]
    qb = ATTN_BLOCK
    nsteps = length // (ATTN_GROUP * qb)
    scale = HEAD_DIM ** -0.5

    def body(qp, qc, qn, kp, kc, kn, vp, vc, vn, dop, doc, don, lp, lc, ln, dp, dc, dn, dq_ref, dk_ref, dv_ref):
        masks = [_band_masks(pl.program_id(1) * ATTN_GROUP + b, length) for b in range(ATTN_GROUP)]
        everything = slice(None)
        lse_e = [_edge(lp, lc, ln, b, everything) for b in range(ATTN_GROUP)]
        del_e = [_edge(dp, dc, dn, b, everything) for b in range(ATTN_GROUP)]
        units = [(b, h, slice(h * HEAD_DIM, (h + 1) * HEAD_DIM)) for b in range(ATTN_GROUP)
                 for h in range(ATTN_HEADS)]
        prods = []
        for b, _, sl in units:
            q_c, k_c, v_c, do_c = _block(qc, b, sl), _block(kc, b, sl), _block(vc, b, sl), _block(doc, b, sl)
            q_e, k_e = _edge(qp, qc, qn, b, sl), _edge(kp, kc, kn, b, sl)
            v_e, do_e = _edge(vp, vc, vn, b, sl), _edge(dop, doc, don, b, sl)
            prods.append((_dot_nt(q_c, k_c), _dot_nt(do_c, v_c), _dot_nt(q_c, k_e), _dot_nt(do_c, v_e),
                          _dot_nt(q_e, k_c), _dot_nt(do_e, v_c)))
        parts = []
        for (b, h, _), (s_cc, dp_cc, s_ek, dp_ek, s_eq, dp_eq) in zip(units, prods):
            valid_c, valid_ek, valid_eq = masks[b]
            hc = slice(h, h + 1)
            lse_c, del_c = _block(lc, b, hc), _block(dc, b, hc)
            p_cc = jnp.where(valid_c, jnp.exp(s_cc * scale - lse_c), 0.0)
            ds_cc = (p_cc * (dp_cc - del_c)).astype(BF16)
            p_ek = jnp.where(valid_ek, jnp.exp(s_ek * scale - lse_c), 0.0)
            ds_ek = (p_ek * (dp_ek - del_c)).astype(BF16)
            p_eq = jnp.where(valid_eq, jnp.exp(s_eq * scale - lse_e[b][:, hc]), 0.0)
            ds_eq = (p_eq * (dp_eq - del_e[b][:, hc])).astype(BF16)
            parts.append((p_cc.astype(BF16), ds_cc, ds_ek, p_eq.astype(BF16), ds_eq))
        for (b, _, sl), (p_cc, ds_cc, ds_ek, p_eq, ds_eq) in zip(units, parts):
            rows = slice(b * qb, (b + 1) * qb)
            q_c, k_c, do_c = _block(qc, b, sl), _block(kc, b, sl), _block(doc, b, sl)
            q_e, k_e, do_e = _edge(qp, qc, qn, b, sl), _edge(kp, kc, kn, b, sl), _edge(dop, doc, don, b, sl)
            dq_ref[rows, sl] = ((_dot(ds_cc, k_c) + _dot(ds_ek, k_e)) * scale).astype(BF16)
            dk_ref[rows, sl] = ((_dot_tn(ds_cc, q_c) + _dot_tn(ds_eq, q_e)) * scale).astype(BF16)
            dv_ref[rows, sl] = (_dot_tn(p_cc, do_c) + _dot_tn(p_eq, do_e)).astype(BF16)

    wide, narrow = list(_window_specs(nsteps, ATTN_W)), list(_window_specs(nsteps, LANES))
    return tuple(pl.pallas_call(
        body, name=name, grid=(dil, nsteps),
        in_specs=wide * 4 + narrow * 2,
        out_specs=[wide[1]] * 3,
        out_shape=[jax.ShapeDtypeStruct((length, dil * ATTN_W), BF16)] * 3,
        compiler_params=_params(("parallel", "parallel")),
    )(q, q, q, k, k, k, v, v, v, do, do, do, lse, lse, lse, delta, delta, delta))


def _gate_matrices(gf_up, gb_up):
    pad = LANES - 2 * GLA_RANK
    uf = jnp.concatenate([gf_up, jnp.zeros((GLA_RANK + pad, GLA_KW), gf_up.dtype)], axis=0)
    ub = jnp.concatenate([jnp.zeros((GLA_RANK, GLA_KW), gb_up.dtype), gb_up, jnp.zeros((pad, GLA_KW), gb_up.dtype)], axis=0)
    return uf.astype(BF16), ub.astype(BF16)


def _log_sigmoid(x):
    return jnp.minimum(x, 0.0) - jnp.log(1.0 + jnp.exp(-jnp.abs(x)))


def _gla_gates(proj, uf, ub, gf_b, gb_b, name="gla_gates"):
    s = proj.shape[0]

    def body(z_ref, uf_ref, ub_ref, bf_ref, bb_ref, gf_ref, gb_ref):
        z = z_ref[...].astype(BF16)
        gf_ref[...] = _log_sigmoid(_dot(z, uf_ref[...]) + bf_ref[...]) * (1.0 / GLA_GATE_NORM)
        gb_ref[...] = _log_sigmoid(_dot(z, ub_ref[...]) + bb_ref[...]) * (1.0 / GLA_GATE_NORM)

    mat = pl.BlockSpec((LANES, GLA_KW), lambda i: (0, 0))
    vec = pl.BlockSpec((1, GLA_KW), lambda i: (0, 0))
    out = pl.BlockSpec((ROW_BLOCK, GLA_KW), lambda i: (i, 0))
    return pl.pallas_call(
        body, name=name, grid=(s // ROW_BLOCK,),
        in_specs=[pl.BlockSpec((ROW_BLOCK, LANES), lambda i: (i, OFF_Z // LANES)), mat, mat, vec, vec],
        out_specs=[out, out],
        out_shape=[jax.ShapeDtypeStruct((s, GLA_KW), F32)] * 2,
        compiler_params=_params(("parallel",)),
    )(proj, uf, ub, gf_b, gb_b)


def _gla_gates_bwd(dgf, dgb, proj, uf, ub, gf_b, gb_b, dproj, name="gla_gates_bwd"):
    s = proj.shape[0]
    tail = IN_PAD - OFF_Z

    def body(dgf_ref, dgb_ref, z_ref, uf_ref, ub_ref, bf_ref, bb_ref, _, dz_ref, guf_ref, gub_ref, gbf_ref, gbb_ref):
        i = pl.program_id(0)
        z = z_ref[...].astype(BF16)
        uf_, ub_ = uf_ref[...], ub_ref[...]
        dpf = dgf_ref[...] * (1.0 / GLA_GATE_NORM) * _sigmoid(-(_dot(z, uf_) + bf_ref[...]))
        dpb = dgb_ref[...] * (1.0 / GLA_GATE_NORM) * _sigmoid(-(_dot(z, ub_) + bb_ref[...]))
        dpf_b, dpb_b = dpf.astype(BF16), dpb.astype(BF16)
        dz_ref[:, 0:LANES] = (_dot_nt(dpf_b, uf_) + _dot_nt(dpb_b, ub_)).astype(BF16)
        dz_ref[:, LANES:tail] = jnp.zeros((ROW_BLOCK, tail - LANES), BF16)

        @pl.when(i == 0)
        def _():
            for r in (guf_ref, gub_ref, gbf_ref, gbb_ref):
                r[...] = jnp.zeros_like(r)

        guf_ref[...] += _dot_tn(z, dpf_b)
        gub_ref[...] += _dot_tn(z, dpb_b)
        gbf_ref[...] += jnp.sum(dpf, axis=0, keepdims=True)
        gbb_ref[...] += jnp.sum(dpb, axis=0, keepdims=True)

    mat = pl.BlockSpec((LANES, GLA_KW), lambda i: (0, 0))
    vec = pl.BlockSpec((1, GLA_KW), lambda i: (0, 0))
    blk = pl.BlockSpec((ROW_BLOCK, GLA_KW), lambda i: (i, 0))
    return pl.pallas_call(
        body, name=name, grid=(s // ROW_BLOCK,),
        in_specs=[blk, blk, pl.BlockSpec((ROW_BLOCK, LANES), lambda i: (i, OFF_Z // LANES)), mat, mat, vec, vec,
                  pl.BlockSpec(memory_space=pl.ANY)],
        out_specs=[pl.BlockSpec((ROW_BLOCK, tail), lambda i: (i, OFF_Z // tail)), mat, mat, vec, vec],
        out_shape=[jax.ShapeDtypeStruct(dproj.shape, dproj.dtype), jax.ShapeDtypeStruct((LANES, GLA_KW), F32),
                   jax.ShapeDtypeStruct((LANES, GLA_KW), F32), jax.ShapeDtypeStruct((1, GLA_KW), F32),
                   jax.ShapeDtypeStruct((1, GLA_KW), F32)],
        input_output_aliases={7: 0},
        compiler_params=_params(("arbitrary",)),
    )(dgf, dgb, proj, uf, ub, gf_b, gb_b, dproj)


def _split3(x):
    x1 = x.astype(BF16)
    r1 = x - x1.astype(F32)
    x2 = r1.astype(BF16)
    x3 = (r1 - x2.astype(F32)).astype(BF16)
    return x1, x2, x3


def _dot_exact(mask_bf, x):
    x1, x2, x3 = _split3(x)
    return _dot(mask_bf, x1) + _dot(mask_bf, x2) + _dot(mask_bf, x3)


def _chunk_masks(reverse):
    c = GLA_CHUNK
    row = lax.broadcasted_iota(jnp.int32, (c, c), 0)
    col = lax.broadcasted_iota(jnp.int32, (c, c), 1)
    allowed = (col >= row) if reverse else (col <= row)
    seen_by = (col <= row) if reverse else (col >= row)
    return allowed, seen_by


def _chunk_terms(q_ref, k_ref, g_ref, rs, hs, allowed, reverse):
    c = GLA_CHUNK
    mid, last = (c // 2, 0) if reverse else (c // 2 - 1, c - 1)
    q = q_ref[rs, hs] * (GLA_DK ** -0.5)
    k = k_ref[rs, hs]
    b = _dot_exact(jnp.where(allowed, 1.0, 0.0).astype(BF16), g_ref[rs, hs])
    bref, blast = b[mid:mid + 1, :], b[last:last + 1, :]
    e_q, e_k, e_in, e_st = jnp.exp(b - bref), jnp.exp(bref - b), jnp.exp(b), jnp.exp(blast - b)
    return dict(last=last, e_q=e_q, e_k=e_k, e_in=e_in, e_st=e_st,
                dec=jnp.exp(blast), qe=q * e_q, ke=k * e_k, qin=q * e_in, kst=k * e_st)


def _gla_blockspecs(s, reverse_order):
    cb = GLA_CHUNKS_PER_STEP
    rows = cb * GLA_CHUNK
    nsteps = s // rows

    def rb(n):
        return (nsteps - 1 - n) if reverse_order else n

    qspec = pl.BlockSpec((rows, GLA_KW), lambda n: (rb(n), OFF_GQ // GLA_KW))
    kspec = pl.BlockSpec((rows, GLA_KW), lambda n: (rb(n), OFF_GK // GLA_KW))
    vspec = pl.BlockSpec((rows, GLA_VW), lambda n: (rb(n), OFF_GV // GLA_VW))
    gspec = pl.BlockSpec((rows, GLA_KW), lambda n: (rb(n), 0))
    ospec = pl.BlockSpec((rows, GLA_VW), lambda n: (rb(n), 0))
    sspec = pl.BlockSpec((GLA_HEADS, cb, GLA_DV, GLA_DK), lambda n: (0, rb(n), 0, 0))
    return cb, rows, nsteps, qspec, kspec, vspec, gspec, ospec, sspec


def _gla_units(cb, order_reversed):
    chunks = list(reversed(range(cb))) if order_reversed else list(range(cb))
    return [(c, h, slice(c * GLA_CHUNK, (c + 1) * GLA_CHUNK), slice(h * GLA_DK, (h + 1) * GLA_DK),
             slice(h * GLA_DV, (h + 1) * GLA_DV)) for c in chunks for h in range(GLA_HEADS)]


def _gla_fwd(proj, g, reverse, name):
    s = proj.shape[0]
    cb, rows, nsteps, qspec, kspec, vspec, gspec, ospec, sspec = _gla_blockspecs(s, reverse)

    def body(q_ref, k_ref, v_ref, g_ref, o_ref, st_ref, state):
        @pl.when(pl.program_id(0) == 0)
        def _():
            state[...] = jnp.zeros_like(state)

        allowed, _ = _chunk_masks(reverse)
        units = _gla_units(cb, reverse)
        terms = [_chunk_terms(q_ref, k_ref, g_ref, rs, hs, allowed, reverse) for _, _, rs, hs, _ in units]
        vals = [v_ref[rs, vs].astype(BF16) for _, _, rs, _, vs in units]
        raw = [(_dot_nt(t["qe"].astype(BF16), t["ke"].astype(BF16)), _dot_tn(v, t["kst"].astype(BF16)))
               for t, v in zip(terms, vals)]
        intra = [_dot(jnp.where(allowed, a, 0.0).astype(BF16), v) for (a, _), v in zip(raw, vals)]
        st = [state[h] for h in range(GLA_HEADS)]
        for (c, h, rs, _, vs), t, (_, kv), o_in in zip(units, terms, raw, intra):
            st_ref[h, c] = st[h]
            o_ref[rs, vs] = o_in + _dot_nt(t["qin"].astype(BF16), st[h].astype(BF16))
            st[h] = st[h] * t["dec"] + kv
        for h in range(GLA_HEADS):
            state[h] = st[h]

    return pl.pallas_call(
        body, name=name, grid=(nsteps,),
        in_specs=[qspec, kspec, vspec, gspec],
        out_specs=[ospec, sspec],
        out_shape=[jax.ShapeDtypeStruct((s, GLA_VW), F32),
                   jax.ShapeDtypeStruct((GLA_HEADS, s // GLA_CHUNK, GLA_DV, GLA_DK), F32)],
        scratch_shapes=[pltpu.VMEM((GLA_HEADS, GLA_DV, GLA_DK), F32)],
        compiler_params=_params(("arbitrary",)),
    )(proj, proj, proj, g)


def _gla_bwd(proj, g, do, states, reverse, name, merge=None):
    s = proj.shape[0]
    cb, rows, nsteps, qspec, kspec, vspec, gspec, ospec, sspec = _gla_blockspecs(s, not reverse)
    gla_cols = OFF_Z - OFF_GQ

    def body(q_ref, k_ref, v_ref, g_ref, do_ref, sp_ref, *rest):
        if merge is None:
            dq_ref, dk_ref, dv_ref, dg_ref, dstate = rest
        else:
            dq_o, dk_o, dv_o, dgr_ref, _, dp_ref, dg_ref, dstate = rest
        @pl.when(pl.program_id(0) == 0)
        def _():
            dstate[...] = jnp.zeros_like(dstate)

        allowed, seen_by = _chunk_masks(reverse)
        units = _gla_units(cb, not reverse)
        terms = [_chunk_terms(q_ref, k_ref, g_ref, rs, hs, allowed, reverse) for _, _, rs, hs, _ in units]
        vals = [v_ref[rs, vs].astype(BF16) for _, _, rs, _, vs in units]
        dos = [do_ref[rs, vs] for _, _, rs, _, vs in units]
        prevs = [sp_ref[h, c] for c, h, _, _, _ in units]
        raw = [(_dot_nt(t["qe"].astype(BF16), t["ke"].astype(BF16)), _dot_nt(do, v),
                _dot(do, sp.astype(BF16)), _dot_tn(do, t["qin"].astype(BF16)))
               for t, v, do, sp in zip(terms, vals, dos, prevs)]
        inner = []
        for t, do, (a, da, _, _) in zip(terms, dos, raw):
            da = jnp.where(allowed, da, 0.0).astype(BF16)
            inner.append((_dot(da, t["ke"].astype(BF16)), _dot_tn(da, t["qe"].astype(BF16)),
                          _dot_tn(jnp.where(allowed, a, 0.0).astype(BF16), do)))
        ds = [dstate[h] for h in range(GLA_HEADS)]
        outer = []
        for (c, h, _, _, _), t, v, sp, (_, _, _, inc) in zip(units, terms, vals, prevs, raw):
            ds_b = ds[h].astype(BF16)
            outer.append((_dot(v, ds_b), _dot_nt(t["kst"].astype(BF16), ds_b),
                          jnp.sum(sp * ds[h], axis=0, keepdims=True)))
            ds[h] = ds[h] * t["dec"] + inc
        for h in range(GLA_HEADS):
            dstate[h] = ds[h]
        seen_bf = jnp.where(seen_by, 1.0, 0.0).astype(BF16)
        rowi = lax.broadcasted_iota(jnp.int32, (GLA_CHUNK, GLA_DK), 0)
        for (c, h, rs, hs, vs), t, (_, _, dqin, _), (dqe, dke, dv_in), (dkst, dv_out, ddec) in zip(
                units, terms, raw, inner, outer):
            dq = (dqe * t["e_q"] + dqin * t["e_in"]) * (GLA_DK ** -0.5)
            dk = dke * t["e_k"] + dkst * t["e_st"]
            if merge is None:
                dq_ref[rs, hs], dk_ref[rs, hs], dv_ref[rs, vs] = dq, dk, dv_in + dv_out
            else:
                lo = OFF_GK - OFF_GQ + h * GLA_DK
                dp_ref[rs, hs] = (dq + dq_o[rs, hs]).astype(BF16)
                dp_ref[rs, lo:lo + GLA_DK] = (dk + dk_o[rs, hs]).astype(BF16)
                lo = OFF_GV - OFF_GQ + h * GLA_DV
                dp_ref[rs, lo:lo + GLA_DV] = (dv_in + dv_out + dv_o[rs, vs]).astype(BF16)
            kk = dkst * t["kst"]
            db = dqe * t["qe"] - dke * t["ke"] + dqin * t["qin"] - kk
            extra = jnp.sum(kk, axis=0, keepdims=True) + ddec * t["dec"]
            db = db + jnp.where(rowi == t["last"], extra, 0.0)
            dg_ref[rs, hs] = _dot_exact(seen_bf, db)
        if merge is not None:
            dp_ref[:, OFF_GR - OFF_GQ:gla_cols] = dgr_ref[...]

    scratch = [pltpu.VMEM((GLA_HEADS, GLA_DV, GLA_DK), F32)]
    if merge is None:
        return pl.pallas_call(
            body, name=name, grid=(nsteps,),
            in_specs=[qspec, kspec, vspec, gspec, ospec, sspec],
            out_specs=[gspec, gspec, ospec, gspec],
            out_shape=[jax.ShapeDtypeStruct((s, GLA_KW), F32), jax.ShapeDtypeStruct((s, GLA_KW), F32),
                       jax.ShapeDtypeStruct((s, GLA_VW), F32), jax.ShapeDtypeStruct((s, GLA_KW), F32)],
            scratch_shapes=scratch,
            compiler_params=_params(("arbitrary",)),
        )(proj, proj, proj, g, do, states)
    dproj = merge[4]
    block = gspec.index_map
    return pl.pallas_call(
        body, name=name, grid=(nsteps,),
        in_specs=[qspec, kspec, vspec, gspec, ospec, sspec, gspec, gspec, ospec, ospec, _ANY],
        out_specs=[pl.BlockSpec((rows, gla_cols), lambda n: (block(n)[0], OFF_GQ // gla_cols)), gspec],
        out_shape=[jax.ShapeDtypeStruct(dproj.shape, dproj.dtype), jax.ShapeDtypeStruct((s, GLA_KW), F32)],
        input_output_aliases={10: 0},
        scratch_shapes=scratch,
        compiler_params=_params(("arbitrary",)),
    )(proj, proj, proj, g, do, states, *merge)


def _gla_post(o_f, o_b, proj, g, cat, name="gla_post"):
    s = o_f.shape[0]

    def body(of_ref, ob_ref, gr_ref, g_ref, _, o_ref):
        gv = g_ref[...]
        for h in range(GLA_HEADS):
            sl = slice(h * GLA_DV, (h + 1) * GLA_DV)
            osum = of_ref[:, sl] + ob_ref[:, sl]
            r = lax.rsqrt(jnp.mean(osum * osum, axis=-1, keepdims=True) + EPS)
            gr = gr_ref[:, sl]
            o_ref[:, sl] = (osum * r * gv * (gr * _sigmoid(gr))).astype(BF16)

    blk = pl.BlockSpec((ROW_BLOCK, GLA_VW), lambda i: (i, 0))
    return pl.pallas_call(
        body, name=name, grid=(s // ROW_BLOCK,),
        in_specs=[blk, blk, pl.BlockSpec((ROW_BLOCK, GLA_VW), lambda i: (i, OFF_GR // GLA_VW)),
                  pl.BlockSpec((1, GLA_DV), lambda i: (0, 0)), pl.BlockSpec(memory_space=pl.ANY)],
        out_specs=pl.BlockSpec((ROW_BLOCK, GLA_VW), lambda i: (i, ATTN_W // GLA_VW)),
        out_shape=jax.ShapeDtypeStruct(cat.shape, cat.dtype),
        input_output_aliases={4: 0},
        compiler_params=_params(("parallel",)),
    )(o_f, o_b, proj, g, cat)


def _gla_post_bwd(dcat, o_f, o_b, proj, g, name="gla_post_bwd"):
    s = o_f.shape[0]

    def body(dy_ref, of_ref, ob_ref, gr_ref, g_ref, do_ref, dgr_ref, gg_ref):
        i = pl.program_id(0)
        gv = g_ref[...]
        gg = jnp.zeros((1, GLA_DV), F32)
        for h in range(GLA_HEADS):
            sl = slice(h * GLA_DV, (h + 1) * GLA_DV)
            osum = of_ref[:, sl] + ob_ref[:, sl]
            r = lax.rsqrt(jnp.mean(osum * osum, axis=-1, keepdims=True) + EPS)
            gr, dy = gr_ref[:, sl], dy_ref[:, sl]
            sg = _sigmoid(gr)
            dgr_ref[:, sl] = (dy * (osum * r * gv) * (sg * (1.0 + gr * (1.0 - sg)))).astype(BF16)
            dn = dy * (gr * sg)
            dng = dn * gv
            c = jnp.mean(dng * osum, axis=-1, keepdims=True)
            do_ref[:, sl] = (r * dng - osum * (r * r * r * c)).astype(BF16)
            gg = gg + jnp.sum(dn * osum * r, axis=0, keepdims=True)

        @pl.when(i == 0)
        def _():
            gg_ref[...] = jnp.zeros_like(gg_ref)

        gg_ref[...] += gg

    blk = pl.BlockSpec((ROW_BLOCK, GLA_VW), lambda i: (i, 0))
    vec = pl.BlockSpec((1, GLA_DV), lambda i: (0, 0))
    return pl.pallas_call(
        body, name=name, grid=(s // ROW_BLOCK,),
        in_specs=[pl.BlockSpec((ROW_BLOCK, GLA_VW), lambda i: (i, 1)), blk, blk,
                  pl.BlockSpec((ROW_BLOCK, GLA_VW), lambda i: (i, OFF_GR // GLA_VW)), vec],
        out_specs=[blk, blk, vec],
        out_shape=[jax.ShapeDtypeStruct((s, GLA_VW), BF16), jax.ShapeDtypeStruct((s, GLA_VW), BF16),
                   jax.ShapeDtypeStruct((1, GLA_DV), F32)],
        compiler_params=_params(("arbitrary",)),
    )(dcat, o_f, o_b, proj, g)


HALO = 16


def _extended(prev_ref, cur_ref, next_ref, i, s, tr, cs):
    first, last = i == 0, i == s // tr - 1
    prev = jnp.where(first, 0.0, prev_ref[:, cs].astype(F32))
    nxt = jnp.where(last, 0.0, next_ref[:, cs].astype(F32))
    return jnp.concatenate([prev, cur_ref[:, cs].astype(F32), nxt], axis=0)


FFN_ROWS = 512
FFN_COLS = 512


FFN_CHUNK = 256


def _lagged(i, ni, multiply, finish, rotate, init):
    chunks = [slice(c, c + FFN_CHUNK) for c in range(0, FFN_COLS, FFN_CHUNK)]

    @pl.when(i == 0)
    def _():
        init()

    @pl.when(i < 2)
    def _():
        rotate([multiply(cs) for cs in chunks], chunks)

    @pl.when((i >= 2) & (i < ni))
    def _():
        new = []
        for cs in chunks:
            new.append(multiply(cs))
            finish(cs)
        rotate(new, chunks)

    @pl.when(i >= ni)
    def _():
        for cs in chunks:
            finish(cs)
        rotate(None, chunks)


def _ffn_in(n2, w_gate, w_up, conv_w, conv_b, name="ffn_in"):
    s, d = n2.shape
    f = w_gate.shape[1]
    tm, tn, edge = FFN_ROWS, FFN_COLS, SUBLANES
    ni = s // tm
    ext = tm + 2 * edge

    def body(a_ref, wg_ref, wu_ref, w_ref, b_ref, gate_ref, up_ref, act_ref, g_tile, u_tile, g_tail):
        i = pl.program_id(1)

        @pl.when(i == 0)
        def _():
            g_tile[...] = jnp.zeros_like(g_tile)
            u_tile[...] = jnp.zeros_like(u_tile)
            g_tail[...] = jnp.zeros_like(g_tail)

        a = a_ref[...]
        g_new = _dot(a, wg_ref[...])
        u_new = _dot(a, wu_ref[...])
        g_old, u_old = g_tile[...], u_tile[...]
        before = jnp.where(i == 1, 0.0, g_tail[...])
        after = jnp.where(i == ni, 0.0, g_new[0:edge])
        ge = jnp.concatenate([before, g_old, after], axis=0)
        w = w_ref[...]
        conv = (w[0:1] * pltpu.roll(ge, 1, 0) + w[1:2] * ge + w[2:3] * pltpu.roll(ge, ext - 1, 0))[edge:edge + tm]
        conv = conv + b_ref[...]
        gate_ref[...] = g_old
        up_ref[...] = u_old
        act_ref[...] = (conv * _sigmoid(conv) * u_old.astype(F32)).astype(BF16)
        g_tail[...] = g_old[tm - edge:tm]
        g_tile[...] = g_new
        u_tile[...] = u_new.astype(BF16)

    lag = pl.BlockSpec((tm, tn), lambda j, i: (jnp.maximum(i - 1, 0), j))
    return pl.pallas_call(
        body, name=name, grid=(f // tn, ni + 1),
        in_specs=[pl.BlockSpec((tm, d), lambda j, i: (jnp.minimum(i, ni - 1), 0)),
                  pl.BlockSpec((d, tn), lambda j, i: (0, j)), pl.BlockSpec((d, tn), lambda j, i: (0, j)),
                  pl.BlockSpec((3, tn), lambda j, i: (0, j)), pl.BlockSpec((1, tn), lambda j, i: (0, j))],
        out_specs=[lag, lag, lag],
        out_shape=[jax.ShapeDtypeStruct((s, f), F32), jax.ShapeDtypeStruct((s, f), BF16),
                   jax.ShapeDtypeStruct((s, f), BF16)],
        scratch_shapes=[pltpu.VMEM((tm, tn), F32), pltpu.VMEM((tm, tn), BF16), pltpu.VMEM((edge, tn), F32)],
        compiler_params=_params(("parallel", "arbitrary")),
    )(n2, w_gate, w_up, conv_w, conv_b)


def _ffn_mid_bwd(dh2, w_down, gate, up, conv_w, conv_b, name="ffn_mid_bwd"):
    s, d = dh2.shape
    f = gate.shape[1]
    tm, tn = FFN_ROWS, FFN_COLS
    ni = s // tm
    ext = tm + 2 * HALO
    per, last_halo = tm // HALO, s // HALO - 1

    def body(a_ref, wd_ref, gp, gc, gn, upp, upc, upn, w_ref, b_ref, dg_ref, du_ref, gw_ref, gb_ref,
             d_near, d_far, d_tail):
        i = pl.program_id(1)

        def multiply(cs):
            return _dot_nt(a_ref[...], wd_ref[cs, :])

        def finish(cs):
            before = jnp.where(i == 2, 0.0, d_tail[:, cs])
            after = jnp.where(i == ni + 1, 0.0, d_near[0:HALO, cs])
            de = jnp.concatenate([before, d_far[:, cs], after], axis=0)
            ge = _extended(gp, gc, gn, i - 2, s, tm, cs)
            ue = _extended(upp, upc, upn, i - 2, s, tm, cs)
            w = w_ref[:, cs]
            g_prev, g_next = pltpu.roll(ge, 1, 0), pltpu.roll(ge, ext - 1, 0)
            conv = w[0:1] * g_prev + w[1:2] * ge + w[2:3] * g_next + b_ref[:, cs]
            sg = _sigmoid(conv)
            inner = slice(HALO, HALO + tm)
            du_ref[:, cs] = (de * (conv * sg))[inner].astype(BF16)
            dconv = de * ue * (sg * (1.0 + conv * (1.0 - sg)))
            dgate = w[0:1] * pltpu.roll(dconv, ext - 1, 0) + w[1:2] * dconv + w[2:3] * pltpu.roll(dconv, 1, 0)
            dg_ref[:, cs] = dgate[inner].astype(BF16)
            dci = dconv[inner]
            gw_ref[0:1, cs] += jnp.sum(dci * g_prev[inner], axis=0, keepdims=True)
            gw_ref[1:2, cs] += jnp.sum(dci * ge[inner], axis=0, keepdims=True)
            gw_ref[2:3, cs] += jnp.sum(dci * g_next[inner], axis=0, keepdims=True)
            gb_ref[:, cs] += jnp.sum(dci, axis=0, keepdims=True)

        def rotate(new, chunks):
            d_tail[...] = d_far[tm - HALO:tm]
            d_far[...] = d_near[...]
            if new is not None:
                for cs, d_new in zip(chunks, new):
                    d_near[:, cs] = d_new

        def init():
            for r in (d_near, d_far, d_tail, gw_ref, gb_ref):
                r[...] = jnp.zeros_like(r)

        _lagged(i, ni, multiply, finish, rotate, init)

    def tile(i):
        return jnp.maximum(i - 2, 0)

    cur = pl.BlockSpec((tm, tn), lambda j, i: (tile(i), j))
    prev = pl.BlockSpec((HALO, tn), lambda j, i: (jnp.maximum(tile(i) * per - 1, 0), j))
    nxt = pl.BlockSpec((HALO, tn), lambda j, i: (jnp.minimum((tile(i) + 1) * per, last_halo), j))
    wspec = pl.BlockSpec((3, tn), lambda j, i: (0, j))
    bspec = pl.BlockSpec((1, tn), lambda j, i: (0, j))
    return pl.pallas_call(
        body, name=name, grid=(f // tn, ni + 2),
        in_specs=[pl.BlockSpec((tm, d), lambda j, i: (jnp.minimum(i, ni - 1), 0)),
                  pl.BlockSpec((tn, d), lambda j, i: (j, 0))] + [prev, cur, nxt] * 2 + [wspec, bspec],
        out_specs=[cur, cur, wspec, bspec],
        out_shape=[jax.ShapeDtypeStruct((s, f), BF16), jax.ShapeDtypeStruct((s, f), BF16),
                   jax.ShapeDtypeStruct((3, f), F32), jax.ShapeDtypeStruct((1, f), F32)],
        scratch_shapes=[pltpu.VMEM((tm, tn), F32), pltpu.VMEM((tm, tn), F32), pltpu.VMEM((HALO, tn), F32)],
        compiler_params=_params(("parallel", "arbitrary")),
    )(dh2, w_down, gate, gate, gate, up, up, up, conv_w, conv_b)


def _local_step(x, target, w, late_weights=None, grad_sink=None, first_dep=()):
    s = x.shape[0]
    tables = _rope_tables(s)
    uf, ub = _gate_matrices(w["gf_up"], w["gb_up"])
    if grad_sink is None:
        grad_sink = lambda names, grads: ()

    n1 = _rms_fwd(x, w["norm1_g"], "norm1")
    proj = _matmul([(n1, w["w_in"])], "nn", F32, 1024, 1280, D_MODEL, "in_proj", deps=first_dep)
    qkv = _rope_fwd(proj, tables)
    branches = [_attn_fwd(*qkv[di], d, f"attn_fwd_d{d}") for di, d in enumerate(DILATIONS)]
    o_mix, ao, lse = _attn_combine([b[0] for b in branches], [b[1] for b in branches], w["attn_norm_g"])
    g_f, g_b = _gla_gates(proj, uf, ub, w["gf_b"], w["gb_b"])
    o_f, st_f = _gla_fwd(proj, g_f, False, "gla_fwd_f")
    o_b, st_b = _gla_fwd(proj, g_b, True, "gla_fwd_b")
    cat = _gla_post(o_f, o_b, proj, w["gla_norm_g"], ao)
    if late_weights is not None:
        w = {**w, **late_weights("mixer", cat)}
    h1 = _matmul([(cat, w["w_out"])], "nn", F32, 512, 1024, D_MODEL, "out_proj", res=x)
    n2 = _rms_fwd(h1, w["norm2_g"], "norm2")
    if late_weights is not None:
        w = {**w, **late_weights("ffn", n2)}
    gate, up, act = _ffn_in(n2, w["w_gate"], w["w_up"], w["conv_w"], w["conv_b"])
    h2 = _matmul([(act, w["w_down"])], "nn", F32, 1024, 1024, 2816, "ffn_down", res=h1)
    dh2, dh2_b, loss_acc, g_final = _final_loss(h2, target, w["final_norm_g"])

    g_w_down = _matmul([(act, dh2_b)], "tn", BF16, 1408, 1024, 2048, "g_w_down")
    dep = grad_sink(["w_down"], [g_w_down])
    dgate, dup, g_conv_w, g_conv_b = _ffn_mid_bwd(dh2_b, w["w_down"], gate, up, w["conv_w"], w["conv_b"])
    g_w_gate = _matmul([(n2, dgate)], "tn", BF16, 2048, 512, 2048, "g_w_gate", deps=dep)
    g_w_up = _matmul([(n2, dup)], "tn", BF16, 2048, 512, 2048, "g_w_up")
    dep = grad_sink(["w_gate", "w_up"], [g_w_gate, g_w_up])
    dn2 = _matmul([(dgate, w["w_gate"])], "nt", F32, 1024, 1024, 2816, "d_n2_gate", deps=dep)
    dn2 = _matmul([(dup, w["w_up"])], "nt", F32, 1024, 1024, 2816, "d_n2_up", res=dn2)
    dh1, dh1_b, g_norm2 = _rms_bwd(dn2, h1, w["norm2_g"], dh2, "norm2_bwd")

    g_w_out = _matmul([(cat, dh1_b)], "tn", BF16, 1024, 1024, 2048, "g_w_out")
    dep = grad_sink(["w_out"], [g_w_out])
    dcat = _matmul([(dh1_b, w["w_out"])], "nt", F32, 512, 1024, D_MODEL, "d_cat", deps=dep)
    do_attn, delta, g_attn_norm = _attn_prebwd(dcat, o_mix, w["attn_norm_g"])
    grads = [_attn_bwd(*qkv[di], do_attn[di], lse[di], delta[di], d, f"attn_bwd_d{d}")
             for di, d in enumerate(DILATIONS)]
    dproj = _rope_bwd(grads, tables)
    do_gla, dgr, g_gla_norm = _gla_post_bwd(dcat, o_f, o_b, proj, w["gla_norm_g"])
    dq_f, dk_f, dv_f, dg_f = _gla_bwd(proj, g_f, do_gla, st_f, False, "gla_bwd_f")
    dproj, dg_b = _gla_bwd(proj, g_b, do_gla, st_b, True, "gla_bwd_b", merge=(dq_f, dk_f, dv_f, dgr, dproj))
    dproj, g_uf, g_ub, g_gf_b, g_gb_b = _gla_gates_bwd(dg_f, dg_b, proj, uf, ub, w["gf_b"], w["gb_b"], dproj)
    g_w_in = _matmul([(n1, dproj)], "tn", BF16, 1024, 1280, 2048, "g_w_in")
    dep = grad_sink(["w_in"], [g_w_in])
    dn1 = _matmul([(dproj, w["w_in"])], "nt", F32, 1024, 2048, 1280, "d_n1", deps=dep)
    grad_x, g_norm1 = _rms_bwd(dn1, x, w["norm1_g"], dh1, "norm1_bwd", bf16_copy=False)

    g = dict(norm1_g=g_norm1, w_in=g_w_in, gf_up=g_uf[:GLA_RANK], gf_b=g_gf_b,
             gb_up=g_ub[GLA_RANK:2 * GLA_RANK], gb_b=g_gb_b, gla_norm_g=g_gla_norm, attn_norm_g=g_attn_norm,
             w_out=g_w_out, norm2_g=g_norm2, w_gate=g_w_gate, w_up=g_w_up, conv_w=g_conv_w, conv_b=g_conv_b,
             w_down=g_w_down, final_norm_g=g_final)
    return loss_acc, grad_x, g


def _me_and_peers():
    x, y, c = lax.axis_index("x"), lax.axis_index("y"), lax.axis_index("c")
    me = 4 * x + 2 * y + c
    peers = []
    for kbits in range(1, N_DEV):
        px, py, pc = x ^ (kbits >> 2 & 1), y ^ (kbits >> 1 & 1), c ^ (kbits & 1)
        peers.append(((px, py, pc), 4 * px + 2 * py + pc))
    return me, peers


_HBM = pl.BlockSpec(memory_space=pltpu.HBM)
_SEM = pl.BlockSpec(memory_space=pltpu.SEMAPHORE)
_ANY = pl.BlockSpec(memory_space=pl.ANY)
_EFFECT = pltpu.SideEffectType.DATAFLOW_SIDE_EFFECTING


def _exchange_copies(src_refs, land_refs, send_sems, recv_sems, scatter):
    me, peers = _me_and_peers()
    out = []
    for a, (src, land) in enumerate(zip(src_refs, land_refs)):
        for kk, (dev, idx) in enumerate(peers):
            out.append(pltpu.make_async_remote_copy(
                src_ref=src.at[idx] if scatter else src, dst_ref=land.at[me],
                send_sem=send_sems.at[a * (N_DEV - 1) + kk], recv_sem=recv_sems.at[a * (N_DEV - 1) + kk],
                device_id=dev, device_id_type=MESH_ID))
    return out


def _exchange_start(srcs, lands, scatter, name, deps=()):
    n, nd = len(srcs), len(deps)

    def body(*refs):
        src_refs, land_refs = refs[:n], refs[n:2 * n]
        send_sems, recv_sems = refs[2 * n + nd:2 * n + nd + 2]
        token = refs[-1]
        for cp in _exchange_copies(src_refs, land_refs, send_sems, recv_sems, scatter):
            cp.start()
        token[...] = jnp.zeros_like(token)

    outs = pl.pallas_call(
        body, name=name,
        in_specs=[_HBM] * (2 * n) + [_ANY] * nd,
        out_specs=[_SEM, _SEM] + [_HBM] * (2 * n) + [pl.BlockSpec(memory_space=pltpu.VMEM)],
        out_shape=[pltpu.SemaphoreType.DMA((n * (N_DEV - 1),)), pltpu.SemaphoreType.DMA((n * (N_DEV - 1),))]
        + [pltpu.HBM(t.shape, t.dtype) for t in srcs] + [pltpu.HBM(t.shape, t.dtype) for t in lands]
        + [jax.ShapeDtypeStruct((SUBLANES, LANES), F32)],
        input_output_aliases={i: 2 + i for i in range(2 * n)},
        compiler_params=pltpu.CompilerParams(has_side_effects=_EFFECT),
    )(*[pltpu.with_memory_space_constraint(t, pltpu.HBM) for t in list(srcs) + list(lands)], *deps)
    send_sems, recv_sems = outs[0], outs[1]
    return dict(send=send_sems, recv=recv_sems, srcs=outs[2:2 + n], lands=outs[2 + n:2 + 2 * n],
                scatter=scatter, token=outs[-1])


def _exchange_wait(started, name, after):
    n = len(started["srcs"])
    scatter = started["scatter"]

    def body(*refs):
        src_refs, land_refs = refs[:n], refs[n:2 * n]
        send_sems, recv_sems = refs[2 * n], refs[2 * n + 1]
        for cp in _exchange_copies(src_refs, land_refs, send_sems, recv_sems, scatter):
            cp.wait_send()
            cp.wait_recv()

    outs = pl.pallas_call(
        body, name=name,
        in_specs=[_HBM] * (2 * n) + [_SEM, _SEM, _ANY],
        out_specs=[_HBM] * (2 * n),
        out_shape=[pltpu.HBM(t.shape, t.dtype) for t in started["srcs"]]
        + [pltpu.HBM(t.shape, t.dtype) for t in started["lands"]],
        input_output_aliases={i: i for i in range(2 * n)},
        compiler_params=pltpu.CompilerParams(has_side_effects=_EFFECT),
    )(*started["srcs"], *started["lands"], started["send"], started["recv"], after)
    return outs[:n], outs[n:]


def _all_gather_two_level(shard, name):
    def body(x_ref, out_ref, send_sems, recv_sems, local_sem):
        x, y, c = lax.axis_index("x"), lax.axis_index("y"), lax.axis_index("c")
        me, sibling = (x, y, c), (x, y, 1 - c)
        chips = [(1 - x, y), (x, 1 - y), (1 - x, 1 - y)]

        def slot(px, py, pc):
            return out_ref.at[4 * px + 2 * py + pc]

        def copy(k, block, to, src=None):
            return pltpu.make_async_remote_copy(
                src_ref=slot(*block) if src is None else src, dst_ref=slot(*block),
                send_sem=send_sems.at[k], recv_sem=recv_sems.at[k], device_id=to, device_id_type=MESH_ID)

        mine = pltpu.make_async_copy(x_ref, slot(*me), local_sem)
        mine.start()
        first = [copy(0, me, sibling, src=x_ref)]
        first += [copy(1 + j, me, (*chip, c), src=x_ref) for j, chip in enumerate(chips)]
        for cp in first:
            cp.start()
        passed = [copy(4 + j, (*chip, c), sibling) for j, chip in enumerate(chips)]
        for j, chip in enumerate(chips):
            copy(1 + j, (*chip, c), me).wait_recv()
            passed[j].start()
        copy(0, sibling, me).wait_recv()
        for j, chip in enumerate(chips):
            copy(4 + j, (*chip, 1 - c), me).wait_recv()
        for cp in first + passed:
            cp.wait_send()
        mine.wait()

    return pl.pallas_call(
        body, name=name,
        in_specs=[_ANY], out_specs=_ANY,
        out_shape=jax.ShapeDtypeStruct((N_DEV,) + shard.shape, shard.dtype),
        scratch_shapes=[pltpu.SemaphoreType.DMA((N_DEV - 1,)), pltpu.SemaphoreType.DMA((N_DEV - 1,)),
                        pltpu.SemaphoreType.DMA],
    )(shard)


def _all_gather_vmem(vec, name):
    r = vec.shape[0]

    def body(v_ref, o_ref, send_sems, recv_sems):
        me, peers = _me_and_peers()
        o_ref[me] = v_ref[...]
        sends = []
        for kk, (dev, _) in enumerate(peers):
            cp = pltpu.make_async_remote_copy(
                src_ref=v_ref, dst_ref=o_ref.at[me],
                send_sem=send_sems.at[kk], recv_sem=recv_sems.at[kk],
                device_id=dev, device_id_type=MESH_ID)
            cp.start()
            sends.append(cp)
        for kk, (dev, idx) in enumerate(peers):
            pltpu.make_async_remote_copy(
                src_ref=v_ref, dst_ref=o_ref.at[idx],
                send_sem=send_sems.at[kk], recv_sem=recv_sems.at[kk],
                device_id=dev, device_id_type=MESH_ID).wait_recv()
        for cp in sends:
            cp.wait_send()

    return pl.pallas_call(
        body, name=name,
        in_specs=[pl.BlockSpec(memory_space=pltpu.VMEM)],
        out_specs=pl.BlockSpec(memory_space=pltpu.VMEM),
        out_shape=jax.ShapeDtypeStruct((N_DEV, r, LANES), F32),
        scratch_shapes=[pltpu.SemaphoreType.DMA((N_DEV - 1,)), pltpu.SemaphoreType.DMA((N_DEV - 1,))],
        compiler_params=pltpu.CompilerParams(vmem_limit_bytes=VMEM_LIMIT),
    )(vec)


def _adamw_math(w, g, m, v):
    m = ADAM_B1 * m + (1.0 - ADAM_B1) * g
    v = ADAM_B2 * v + (1.0 - ADAM_B2) * (g * g)
    m_hat = m / (1.0 - ADAM_B1 ** ADAM_STEP)
    v_hat = v / (1.0 - ADAM_B2 ** ADAM_STEP)
    delta = -ADAM_LR * (m_hat / (jnp.sqrt(v_hat) + ADAM_EPS) + ADAM_WD * w)
    return delta, m, v


def _adamw_sum(parts, w, m, v, tr, name, own=None, me=None):
    r, c = w.shape

    def body(*refs):
        if own is None:
            p_ref, w_ref, m_ref, v_ref, g_ref, d_ref, nm_ref, nv_ref = refs
            terms = [p_ref[kk] for kk in range(N_DEV)]
        else:
            me_ref, p_ref, own_ref, w_ref, m_ref, v_ref, g_ref, d_ref, nm_ref, nv_ref = refs
            terms = [jnp.where(me_ref[0] == kk, own_ref[0], p_ref[kk]).astype(F32) for kk in range(N_DEV)]
        g = terms[0]
        for t in terms[1:]:
            g = g + t
        g_ref[...] = g
        d_ref[...], nm_ref[...], nv_ref[...] = _adamw_math(w_ref[...], g, m_ref[...], v_ref[...])

    out_shape = [jax.ShapeDtypeStruct((r, c), F32)] * 4
    if own is None:
        blk = pl.BlockSpec((tr, c), lambda i: (i, 0))
        return pl.pallas_call(
            body, name=name, grid=(r // tr,),
            in_specs=[pl.BlockSpec((N_DEV, tr, c), lambda i: (0, i, 0)), blk, blk, blk],
            out_specs=[blk] * 4, out_shape=out_shape,
            compiler_params=_params(("parallel",)),
        )(parts, w, m, v)
    blk = pl.BlockSpec((tr, c), lambda i, me_ref: (i, 0))
    return pl.pallas_call(
        body, name=name,
        grid_spec=pltpu.PrefetchScalarGridSpec(
            num_scalar_prefetch=1, grid=(r // tr,),
            in_specs=[pl.BlockSpec((N_DEV, tr, c), lambda i, me_ref: (0, i, 0)),
                      pl.BlockSpec((1, tr, c), lambda i, me_ref: (me_ref[0], i, 0)), blk, blk, blk],
            out_specs=[blk] * 4),
        out_shape=out_shape,
        compiler_params=_params(("parallel",)),
    )(jnp.reshape(me, (1,)).astype(jnp.int32), parts, own, w, m, v)


_SMALL = ("norm1_g", "gf_b", "gb_b", "gla_norm_g", "attn_norm_g", "norm2_g", "conv_b", "final_norm_g",
          "gf_up", "gb_up", "conv_w")


def _pack(named):
    flat = jnp.concatenate([jnp.ravel(t).astype(F32) for t in named])
    tile = SUBLANES * LANES
    total = -(-flat.shape[0] // tile) * tile
    return jnp.pad(flat, (0, total - flat.shape[0])).reshape(total // LANES, LANES)


def _unpack(packed, shapes):
    flat = packed.reshape(-1)
    out, off = [], 0
    for shp in shapes:
        size = int(np.prod(shp))
        out.append(flat[off:off + size].reshape(shp))
        off += size
    return out


def kernel(x, norm1_g, w_in, gf_up, gf_b, gb_up, gb_b, gla_norm_g, attn_norm_g, w_out, norm2_g, w_gate, w_up, conv_w, conv_b, w_down, final_norm_g, loss_target, m_norm1_g, m_w_in, m_gf_up, m_gf_b, m_gb_up, m_gb_b, m_gla_norm_g, m_attn_norm_g, m_w_out, m_norm2_g, m_w_gate, m_w_up, m_conv_w, m_conv_b, m_w_down, m_final_norm_g, v_norm1_g, v_w_in, v_gf_up, v_gf_b, v_gb_up, v_gb_b, v_gla_norm_g, v_attn_norm_g, v_w_out, v_norm2_g, v_w_gate, v_w_up, v_conv_w, v_conv_b, v_w_down, v_final_norm_g):
    names = ("norm1_g", "w_in", "gf_up", "gf_b", "gb_up", "gb_b", "gla_norm_g", "attn_norm_g", "w_out", "norm2_g",
             "w_gate", "w_up", "conv_w", "conv_b", "w_down", "final_norm_g")
    ws = dict(zip(names, (norm1_g, w_in, gf_up, gf_b, gb_up, gb_b, gla_norm_g, attn_norm_g, w_out, norm2_g,
                          w_gate, w_up, conv_w, conv_b, w_down, final_norm_g)))
    ms = dict(zip(names, (m_norm1_g, m_w_in, m_gf_up, m_gf_b, m_gb_up, m_gb_b, m_gla_norm_g, m_attn_norm_g, m_w_out,
                          m_norm2_g, m_w_gate, m_w_up, m_conv_w, m_conv_b, m_w_down, m_final_norm_g)))
    vs = dict(zip(names, (v_norm1_g, v_w_in, v_gf_up, v_gf_b, v_gb_up, v_gb_b, v_gla_norm_g, v_attn_norm_g, v_w_out,
                          v_norm2_g, v_w_gate, v_w_up, v_conv_w, v_conv_b, v_w_down, v_final_norm_g)))
    me = 4 * lax.axis_index("x") + 2 * lax.axis_index("y") + lax.axis_index("c")
    big = ("w_in", "w_out", "w_gate", "w_up", "w_down")
    col_sharded = ("w_in", "w_gate", "w_up")

    def gather_start(group, name, deps=()):
        shards = [ws[n][0].astype(BF16) for n in group]
        lands = [lax.empty((N_DEV,) + t.shape, BF16) for t in shards]
        return _exchange_start(shards, lands, False, name, deps)

    def gather_finish(group, started, name, after):
        full = {}
        for n, own, t in zip(group, *_exchange_wait(started, name, after)):
            t = lax.dynamic_update_slice(t, own[None], (me, 0, 0))
            if n in col_sharded:
                full[n] = jnp.transpose(t, (1, 0, 2)).reshape(t.shape[1], N_DEV * t.shape[2])
            else:
                full[n] = t.reshape(N_DEV * t.shape[1], t.shape[2])
        return full

    w_in_all = _all_gather_two_level(ws["w_in"][0].astype(BF16), "gather_w_in")
    full = {"w_in": jnp.pad(jnp.transpose(w_in_all, (1, 0, 2)).reshape(D_MODEL, IN_WIDTH),
                            ((0, 0), (0, IN_PAD - IN_WIDTH)))}
    late = {"mixer": ("w_out",), "ffn": ("w_gate", "w_up", "w_down")}
    started_late = {"mixer": gather_start(late["mixer"], "gather_w_out_start", deps=(full["w_in"],))}
    started_late["ffn"] = gather_start(late["ffn"], "gather_ffn_start", deps=(started_late["mixer"]["token"],))

    def late_weights(part, after):
        return gather_finish(late[part], started_late[part], "gather_" + part + "_wait", after)

    small_sharded = ("gf_up", "gb_up", "conv_w")
    sm = _all_gather_vmem(_pack([ws[n][0] for n in small_sharded]), "gather_small")
    shard_shapes = [ws[n][0].shape for n in small_sharded]
    per_dev = [_unpack(sm[d], shard_shapes) for d in range(N_DEV)]
    for i, n in enumerate(small_sharded):
        full[n] = jnp.concatenate([per_dev[d][i] for d in range(N_DEV)], axis=1)
    for n in ("norm1_g", "gf_b", "gb_b", "gla_norm_g", "attn_norm_g", "norm2_g", "conv_b"):
        full[n] = ws[n]
    full["final_norm_g"] = final_norm_g.reshape(1, D_MODEL)

    in_flight = []

    def grad_sink(group, grads):
        partials = []
        for n, t in zip(group, grads):
            if n == "w_in":
                t = t[:, :IN_WIDTH]
            t = t.astype(BF16)
            if n in col_sharded:
                t = jnp.transpose(t.reshape(t.shape[0], N_DEV, t.shape[1] // N_DEV), (1, 0, 2))
            else:
                t = t.reshape(N_DEV, t.shape[0] // N_DEV, t.shape[1])
            partials.append(t)
        lands = [lax.empty(t.shape, t.dtype) for t in partials]
        started = _exchange_start(partials, lands, True, "exchange_" + "_".join(group) + "_start")
        in_flight.append((group, started))
        return (started["token"],)

    loss_acc, grad_x, g = _local_step(x[0], loss_target[0], full, late_weights, grad_sink,
                                      first_dep=(started_late["ffn"]["token"],))

    out = {}
    for group, started in in_flight:
        sent, landed = _exchange_wait(started, "exchange_" + "_".join(group) + "_wait", grad_x)
        for n, parts, own in zip(group, landed, sent):
            out[n] = _adamw_sum(parts, ws[n][0], ms[n][0], vs[n][0], 64, "adamw_" + n, own=own, me=me)

    small_full_shapes = [g[n].shape for n in _SMALL]
    gsmall = _pack([g[n] for n in _SMALL] + [loss_acc[0:1, 0:1]])
    gathered_small = _all_gather_vmem(gsmall, "gather_small_grads")

    def full_small(d):
        parts = []
        for n in _SMALL:
            t = d[n].reshape(d[n].shape[-2:]) if d[n].ndim == 3 else d[n].reshape(1, -1)
            if n in small_sharded:
                wide = jnp.zeros((t.shape[0], t.shape[1] * N_DEV), F32)
                t = lax.dynamic_update_slice_in_dim(wide, t, me * t.shape[1], axis=1)
            parts.append(t)
        return _pack(parts + [jnp.zeros((1, 1), F32)])

    rows = gsmall.shape[0]
    res_small = _adamw_sum(gathered_small, full_small(ws), full_small(ms), full_small(vs), rows, "adamw_small")
    loss = res_small[0].reshape(-1)[sum(int(np.prod(sh)) for sh in small_full_shapes)]
    unpacked = [_unpack(t, small_full_shapes) for t in res_small]
    for i, n in enumerate(_SMALL):
        vals = [u[i] for u in unpacked]
        if n in small_sharded:
            width = vals[0].shape[1] // N_DEV
            vals = [lax.dynamic_slice_in_dim(t, me * width, width, axis=1) for t in vals]
        out[n] = vals

    result = [loss, grad_x[None]]
    for kind in range(4):
        for n in names:
            result.append(out[n][kind].reshape(ws[n].shape))
    return tuple(result)
```

```python
import functools

import numpy as np
import jax
import jax.numpy as jnp
from jax import lax
from jax.experimental import pallas as pl
from jax.experimental.pallas import tpu as pltpu

F32 = jnp.float32
BF16 = jnp.bfloat16

D_MODEL = 2048
ATTN_W = 1024
ATTN_HEADS = 8
HEAD_DIM = 128
ROPE_DIM = 32
ROPE_THETA = 500000.0
DILATIONS = (1, 4, 16)
N_SIDE = 64
GLA_KW = 512
GLA_VW = 1024
GLA_HEADS = 4
GLA_DK = 128
GLA_DV = 256
GLA_RANK = 16
GLA_GATE_NORM = 16.0
GLA_CHUNK = 64
IN_WIDTH = 6176
IN_PAD = 6400
D_FF = 5632
EPS = 1e-6
N_DEV = 8

OFF_AQ, OFF_AK, OFF_AV = 0, 1024, 2048
OFF_GQ, OFF_GK, OFF_GV, OFF_GR, OFF_Z = 3072, 3584, 4096, 5120, 6144

ADAM_LR, ADAM_B1, ADAM_B2, ADAM_EPS, ADAM_WD, ADAM_STEP = 0.001, 0.9, 0.999, 1e-08, 0.01, 10

LANES = 128
SUBLANES = 8
VMEM_LIMIT = 56 * 1024 * 1024
ROW_BLOCK = 256
ATTN_BLOCK = 128
GLA_CHUNKS_PER_STEP = 4
NEG = -1e30
MESH_ID = pl.DeviceIdType.MESH


def _params(sem):
    return pltpu.CompilerParams(dimension_semantics=sem, vmem_limit_bytes=VMEM_LIMIT)


def _dot(a, b):
    return lax.dot_general(a, b, (((1,), (0,)), ((), ())), preferred_element_type=F32)


def _dot_nt(a, b):
    return lax.dot_general(a, b, (((1,), (1,)), ((), ())), preferred_element_type=F32)


def _dot_tn(a, b):
    return lax.dot_general(a, b, (((0,), (0,)), ((), ())), preferred_element_type=F32)


def _sigmoid(x):
    return 0.5 * jnp.tanh(0.5 * x) + 0.5


def _matmul(pairs, mode, out_dtype, tm, tn, tk, name, res=None, deps=()):
    a0, b0 = pairs[0]
    if mode == "nn":
        (m, kdim), n = a0.shape, b0.shape[1]
    elif mode == "nt":
        (m, kdim), n = a0.shape, b0.shape[0]
    else:
        (kdim, m), n = a0.shape, b0.shape[1]
    assert m % tm == 0 and n % tn == 0 and kdim % tk == 0, (name, m, n, kdim)
    nk = kdim // tk
    npairs = len(pairs)
    steps = nk * npairs
    dot = {"nn": _dot, "nt": _dot_nt, "tn": _dot_tn}[mode]

    def kidx(p):
        return lambda k: jnp.clip(k - p * nk, 0, nk - 1)

    in_specs, args = [], []
    for p, (a, b) in enumerate(pairs):
        kk = kidx(p)
        if mode == "nn":
            in_specs += [pl.BlockSpec((tm, tk), lambda i, j, k, kk=kk: (i, kk(k))),
                         pl.BlockSpec((tk, tn), lambda i, j, k, kk=kk: (kk(k), j))]
        elif mode == "nt":
            in_specs += [pl.BlockSpec((tm, tk), lambda i, j, k, kk=kk: (i, kk(k))),
                         pl.BlockSpec((tn, tk), lambda i, j, k, kk=kk: (j, kk(k)))]
        else:
            in_specs += [pl.BlockSpec((tk, tm), lambda i, j, k, kk=kk: (kk(k), i)),
                         pl.BlockSpec((tk, tn), lambda i, j, k, kk=kk: (kk(k), j))]
        args += [a, b]
    if res is not None:
        in_specs.append(pl.BlockSpec((tm, tn), lambda i, j, k: (i, j)))
        args.append(res)
    in_specs += [pl.BlockSpec(memory_space=pl.ANY)] * len(deps)
    args += list(deps)

    def body(*refs):
        ab = refs[:2 * npairs]
        res_ref = refs[2 * npairs] if res is not None else None
        o_ref = refs[2 * npairs + (1 if res is not None else 0) + len(deps)]

        def finish(acc):
            if res_ref is not None:
                acc = acc + res_ref[...]
            o_ref[...] = acc.astype(out_dtype)

        if steps == 1:
            finish(dot(ab[0][...], ab[1][...]))
            return
        acc_ref = refs[-1]
        k = pl.program_id(2)

        @pl.when(k == 0)
        def _():
            acc_ref[...] = jnp.zeros_like(acc_ref)

        for p in range(npairs):
            @pl.when((k >= p * nk) & (k < (p + 1) * nk))
            def _(p=p):
                acc_ref[...] += dot(ab[2 * p][...], ab[2 * p + 1][...])

        @pl.when(k == steps - 1)
        def _():
            finish(acc_ref[...])

    return pl.pallas_call(
        body, name=name,
        grid=(m // tm, n // tn, steps),
        in_specs=in_specs,
        out_specs=pl.BlockSpec((tm, tn), lambda i, j, k: (i, j)),
        out_shape=jax.ShapeDtypeStruct((m, n), out_dtype),
        scratch_shapes=[] if steps == 1 else [pltpu.VMEM((tm, tn), F32)],
        compiler_params=_params(("parallel", "parallel", "arbitrary")),
    )(*args)


def _rms_fwd(x, g, name):
    s, d = x.shape

    def body(x_ref, g_ref, o_ref):
        xv = x_ref[...]
        r = lax.rsqrt(jnp.mean(xv * xv, axis=-1, keepdims=True) + EPS)
        o_ref[...] = (xv * r * g_ref[...]).astype(BF16)

    return pl.pallas_call(
        body, name=name, grid=(s // ROW_BLOCK,),
        in_specs=[pl.BlockSpec((ROW_BLOCK, d), lambda i: (i, 0)), pl.BlockSpec((1, d), lambda i: (0, 0))],
        out_specs=pl.BlockSpec((ROW_BLOCK, d), lambda i: (i, 0)),
        out_shape=jax.ShapeDtypeStruct((s, d), BF16),
        compiler_params=_params(("parallel",)),
    )(x, g)


def _rms_bwd(dn, x, g, dres, name, bf16_copy=True):
    s, d = x.shape

    def body(dn_ref, x_ref, g_ref, dres_ref, dx_ref, *rest):
        gg_ref = rest[-1]
        i = pl.program_id(0)
        xv, dnv = x_ref[...], dn_ref[...]
        r = lax.rsqrt(jnp.mean(xv * xv, axis=-1, keepdims=True) + EPS)
        dng = dnv * g_ref[...]
        c = jnp.mean(dng * xv, axis=-1, keepdims=True)
        dx = dres_ref[...] + r * dng - xv * (r * r * r * c)
        dx_ref[...] = dx
        if bf16_copy:
            rest[0][...] = dx.astype(BF16)

        @pl.when(i == 0)
        def _():
            gg_ref[...] = jnp.zeros_like(gg_ref)

        gg_ref[...] += jnp.sum(dnv * xv * r, axis=0, keepdims=True)

    row = pl.BlockSpec((ROW_BLOCK, d), lambda i: (i, 0))
    vec = pl.BlockSpec((1, d), lambda i: (0, 0))
    return pl.pallas_call(
        body, name=name, grid=(s // ROW_BLOCK,),
        in_specs=[row, row, vec, row],
        out_specs=[row] + [row] * bf16_copy + [vec],
        out_shape=[jax.ShapeDtypeStruct((s, d), F32)] + [jax.ShapeDtypeStruct((s, d), BF16)] * bf16_copy
        + [jax.ShapeDtypeStruct((1, d), F32)],
        compiler_params=_params(("arbitrary",)),
    )(dn, x, g, dres)


def _final_loss(h2, target, g, name="final_loss"):
    s, d = h2.shape

    def body(h_ref, t_ref, g_ref, dh_ref, dhb_ref, loss_ref, gg_ref):
        i = pl.program_id(0)
        hv, gv = h_ref[...], g_ref[...]
        r = lax.rsqrt(jnp.mean(hv * hv, axis=-1, keepdims=True) + EPS)
        e = hv * r * gv - t_ref[...]
        dy = e * (1.0 / d)
        dyg = dy * gv
        c = jnp.mean(dyg * hv, axis=-1, keepdims=True)
        dh = r * dyg - hv * (r * r * r * c)
        dh_ref[...] = dh
        dhb_ref[...] = dh.astype(BF16)

        @pl.when(i == 0)
        def _():
            gg_ref[...] = jnp.zeros_like(gg_ref)
            loss_ref[...] = jnp.zeros_like(loss_ref)

        gg_ref[...] += jnp.sum(dy * hv * r, axis=0, keepdims=True)
        loss_ref[...] += jnp.sum(jnp.sum(e * e, axis=-1, keepdims=True), axis=0, keepdims=True) * (0.5 / d)

    row = pl.BlockSpec((ROW_BLOCK, d), lambda i: (i, 0))
    vec = pl.BlockSpec((1, d), lambda i: (0, 0))
    return pl.pallas_call(
        body, name=name, grid=(s // ROW_BLOCK,),
        in_specs=[row, row, vec],
        out_specs=[row, row, pl.BlockSpec((SUBLANES, LANES), lambda i: (0, 0)), vec],
        out_shape=[jax.ShapeDtypeStruct((s, d), F32), jax.ShapeDtypeStruct((s, d), BF16),
                   jax.ShapeDtypeStruct((SUBLANES, LANES), F32), jax.ShapeDtypeStruct((1, d), F32)],
        compiler_params=_params(("arbitrary",)),
    )(h2, target, g)


def _rope_tables(s):
    pos = jnp.arange(s, dtype=F32)
    inv_freq = ROPE_THETA ** (-jnp.arange(0, ROPE_DIM, 2, dtype=F32) / ROPE_DIM)
    ang = pos[:, None] * inv_freq[None, :]
    cos, sin = jnp.cos(ang), jnp.sin(ang)
    half = ROPE_DIM // 2
    rest = HEAD_DIM - ROPE_DIM
    c = jnp.concatenate([cos, cos, jnp.ones((s, rest), F32)], axis=1)
    sm = jnp.concatenate([-sin, jnp.zeros((s, half + rest), F32)], axis=1)
    sp = jnp.concatenate([jnp.zeros((s, half), F32), sin, jnp.zeros((s, rest), F32)], axis=1)
    return c, sm, sp


def _res_shape(s, groups, dil, dtype):
    return jax.ShapeDtypeStruct((s // dil, dil * groups * LANES), dtype)


def _res_spec(groups, dil):
    return pl.BlockSpec((ROW_BLOCK // dil, dil * groups * LANES), lambda i: (i, 0))


def _to_residues(scr, o_ref, dil):
    groups, rows = scr.shape[0], ROW_BLOCK // dil
    for r in range(dil):
        for h in range(groups):
            piece = scr[h] if dil == 1 else scr.at[h][pl.ds(r, rows, stride=dil), :]
            o_ref[:, (r * groups + h) * LANES:(r * groups + h + 1) * LANES] = piece.astype(o_ref.dtype)


def _from_residues(i_ref, scr, dil):
    groups, rows = scr.shape[0], ROW_BLOCK // dil
    for r in range(dil):
        for h in range(groups):
            piece = i_ref[:, (r * groups + h) * LANES:(r * groups + h + 1) * LANES].astype(F32)
            if dil == 1:
                scr[h] = piece
            else:
                scr.at[h][pl.ds(r, rows, stride=dil), :] = piece


def _rope_fwd(proj, tables, name="rope_fwd"):
    s = proj.shape[0]
    half = ROPE_DIM // 2
    nd = len(DILATIONS)

    def body(p_ref, c_ref, sm_ref, sp_ref, *rest):
        outs, scr = rest[:3 * nd], rest[3 * nd]
        c, sm, sp = c_ref[...], sm_ref[...], sp_ref[...]
        for gi, off in enumerate((OFF_AQ, OFF_AK, OFF_AV)):
            for h in range(ATTN_HEADS):
                t = p_ref[:, off + h * HEAD_DIM: off + (h + 1) * HEAD_DIM]
                if off != OFF_AV:
                    t = t * c + pltpu.roll(t, HEAD_DIM - half, 1) * sm + pltpu.roll(t, half, 1) * sp
                scr[h] = t
            for di, dil in enumerate(DILATIONS):
                _to_residues(scr, outs[3 * di + gi], dil)

    tab = pl.BlockSpec((ROW_BLOCK, HEAD_DIM), lambda i: (i, 0))
    outs = pl.pallas_call(
        body, name=name, grid=(s // ROW_BLOCK,),
        in_specs=[pl.BlockSpec((ROW_BLOCK, 3 * ATTN_W), lambda i: (i, 0)), tab, tab, tab],
        out_specs=[_res_spec(ATTN_HEADS, d) for d in DILATIONS for _ in range(3)],
        out_shape=[_res_shape(s, ATTN_HEADS, d, BF16) for d in DILATIONS for _ in range(3)],
        scratch_shapes=[pltpu.VMEM((ATTN_HEADS, ROW_BLOCK, LANES), F32)],
        compiler_params=_params(("parallel",)),
    )(proj, *tables)
    return [tuple(outs[3 * di:3 * di + 3]) for di in range(nd)]


def _rope_bwd(grads, tables, name="rope_bwd"):
    s = grads[0][0].shape[0] * DILATIONS[0]
    half = ROPE_DIM // 2
    nd = len(DILATIONS)

    def body(*refs):
        ins = refs[:3 * nd]
        c_ref, sm_ref, sp_ref, o_ref = refs[3 * nd:3 * nd + 4]
        scrs = refs[3 * nd + 4:]
        c, sm, sp = c_ref[...], sm_ref[...], sp_ref[...]
        for gi, off in enumerate((OFF_AQ, OFF_AK, OFF_AV)):
            for di, dil in enumerate(DILATIONS):
                _from_residues(ins[3 * di + gi], scrs[di], dil)
            for h in range(ATTN_HEADS):
                t = scrs[0][h]
                for scr in scrs[1:]:
                    t = t + scr[h]
                if off != OFF_AV:
                    t = t * c + pltpu.roll(t * sm, half, 1) + pltpu.roll(t * sp, HEAD_DIM - half, 1)
                o_ref[:, off + h * HEAD_DIM: off + (h + 1) * HEAD_DIM] = t.astype(BF16)

    tab = pl.BlockSpec((ROW_BLOCK, HEAD_DIM), lambda i: (i, 0))
    return pl.pallas_call(
        body, name=name, grid=(s // ROW_BLOCK,),
        in_specs=[_res_spec(ATTN_HEADS, d) for d in DILATIONS for _ in range(3)] + [tab, tab, tab],
        out_specs=pl.BlockSpec((ROW_BLOCK, 3 * ATTN_W), lambda i: (i, 0)),
        out_shape=jax.ShapeDtypeStruct((s, IN_PAD), BF16),
        scratch_shapes=[pltpu.VMEM((ATTN_HEADS, ROW_BLOCK, LANES), F32) for _ in DILATIONS],
        compiler_params=_params(("parallel",)),
    )(*[t for g in grads for t in g], *tables)


ATTN_GROUP = 2


def _window_specs(nsteps, width):
    rows, hb = ATTN_GROUP * ATTN_BLOCK, N_SIDE
    per = rows // hb
    cur = pl.BlockSpec((rows, width), lambda r, j: (j, r))
    prev = pl.BlockSpec((hb, width), lambda r, j: (jnp.maximum(per * j - 1, 0), r))
    nxt = pl.BlockSpec((hb, width), lambda r, j: (jnp.minimum(per * (j + 1), per * nsteps - 1), r))
    return prev, cur, nxt


def _block(ref, b, sl):
    return ref[b * ATTN_BLOCK:(b + 1) * ATTN_BLOCK, sl]


def _edge(prev_ref, cur_ref, next_ref, b, sl):
    qb, hb = ATTN_BLOCK, N_SIDE
    before = prev_ref[:, sl] if b == 0 else cur_ref[b * qb - hb:b * qb, sl]
    after = next_ref[:, sl] if b == ATTN_GROUP - 1 else cur_ref[(b + 1) * qb:(b + 1) * qb + hb, sl]
    return jnp.concatenate([before, after], axis=0)


def _band_masks(j, length):
    qb, hb = ATTN_BLOCK, N_SIDE
    row = lax.broadcasted_iota(jnp.int32, (qb, qb), 0)
    col = lax.broadcasted_iota(jnp.int32, (qb, qb), 1)

    def edge_pos(i):
        return j * qb - hb + i + jnp.where(i >= hb, qb, 0)

    def ok(a, b, outside):
        return (jnp.abs(a - b) <= N_SIDE) & (outside >= 0) & (outside < length)

    cur = jnp.abs(row - col) <= N_SIDE
    edge_k = ok(j * qb + row, edge_pos(col), edge_pos(col))
    edge_q = ok(edge_pos(row), j * qb + col, edge_pos(row))
    return cur, edge_k, edge_q


def _attn_fwd(q, k, v, dil, name):
    length = q.shape[0]
    qb = ATTN_BLOCK
    nsteps = length // (ATTN_GROUP * qb)
    scale = HEAD_DIM ** -0.5

    def body(q_ref, kp_ref, kc_ref, kn_ref, vp_ref, vc_ref, vn_ref, o_ref, lse_ref):
        masks = [_band_masks(pl.program_id(1) * ATTN_GROUP + b, length) for b in range(ATTN_GROUP)]
        lane = lax.broadcasted_iota(jnp.int32, (qb, LANES), 1)
        units = [(b, h, slice(h * HEAD_DIM, (h + 1) * HEAD_DIM)) for b in range(ATTN_GROUP)
                 for h in range(ATTN_HEADS)]
        scores = [(_dot_nt(_block(q_ref, b, sl), _block(kc_ref, b, sl)),
                   _dot_nt(_block(q_ref, b, sl), _edge(kp_ref, kc_ref, kn_ref, b, sl))) for b, _, sl in units]
        probs = []
        lse_acc = [jnp.zeros((qb, LANES), F32) for _ in range(ATTN_GROUP)]
        for (b, h, _), (s_c, s_e) in zip(units, scores):
            valid_c, valid_e, _ = masks[b]
            s_c = jnp.where(valid_c, s_c * scale, NEG)
            s_e = jnp.where(valid_e, s_e * scale, NEG)
            m = jnp.max(jnp.maximum(s_c, s_e), axis=-1, keepdims=True)
            p_c, p_e = jnp.exp(s_c - m), jnp.exp(s_e - m)
            den = jnp.sum(p_c + p_e, axis=-1, keepdims=True)
            probs.append((p_c.astype(BF16), p_e.astype(BF16), 1.0 / den))
            lse_acc[b] = jnp.where(lane == h, m + jnp.log(den), lse_acc[b])
        for (b, _, sl), (p_c, p_e, inv) in zip(units, probs):
            o_ref[b * qb:(b + 1) * qb, sl] = (_dot(p_c, _block(vc_ref, b, sl))
                                              + _dot(p_e, _edge(vp_ref, vc_ref, vn_ref, b, sl))) * inv
        for b in range(ATTN_GROUP):
            lse_ref[b * qb:(b + 1) * qb, :] = lse_acc[b]

    prev, cur, nxt = _window_specs(nsteps, ATTN_W)
    return pl.pallas_call(
        body, name=name, grid=(dil, nsteps),
        in_specs=[cur, prev, cur, nxt, prev, cur, nxt],
        out_specs=[cur, pl.BlockSpec((ATTN_GROUP * qb, LANES), lambda r, j: (j, r))],
        out_shape=[jax.ShapeDtypeStruct((length, dil * ATTN_W), F32),
                   jax.ShapeDtypeStruct((length, dil * LANES), F32)],
        compiler_params=_params(("parallel", "parallel")),
    )(q, k, k, k, v, v, v)


def _attn_combine(outs, lses, g, name="attn_combine"):
    s = outs[0].shape[0] * DILATIONS[0]
    nd = len(DILATIONS)

    def body(*refs):
        o_refs, l_refs = refs[:nd], refs[nd:2 * nd]
        g_ref, o_ref, n_ref = refs[2 * nd:2 * nd + 3]
        lse_outs = refs[2 * nd + 3:3 * nd + 3]
        o_scr, l_scr = refs[3 * nd + 3:4 * nd + 3], refs[4 * nd + 3:5 * nd + 3]
        for di, dil in enumerate(DILATIONS):
            _from_residues(o_refs[di], o_scr[di], dil)
            _from_residues(l_refs[di], l_scr[di], dil)
        ls = [scr[0] for scr in l_scr]
        m = ls[0]
        for l in ls[1:]:
            m = jnp.maximum(m, l)
        es = [jnp.exp(l - m) for l in ls]
        z = es[0]
        for e in es[1:]:
            z = z + e
        ws = [e / z for e in es]
        l_scr[0][0] = m + jnp.log(z)
        for di, dil in enumerate(DILATIONS):
            _to_residues(l_scr[0], lse_outs[di], dil)
        ssq = jnp.zeros((ROW_BLOCK, 1), F32)
        for h in range(ATTN_HEADS):
            sl = slice(h * HEAD_DIM, (h + 1) * HEAD_DIM)
            acc = ws[0][:, h:h + 1] * o_scr[0][h]
            for w, scr in zip(ws[1:], o_scr[1:]):
                acc = acc + w[:, h:h + 1] * scr[h]
            o_ref[:, sl] = acc
            ssq = ssq + jnp.sum(acc * acc, axis=-1, keepdims=True)
        r = lax.rsqrt(ssq * (1.0 / ATTN_W) + EPS)
        n_ref[...] = (o_ref[...] * r * g_ref[...]).astype(BF16)

    blk = pl.BlockSpec((ROW_BLOCK, ATTN_W), lambda i: (i, 0))
    outs_ = pl.pallas_call(
        body, name=name, grid=(s // ROW_BLOCK,),
        in_specs=[_res_spec(ATTN_HEADS, d) for d in DILATIONS] + [_res_spec(1, d) for d in DILATIONS]
        + [pl.BlockSpec((1, ATTN_W), lambda i: (0, 0))],
        out_specs=[blk, blk] + [_res_spec(1, d) for d in DILATIONS],
        out_shape=[jax.ShapeDtypeStruct((s, ATTN_W), F32), jax.ShapeDtypeStruct((s, D_MODEL), BF16)]
        + [_res_shape(s, 1, d, F32) for d in DILATIONS],
        scratch_shapes=[pltpu.VMEM((ATTN_HEADS, ROW_BLOCK, LANES), F32) for _ in DILATIONS]
        + [pltpu.VMEM((1, ROW_BLOCK, LANES), F32) for _ in DILATIONS],
        compiler_params=_params(("parallel",)),
    )(*outs, *lses, g)
    return outs_[0], outs_[1], list(outs_[2:])


def _attn_prebwd(dcat, o, g, name="attn_prebwd"):
    s = o.shape[0]
    nd = len(DILATIONS)

    def body(dy_ref, o_ref, g_ref, *rest):
        do_outs, delta_outs, gg_ref = rest[:nd], rest[nd:2 * nd], rest[2 * nd]
        do_scr, delta_scr = rest[2 * nd + 1], rest[2 * nd + 2]
        i = pl.program_id(0)
        dy, ov = dy_ref[...], o_ref[...]
        r = lax.rsqrt(jnp.mean(ov * ov, axis=-1, keepdims=True) + EPS)
        dyg = dy * g_ref[...]
        c = jnp.mean(dyg * ov, axis=-1, keepdims=True)
        do = r * dyg - ov * (r * r * r * c)
        prod = do * ov
        lane = lax.broadcasted_iota(jnp.int32, (ROW_BLOCK, LANES), 1)
        acc = jnp.zeros((ROW_BLOCK, LANES), F32)
        for h in range(ATTN_HEADS):
            sl = slice(h * HEAD_DIM, (h + 1) * HEAD_DIM)
            do_scr[h] = do[:, sl]
            acc = jnp.where(lane == h, jnp.sum(prod[:, sl], axis=-1, keepdims=True), acc)
        delta_scr[0] = acc
        for di, dil in enumerate(DILATIONS):
            _to_residues(do_scr, do_outs[di], dil)
            _to_residues(delta_scr, delta_outs[di], dil)

        @pl.when(i == 0)
        def _():
            gg_ref[...] = jnp.zeros_like(gg_ref)

        gg_ref[...] += jnp.sum(dy * ov * r, axis=0, keepdims=True)

    blk = pl.BlockSpec((ROW_BLOCK, ATTN_W), lambda i: (i, 0))
    vec = pl.BlockSpec((1, ATTN_W), lambda i: (0, 0))
    outs = pl.pallas_call(
        body, name=name, grid=(s // ROW_BLOCK,),
        in_specs=[blk, blk, vec],
        out_specs=[_res_spec(ATTN_HEADS, d) for d in DILATIONS] + [_res_spec(1, d) for d in DILATIONS] + [vec],
        out_shape=[_res_shape(s, ATTN_HEADS, d, BF16) for d in DILATIONS]
        + [_res_shape(s, 1, d, F32) for d in DILATIONS] + [jax.ShapeDtypeStruct((1, ATTN_W), F32)],
        scratch_shapes=[pltpu.VMEM((ATTN_HEADS, ROW_BLOCK, LANES), F32), pltpu.VMEM((1, ROW_BLOCK, LANES), F32)],
        compiler_params=_params(("arbitrary",)),
    )(dcat, o, g)
    return list(outs[:nd]), list(outs[nd:2 * nd]), outs[2 * nd]


def _attn_bwd(q, k, v, do, lse, delta, dil, name):
    length = q.shape[0]
    qb = ATTN_BLOCK
    nsteps = length // (ATTN_GROUP * qb)
    scale = HEAD_DIM ** -0.5

    def body(qp, qc, qn, kp, kc, kn, vp, vc, vn, dop, doc, don, lp, lc, ln, dp, dc, dn, dq_ref, dk_ref, dv_ref):
        masks = [_band_masks(pl.program_id(1) * ATTN_GROUP + b, length) for b in range(ATTN_GROUP)]
        everything = slice(None)
        lse_e = [_edge(lp, lc, ln, b, everything) for b in range(ATTN_GROUP)]
        del_e = [_edge(dp, dc, dn, b, everything) for b in range(ATTN_GROUP)]
        units = [(b, h, slice(h * HEAD_DIM, (h + 1) * HEAD_DIM)) for b in range(ATTN_GROUP)
                 for h in range(ATTN_HEADS)]
        prods = []
        for b, _, sl in units:
            q_c, k_c, v_c, do_c = _block(qc, b, sl), _block(kc, b, sl), _block(vc, b, sl), _block(doc, b, sl)
            q_e, k_e = _edge(qp, qc, qn, b, sl), _edge(kp, kc, kn, b, sl)
            v_e, do_e = _edge(vp, vc, vn, b, sl), _edge(dop, doc, don, b, sl)
            prods.append((_dot_nt(q_c, k_c), _dot_nt(do_c, v_c), _dot_nt(q_c, k_e), _dot_nt(do_c, v_e),
                          _dot_nt(q_e, k_c), _dot_nt(do_e, v_c)))
        parts = []
        for (b, h, _), (s_cc, dp_cc, s_ek, dp_ek, s_eq, dp_eq) in zip(units, prods):
            valid_c, valid_ek, valid_eq = masks[b]
            hc = slice(h, h + 1)
            lse_c, del_c = _block(lc, b, hc), _block(dc, b, hc)
            p_cc = jnp.where(valid_c, jnp.exp(s_cc * scale - lse_c), 0.0)
            ds_cc = (p_cc * (dp_cc - del_c)).astype(BF16)
            p_ek = jnp.where(valid_ek, jnp.exp(s_ek * scale - lse_c), 0.0)
            ds_ek = (p_ek * (dp_ek - del_c)).astype(BF16)
            p_eq = jnp.where(valid_eq, jnp.exp(s_eq * scale - lse_e[b][:, hc]), 0.0)
            ds_eq = (p_eq * (dp_eq - del_e[b][:, hc])).astype(BF16)
            parts.append((p_cc.astype(BF16), ds_cc, ds_ek, p_eq.astype(BF16), ds_eq))
        for (b, _, sl), (p_cc, ds_cc, ds_ek, p_eq, ds_eq) in zip(units, parts):
            rows = slice(b * qb, (b + 1) * qb)
            q_c, k_c, do_c = _block(qc, b, sl), _block(kc, b, sl), _block(doc, b, sl)
            q_e, k_e, do_e = _edge(qp, qc, qn, b, sl), _edge(kp, kc, kn, b, sl), _edge(dop, doc, don, b, sl)
            dq_ref[rows, sl] = ((_dot(ds_cc, k_c) + _dot(ds_ek, k_e)) * scale).astype(BF16)
            dk_ref[rows, sl] = ((_dot_tn(ds_cc, q_c) + _dot_tn(ds_eq, q_e)) * scale).astype(BF16)
            dv_ref[rows, sl] = (_dot_tn(p_cc, do_c) + _dot_tn(p_eq, do_e)).astype(BF16)

    wide, narrow = list(_window_specs(nsteps, ATTN_W)), list(_window_specs(nsteps, LANES))
    return tuple(pl.pallas_call(
        body, name=name, grid=(dil, nsteps),
        in_specs=wide * 4 + narrow * 2,
        out_specs=[wide[1]] * 3,
        out_shape=[jax.ShapeDtypeStruct((length, dil * ATTN_W), BF16)] * 3,
        compiler_params=_params(("parallel", "parallel")),
    )(q, q, q, k, k, k, v, v, v, do, do, do, lse, lse, lse, delta, delta, delta))


def _gate_matrices(gf_up, gb_up):
    pad = LANES - 2 * GLA_RANK
    uf = jnp.concatenate([gf_up, jnp.zeros((GLA_RANK + pad, GLA_KW), gf_up.dtype)], axis=0)
    ub = jnp.concatenate([jnp.zeros((GLA_RANK, GLA_KW), gb_up.dtype), gb_up, jnp.zeros((pad, GLA_KW), gb_up.dtype)], axis=0)
    return uf.astype(BF16), ub.astype(BF16)


def _log_sigmoid(x):
    return jnp.minimum(x, 0.0) - jnp.log(1.0 + jnp.exp(-jnp.abs(x)))


def _gla_gates(proj, uf, ub, gf_b, gb_b, name="gla_gates"):
    s = proj.shape[0]

    def body(z_ref, uf_ref, ub_ref, bf_ref, bb_ref, gf_ref, gb_ref):
        z = z_ref[...].astype(BF16)
        gf_ref[...] = _log_sigmoid(_dot(z, uf_ref[...]) + bf_ref[...]) * (1.0 / GLA_GATE_NORM)
        gb_ref[...] = _log_sigmoid(_dot(z, ub_ref[...]) + bb_ref[...]) * (1.0 / GLA_GATE_NORM)

    mat = pl.BlockSpec((LANES, GLA_KW), lambda i: (0, 0))
    vec = pl.BlockSpec((1, GLA_KW), lambda i: (0, 0))
    out = pl.BlockSpec((ROW_BLOCK, GLA_KW), lambda i: (i, 0))
    return pl.pallas_call(
        body, name=name, grid=(s // ROW_BLOCK,),
        in_specs=[pl.BlockSpec((ROW_BLOCK, LANES), lambda i: (i, OFF_Z // LANES)), mat, mat, vec, vec],
        out_specs=[out, out],
        out_shape=[jax.ShapeDtypeStruct((s, GLA_KW), F32)] * 2,
        compiler_params=_params(("parallel",)),
    )(proj, uf, ub, gf_b, gb_b)


def _gla_gates_bwd(dgf, dgb, proj, uf, ub, gf_b, gb_b, dproj, name="gla_gates_bwd"):
    s = proj.shape[0]
    tail = IN_PAD - OFF_Z

    def body(dgf_ref, dgb_ref, z_ref, uf_ref, ub_ref, bf_ref, bb_ref, _, dz_ref, guf_ref, gub_ref, gbf_ref, gbb_ref):
        i = pl.program_id(0)
        z = z_ref[...].astype(BF16)
        uf_, ub_ = uf_ref[...], ub_ref[...]
        dpf = dgf_ref[...] * (1.0 / GLA_GATE_NORM) * _sigmoid(-(_dot(z, uf_) + bf_ref[...]))
        dpb = dgb_ref[...] * (1.0 / GLA_GATE_NORM) * _sigmoid(-(_dot(z, ub_) + bb_ref[...]))
        dpf_b, dpb_b = dpf.astype(BF16), dpb.astype(BF16)
        dz_ref[:, 0:LANES] = (_dot_nt(dpf_b, uf_) + _dot_nt(dpb_b, ub_)).astype(BF16)
        dz_ref[:, LANES:tail] = jnp.zeros((ROW_BLOCK, tail - LANES), BF16)

        @pl.when(i == 0)
        def _():
            for r in (guf_ref, gub_ref, gbf_ref, gbb_ref):
                r[...] = jnp.zeros_like(r)

        guf_ref[...] += _dot_tn(z, dpf_b)
        gub_ref[...] += _dot_tn(z, dpb_b)
        gbf_ref[...] += jnp.sum(dpf, axis=0, keepdims=True)
        gbb_ref[...] += jnp.sum(dpb, axis=0, keepdims=True)

    mat = pl.BlockSpec((LANES, GLA_KW), lambda i: (0, 0))
    vec = pl.BlockSpec((1, GLA_KW), lambda i: (0, 0))
    blk = pl.BlockSpec((ROW_BLOCK, GLA_KW), lambda i: (i, 0))
    return pl.pallas_call(
        body, name=name, grid=(s // ROW_BLOCK,),
        in_specs=[blk, blk, pl.BlockSpec((ROW_BLOCK, LANES), lambda i: (i, OFF_Z // LANES)), mat, mat, vec, vec,
                  pl.BlockSpec(memory_space=pl.ANY)],
        out_specs=[pl.BlockSpec((ROW_BLOCK, tail), lambda i: (i, OFF_Z // tail)), mat, mat, vec, vec],
        out_shape=[jax.ShapeDtypeStruct(dproj.shape, dproj.dtype), jax.ShapeDtypeStruct((LANES, GLA_KW), F32),
                   jax.ShapeDtypeStruct((LANES, GLA_KW), F32), jax.ShapeDtypeStruct((1, GLA_KW), F32),
                   jax.ShapeDtypeStruct((1, GLA_KW), F32)],
        input_output_aliases={7: 0},
        compiler_params=_params(("arbitrary",)),
    )(dgf, dgb, proj, uf, ub, gf_b, gb_b, dproj)


def _split3(x):
    x1 = x.astype(BF16)
    r1 = x - x1.astype(F32)
    x2 = r1.astype(BF16)
    x3 = (r1 - x2.astype(F32)).astype(BF16)
    return x1, x2, x3


def _dot_exact(mask_bf, x):
    x1, x2, x3 = _split3(x)
    return _dot(mask_bf, x1) + _dot(mask_bf, x2) + _dot(mask_bf, x3)


def _chunk_masks(reverse):
    c = GLA_CHUNK
    row = lax.broadcasted_iota(jnp.int32, (c, c), 0)
    col = lax.broadcasted_iota(jnp.int32, (c, c), 1)
    allowed = (col >= row) if reverse else (col <= row)
    seen_by = (col <= row) if reverse else (col >= row)
    return allowed, seen_by


def _chunk_terms(q_ref, k_ref, g_ref, rs, hs, allowed, reverse):
    c = GLA_CHUNK
    mid, last = (c // 2, 0) if reverse else (c // 2 - 1, c - 1)
    q = q_ref[rs, hs] * (GLA_DK ** -0.5)
    k = k_ref[rs, hs]
    b = _dot_exact(jnp.where(allowed, 1.0, 0.0).astype(BF16), g_ref[rs, hs])
    bref, blast = b[mid:mid + 1, :], b[last:last + 1, :]
    e_q, e_k, e_in, e_st = jnp.exp(b - bref), jnp.exp(bref - b), jnp.exp(b), jnp.exp(blast - b)
    return dict(last=last, e_q=e_q, e_k=e_k, e_in=e_in, e_st=e_st,
                dec=jnp.exp(blast), qe=q * e_q, ke=k * e_k, qin=q * e_in, kst=k * e_st)


def _gla_blockspecs(s, reverse_order):
    cb = GLA_CHUNKS_PER_STEP
    rows = cb * GLA_CHUNK
    nsteps = s // rows

    def rb(n):
        return (nsteps - 1 - n) if reverse_order else n

    qspec = pl.BlockSpec((rows, GLA_KW), lambda n: (rb(n), OFF_GQ // GLA_KW))
    kspec = pl.BlockSpec((rows, GLA_KW), lambda n: (rb(n), OFF_GK // GLA_KW))
    vspec = pl.BlockSpec((rows, GLA_VW), lambda n: (rb(n), OFF_GV // GLA_VW))
    gspec = pl.BlockSpec((rows, GLA_KW), lambda n: (rb(n), 0))
    ospec = pl.BlockSpec((rows, GLA_VW), lambda n: (rb(n), 0))
    sspec = pl.BlockSpec((GLA_HEADS, cb, GLA_DV, GLA_DK), lambda n: (0, rb(n), 0, 0))
    return cb, rows, nsteps, qspec, kspec, vspec, gspec, ospec, sspec


def _gla_units(cb, order_reversed):
    chunks = list(reversed(range(cb))) if order_reversed else list(range(cb))
    return [(c, h, slice(c * GLA_CHUNK, (c + 1) * GLA_CHUNK), slice(h * GLA_DK, (h + 1) * GLA_DK),
             slice(h * GLA_DV, (h + 1) * GLA_DV)) for c in chunks for h in range(GLA_HEADS)]


def _gla_fwd(proj, g, reverse, name):
    s = proj.shape[0]
    cb, rows, nsteps, qspec, kspec, vspec, gspec, ospec, sspec = _gla_blockspecs(s, reverse)

    def body(q_ref, k_ref, v_ref, g_ref, o_ref, st_ref, state):
        @pl.when(pl.program_id(0) == 0)
        def _():
            state[...] = jnp.zeros_like(state)

        allowed, _ = _chunk_masks(reverse)
        units = _gla_units(cb, reverse)
        terms = [_chunk_terms(q_ref, k_ref, g_ref, rs, hs, allowed, reverse) for _, _, rs, hs, _ in units]
        vals = [v_ref[rs, vs].astype(BF16) for _, _, rs, _, vs in units]
        raw = [(_dot_nt(t["qe"].astype(BF16), t["ke"].astype(BF16)), _dot_tn(v, t["kst"].astype(BF16)))
               for t, v in zip(terms, vals)]
        intra = [_dot(jnp.where(allowed, a, 0.0).astype(BF16), v) for (a, _), v in zip(raw, vals)]
        st = [state[h] for h in range(GLA_HEADS)]
        for (c, h, rs, _, vs), t, (_, kv), o_in in zip(units, terms, raw, intra):
            st_ref[h, c] = st[h]
            o_ref[rs, vs] = o_in + _dot_nt(t["qin"].astype(BF16), st[h].astype(BF16))
            st[h] = st[h] * t["dec"] + kv
        for h in range(GLA_HEADS):
            state[h] = st[h]

    return pl.pallas_call(
        body, name=name, grid=(nsteps,),
        in_specs=[qspec, kspec, vspec, gspec],
        out_specs=[ospec, sspec],
        out_shape=[jax.ShapeDtypeStruct((s, GLA_VW), F32),
                   jax.ShapeDtypeStruct((GLA_HEADS, s // GLA_CHUNK, GLA_DV, GLA_DK), F32)],
        scratch_shapes=[pltpu.VMEM((GLA_HEADS, GLA_DV, GLA_DK), F32)],
        compiler_params=_params(("arbitrary",)),
    )(proj, proj, proj, g)


def _gla_bwd(proj, g, do, states, reverse, name, merge=None):
    s = proj.shape[0]
    cb, rows, nsteps, qspec, kspec, vspec, gspec, ospec, sspec = _gla_blockspecs(s, not reverse)
    gla_cols = OFF_Z - OFF_GQ

    def body(q_ref, k_ref, v_ref, g_ref, do_ref, sp_ref, *rest):
        if merge is None:
            dq_ref, dk_ref, dv_ref, dg_ref, dstate = rest
        else:
            dq_o, dk_o, dv_o, dgr_ref, _, dp_ref, dg_ref, dstate = rest
        @pl.when(pl.program_id(0) == 0)
        def _():
            dstate[...] = jnp.zeros_like(dstate)

        allowed, seen_by = _chunk_masks(reverse)
        units = _gla_units(cb, not reverse)
        terms = [_chunk_terms(q_ref, k_ref, g_ref, rs, hs, allowed, reverse) for _, _, rs, hs, _ in units]
        vals = [v_ref[rs, vs].astype(BF16) for _, _, rs, _, vs in units]
        dos = [do_ref[rs, vs] for _, _, rs, _, vs in units]
        prevs = [sp_ref[h, c] for c, h, _, _, _ in units]
        raw = [(_dot_nt(t["qe"].astype(BF16), t["ke"].astype(BF16)), _dot_nt(do, v),
                _dot(do, sp.astype(BF16)), _dot_tn(do, t["qin"].astype(BF16)))
               for t, v, do, sp in zip(terms, vals, dos, prevs)]
        inner = []
        for t, do, (a, da, _, _) in zip(terms, dos, raw):
            da = jnp.where(allowed, da, 0.0).astype(BF16)
            inner.append((_dot(da, t["ke"].astype(BF16)), _dot_tn(da, t["qe"].astype(BF16)),
                          _dot_tn(jnp.where(allowed, a, 0.0).astype(BF16), do)))
        ds = [dstate[h] for h in range(GLA_HEADS)]
        outer = []
        for (c, h, _, _, _), t, v, sp, (_, _, _, inc) in zip(units, terms, vals, prevs, raw):
            ds_b = ds[h].astype(BF16)
            outer.append((_dot(v, ds_b), _dot_nt(t["kst"].astype(BF16), ds_b),
                          jnp.sum(sp * ds[h], axis=0, keepdims=True)))
            ds[h] = ds[h] * t["dec"] + inc
        for h in range(GLA_HEADS):
            dstate[h] = ds[h]
        seen_bf = jnp.where(seen_by, 1.0, 0.0).astype(BF16)
        rowi = lax.broadcasted_iota(jnp.int32, (GLA_CHUNK, GLA_DK), 0)
        for (c, h, rs, hs, vs), t, (_, _, dqin, _), (dqe, dke, dv_in), (dkst, dv_out, ddec) in zip(
                units, terms, raw, inner, outer):
            dq = (dqe * t["e_q"] + dqin * t["e_in"]) * (GLA_DK ** -0.5)
            dk = dke * t["e_k"] + dkst * t["e_st"]
            if merge is None:
                dq_ref[rs, hs], dk_ref[rs, hs], dv_ref[rs, vs] = dq, dk, dv_in + dv_out
            else:
                lo = OFF_GK - OFF_GQ + h * GLA_DK
                dp_ref[rs, hs] = (dq + dq_o[rs, hs]).astype(BF16)
                dp_ref[rs, lo:lo + GLA_DK] = (dk + dk_o[rs, hs]).astype(BF16)
                lo = OFF_GV - OFF_GQ + h * GLA_DV
                dp_ref[rs, lo:lo + GLA_DV] = (dv_in + dv_out + dv_o[rs, vs]).astype(BF16)
            kk = dkst * t["kst"]
            db = dqe * t["qe"] - dke * t["ke"] + dqin * t["qin"] - kk
            extra = jnp.sum(kk, axis=0, keepdims=True) + ddec * t["dec"]
            db = db + jnp.where(rowi == t["last"], extra, 0.0)
            dg_ref[rs, hs] = _dot_exact(seen_bf, db)
        if merge is not None:
            dp_ref[:, OFF_GR - OFF_GQ:gla_cols] = dgr_ref[...]

    scratch = [pltpu.VMEM((GLA_HEADS, GLA_DV, GLA_DK), F32)]
    if merge is None:
        return pl.pallas_call(
            body, name=name, grid=(nsteps,),
            in_specs=[qspec, kspec, vspec, gspec, ospec, sspec],
            out_specs=[gspec, gspec, ospec, gspec],
            out_shape=[jax.ShapeDtypeStruct((s, GLA_KW), F32), jax.ShapeDtypeStruct((s, GLA_KW), F32),
                       jax.ShapeDtypeStruct((s, GLA_VW), F32), jax.ShapeDtypeStruct((s, GLA_KW), F32)],
            scratch_shapes=scratch,
            compiler_params=_params(("arbitrary",)),
        )(proj, proj, proj, g, do, states)
    dproj = merge[4]
    block = gspec.index_map
    return pl.pallas_call(
        body, name=name, grid=(nsteps,),
        in_specs=[qspec, kspec, vspec, gspec, ospec, sspec, gspec, gspec, ospec, ospec, _ANY],
        out_specs=[pl.BlockSpec((rows, gla_cols), lambda n: (block(n)[0], OFF_GQ // gla_cols)), gspec],
        out_shape=[jax.ShapeDtypeStruct(dproj.shape, dproj.dtype), jax.ShapeDtypeStruct((s, GLA_KW), F32)],
        input_output_aliases={10: 0},
        scratch_shapes=scratch,
        compiler_params=_params(("arbitrary",)),
    )(proj, proj, proj, g, do, states, *merge)


def _gla_post(o_f, o_b, proj, g, cat, name="gla_post"):
    s = o_f.shape[0]

    def body(of_ref, ob_ref, gr_ref, g_ref, _, o_ref):
        gv = g_ref[...]
        for h in range(GLA_HEADS):
            sl = slice(h * GLA_DV, (h + 1) * GLA_DV)
            osum = of_ref[:, sl] + ob_ref[:, sl]
            r = lax.rsqrt(jnp.mean(osum * osum, axis=-1, keepdims=True) + EPS)
            gr = gr_ref[:, sl]
            o_ref[:, sl] = (osum * r * gv * (gr * _sigmoid(gr))).astype(BF16)

    blk = pl.BlockSpec((ROW_BLOCK, GLA_VW), lambda i: (i, 0))
    return pl.pallas_call(
        body, name=name, grid=(s // ROW_BLOCK,),
        in_specs=[blk, blk, pl.BlockSpec((ROW_BLOCK, GLA_VW), lambda i: (i, OFF_GR // GLA_VW)),
                  pl.BlockSpec((1, GLA_DV), lambda i: (0, 0)), pl.BlockSpec(memory_space=pl.ANY)],
        out_specs=pl.BlockSpec((ROW_BLOCK, GLA_VW), lambda i: (i, ATTN_W // GLA_VW)),
        out_shape=jax.ShapeDtypeStruct(cat.shape, cat.dtype),
        input_output_aliases={4: 0},
        compiler_params=_params(("parallel",)),
    )(o_f, o_b, proj, g, cat)


def _gla_post_bwd(dcat, o_f, o_b, proj, g, name="gla_post_bwd"):
    s = o_f.shape[0]

    def body(dy_ref, of_ref, ob_ref, gr_ref, g_ref, do_ref, dgr_ref, gg_ref):
        i = pl.program_id(0)
        gv = g_ref[...]
        gg = jnp.zeros((1, GLA_DV), F32)
        for h in range(GLA_HEADS):
            sl = slice(h * GLA_DV, (h + 1) * GLA_DV)
            osum = of_ref[:, sl] + ob_ref[:, sl]
            r = lax.rsqrt(jnp.mean(osum * osum, axis=-1, keepdims=True) + EPS)
            gr, dy = gr_ref[:, sl], dy_ref[:, sl]
            sg = _sigmoid(gr)
            dgr_ref[:, sl] = (dy * (osum * r * gv) * (sg * (1.0 + gr * (1.0 - sg)))).astype(BF16)
            dn = dy * (gr * sg)
            dng = dn * gv
            c = jnp.mean(dng * osum, axis=-1, keepdims=True)
            do_ref[:, sl] = (r * dng - osum * (r * r * r * c)).astype(BF16)
            gg = gg + jnp.sum(dn * osum * r, axis=0, keepdims=True)

        @pl.when(i == 0)
        def _():
            gg_ref[...] = jnp.zeros_like(gg_ref)

        gg_ref[...] += gg

    blk = pl.BlockSpec((ROW_BLOCK, GLA_VW), lambda i: (i, 0))
    vec = pl.BlockSpec((1, GLA_DV), lambda i: (0, 0))
    return pl.pallas_call(
        body, name=name, grid=(s // ROW_BLOCK,),
        in_specs=[pl.BlockSpec((ROW_BLOCK, GLA_VW), lambda i: (i, 1)), blk, blk,
                  pl.BlockSpec((ROW_BLOCK, GLA_VW), lambda i: (i, OFF_GR // GLA_VW)), vec],
        out_specs=[blk, blk, vec],
        out_shape=[jax.ShapeDtypeStruct((s, GLA_VW), BF16), jax.ShapeDtypeStruct((s, GLA_VW), BF16),
                   jax.ShapeDtypeStruct((1, GLA_DV), F32)],
        compiler_params=_params(("arbitrary",)),
    )(dcat, o_f, o_b, proj, g)


HALO = 16


def _extended(prev_ref, cur_ref, next_ref, i, s, tr, cs):
    first, last = i == 0, i == s // tr - 1
    prev = jnp.where(first, 0.0, prev_ref[:, cs].astype(F32))
    nxt = jnp.where(last, 0.0, next_ref[:, cs].astype(F32))
    return jnp.concatenate([prev, cur_ref[:, cs].astype(F32), nxt], axis=0)


FFN_ROWS = 512
FFN_COLS = 512


FFN_CHUNK = 256


def _lagged(i, ni, multiply, finish, rotate, init):
    chunks = [slice(c, c + FFN_CHUNK) for c in range(0, FFN_COLS, FFN_CHUNK)]

    @pl.when(i == 0)
    def _():
        init()

    @pl.when(i < 2)
    def _():
        rotate([multiply(cs) for cs in chunks], chunks)

    @pl.when((i >= 2) & (i < ni))
    def _():
        new = []
        for cs in chunks:
            new.append(multiply(cs))
            finish(cs)
        rotate(new, chunks)

    @pl.when(i >= ni)
    def _():
        for cs in chunks:
            finish(cs)
        rotate(None, chunks)


def _ffn_in(n2, w_gate, w_up, conv_w, conv_b, name="ffn_in"):
    s, d = n2.shape
    f = w_gate.shape[1]
    tm, tn, edge = FFN_ROWS, FFN_COLS, SUBLANES
    ni = s // tm
    ext = tm + 2 * edge

    def body(a_ref, wg_ref, wu_ref, w_ref, b_ref, gate_ref, silu_ref, slope_ref, act_ref, g_tile, u_tile, g_tail):
        i = pl.program_id(1)

        @pl.when(i == 0)
        def _():
            g_tile[...] = jnp.zeros_like(g_tile)
            u_tile[...] = jnp.zeros_like(u_tile)
            g_tail[...] = jnp.zeros_like(g_tail)

        a = a_ref[...]
        g_new = _dot(a, wg_ref[...])
        u_new = _dot(a, wu_ref[...])
        g_old, u_old = g_tile[...], u_tile[...]
        before = jnp.where(i == 1, 0.0, g_tail[...])
        after = jnp.where(i == ni, 0.0, g_new[0:edge])
        ge = jnp.concatenate([before, g_old, after], axis=0)
        w = w_ref[...]
        conv = (w[0:1] * pltpu.roll(ge, 1, 0) + w[1:2] * ge + w[2:3] * pltpu.roll(ge, ext - 1, 0))[edge:edge + tm]
        conv = conv + b_ref[...]
        sg = _sigmoid(conv)
        silu = conv * sg
        u_f = u_old.astype(F32)
        gate_ref[...] = g_old
        silu_ref[...] = silu.astype(BF16)
        slope_ref[...] = (u_f * (sg * (1.0 + conv * (1.0 - sg)))).astype(BF16)
        act_ref[...] = (silu * u_f).astype(BF16)
        g_tail[...] = g_old[tm - edge:tm]
        g_tile[...] = g_new
        u_tile[...] = u_new.astype(BF16)

    lag = pl.BlockSpec((tm, tn), lambda j, i: (jnp.maximum(i - 1, 0), j))
    return pl.pallas_call(
        body, name=name, grid=(f // tn, ni + 1),
        in_specs=[pl.BlockSpec((tm, d), lambda j, i: (jnp.minimum(i, ni - 1), 0)),
                  pl.BlockSpec((d, tn), lambda j, i: (0, j)), pl.BlockSpec((d, tn), lambda j, i: (0, j)),
                  pl.BlockSpec((3, tn), lambda j, i: (0, j)), pl.BlockSpec((1, tn), lambda j, i: (0, j))],
        out_specs=[lag, lag, lag, lag],
        out_shape=[jax.ShapeDtypeStruct((s, f), F32)] + [jax.ShapeDtypeStruct((s, f), BF16)] * 3,
        scratch_shapes=[pltpu.VMEM((tm, tn), F32), pltpu.VMEM((tm, tn), BF16), pltpu.VMEM((edge, tn), F32)],
        compiler_params=_params(("parallel", "arbitrary")),
    )(n2, w_gate, w_up, conv_w, conv_b)


def _ffn_mid_bwd(dh2, w_down, gate, silu, slope, conv_w, name="ffn_mid_bwd"):
    s, d = dh2.shape
    f = gate.shape[1]
    tm, tn = FFN_ROWS, FFN_COLS
    ni = s // tm
    ext = tm + 2 * HALO
    per, last_halo = tm // HALO, s // HALO - 1

    def body(a_ref, wd_ref, gp, gc, gn, sp, sc, sn, silu_ref, w_ref, dg_ref, du_ref, gw_ref, gb_ref,
             d_near, d_far, d_tail):
        i = pl.program_id(1)

        def multiply(cs):
            return _dot_nt(a_ref[...], wd_ref[cs, :])

        def finish(cs):
            before = jnp.where(i == 2, 0.0, d_tail[:, cs])
            after = jnp.where(i == ni + 1, 0.0, d_near[0:HALO, cs])
            d_mid = d_far[:, cs]
            de = jnp.concatenate([before, d_mid, after], axis=0)
            ge = _extended(gp, gc, gn, i - 2, s, tm, cs)
            w = w_ref[:, cs]
            g_prev, g_next = pltpu.roll(ge, 1, 0), pltpu.roll(ge, ext - 1, 0)
            inner = slice(HALO, HALO + tm)
            du_ref[:, cs] = (d_mid * silu_ref[:, cs].astype(F32)).astype(BF16)
            dconv = de * _extended(sp, sc, sn, i - 2, s, tm, cs)
            dgate = w[0:1] * pltpu.roll(dconv, ext - 1, 0) + w[1:2] * dconv + w[2:3] * pltpu.roll(dconv, 1, 0)
            dg_ref[:, cs] = dgate[inner].astype(BF16)
            dci = dconv[inner]
            gw_ref[0:1, cs] += jnp.sum(dci * g_prev[inner], axis=0, keepdims=True)
            gw_ref[1:2, cs] += jnp.sum(dci * ge[inner], axis=0, keepdims=True)
            gw_ref[2:3, cs] += jnp.sum(dci * g_next[inner], axis=0, keepdims=True)
            gb_ref[:, cs] += jnp.sum(dci, axis=0, keepdims=True)

        def rotate(new, chunks):
            d_tail[...] = d_far[tm - HALO:tm]
            d_far[...] = d_near[...]
            if new is not None:
                for cs, d_new in zip(chunks, new):
                    d_near[:, cs] = d_new

        def init():
            for r in (d_near, d_far, d_tail, gw_ref, gb_ref):
                r[...] = jnp.zeros_like(r)

        _lagged(i, ni, multiply, finish, rotate, init)

    def tile(i):
        return jnp.maximum(i - 2, 0)

    cur = pl.BlockSpec((tm, tn), lambda j, i: (tile(i), j))
    prev = pl.BlockSpec((HALO, tn), lambda j, i: (jnp.maximum(tile(i) * per - 1, 0), j))
    nxt = pl.BlockSpec((HALO, tn), lambda j, i: (jnp.minimum((tile(i) + 1) * per, last_halo), j))
    wspec = pl.BlockSpec((3, tn), lambda j, i: (0, j))
    bspec = pl.BlockSpec((1, tn), lambda j, i: (0, j))
    return pl.pallas_call(
        body, name=name, grid=(f // tn, ni + 2),
        in_specs=[pl.BlockSpec((tm, d), lambda j, i: (jnp.minimum(i, ni - 1), 0)),
                  pl.BlockSpec((tn, d), lambda j, i: (j, 0))] + [prev, cur, nxt] * 2 + [cur, wspec],
        out_specs=[cur, cur, wspec, bspec],
        out_shape=[jax.ShapeDtypeStruct((s, f), BF16), jax.ShapeDtypeStruct((s, f), BF16),
                   jax.ShapeDtypeStruct((3, f), F32), jax.ShapeDtypeStruct((1, f), F32)],
        scratch_shapes=[pltpu.VMEM((tm, tn), F32), pltpu.VMEM((tm, tn), F32), pltpu.VMEM((HALO, tn), F32)],
        compiler_params=_params(("parallel", "arbitrary")),
    )(dh2, w_down, gate, gate, gate, slope, slope, slope, silu, conv_w)


def _local_step(x, target, w, late_weights=None, grad_sink=None, first_dep=()):
    s = x.shape[0]
    tables = _rope_tables(s)
    uf, ub = _gate_matrices(w["gf_up"], w["gb_up"])
    if grad_sink is None:
        grad_sink = lambda names, grads: ()

    n1 = _rms_fwd(x, w["norm1_g"], "norm1")
    proj = _matmul([(n1, w["w_in"])], "nn", F32, 1024, 1280, D_MODEL, "in_proj", deps=first_dep)
    qkv = _rope_fwd(proj, tables)
    branches = [_attn_fwd(*qkv[di], d, f"attn_fwd_d{d}") for di, d in enumerate(DILATIONS)]
    o_mix, ao, lse = _attn_combine([b[0] for b in branches], [b[1] for b in branches], w["attn_norm_g"])
    g_f, g_b = _gla_gates(proj, uf, ub, w["gf_b"], w["gb_b"])
    o_f, st_f = _gla_fwd(proj, g_f, False, "gla_fwd_f")
    o_b, st_b = _gla_fwd(proj, g_b, True, "gla_fwd_b")
    cat = _gla_post(o_f, o_b, proj, w["gla_norm_g"], ao)
    if late_weights is not None:
        w = {**w, **late_weights("mixer", cat)}
    h1 = _matmul([(cat, w["w_out"])], "nn", F32, 512, 1024, D_MODEL, "out_proj", res=x)
    n2 = _rms_fwd(h1, w["norm2_g"], "norm2")
    if late_weights is not None:
        w = {**w, **late_weights("ffn", n2)}
    gate, silu, slope, act = _ffn_in(n2, w["w_gate"], w["w_up"], w["conv_w"], w["conv_b"])
    h2 = _matmul([(act, w["w_down"])], "nn", F32, 1024, 1024, 2816, "ffn_down", res=h1)
    dh2, dh2_b, loss_acc, g_final = _final_loss(h2, target, w["final_norm_g"])

    g_w_down = _matmul([(act, dh2_b)], "tn", BF16, 1408, 1024, 2048, "g_w_down")
    dep = grad_sink(["w_down"], [g_w_down])
    dgate, dup, g_conv_w, g_conv_b = _ffn_mid_bwd(dh2_b, w["w_down"], gate, silu, slope, w["conv_w"])
    g_w_gate = _matmul([(n2, dgate)], "tn", BF16, 2048, 512, 2048, "g_w_gate", deps=dep)
    g_w_up = _matmul([(n2, dup)], "tn", BF16, 2048, 512, 2048, "g_w_up")
    dep = grad_sink(["w_gate", "w_up"], [g_w_gate, g_w_up])
    dn2 = _matmul([(dgate, w["w_gate"])], "nt", F32, 1024, 1024, 2816, "d_n2_gate", deps=dep)
    dn2 = _matmul([(dup, w["w_up"])], "nt", F32, 1024, 1024, 2816, "d_n2_up", res=dn2)
    dh1, dh1_b, g_norm2 = _rms_bwd(dn2, h1, w["norm2_g"], dh2, "norm2_bwd")

    g_w_out = _matmul([(cat, dh1_b)], "tn", BF16, 1024, 1024, 2048, "g_w_out")
    dep = grad_sink(["w_out"], [g_w_out])
    dcat = _matmul([(dh1_b, w["w_out"])], "nt", F32, 512, 1024, D_MODEL, "d_cat", deps=dep)
    do_attn, delta, g_attn_norm = _attn_prebwd(dcat, o_mix, w["attn_norm_g"])
    grads = [_attn_bwd(*qkv[di], do_attn[di], lse[di], delta[di], d, f"attn_bwd_d{d}")
             for di, d in enumerate(DILATIONS)]
    dproj = _rope_bwd(grads, tables)
    do_gla, dgr, g_gla_norm = _gla_post_bwd(dcat, o_f, o_b, proj, w["gla_norm_g"])
    dq_f, dk_f, dv_f, dg_f = _gla_bwd(proj, g_f, do_gla, st_f, False, "gla_bwd_f")
    dproj, dg_b = _gla_bwd(proj, g_b, do_gla, st_b, True, "gla_bwd_b", merge=(dq_f, dk_f, dv_f, dgr, dproj))
    dproj, g_uf, g_ub, g_gf_b, g_gb_b = _gla_gates_bwd(dg_f, dg_b, proj, uf, ub, w["gf_b"], w["gb_b"], dproj)
    g_w_in = _matmul([(n1, dproj)], "tn", BF16, 1024, 1280, 2048, "g_w_in")
    dep = grad_sink(["w_in"], [g_w_in])
    dn1 = _matmul([(dproj, w["w_in"])], "nt", F32, 1024, 2048, 1280, "d_n1", deps=dep)
    grad_x, g_norm1 = _rms_bwd(dn1, x, w["norm1_g"], dh1, "norm1_bwd", bf16_copy=False)

    g = dict(norm1_g=g_norm1, w_in=g_w_in, gf_up=g_uf[:GLA_RANK], gf_b=g_gf_b,
             gb_up=g_ub[GLA_RANK:2 * GLA_RANK], gb_b=g_gb_b, gla_norm_g=g_gla_norm, attn_norm_g=g_attn_norm,
             w_out=g_w_out, norm2_g=g_norm2, w_gate=g_w_gate, w_up=g_w_up, conv_w=g_conv_w, conv_b=g_conv_b,
             w_down=g_w_down, final_norm_g=g_final)
    return loss_acc, grad_x, g


def _me_and_peers():
    x, y, c = lax.axis_index("x"), lax.axis_index("y"), lax.axis_index("c")
    me = 4 * x + 2 * y + c
    peers = []
    for kbits in range(1, N_DEV):
        px, py, pc = x ^ (kbits >> 2 & 1), y ^ (kbits >> 1 & 1), c ^ (kbits & 1)
        peers.append(((px, py, pc), 4 * px + 2 * py + pc))
    return me, peers


_HBM = pl.BlockSpec(memory_space=pltpu.HBM)
_SEM = pl.BlockSpec(memory_space=pltpu.SEMAPHORE)
_ANY = pl.BlockSpec(memory_space=pl.ANY)
_EFFECT = pltpu.SideEffectType.DATAFLOW_SIDE_EFFECTING


def _exchange_copies(src_refs, land_refs, send_sems, recv_sems, scatter):
    me, peers = _me_and_peers()
    out = []
    for a, (src, land) in enumerate(zip(src_refs, land_refs)):
        for kk, (dev, idx) in enumerate(peers):
            out.append(pltpu.make_async_remote_copy(
                src_ref=src.at[idx] if scatter else src, dst_ref=land.at[me],
                send_sem=send_sems.at[a * (N_DEV - 1) + kk], recv_sem=recv_sems.at[a * (N_DEV - 1) + kk],
                device_id=dev, device_id_type=MESH_ID))
    return out


def _exchange_start(srcs, lands, scatter, name, deps=()):
    n, nd = len(srcs), len(deps)

    def body(*refs):
        src_refs, land_refs = refs[:n], refs[n:2 * n]
        send_sems, recv_sems = refs[2 * n + nd:2 * n + nd + 2]
        token = refs[-1]
        for cp in _exchange_copies(src_refs, land_refs, send_sems, recv_sems, scatter):
            cp.start()
        token[...] = jnp.zeros_like(token)

    outs = pl.pallas_call(
        body, name=name,
        in_specs=[_HBM] * (2 * n) + [_ANY] * nd,
        out_specs=[_SEM, _SEM] + [_HBM] * (2 * n) + [pl.BlockSpec(memory_space=pltpu.VMEM)],
        out_shape=[pltpu.SemaphoreType.DMA((n * (N_DEV - 1),)), pltpu.SemaphoreType.DMA((n * (N_DEV - 1),))]
        + [pltpu.HBM(t.shape, t.dtype) for t in srcs] + [pltpu.HBM(t.shape, t.dtype) for t in lands]
        + [jax.ShapeDtypeStruct((SUBLANES, LANES), F32)],
        input_output_aliases={i: 2 + i for i in range(2 * n)},
        compiler_params=pltpu.CompilerParams(has_side_effects=_EFFECT),
    )(*[pltpu.with_memory_space_constraint(t, pltpu.HBM) for t in list(srcs) + list(lands)], *deps)
    send_sems, recv_sems = outs[0], outs[1]
    return dict(send=send_sems, recv=recv_sems, srcs=outs[2:2 + n], lands=outs[2 + n:2 + 2 * n],
                scatter=scatter, token=outs[-1])


def _exchange_wait(started, name, after):
    n = len(started["srcs"])
    scatter = started["scatter"]

    def body(*refs):
        src_refs, land_refs = refs[:n], refs[n:2 * n]
        send_sems, recv_sems = refs[2 * n], refs[2 * n + 1]
        for cp in _exchange_copies(src_refs, land_refs, send_sems, recv_sems, scatter):
            cp.wait_send()
            cp.wait_recv()

    outs = pl.pallas_call(
        body, name=name,
        in_specs=[_HBM] * (2 * n) + [_SEM, _SEM, _ANY],
        out_specs=[_HBM] * (2 * n),
        out_shape=[pltpu.HBM(t.shape, t.dtype) for t in started["srcs"]]
        + [pltpu.HBM(t.shape, t.dtype) for t in started["lands"]],
        input_output_aliases={i: i for i in range(2 * n)},
        compiler_params=pltpu.CompilerParams(has_side_effects=_EFFECT),
    )(*started["srcs"], *started["lands"], started["send"], started["recv"], after)
    return outs[:n], outs[n:]


def _all_gather_two_level(shard, name):
    def body(x_ref, out_ref, send_sems, recv_sems, local_sem):
        x, y, c = lax.axis_index("x"), lax.axis_index("y"), lax.axis_index("c")
        me, sibling = (x, y, c), (x, y, 1 - c)
        chips = [(1 - x, y), (x, 1 - y), (1 - x, 1 - y)]

        def slot(px, py, pc):
            return out_ref.at[4 * px + 2 * py + pc]

        def copy(k, block, to, src=None):
            return pltpu.make_async_remote_copy(
                src_ref=slot(*block) if src is None else src, dst_ref=slot(*block),
                send_sem=send_sems.at[k], recv_sem=recv_sems.at[k], device_id=to, device_id_type=MESH_ID)

        mine = pltpu.make_async_copy(x_ref, slot(*me), local_sem)
        mine.start()
        first = [copy(0, me, sibling, src=x_ref)]
        first += [copy(1 + j, me, (*chip, c), src=x_ref) for j, chip in enumerate(chips)]
        for cp in first:
            cp.start()
        passed = [copy(4 + j, (*chip, c), sibling) for j, chip in enumerate(chips)]
        for j, chip in enumerate(chips):
            copy(1 + j, (*chip, c), me).wait_recv()
            passed[j].start()
        copy(0, sibling, me).wait_recv()
        for j, chip in enumerate(chips):
            copy(4 + j, (*chip, 1 - c), me).wait_recv()
        for cp in first + passed:
            cp.wait_send()
        mine.wait()

    return pl.pallas_call(
        body, name=name,
        in_specs=[_ANY], out_specs=_ANY,
        out_shape=jax.ShapeDtypeStruct((N_DEV,) + shard.shape, shard.dtype),
        scratch_shapes=[pltpu.SemaphoreType.DMA((N_DEV - 1,)), pltpu.SemaphoreType.DMA((N_DEV - 1,)),
                        pltpu.SemaphoreType.DMA],
    )(shard)


def _all_gather_vmem(vec, name):
    r = vec.shape[0]

    def body(v_ref, o_ref, send_sems, recv_sems):
        me, peers = _me_and_peers()
        o_ref[me] = v_ref[...]
        sends = []
        for kk, (dev, _) in enumerate(peers):
            cp = pltpu.make_async_remote_copy(
                src_ref=v_ref, dst_ref=o_ref.at[me],
                send_sem=send_sems.at[kk], recv_sem=recv_sems.at[kk],
                device_id=dev, device_id_type=MESH_ID)
            cp.start()
            sends.append(cp)
        for kk, (dev, idx) in enumerate(peers):
            pltpu.make_async_remote_copy(
                src_ref=v_ref, dst_ref=o_ref.at[idx],
                send_sem=send_sems.at[kk], recv_sem=recv_sems.at[kk],
                device_id=dev, device_id_type=MESH_ID).wait_recv()
        for cp in sends:
            cp.wait_send()

    return pl.pallas_call(
        body, name=name,
        in_specs=[pl.BlockSpec(memory_space=pltpu.VMEM)],
        out_specs=pl.BlockSpec(memory_space=pltpu.VMEM),
        out_shape=jax.ShapeDtypeStruct((N_DEV, r, LANES), F32),
        scratch_shapes=[pltpu.SemaphoreType.DMA((N_DEV - 1,)), pltpu.SemaphoreType.DMA((N_DEV - 1,))],
        compiler_params=pltpu.CompilerParams(vmem_limit_bytes=VMEM_LIMIT),
    )(vec)


def _adamw_math(w, g, m, v):
    m = ADAM_B1 * m + (1.0 - ADAM_B1) * g
    v = ADAM_B2 * v + (1.0 - ADAM_B2) * (g * g)
    m_hat = m / (1.0 - ADAM_B1 ** ADAM_STEP)
    v_hat = v / (1.0 - ADAM_B2 ** ADAM_STEP)
    delta = -ADAM_LR * (m_hat / (jnp.sqrt(v_hat) + ADAM_EPS) + ADAM_WD * w)
    return delta, m, v


def _adamw_sum(parts, w, m, v, tr, name, own=None, me=None):
    r, c = w.shape

    def body(*refs):
        if own is None:
            p_ref, w_ref, m_ref, v_ref, g_ref, d_ref, nm_ref, nv_ref = refs
            terms = [p_ref[kk] for kk in range(N_DEV)]
        else:
            me_ref, p_ref, own_ref, w_ref, m_ref, v_ref, g_ref, d_ref, nm_ref, nv_ref = refs
            terms = [jnp.where(me_ref[0] == kk, own_ref[0], p_ref[kk]).astype(F32) for kk in range(N_DEV)]
        g = terms[0]
        for t in terms[1:]:
            g = g + t
        g_ref[...] = g
        d_ref[...], nm_ref[...], nv_ref[...] = _adamw_math(w_ref[...], g, m_ref[...], v_ref[...])

    out_shape = [jax.ShapeDtypeStruct((r, c), F32)] * 4
    if own is None:
        blk = pl.BlockSpec((tr, c), lambda i: (i, 0))
        return pl.pallas_call(
            body, name=name, grid=(r // tr,),
            in_specs=[pl.BlockSpec((N_DEV, tr, c), lambda i: (0, i, 0)), blk, blk, blk],
            out_specs=[blk] * 4, out_shape=out_shape,
            compiler_params=_params(("parallel",)),
        )(parts, w, m, v)
    blk = pl.BlockSpec((tr, c), lambda i, me_ref: (i, 0))
    return pl.pallas_call(
        body, name=name,
        grid_spec=pltpu.PrefetchScalarGridSpec(
            num_scalar_prefetch=1, grid=(r // tr,),
            in_specs=[pl.BlockSpec((N_DEV, tr, c), lambda i, me_ref: (0, i, 0)),
                      pl.BlockSpec((1, tr, c), lambda i, me_ref: (me_ref[0], i, 0)), blk, blk, blk],
            out_specs=[blk] * 4),
        out_shape=out_shape,
        compiler_params=_params(("parallel",)),
    )(jnp.reshape(me, (1,)).astype(jnp.int32), parts, own, w, m, v)


_SMALL = ("norm1_g", "gf_b", "gb_b", "gla_norm_g", "attn_norm_g", "norm2_g", "conv_b", "final_norm_g",
          "gf_up", "gb_up", "conv_w")


def _pack(named):
    flat = jnp.concatenate([jnp.ravel(t).astype(F32) for t in named])
    tile = SUBLANES * LANES
    total = -(-flat.shape[0] // tile) * tile
    return jnp.pad(flat, (0, total - flat.shape[0])).reshape(total // LANES, LANES)


def _unpack(packed, shapes):
    flat = packed.reshape(-1)
    out, off = [], 0
    for shp in shapes:
        size = int(np.prod(shp))
        out.append(flat[off:off + size].reshape(shp))
        off += size
    return out


def kernel(x, norm1_g, w_in, gf_up, gf_b, gb_up, gb_b, gla_norm_g, attn_norm_g, w_out, norm2_g, w_gate, w_up, conv_w, conv_b, w_down, final_norm_g, loss_target, m_norm1_g, m_w_in, m_gf_up, m_gf_b, m_gb_up, m_gb_b, m_gla_norm_g, m_attn_norm_g, m_w_out, m_norm2_g, m_w_gate, m_w_up, m_conv_w, m_conv_b, m_w_down, m_final_norm_g, v_norm1_g, v_w_in, v_gf_up, v_gf_b, v_gb_up, v_gb_b, v_gla_norm_g, v_attn_norm_g, v_w_out, v_norm2_g, v_w_gate, v_w_up, v_conv_w, v_conv_b, v_w_down, v_final_norm_g):
    names = ("norm1_g", "w_in", "gf_up", "gf_b", "gb_up", "gb_b", "gla_norm_g", "attn_norm_g", "w_out", "norm2_g",
             "w_gate", "w_up", "conv_w", "conv_b", "w_down", "final_norm_g")
    ws = dict(zip(names, (norm1_g, w_in, gf_up, gf_b, gb_up, gb_b, gla_norm_g, attn_norm_g, w_out, norm2_g,
                          w_gate, w_up, conv_w, conv_b, w_down, final_norm_g)))
    ms = dict(zip(names, (m_norm1_g, m_w_in, m_gf_up, m_gf_b, m_gb_up, m_gb_b, m_gla_norm_g, m_attn_norm_g, m_w_out,
                          m_norm2_g, m_w_gate, m_w_up, m_conv_w, m_conv_b, m_w_down, m_final_norm_g)))
    vs = dict(zip(names, (v_norm1_g, v_w_in, v_gf_up, v_gf_b, v_gb_up, v_gb_b, v_gla_norm_g, v_attn_norm_g, v_w_out,
                          v_norm2_g, v_w_gate, v_w_up, v_conv_w, v_conv_b, v_w_down, v_final_norm_g)))
    me = 4 * lax.axis_index("x") + 2 * lax.axis_index("y") + lax.axis_index("c")
    big = ("w_in", "w_out", "w_gate", "w_up", "w_down")
    col_sharded = ("w_in", "w_gate", "w_up")

    def gather_start(group, name, deps=()):
        shards = [ws[n][0].astype(BF16) for n in group]
        lands = [lax.empty((N_DEV,) + t.shape, BF16) for t in shards]
        return _exchange_start(shards, lands, False, name, deps)

    def gather_finish(group, started, name, after):
        full = {}
        for n, own, t in zip(group, *_exchange_wait(started, name, after)):
            t = lax.dynamic_update_slice(t, own[None], (me, 0, 0))
            if n in col_sharded:
                full[n] = jnp.transpose(t, (1, 0, 2)).reshape(t.shape[1], N_DEV * t.shape[2])
            else:
                full[n] = t.reshape(N_DEV * t.shape[1], t.shape[2])
        return full

    w_in_all = _all_gather_two_level(ws["w_in"][0].astype(BF16), "gather_w_in")
    full = {"w_in": jnp.pad(jnp.transpose(w_in_all, (1, 0, 2)).reshape(D_MODEL, IN_WIDTH),
                            ((0, 0), (0, IN_PAD - IN_WIDTH)))}
    late = {"mixer": ("w_out",), "ffn": ("w_gate", "w_up", "w_down")}
    started_late = {"mixer": gather_start(late["mixer"], "gather_w_out_start", deps=(full["w_in"],))}
    started_late["ffn"] = gather_start(late["ffn"], "gather_ffn_start", deps=(started_late["mixer"]["token"],))

    def late_weights(part, after):
        return gather_finish(late[part], started_late[part], "gather_" + part + "_wait", after)

    small_sharded = ("gf_up", "gb_up", "conv_w")
    sm = _all_gather_vmem(_pack([ws[n][0] for n in small_sharded]), "gather_small")
    shard_shapes = [ws[n][0].shape for n in small_sharded]
    per_dev = [_unpack(sm[d], shard_shapes) for d in range(N_DEV)]
    for i, n in enumerate(small_sharded):
        full[n] = jnp.concatenate([per_dev[d][i] for d in range(N_DEV)], axis=1)
    for n in ("norm1_g", "gf_b", "gb_b", "gla_norm_g", "attn_norm_g", "norm2_g", "conv_b"):
        full[n] = ws[n]
    full["final_norm_g"] = final_norm_g.reshape(1, D_MODEL)

    in_flight = []

    def grad_sink(group, grads):
        partials = []
        for n, t in zip(group, grads):
            if n == "w_in":
                t = t[:, :IN_WIDTH]
            t = t.astype(BF16)
            if n in col_sharded:
                t = jnp.transpose(t.reshape(t.shape[0], N_DEV, t.shape[1] // N_DEV), (1, 0, 2))
            else:
                t = t.reshape(N_DEV, t.shape[0] // N_DEV, t.shape[1])
            partials.append(t)
        lands = [lax.empty(t.shape, t.dtype) for t in partials]
        started = _exchange_start(partials, lands, True, "exchange_" + "_".join(group) + "_start")
        in_flight.append((group, started))
        return (started["token"],)

    loss_acc, grad_x, g = _local_step(x[0], loss_target[0], full, late_weights, grad_sink,
                                      first_dep=(started_late["ffn"]["token"],))

    out = {}
    for group, started in in_flight:
        sent, landed = _exchange_wait(started, "exchange_" + "_".join(group) + "_wait", grad_x)
        for n, parts, own in zip(group, landed, sent):
            out[n] = _adamw_sum(parts, ws[n][0], ms[n][0], vs[n][0], 64, "adamw_" + n, own=own, me=me)

    small_full_shapes = [g[n].shape for n in _SMALL]
    gsmall = _pack([g[n] for n in _SMALL] + [loss_acc[0:1, 0:1]])
    gathered_small = _all_gather_vmem(gsmall, "gather_small_grads")

    def full_small(d):
        parts = []
        for n in _SMALL:
            t = d[n].reshape(d[n].shape[-2:]) if d[n].ndim == 3 else d[n].reshape(1, -1)
            if n in small_sharded:
                wide = jnp.zeros((t.shape[0], t.shape[1] * N_DEV), F32)
                t = lax.dynamic_update_slice_in_dim(wide, t, me * t.shape[1], axis=1)
            parts.append(t)
        return _pack(parts + [jnp.zeros((1, 1), F32)])

    rows = gsmall.shape[0]
    res_small = _adamw_sum(gathered_small, full_small(ws), full_small(ms), full_small(vs), rows, "adamw_small")
    loss = res_small[0].reshape(-1)[sum(int(np.prod(sh)) for sh in small_full_shapes)]
    unpacked = [_unpack(t, small_full_shapes) for t in res_small]
    for i, n in enumerate(_SMALL):
        vals = [u[i] for u in unpacked]
        if n in small_sharded:
            width = vals[0].shape[1] // N_DEV
            vals = [lax.dynamic_slice_in_dim(t, me * width, width, axis=1) for t in vals]
        out[n] = vals

    result = [loss, grad_x[None]]
    for kind in range(4):
        for n in names:
            result.append(out[n][kind].reshape(ws[n].shape))
    return tuple(result)
```

```python
import functools

import numpy as np
import jax
import jax.numpy as jnp
from jax import lax
from jax.experimental import pallas as pl
from jax.experimental.pallas import tpu as pltpu

F32 = jnp.float32
BF16 = jnp.bfloat16

D_MODEL = 2048
ATTN_W = 1024
ATTN_HEADS = 8
HEAD_DIM = 128
ROPE_DIM = 32
ROPE_THETA = 500000.0
DILATIONS = (1, 4, 16)
N_SIDE = 64
GLA_KW = 512
GLA_VW = 1024
GLA_HEADS = 4
GLA_DK = 128
GLA_DV = 256
GLA_RANK = 16
GLA_GATE_NORM = 16.0
GLA_CHUNK = 64
IN_WIDTH = 6176
IN_PAD = 6400
D_FF = 5632
EPS = 1e-6
N_DEV = 8

OFF_AQ, OFF_AK, OFF_AV = 0, 1024, 2048
OFF_GQ, OFF_GK, OFF_GV, OFF_GR, OFF_Z = 3072, 3584, 4096, 5120, 6144

ADAM_LR, ADAM_B1, ADAM_B2, ADAM_EPS, ADAM_WD, ADAM_STEP = 0.001, 0.9, 0.999, 1e-08, 0.01, 10

LANES = 128
SUBLANES = 8
VMEM_LIMIT = 56 * 1024 * 1024
ROW_BLOCK = 256
ATTN_BLOCK = 128
GLA_CHUNKS_PER_STEP = 4
NEG = -1e30
MESH_ID = pl.DeviceIdType.MESH


def _params(sem):
    return pltpu.CompilerParams(dimension_semantics=sem, vmem_limit_bytes=VMEM_LIMIT)


def _dot(a, b):
    return lax.dot_general(a, b, (((1,), (0,)), ((), ())), preferred_element_type=F32)


def _dot_nt(a, b):
    return lax.dot_general(a, b, (((1,), (1,)), ((), ())), preferred_element_type=F32)


def _dot_tn(a, b):
    return lax.dot_general(a, b, (((0,), (0,)), ((), ())), preferred_element_type=F32)


def _sigmoid(x):
    return 0.5 * jnp.tanh(0.5 * x) + 0.5


def _matmul(pairs, mode, out_dtype, tm, tn, tk, name, res=None, deps=()):
    a0, b0 = pairs[0]
    if mode == "nn":
        (m, kdim), n = a0.shape, b0.shape[1]
    elif mode == "nt":
        (m, kdim), n = a0.shape, b0.shape[0]
    else:
        (kdim, m), n = a0.shape, b0.shape[1]
    assert m % tm == 0 and n % tn == 0 and kdim % tk == 0, (name, m, n, kdim)
    nk = kdim // tk
    npairs = len(pairs)
    steps = nk * npairs
    dot = {"nn": _dot, "nt": _dot_nt, "tn": _dot_tn}[mode]

    def kidx(p):
        return lambda k: jnp.clip(k - p * nk, 0, nk - 1)

    in_specs, args = [], []
    for p, (a, b) in enumerate(pairs):
        kk = kidx(p)
        if mode == "nn":
            in_specs += [pl.BlockSpec((tm, tk), lambda i, j, k, kk=kk: (i, kk(k))),
                         pl.BlockSpec((tk, tn), lambda i, j, k, kk=kk: (kk(k), j))]
        elif mode == "nt":
            in_specs += [pl.BlockSpec((tm, tk), lambda i, j, k, kk=kk: (i, kk(k))),
                         pl.BlockSpec((tn, tk), lambda i, j, k, kk=kk: (j, kk(k)))]
        else:
            in_specs += [pl.BlockSpec((tk, tm), lambda i, j, k, kk=kk: (kk(k), i)),
                         pl.BlockSpec((tk, tn), lambda i, j, k, kk=kk: (kk(k), j))]
        args += [a, b]
    if res is not None:
        in_specs.append(pl.BlockSpec((tm, tn), lambda i, j, k: (i, j)))
        args.append(res)
    in_specs += [pl.BlockSpec(memory_space=pl.ANY)] * len(deps)
    args += list(deps)

    def body(*refs):
        ab = refs[:2 * npairs]
        res_ref = refs[2 * npairs] if res is not None else None
        o_ref = refs[2 * npairs + (1 if res is not None else 0) + len(deps)]

        def finish(acc):
            if res_ref is not None:
                acc = acc + res_ref[...]
            o_ref[...] = acc.astype(out_dtype)

        if steps == 1:
            finish(dot(ab[0][...], ab[1][...]))
            return
        acc_ref = refs[-1]
        k = pl.program_id(2)

        @pl.when(k == 0)
        def _():
            acc_ref[...] = jnp.zeros_like(acc_ref)

        for p in range(npairs):
            @pl.when((k >= p * nk) & (k < (p + 1) * nk))
            def _(p=p):
                acc_ref[...] += dot(ab[2 * p][...], ab[2 * p + 1][...])

        @pl.when(k == steps - 1)
        def _():
            finish(acc_ref[...])

    return pl.pallas_call(
        body, name=name,
        grid=(m // tm, n // tn, steps),
        in_specs=in_specs,
        out_specs=pl.BlockSpec((tm, tn), lambda i, j, k: (i, j)),
        out_shape=jax.ShapeDtypeStruct((m, n), out_dtype),
        scratch_shapes=[] if steps == 1 else [pltpu.VMEM((tm, tn), F32)],
        compiler_params=_params(("parallel", "parallel", "arbitrary")),
    )(*args)


def _rms_fwd(x, g, name):
    s, d = x.shape

    def body(x_ref, g_ref, o_ref):
        xv = x_ref[...]
        r = lax.rsqrt(jnp.mean(xv * xv, axis=-1, keepdims=True) + EPS)
        o_ref[...] = (xv * r * g_ref[...]).astype(BF16)

    return pl.pallas_call(
        body, name=name, grid=(s // ROW_BLOCK,),
        in_specs=[pl.BlockSpec((ROW_BLOCK, d), lambda i: (i, 0)), pl.BlockSpec((1, d), lambda i: (0, 0))],
        out_specs=pl.BlockSpec((ROW_BLOCK, d), lambda i: (i, 0)),
        out_shape=jax.ShapeDtypeStruct((s, d), BF16),
        compiler_params=_params(("parallel",)),
    )(x, g)


def _rms_bwd(dn, x, g, dres, name, bf16_copy=True):
    s, d = x.shape

    def body(dn_ref, x_ref, g_ref, dres_ref, dx_ref, *rest):
        gg_ref = rest[-1]
        i = pl.program_id(0)
        xv, dnv = x_ref[...], dn_ref[...]
        r = lax.rsqrt(jnp.mean(xv * xv, axis=-1, keepdims=True) + EPS)
        dng = dnv * g_ref[...]
        c = jnp.mean(dng * xv, axis=-1, keepdims=True)
        dx = dres_ref[...] + r * dng - xv * (r * r * r * c)
        dx_ref[...] = dx
        if bf16_copy:
            rest[0][...] = dx.astype(BF16)

        @pl.when(i == 0)
        def _():
            gg_ref[...] = jnp.zeros_like(gg_ref)

        gg_ref[...] += jnp.sum(dnv * xv * r, axis=0, keepdims=True)

    row = pl.BlockSpec((ROW_BLOCK, d), lambda i: (i, 0))
    vec = pl.BlockSpec((1, d), lambda i: (0, 0))
    return pl.pallas_call(
        body, name=name, grid=(s // ROW_BLOCK,),
        in_specs=[row, row, vec, row],
        out_specs=[row] + [row] * bf16_copy + [vec],
        out_shape=[jax.ShapeDtypeStruct((s, d), F32)] + [jax.ShapeDtypeStruct((s, d), BF16)] * bf16_copy
        + [jax.ShapeDtypeStruct((1, d), F32)],
        compiler_params=_params(("arbitrary",)),
    )(dn, x, g, dres)


def _final_loss(h2, target, g, name="final_loss"):
    s, d = h2.shape

    def body(h_ref, t_ref, g_ref, dh_ref, dhb_ref, loss_ref, gg_ref):
        i = pl.program_id(0)
        hv, gv = h_ref[...], g_ref[...]
        r = lax.rsqrt(jnp.mean(hv * hv, axis=-1, keepdims=True) + EPS)
        e = hv * r * gv - t_ref[...]
        dy = e * (1.0 / d)
        dyg = dy * gv
        c = jnp.mean(dyg * hv, axis=-1, keepdims=True)
        dh = r * dyg - hv * (r * r * r * c)
        dh_ref[...] = dh
        dhb_ref[...] = dh.astype(BF16)

        @pl.when(i == 0)
        def _():
            gg_ref[...] = jnp.zeros_like(gg_ref)
            loss_ref[...] = jnp.zeros_like(loss_ref)

        gg_ref[...] += jnp.sum(dy * hv * r, axis=0, keepdims=True)
        loss_ref[...] += jnp.sum(jnp.sum(e * e, axis=-1, keepdims=True), axis=0, keepdims=True) * (0.5 / d)

    row = pl.BlockSpec((ROW_BLOCK, d), lambda i: (i, 0))
    vec = pl.BlockSpec((1, d), lambda i: (0, 0))
    return pl.pallas_call(
        body, name=name, grid=(s // ROW_BLOCK,),
        in_specs=[row, row, vec],
        out_specs=[row, row, pl.BlockSpec((SUBLANES, LANES), lambda i: (0, 0)), vec],
        out_shape=[jax.ShapeDtypeStruct((s, d), F32), jax.ShapeDtypeStruct((s, d), BF16),
                   jax.ShapeDtypeStruct((SUBLANES, LANES), F32), jax.ShapeDtypeStruct((1, d), F32)],
        compiler_params=_params(("arbitrary",)),
    )(h2, target, g)


def _rope_tables(s):
    pos = jnp.arange(s, dtype=F32)
    inv_freq = ROPE_THETA ** (-jnp.arange(0, ROPE_DIM, 2, dtype=F32) / ROPE_DIM)
    ang = pos[:, None] * inv_freq[None, :]
    cos, sin = jnp.cos(ang), jnp.sin(ang)
    half = ROPE_DIM // 2
    rest = HEAD_DIM - ROPE_DIM
    c = jnp.concatenate([cos, cos, jnp.ones((s, rest), F32)], axis=1)
    sm = jnp.concatenate([-sin, jnp.zeros((s, half + rest), F32)], axis=1)
    sp = jnp.concatenate([jnp.zeros((s, half), F32), sin, jnp.zeros((s, rest), F32)], axis=1)
    return c, sm, sp


def _res_shape(s, groups, dil, dtype):
    return jax.ShapeDtypeStruct((s // dil, dil * groups * LANES), dtype)


def _res_spec(groups, dil):
    return pl.BlockSpec((ROW_BLOCK // dil, dil * groups * LANES), lambda i: (i, 0))


def _to_residues(scr, o_ref, dil):
    groups, rows = scr.shape[0], ROW_BLOCK // dil
    for r in range(dil):
        for h in range(groups):
            piece = scr[h] if dil == 1 else scr.at[h][pl.ds(r, rows, stride=dil), :]
            o_ref[:, (r * groups + h) * LANES:(r * groups + h + 1) * LANES] = piece.astype(o_ref.dtype)


def _from_residues(i_ref, scr, dil):
    groups, rows = scr.shape[0], ROW_BLOCK // dil
    for r in range(dil):
        for h in range(groups):
            piece = i_ref[:, (r * groups + h) * LANES:(r * groups + h + 1) * LANES].astype(F32)
            if dil == 1:
                scr[h] = piece
            else:
                scr.at[h][pl.ds(r, rows, stride=dil), :] = piece


def _rope_fwd(proj, tables, name="rope_fwd"):
    s = proj.shape[0]
    half = ROPE_DIM // 2
    nd = len(DILATIONS)

    def body(p_ref, c_ref, sm_ref, sp_ref, *rest):
        outs, scr = rest[:3 * nd], rest[3 * nd]
        c, sm, sp = c_ref[...], sm_ref[...], sp_ref[...]
        for gi, off in enumerate((OFF_AQ, OFF_AK, OFF_AV)):
            for h in range(ATTN_HEADS):
                t = p_ref[:, off + h * HEAD_DIM: off + (h + 1) * HEAD_DIM]
                if off != OFF_AV:
                    t = t * c + pltpu.roll(t, HEAD_DIM - half, 1) * sm + pltpu.roll(t, half, 1) * sp
                scr[h] = t
            for di, dil in enumerate(DILATIONS):
                _to_residues(scr, outs[3 * di + gi], dil)

    tab = pl.BlockSpec((ROW_BLOCK, HEAD_DIM), lambda i: (i, 0))
    outs = pl.pallas_call(
        body, name=name, grid=(s // ROW_BLOCK,),
        in_specs=[pl.BlockSpec((ROW_BLOCK, 3 * ATTN_W), lambda i: (i, 0)), tab, tab, tab],
        out_specs=[_res_spec(ATTN_HEADS, d) for d in DILATIONS for _ in range(3)],
        out_shape=[_res_shape(s, ATTN_HEADS, d, BF16) for d in DILATIONS for _ in range(3)],
        scratch_shapes=[pltpu.VMEM((ATTN_HEADS, ROW_BLOCK, LANES), F32)],
        compiler_params=_params(("parallel",)),
    )(proj, *tables)
    return [tuple(outs[3 * di:3 * di + 3]) for di in range(nd)]


def _rope_bwd(grads, tables, name="rope_bwd"):
    s = grads[0][0].shape[0] * DILATIONS[0]
    half = ROPE_DIM // 2
    nd = len(DILATIONS)

    def body(*refs):
        ins = refs[:3 * nd]
        c_ref, sm_ref, sp_ref, o_ref = refs[3 * nd:3 * nd + 4]
        scrs = refs[3 * nd + 4:]
        c, sm, sp = c_ref[...], sm_ref[...], sp_ref[...]
        for gi, off in enumerate((OFF_AQ, OFF_AK, OFF_AV)):
            for di, dil in enumerate(DILATIONS):
                _from_residues(ins[3 * di + gi], scrs[di], dil)
            for h in range(ATTN_HEADS):
                t = scrs[0][h]
                for scr in scrs[1:]:
                    t = t + scr[h]
                if off != OFF_AV:
                    t = t * c + pltpu.roll(t * sm, half, 1) + pltpu.roll(t * sp, HEAD_DIM - half, 1)
                o_ref[:, off + h * HEAD_DIM: off + (h + 1) * HEAD_DIM] = t.astype(BF16)

    tab = pl.BlockSpec((ROW_BLOCK, HEAD_DIM), lambda i: (i, 0))
    return pl.pallas_call(
        body, name=name, grid=(s // ROW_BLOCK,),
        in_specs=[_res_spec(ATTN_HEADS, d) for d in DILATIONS for _ in range(3)] + [tab, tab, tab],
        out_specs=pl.BlockSpec((ROW_BLOCK, 3 * ATTN_W), lambda i: (i, 0)),
        out_shape=jax.ShapeDtypeStruct((s, IN_PAD), BF16),
        scratch_shapes=[pltpu.VMEM((ATTN_HEADS, ROW_BLOCK, LANES), F32) for _ in DILATIONS],
        compiler_params=_params(("parallel",)),
    )(*[t for g in grads for t in g], *tables)


ATTN_GROUP = 2


def _window_specs(nsteps, width):
    rows, hb = ATTN_GROUP * ATTN_BLOCK, N_SIDE
    per = rows // hb
    cur = pl.BlockSpec((rows, width), lambda r, j: (j, r))
    prev = pl.BlockSpec((hb, width), lambda r, j: (jnp.maximum(per * j - 1, 0), r))
    nxt = pl.BlockSpec((hb, width), lambda r, j: (jnp.minimum(per * (j + 1), per * nsteps - 1), r))
    return prev, cur, nxt


def _block(ref, b, sl):
    return ref[b * ATTN_BLOCK:(b + 1) * ATTN_BLOCK, sl]


def _edge(prev_ref, cur_ref, next_ref, b, sl):
    qb, hb = ATTN_BLOCK, N_SIDE
    before = prev_ref[:, sl] if b == 0 else cur_ref[b * qb - hb:b * qb, sl]
    after = next_ref[:, sl] if b == ATTN_GROUP - 1 else cur_ref[(b + 1) * qb:(b + 1) * qb + hb, sl]
    return jnp.concatenate([before, after], axis=0)


def _band_masks(j, length):
    qb, hb = ATTN_BLOCK, N_SIDE
    row = lax.broadcasted_iota(jnp.int32, (qb, qb), 0)
    col = lax.broadcasted_iota(jnp.int32, (qb, qb), 1)

    def edge_pos(i):
        return j * qb - hb + i + jnp.where(i >= hb, qb, 0)

    def ok(a, b, outside):
        return (jnp.abs(a - b) <= N_SIDE) & (outside >= 0) & (outside < length)

    cur = jnp.abs(row - col) <= N_SIDE
    edge_k = ok(j * qb + row, edge_pos(col), edge_pos(col))
    edge_q = ok(edge_pos(row), j * qb + col, edge_pos(row))
    return cur, edge_k, edge_q


def _attn_fwd(q, k, v, dil, name):
    length = q.shape[0]
    qb = ATTN_BLOCK
    nsteps = length // (ATTN_GROUP * qb)
    scale = HEAD_DIM ** -0.5

    def body(q_ref, kp_ref, kc_ref, kn_ref, vp_ref, vc_ref, vn_ref, o_ref, lse_ref):
        masks = [_band_masks(pl.program_id(1) * ATTN_GROUP + b, length) for b in range(ATTN_GROUP)]
        lane = lax.broadcasted_iota(jnp.int32, (qb, LANES), 1)
        units = [(b, h, slice(h * HEAD_DIM, (h + 1) * HEAD_DIM)) for b in range(ATTN_GROUP)
                 for h in range(ATTN_HEADS)]
        scores = [(_dot_nt(_block(q_ref, b, sl), _block(kc_ref, b, sl)),
                   _dot_nt(_block(q_ref, b, sl), _edge(kp_ref, kc_ref, kn_ref, b, sl))) for b, _, sl in units]
        probs = []
        lse_acc = [jnp.zeros((qb, LANES), F32) for _ in range(ATTN_GROUP)]
        for (b, h, _), (s_c, s_e) in zip(units, scores):
            valid_c, valid_e, _ = masks[b]
            s_c = jnp.where(valid_c, s_c * scale, NEG)
            s_e = jnp.where(valid_e, s_e * scale, NEG)
            m = jnp.max(jnp.maximum(s_c, s_e), axis=-1, keepdims=True)
            p_c, p_e = jnp.exp(s_c - m), jnp.exp(s_e - m)
            den = jnp.sum(p_c + p_e, axis=-1, keepdims=True)
            probs.append((p_c.astype(BF16), p_e.astype(BF16), 1.0 / den))
            lse_acc[b] = jnp.where(lane == h, m + jnp.log(den), lse_acc[b])
        for (b, _, sl), (p_c, p_e, inv) in zip(units, probs):
            o_ref[b * qb:(b + 1) * qb, sl] = (_dot(p_c, _block(vc_ref, b, sl))
                                              + _dot(p_e, _edge(vp_ref, vc_ref, vn_ref, b, sl))) * inv
        for b in range(ATTN_GROUP):
            lse_ref[b * qb:(b + 1) * qb, :] = lse_acc[b]

    prev, cur, nxt = _window_specs(nsteps, ATTN_W)
    return pl.pallas_call(
        body, name=name, grid=(dil, nsteps),
        in_specs=[cur, prev, cur, nxt, prev, cur, nxt],
        out_specs=[cur, pl.BlockSpec((ATTN_GROUP * qb, LANES), lambda r, j: (j, r))],
        out_shape=[jax.ShapeDtypeStruct((length, dil * ATTN_W), F32),
                   jax.ShapeDtypeStruct((length, dil * LANES), F32)],
        compiler_params=_params(("parallel", "parallel")),
    )(q, k, k, k, v, v, v)


def _attn_combine(outs, lses, g, name="attn_combine"):
    s = outs[0].shape[0] * DILATIONS[0]
    nd = len(DILATIONS)

    def body(*refs):
        o_refs, l_refs = refs[:nd], refs[nd:2 * nd]
        g_ref, o_ref, n_ref = refs[2 * nd:2 * nd + 3]
        lse_outs = refs[2 * nd + 3:3 * nd + 3]
        o_scr, l_scr = refs[3 * nd + 3:4 * nd + 3], refs[4 * nd + 3:5 * nd + 3]
        for di, dil in enumerate(DILATIONS):
            _from_residues(o_refs[di], o_scr[di], dil)
            _from_residues(l_refs[di], l_scr[di], dil)
        ls = [scr[0] for scr in l_scr]
        m = ls[0]
        for l in ls[1:]:
            m = jnp.maximum(m, l)
        es = [jnp.exp(l - m) for l in ls]
        z = es[0]
        for e in es[1:]:
            z = z + e
        ws = [e / z for e in es]
        l_scr[0][0] = m + jnp.log(z)
        for di, dil in enumerate(DILATIONS):
            _to_residues(l_scr[0], lse_outs[di], dil)
        ssq = jnp.zeros((ROW_BLOCK, 1), F32)
        for h in range(ATTN_HEADS):
            sl = slice(h * HEAD_DIM, (h + 1) * HEAD_DIM)
            acc = ws[0][:, h:h + 1] * o_scr[0][h]
            for w, scr in zip(ws[1:], o_scr[1:]):
                acc = acc + w[:, h:h + 1] * scr[h]
            o_ref[:, sl] = acc
            ssq = ssq + jnp.sum(acc * acc, axis=-1, keepdims=True)
        r = lax.rsqrt(ssq * (1.0 / ATTN_W) + EPS)
        n_ref[...] = (o_ref[...] * r * g_ref[...]).astype(BF16)

    blk = pl.BlockSpec((ROW_BLOCK, ATTN_W), lambda i: (i, 0))
    outs_ = pl.pallas_call(
        body, name=name, grid=(s // ROW_BLOCK,),
        in_specs=[_res_spec(ATTN_HEADS, d) for d in DILATIONS] + [_res_spec(1, d) for d in DILATIONS]
        + [pl.BlockSpec((1, ATTN_W), lambda i: (0, 0))],
        out_specs=[blk, blk] + [_res_spec(1, d) for d in DILATIONS],
        out_shape=[jax.ShapeDtypeStruct((s, ATTN_W), F32), jax.ShapeDtypeStruct((s, D_MODEL), BF16)]
        + [_res_shape(s, 1, d, F32) for d in DILATIONS],
        scratch_shapes=[pltpu.VMEM((ATTN_HEADS, ROW_BLOCK, LANES), F32) for _ in DILATIONS]
        + [pltpu.VMEM((1, ROW_BLOCK, LANES), F32) for _ in DILATIONS],
        compiler_params=_params(("parallel",)),
    )(*outs, *lses, g)
    return outs_[0], outs_[1], list(outs_[2:])


def _attn_prebwd(dcat, o, g, name="attn_prebwd"):
    s = o.shape[0]
    nd = len(DILATIONS)

    def body(dy_ref, o_ref, g_ref, *rest):
        do_outs, delta_outs, gg_ref = rest[:nd], rest[nd:2 * nd], rest[2 * nd]
        do_scr, delta_scr = rest[2 * nd + 1], rest[2 * nd + 2]
        i = pl.program_id(0)
        dy, ov = dy_ref[...], o_ref[...]
        r = lax.rsqrt(jnp.mean(ov * ov, axis=-1, keepdims=True) + EPS)
        dyg = dy * g_ref[...]
        c = jnp.mean(dyg * ov, axis=-1, keepdims=True)
        do = r * dyg - ov * (r * r * r * c)
        prod = do * ov
        lane = lax.broadcasted_iota(jnp.int32, (ROW_BLOCK, LANES), 1)
        acc = jnp.zeros((ROW_BLOCK, LANES), F32)
        for h in range(ATTN_HEADS):
            sl = slice(h * HEAD_DIM, (h + 1) * HEAD_DIM)
            do_scr[h] = do[:, sl]
            acc = jnp.where(lane == h, jnp.sum(prod[:, sl], axis=-1, keepdims=True), acc)
        delta_scr[0] = acc
        for di, dil in enumerate(DILATIONS):
            _to_residues(do_scr, do_outs[di], dil)
            _to_residues(delta_scr, delta_outs[di], dil)

        @pl.when(i == 0)
        def _():
            gg_ref[...] = jnp.zeros_like(gg_ref)

        gg_ref[...] += jnp.sum(dy * ov * r, axis=0, keepdims=True)

    blk = pl.BlockSpec((ROW_BLOCK, ATTN_W), lambda i: (i, 0))
    vec = pl.BlockSpec((1, ATTN_W), lambda i: (0, 0))
    outs = pl.pallas_call(
        body, name=name, grid=(s // ROW_BLOCK,),
        in_specs=[blk, blk, vec],
        out_specs=[_res_spec(ATTN_HEADS, d) for d in DILATIONS] + [_res_spec(1, d) for d in DILATIONS] + [vec],
        out_shape=[_res_shape(s, ATTN_HEADS, d, BF16) for d in DILATIONS]
        + [_res_shape(s, 1, d, F32) for d in DILATIONS] + [jax.ShapeDtypeStruct((1, ATTN_W), F32)],
        scratch_shapes=[pltpu.VMEM((ATTN_HEADS, ROW_BLOCK, LANES), F32), pltpu.VMEM((1, ROW_BLOCK, LANES), F32)],
        compiler_params=_params(("arbitrary",)),
    )(dcat, o, g)
    return list(outs[:nd]), list(outs[nd:2 * nd]), outs[2 * nd]


def _attn_bwd(q, k, v, do, lse, delta, dil, name):
    length = q.shape[0]
    qb = ATTN_BLOCK
    nsteps = length // (ATTN_GROUP * qb)
    scale = HEAD_DIM ** -0.5

    def body(qp, qc, qn, kp, kc, kn, vp, vc, vn, dop, doc, don, lp, lc, ln, dp, dc, dn, dq_ref, dk_ref, dv_ref):
        masks = [_band_masks(pl.program_id(1) * ATTN_GROUP + b, length) for b in range(ATTN_GROUP)]
        everything = slice(None)
        lse_e = [_edge(lp, lc, ln, b, everything) for b in range(ATTN_GROUP)]
        del_e = [_edge(dp, dc, dn, b, everything) for b in range(ATTN_GROUP)]
        units = [(b, h, slice(h * HEAD_DIM, (h + 1) * HEAD_DIM)) for b in range(ATTN_GROUP)
                 for h in range(ATTN_HEADS)]
        prods = []
        for b, _, sl in units:
            q_c, k_c, v_c, do_c = _block(qc, b, sl), _block(kc, b, sl), _block(vc, b, sl), _block(doc, b, sl)
            q_e, k_e = _edge(qp, qc, qn, b, sl), _edge(kp, kc, kn, b, sl)
            v_e, do_e = _edge(vp, vc, vn, b, sl), _edge(dop, doc, don, b, sl)
            prods.append((_dot_nt(q_c, k_c), _dot_nt(do_c, v_c), _dot_nt(q_c, k_e), _dot_nt(do_c, v_e),
                          _dot_nt(q_e, k_c), _dot_nt(do_e, v_c)))
        parts = []
        for (b, h, _), (s_cc, dp_cc, s_ek, dp_ek, s_eq, dp_eq) in zip(units, prods):
            valid_c, valid_ek, valid_eq = masks[b]
            hc = slice(h, h + 1)
            lse_c, del_c = _block(lc, b, hc), _block(dc, b, hc)
            p_cc = jnp.where(valid_c, jnp.exp(s_cc * scale - lse_c), 0.0)
            ds_cc = (p_cc * (dp_cc - del_c)).astype(BF16)
            p_ek = jnp.where(valid_ek, jnp.exp(s_ek * scale - lse_c), 0.0)
            ds_ek = (p_ek * (dp_ek - del_c)).astype(BF16)
            p_eq = jnp.where(valid_eq, jnp.exp(s_eq * scale - lse_e[b][:, hc]), 0.0)
            ds_eq = (p_eq * (dp_eq - del_e[b][:, hc])).astype(BF16)
            parts.append((p_cc.astype(BF16), ds_cc, ds_ek, p_eq.astype(BF16), ds_eq))
        for (b, _, sl), (p_cc, ds_cc, ds_ek, p_eq, ds_eq) in zip(units, parts):
            rows = slice(b * qb, (b + 1) * qb)
            q_c, k_c, do_c = _block(qc, b, sl), _block(kc, b, sl), _block(doc, b, sl)
            q_e, k_e, do_e = _edge(qp, qc, qn, b, sl), _edge(kp, kc, kn, b, sl), _edge(dop, doc, don, b, sl)
            dq_ref[rows, sl] = ((_dot(ds_cc, k_c) + _dot(ds_ek, k_e)) * scale).astype(BF16)
            dk_ref[rows, sl] = ((_dot_tn(ds_cc, q_c) + _dot_tn(ds_eq, q_e)) * scale).astype(BF16)
            dv_ref[rows, sl] = (_dot_tn(p_cc, do_c) + _dot_tn(p_eq, do_e)).astype(BF16)

    wide, narrow = list(_window_specs(nsteps, ATTN_W)), list(_window_specs(nsteps, LANES))
    return tuple(pl.pallas_call(
        body, name=name, grid=(dil, nsteps),
        in_specs=wide * 4 + narrow * 2,
        out_specs=[wide[1]] * 3,
        out_shape=[jax.ShapeDtypeStruct((length, dil * ATTN_W), BF16)] * 3,
        compiler_params=_params(("parallel", "parallel")),
    )(q, q, q, k, k, k, v, v, v, do, do, do, lse, lse, lse, delta, delta, delta))


def _gate_matrices(gf_up, gb_up):
    pad = LANES - 2 * GLA_RANK
    uf = jnp.concatenate([gf_up, jnp.zeros((GLA_RANK + pad, GLA_KW), gf_up.dtype)], axis=0)
    ub = jnp.concatenate([jnp.zeros((GLA_RANK, GLA_KW), gb_up.dtype), gb_up, jnp.zeros((pad, GLA_KW), gb_up.dtype)], axis=0)
    return uf.astype(BF16), ub.astype(BF16)


def _log_sigmoid(x):
    return jnp.minimum(x, 0.0) - jnp.log(1.0 + jnp.exp(-jnp.abs(x)))


def _gla_gates(proj, uf, ub, gf_b, gb_b, name="gla_gates"):
    s = proj.shape[0]

    def body(z_ref, uf_ref, ub_ref, bf_ref, bb_ref, gf_ref, gb_ref):
        z = z_ref[...].astype(BF16)
        gf_ref[...] = _log_sigmoid(_dot(z, uf_ref[...]) + bf_ref[...]) * (1.0 / GLA_GATE_NORM)
        gb_ref[...] = _log_sigmoid(_dot(z, ub_ref[...]) + bb_ref[...]) * (1.0 / GLA_GATE_NORM)

    mat = pl.BlockSpec((LANES, GLA_KW), lambda i: (0, 0))
    vec = pl.BlockSpec((1, GLA_KW), lambda i: (0, 0))
    out = pl.BlockSpec((ROW_BLOCK, GLA_KW), lambda i: (i, 0))
    return pl.pallas_call(
        body, name=name, grid=(s // ROW_BLOCK,),
        in_specs=[pl.BlockSpec((ROW_BLOCK, LANES), lambda i: (i, OFF_Z // LANES)), mat, mat, vec, vec],
        out_specs=[out, out],
        out_shape=[jax.ShapeDtypeStruct((s, GLA_KW), F32)] * 2,
        compiler_params=_params(("parallel",)),
    )(proj, uf, ub, gf_b, gb_b)


def _gla_gates_bwd(dgf, dgb, proj, uf, ub, gf_b, gb_b, dproj, name="gla_gates_bwd"):
    s = proj.shape[0]
    tail = IN_PAD - OFF_Z

    def body(dgf_ref, dgb_ref, z_ref, uf_ref, ub_ref, bf_ref, bb_ref, _, dz_ref, guf_ref, gub_ref, gbf_ref, gbb_ref):
        i = pl.program_id(0)
        z = z_ref[...].astype(BF16)
        uf_, ub_ = uf_ref[...], ub_ref[...]
        dpf = dgf_ref[...] * (1.0 / GLA_GATE_NORM) * _sigmoid(-(_dot(z, uf_) + bf_ref[...]))
        dpb = dgb_ref[...] * (1.0 / GLA_GATE_NORM) * _sigmoid(-(_dot(z, ub_) + bb_ref[...]))
        dpf_b, dpb_b = dpf.astype(BF16), dpb.astype(BF16)
        dz_ref[:, 0:LANES] = (_dot_nt(dpf_b, uf_) + _dot_nt(dpb_b, ub_)).astype(BF16)
        dz_ref[:, LANES:tail] = jnp.zeros((ROW_BLOCK, tail - LANES), BF16)

        @pl.when(i == 0)
        def _():
            for r in (guf_ref, gub_ref, gbf_ref, gbb_ref):
                r[...] = jnp.zeros_like(r)

        guf_ref[...] += _dot_tn(z, dpf_b)
        gub_ref[...] += _dot_tn(z, dpb_b)
        gbf_ref[...] += jnp.sum(dpf, axis=0, keepdims=True)
        gbb_ref[...] += jnp.sum(dpb, axis=0, keepdims=True)

    mat = pl.BlockSpec((LANES, GLA_KW), lambda i: (0, 0))
    vec = pl.BlockSpec((1, GLA_KW), lambda i: (0, 0))
    blk = pl.BlockSpec((ROW_BLOCK, GLA_KW), lambda i: (i, 0))
    return pl.pallas_call(
        body, name=name, grid=(s // ROW_BLOCK,),
        in_specs=[blk, blk, pl.BlockSpec((ROW_BLOCK, LANES), lambda i: (i, OFF_Z // LANES)), mat, mat, vec, vec,
                  pl.BlockSpec(memory_space=pl.ANY)],
        out_specs=[pl.BlockSpec((ROW_BLOCK, tail), lambda i: (i, OFF_Z // tail)), mat, mat, vec, vec],
        out_shape=[jax.ShapeDtypeStruct(dproj.shape, dproj.dtype), jax.ShapeDtypeStruct((LANES, GLA_KW), F32),
                   jax.ShapeDtypeStruct((LANES, GLA_KW), F32), jax.ShapeDtypeStruct((1, GLA_KW), F32),
                   jax.ShapeDtypeStruct((1, GLA_KW), F32)],
        input_output_aliases={7: 0},
        compiler_params=_params(("arbitrary",)),
    )(dgf, dgb, proj, uf, ub, gf_b, gb_b, dproj)


def _split3(x):
    x1 = x.astype(BF16)
    r1 = x - x1.astype(F32)
    x2 = r1.astype(BF16)
    x3 = (r1 - x2.astype(F32)).astype(BF16)
    return x1, x2, x3


def _dot_exact(mask_bf, x):
    x1, x2, x3 = _split3(x)
    return _dot(mask_bf, x1) + _dot(mask_bf, x2) + _dot(mask_bf, x3)


def _chunk_masks(reverse):
    c = GLA_CHUNK
    row = lax.broadcasted_iota(jnp.int32, (c, c), 0)
    col = lax.broadcasted_iota(jnp.int32, (c, c), 1)
    allowed = (col >= row) if reverse else (col <= row)
    seen_by = (col <= row) if reverse else (col >= row)
    return allowed, seen_by


def _chunk_terms(q_ref, k_ref, g_ref, rs, hs, allowed, reverse):
    c = GLA_CHUNK
    mid, last = (c // 2, 0) if reverse else (c // 2 - 1, c - 1)
    q = q_ref[rs, hs] * (GLA_DK ** -0.5)
    k = k_ref[rs, hs]
    b = _dot_exact(jnp.where(allowed, 1.0, 0.0).astype(BF16), g_ref[rs, hs])
    bref, blast = b[mid:mid + 1, :], b[last:last + 1, :]
    e_q, e_k, e_in, e_st = jnp.exp(b - bref), jnp.exp(bref - b), jnp.exp(b), jnp.exp(blast - b)
    return dict(last=last, e_q=e_q, e_k=e_k, e_in=e_in, e_st=e_st,
                dec=jnp.exp(blast), qe=q * e_q, ke=k * e_k, qin=q * e_in, kst=k * e_st)


def _gla_blockspecs(s, reverse_order):
    cb = GLA_CHUNKS_PER_STEP
    rows = cb * GLA_CHUNK
    nsteps = s // rows

    def rb(n):
        return (nsteps - 1 - n) if reverse_order else n

    qspec = pl.BlockSpec((rows, GLA_KW), lambda n: (rb(n), OFF_GQ // GLA_KW))
    kspec = pl.BlockSpec((rows, GLA_KW), lambda n: (rb(n), OFF_GK // GLA_KW))
    vspec = pl.BlockSpec((rows, GLA_VW), lambda n: (rb(n), OFF_GV // GLA_VW))
    gspec = pl.BlockSpec((rows, GLA_KW), lambda n: (rb(n), 0))
    ospec = pl.BlockSpec((rows, GLA_VW), lambda n: (rb(n), 0))
    sspec = pl.BlockSpec((GLA_HEADS, cb, GLA_DV, GLA_DK), lambda n: (0, rb(n), 0, 0))
    return cb, rows, nsteps, qspec, kspec, vspec, gspec, ospec, sspec


def _gla_units(cb, order_reversed):
    chunks = list(reversed(range(cb))) if order_reversed else list(range(cb))
    return [(c, h, slice(c * GLA_CHUNK, (c + 1) * GLA_CHUNK), slice(h * GLA_DK, (h + 1) * GLA_DK),
             slice(h * GLA_DV, (h + 1) * GLA_DV)) for c in chunks for h in range(GLA_HEADS)]


def _gla_fwd(proj, g, reverse, name):
    s = proj.shape[0]
    cb, rows, nsteps, qspec, kspec, vspec, gspec, ospec, sspec = _gla_blockspecs(s, reverse)

    def body(q_ref, k_ref, v_ref, g_ref, o_ref, st_ref, state):
        @pl.when(pl.program_id(0) == 0)
        def _():
            state[...] = jnp.zeros_like(state)

        allowed, _ = _chunk_masks(reverse)
        units = _gla_units(cb, reverse)
        terms = [_chunk_terms(q_ref, k_ref, g_ref, rs, hs, allowed, reverse) for _, _, rs, hs, _ in units]
        vals = [v_ref[rs, vs].astype(BF16) for _, _, rs, _, vs in units]
        raw = [(_dot_nt(t["qe"].astype(BF16), t["ke"].astype(BF16)), _dot_tn(v, t["kst"].astype(BF16)))
               for t, v in zip(terms, vals)]
        intra = [_dot(jnp.where(allowed, a, 0.0).astype(BF16), v) for (a, _), v in zip(raw, vals)]
        st = [state[h] for h in range(GLA_HEADS)]
        for (c, h, rs, _, vs), t, (_, kv), o_in in zip(units, terms, raw, intra):
            st_ref[h, c] = st[h]
            o_ref[rs, vs] = o_in + _dot_nt(t["qin"].astype(BF16), st[h].astype(BF16))
            st[h] = st[h] * t["dec"] + kv
        for h in range(GLA_HEADS):
            state[h] = st[h]

    return pl.pallas_call(
        body, name=name, grid=(nsteps,),
        in_specs=[qspec, kspec, vspec, gspec],
        out_specs=[ospec, sspec],
        out_shape=[jax.ShapeDtypeStruct((s, GLA_VW), F32),
                   jax.ShapeDtypeStruct((GLA_HEADS, s // GLA_CHUNK, GLA_DV, GLA_DK), F32)],
        scratch_shapes=[pltpu.VMEM((GLA_HEADS, GLA_DV, GLA_DK), F32)],
        compiler_params=_params(("arbitrary",)),
    )(proj, proj, proj, g)


def _gla_bwd(proj, g, do, states, reverse, name, merge=None):
    s = proj.shape[0]
    cb, rows, nsteps, qspec, kspec, vspec, gspec, ospec, sspec = _gla_blockspecs(s, not reverse)
    gla_cols = OFF_Z - OFF_GQ

    def body(q_ref, k_ref, v_ref, g_ref, do_ref, sp_ref, *rest):
        if merge is None:
            dq_ref, dk_ref, dv_ref, dg_ref, dstate = rest
        else:
            dq_o, dk_o, dv_o, dgr_ref, _, dp_ref, dg_ref, dstate = rest
        @pl.when(pl.program_id(0) == 0)
        def _():
            dstate[...] = jnp.zeros_like(dstate)

        allowed, seen_by = _chunk_masks(reverse)
        units = _gla_units(cb, not reverse)
        terms = [_chunk_terms(q_ref, k_ref, g_ref, rs, hs, allowed, reverse) for _, _, rs, hs, _ in units]
        vals = [v_ref[rs, vs].astype(BF16) for _, _, rs, _, vs in units]
        dos = [do_ref[rs, vs] for _, _, rs, _, vs in units]
        prevs = [sp_ref[h, c] for c, h, _, _, _ in units]
        raw = [(_dot_nt(t["qe"].astype(BF16), t["ke"].astype(BF16)), _dot_nt(do, v),
                _dot(do, sp.astype(BF16)), _dot_tn(do, t["qin"].astype(BF16)))
               for t, v, do, sp in zip(terms, vals, dos, prevs)]
        inner = []
        for t, do, (a, da, _, _) in zip(terms, dos, raw):
            da = jnp.where(allowed, da, 0.0).astype(BF16)
            inner.append((_dot(da, t["ke"].astype(BF16)), _dot_tn(da, t["qe"].astype(BF16)),
                          _dot_tn(jnp.where(allowed, a, 0.0).astype(BF16), do)))
        ds = [dstate[h] for h in range(GLA_HEADS)]
        outer = []
        for (c, h, _, _, _), t, v, sp, (_, _, _, inc) in zip(units, terms, vals, prevs, raw):
            ds_b = ds[h].astype(BF16)
            outer.append((_dot(v, ds_b), _dot_nt(t["kst"].astype(BF16), ds_b),
                          jnp.sum(sp * ds[h], axis=0, keepdims=True)))
            ds[h] = ds[h] * t["dec"] + inc
        for h in range(GLA_HEADS):
            dstate[h] = ds[h]
        seen_bf = jnp.where(seen_by, 1.0, 0.0).astype(BF16)
        rowi = lax.broadcasted_iota(jnp.int32, (GLA_CHUNK, GLA_DK), 0)
        for (c, h, rs, hs, vs), t, (_, _, dqin, _), (dqe, dke, dv_in), (dkst, dv_out, ddec) in zip(
                units, terms, raw, inner, outer):
            dq = (dqe * t["e_q"] + dqin * t["e_in"]) * (GLA_DK ** -0.5)
            dk = dke * t["e_k"] + dkst * t["e_st"]
            if merge is None:
                dq_ref[rs, hs], dk_ref[rs, hs], dv_ref[rs, vs] = dq, dk, dv_in + dv_out
            else:
                lo = OFF_GK - OFF_GQ + h * GLA_DK
                dp_ref[rs, hs] = (dq + dq_o[rs, hs]).astype(BF16)
                dp_ref[rs, lo:lo + GLA_DK] = (dk + dk_o[rs, hs]).astype(BF16)
                lo = OFF_GV - OFF_GQ + h * GLA_DV
                dp_ref[rs, lo:lo + GLA_DV] = (dv_in + dv_out + dv_o[rs, vs]).astype(BF16)
            kk = dkst * t["kst"]
            db = dqe * t["qe"] - dke * t["ke"] + dqin * t["qin"] - kk
            extra = jnp.sum(kk, axis=0, keepdims=True) + ddec * t["dec"]
            db = db + jnp.where(rowi == t["last"], extra, 0.0)
            dg_ref[rs, hs] = _dot_exact(seen_bf, db)
        if merge is not None:
            dp_ref[:, OFF_GR - OFF_GQ:gla_cols] = dgr_ref[...]

    scratch = [pltpu.VMEM((GLA_HEADS, GLA_DV, GLA_DK), F32)]
    if merge is None:
        return pl.pallas_call(
            body, name=name, grid=(nsteps,),
            in_specs=[qspec, kspec, vspec, gspec, ospec, sspec],
            out_specs=[gspec, gspec, ospec, gspec],
            out_shape=[jax.ShapeDtypeStruct((s, GLA_KW), F32), jax.ShapeDtypeStruct((s, GLA_KW), F32),
                       jax.ShapeDtypeStruct((s, GLA_VW), F32), jax.ShapeDtypeStruct((s, GLA_KW), F32)],
            scratch_shapes=scratch,
            compiler_params=_params(("arbitrary",)),
        )(proj, proj, proj, g, do, states)
    dproj = merge[4]
    block = gspec.index_map
    return pl.pallas_call(
        body, name=name, grid=(nsteps,),
        in_specs=[qspec, kspec, vspec, gspec, ospec, sspec, gspec, gspec, ospec, ospec, _ANY],
        out_specs=[pl.BlockSpec((rows, gla_cols), lambda n: (block(n)[0], OFF_GQ // gla_cols)), gspec],
        out_shape=[jax.ShapeDtypeStruct(dproj.shape, dproj.dtype), jax.ShapeDtypeStruct((s, GLA_KW), F32)],
        input_output_aliases={10: 0},
        scratch_shapes=scratch,
        compiler_params=_params(("arbitrary",)),
    )(proj, proj, proj, g, do, states, *merge)


def _gla_post(o_f, o_b, proj, g, cat, name="gla_post"):
    s = o_f.shape[0]

    def body(of_ref, ob_ref, gr_ref, g_ref, _, o_ref):
        gv = g_ref[...]
        for h in range(GLA_HEADS):
            sl = slice(h * GLA_DV, (h + 1) * GLA_DV)
            osum = of_ref[:, sl] + ob_ref[:, sl]
            r = lax.rsqrt(jnp.mean(osum * osum, axis=-1, keepdims=True) + EPS)
            gr = gr_ref[:, sl]
            o_ref[:, sl] = (osum * r * gv * (gr * _sigmoid(gr))).astype(BF16)

    blk = pl.BlockSpec((ROW_BLOCK, GLA_VW), lambda i: (i, 0))
    return pl.pallas_call(
        body, name=name, grid=(s // ROW_BLOCK,),
        in_specs=[blk, blk, pl.BlockSpec((ROW_BLOCK, GLA_VW), lambda i: (i, OFF_GR // GLA_VW)),
                  pl.BlockSpec((1, GLA_DV), lambda i: (0, 0)), pl.BlockSpec(memory_space=pl.ANY)],
        out_specs=pl.BlockSpec((ROW_BLOCK, GLA_VW), lambda i: (i, ATTN_W // GLA_VW)),
        out_shape=jax.ShapeDtypeStruct(cat.shape, cat.dtype),
        input_output_aliases={4: 0},
        compiler_params=_params(("parallel",)),
    )(o_f, o_b, proj, g, cat)


def _gla_post_bwd(dcat, o_f, o_b, proj, g, name="gla_post_bwd"):
    s = o_f.shape[0]

    def body(dy_ref, of_ref, ob_ref, gr_ref, g_ref, do_ref, dgr_ref, gg_ref):
        i = pl.program_id(0)
        gv = g_ref[...]
        gg = jnp.zeros((1, GLA_DV), F32)
        for h in range(GLA_HEADS):
            sl = slice(h * GLA_DV, (h + 1) * GLA_DV)
            osum = of_ref[:, sl] + ob_ref[:, sl]
            r = lax.rsqrt(jnp.mean(osum * osum, axis=-1, keepdims=True) + EPS)
            gr, dy = gr_ref[:, sl], dy_ref[:, sl]
            sg = _sigmoid(gr)
            dgr_ref[:, sl] = (dy * (osum * r * gv) * (sg * (1.0 + gr * (1.0 - sg)))).astype(BF16)
            dn = dy * (gr * sg)
            dng = dn * gv
            c = jnp.mean(dng * osum, axis=-1, keepdims=True)
            do_ref[:, sl] = (r * dng - osum * (r * r * r * c)).astype(BF16)
            gg = gg + jnp.sum(dn * osum * r, axis=0, keepdims=True)

        @pl.when(i == 0)
        def _():
            gg_ref[...] = jnp.zeros_like(gg_ref)

        gg_ref[...] += gg

    blk = pl.BlockSpec((ROW_BLOCK, GLA_VW), lambda i: (i, 0))
    vec = pl.BlockSpec((1, GLA_DV), lambda i: (0, 0))
    return pl.pallas_call(
        body, name=name, grid=(s // ROW_BLOCK,),
        in_specs=[pl.BlockSpec((ROW_BLOCK, GLA_VW), lambda i: (i, 1)), blk, blk,
                  pl.BlockSpec((ROW_BLOCK, GLA_VW), lambda i: (i, OFF_GR // GLA_VW)), vec],
        out_specs=[blk, blk, vec],
        out_shape=[jax.ShapeDtypeStruct((s, GLA_VW), BF16), jax.ShapeDtypeStruct((s, GLA_VW), BF16),
                   jax.ShapeDtypeStruct((1, GLA_DV), F32)],
        compiler_params=_params(("arbitrary",)),
    )(dcat, o_f, o_b, proj, g)


HALO = 16


def _extended(prev_ref, cur_ref, next_ref, i, s, tr, cs):
    first, last = i == 0, i == s // tr - 1
    prev = jnp.where(first, 0.0, prev_ref[:, cs].astype(F32))
    nxt = jnp.where(last, 0.0, next_ref[:, cs].astype(F32))
    return jnp.concatenate([prev, cur_ref[:, cs].astype(F32), nxt], axis=0)


FFN_ROWS = 512
FFN_COLS = 512


FFN_CHUNK = 256


def _lagged(i, ni, multiply, finish, rotate, init):
    chunks = [slice(c, c + FFN_CHUNK) for c in range(0, FFN_COLS, FFN_CHUNK)]

    @pl.when(i == 0)
    def _():
        init()

    @pl.when(i < 2)
    def _():
        rotate([multiply(cs) for cs in chunks], chunks)

    @pl.when((i >= 2) & (i < ni))
    def _():
        new = []
        for cs in chunks:
            new.append(multiply(cs))
            finish(cs)
        rotate(new, chunks)

    @pl.when(i >= ni)
    def _():
        for cs in chunks:
            finish(cs)
        rotate(None, chunks)


def _ffn_in(n2, w_gate, w_up, conv_w, conv_b, name="ffn_in"):
    s, d = n2.shape
    f = w_gate.shape[1]
    tm, tn, edge = FFN_ROWS, FFN_COLS, SUBLANES
    ni = s // tm
    ext = tm + 2 * edge

    def body(a_ref, wg_ref, wu_ref, w_ref, b_ref, gate_ref, silu_ref, slope_ref, act_ref, g_tile, u_tile, g_tail):
        i = pl.program_id(1)

        @pl.when(i == 0)
        def _():
            g_tile[...] = jnp.zeros_like(g_tile)
            u_tile[...] = jnp.zeros_like(u_tile)
            g_tail[...] = jnp.zeros_like(g_tail)

        a = a_ref[...]
        g_new = _dot(a, wg_ref[...])
        u_new = _dot(a, wu_ref[...])
        g_old, u_old = g_tile[...], u_tile[...]
        before = jnp.where(i == 1, 0.0, g_tail[...])
        after = jnp.where(i == ni, 0.0, g_new[0:edge])
        ge = jnp.concatenate([before, g_old, after], axis=0)
        w = w_ref[...]
        conv = (w[0:1] * pltpu.roll(ge, 1, 0) + w[1:2] * ge + w[2:3] * pltpu.roll(ge, ext - 1, 0))[edge:edge + tm]
        conv = conv + b_ref[...]
        sg = _sigmoid(conv)
        silu = conv * sg
        u_f = u_old.astype(F32)
        gate_ref[...] = g_old
        silu_ref[...] = silu.astype(BF16)
        slope_ref[...] = (u_f * (sg * (1.0 + conv * (1.0 - sg)))).astype(BF16)
        act_ref[...] = (silu * u_f).astype(BF16)
        g_tail[...] = g_old[tm - edge:tm]
        g_tile[...] = g_new
        u_tile[...] = u_new.astype(BF16)

    lag = pl.BlockSpec((tm, tn), lambda j, i: (jnp.maximum(i - 1, 0), j))
    return pl.pallas_call(
        body, name=name, grid=(f // tn, ni + 1),
        in_specs=[pl.BlockSpec((tm, d), lambda j, i: (jnp.minimum(i, ni - 1), 0)),
                  pl.BlockSpec((d, tn), lambda j, i: (0, j)), pl.BlockSpec((d, tn), lambda j, i: (0, j)),
                  pl.BlockSpec((3, tn), lambda j, i: (0, j)), pl.BlockSpec((1, tn), lambda j, i: (0, j))],
        out_specs=[lag, lag, lag, lag],
        out_shape=[jax.ShapeDtypeStruct((s, f), F32)] + [jax.ShapeDtypeStruct((s, f), BF16)] * 3,
        scratch_shapes=[pltpu.VMEM((tm, tn), F32), pltpu.VMEM((tm, tn), BF16), pltpu.VMEM((edge, tn), F32)],
        compiler_params=_params(("parallel", "arbitrary")),
    )(n2, w_gate, w_up, conv_w, conv_b)


def _ffn_mid_bwd(dh2, w_down, gate, silu, slope, conv_w, name="ffn_mid_bwd"):
    s, d = dh2.shape
    f = gate.shape[1]
    tm, tn = FFN_ROWS, FFN_COLS
    ni = s // tm
    ext = tm + 2 * HALO
    per, last_halo = tm // HALO, s // HALO - 1

    def body(a_ref, wd_ref, gp, gc, gn, sp, sc, sn, silu_ref, w_ref, dg_ref, du_ref, gw_ref, gb_ref,
             d_near, d_far, d_tail):
        i = pl.program_id(1)

        def multiply(cs):
            return _dot_nt(a_ref[...], wd_ref[cs, :])

        def finish(cs):
            before = jnp.where(i == 2, 0.0, d_tail[:, cs])
            after = jnp.where(i == ni + 1, 0.0, d_near[0:HALO, cs])
            d_mid = d_far[:, cs]
            de = jnp.concatenate([before, d_mid, after], axis=0)
            ge = _extended(gp, gc, gn, i - 2, s, tm, cs)
            w = w_ref[:, cs]
            g_prev, g_next = pltpu.roll(ge, 1, 0), pltpu.roll(ge, ext - 1, 0)
            inner = slice(HALO, HALO + tm)
            du_ref[:, cs] = (d_mid * silu_ref[:, cs].astype(F32)).astype(BF16)
            dconv = de * _extended(sp, sc, sn, i - 2, s, tm, cs)
            dgate = w[0:1] * pltpu.roll(dconv, ext - 1, 0) + w[1:2] * dconv + w[2:3] * pltpu.roll(dconv, 1, 0)
            dg_ref[:, cs] = dgate[inner].astype(BF16)
            dci = dconv[inner]
            gw_ref[0:1, cs] += jnp.sum(dci * g_prev[inner], axis=0, keepdims=True)
            gw_ref[1:2, cs] += jnp.sum(dci * ge[inner], axis=0, keepdims=True)
            gw_ref[2:3, cs] += jnp.sum(dci * g_next[inner], axis=0, keepdims=True)
            gb_ref[:, cs] += jnp.sum(dci, axis=0, keepdims=True)

        def rotate(new, chunks):
            d_tail[...] = d_far[tm - HALO:tm]
            d_far[...] = d_near[...]
            if new is not None:
                for cs, d_new in zip(chunks, new):
                    d_near[:, cs] = d_new

        def init():
            for r in (d_near, d_far, d_tail, gw_ref, gb_ref):
                r[...] = jnp.zeros_like(r)

        _lagged(i, ni, multiply, finish, rotate, init)

    def tile(i):
        return jnp.maximum(i - 2, 0)

    cur = pl.BlockSpec((tm, tn), lambda j, i: (tile(i), j))
    prev = pl.BlockSpec((HALO, tn), lambda j, i: (jnp.maximum(tile(i) * per - 1, 0), j))
    nxt = pl.BlockSpec((HALO, tn), lambda j, i: (jnp.minimum((tile(i) + 1) * per, last_halo), j))
    wspec = pl.BlockSpec((3, tn), lambda j, i: (0, j))
    bspec = pl.BlockSpec((1, tn), lambda j, i: (0, j))
    return pl.pallas_call(
        body, name=name, grid=(f // tn, ni + 2),
        in_specs=[pl.BlockSpec((tm, d), lambda j, i: (jnp.minimum(i, ni - 1), 0)),
                  pl.BlockSpec((tn, d), lambda j, i: (j, 0))] + [prev, cur, nxt] * 2 + [cur, wspec],
        out_specs=[cur, cur, wspec, bspec],
        out_shape=[jax.ShapeDtypeStruct((s, f), BF16), jax.ShapeDtypeStruct((s, f), BF16),
                   jax.ShapeDtypeStruct((3, f), F32), jax.ShapeDtypeStruct((1, f), F32)],
        scratch_shapes=[pltpu.VMEM((tm, tn), F32), pltpu.VMEM((tm, tn), F32), pltpu.VMEM((HALO, tn), F32)],
        compiler_params=_params(("parallel", "arbitrary")),
    )(dh2, w_down, gate, gate, gate, slope, slope, slope, silu, conv_w)


def _local_step(x, target, w, late_weights=None, grad_sink=None, first_dep=()):
    s = x.shape[0]
    tables = _rope_tables(s)
    uf, ub = _gate_matrices(w["gf_up"], w["gb_up"])
    if grad_sink is None:
        grad_sink = lambda names, grads: ()

    n1 = _rms_fwd(x, w["norm1_g"], "norm1")
    proj = _matmul([(n1, w["w_in"])], "nn", F32, 1024, 1280, D_MODEL, "in_proj", deps=first_dep)
    qkv = _rope_fwd(proj, tables)
    branches = [_attn_fwd(*qkv[di], d, f"attn_fwd_d{d}") for di, d in enumerate(DILATIONS)]
    o_mix, ao, lse = _attn_combine([b[0] for b in branches], [b[1] for b in branches], w["attn_norm_g"])
    g_f, g_b = _gla_gates(proj, uf, ub, w["gf_b"], w["gb_b"])
    o_f, st_f = _gla_fwd(proj, g_f, False, "gla_fwd_f")
    o_b, st_b = _gla_fwd(proj, g_b, True, "gla_fwd_b")
    cat = _gla_post(o_f, o_b, proj, w["gla_norm_g"], ao)
    if late_weights is not None:
        w = {**w, **late_weights("mixer", cat)}
    h1 = _matmul([(cat, w["w_out"])], "nn", F32, 512, 1024, D_MODEL, "out_proj", res=x)
    n2 = _rms_fwd(h1, w["norm2_g"], "norm2")
    if late_weights is not None:
        w = {**w, **late_weights("ffn", n2)}
    gate, silu, slope, act = _ffn_in(n2, w["w_gate"], w["w_up"], w["conv_w"], w["conv_b"])
    h2 = _matmul([(act, w["w_down"])], "nn", F32, 1024, 1024, 2816, "ffn_down", res=h1)
    dh2, dh2_b, loss_acc, g_final = _final_loss(h2, target, w["final_norm_g"])

    g_w_down = _matmul([(act, dh2_b)], "tn", BF16, 1408, 1024, 2048, "g_w_down")
    dep = grad_sink(["w_down"], [g_w_down])
    dgate, dup, g_conv_w, g_conv_b = _ffn_mid_bwd(dh2_b, w["w_down"], gate, silu, slope, w["conv_w"])
    g_w_gate = _matmul([(n2, dgate)], "tn", BF16, 2048, 512, 2048, "g_w_gate", deps=dep)
    g_w_up = _matmul([(n2, dup)], "tn", BF16, 2048, 512, 2048, "g_w_up")
    dep = grad_sink(["w_gate", "w_up"], [g_w_gate, g_w_up])
    dn2 = _matmul([(dgate, w["w_gate"])], "nt", F32, 1024, 1024, 2816, "d_n2_gate", deps=dep)
    dn2 = _matmul([(dup, w["w_up"])], "nt", F32, 1024, 1024, 2816, "d_n2_up", res=dn2)
    dh1, dh1_b, g_norm2 = _rms_bwd(dn2, h1, w["norm2_g"], dh2, "norm2_bwd")

    g_w_out = _matmul([(cat, dh1_b)], "tn", BF16, 1024, 1024, 2048, "g_w_out")
    dep = grad_sink(["w_out"], [g_w_out])
    dcat = _matmul([(dh1_b, w["w_out"])], "nt", F32, 512, 1024, D_MODEL, "d_cat", deps=dep)
    do_attn, delta, g_attn_norm = _attn_prebwd(dcat, o_mix, w["attn_norm_g"])
    grads = [_attn_bwd(*qkv[di], do_attn[di], lse[di], delta[di], d, f"attn_bwd_d{d}")
             for di, d in enumerate(DILATIONS)]
    dproj = _rope_bwd(grads, tables)
    do_gla, dgr, g_gla_norm = _gla_post_bwd(dcat, o_f, o_b, proj, w["gla_norm_g"])
    dq_f, dk_f, dv_f, dg_f = _gla_bwd(proj, g_f, do_gla, st_f, False, "gla_bwd_f")
    dproj, dg_b = _gla_bwd(proj, g_b, do_gla, st_b, True, "gla_bwd_b", merge=(dq_f, dk_f, dv_f, dgr, dproj))
    dproj, g_uf, g_ub, g_gf_b, g_gb_b = _gla_gates_bwd(dg_f, dg_b, proj, uf, ub, w["gf_b"], w["gb_b"], dproj)
    g_w_in = _matmul([(n1, dproj)], "tn", BF16, 1024, 1280, 2048, "g_w_in")
    dep = grad_sink(["w_in"], [g_w_in])
    dn1 = _matmul([(dproj, w["w_in"])], "nt", F32, 1024, 2048, 1280, "d_n1", deps=dep)
    grad_x, g_norm1 = _rms_bwd(dn1, x, w["norm1_g"], dh1, "norm1_bwd", bf16_copy=False)

    g = dict(norm1_g=g_norm1, w_in=g_w_in, gf_up=g_uf[:GLA_RANK], gf_b=g_gf_b,
             gb_up=g_ub[GLA_RANK:2 * GLA_RANK], gb_b=g_gb_b, gla_norm_g=g_gla_norm, attn_norm_g=g_attn_norm,
             w_out=g_w_out, norm2_g=g_norm2, w_gate=g_w_gate, w_up=g_w_up, conv_w=g_conv_w, conv_b=g_conv_b,
             w_down=g_w_down, final_norm_g=g_final)
    return loss_acc, grad_x, g


def _me_and_peers():
    x, y, c = lax.axis_index("x"), lax.axis_index("y"), lax.axis_index("c")
    me = 4 * x + 2 * y + c
    peers = []
    for kbits in range(1, N_DEV):
        px, py, pc = x ^ (kbits >> 2 & 1), y ^ (kbits >> 1 & 1), c ^ (kbits & 1)
        peers.append(((px, py, pc), 4 * px + 2 * py + pc))
    return me, peers


_HBM = pl.BlockSpec(memory_space=pltpu.HBM)
_SEM = pl.BlockSpec(memory_space=pltpu.SEMAPHORE)
_ANY = pl.BlockSpec(memory_space=pl.ANY)
_EFFECT = pltpu.SideEffectType.DATAFLOW_SIDE_EFFECTING


def _exchange_copies(src_refs, land_refs, send_sems, recv_sems, scatter):
    me, peers = _me_and_peers()
    out = []
    for a, (src, land) in enumerate(zip(src_refs, land_refs)):
        for kk, (dev, idx) in enumerate(peers):
            out.append(pltpu.make_async_remote_copy(
                src_ref=src.at[idx] if scatter else src, dst_ref=land.at[me],
                send_sem=send_sems.at[a * (N_DEV - 1) + kk], recv_sem=recv_sems.at[a * (N_DEV - 1) + kk],
                device_id=dev, device_id_type=MESH_ID))
    return out


def _exchange_start(srcs, lands, scatter, name, deps=()):
    n, nd = len(srcs), len(deps)

    def body(*refs):
        src_refs, land_refs = refs[:n], refs[n:2 * n]
        send_sems, recv_sems = refs[2 * n + nd:2 * n + nd + 2]
        token = refs[-1]
        for cp in _exchange_copies(src_refs, land_refs, send_sems, recv_sems, scatter):
            cp.start()
        token[...] = jnp.zeros_like(token)

    outs = pl.pallas_call(
        body, name=name,
        in_specs=[_HBM] * (2 * n) + [_ANY] * nd,
        out_specs=[_SEM, _SEM] + [_HBM] * (2 * n) + [pl.BlockSpec(memory_space=pltpu.VMEM)],
        out_shape=[pltpu.SemaphoreType.DMA((n * (N_DEV - 1),)), pltpu.SemaphoreType.DMA((n * (N_DEV - 1),))]
        + [pltpu.HBM(t.shape, t.dtype) for t in srcs] + [pltpu.HBM(t.shape, t.dtype) for t in lands]
        + [jax.ShapeDtypeStruct((SUBLANES, LANES), F32)],
        input_output_aliases={i: 2 + i for i in range(2 * n)},
        compiler_params=pltpu.CompilerParams(has_side_effects=_EFFECT),
    )(*[pltpu.with_memory_space_constraint(t, pltpu.HBM) for t in list(srcs) + list(lands)], *deps)
    send_sems, recv_sems = outs[0], outs[1]
    return dict(send=send_sems, recv=recv_sems, srcs=outs[2:2 + n], lands=outs[2 + n:2 + 2 * n],
                scatter=scatter, token=outs[-1])


def _exchange_wait(started, name, after):
    n = len(started["srcs"])
    scatter = started["scatter"]

    def body(*refs):
        src_refs, land_refs = refs[:n], refs[n:2 * n]
        send_sems, recv_sems = refs[2 * n], refs[2 * n + 1]
        for cp in _exchange_copies(src_refs, land_refs, send_sems, recv_sems, scatter):
            cp.wait_send()
            cp.wait_recv()

    outs = pl.pallas_call(
        body, name=name,
        in_specs=[_HBM] * (2 * n) + [_SEM, _SEM, _ANY],
        out_specs=[_HBM] * (2 * n),
        out_shape=[pltpu.HBM(t.shape, t.dtype) for t in started["srcs"]]
        + [pltpu.HBM(t.shape, t.dtype) for t in started["lands"]],
        input_output_aliases={i: i for i in range(2 * n)},
        compiler_params=pltpu.CompilerParams(has_side_effects=_EFFECT),
    )(*started["srcs"], *started["lands"], started["send"], started["recv"], after)
    return outs[:n], outs[n:]


def _all_gather_two_level(shard, name):
    def body(x_ref, out_ref, send_sems, recv_sems, local_sem):
        x, y, c = lax.axis_index("x"), lax.axis_index("y"), lax.axis_index("c")
        me, sibling = (x, y, c), (x, y, 1 - c)
        chips = [(1 - x, y), (x, 1 - y), (1 - x, 1 - y)]

        def slot(px, py, pc):
            return out_ref.at[4 * px + 2 * py + pc]

        def copy(k, block, to, src=None):
            return pltpu.make_async_remote_copy(
                src_ref=slot(*block) if src is None else src, dst_ref=slot(*block),
                send_sem=send_sems.at[k], recv_sem=recv_sems.at[k], device_id=to, device_id_type=MESH_ID)

        mine = pltpu.make_async_copy(x_ref, slot(*me), local_sem)
        mine.start()
        first = [copy(0, me, sibling, src=x_ref)]
        first += [copy(1 + j, me, (*chip, c), src=x_ref) for j, chip in enumerate(chips)]
        for cp in first:
            cp.start()
        passed = [copy(4 + j, (*chip, c), sibling) for j, chip in enumerate(chips)]
        for j, chip in enumerate(chips):
            copy(1 + j, (*chip, c), me).wait_recv()
            passed[j].start()
        copy(0, sibling, me).wait_recv()
        for j, chip in enumerate(chips):
            copy(4 + j, (*chip, 1 - c), me).wait_recv()
        for cp in first + passed:
            cp.wait_send()
        mine.wait()

    return pl.pallas_call(
        body, name=name,
        in_specs=[_ANY], out_specs=_ANY,
        out_shape=jax.ShapeDtypeStruct((N_DEV,) + shard.shape, shard.dtype),
        scratch_shapes=[pltpu.SemaphoreType.DMA((N_DEV - 1,)), pltpu.SemaphoreType.DMA((N_DEV - 1,)),
                        pltpu.SemaphoreType.DMA],
    )(shard)


def _all_gather_vmem(vec, name):
    r = vec.shape[0]

    def body(v_ref, o_ref, send_sems, recv_sems):
        me, peers = _me_and_peers()
        o_ref[me] = v_ref[...]
        sends = []
        for kk, (dev, _) in enumerate(peers):
            cp = pltpu.make_async_remote_copy(
                src_ref=v_ref, dst_ref=o_ref.at[me],
                send_sem=send_sems.at[kk], recv_sem=recv_sems.at[kk],
                device_id=dev, device_id_type=MESH_ID)
            cp.start()
            sends.append(cp)
        for kk, (dev, idx) in enumerate(peers):
            pltpu.make_async_remote_copy(
                src_ref=v_ref, dst_ref=o_ref.at[idx],
                send_sem=send_sems.at[kk], recv_sem=recv_sems.at[kk],
                device_id=dev, device_id_type=MESH_ID).wait_recv()
        for cp in sends:
            cp.wait_send()

    return pl.pallas_call(
        body, name=name,
        in_specs=[pl.BlockSpec(memory_space=pltpu.VMEM)],
        out_specs=pl.BlockSpec(memory_space=pltpu.VMEM),
        out_shape=jax.ShapeDtypeStruct((N_DEV, r, LANES), F32),
        scratch_shapes=[pltpu.SemaphoreType.DMA((N_DEV - 1,)), pltpu.SemaphoreType.DMA((N_DEV - 1,))],
        compiler_params=pltpu.CompilerParams(vmem_limit_bytes=VMEM_LIMIT),
    )(vec)


def _adamw_math(w, g, m, v):
    m = ADAM_B1 * m + (1.0 - ADAM_B1) * g
    v = ADAM_B2 * v + (1.0 - ADAM_B2) * (g * g)
    m_hat = m / (1.0 - ADAM_B1 ** ADAM_STEP)
    v_hat = v / (1.0 - ADAM_B2 ** ADAM_STEP)
    delta = -ADAM_LR * (m_hat / (jnp.sqrt(v_hat) + ADAM_EPS) + ADAM_WD * w)
    return delta, m, v


def _adamw_sum(parts, w, m, v, tr, name, own=None, me=None):
    r, c = w.shape

    def body(*refs):
        if own is None:
            p_ref, w_ref, m_ref, v_ref, g_ref, d_ref, nm_ref, nv_ref = refs
            terms = [p_ref[kk] for kk in range(N_DEV)]
        else:
            me_ref, p_ref, own_ref, w_ref, m_ref, v_ref, g_ref, d_ref, nm_ref, nv_ref = refs
            terms = [jnp.where(me_ref[0] == kk, own_ref[0], p_ref[kk]).astype(F32) for kk in range(N_DEV)]
        g = terms[0]
        for t in terms[1:]:
            g = g + t
        g_ref[...] = g
        d_ref[...], nm_ref[...], nv_ref[...] = _adamw_math(w_ref[...], g, m_ref[...], v_ref[...])

    out_shape = [jax.ShapeDtypeStruct((r, c), F32)] * 4
    if own is None:
        blk = pl.BlockSpec((tr, c), lambda i: (i, 0))
        return pl.pallas_call(
            body, name=name, grid=(r // tr,),
            in_specs=[pl.BlockSpec((N_DEV, tr, c), lambda i: (0, i, 0)), blk, blk, blk],
            out_specs=[blk] * 4, out_shape=out_shape,
            compiler_params=_params(("parallel",)),
        )(parts, w, m, v)
    blk = pl.BlockSpec((tr, c), lambda i, me_ref: (i, 0))
    return pl.pallas_call(
        body, name=name,
        grid_spec=pltpu.PrefetchScalarGridSpec(
            num_scalar_prefetch=1, grid=(r // tr,),
            in_specs=[pl.BlockSpec((N_DEV, tr, c), lambda i, me_ref: (0, i, 0)),
                      pl.BlockSpec((1, tr, c), lambda i, me_ref: (me_ref[0], i, 0)), blk, blk, blk],
            out_specs=[blk] * 4),
        out_shape=out_shape,
        compiler_params=_params(("parallel",)),
    )(jnp.reshape(me, (1,)).astype(jnp.int32), parts, own, w, m, v)


def _slabs_to_wide(slabs, width, name):
    n, r, c = slabs.shape

    def body(i_ref, o_ref):
        for k in range(n):
            o_ref[:, c * k:c * (k + 1)] = i_ref[k]
        if width > n * c:
            o_ref[:, n * c:width] = jnp.zeros((ROW_BLOCK, width - n * c), o_ref.dtype)

    return pl.pallas_call(
        body, name=name, grid=(r // ROW_BLOCK,),
        in_specs=[pl.BlockSpec((n, ROW_BLOCK, c), lambda i: (0, i, 0))],
        out_specs=pl.BlockSpec((ROW_BLOCK, width), lambda i: (i, 0)),
        out_shape=jax.ShapeDtypeStruct((r, width), slabs.dtype),
        compiler_params=_params(("parallel",)),
    )(slabs)


def _wide_to_slabs(wide, c, name):
    r, width = wide.shape

    def body(i_ref, o_ref):
        for k in range(N_DEV):
            o_ref[k] = i_ref[:, c * k:c * (k + 1)]

    return pl.pallas_call(
        body, name=name, grid=(r // ROW_BLOCK,),
        in_specs=[pl.BlockSpec((ROW_BLOCK, width), lambda i: (i, 0))],
        out_specs=pl.BlockSpec((N_DEV, ROW_BLOCK, c), lambda i: (0, i, 0)),
        out_shape=jax.ShapeDtypeStruct((N_DEV, r, c), wide.dtype),
        compiler_params=_params(("parallel",)),
    )(wide)


_SMALL = ("norm1_g", "gf_b", "gb_b", "gla_norm_g", "attn_norm_g", "norm2_g", "conv_b", "final_norm_g",
          "gf_up", "gb_up", "conv_w")


def _pack(named):
    flat = jnp.concatenate([jnp.ravel(t).astype(F32) for t in named])
    tile = SUBLANES * LANES
    total = -(-flat.shape[0] // tile) * tile
    return jnp.pad(flat, (0, total - flat.shape[0])).reshape(total // LANES, LANES)


def _unpack(packed, shapes):
    flat = packed.reshape(-1)
    out, off = [], 0
    for shp in shapes:
        size = int(np.prod(shp))
        out.append(flat[off:off + size].reshape(shp))
        off += size
    return out


def kernel(x, norm1_g, w_in, gf_up, gf_b, gb_up, gb_b, gla_norm_g, attn_norm_g, w_out, norm2_g, w_gate, w_up, conv_w, conv_b, w_down, final_norm_g, loss_target, m_norm1_g, m_w_in, m_gf_up, m_gf_b, m_gb_up, m_gb_b, m_gla_norm_g, m_attn_norm_g, m_w_out, m_norm2_g, m_w_gate, m_w_up, m_conv_w, m_conv_b, m_w_down, m_final_norm_g, v_norm1_g, v_w_in, v_gf_up, v_gf_b, v_gb_up, v_gb_b, v_gla_norm_g, v_attn_norm_g, v_w_out, v_norm2_g, v_w_gate, v_w_up, v_conv_w, v_conv_b, v_w_down, v_final_norm_g):
    names = ("norm1_g", "w_in", "gf_up", "gf_b", "gb_up", "gb_b", "gla_norm_g", "attn_norm_g", "w_out", "norm2_g",
             "w_gate", "w_up", "conv_w", "conv_b", "w_down", "final_norm_g")
    ws = dict(zip(names, (norm1_g, w_in, gf_up, gf_b, gb_up, gb_b, gla_norm_g, attn_norm_g, w_out, norm2_g,
                          w_gate, w_up, conv_w, conv_b, w_down, final_norm_g)))
    ms = dict(zip(names, (m_norm1_g, m_w_in, m_gf_up, m_gf_b, m_gb_up, m_gb_b, m_gla_norm_g, m_attn_norm_g, m_w_out,
                          m_norm2_g, m_w_gate, m_w_up, m_conv_w, m_conv_b, m_w_down, m_final_norm_g)))
    vs = dict(zip(names, (v_norm1_g, v_w_in, v_gf_up, v_gf_b, v_gb_up, v_gb_b, v_gla_norm_g, v_attn_norm_g, v_w_out,
                          v_norm2_g, v_w_gate, v_w_up, v_conv_w, v_conv_b, v_w_down, v_final_norm_g)))
    me = 4 * lax.axis_index("x") + 2 * lax.axis_index("y") + lax.axis_index("c")
    big = ("w_in", "w_out", "w_gate", "w_up", "w_down")
    col_sharded = ("w_in", "w_gate", "w_up")

    def gather_start(group, name, deps=()):
        shards = [ws[n][0].astype(BF16) for n in group]
        lands = [lax.empty((N_DEV,) + t.shape, BF16) for t in shards]
        return _exchange_start(shards, lands, False, name, deps)

    def gather_finish(group, started, name, after):
        full = {}
        for n, own, t in zip(group, *_exchange_wait(started, name, after)):
            t = lax.dynamic_update_slice(t, own[None], (me, 0, 0))
            if n in col_sharded:
                full[n] = _slabs_to_wide(t, N_DEV * t.shape[2], "widen_" + n)
            else:
                full[n] = t.reshape(N_DEV * t.shape[1], t.shape[2])
        return full

    w_in_all = _all_gather_two_level(ws["w_in"][0].astype(BF16), "gather_w_in")
    full = {"w_in": _slabs_to_wide(w_in_all, IN_PAD, "widen_w_in")}
    late = {"mixer": ("w_out",), "ffn": ("w_gate", "w_up", "w_down")}
    started_late = {"mixer": gather_start(late["mixer"], "gather_w_out_start", deps=(full["w_in"],))}
    started_late["ffn"] = gather_start(late["ffn"], "gather_ffn_start", deps=(started_late["mixer"]["token"],))

    def late_weights(part, after):
        return gather_finish(late[part], started_late[part], "gather_" + part + "_wait", after)

    small_sharded = ("gf_up", "gb_up", "conv_w")
    sm = _all_gather_vmem(_pack([ws[n][0] for n in small_sharded]), "gather_small")
    shard_shapes = [ws[n][0].shape for n in small_sharded]
    per_dev = [_unpack(sm[d], shard_shapes) for d in range(N_DEV)]
    for i, n in enumerate(small_sharded):
        full[n] = jnp.concatenate([per_dev[d][i] for d in range(N_DEV)], axis=1)
    for n in ("norm1_g", "gf_b", "gb_b", "gla_norm_g", "attn_norm_g", "norm2_g", "conv_b"):
        full[n] = ws[n]
    full["final_norm_g"] = final_norm_g.reshape(1, D_MODEL)

    in_flight = []

    def grad_sink(group, grads):
        partials = []
        for n, t in zip(group, grads):
            t = t.astype(BF16)
            if n in col_sharded:
                t = _wide_to_slabs(t, ws[n].shape[2], "slabs_" + n)
            else:
                t = t.reshape(N_DEV, t.shape[0] // N_DEV, t.shape[1])
            partials.append(t)
        lands = [lax.empty(t.shape, t.dtype) for t in partials]
        started = _exchange_start(partials, lands, True, "exchange_" + "_".join(group) + "_start")
        in_flight.append((group, started))
        return (started["token"],)

    loss_acc, grad_x, g = _local_step(x[0], loss_target[0], full, late_weights, grad_sink,
                                      first_dep=(started_late["ffn"]["token"],))

    out = {}
    for group, started in in_flight:
        sent, landed = _exchange_wait(started, "exchange_" + "_".join(group) + "_wait", grad_x)
        for n, parts, own in zip(group, landed, sent):
            out[n] = _adamw_sum(parts, ws[n][0], ms[n][0], vs[n][0], 64, "adamw_" + n, own=own, me=me)

    small_full_shapes = [g[n].shape for n in _SMALL]
    gsmall = _pack([g[n] for n in _SMALL] + [loss_acc[0:1, 0:1]])
    gathered_small = _all_gather_vmem(gsmall, "gather_small_grads")

    def full_small(d):
        parts = []
        for n in _SMALL:
            t = d[n].reshape(d[n].shape[-2:]) if d[n].ndim == 3 else d[n].reshape(1, -1)
            if n in small_sharded:
                wide = jnp.zeros((t.shape[0], t.shape[1] * N_DEV), F32)
                t = lax.dynamic_update_slice_in_dim(wide, t, me * t.shape[1], axis=1)
            parts.append(t)
        return _pack(parts + [jnp.zeros((1, 1), F32)])

    rows = gsmall.shape[0]
    res_small = _adamw_sum(gathered_small, full_small(ws), full_small(ms), full_small(vs), rows, "adamw_small")
    loss = res_small[0].reshape(-1)[sum(int(np.prod(sh)) for sh in small_full_shapes)]
    unpacked = [_unpack(t, small_full_shapes) for t in res_small]
    for i, n in enumerate(_SMALL):
        vals = [u[i] for u in unpacked]
        if n in small_sharded:
            width = vals[0].shape[1] // N_DEV
            vals = [lax.dynamic_slice_in_dim(t, me * width, width, axis=1) for t in vals]
        out[n] = vals

    result = [loss, grad_x[None]]
    for kind in range(4):
        for n in names:
            result.append(out[n][kind].reshape(ws[n].shape))
    return tuple(result)
```

```python
import functools

import numpy as np
import jax
import jax.numpy as jnp
from jax import lax
from jax.experimental import pallas as pl
from jax.experimental.pallas import tpu as pltpu

F32 = jnp.float32
BF16 = jnp.bfloat16

D_MODEL = 2048
ATTN_W = 1024
ATTN_HEADS = 8
HEAD_DIM = 128
ROPE_DIM = 32
ROPE_THETA = 500000.0
DILATIONS = (1, 4, 16)
N_SIDE = 64
GLA_KW = 512
GLA_VW = 1024
GLA_HEADS = 4
GLA_DK = 128
GLA_DV = 256
GLA_RANK = 16
GLA_GATE_NORM = 16.0
GLA_CHUNK = 64
IN_WIDTH = 6176
IN_PAD = 6400
D_FF = 5632
EPS = 1e-6
N_DEV = 8

OFF_AQ, OFF_AK, OFF_AV = 0, 1024, 2048
OFF_GQ, OFF_GK, OFF_GV, OFF_GR, OFF_Z = 3072, 3584, 4096, 5120, 6144

ADAM_LR, ADAM_B1, ADAM_B2, ADAM_EPS, ADAM_WD, ADAM_STEP = 0.001, 0.9, 0.999, 1e-08, 0.01, 10

LANES = 128
SUBLANES = 8
VMEM_LIMIT = 56 * 1024 * 1024
ROW_BLOCK = 256
ATTN_BLOCK = 128
GLA_CHUNKS_PER_STEP = 4
NEG = -1e30
MESH_ID = pl.DeviceIdType.MESH


def _params(sem):
    return pltpu.CompilerParams(dimension_semantics=sem, vmem_limit_bytes=VMEM_LIMIT)


def _dot(a, b):
    return lax.dot_general(a, b, (((1,), (0,)), ((), ())), preferred_element_type=F32)


def _dot_nt(a, b):
    return lax.dot_general(a, b, (((1,), (1,)), ((), ())), preferred_element_type=F32)


def _dot_tn(a, b):
    return lax.dot_general(a, b, (((0,), (0,)), ((), ())), preferred_element_type=F32)


def _sigmoid(x):
    return 0.5 * jnp.tanh(0.5 * x) + 0.5


def _matmul(pairs, mode, out_dtype, tm, tn, tk, name, res=None, deps=()):
    a0, b0 = pairs[0]
    if mode == "nn":
        (m, kdim), n = a0.shape, b0.shape[1]
    elif mode == "nt":
        (m, kdim), n = a0.shape, b0.shape[0]
    else:
        (kdim, m), n = a0.shape, b0.shape[1]
    assert m % tm == 0 and n % tn == 0 and kdim % tk == 0, (name, m, n, kdim)
    nk = kdim // tk
    npairs = len(pairs)
    steps = nk * npairs
    dot = {"nn": _dot, "nt": _dot_nt, "tn": _dot_tn}[mode]

    def kidx(p):
        return lambda k: jnp.clip(k - p * nk, 0, nk - 1)

    in_specs, args = [], []
    for p, (a, b) in enumerate(pairs):
        kk = kidx(p)
        if mode == "nn":
            in_specs += [pl.BlockSpec((tm, tk), lambda i, j, k, kk=kk: (i, kk(k))),
                         pl.BlockSpec((tk, tn), lambda i, j, k, kk=kk: (kk(k), j))]
        elif mode == "nt":
            in_specs += [pl.BlockSpec((tm, tk), lambda i, j, k, kk=kk: (i, kk(k))),
                         pl.BlockSpec((tn, tk), lambda i, j, k, kk=kk: (j, kk(k)))]
        else:
            in_specs += [pl.BlockSpec((tk, tm), lambda i, j, k, kk=kk: (kk(k), i)),
                         pl.BlockSpec((tk, tn), lambda i, j, k, kk=kk: (kk(k), j))]
        args += [a, b]
    if res is not None:
        in_specs.append(pl.BlockSpec((tm, tn), lambda i, j, k: (i, j)))
        args.append(res)
    in_specs += [pl.BlockSpec(memory_space=pl.ANY)] * len(deps)
    args += list(deps)

    def body(*refs):
        ab = refs[:2 * npairs]
        res_ref = refs[2 * npairs] if res is not None else None
        o_ref = refs[2 * npairs + (1 if res is not None else 0) + len(deps)]

        def finish(acc):
            if res_ref is not None:
                acc = acc + res_ref[...]
            o_ref[...] = acc.astype(out_dtype)

        if steps == 1:
            finish(dot(ab[0][...], ab[1][...]))
            return
        acc_ref = refs[-1]
        k = pl.program_id(2)

        @pl.when(k == 0)
        def _():
            acc_ref[...] = jnp.zeros_like(acc_ref)

        for p in range(npairs):
            @pl.when((k >= p * nk) & (k < (p + 1) * nk))
            def _(p=p):
                acc_ref[...] += dot(ab[2 * p][...], ab[2 * p + 1][...])

        @pl.when(k == steps - 1)
        def _():
            finish(acc_ref[...])

    return pl.pallas_call(
        body, name=name,
        grid=(m // tm, n // tn, steps),
        in_specs=in_specs,
        out_specs=pl.BlockSpec((tm, tn), lambda i, j, k: (i, j)),
        out_shape=jax.ShapeDtypeStruct((m, n), out_dtype),
        scratch_shapes=[] if steps == 1 else [pltpu.VMEM((tm, tn), F32)],
        compiler_params=_params(("parallel", "parallel", "arbitrary")),
    )(*args)


def _rms_fwd(x, g, name):
    s, d = x.shape

    def body(x_ref, g_ref, o_ref):
        xv = x_ref[...]
        r = lax.rsqrt(jnp.mean(xv * xv, axis=-1, keepdims=True) + EPS)
        o_ref[...] = (xv * r * g_ref[...]).astype(BF16)

    return pl.pallas_call(
        body, name=name, grid=(s // ROW_BLOCK,),
        in_specs=[pl.BlockSpec((ROW_BLOCK, d), lambda i: (i, 0)), pl.BlockSpec((1, d), lambda i: (0, 0))],
        out_specs=pl.BlockSpec((ROW_BLOCK, d), lambda i: (i, 0)),
        out_shape=jax.ShapeDtypeStruct((s, d), BF16),
        compiler_params=_params(("parallel",)),
    )(x, g)


def _rms_bwd(dn, x, g, dres, name, bf16_copy=True):
    s, d = x.shape

    def body(dn_ref, x_ref, g_ref, dres_ref, dx_ref, *rest):
        gg_ref = rest[-1]
        i = pl.program_id(0)
        xv, dnv = x_ref[...], dn_ref[...]
        r = lax.rsqrt(jnp.mean(xv * xv, axis=-1, keepdims=True) + EPS)
        dng = dnv * g_ref[...]
        c = jnp.mean(dng * xv, axis=-1, keepdims=True)
        dx = dres_ref[...] + r * dng - xv * (r * r * r * c)
        dx_ref[...] = dx
        if bf16_copy:
            rest[0][...] = dx.astype(BF16)

        @pl.when(i == 0)
        def _():
            gg_ref[...] = jnp.zeros_like(gg_ref)

        gg_ref[...] += jnp.sum(dnv * xv * r, axis=0, keepdims=True)

    row = pl.BlockSpec((ROW_BLOCK, d), lambda i: (i, 0))
    vec = pl.BlockSpec((1, d), lambda i: (0, 0))
    return pl.pallas_call(
        body, name=name, grid=(s // ROW_BLOCK,),
        in_specs=[row, row, vec, row],
        out_specs=[row] + [row] * bf16_copy + [vec],
        out_shape=[jax.ShapeDtypeStruct((s, d), F32)] + [jax.ShapeDtypeStruct((s, d), BF16)] * bf16_copy
        + [jax.ShapeDtypeStruct((1, d), F32)],
        compiler_params=_params(("arbitrary",)),
    )(dn, x, g, dres)


def _final_loss(h2, target, g, name="final_loss"):
    s, d = h2.shape

    def body(h_ref, t_ref, g_ref, dh_ref, dhb_ref, loss_ref, gg_ref):
        i = pl.program_id(0)
        hv, gv = h_ref[...], g_ref[...]
        r = lax.rsqrt(jnp.mean(hv * hv, axis=-1, keepdims=True) + EPS)
        e = hv * r * gv - t_ref[...]
        dy = e * (1.0 / d)
        dyg = dy * gv
        c = jnp.mean(dyg * hv, axis=-1, keepdims=True)
        dh = r * dyg - hv * (r * r * r * c)
        dh_ref[...] = dh
        dhb_ref[...] = dh.astype(BF16)

        @pl.when(i == 0)
        def _():
            gg_ref[...] = jnp.zeros_like(gg_ref)
            loss_ref[...] = jnp.zeros_like(loss_ref)

        gg_ref[...] += jnp.sum(dy * hv * r, axis=0, keepdims=True)
        loss_ref[...] += jnp.sum(jnp.sum(e * e, axis=-1, keepdims=True), axis=0, keepdims=True) * (0.5 / d)

    row = pl.BlockSpec((ROW_BLOCK, d), lambda i: (i, 0))
    vec = pl.BlockSpec((1, d), lambda i: (0, 0))
    return pl.pallas_call(
        body, name=name, grid=(s // ROW_BLOCK,),
        in_specs=[row, row, vec],
        out_specs=[row, row, pl.BlockSpec((SUBLANES, LANES), lambda i: (0, 0)), vec],
        out_shape=[jax.ShapeDtypeStruct((s, d), F32), jax.ShapeDtypeStruct((s, d), BF16),
                   jax.ShapeDtypeStruct((SUBLANES, LANES), F32), jax.ShapeDtypeStruct((1, d), F32)],
        compiler_params=_params(("arbitrary",)),
    )(h2, target, g)


def _rope_tables(s):
    pos = jnp.arange(s, dtype=F32)
    inv_freq = ROPE_THETA ** (-jnp.arange(0, ROPE_DIM, 2, dtype=F32) / ROPE_DIM)
    ang = pos[:, None] * inv_freq[None, :]
    cos, sin = jnp.cos(ang), jnp.sin(ang)
    half = ROPE_DIM // 2
    rest = HEAD_DIM - ROPE_DIM
    c = jnp.concatenate([cos, cos, jnp.ones((s, rest), F32)], axis=1)
    sm = jnp.concatenate([-sin, jnp.zeros((s, half + rest), F32)], axis=1)
    sp = jnp.concatenate([jnp.zeros((s, half), F32), sin, jnp.zeros((s, rest), F32)], axis=1)
    return c, sm, sp


def _res_shape(s, groups, dil, dtype):
    return jax.ShapeDtypeStruct((s // dil, dil * groups * LANES), dtype)


def _res_spec(groups, dil):
    return pl.BlockSpec((ROW_BLOCK // dil, dil * groups * LANES), lambda i: (i, 0))


def _to_residues(scr, o_ref, dil):
    groups, rows = scr.shape[0], ROW_BLOCK // dil
    for r in range(dil):
        for h in range(groups):
            piece = scr[h] if dil == 1 else scr.at[h][pl.ds(r, rows, stride=dil), :]
            o_ref[:, (r * groups + h) * LANES:(r * groups + h + 1) * LANES] = piece.astype(o_ref.dtype)


def _from_residues(i_ref, scr, dil):
    groups, rows = scr.shape[0], ROW_BLOCK // dil
    for r in range(dil):
        for h in range(groups):
            piece = i_ref[:, (r * groups + h) * LANES:(r * groups + h + 1) * LANES].astype(F32)
            if dil == 1:
                scr[h] = piece
            else:
                scr.at[h][pl.ds(r, rows, stride=dil), :] = piece


def _rope_fwd(proj, tables, name="rope_fwd"):
    s = proj.shape[0]
    half = ROPE_DIM // 2
    nd = len(DILATIONS)

    def body(p_ref, c_ref, sm_ref, sp_ref, *rest):
        outs, scr = rest[:3 * nd], rest[3 * nd]
        c, sm, sp = c_ref[...], sm_ref[...], sp_ref[...]
        for gi, off in enumerate((OFF_AQ, OFF_AK, OFF_AV)):
            for h in range(ATTN_HEADS):
                t = p_ref[:, off + h * HEAD_DIM: off + (h + 1) * HEAD_DIM]
                if off != OFF_AV:
                    t = t * c + pltpu.roll(t, HEAD_DIM - half, 1) * sm + pltpu.roll(t, half, 1) * sp
                scr[h] = t
            for di, dil in enumerate(DILATIONS):
                _to_residues(scr, outs[3 * di + gi], dil)

    tab = pl.BlockSpec((ROW_BLOCK, HEAD_DIM), lambda i: (i, 0))
    outs = pl.pallas_call(
        body, name=name, grid=(s // ROW_BLOCK,),
        in_specs=[pl.BlockSpec((ROW_BLOCK, 3 * ATTN_W), lambda i: (i, 0)), tab, tab, tab],
        out_specs=[_res_spec(ATTN_HEADS, d) for d in DILATIONS for _ in range(3)],
        out_shape=[_res_shape(s, ATTN_HEADS, d, BF16) for d in DILATIONS for _ in range(3)],
        scratch_shapes=[pltpu.VMEM((ATTN_HEADS, ROW_BLOCK, LANES), F32)],
        compiler_params=_params(("parallel",)),
    )(proj, *tables)
    return [tuple(outs[3 * di:3 * di + 3]) for di in range(nd)]


def _rope_bwd(grads, tables, name="rope_bwd"):
    s = grads[0][0].shape[0] * DILATIONS[0]
    half = ROPE_DIM // 2
    nd = len(DILATIONS)

    def body(*refs):
        ins = refs[:3 * nd]
        c_ref, sm_ref, sp_ref, o_ref = refs[3 * nd:3 * nd + 4]
        scrs = refs[3 * nd + 4:]
        c, sm, sp = c_ref[...], sm_ref[...], sp_ref[...]
        for gi, off in enumerate((OFF_AQ, OFF_AK, OFF_AV)):
            for di, dil in enumerate(DILATIONS):
                _from_residues(ins[3 * di + gi], scrs[di], dil)
            for h in range(ATTN_HEADS):
                t = scrs[0][h]
                for scr in scrs[1:]:
                    t = t + scr[h]
                if off != OFF_AV:
                    t = t * c + pltpu.roll(t * sm, half, 1) + pltpu.roll(t * sp, HEAD_DIM - half, 1)
                o_ref[:, off + h * HEAD_DIM: off + (h + 1) * HEAD_DIM] = t.astype(BF16)

    tab = pl.BlockSpec((ROW_BLOCK, HEAD_DIM), lambda i: (i, 0))
    return pl.pallas_call(
        body, name=name, grid=(s // ROW_BLOCK,),
        in_specs=[_res_spec(ATTN_HEADS, d) for d in DILATIONS for _ in range(3)] + [tab, tab, tab],
        out_specs=pl.BlockSpec((ROW_BLOCK, 3 * ATTN_W), lambda i: (i, 0)),
        out_shape=jax.ShapeDtypeStruct((s, IN_PAD), BF16),
        scratch_shapes=[pltpu.VMEM((ATTN_HEADS, ROW_BLOCK, LANES), F32) for _ in DILATIONS],
        compiler_params=_params(("parallel",)),
    )(*[t for g in grads for t in g], *tables)


ATTN_GROUP = 4


def _window_specs(nsteps, width):
    rows, hb = ATTN_GROUP * ATTN_BLOCK, N_SIDE
    per = rows // hb
    cur = pl.BlockSpec((rows, width), lambda r, j: (j, r))
    prev = pl.BlockSpec((hb, width), lambda r, j: (jnp.maximum(per * j - 1, 0), r))
    nxt = pl.BlockSpec((hb, width), lambda r, j: (jnp.minimum(per * (j + 1), per * nsteps - 1), r))
    return prev, cur, nxt


def _block(ref, b, sl):
    return ref[b * ATTN_BLOCK:(b + 1) * ATTN_BLOCK, sl]


def _edge(prev_ref, cur_ref, next_ref, b, sl):
    qb, hb = ATTN_BLOCK, N_SIDE
    before = prev_ref[:, sl] if b == 0 else cur_ref[b * qb - hb:b * qb, sl]
    after = next_ref[:, sl] if b == ATTN_GROUP - 1 else cur_ref[(b + 1) * qb:(b + 1) * qb + hb, sl]
    return jnp.concatenate([before, after], axis=0)


def _band_masks(j, length):
    qb, hb = ATTN_BLOCK, N_SIDE
    row = lax.broadcasted_iota(jnp.int32, (qb, qb), 0)
    col = lax.broadcasted_iota(jnp.int32, (qb, qb), 1)

    def edge_pos(i):
        return j * qb - hb + i + jnp.where(i >= hb, qb, 0)

    def ok(a, b, outside):
        return (jnp.abs(a - b) <= N_SIDE) & (outside >= 0) & (outside < length)

    cur = jnp.abs(row - col) <= N_SIDE
    edge_k = ok(j * qb + row, edge_pos(col), edge_pos(col))
    edge_q = ok(edge_pos(row), j * qb + col, edge_pos(row))
    return cur, edge_k, edge_q


def _attn_fwd(q, k, v, dil, name):
    length = q.shape[0]
    qb = ATTN_BLOCK
    nsteps = length // (ATTN_GROUP * qb)
    scale = HEAD_DIM ** -0.5

    def body(q_ref, kp_ref, kc_ref, kn_ref, vp_ref, vc_ref, vn_ref, o_ref, lse_ref):
        masks = [_band_masks(pl.program_id(1) * ATTN_GROUP + b, length) for b in range(ATTN_GROUP)]
        lane = lax.broadcasted_iota(jnp.int32, (qb, LANES), 1)
        units = [(b, h, slice(h * HEAD_DIM, (h + 1) * HEAD_DIM)) for b in range(ATTN_GROUP)
                 for h in range(ATTN_HEADS)]
        scores = [(_dot_nt(_block(q_ref, b, sl), _block(kc_ref, b, sl)),
                   _dot_nt(_block(q_ref, b, sl), _edge(kp_ref, kc_ref, kn_ref, b, sl))) for b, _, sl in units]
        probs = []
        lse_acc = [jnp.zeros((qb, LANES), F32) for _ in range(ATTN_GROUP)]
        for (b, h, _), (s_c, s_e) in zip(units, scores):
            valid_c, valid_e, _ = masks[b]
            s_c = jnp.where(valid_c, s_c * scale, NEG)
            s_e = jnp.where(valid_e, s_e * scale, NEG)
            m = jnp.max(jnp.maximum(s_c, s_e), axis=-1, keepdims=True)
            p_c, p_e = jnp.exp(s_c - m), jnp.exp(s_e - m)
            den = jnp.sum(p_c + p_e, axis=-1, keepdims=True)
            probs.append((p_c.astype(BF16), p_e.astype(BF16), 1.0 / den))
            lse_acc[b] = jnp.where(lane == h, m + jnp.log(den), lse_acc[b])
        for (b, _, sl), (p_c, p_e, inv) in zip(units, probs):
            o_ref[b * qb:(b + 1) * qb, sl] = (_dot(p_c, _block(vc_ref, b, sl))
                                              + _dot(p_e, _edge(vp_ref, vc_ref, vn_ref, b, sl))) * inv
        for b in range(ATTN_GROUP):
            lse_ref[b * qb:(b + 1) * qb, :] = lse_acc[b]

    prev, cur, nxt = _window_specs(nsteps, ATTN_W)
    return pl.pallas_call(
        body, name=name, grid=(dil, nsteps),
        in_specs=[cur, prev, cur, nxt, prev, cur, nxt],
        out_specs=[cur, pl.BlockSpec((ATTN_GROUP * qb, LANES), lambda r, j: (j, r))],
        out_shape=[jax.ShapeDtypeStruct((length, dil * ATTN_W), F32),
                   jax.ShapeDtypeStruct((length, dil * LANES), F32)],
        compiler_params=_params(("parallel", "parallel")),
    )(q, k, k, k, v, v, v)


def _attn_combine(outs, lses, g, name="attn_combine"):
    s = outs[0].shape[0] * DILATIONS[0]
    nd = len(DILATIONS)

    def body(*refs):
        o_refs, l_refs = refs[:nd], refs[nd:2 * nd]
        g_ref, o_ref, n_ref = refs[2 * nd:2 * nd + 3]
        lse_outs = refs[2 * nd + 3:3 * nd + 3]
        o_scr, l_scr = refs[3 * nd + 3:4 * nd + 3], refs[4 * nd + 3:5 * nd + 3]
        for di, dil in enumerate(DILATIONS):
            _from_residues(o_refs[di], o_scr[di], dil)
            _from_residues(l_refs[di], l_scr[di], dil)
        ls = [scr[0] for scr in l_scr]
        m = ls[0]
        for l in ls[1:]:
            m = jnp.maximum(m, l)
        es = [jnp.exp(l - m) for l in ls]
        z = es[0]
        for e in es[1:]:
            z = z + e
        ws = [e / z for e in es]
        l_scr[0][0] = m + jnp.log(z)
        for di, dil in enumerate(DILATIONS):
            _to_residues(l_scr[0], lse_outs[di], dil)
        ssq = jnp.zeros((ROW_BLOCK, 1), F32)
        for h in range(ATTN_HEADS):
            sl = slice(h * HEAD_DIM, (h + 1) * HEAD_DIM)
            acc = ws[0][:, h:h + 1] * o_scr[0][h]
            for w, scr in zip(ws[1:], o_scr[1:]):
                acc = acc + w[:, h:h + 1] * scr[h]
            o_ref[:, sl] = acc
            ssq = ssq + jnp.sum(acc * acc, axis=-1, keepdims=True)
        r = lax.rsqrt(ssq * (1.0 / ATTN_W) + EPS)
        n_ref[...] = (o_ref[...] * r * g_ref[...]).astype(BF16)

    blk = pl.BlockSpec((ROW_BLOCK, ATTN_W), lambda i: (i, 0))
    outs_ = pl.pallas_call(
        body, name=name, grid=(s // ROW_BLOCK,),
        in_specs=[_res_spec(ATTN_HEADS, d) for d in DILATIONS] + [_res_spec(1, d) for d in DILATIONS]
        + [pl.BlockSpec((1, ATTN_W), lambda i: (0, 0))],
        out_specs=[blk, blk] + [_res_spec(1, d) for d in DILATIONS],
        out_shape=[jax.ShapeDtypeStruct((s, ATTN_W), F32), jax.ShapeDtypeStruct((s, D_MODEL), BF16)]
        + [_res_shape(s, 1, d, F32) for d in DILATIONS],
        scratch_shapes=[pltpu.VMEM((ATTN_HEADS, ROW_BLOCK, LANES), F32) for _ in DILATIONS]
        + [pltpu.VMEM((1, ROW_BLOCK, LANES), F32) for _ in DILATIONS],
        compiler_params=_params(("parallel",)),
    )(*outs, *lses, g)
    return outs_[0], outs_[1], list(outs_[2:])


def _attn_prebwd(dcat, o, g, name="attn_prebwd"):
    s = o.shape[0]
    nd = len(DILATIONS)

    def body(dy_ref, o_ref, g_ref, *rest):
        do_outs, delta_outs, gg_ref = rest[:nd], rest[nd:2 * nd], rest[2 * nd]
        do_scr, delta_scr = rest[2 * nd + 1], rest[2 * nd + 2]
        i = pl.program_id(0)
        dy, ov = dy_ref[...], o_ref[...]
        r = lax.rsqrt(jnp.mean(ov * ov, axis=-1, keepdims=True) + EPS)
        dyg = dy * g_ref[...]
        c = jnp.mean(dyg * ov, axis=-1, keepdims=True)
        do = r * dyg - ov * (r * r * r * c)
        prod = do * ov
        lane = lax.broadcasted_iota(jnp.int32, (ROW_BLOCK, LANES), 1)
        acc = jnp.zeros((ROW_BLOCK, LANES), F32)
        for h in range(ATTN_HEADS):
            sl = slice(h * HEAD_DIM, (h + 1) * HEAD_DIM)
            do_scr[h] = do[:, sl]
            acc = jnp.where(lane == h, jnp.sum(prod[:, sl], axis=-1, keepdims=True), acc)
        delta_scr[0] = acc
        for di, dil in enumerate(DILATIONS):
            _to_residues(do_scr, do_outs[di], dil)
            _to_residues(delta_scr, delta_outs[di], dil)

        @pl.when(i == 0)
        def _():
            gg_ref[...] = jnp.zeros_like(gg_ref)

        gg_ref[...] += jnp.sum(dy * ov * r, axis=0, keepdims=True)

    blk = pl.BlockSpec((ROW_BLOCK, ATTN_W), lambda i: (i, 0))
    vec = pl.BlockSpec((1, ATTN_W), lambda i: (0, 0))
    outs = pl.pallas_call(
        body, name=name, grid=(s // ROW_BLOCK,),
        in_specs=[blk, blk, vec],
        out_specs=[_res_spec(ATTN_HEADS, d) for d in DILATIONS] + [_res_spec(1, d) for d in DILATIONS] + [vec],
        out_shape=[_res_shape(s, ATTN_HEADS, d, BF16) for d in DILATIONS]
        + [_res_shape(s, 1, d, F32) for d in DILATIONS] + [jax.ShapeDtypeStruct((1, ATTN_W), F32)],
        scratch_shapes=[pltpu.VMEM((ATTN_HEADS, ROW_BLOCK, LANES), F32), pltpu.VMEM((1, ROW_BLOCK, LANES), F32)],
        compiler_params=_params(("arbitrary",)),
    )(dcat, o, g)
    return list(outs[:nd]), list(outs[nd:2 * nd]), outs[2 * nd]


def _attn_bwd(q, k, v, do, lse, delta, dil, name):
    length = q.shape[0]
    qb = ATTN_BLOCK
    nsteps = length // (ATTN_GROUP * qb)
    scale = HEAD_DIM ** -0.5

    def body(qp, qc, qn, kp, kc, kn, vp, vc, vn, dop, doc, don, lp, lc, ln, dp, dc, dn, dq_ref, dk_ref, dv_ref):
        masks = [_band_masks(pl.program_id(1) * ATTN_GROUP + b, length) for b in range(ATTN_GROUP)]
        everything = slice(None)
        lse_e = [_edge(lp, lc, ln, b, everything) for b in range(ATTN_GROUP)]
        del_e = [_edge(dp, dc, dn, b, everything) for b in range(ATTN_GROUP)]
        units = [(b, h, slice(h * HEAD_DIM, (h + 1) * HEAD_DIM)) for b in range(ATTN_GROUP)
                 for h in range(ATTN_HEADS)]
        prods = []
        for b, _, sl in units:
            q_c, k_c, v_c, do_c = _block(qc, b, sl), _block(kc, b, sl), _block(vc, b, sl), _block(doc, b, sl)
            q_e, k_e = _edge(qp, qc, qn, b, sl), _edge(kp, kc, kn, b, sl)
            v_e, do_e = _edge(vp, vc, vn, b, sl), _edge(dop, doc, don, b, sl)
            prods.append((_dot_nt(q_c, k_c), _dot_nt(do_c, v_c), _dot_nt(q_c, k_e), _dot_nt(do_c, v_e),
                          _dot_nt(q_e, k_c), _dot_nt(do_e, v_c)))
        parts = []
        for (b, h, _), (s_cc, dp_cc, s_ek, dp_ek, s_eq, dp_eq) in zip(units, prods):
            valid_c, valid_ek, valid_eq = masks[b]
            hc = slice(h, h + 1)
            lse_c, del_c = _block(lc, b, hc), _block(dc, b, hc)
            p_cc = jnp.where(valid_c, jnp.exp(s_cc * scale - lse_c), 0.0)
            ds_cc = (p_cc * (dp_cc - del_c)).astype(BF16)
            p_ek = jnp.where(valid_ek, jnp.exp(s_ek * scale - lse_c), 0.0)
            ds_ek = (p_ek * (dp_ek - del_c)).astype(BF16)
            p_eq = jnp.where(valid_eq, jnp.exp(s_eq * scale - lse_e[b][:, hc]), 0.0)
            ds_eq = (p_eq * (dp_eq - del_e[b][:, hc])).astype(BF16)
            parts.append((p_cc.astype(BF16), ds_cc, ds_ek, p_eq.astype(BF16), ds_eq))
        for (b, _, sl), (p_cc, ds_cc, ds_ek, p_eq, ds_eq) in zip(units, parts):
            rows = slice(b * qb, (b + 1) * qb)
            q_c, k_c, do_c = _block(qc, b, sl), _block(kc, b, sl), _block(doc, b, sl)
            q_e, k_e, do_e = _edge(qp, qc, qn, b, sl), _edge(kp, kc, kn, b, sl), _edge(dop, doc, don, b, sl)
            dq_ref[rows, sl] = ((_dot(ds_cc, k_c) + _dot(ds_ek, k_e)) * scale).astype(BF16)
            dk_ref[rows, sl] = ((_dot_tn(ds_cc, q_c) + _dot_tn(ds_eq, q_e)) * scale).astype(BF16)
            dv_ref[rows, sl] = (_dot_tn(p_cc, do_c) + _dot_tn(p_eq, do_e)).astype(BF16)

    wide, narrow = list(_window_specs(nsteps, ATTN_W)), list(_window_specs(nsteps, LANES))
    return tuple(pl.pallas_call(
        body, name=name, grid=(dil, nsteps),
        in_specs=wide * 4 + narrow * 2,
        out_specs=[wide[1]] * 3,
        out_shape=[jax.ShapeDtypeStruct((length, dil * ATTN_W), BF16)] * 3,
        compiler_params=_params(("parallel", "parallel")),
    )(q, q, q, k, k, k, v, v, v, do, do, do, lse, lse, lse, delta, delta, delta))


def _gate_matrices(gf_up, gb_up):
    pad = LANES - 2 * GLA_RANK
    uf = jnp.concatenate([gf_up, jnp.zeros((GLA_RANK + pad, GLA_KW), gf_up.dtype)], axis=0)
    ub = jnp.concatenate([jnp.zeros((GLA_RANK, GLA_KW), gb_up.dtype), gb_up, jnp.zeros((pad, GLA_KW), gb_up.dtype)], axis=0)
    return uf.astype(BF16), ub.astype(BF16)


def _log_sigmoid(x):
    return jnp.minimum(x, 0.0) - jnp.log(1.0 + jnp.exp(-jnp.abs(x)))


def _gla_gates(proj, uf, ub, gf_b, gb_b, name="gla_gates"):
    s = proj.shape[0]

    def body(z_ref, uf_ref, ub_ref, bf_ref, bb_ref, gf_ref, gb_ref):
        z = z_ref[...].astype(BF16)
        gf_ref[...] = _log_sigmoid(_dot(z, uf_ref[...]) + bf_ref[...]) * (1.0 / GLA_GATE_NORM)
        gb_ref[...] = _log_sigmoid(_dot(z, ub_ref[...]) + bb_ref[...]) * (1.0 / GLA_GATE_NORM)

    mat = pl.BlockSpec((LANES, GLA_KW), lambda i: (0, 0))
    vec = pl.BlockSpec((1, GLA_KW), lambda i: (0, 0))
    out = pl.BlockSpec((ROW_BLOCK, GLA_KW), lambda i: (i, 0))
    return pl.pallas_call(
        body, name=name, grid=(s // ROW_BLOCK,),
        in_specs=[pl.BlockSpec((ROW_BLOCK, LANES), lambda i: (i, OFF_Z // LANES)), mat, mat, vec, vec],
        out_specs=[out, out],
        out_shape=[jax.ShapeDtypeStruct((s, GLA_KW), F32)] * 2,
        compiler_params=_params(("parallel",)),
    )(proj, uf, ub, gf_b, gb_b)


def _gla_gates_bwd(dgf, dgb, proj, uf, ub, gf_b, gb_b, dproj, name="gla_gates_bwd"):
    s = proj.shape[0]
    tail = IN_PAD - OFF_Z

    def body(dgf_ref, dgb_ref, z_ref, uf_ref, ub_ref, bf_ref, bb_ref, _, dz_ref, guf_ref, gub_ref, gbf_ref, gbb_ref):
        i = pl.program_id(0)
        z = z_ref[...].astype(BF16)
        uf_, ub_ = uf_ref[...], ub_ref[...]
        dpf = dgf_ref[...] * (1.0 / GLA_GATE_NORM) * _sigmoid(-(_dot(z, uf_) + bf_ref[...]))
        dpb = dgb_ref[...] * (1.0 / GLA_GATE_NORM) * _sigmoid(-(_dot(z, ub_) + bb_ref[...]))
        dpf_b, dpb_b = dpf.astype(BF16), dpb.astype(BF16)
        dz_ref[:, 0:LANES] = (_dot_nt(dpf_b, uf_) + _dot_nt(dpb_b, ub_)).astype(BF16)
        dz_ref[:, LANES:tail] = jnp.zeros((ROW_BLOCK, tail - LANES), BF16)

        @pl.when(i == 0)
        def _():
            for r in (guf_ref, gub_ref, gbf_ref, gbb_ref):
                r[...] = jnp.zeros_like(r)

        guf_ref[...] += _dot_tn(z, dpf_b)
        gub_ref[...] += _dot_tn(z, dpb_b)
        gbf_ref[...] += jnp.sum(dpf, axis=0, keepdims=True)
        gbb_ref[...] += jnp.sum(dpb, axis=0, keepdims=True)

    mat = pl.BlockSpec((LANES, GLA_KW), lambda i: (0, 0))
    vec = pl.BlockSpec((1, GLA_KW), lambda i: (0, 0))
    blk = pl.BlockSpec((ROW_BLOCK, GLA_KW), lambda i: (i, 0))
    return pl.pallas_call(
        body, name=name, grid=(s // ROW_BLOCK,),
        in_specs=[blk, blk, pl.BlockSpec((ROW_BLOCK, LANES), lambda i: (i, OFF_Z // LANES)), mat, mat, vec, vec,
                  pl.BlockSpec(memory_space=pl.ANY)],
        out_specs=[pl.BlockSpec((ROW_BLOCK, tail), lambda i: (i, OFF_Z // tail)), mat, mat, vec, vec],
        out_shape=[jax.ShapeDtypeStruct(dproj.shape, dproj.dtype), jax.ShapeDtypeStruct((LANES, GLA_KW), F32),
                   jax.ShapeDtypeStruct((LANES, GLA_KW), F32), jax.ShapeDtypeStruct((1, GLA_KW), F32),
                   jax.ShapeDtypeStruct((1, GLA_KW), F32)],
        input_output_aliases={7: 0},
        compiler_params=_params(("arbitrary",)),
    )(dgf, dgb, proj, uf, ub, gf_b, gb_b, dproj)


def _split3(x):
    x1 = x.astype(BF16)
    r1 = x - x1.astype(F32)
    x2 = r1.astype(BF16)
    x3 = (r1 - x2.astype(F32)).astype(BF16)
    return x1, x2, x3


def _dot_exact(mask_bf, x):
    x1, x2, x3 = _split3(x)
    return _dot(mask_bf, x1) + _dot(mask_bf, x2) + _dot(mask_bf, x3)


def _chunk_masks(reverse):
    c = GLA_CHUNK
    row = lax.broadcasted_iota(jnp.int32, (c, c), 0)
    col = lax.broadcasted_iota(jnp.int32, (c, c), 1)
    allowed = (col >= row) if reverse else (col <= row)
    seen_by = (col <= row) if reverse else (col >= row)
    return allowed, seen_by


def _chunk_terms(q_ref, k_ref, g_ref, rs, hs, allowed, reverse):
    c = GLA_CHUNK
    mid, last = (c // 2, 0) if reverse else (c // 2 - 1, c - 1)
    q = q_ref[rs, hs] * (GLA_DK ** -0.5)
    k = k_ref[rs, hs]
    b = _dot_exact(jnp.where(allowed, 1.0, 0.0).astype(BF16), g_ref[rs, hs])
    bref, blast = b[mid:mid + 1, :], b[last:last + 1, :]
    e_q, e_k, e_in, e_st = jnp.exp(b - bref), jnp.exp(bref - b), jnp.exp(b), jnp.exp(blast - b)
    return dict(last=last, e_q=e_q, e_k=e_k, e_in=e_in, e_st=e_st,
                dec=jnp.exp(blast), qe=q * e_q, ke=k * e_k, qin=q * e_in, kst=k * e_st)


def _gla_blockspecs(s, reverse_order):
    cb = GLA_CHUNKS_PER_STEP
    rows = cb * GLA_CHUNK
    nsteps = s // rows

    def rb(n):
        return (nsteps - 1 - n) if reverse_order else n

    qspec = pl.BlockSpec((rows, GLA_KW), lambda n: (rb(n), OFF_GQ // GLA_KW))
    kspec = pl.BlockSpec((rows, GLA_KW), lambda n: (rb(n), OFF_GK // GLA_KW))
    vspec = pl.BlockSpec((rows, GLA_VW), lambda n: (rb(n), OFF_GV // GLA_VW))
    gspec = pl.BlockSpec((rows, GLA_KW), lambda n: (rb(n), 0))
    ospec = pl.BlockSpec((rows, GLA_VW), lambda n: (rb(n), 0))
    sspec = pl.BlockSpec((GLA_HEADS, cb, GLA_DV, GLA_DK), lambda n: (0, rb(n), 0, 0))
    return cb, rows, nsteps, qspec, kspec, vspec, gspec, ospec, sspec


def _gla_units(cb, order_reversed):
    chunks = list(reversed(range(cb))) if order_reversed else list(range(cb))
    return [(c, h, slice(c * GLA_CHUNK, (c + 1) * GLA_CHUNK), slice(h * GLA_DK, (h + 1) * GLA_DK),
             slice(h * GLA_DV, (h + 1) * GLA_DV)) for c in chunks for h in range(GLA_HEADS)]


def _gla_fwd(proj, g, reverse, name):
    s = proj.shape[0]
    cb, rows, nsteps, qspec, kspec, vspec, gspec, ospec, sspec = _gla_blockspecs(s, reverse)

    def body(q_ref, k_ref, v_ref, g_ref, o_ref, st_ref, state):
        @pl.when(pl.program_id(0) == 0)
        def _():
            state[...] = jnp.zeros_like(state)

        allowed, _ = _chunk_masks(reverse)
        units = _gla_units(cb, reverse)
        terms = [_chunk_terms(q_ref, k_ref, g_ref, rs, hs, allowed, reverse) for _, _, rs, hs, _ in units]
        vals = [v_ref[rs, vs].astype(BF16) for _, _, rs, _, vs in units]
        raw = [(_dot_nt(t["qe"].astype(BF16), t["ke"].astype(BF16)), _dot_tn(v, t["kst"].astype(BF16)))
               for t, v in zip(terms, vals)]
        intra = [_dot(jnp.where(allowed, a, 0.0).astype(BF16), v) for (a, _), v in zip(raw, vals)]
        st = [state[h] for h in range(GLA_HEADS)]
        for (c, h, rs, _, vs), t, (_, kv), o_in in zip(units, terms, raw, intra):
            st_b = st[h].astype(BF16)
            st_ref[h, c] = st_b
            o_ref[rs, vs] = o_in + _dot_nt(t["qin"].astype(BF16), st_b)
            st[h] = st[h] * t["dec"] + kv
        for h in range(GLA_HEADS):
            state[h] = st[h]

    return pl.pallas_call(
        body, name=name, grid=(nsteps,),
        in_specs=[qspec, kspec, vspec, gspec],
        out_specs=[ospec, sspec],
        out_shape=[jax.ShapeDtypeStruct((s, GLA_VW), F32),
                   jax.ShapeDtypeStruct((GLA_HEADS, s // GLA_CHUNK, GLA_DV, GLA_DK), BF16)],
        scratch_shapes=[pltpu.VMEM((GLA_HEADS, GLA_DV, GLA_DK), F32)],
        compiler_params=_params(("arbitrary",)),
    )(proj, proj, proj, g)


def _gla_bwd(proj, g, do, states, reverse, name, merge=None):
    s = proj.shape[0]
    cb, rows, nsteps, qspec, kspec, vspec, gspec, ospec, sspec = _gla_blockspecs(s, not reverse)
    gla_cols = OFF_Z - OFF_GQ

    def body(q_ref, k_ref, v_ref, g_ref, do_ref, sp_ref, *rest):
        if merge is None:
            dq_ref, dk_ref, dv_ref, dg_ref, dstate = rest
        else:
            dq_o, dk_o, dv_o, dgr_ref, _, dp_ref, dg_ref, dstate = rest
        @pl.when(pl.program_id(0) == 0)
        def _():
            dstate[...] = jnp.zeros_like(dstate)

        allowed, seen_by = _chunk_masks(reverse)
        units = _gla_units(cb, not reverse)
        terms = [_chunk_terms(q_ref, k_ref, g_ref, rs, hs, allowed, reverse) for _, _, rs, hs, _ in units]
        vals = [v_ref[rs, vs].astype(BF16) for _, _, rs, _, vs in units]
        dos = [do_ref[rs, vs] for _, _, rs, _, vs in units]
        prevs = [sp_ref[h, c] for c, h, _, _, _ in units]
        raw = [(_dot_nt(t["qe"].astype(BF16), t["ke"].astype(BF16)), _dot_nt(do, v),
                _dot(do, sp), _dot_tn(do, t["qin"].astype(BF16)))
               for t, v, do, sp in zip(terms, vals, dos, prevs)]
        inner = []
        for t, do, (a, da, _, _) in zip(terms, dos, raw):
            da = jnp.where(allowed, da, 0.0).astype(BF16)
            inner.append((_dot(da, t["ke"].astype(BF16)), _dot_tn(da, t["qe"].astype(BF16)),
                          _dot_tn(jnp.where(allowed, a, 0.0).astype(BF16), do)))
        ds = [dstate[h] for h in range(GLA_HEADS)]
        outer = []
        for (c, h, _, _, _), t, v, sp, (_, _, _, inc) in zip(units, terms, vals, prevs, raw):
            ds_b = ds[h].astype(BF16)
            outer.append((_dot(v, ds_b), _dot_nt(t["kst"].astype(BF16), ds_b),
                          jnp.sum(sp.astype(F32) * ds[h], axis=0, keepdims=True)))
            ds[h] = ds[h] * t["dec"] + inc
        for h in range(GLA_HEADS):
            dstate[h] = ds[h]
        seen_bf = jnp.where(seen_by, 1.0, 0.0).astype(BF16)
        rowi = lax.broadcasted_iota(jnp.int32, (GLA_CHUNK, GLA_DK), 0)
        for (c, h, rs, hs, vs), t, (_, _, dqin, _), (dqe, dke, dv_in), (dkst, dv_out, ddec) in zip(
                units, terms, raw, inner, outer):
            dq = (dqe * t["e_q"] + dqin * t["e_in"]) * (GLA_DK ** -0.5)
            dk = dke * t["e_k"] + dkst * t["e_st"]
            if merge is None:
                dq_ref[rs, hs], dk_ref[rs, hs], dv_ref[rs, vs] = dq, dk, dv_in + dv_out
            else:
                lo = OFF_GK - OFF_GQ + h * GLA_DK
                dp_ref[rs, hs] = (dq + dq_o[rs, hs]).astype(BF16)
                dp_ref[rs, lo:lo + GLA_DK] = (dk + dk_o[rs, hs]).astype(BF16)
                lo = OFF_GV - OFF_GQ + h * GLA_DV
                dp_ref[rs, lo:lo + GLA_DV] = (dv_in + dv_out + dv_o[rs, vs]).astype(BF16)
            kk = dkst * t["kst"]
            db = dqe * t["qe"] - dke * t["ke"] + dqin * t["qin"] - kk
            extra = jnp.sum(kk, axis=0, keepdims=True) + ddec * t["dec"]
            db = db + jnp.where(rowi == t["last"], extra, 0.0)
            dg_ref[rs, hs] = _dot_exact(seen_bf, db)
        if merge is not None:
            dp_ref[:, OFF_GR - OFF_GQ:gla_cols] = dgr_ref[...]

    scratch = [pltpu.VMEM((GLA_HEADS, GLA_DV, GLA_DK), F32)]
    if merge is None:
        return pl.pallas_call(
            body, name=name, grid=(nsteps,),
            in_specs=[qspec, kspec, vspec, gspec, ospec, sspec],
            out_specs=[gspec, gspec, ospec, gspec],
            out_shape=[jax.ShapeDtypeStruct((s, GLA_KW), F32), jax.ShapeDtypeStruct((s, GLA_KW), F32),
                       jax.ShapeDtypeStruct((s, GLA_VW), F32), jax.ShapeDtypeStruct((s, GLA_KW), F32)],
            scratch_shapes=scratch,
            compiler_params=_params(("arbitrary",)),
        )(proj, proj, proj, g, do, states)
    dproj = merge[4]
    block = gspec.index_map
    return pl.pallas_call(
        body, name=name, grid=(nsteps,),
        in_specs=[qspec, kspec, vspec, gspec, ospec, sspec, gspec, gspec, ospec, ospec, _ANY],
        out_specs=[pl.BlockSpec((rows, gla_cols), lambda n: (block(n)[0], OFF_GQ // gla_cols)), gspec],
        out_shape=[jax.ShapeDtypeStruct(dproj.shape, dproj.dtype), jax.ShapeDtypeStruct((s, GLA_KW), F32)],
        input_output_aliases={10: 0},
        scratch_shapes=scratch,
        compiler_params=_params(("arbitrary",)),
    )(proj, proj, proj, g, do, states, *merge)


def _gla_post(o_f, o_b, proj, g, cat, name="gla_post"):
    s = o_f.shape[0]

    def body(of_ref, ob_ref, gr_ref, g_ref, _, o_ref):
        gv = g_ref[...]
        for h in range(GLA_HEADS):
            sl = slice(h * GLA_DV, (h + 1) * GLA_DV)
            osum = of_ref[:, sl] + ob_ref[:, sl]
            r = lax.rsqrt(jnp.mean(osum * osum, axis=-1, keepdims=True) + EPS)
            gr = gr_ref[:, sl]
            o_ref[:, sl] = (osum * r * gv * (gr * _sigmoid(gr))).astype(BF16)

    blk = pl.BlockSpec((ROW_BLOCK, GLA_VW), lambda i: (i, 0))
    return pl.pallas_call(
        body, name=name, grid=(s // ROW_BLOCK,),
        in_specs=[blk, blk, pl.BlockSpec((ROW_BLOCK, GLA_VW), lambda i: (i, OFF_GR // GLA_VW)),
                  pl.BlockSpec((1, GLA_DV), lambda i: (0, 0)), pl.BlockSpec(memory_space=pl.ANY)],
        out_specs=pl.BlockSpec((ROW_BLOCK, GLA_VW), lambda i: (i, ATTN_W // GLA_VW)),
        out_shape=jax.ShapeDtypeStruct(cat.shape, cat.dtype),
        input_output_aliases={4: 0},
        compiler_params=_params(("parallel",)),
    )(o_f, o_b, proj, g, cat)


def _gla_post_bwd(dcat, o_f, o_b, proj, g, name="gla_post_bwd"):
    s = o_f.shape[0]

    def body(dy_ref, of_ref, ob_ref, gr_ref, g_ref, do_ref, dgr_ref, gg_ref):
        i = pl.program_id(0)
        gv = g_ref[...]
        gg = jnp.zeros((1, GLA_DV), F32)
        for h in range(GLA_HEADS):
            sl = slice(h * GLA_DV, (h + 1) * GLA_DV)
            osum = of_ref[:, sl] + ob_ref[:, sl]
            r = lax.rsqrt(jnp.mean(osum * osum, axis=-1, keepdims=True) + EPS)
            gr, dy = gr_ref[:, sl], dy_ref[:, sl]
            sg = _sigmoid(gr)
            dgr_ref[:, sl] = (dy * (osum * r * gv) * (sg * (1.0 + gr * (1.0 - sg)))).astype(BF16)
            dn = dy * (gr * sg)
            dng = dn * gv
            c = jnp.mean(dng * osum, axis=-1, keepdims=True)
            do_ref[:, sl] = (r * dng - osum * (r * r * r * c)).astype(BF16)
            gg = gg + jnp.sum(dn * osum * r, axis=0, keepdims=True)

        @pl.when(i == 0)
        def _():
            gg_ref[...] = jnp.zeros_like(gg_ref)

        gg_ref[...] += gg

    blk = pl.BlockSpec((ROW_BLOCK, GLA_VW), lambda i: (i, 0))
    vec = pl.BlockSpec((1, GLA_DV), lambda i: (0, 0))
    return pl.pallas_call(
        body, name=name, grid=(s // ROW_BLOCK,),
        in_specs=[pl.BlockSpec((ROW_BLOCK, GLA_VW), lambda i: (i, 1)), blk, blk,
                  pl.BlockSpec((ROW_BLOCK, GLA_VW), lambda i: (i, OFF_GR // GLA_VW)), vec],
        out_specs=[blk, blk, vec],
        out_shape=[jax.ShapeDtypeStruct((s, GLA_VW), BF16), jax.ShapeDtypeStruct((s, GLA_VW), BF16),
                   jax.ShapeDtypeStruct((1, GLA_DV), F32)],
        compiler_params=_params(("arbitrary",)),
    )(dcat, o_f, o_b, proj, g)


HALO = 16


def _extended(prev_ref, cur_ref, next_ref, i, s, tr, cs):
    first, last = i == 0, i == s // tr - 1
    prev = jnp.where(first, 0.0, prev_ref[:, cs].astype(F32))
    nxt = jnp.where(last, 0.0, next_ref[:, cs].astype(F32))
    return jnp.concatenate([prev, cur_ref[:, cs].astype(F32), nxt], axis=0)


FFN_ROWS = 512
FFN_COLS = 512


FFN_CHUNK = 256


def _lagged(i, ni, multiply, finish, rotate, init):
    chunks = [slice(c, c + FFN_CHUNK) for c in range(0, FFN_COLS, FFN_CHUNK)]

    @pl.when(i == 0)
    def _():
        init()

    @pl.when(i < 2)
    def _():
        rotate([multiply(cs) for cs in chunks], chunks)

    @pl.when((i >= 2) & (i < ni))
    def _():
        new = []
        for cs in chunks:
            new.append(multiply(cs))
            finish(cs)
        rotate(new, chunks)

    @pl.when(i >= ni)
    def _():
        for cs in chunks:
            finish(cs)
        rotate(None, chunks)


def _ffn_in(n2, w_gate, w_up, conv_w, conv_b, name="ffn_in"):
    s, d = n2.shape
    f = w_gate.shape[1]
    tm, tn, edge = FFN_ROWS, FFN_COLS, SUBLANES
    ni = s // tm
    ext = tm + 2 * edge

    def body(a_ref, wg_ref, wu_ref, w_ref, b_ref, gate_ref, silu_ref, slope_ref, act_ref, g_tile, u_tile, g_tail):
        i = pl.program_id(1)

        @pl.when(i == 0)
        def _():
            g_tile[...] = jnp.zeros_like(g_tile)
            u_tile[...] = jnp.zeros_like(u_tile)
            g_tail[...] = jnp.zeros_like(g_tail)

        a = a_ref[...]
        g_new = _dot(a, wg_ref[...])
        u_new = _dot(a, wu_ref[...])
        g_old, u_old = g_tile[...], u_tile[...]
        before = jnp.where(i == 1, 0.0, g_tail[...])
        after = jnp.where(i == ni, 0.0, g_new[0:edge])
        ge = jnp.concatenate([before, g_old, after], axis=0)
        w = w_ref[...]
        conv = (w[0:1] * pltpu.roll(ge, 1, 0) + w[1:2] * ge + w[2:3] * pltpu.roll(ge, ext - 1, 0))[edge:edge + tm]
        conv = conv + b_ref[...]
        sg = _sigmoid(conv)
        silu = conv * sg
        u_f = u_old.astype(F32)
        gate_ref[...] = g_old
        silu_ref[...] = silu.astype(BF16)
        slope_ref[...] = (u_f * (sg * (1.0 + conv * (1.0 - sg)))).astype(BF16)
        act_ref[...] = (silu * u_f).astype(BF16)
        g_tail[...] = g_old[tm - edge:tm]
        g_tile[...] = g_new
        u_tile[...] = u_new.astype(BF16)

    lag = pl.BlockSpec((tm, tn), lambda j, i: (jnp.maximum(i - 1, 0), j))
    return pl.pallas_call(
        body, name=name, grid=(f // tn, ni + 1),
        in_specs=[pl.BlockSpec((tm, d), lambda j, i: (jnp.minimum(i, ni - 1), 0)),
                  pl.BlockSpec((d, tn), lambda j, i: (0, j)), pl.BlockSpec((d, tn), lambda j, i: (0, j)),
                  pl.BlockSpec((3, tn), lambda j, i: (0, j)), pl.BlockSpec((1, tn), lambda j, i: (0, j))],
        out_specs=[lag, lag, lag, lag],
        out_shape=[jax.ShapeDtypeStruct((s, f), F32)] + [jax.ShapeDtypeStruct((s, f), BF16)] * 3,
        scratch_shapes=[pltpu.VMEM((tm, tn), F32), pltpu.VMEM((tm, tn), BF16), pltpu.VMEM((edge, tn), F32)],
        compiler_params=_params(("parallel", "arbitrary")),
    )(n2, w_gate, w_up, conv_w, conv_b)


def _ffn_mid_bwd(dh2, w_down, gate, silu, slope, conv_w, name="ffn_mid_bwd"):
    s, d = dh2.shape
    f = gate.shape[1]
    tm, tn = FFN_ROWS, FFN_COLS
    ni = s // tm
    ext = tm + 2 * HALO
    per, last_halo = tm // HALO, s // HALO - 1

    def body(a_ref, wd_ref, gp, gc, gn, sp, sc, sn, silu_ref, w_ref, dg_ref, du_ref, gw_ref, gb_ref,
             d_near, d_far, d_tail):
        i = pl.program_id(1)

        def multiply(cs):
            return _dot_nt(a_ref[...], wd_ref[cs, :])

        def finish(cs):
            before = jnp.where(i == 2, 0.0, d_tail[:, cs])
            after = jnp.where(i == ni + 1, 0.0, d_near[0:HALO, cs])
            d_mid = d_far[:, cs]
            de = jnp.concatenate([before, d_mid, after], axis=0)
            ge = _extended(gp, gc, gn, i - 2, s, tm, cs)
            w = w_ref[:, cs]
            g_prev, g_next = pltpu.roll(ge, 1, 0), pltpu.roll(ge, ext - 1, 0)
            inner = slice(HALO, HALO + tm)
            du_ref[:, cs] = (d_mid * silu_ref[:, cs].astype(F32)).astype(BF16)
            dconv = de * _extended(sp, sc, sn, i - 2, s, tm, cs)
            dgate = w[0:1] * pltpu.roll(dconv, ext - 1, 0) + w[1:2] * dconv + w[2:3] * pltpu.roll(dconv, 1, 0)
            dg_ref[:, cs] = dgate[inner].astype(BF16)
            dci = dconv[inner]
            gw_ref[0:1, cs] += jnp.sum(dci * g_prev[inner], axis=0, keepdims=True)
            gw_ref[1:2, cs] += jnp.sum(dci * ge[inner], axis=0, keepdims=True)
            gw_ref[2:3, cs] += jnp.sum(dci * g_next[inner], axis=0, keepdims=True)
            gb_ref[:, cs] += jnp.sum(dci, axis=0, keepdims=True)

        def rotate(new, chunks):
            d_tail[...] = d_far[tm - HALO:tm]
            d_far[...] = d_near[...]
            if new is not None:
                for cs, d_new in zip(chunks, new):
                    d_near[:, cs] = d_new

        def init():
            for r in (d_near, d_far, d_tail, gw_ref, gb_ref):
                r[...] = jnp.zeros_like(r)

        _lagged(i, ni, multiply, finish, rotate, init)

    def tile(i):
        return jnp.maximum(i - 2, 0)

    cur = pl.BlockSpec((tm, tn), lambda j, i: (tile(i), j))
    prev = pl.BlockSpec((HALO, tn), lambda j, i: (jnp.maximum(tile(i) * per - 1, 0), j))
    nxt = pl.BlockSpec((HALO, tn), lambda j, i: (jnp.minimum((tile(i) + 1) * per, last_halo), j))
    wspec = pl.BlockSpec((3, tn), lambda j, i: (0, j))
    bspec = pl.BlockSpec((1, tn), lambda j, i: (0, j))
    return pl.pallas_call(
        body, name=name, grid=(f // tn, ni + 2),
        in_specs=[pl.BlockSpec((tm, d), lambda j, i: (jnp.minimum(i, ni - 1), 0)),
                  pl.BlockSpec((tn, d), lambda j, i: (j, 0))] + [prev, cur, nxt] * 2 + [cur, wspec],
        out_specs=[cur, cur, wspec, bspec],
        out_shape=[jax.ShapeDtypeStruct((s, f), BF16), jax.ShapeDtypeStruct((s, f), BF16),
                   jax.ShapeDtypeStruct((3, f), F32), jax.ShapeDtypeStruct((1, f), F32)],
        scratch_shapes=[pltpu.VMEM((tm, tn), F32), pltpu.VMEM((tm, tn), F32), pltpu.VMEM((HALO, tn), F32)],
        compiler_params=_params(("parallel", "arbitrary")),
    )(dh2, w_down, gate, gate, gate, slope, slope, slope, silu, conv_w)


def _local_step(x, target, w, late_weights=None, grad_sink=None, first_dep=()):
    s = x.shape[0]
    tables = _rope_tables(s)
    uf, ub = _gate_matrices(w["gf_up"], w["gb_up"])
    if grad_sink is None:
        grad_sink = lambda names, grads: ()

    n1 = _rms_fwd(x, w["norm1_g"], "norm1")
    proj = _matmul([(n1, w["w_in"])], "nn", F32, 1024, 1280, D_MODEL, "in_proj", deps=first_dep)
    qkv = _rope_fwd(proj, tables)
    branches = [_attn_fwd(*qkv[di], d, f"attn_fwd_d{d}") for di, d in enumerate(DILATIONS)]
    o_mix, ao, lse = _attn_combine([b[0] for b in branches], [b[1] for b in branches], w["attn_norm_g"])
    g_f, g_b = _gla_gates(proj, uf, ub, w["gf_b"], w["gb_b"])
    o_f, st_f = _gla_fwd(proj, g_f, False, "gla_fwd_f")
    o_b, st_b = _gla_fwd(proj, g_b, True, "gla_fwd_b")
    cat = _gla_post(o_f, o_b, proj, w["gla_norm_g"], ao)
    if late_weights is not None:
        w = {**w, **late_weights("mixer", cat)}
    h1 = _matmul([(cat, w["w_out"])], "nn", F32, 512, 1024, D_MODEL, "out_proj", res=x)
    n2 = _rms_fwd(h1, w["norm2_g"], "norm2")
    if late_weights is not None:
        w = {**w, **late_weights("ffn", n2)}
    gate, silu, slope, act = _ffn_in(n2, w["w_gate"], w["w_up"], w["conv_w"], w["conv_b"])
    h2 = _matmul([(act, w["w_down"])], "nn", F32, 1024, 1024, 2816, "ffn_down", res=h1)
    dh2, dh2_b, loss_acc, g_final = _final_loss(h2, target, w["final_norm_g"])

    g_w_down = _matmul([(act, dh2_b)], "tn", BF16, 1408, 1024, 2048, "g_w_down")
    dep = grad_sink(["w_down"], [g_w_down])
    dgate, dup, g_conv_w, g_conv_b = _ffn_mid_bwd(dh2_b, w["w_down"], gate, silu, slope, w["conv_w"])
    g_w_gate = _matmul([(n2, dgate)], "tn", BF16, 2048, 512, 2048, "g_w_gate", deps=dep)
    g_w_up = _matmul([(n2, dup)], "tn", BF16, 2048, 512, 2048, "g_w_up")
    dep = grad_sink(["w_gate", "w_up"], [g_w_gate, g_w_up])
    dn2 = _matmul([(dgate, w["w_gate"])], "nt", F32, 1024, 1024, 2816, "d_n2_gate", deps=dep)
    dn2 = _matmul([(dup, w["w_up"])], "nt", F32, 1024, 1024, 2816, "d_n2_up", res=dn2)
    dh1, dh1_b, g_norm2 = _rms_bwd(dn2, h1, w["norm2_g"], dh2, "norm2_bwd")

    g_w_out = _matmul([(cat, dh1_b)], "tn", BF16, 1024, 1024, 2048, "g_w_out")
    dep = grad_sink(["w_out"], [g_w_out])
    dcat = _matmul([(dh1_b, w["w_out"])], "nt", F32, 512, 1024, D_MODEL, "d_cat", deps=dep)
    do_attn, delta, g_attn_norm = _attn_prebwd(dcat, o_mix, w["attn_norm_g"])
    grads = [_attn_bwd(*qkv[di], do_attn[di], lse[di], delta[di], d, f"attn_bwd_d{d}")
             for di, d in enumerate(DILATIONS)]
    dproj = _rope_bwd(grads, tables)
    do_gla, dgr, g_gla_norm = _gla_post_bwd(dcat, o_f, o_b, proj, w["gla_norm_g"])
    dq_f, dk_f, dv_f, dg_f = _gla_bwd(proj, g_f, do_gla, st_f, False, "gla_bwd_f")
    dproj, dg_b = _gla_bwd(proj, g_b, do_gla, st_b, True, "gla_bwd_b", merge=(dq_f, dk_f, dv_f, dgr, dproj))
    dproj, g_uf, g_ub, g_gf_b, g_gb_b = _gla_gates_bwd(dg_f, dg_b, proj, uf, ub, w["gf_b"], w["gb_b"], dproj)
    g_w_in = _matmul([(n1, dproj)], "tn", BF16, 1024, 1280, 2048, "g_w_in")
    dep = grad_sink(["w_in"], [g_w_in])
    dn1 = _matmul([(dproj, w["w_in"])], "nt", F32, 1024, 2048, 1280, "d_n1", deps=dep)
    grad_x, g_norm1 = _rms_bwd(dn1, x, w["norm1_g"], dh1, "norm1_bwd", bf16_copy=False)

    g = dict(norm1_g=g_norm1, w_in=g_w_in, gf_up=g_uf[:GLA_RANK], gf_b=g_gf_b,
             gb_up=g_ub[GLA_RANK:2 * GLA_RANK], gb_b=g_gb_b, gla_norm_g=g_gla_norm, attn_norm_g=g_attn_norm,
             w_out=g_w_out, norm2_g=g_norm2, w_gate=g_w_gate, w_up=g_w_up, conv_w=g_conv_w, conv_b=g_conv_b,
             w_down=g_w_down, final_norm_g=g_final)
    return loss_acc, grad_x, g


def _me_and_peers():
    x, y, c = lax.axis_index("x"), lax.axis_index("y"), lax.axis_index("c")
    me = 4 * x + 2 * y + c
    peers = []
    for kbits in range(1, N_DEV):
        px, py, pc = x ^ (kbits >> 2 & 1), y ^ (kbits >> 1 & 1), c ^ (kbits & 1)
        peers.append(((px, py, pc), 4 * px + 2 * py + pc))
    return me, peers


_HBM = pl.BlockSpec(memory_space=pltpu.HBM)
_SEM = pl.BlockSpec(memory_space=pltpu.SEMAPHORE)
_ANY = pl.BlockSpec(memory_space=pl.ANY)
_EFFECT = pltpu.SideEffectType.DATAFLOW_SIDE_EFFECTING


def _exchange_copies(src_refs, land_refs, send_sems, recv_sems, scatter):
    me, peers = _me_and_peers()
    out = []
    for a, (src, land) in enumerate(zip(src_refs, land_refs)):
        for kk, (dev, idx) in enumerate(peers):
            out.append(pltpu.make_async_remote_copy(
                src_ref=src.at[idx] if scatter else src, dst_ref=land.at[me],
                send_sem=send_sems.at[a * (N_DEV - 1) + kk], recv_sem=recv_sems.at[a * (N_DEV - 1) + kk],
                device_id=dev, device_id_type=MESH_ID))
    return out


def _exchange_start(srcs, lands, scatter, name, deps=()):
    n, nd = len(srcs), len(deps)

    def body(*refs):
        src_refs, land_refs = refs[:n], refs[n:2 * n]
        send_sems, recv_sems = refs[2 * n + nd:2 * n + nd + 2]
        token = refs[-1]
        for cp in _exchange_copies(src_refs, land_refs, send_sems, recv_sems, scatter):
            cp.start()
        token[...] = jnp.zeros_like(token)

    outs = pl.pallas_call(
        body, name=name,
        in_specs=[_HBM] * (2 * n) + [_ANY] * nd,
        out_specs=[_SEM, _SEM] + [_HBM] * (2 * n) + [pl.BlockSpec(memory_space=pltpu.VMEM)],
        out_shape=[pltpu.SemaphoreType.DMA((n * (N_DEV - 1),)), pltpu.SemaphoreType.DMA((n * (N_DEV - 1),))]
        + [pltpu.HBM(t.shape, t.dtype) for t in srcs] + [pltpu.HBM(t.shape, t.dtype) for t in lands]
        + [jax.ShapeDtypeStruct((SUBLANES, LANES), F32)],
        input_output_aliases={i: 2 + i for i in range(2 * n)},
        compiler_params=pltpu.CompilerParams(has_side_effects=_EFFECT),
    )(*[pltpu.with_memory_space_constraint(t, pltpu.HBM) for t in list(srcs) + list(lands)], *deps)
    send_sems, recv_sems = outs[0], outs[1]
    return dict(send=send_sems, recv=recv_sems, srcs=outs[2:2 + n], lands=outs[2 + n:2 + 2 * n],
                scatter=scatter, token=outs[-1])


def _exchange_wait(started, name, after):
    n = len(started["srcs"])
    scatter = started["scatter"]

    def body(*refs):
        src_refs, land_refs = refs[:n], refs[n:2 * n]
        send_sems, recv_sems = refs[2 * n], refs[2 * n + 1]
        for cp in _exchange_copies(src_refs, land_refs, send_sems, recv_sems, scatter):
            cp.wait_send()
            cp.wait_recv()

    outs = pl.pallas_call(
        body, name=name,
        in_specs=[_HBM] * (2 * n) + [_SEM, _SEM, _ANY],
        out_specs=[_HBM] * (2 * n),
        out_shape=[pltpu.HBM(t.shape, t.dtype) for t in started["srcs"]]
        + [pltpu.HBM(t.shape, t.dtype) for t in started["lands"]],
        input_output_aliases={i: i for i in range(2 * n)},
        compiler_params=pltpu.CompilerParams(has_side_effects=_EFFECT),
    )(*started["srcs"], *started["lands"], started["send"], started["recv"], after)
    return outs[:n], outs[n:]


def _all_gather_two_level(shard, name):
    def body(x_ref, out_ref, send_sems, recv_sems, local_sem):
        x, y, c = lax.axis_index("x"), lax.axis_index("y"), lax.axis_index("c")
        me, sibling = (x, y, c), (x, y, 1 - c)
        chips = [(1 - x, y), (x, 1 - y), (1 - x, 1 - y)]

        def slot(px, py, pc):
            return out_ref.at[4 * px + 2 * py + pc]

        def copy(k, block, to, src=None):
            return pltpu.make_async_remote_copy(
                src_ref=slot(*block) if src is None else src, dst_ref=slot(*block),
                send_sem=send_sems.at[k], recv_sem=recv_sems.at[k], device_id=to, device_id_type=MESH_ID)

        mine = pltpu.make_async_copy(x_ref, slot(*me), local_sem)
        mine.start()
        first = [copy(0, me, sibling, src=x_ref)]
        first += [copy(1 + j, me, (*chip, c), src=x_ref) for j, chip in enumerate(chips)]
        for cp in first:
            cp.start()
        passed = [copy(4 + j, (*chip, c), sibling) for j, chip in enumerate(chips)]
        for j, chip in enumerate(chips):
            copy(1 + j, (*chip, c), me).wait_recv()
            passed[j].start()
        copy(0, sibling, me).wait_recv()
        for j, chip in enumerate(chips):
            copy(4 + j, (*chip, 1 - c), me).wait_recv()
        for cp in first + passed:
            cp.wait_send()
        mine.wait()

    return pl.pallas_call(
        body, name=name,
        in_specs=[_ANY], out_specs=_ANY,
        out_shape=jax.ShapeDtypeStruct((N_DEV,) + shard.shape, shard.dtype),
        scratch_shapes=[pltpu.SemaphoreType.DMA((N_DEV - 1,)), pltpu.SemaphoreType.DMA((N_DEV - 1,)),
                        pltpu.SemaphoreType.DMA],
    )(shard)


def _all_gather_vmem(vec, name):
    r = vec.shape[0]

    def body(v_ref, o_ref, send_sems, recv_sems):
        me, peers = _me_and_peers()
        o_ref[me] = v_ref[...]
        sends = []
        for kk, (dev, _) in enumerate(peers):
            cp = pltpu.make_async_remote_copy(
                src_ref=v_ref, dst_ref=o_ref.at[me],
                send_sem=send_sems.at[kk], recv_sem=recv_sems.at[kk],
                device_id=dev, device_id_type=MESH_ID)
            cp.start()
            sends.append(cp)
        for kk, (dev, idx) in enumerate(peers):
            pltpu.make_async_remote_copy(
                src_ref=v_ref, dst_ref=o_ref.at[idx],
                send_sem=send_sems.at[kk], recv_sem=recv_sems.at[kk],
                device_id=dev, device_id_type=MESH_ID).wait_recv()
        for cp in sends:
            cp.wait_send()

    return pl.pallas_call(
        body, name=name,
        in_specs=[pl.BlockSpec(memory_space=pltpu.VMEM)],
        out_specs=pl.BlockSpec(memory_space=pltpu.VMEM),
        out_shape=jax.ShapeDtypeStruct((N_DEV, r, LANES), F32),
        scratch_shapes=[pltpu.SemaphoreType.DMA((N_DEV - 1,)), pltpu.SemaphoreType.DMA((N_DEV - 1,))],
        compiler_params=pltpu.CompilerParams(vmem_limit_bytes=VMEM_LIMIT),
    )(vec)


def _adamw_math(w, g, m, v):
    m = ADAM_B1 * m + (1.0 - ADAM_B1) * g
    v = ADAM_B2 * v + (1.0 - ADAM_B2) * (g * g)
    m_hat = m / (1.0 - ADAM_B1 ** ADAM_STEP)
    v_hat = v / (1.0 - ADAM_B2 ** ADAM_STEP)
    delta = -ADAM_LR * (m_hat / (jnp.sqrt(v_hat) + ADAM_EPS) + ADAM_WD * w)
    return delta, m, v


def _adamw_sum(parts, w, m, v, tr, name, own=None, me=None):
    r, c = w.shape

    def body(*refs):
        if own is None:
            p_ref, w_ref, m_ref, v_ref, g_ref, d_ref, nm_ref, nv_ref = refs
            terms = [p_ref[kk] for kk in range(N_DEV)]
        else:
            me_ref, p_ref, own_ref, w_ref, m_ref, v_ref, g_ref, d_ref, nm_ref, nv_ref = refs
            terms = [jnp.where(me_ref[0] == kk, own_ref[0], p_ref[kk]).astype(F32) for kk in range(N_DEV)]
        g = terms[0]
        for t in terms[1:]:
            g = g + t
        g_ref[...] = g
        d_ref[...], nm_ref[...], nv_ref[...] = _adamw_math(w_ref[...], g, m_ref[...], v_ref[...])

    out_shape = [jax.ShapeDtypeStruct((r, c), F32)] * 4
    if own is None:
        blk = pl.BlockSpec((tr, c), lambda i: (i, 0))
        return pl.pallas_call(
            body, name=name, grid=(r // tr,),
            in_specs=[pl.BlockSpec((N_DEV, tr, c), lambda i: (0, i, 0)), blk, blk, blk],
            out_specs=[blk] * 4, out_shape=out_shape,
            compiler_params=_params(("parallel",)),
        )(parts, w, m, v)
    blk = pl.BlockSpec((tr, c), lambda i, me_ref: (i, 0))
    return pl.pallas_call(
        body, name=name,
        grid_spec=pltpu.PrefetchScalarGridSpec(
            num_scalar_prefetch=1, grid=(r // tr,),
            in_specs=[pl.BlockSpec((N_DEV, tr, c), lambda i, me_ref: (0, i, 0)),
                      pl.BlockSpec((1, tr, c), lambda i, me_ref: (me_ref[0], i, 0)), blk, blk, blk],
            out_specs=[blk] * 4),
        out_shape=out_shape,
        compiler_params=_params(("parallel",)),
    )(jnp.reshape(me, (1,)).astype(jnp.int32), parts, own, w, m, v)


def _slabs_to_wide(slabs, width, name):
    n, r, c = slabs.shape

    def body(i_ref, o_ref):
        for k in range(n):
            o_ref[:, c * k:c * (k + 1)] = i_ref[k]
        if width > n * c:
            o_ref[:, n * c:width] = jnp.zeros((ROW_BLOCK, width - n * c), o_ref.dtype)

    return pl.pallas_call(
        body, name=name, grid=(r // ROW_BLOCK,),
        in_specs=[pl.BlockSpec((n, ROW_BLOCK, c), lambda i: (0, i, 0))],
        out_specs=pl.BlockSpec((ROW_BLOCK, width), lambda i: (i, 0)),
        out_shape=jax.ShapeDtypeStruct((r, width), slabs.dtype),
        compiler_params=_params(("parallel",)),
    )(slabs)


def _wide_to_slabs(wide, c, name):
    r, width = wide.shape

    def body(i_ref, o_ref):
        for k in range(N_DEV):
            o_ref[k] = i_ref[:, c * k:c * (k + 1)]

    return pl.pallas_call(
        body, name=name, grid=(r // ROW_BLOCK,),
        in_specs=[pl.BlockSpec((ROW_BLOCK, width), lambda i: (i, 0))],
        out_specs=pl.BlockSpec((N_DEV, ROW_BLOCK, c), lambda i: (0, i, 0)),
        out_shape=jax.ShapeDtypeStruct((N_DEV, r, c), wide.dtype),
        compiler_params=_params(("parallel",)),
    )(wide)


_SMALL = ("norm1_g", "gf_b", "gb_b", "gla_norm_g", "attn_norm_g", "norm2_g", "conv_b", "final_norm_g",
          "gf_up", "gb_up", "conv_w")


def _pack(named):
    flat = jnp.concatenate([jnp.ravel(t).astype(F32) for t in named])
    tile = SUBLANES * LANES
    total = -(-flat.shape[0] // tile) * tile
    return jnp.pad(flat, (0, total - flat.shape[0])).reshape(total // LANES, LANES)


def _unpack(packed, shapes):
    flat = packed.reshape(-1)
    out, off = [], 0
    for shp in shapes:
        size = int(np.prod(shp))
        out.append(flat[off:off + size].reshape(shp))
        off += size
    return out


def kernel(x, norm1_g, w_in, gf_up, gf_b, gb_up, gb_b, gla_norm_g, attn_norm_g, w_out, norm2_g, w_gate, w_up, conv_w, conv_b, w_down, final_norm_g, loss_target, m_norm1_g, m_w_in, m_gf_up, m_gf_b, m_gb_up, m_gb_b, m_gla_norm_g, m_attn_norm_g, m_w_out, m_norm2_g, m_w_gate, m_w_up, m_conv_w, m_conv_b, m_w_down, m_final_norm_g, v_norm1_g, v_w_in, v_gf_up, v_gf_b, v_gb_up, v_gb_b, v_gla_norm_g, v_attn_norm_g, v_w_out, v_norm2_g, v_w_gate, v_w_up, v_conv_w, v_conv_b, v_w_down, v_final_norm_g):
    names = ("norm1_g", "w_in", "gf_up", "gf_b", "gb_up", "gb_b", "gla_norm_g", "attn_norm_g", "w_out", "norm2_g",
             "w_gate", "w_up", "conv_w", "conv_b", "w_down", "final_norm_g")
    ws = dict(zip(names, (norm1_g, w_in, gf_up, gf_b, gb_up, gb_b, gla_norm_g, attn_norm_g, w_out, norm2_g,
                          w_gate, w_up, conv_w, conv_b, w_down, final_norm_g)))
    ms = dict(zip(names, (m_norm1_g, m_w_in, m_gf_up, m_gf_b, m_gb_up, m_gb_b, m_gla_norm_g, m_attn_norm_g, m_w_out,
                          m_norm2_g, m_w_gate, m_w_up, m_conv_w, m_conv_b, m_w_down, m_final_norm_g)))
    vs = dict(zip(names, (v_norm1_g, v_w_in, v_gf_up, v_gf_b, v_gb_up, v_gb_b, v_gla_norm_g, v_attn_norm_g, v_w_out,
                          v_norm2_g, v_w_gate, v_w_up, v_conv_w, v_conv_b, v_w_down, v_final_norm_g)))
    me = 4 * lax.axis_index("x") + 2 * lax.axis_index("y") + lax.axis_index("c")
    big = ("w_in", "w_out", "w_gate", "w_up", "w_down")
    col_sharded = ("w_in", "w_gate", "w_up")

    def gather_start(group, name, deps=()):
        shards = [ws[n][0].astype(BF16) for n in group]
        lands = [lax.empty((N_DEV,) + t.shape, BF16) for t in shards]
        return _exchange_start(shards, lands, False, name, deps)

    def gather_finish(group, started, name, after):
        full = {}
        for n, own, t in zip(group, *_exchange_wait(started, name, after)):
            t = lax.dynamic_update_slice(t, own[None], (me, 0, 0))
            if n in col_sharded:
                full[n] = _slabs_to_wide(t, N_DEV * t.shape[2], "widen_" + n)
            else:
                full[n] = t.reshape(N_DEV * t.shape[1], t.shape[2])
        return full

    w_in_all = _all_gather_two_level(ws["w_in"][0].astype(BF16), "gather_w_in")
    full = {"w_in": _slabs_to_wide(w_in_all, IN_PAD, "widen_w_in")}
    late = {"mixer": ("w_out",), "ffn": ("w_gate", "w_up", "w_down")}
    started_late = {"mixer": gather_start(late["mixer"], "gather_w_out_start", deps=(full["w_in"],))}
    started_late["ffn"] = gather_start(late["ffn"], "gather_ffn_start", deps=(started_late["mixer"]["token"],))

    def late_weights(part, after):
        return gather_finish(late[part], started_late[part], "gather_" + part + "_wait", after)

    small_sharded = ("gf_up", "gb_up", "conv_w")
    sm = _all_gather_vmem(_pack([ws[n][0] for n in small_sharded]), "gather_small")
    shard_shapes = [ws[n][0].shape for n in small_sharded]
    per_dev = [_unpack(sm[d], shard_shapes) for d in range(N_DEV)]
    for i, n in enumerate(small_sharded):
        full[n] = jnp.concatenate([per_dev[d][i] for d in range(N_DEV)], axis=1)
    for n in ("norm1_g", "gf_b", "gb_b", "gla_norm_g", "attn_norm_g", "norm2_g", "conv_b"):
        full[n] = ws[n]
    full["final_norm_g"] = final_norm_g.reshape(1, D_MODEL)

    in_flight = []

    def grad_sink(group, grads):
        partials = []
        for n, t in zip(group, grads):
            t = t.astype(BF16)
            if n in col_sharded:
                t = _wide_to_slabs(t, ws[n].shape[2], "slabs_" + n)
            else:
                t = t.reshape(N_DEV, t.shape[0] // N_DEV, t.shape[1])
            partials.append(t)
        lands = [lax.empty(t.shape, t.dtype) for t in partials]
        started = _exchange_start(partials, lands, True, "exchange_" + "_".join(group) + "_start")
        in_flight.append((group, started))
        return (started["token"],)

    loss_acc, grad_x, g = _local_step(x[0], loss_target[0], full, late_weights, grad_sink,
                                      first_dep=(started_late["ffn"]["token"],))

    out = {}
    for group, started in in_flight:
        sent, landed = _exchange_wait(started, "exchange_" + "_".join(group) + "_wait", grad_x)
        for n, parts, own in zip(group, landed, sent):
            out[n] = _adamw_sum(parts, ws[n][0], ms[n][0], vs[n][0], 64, "adamw_" + n, own=own, me=me)

    small_full_shapes = [g[n].shape for n in _SMALL]
    gsmall = _pack([g[n] for n in _SMALL] + [loss_acc[0:1, 0:1]])
    gathered_small = _all_gather_vmem(gsmall, "gather_small_grads")

    def full_small(d):
        parts = []
        for n in _SMALL:
            t = d[n].reshape(d[n].shape[-2:]) if d[n].ndim == 3 else d[n].reshape(1, -1)
            if n in small_sharded:
                wide = jnp.zeros((t.shape[0], t.shape[1] * N_DEV), F32)
                t = lax.dynamic_update_slice_in_dim(wide, t, me * t.shape[1], axis=1)
            parts.append(t)
        return _pack(parts + [jnp.zeros((1, 1), F32)])

    rows = gsmall.shape[0]
    res_small = _adamw_sum(gathered_small, full_small(ws), full_small(ms), full_small(vs), rows, "adamw_small")
    loss = res_small[0].reshape(-1)[sum(int(np.prod(sh)) for sh in small_full_shapes)]
    unpacked = [_unpack(t, small_full_shapes) for t in res_small]
    for i, n in enumerate(_SMALL):
        vals = [u[i] for u in unpacked]
        if n in small_sharded:
            width = vals[0].shape[1] // N_DEV
            vals = [lax.dynamic_slice_in_dim(t, me * width, width, axis=1) for t in vals]
        out[n] = vals

    result = [loss, grad_x[None]]
    for kind in range(4):
        for n in names:
            result.append(out[n][kind].reshape(ws[n].shape))
    return tuple(result)
```

```python
import functools

import numpy as np
import jax
import jax.numpy as jnp
from jax import lax
from jax.experimental import pallas as pl
from jax.experimental.pallas import tpu as pltpu

F32 = jnp.float32
BF16 = jnp.bfloat16

D_MODEL = 2048
ATTN_W = 1024
ATTN_HEADS = 8
HEAD_DIM = 128
ROPE_DIM = 32
ROPE_THETA = 500000.0
DILATIONS = (1, 4, 16)
N_SIDE = 64
GLA_KW = 512
GLA_VW = 1024
GLA_HEADS = 4
GLA_DK = 128
GLA_DV = 256
GLA_RANK = 16
GLA_GATE_NORM = 16.0
GLA_CHUNK = 64
IN_WIDTH = 6176
IN_PAD = 6400
D_FF = 5632
EPS = 1e-6
N_DEV = 8

OFF_AQ, OFF_AK, OFF_AV = 0, 1024, 2048
OFF_GQ, OFF_GK, OFF_GV, OFF_GR, OFF_Z = 3072, 3584, 4096, 5120, 6144

ADAM_LR, ADAM_B1, ADAM_B2, ADAM_EPS, ADAM_WD, ADAM_STEP = 0.001, 0.9, 0.999, 1e-08, 0.01, 10

LANES = 128
SUBLANES = 8
VMEM_LIMIT = 56 * 1024 * 1024
ROW_BLOCK = 256
ATTN_BLOCK = 128
GLA_CHUNKS_PER_STEP = 4
NEG = -1e30
MESH_ID = pl.DeviceIdType.MESH


def _params(sem):
    return pltpu.CompilerParams(dimension_semantics=sem, vmem_limit_bytes=VMEM_LIMIT)


def _dot(a, b):
    return lax.dot_general(a, b, (((1,), (0,)), ((), ())), preferred_element_type=F32)


def _dot_nt(a, b):
    return lax.dot_general(a, b, (((1,), (1,)), ((), ())), preferred_element_type=F32)


def _dot_tn(a, b):
    return lax.dot_general(a, b, (((0,), (0,)), ((), ())), preferred_element_type=F32)


def _sigmoid(x):
    return 0.5 * jnp.tanh(0.5 * x) + 0.5


def _matmul(pairs, mode, out_dtype, tm, tn, tk, name, res=None, deps=()):
    a0, b0 = pairs[0]
    if mode == "nn":
        (m, kdim), n = a0.shape, b0.shape[1]
    elif mode == "nt":
        (m, kdim), n = a0.shape, b0.shape[0]
    else:
        (kdim, m), n = a0.shape, b0.shape[1]
    assert m % tm == 0 and n % tn == 0 and kdim % tk == 0, (name, m, n, kdim)
    nk = kdim // tk
    npairs = len(pairs)
    steps = nk * npairs
    dot = {"nn": _dot, "nt": _dot_nt, "tn": _dot_tn}[mode]

    def kidx(p):
        return lambda k: jnp.clip(k - p * nk, 0, nk - 1)

    in_specs, args = [], []
    for p, (a, b) in enumerate(pairs):
        kk = kidx(p)
        if mode == "nn":
            in_specs += [pl.BlockSpec((tm, tk), lambda i, j, k, kk=kk: (i, kk(k))),
                         pl.BlockSpec((tk, tn), lambda i, j, k, kk=kk: (kk(k), j))]
        elif mode == "nt":
            in_specs += [pl.BlockSpec((tm, tk), lambda i, j, k, kk=kk: (i, kk(k))),
                         pl.BlockSpec((tn, tk), lambda i, j, k, kk=kk: (j, kk(k)))]
        else:
            in_specs += [pl.BlockSpec((tk, tm), lambda i, j, k, kk=kk: (kk(k), i)),
                         pl.BlockSpec((tk, tn), lambda i, j, k, kk=kk: (kk(k), j))]
        args += [a, b]
    if res is not None:
        in_specs.append(pl.BlockSpec((tm, tn), lambda i, j, k: (i, j)))
        args.append(res)
    in_specs += [pl.BlockSpec(memory_space=pl.ANY)] * len(deps)
    args += list(deps)

    def body(*refs):
        ab = refs[:2 * npairs]
        res_ref = refs[2 * npairs] if res is not None else None
        o_ref = refs[2 * npairs + (1 if res is not None else 0) + len(deps)]

        def finish(acc):
            if res_ref is not None:
                acc = acc + res_ref[...]
            o_ref[...] = acc.astype(out_dtype)

        if steps == 1:
            finish(dot(ab[0][...], ab[1][...]))
            return
        acc_ref = refs[-1]
        k = pl.program_id(2)

        @pl.when(k == 0)
        def _():
            acc_ref[...] = jnp.zeros_like(acc_ref)

        for p in range(npairs):
            @pl.when((k >= p * nk) & (k < (p + 1) * nk))
            def _(p=p):
                acc_ref[...] += dot(ab[2 * p][...], ab[2 * p + 1][...])

        @pl.when(k == steps - 1)
        def _():
            finish(acc_ref[...])

    return pl.pallas_call(
        body, name=name,
        grid=(m // tm, n // tn, steps),
        in_specs=in_specs,
        out_specs=pl.BlockSpec((tm, tn), lambda i, j, k: (i, j)),
        out_shape=jax.ShapeDtypeStruct((m, n), out_dtype),
        scratch_shapes=[] if steps == 1 else [pltpu.VMEM((tm, tn), F32)],
        compiler_params=_params(("parallel", "parallel", "arbitrary")),
    )(*args)


def _rms_fwd(x, g, name):
    s, d = x.shape

    def body(x_ref, g_ref, o_ref):
        xv = x_ref[...]
        r = lax.rsqrt(jnp.mean(xv * xv, axis=-1, keepdims=True) + EPS)
        o_ref[...] = (xv * r * g_ref[...]).astype(BF16)

    return pl.pallas_call(
        body, name=name, grid=(s // ROW_BLOCK,),
        in_specs=[pl.BlockSpec((ROW_BLOCK, d), lambda i: (i, 0)), pl.BlockSpec((1, d), lambda i: (0, 0))],
        out_specs=pl.BlockSpec((ROW_BLOCK, d), lambda i: (i, 0)),
        out_shape=jax.ShapeDtypeStruct((s, d), BF16),
        compiler_params=_params(("parallel",)),
    )(x, g)


def _rms_bwd(dn, x, g, dres, name, bf16_copy=True):
    s, d = x.shape

    def body(dn_ref, x_ref, g_ref, dres_ref, dx_ref, *rest):
        gg_ref = rest[-1]
        i = pl.program_id(0)
        xv, dnv = x_ref[...], dn_ref[...]
        r = lax.rsqrt(jnp.mean(xv * xv, axis=-1, keepdims=True) + EPS)
        dng = dnv * g_ref[...]
        c = jnp.mean(dng * xv, axis=-1, keepdims=True)
        dx = dres_ref[...] + r * dng - xv * (r * r * r * c)
        dx_ref[...] = dx
        if bf16_copy:
            rest[0][...] = dx.astype(BF16)

        @pl.when(i == 0)
        def _():
            gg_ref[...] = jnp.zeros_like(gg_ref)

        gg_ref[...] += jnp.sum(dnv * xv * r, axis=0, keepdims=True)

    row = pl.BlockSpec((ROW_BLOCK, d), lambda i: (i, 0))
    vec = pl.BlockSpec((1, d), lambda i: (0, 0))
    return pl.pallas_call(
        body, name=name, grid=(s // ROW_BLOCK,),
        in_specs=[row, row, vec, row],
        out_specs=[row] + [row] * bf16_copy + [vec],
        out_shape=[jax.ShapeDtypeStruct((s, d), F32)] + [jax.ShapeDtypeStruct((s, d), BF16)] * bf16_copy
        + [jax.ShapeDtypeStruct((1, d), F32)],
        compiler_params=_params(("arbitrary",)),
    )(dn, x, g, dres)


def _final_loss(h2, target, g, name="final_loss"):
    s, d = h2.shape

    def body(h_ref, t_ref, g_ref, dh_ref, dhb_ref, loss_ref, gg_ref):
        i = pl.program_id(0)
        hv, gv = h_ref[...], g_ref[...]
        r = lax.rsqrt(jnp.mean(hv * hv, axis=-1, keepdims=True) + EPS)
        e = hv * r * gv - t_ref[...]
        dy = e * (1.0 / d)
        dyg = dy * gv
        c = jnp.mean(dyg * hv, axis=-1, keepdims=True)
        dh = r * dyg - hv * (r * r * r * c)
        dh_ref[...] = dh
        dhb_ref[...] = dh.astype(BF16)

        @pl.when(i == 0)
        def _():
            gg_ref[...] = jnp.zeros_like(gg_ref)
            loss_ref[...] = jnp.zeros_like(loss_ref)

        gg_ref[...] += jnp.sum(dy * hv * r, axis=0, keepdims=True)
        loss_ref[...] += jnp.sum(jnp.sum(e * e, axis=-1, keepdims=True), axis=0, keepdims=True) * (0.5 / d)

    row = pl.BlockSpec((ROW_BLOCK, d), lambda i: (i, 0))
    vec = pl.BlockSpec((1, d), lambda i: (0, 0))
    return pl.pallas_call(
        body, name=name, grid=(s // ROW_BLOCK,),
        in_specs=[row, row, vec],
        out_specs=[row, row, pl.BlockSpec((SUBLANES, LANES), lambda i: (0, 0)), vec],
        out_shape=[jax.ShapeDtypeStruct((s, d), F32), jax.ShapeDtypeStruct((s, d), BF16),
                   jax.ShapeDtypeStruct((SUBLANES, LANES), F32), jax.ShapeDtypeStruct((1, d), F32)],
        compiler_params=_params(("arbitrary",)),
    )(h2, target, g)


def _rope_tables(s):
    pos = jnp.arange(s, dtype=F32)
    inv_freq = ROPE_THETA ** (-jnp.arange(0, ROPE_DIM, 2, dtype=F32) / ROPE_DIM)
    ang = pos[:, None] * inv_freq[None, :]
    cos, sin = jnp.cos(ang), jnp.sin(ang)
    half = ROPE_DIM // 2
    rest = HEAD_DIM - ROPE_DIM
    c = jnp.concatenate([cos, cos, jnp.ones((s, rest), F32)], axis=1)
    sm = jnp.concatenate([-sin, jnp.zeros((s, half + rest), F32)], axis=1)
    sp = jnp.concatenate([jnp.zeros((s, half), F32), sin, jnp.zeros((s, rest), F32)], axis=1)
    return c, sm, sp


def _res_shape(s, groups, dil, dtype):
    return jax.ShapeDtypeStruct((s // dil, dil * groups * LANES), dtype)


def _res_spec(groups, dil):
    return pl.BlockSpec((ROW_BLOCK // dil, dil * groups * LANES), lambda i: (i, 0))


def _to_residues(scr, o_ref, dil):
    groups, rows = scr.shape[0], ROW_BLOCK // dil
    for r in range(dil):
        for h in range(groups):
            piece = scr[h] if dil == 1 else scr.at[h][pl.ds(r, rows, stride=dil), :]
            o_ref[:, (r * groups + h) * LANES:(r * groups + h + 1) * LANES] = piece.astype(o_ref.dtype)


def _from_residues(i_ref, scr, dil):
    groups, rows = scr.shape[0], ROW_BLOCK // dil
    for r in range(dil):
        for h in range(groups):
            piece = i_ref[:, (r * groups + h) * LANES:(r * groups + h + 1) * LANES].astype(F32)
            if dil == 1:
                scr[h] = piece
            else:
                scr.at[h][pl.ds(r, rows, stride=dil), :] = piece


def _rope_fwd(proj, tables, name="rope_fwd"):
    s = proj.shape[0]
    half = ROPE_DIM // 2
    nd = len(DILATIONS)

    def body(p_ref, c_ref, sm_ref, sp_ref, *rest):
        outs, scr = rest[:3 * nd], rest[3 * nd]
        c, sm, sp = c_ref[...], sm_ref[...], sp_ref[...]
        for gi, off in enumerate((OFF_AQ, OFF_AK, OFF_AV)):
            for h in range(ATTN_HEADS):
                t = p_ref[:, off + h * HEAD_DIM: off + (h + 1) * HEAD_DIM]
                if off != OFF_AV:
                    t = t * c + pltpu.roll(t, HEAD_DIM - half, 1) * sm + pltpu.roll(t, half, 1) * sp
                scr[h] = t
            for di, dil in enumerate(DILATIONS):
                _to_residues(scr, outs[3 * di + gi], dil)

    tab = pl.BlockSpec((ROW_BLOCK, HEAD_DIM), lambda i: (i, 0))
    outs = pl.pallas_call(
        body, name=name, grid=(s // ROW_BLOCK,),
        in_specs=[pl.BlockSpec((ROW_BLOCK, 3 * ATTN_W), lambda i: (i, 0)), tab, tab, tab],
        out_specs=[_res_spec(ATTN_HEADS, d) for d in DILATIONS for _ in range(3)],
        out_shape=[_res_shape(s, ATTN_HEADS, d, BF16) for d in DILATIONS for _ in range(3)],
        scratch_shapes=[pltpu.VMEM((ATTN_HEADS, ROW_BLOCK, LANES), F32)],
        compiler_params=_params(("parallel",)),
    )(proj, *tables)
    return [tuple(outs[3 * di:3 * di + 3]) for di in range(nd)]


def _rope_bwd(grads, tables, name="rope_bwd"):
    s = grads[0][0].shape[0] * DILATIONS[0]
    half = ROPE_DIM // 2
    nd = len(DILATIONS)

    def body(*refs):
        ins = refs[:3 * nd]
        c_ref, sm_ref, sp_ref, o_ref = refs[3 * nd:3 * nd + 4]
        scrs = refs[3 * nd + 4:]
        c, sm, sp = c_ref[...], sm_ref[...], sp_ref[...]
        for gi, off in enumerate((OFF_AQ, OFF_AK, OFF_AV)):
            for di, dil in enumerate(DILATIONS):
                _from_residues(ins[3 * di + gi], scrs[di], dil)
            for h in range(ATTN_HEADS):
                t = scrs[0][h]
                for scr in scrs[1:]:
                    t = t + scr[h]
                if off != OFF_AV:
                    t = t * c + pltpu.roll(t * sm, half, 1) + pltpu.roll(t * sp, HEAD_DIM - half, 1)
                o_ref[:, off + h * HEAD_DIM: off + (h + 1) * HEAD_DIM] = t.astype(BF16)

    tab = pl.BlockSpec((ROW_BLOCK, HEAD_DIM), lambda i: (i, 0))
    return pl.pallas_call(
        body, name=name, grid=(s // ROW_BLOCK,),
        in_specs=[_res_spec(ATTN_HEADS, d) for d in DILATIONS for _ in range(3)] + [tab, tab, tab],
        out_specs=pl.BlockSpec((ROW_BLOCK, 3 * ATTN_W), lambda i: (i, 0)),
        out_shape=jax.ShapeDtypeStruct((s, IN_PAD), BF16),
        scratch_shapes=[pltpu.VMEM((ATTN_HEADS, ROW_BLOCK, LANES), F32) for _ in DILATIONS],
        compiler_params=_params(("parallel",)),
    )(*[t for g in grads for t in g], *tables)


ATTN_GROUP = 4


def _window_specs(nsteps, width):
    rows, hb = ATTN_GROUP * ATTN_BLOCK, N_SIDE
    per = rows // hb
    cur = pl.BlockSpec((rows, width), lambda r, j: (j, r))
    prev = pl.BlockSpec((hb, width), lambda r, j: (jnp.maximum(per * j - 1, 0), r))
    nxt = pl.BlockSpec((hb, width), lambda r, j: (jnp.minimum(per * (j + 1), per * nsteps - 1), r))
    return prev, cur, nxt


def _block(ref, b, sl):
    return ref[b * ATTN_BLOCK:(b + 1) * ATTN_BLOCK, sl]


def _edge(prev_ref, cur_ref, next_ref, b, sl):
    qb, hb = ATTN_BLOCK, N_SIDE
    before = prev_ref[:, sl] if b == 0 else cur_ref[b * qb - hb:b * qb, sl]
    after = next_ref[:, sl] if b == ATTN_GROUP - 1 else cur_ref[(b + 1) * qb:(b + 1) * qb + hb, sl]
    return jnp.concatenate([before, after], axis=0)


def _band_masks(j, length):
    qb, hb = ATTN_BLOCK, N_SIDE
    row = lax.broadcasted_iota(jnp.int32, (qb, qb), 0)
    col = lax.broadcasted_iota(jnp.int32, (qb, qb), 1)

    def edge_pos(i):
        return j * qb - hb + i + jnp.where(i >= hb, qb, 0)

    def ok(a, b, outside):
        return (jnp.abs(a - b) <= N_SIDE) & (outside >= 0) & (outside < length)

    cur = jnp.abs(row - col) <= N_SIDE
    edge_k = ok(j * qb + row, edge_pos(col), edge_pos(col))
    edge_q = ok(edge_pos(row), j * qb + col, edge_pos(row))
    return cur, edge_k, edge_q


def _attn_fwd(q, k, v, dil, name):
    length = q.shape[0]
    qb = ATTN_BLOCK
    nsteps = length // (ATTN_GROUP * qb)
    scale = HEAD_DIM ** -0.5

    def body(q_ref, kp_ref, kc_ref, kn_ref, vp_ref, vc_ref, vn_ref, o_ref, lse_ref):
        masks = [_band_masks(pl.program_id(1) * ATTN_GROUP + b, length) for b in range(ATTN_GROUP)]
        lane = lax.broadcasted_iota(jnp.int32, (qb, LANES), 1)
        units = [(b, h, slice(h * HEAD_DIM, (h + 1) * HEAD_DIM)) for b in range(ATTN_GROUP)
                 for h in range(ATTN_HEADS)]
        scores = [(_dot_nt(_block(q_ref, b, sl), _block(kc_ref, b, sl)),
                   _dot_nt(_block(q_ref, b, sl), _edge(kp_ref, kc_ref, kn_ref, b, sl))) for b, _, sl in units]
        probs = []
        lse_acc = [jnp.zeros((qb, LANES), F32) for _ in range(ATTN_GROUP)]
        for (b, h, _), (s_c, s_e) in zip(units, scores):
            valid_c, valid_e, _ = masks[b]
            s_c = jnp.where(valid_c, s_c * scale, NEG)
            s_e = jnp.where(valid_e, s_e * scale, NEG)
            m = jnp.max(jnp.maximum(s_c, s_e), axis=-1, keepdims=True)
            p_c, p_e = jnp.exp(s_c - m), jnp.exp(s_e - m)
            den = jnp.sum(p_c + p_e, axis=-1, keepdims=True)
            probs.append((p_c.astype(BF16), p_e.astype(BF16), 1.0 / den))
            lse_acc[b] = jnp.where(lane == h, m + jnp.log(den), lse_acc[b])
        for (b, _, sl), (p_c, p_e, inv) in zip(units, probs):
            o_ref[b * qb:(b + 1) * qb, sl] = (_dot(p_c, _block(vc_ref, b, sl))
                                              + _dot(p_e, _edge(vp_ref, vc_ref, vn_ref, b, sl))) * inv
        for b in range(ATTN_GROUP):
            lse_ref[b * qb:(b + 1) * qb, :] = lse_acc[b]

    prev, cur, nxt = _window_specs(nsteps, ATTN_W)
    return pl.pallas_call(
        body, name=name, grid=(dil, nsteps),
        in_specs=[cur, prev, cur, nxt, prev, cur, nxt],
        out_specs=[cur, pl.BlockSpec((ATTN_GROUP * qb, LANES), lambda r, j: (j, r))],
        out_shape=[jax.ShapeDtypeStruct((length, dil * ATTN_W), F32),
                   jax.ShapeDtypeStruct((length, dil * LANES), F32)],
        compiler_params=_params(("parallel", "parallel")),
    )(q, k, k, k, v, v, v)


def _attn_combine(outs, lses, g, name="attn_combine"):
    s = outs[0].shape[0] * DILATIONS[0]
    nd = len(DILATIONS)

    def body(*refs):
        o_refs, l_refs = refs[:nd], refs[nd:2 * nd]
        g_ref, o_ref, n_ref = refs[2 * nd:2 * nd + 3]
        lse_outs = refs[2 * nd + 3:3 * nd + 3]
        o_scr, l_scr = refs[3 * nd + 3:4 * nd + 3], refs[4 * nd + 3:5 * nd + 3]
        for di, dil in enumerate(DILATIONS):
            _from_residues(o_refs[di], o_scr[di], dil)
            _from_residues(l_refs[di], l_scr[di], dil)
        ls = [scr[0] for scr in l_scr]
        m = ls[0]
        for l in ls[1:]:
            m = jnp.maximum(m, l)
        es = [jnp.exp(l - m) for l in ls]
        z = es[0]
        for e in es[1:]:
            z = z + e
        ws = [e / z for e in es]
        l_scr[0][0] = m + jnp.log(z)
        for di, dil in enumerate(DILATIONS):
            _to_residues(l_scr[0], lse_outs[di], dil)
        ssq = jnp.zeros((ROW_BLOCK, 1), F32)
        for h in range(ATTN_HEADS):
            sl = slice(h * HEAD_DIM, (h + 1) * HEAD_DIM)
            acc = ws[0][:, h:h + 1] * o_scr[0][h]
            for w, scr in zip(ws[1:], o_scr[1:]):
                acc = acc + w[:, h:h + 1] * scr[h]
            o_ref[:, sl] = acc
            ssq = ssq + jnp.sum(acc * acc, axis=-1, keepdims=True)
        r = lax.rsqrt(ssq * (1.0 / ATTN_W) + EPS)
        n_ref[...] = (o_ref[...] * r * g_ref[...]).astype(BF16)

    blk = pl.BlockSpec((ROW_BLOCK, ATTN_W), lambda i: (i, 0))
    outs_ = pl.pallas_call(
        body, name=name, grid=(s // ROW_BLOCK,),
        in_specs=[_res_spec(ATTN_HEADS, d) for d in DILATIONS] + [_res_spec(1, d) for d in DILATIONS]
        + [pl.BlockSpec((1, ATTN_W), lambda i: (0, 0))],
        out_specs=[blk, blk] + [_res_spec(1, d) for d in DILATIONS],
        out_shape=[jax.ShapeDtypeStruct((s, ATTN_W), F32), jax.ShapeDtypeStruct((s, D_MODEL), BF16)]
        + [_res_shape(s, 1, d, F32) for d in DILATIONS],
        scratch_shapes=[pltpu.VMEM((ATTN_HEADS, ROW_BLOCK, LANES), F32) for _ in DILATIONS]
        + [pltpu.VMEM((1, ROW_BLOCK, LANES), F32) for _ in DILATIONS],
        compiler_params=_params(("parallel",)),
    )(*outs, *lses, g)
    return outs_[0], outs_[1], list(outs_[2:])


def _attn_prebwd(dcat, o, g, name="attn_prebwd"):
    s = o.shape[0]
    nd = len(DILATIONS)

    def body(dy_ref, o_ref, g_ref, *rest):
        do_outs, delta_outs, gg_ref = rest[:nd], rest[nd:2 * nd], rest[2 * nd]
        do_scr, delta_scr = rest[2 * nd + 1], rest[2 * nd + 2]
        i = pl.program_id(0)
        dy, ov = dy_ref[...], o_ref[...]
        r = lax.rsqrt(jnp.mean(ov * ov, axis=-1, keepdims=True) + EPS)
        dyg = dy * g_ref[...]
        c = jnp.mean(dyg * ov, axis=-1, keepdims=True)
        do = r * dyg - ov * (r * r * r * c)
        prod = do * ov
        lane = lax.broadcasted_iota(jnp.int32, (ROW_BLOCK, LANES), 1)
        acc = jnp.zeros((ROW_BLOCK, LANES), F32)
        for h in range(ATTN_HEADS):
            sl = slice(h * HEAD_DIM, (h + 1) * HEAD_DIM)
            do_scr[h] = do[:, sl]
            acc = jnp.where(lane == h, jnp.sum(prod[:, sl], axis=-1, keepdims=True), acc)
        delta_scr[0] = acc
        for di, dil in enumerate(DILATIONS):
            _to_residues(do_scr, do_outs[di], dil)
            _to_residues(delta_scr, delta_outs[di], dil)

        @pl.when(i == 0)
        def _():
            gg_ref[...] = jnp.zeros_like(gg_ref)

        gg_ref[...] += jnp.sum(dy * ov * r, axis=0, keepdims=True)

    blk = pl.BlockSpec((ROW_BLOCK, ATTN_W), lambda i: (i, 0))
    vec = pl.BlockSpec((1, ATTN_W), lambda i: (0, 0))
    outs = pl.pallas_call(
        body, name=name, grid=(s // ROW_BLOCK,),
        in_specs=[blk, blk, vec],
        out_specs=[_res_spec(ATTN_HEADS, d) for d in DILATIONS] + [_res_spec(1, d) for d in DILATIONS] + [vec],
        out_shape=[_res_shape(s, ATTN_HEADS, d, BF16) for d in DILATIONS]
        + [_res_shape(s, 1, d, F32) for d in DILATIONS] + [jax.ShapeDtypeStruct((1, ATTN_W), F32)],
        scratch_shapes=[pltpu.VMEM((ATTN_HEADS, ROW_BLOCK, LANES), F32), pltpu.VMEM((1, ROW_BLOCK, LANES), F32)],
        compiler_params=_params(("arbitrary",)),
    )(dcat, o, g)
    return list(outs[:nd]), list(outs[nd:2 * nd]), outs[2 * nd]


def _attn_bwd(q, k, v, do, lse, delta, dil, name):
    length = q.shape[0]
    qb = ATTN_BLOCK
    nsteps = length // (ATTN_GROUP * qb)
    scale = HEAD_DIM ** -0.5

    def body(qp, qc, qn, kp, kc, kn, vp, vc, vn, dop, doc, don, lp, lc, ln, dp, dc, dn, dq_ref, dk_ref, dv_ref):
        masks = [_band_masks(pl.program_id(1) * ATTN_GROUP + b, length) for b in range(ATTN_GROUP)]
        everything = slice(None)
        lse_e = [_edge(lp, lc, ln, b, everything) for b in range(ATTN_GROUP)]
        del_e = [_edge(dp, dc, dn, b, everything) for b in range(ATTN_GROUP)]
        units = [(b, h, slice(h * HEAD_DIM, (h + 1) * HEAD_DIM)) for b in range(ATTN_GROUP)
                 for h in range(ATTN_HEADS)]
        prods = []
        for b, _, sl in units:
            q_c, k_c, v_c, do_c = _block(qc, b, sl), _block(kc, b, sl), _block(vc, b, sl), _block(doc, b, sl)
            q_e, k_e = _edge(qp, qc, qn, b, sl), _edge(kp, kc, kn, b, sl)
            v_e, do_e = _edge(vp, vc, vn, b, sl), _edge(dop, doc, don, b, sl)
            prods.append((_dot_nt(q_c, k_c), _dot_nt(do_c, v_c), _dot_nt(q_c, k_e), _dot_nt(do_c, v_e),
                          _dot_nt(q_e, k_c), _dot_nt(do_e, v_c)))
        parts = []
        for (b, h, _), (s_cc, dp_cc, s_ek, dp_ek, s_eq, dp_eq) in zip(units, prods):
            valid_c, valid_ek, valid_eq = masks[b]
            hc = slice(h, h + 1)
            lse_c, del_c = _block(lc, b, hc), _block(dc, b, hc)
            p_cc = jnp.where(valid_c, jnp.exp(s_cc * scale - lse_c), 0.0)
            ds_cc = (p_cc * (dp_cc - del_c)).astype(BF16)
            p_ek = jnp.where(valid_ek, jnp.exp(s_ek * scale - lse_c), 0.0)
            ds_ek = (p_ek * (dp_ek - del_c)).astype(BF16)
            p_eq = jnp.where(valid_eq, jnp.exp(s_eq * scale - lse_e[b][:, hc]), 0.0)
            ds_eq = (p_eq * (dp_eq - del_e[b][:, hc])).astype(BF16)
            parts.append((p_cc.astype(BF16), ds_cc, ds_ek, p_eq.astype(BF16), ds_eq))
        for (b, _, sl), (p_cc, ds_cc, ds_ek, p_eq, ds_eq) in zip(units, parts):
            rows = slice(b * qb, (b + 1) * qb)
            q_c, k_c, do_c = _block(qc, b, sl), _block(kc, b, sl), _block(doc, b, sl)
            q_e, k_e, do_e = _edge(qp, qc, qn, b, sl), _edge(kp, kc, kn, b, sl), _edge(dop, doc, don, b, sl)
            dq_ref[rows, sl] = ((_dot(ds_cc, k_c) + _dot(ds_ek, k_e)) * scale).astype(BF16)
            dk_ref[rows, sl] = ((_dot_tn(ds_cc, q_c) + _dot_tn(ds_eq, q_e)) * scale).astype(BF16)
            dv_ref[rows, sl] = (_dot_tn(p_cc, do_c) + _dot_tn(p_eq, do_e)).astype(BF16)

    wide, narrow = list(_window_specs(nsteps, ATTN_W)), list(_window_specs(nsteps, LANES))
    return tuple(pl.pallas_call(
        body, name=name, grid=(dil, nsteps),
        in_specs=wide * 4 + narrow * 2,
        out_specs=[wide[1]] * 3,
        out_shape=[jax.ShapeDtypeStruct((length, dil * ATTN_W), BF16)] * 3,
        compiler_params=_params(("parallel", "parallel")),
    )(q, q, q, k, k, k, v, v, v, do, do, do, lse, lse, lse, delta, delta, delta))


def _gate_matrices(gf_up, gb_up):
    pad = LANES - 2 * GLA_RANK
    uf = jnp.concatenate([gf_up, jnp.zeros((GLA_RANK + pad, GLA_KW), gf_up.dtype)], axis=0)
    ub = jnp.concatenate([jnp.zeros((GLA_RANK, GLA_KW), gb_up.dtype), gb_up, jnp.zeros((pad, GLA_KW), gb_up.dtype)], axis=0)
    return uf.astype(BF16), ub.astype(BF16)


def _log_sigmoid(x):
    return jnp.minimum(x, 0.0) - jnp.log(1.0 + jnp.exp(-jnp.abs(x)))


def _gla_gates(proj, uf, ub, gf_b, gb_b, name="gla_gates"):
    s = proj.shape[0]

    def body(z_ref, uf_ref, ub_ref, bf_ref, bb_ref, gf_ref, gb_ref):
        z = z_ref[...].astype(BF16)
        gf_ref[...] = _log_sigmoid(_dot(z, uf_ref[...]) + bf_ref[...]) * (1.0 / GLA_GATE_NORM)
        gb_ref[...] = _log_sigmoid(_dot(z, ub_ref[...]) + bb_ref[...]) * (1.0 / GLA_GATE_NORM)

    mat = pl.BlockSpec((LANES, GLA_KW), lambda i: (0, 0))
    vec = pl.BlockSpec((1, GLA_KW), lambda i: (0, 0))
    out = pl.BlockSpec((ROW_BLOCK, GLA_KW), lambda i: (i, 0))
    return pl.pallas_call(
        body, name=name, grid=(s // ROW_BLOCK,),
        in_specs=[pl.BlockSpec((ROW_BLOCK, LANES), lambda i: (i, OFF_Z // LANES)), mat, mat, vec, vec],
        out_specs=[out, out],
        out_shape=[jax.ShapeDtypeStruct((s, GLA_KW), F32)] * 2,
        compiler_params=_params(("parallel",)),
    )(proj, uf, ub, gf_b, gb_b)


def _gla_gates_bwd(dgf, dgb, proj, uf, ub, gf_b, gb_b, dproj, name="gla_gates_bwd"):
    s = proj.shape[0]
    tail = IN_PAD - OFF_Z

    def body(dgf_ref, dgb_ref, z_ref, uf_ref, ub_ref, bf_ref, bb_ref, _, dz_ref, guf_ref, gub_ref, gbf_ref, gbb_ref):
        i = pl.program_id(0)
        z = z_ref[...].astype(BF16)
        uf_, ub_ = uf_ref[...], ub_ref[...]
        dpf = dgf_ref[...] * (1.0 / GLA_GATE_NORM) * _sigmoid(-(_dot(z, uf_) + bf_ref[...]))
        dpb = dgb_ref[...] * (1.0 / GLA_GATE_NORM) * _sigmoid(-(_dot(z, ub_) + bb_ref[...]))
        dpf_b, dpb_b = dpf.astype(BF16), dpb.astype(BF16)
        dz_ref[:, 0:LANES] = (_dot_nt(dpf_b, uf_) + _dot_nt(dpb_b, ub_)).astype(BF16)
        dz_ref[:, LANES:tail] = jnp.zeros((ROW_BLOCK, tail - LANES), BF16)

        @pl.when(i == 0)
        def _():
            for r in (guf_ref, gub_ref, gbf_ref, gbb_ref):
                r[...] = jnp.zeros_like(r)

        guf_ref[...] += _dot_tn(z, dpf_b)
        gub_ref[...] += _dot_tn(z, dpb_b)
        gbf_ref[...] += jnp.sum(dpf, axis=0, keepdims=True)
        gbb_ref[...] += jnp.sum(dpb, axis=0, keepdims=True)

    mat = pl.BlockSpec((LANES, GLA_KW), lambda i: (0, 0))
    vec = pl.BlockSpec((1, GLA_KW), lambda i: (0, 0))
    blk = pl.BlockSpec((ROW_BLOCK, GLA_KW), lambda i: (i, 0))
    return pl.pallas_call(
        body, name=name, grid=(s // ROW_BLOCK,),
        in_specs=[blk, blk, pl.BlockSpec((ROW_BLOCK, LANES), lambda i: (i, OFF_Z // LANES)), mat, mat, vec, vec,
                  pl.BlockSpec(memory_space=pl.ANY)],
        out_specs=[pl.BlockSpec((ROW_BLOCK, tail), lambda i: (i, OFF_Z // tail)), mat, mat, vec, vec],
        out_shape=[jax.ShapeDtypeStruct(dproj.shape, dproj.dtype), jax.ShapeDtypeStruct((LANES, GLA_KW), F32),
                   jax.ShapeDtypeStruct((LANES, GLA_KW), F32), jax.ShapeDtypeStruct((1, GLA_KW), F32),
                   jax.ShapeDtypeStruct((1, GLA_KW), F32)],
        input_output_aliases={7: 0},
        compiler_params=_params(("arbitrary",)),
    )(dgf, dgb, proj, uf, ub, gf_b, gb_b, dproj)


def _split3(x):
    x1 = x.astype(BF16)
    r1 = x - x1.astype(F32)
    x2 = r1.astype(BF16)
    x3 = (r1 - x2.astype(F32)).astype(BF16)
    return x1, x2, x3


def _dot_exact(mask_bf, x):
    x1, x2, x3 = _split3(x)
    return _dot(mask_bf, x1) + _dot(mask_bf, x2) + _dot(mask_bf, x3)


def _chunk_masks(reverse):
    c = GLA_CHUNK
    row = lax.broadcasted_iota(jnp.int32, (c, c), 0)
    col = lax.broadcasted_iota(jnp.int32, (c, c), 1)
    allowed = (col >= row) if reverse else (col <= row)
    seen_by = (col <= row) if reverse else (col >= row)
    return allowed, seen_by


def _chunk_terms(q_ref, k_ref, g_ref, rs, hs, allowed, reverse):
    c = GLA_CHUNK
    mid, last = (c // 2, 0) if reverse else (c // 2 - 1, c - 1)
    q = q_ref[rs, hs] * (GLA_DK ** -0.5)
    k = k_ref[rs, hs]
    b = _dot_exact(jnp.where(allowed, 1.0, 0.0).astype(BF16), g_ref[rs, hs])
    bref, blast = b[mid:mid + 1, :], b[last:last + 1, :]
    e_q, e_k, e_in, e_st = jnp.exp(b - bref), jnp.exp(bref - b), jnp.exp(b), jnp.exp(blast - b)
    return dict(last=last, e_q=e_q, e_k=e_k, e_in=e_in, e_st=e_st,
                dec=jnp.exp(blast), qe=q * e_q, ke=k * e_k, qin=q * e_in, kst=k * e_st)


def _gla_blockspecs(s, reverse_order):
    cb = GLA_CHUNKS_PER_STEP
    rows = cb * GLA_CHUNK
    nsteps = s // rows

    def rb(n):
        return (nsteps - 1 - n) if reverse_order else n

    qspec = pl.BlockSpec((rows, GLA_KW), lambda n: (rb(n), OFF_GQ // GLA_KW))
    kspec = pl.BlockSpec((rows, GLA_KW), lambda n: (rb(n), OFF_GK // GLA_KW))
    vspec = pl.BlockSpec((rows, GLA_VW), lambda n: (rb(n), OFF_GV // GLA_VW))
    gspec = pl.BlockSpec((rows, GLA_KW), lambda n: (rb(n), 0))
    ospec = pl.BlockSpec((rows, GLA_VW), lambda n: (rb(n), 0))
    sspec = pl.BlockSpec((GLA_HEADS, cb, GLA_DV, GLA_DK), lambda n: (0, rb(n), 0, 0))
    return cb, rows, nsteps, qspec, kspec, vspec, gspec, ospec, sspec


def _gla_units(cb, order_reversed):
    chunks = list(reversed(range(cb))) if order_reversed else list(range(cb))
    return [(c, h, slice(c * GLA_CHUNK, (c + 1) * GLA_CHUNK), slice(h * GLA_DK, (h + 1) * GLA_DK),
             slice(h * GLA_DV, (h + 1) * GLA_DV)) for c in chunks for h in range(GLA_HEADS)]


def _gla_fwd(proj, g, reverse, name):
    s = proj.shape[0]
    cb, rows, nsteps, qspec, kspec, vspec, gspec, ospec, sspec = _gla_blockspecs(s, reverse)

    def body(q_ref, k_ref, v_ref, g_ref, o_ref, st_ref, state):
        @pl.when(pl.program_id(0) == 0)
        def _():
            state[...] = jnp.zeros_like(state)

        allowed, _ = _chunk_masks(reverse)
        units = _gla_units(cb, reverse)
        terms = [_chunk_terms(q_ref, k_ref, g_ref, rs, hs, allowed, reverse) for _, _, rs, hs, _ in units]
        vals = [v_ref[rs, vs].astype(BF16) for _, _, rs, _, vs in units]
        raw = [(_dot_nt(t["qe"].astype(BF16), t["ke"].astype(BF16)), _dot_tn(v, t["kst"].astype(BF16)))
               for t, v in zip(terms, vals)]
        intra = [_dot(jnp.where(allowed, a, 0.0).astype(BF16), v) for (a, _), v in zip(raw, vals)]
        st = [state[h] for h in range(GLA_HEADS)]
        for (c, h, rs, _, vs), t, (_, kv), o_in in zip(units, terms, raw, intra):
            st_b = st[h].astype(BF16)
            st_ref[h, c] = st_b
            o_ref[rs, vs] = o_in + _dot_nt(t["qin"].astype(BF16), st_b)
            st[h] = st[h] * t["dec"] + kv
        for h in range(GLA_HEADS):
            state[h] = st[h]

    return pl.pallas_call(
        body, name=name, grid=(nsteps,),
        in_specs=[qspec, kspec, vspec, gspec],
        out_specs=[ospec, sspec],
        out_shape=[jax.ShapeDtypeStruct((s, GLA_VW), F32),
                   jax.ShapeDtypeStruct((GLA_HEADS, s // GLA_CHUNK, GLA_DV, GLA_DK), BF16)],
        scratch_shapes=[pltpu.VMEM((GLA_HEADS, GLA_DV, GLA_DK), F32)],
        compiler_params=_params(("arbitrary",)),
    )(proj, proj, proj, g)


def _gla_bwd(proj, g, do, states, reverse, name, merge=None):
    s = proj.shape[0]
    cb, rows, nsteps, qspec, kspec, vspec, gspec, ospec, sspec = _gla_blockspecs(s, not reverse)
    gla_cols = OFF_Z - OFF_GQ

    def body(q_ref, k_ref, v_ref, g_ref, do_ref, sp_ref, *rest):
        if merge is None:
            dq_ref, dk_ref, dv_ref, dg_ref, dstate = rest
        else:
            dq_o, dk_o, dv_o, dgr_ref, _, dp_ref, dg_ref, dstate = rest
        @pl.when(pl.program_id(0) == 0)
        def _():
            dstate[...] = jnp.zeros_like(dstate)

        allowed, seen_by = _chunk_masks(reverse)
        units = _gla_units(cb, not reverse)
        terms = [_chunk_terms(q_ref, k_ref, g_ref, rs, hs, allowed, reverse) for _, _, rs, hs, _ in units]
        vals = [v_ref[rs, vs].astype(BF16) for _, _, rs, _, vs in units]
        dos = [do_ref[rs, vs] for _, _, rs, _, vs in units]
        prevs = [sp_ref[h, c] for c, h, _, _, _ in units]
        raw = [(_dot_nt(t["qe"].astype(BF16), t["ke"].astype(BF16)), _dot_nt(do, v),
                _dot(do, sp), _dot_tn(do, t["qin"].astype(BF16)))
               for t, v, do, sp in zip(terms, vals, dos, prevs)]
        inner = []
        for t, do, (a, da, _, _) in zip(terms, dos, raw):
            da = jnp.where(allowed, da, 0.0).astype(BF16)
            inner.append((_dot(da, t["ke"].astype(BF16)), _dot_tn(da, t["qe"].astype(BF16)),
                          _dot_tn(jnp.where(allowed, a, 0.0).astype(BF16), do)))
        ds = [dstate[h] for h in range(GLA_HEADS)]
        outer = []
        for (c, h, _, _, _), t, v, sp, (_, _, _, inc) in zip(units, terms, vals, prevs, raw):
            ds_b = ds[h].astype(BF16)
            outer.append((_dot(v, ds_b), _dot_nt(t["kst"].astype(BF16), ds_b),
                          jnp.sum(sp.astype(F32) * ds[h], axis=0, keepdims=True)))
            ds[h] = ds[h] * t["dec"] + inc
        for h in range(GLA_HEADS):
            dstate[h] = ds[h]
        seen_bf = jnp.where(seen_by, 1.0, 0.0).astype(BF16)
        rowi = lax.broadcasted_iota(jnp.int32, (GLA_CHUNK, GLA_DK), 0)
        for (c, h, rs, hs, vs), t, (_, _, dqin, _), (dqe, dke, dv_in), (dkst, dv_out, ddec) in zip(
                units, terms, raw, inner, outer):
            dq = (dqe * t["e_q"] + dqin * t["e_in"]) * (GLA_DK ** -0.5)
            dk = dke * t["e_k"] + dkst * t["e_st"]
            if merge is None:
                dq_ref[rs, hs], dk_ref[rs, hs], dv_ref[rs, vs] = dq, dk, dv_in + dv_out
            else:
                lo = OFF_GK - OFF_GQ + h * GLA_DK
                dp_ref[rs, hs] = (dq + dq_o[rs, hs]).astype(BF16)
                dp_ref[rs, lo:lo + GLA_DK] = (dk + dk_o[rs, hs]).astype(BF16)
                lo = OFF_GV - OFF_GQ + h * GLA_DV
                dp_ref[rs, lo:lo + GLA_DV] = (dv_in + dv_out + dv_o[rs, vs]).astype(BF16)
            kk = dkst * t["kst"]
            db = dqe * t["qe"] - dke * t["ke"] + dqin * t["qin"] - kk
            extra = jnp.sum(kk, axis=0, keepdims=True) + ddec * t["dec"]
            db = db + jnp.where(rowi == t["last"], extra, 0.0)
            dg_ref[rs, hs] = _dot_exact(seen_bf, db)
        if merge is not None:
            dp_ref[:, OFF_GR - OFF_GQ:gla_cols] = dgr_ref[...]

    scratch = [pltpu.VMEM((GLA_HEADS, GLA_DV, GLA_DK), F32)]
    if merge is None:
        return pl.pallas_call(
            body, name=name, grid=(nsteps,),
            in_specs=[qspec, kspec, vspec, gspec, ospec, sspec],
            out_specs=[gspec, gspec, ospec, gspec],
            out_shape=[jax.ShapeDtypeStruct((s, GLA_KW), F32), jax.ShapeDtypeStruct((s, GLA_KW), F32),
                       jax.ShapeDtypeStruct((s, GLA_VW), F32), jax.ShapeDtypeStruct((s, GLA_KW), F32)],
            scratch_shapes=scratch,
            compiler_params=_params(("arbitrary",)),
        )(proj, proj, proj, g, do, states)
    dproj = merge[4]
    block = gspec.index_map
    return pl.pallas_call(
        body, name=name, grid=(nsteps,),
        in_specs=[qspec, kspec, vspec, gspec, ospec, sspec, gspec, gspec, ospec, ospec, _ANY],
        out_specs=[pl.BlockSpec((rows, gla_cols), lambda n: (block(n)[0], OFF_GQ // gla_cols)), gspec],
        out_shape=[jax.ShapeDtypeStruct(dproj.shape, dproj.dtype), jax.ShapeDtypeStruct((s, GLA_KW), F32)],
        input_output_aliases={10: 0},
        scratch_shapes=scratch,
        compiler_params=_params(("arbitrary",)),
    )(proj, proj, proj, g, do, states, *merge)


def _gla_post(o_f, o_b, proj, g, cat, name="gla_post"):
    s = o_f.shape[0]

    def body(of_ref, ob_ref, gr_ref, g_ref, _, o_ref):
        gv = g_ref[...]
        for h in range(GLA_HEADS):
            sl = slice(h * GLA_DV, (h + 1) * GLA_DV)
            osum = of_ref[:, sl] + ob_ref[:, sl]
            r = lax.rsqrt(jnp.mean(osum * osum, axis=-1, keepdims=True) + EPS)
            gr = gr_ref[:, sl]
            o_ref[:, sl] = (osum * r * gv * (gr * _sigmoid(gr))).astype(BF16)

    blk = pl.BlockSpec((ROW_BLOCK, GLA_VW), lambda i: (i, 0))
    return pl.pallas_call(
        body, name=name, grid=(s // ROW_BLOCK,),
        in_specs=[blk, blk, pl.BlockSpec((ROW_BLOCK, GLA_VW), lambda i: (i, OFF_GR // GLA_VW)),
                  pl.BlockSpec((1, GLA_DV), lambda i: (0, 0)), pl.BlockSpec(memory_space=pl.ANY)],
        out_specs=pl.BlockSpec((ROW_BLOCK, GLA_VW), lambda i: (i, ATTN_W // GLA_VW)),
        out_shape=jax.ShapeDtypeStruct(cat.shape, cat.dtype),
        input_output_aliases={4: 0},
        compiler_params=_params(("parallel",)),
    )(o_f, o_b, proj, g, cat)


def _gla_post_bwd(dcat, o_f, o_b, proj, g, name="gla_post_bwd"):
    s = o_f.shape[0]

    def body(dy_ref, of_ref, ob_ref, gr_ref, g_ref, do_ref, dgr_ref, gg_ref):
        i = pl.program_id(0)
        gv = g_ref[...]
        gg = jnp.zeros((1, GLA_DV), F32)
        for h in range(GLA_HEADS):
            sl = slice(h * GLA_DV, (h + 1) * GLA_DV)
            osum = of_ref[:, sl] + ob_ref[:, sl]
            r = lax.rsqrt(jnp.mean(osum * osum, axis=-1, keepdims=True) + EPS)
            gr, dy = gr_ref[:, sl], dy_ref[:, sl]
            sg = _sigmoid(gr)
            dgr_ref[:, sl] = (dy * (osum * r * gv) * (sg * (1.0 + gr * (1.0 - sg)))).astype(BF16)
            dn = dy * (gr * sg)
            dng = dn * gv
            c = jnp.mean(dng * osum, axis=-1, keepdims=True)
            do_ref[:, sl] = (r * dng - osum * (r * r * r * c)).astype(BF16)
            gg = gg + jnp.sum(dn * osum * r, axis=0, keepdims=True)

        @pl.when(i == 0)
        def _():
            gg_ref[...] = jnp.zeros_like(gg_ref)

        gg_ref[...] += gg

    blk = pl.BlockSpec((ROW_BLOCK, GLA_VW), lambda i: (i, 0))
    vec = pl.BlockSpec((1, GLA_DV), lambda i: (0, 0))
    return pl.pallas_call(
        body, name=name, grid=(s // ROW_BLOCK,),
        in_specs=[pl.BlockSpec((ROW_BLOCK, GLA_VW), lambda i: (i, 1)), blk, blk,
                  pl.BlockSpec((ROW_BLOCK, GLA_VW), lambda i: (i, OFF_GR // GLA_VW)), vec],
        out_specs=[blk, blk, vec],
        out_shape=[jax.ShapeDtypeStruct((s, GLA_VW), BF16), jax.ShapeDtypeStruct((s, GLA_VW), BF16),
                   jax.ShapeDtypeStruct((1, GLA_DV), F32)],
        compiler_params=_params(("arbitrary",)),
    )(dcat, o_f, o_b, proj, g)


HALO = 16


def _extended(prev_ref, cur_ref, next_ref, i, s, tr, cs):
    first, last = i == 0, i == s // tr - 1
    prev = jnp.where(first, 0.0, prev_ref[:, cs].astype(F32))
    nxt = jnp.where(last, 0.0, next_ref[:, cs].astype(F32))
    return jnp.concatenate([prev, cur_ref[:, cs].astype(F32), nxt], axis=0)


FFN_ROWS = 512
FFN_COLS = 512


FFN_CHUNK = 256


def _lagged(i, ni, multiply, finish, rotate, init):
    chunks = [slice(c, c + FFN_CHUNK) for c in range(0, FFN_COLS, FFN_CHUNK)]

    @pl.when(i == 0)
    def _():
        init()

    @pl.when(i < 2)
    def _():
        rotate([multiply(cs) for cs in chunks], chunks)

    @pl.when((i >= 2) & (i < ni))
    def _():
        new = []
        for cs in chunks:
            new.append(multiply(cs))
            finish(cs)
        rotate(new, chunks)

    @pl.when(i >= ni)
    def _():
        for cs in chunks:
            finish(cs)
        rotate(None, chunks)


def _ffn_in(n2, w_gate, w_up, conv_w, conv_b, name="ffn_in"):
    s, d = n2.shape
    f = w_gate.shape[1]
    tm, tn, edge = FFN_ROWS, FFN_COLS, SUBLANES
    ni = s // tm
    ext = tm + 2 * edge

    def body(a_ref, wg_ref, wu_ref, w_ref, b_ref, gate_ref, silu_ref, slope_ref, act_ref, g_tile, u_tile, g_tail):
        i = pl.program_id(1)

        @pl.when(i == 0)
        def _():
            g_tile[...] = jnp.zeros_like(g_tile)
            u_tile[...] = jnp.zeros_like(u_tile)
            g_tail[...] = jnp.zeros_like(g_tail)

        a = a_ref[...]
        g_new = _dot(a, wg_ref[...])
        u_new = _dot(a, wu_ref[...])
        g_old, u_old = g_tile[...], u_tile[...]
        before = jnp.where(i == 1, 0.0, g_tail[...])
        after = jnp.where(i == ni, 0.0, g_new[0:edge])
        ge = jnp.concatenate([before, g_old, after], axis=0)
        w = w_ref[...]
        conv = (w[0:1] * pltpu.roll(ge, 1, 0) + w[1:2] * ge + w[2:3] * pltpu.roll(ge, ext - 1, 0))[edge:edge + tm]
        conv = conv + b_ref[...]
        sg = _sigmoid(conv)
        silu = conv * sg
        u_f = u_old.astype(F32)
        gate_ref[...] = g_old
        silu_ref[...] = silu.astype(BF16)
        slope_ref[...] = (u_f * (sg * (1.0 + conv * (1.0 - sg)))).astype(BF16)
        act_ref[...] = (silu * u_f).astype(BF16)
        g_tail[...] = g_old[tm - edge:tm]
        g_tile[...] = g_new
        u_tile[...] = u_new.astype(BF16)

    lag = pl.BlockSpec((tm, tn), lambda j, i: (jnp.maximum(i - 1, 0), j))
    return pl.pallas_call(
        body, name=name, grid=(f // tn, ni + 1),
        in_specs=[pl.BlockSpec((tm, d), lambda j, i: (jnp.minimum(i, ni - 1), 0)),
                  pl.BlockSpec((d, tn), lambda j, i: (0, j)), pl.BlockSpec((d, tn), lambda j, i: (0, j)),
                  pl.BlockSpec((3, tn), lambda j, i: (0, j)), pl.BlockSpec((1, tn), lambda j, i: (0, j))],
        out_specs=[lag, lag, lag, lag],
        out_shape=[jax.ShapeDtypeStruct((s, f), F32)] + [jax.ShapeDtypeStruct((s, f), BF16)] * 3,
        scratch_shapes=[pltpu.VMEM((tm, tn), F32), pltpu.VMEM((tm, tn), BF16), pltpu.VMEM((edge, tn), F32)],
        compiler_params=_params(("parallel", "arbitrary")),
    )(n2, w_gate, w_up, conv_w, conv_b)


def _ffn_mid_bwd(dh2, w_down, gate, silu, slope, conv_w, name="ffn_mid_bwd"):
    s, d = dh2.shape
    f = gate.shape[1]
    tm, tn = FFN_ROWS, FFN_COLS
    ni = s // tm
    ext = tm + 2 * HALO
    per, last_halo = tm // HALO, s // HALO - 1

    def body(a_ref, wd_ref, gp, gc, gn, sp, sc, sn, silu_ref, w_ref, dg_ref, du_ref, gw_ref, gb_ref,
             d_near, d_far, d_tail):
        i = pl.program_id(1)

        def multiply(cs):
            return _dot_nt(a_ref[...], wd_ref[cs, :])

        def finish(cs):
            before = jnp.where(i == 2, 0.0, d_tail[:, cs])
            after = jnp.where(i == ni + 1, 0.0, d_near[0:HALO, cs])
            d_mid = d_far[:, cs]
            de = jnp.concatenate([before, d_mid, after], axis=0)
            ge = _extended(gp, gc, gn, i - 2, s, tm, cs)
            w = w_ref[:, cs]
            g_prev, g_next = pltpu.roll(ge, 1, 0), pltpu.roll(ge, ext - 1, 0)
            inner = slice(HALO, HALO + tm)
            du_ref[:, cs] = (d_mid * silu_ref[:, cs].astype(F32)).astype(BF16)
            dconv = de * _extended(sp, sc, sn, i - 2, s, tm, cs)
            dgate = w[0:1] * pltpu.roll(dconv, ext - 1, 0) + w[1:2] * dconv + w[2:3] * pltpu.roll(dconv, 1, 0)
            dg_ref[:, cs] = dgate[inner].astype(BF16)
            dci = dconv[inner]
            gw_ref[0:1, cs] += jnp.sum(dci * g_prev[inner], axis=0, keepdims=True)
            gw_ref[1:2, cs] += jnp.sum(dci * ge[inner], axis=0, keepdims=True)
            gw_ref[2:3, cs] += jnp.sum(dci * g_next[inner], axis=0, keepdims=True)
            gb_ref[:, cs] += jnp.sum(dci, axis=0, keepdims=True)

        def rotate(new, chunks):
            d_tail[...] = d_far[tm - HALO:tm]
            d_far[...] = d_near[...]
            if new is not None:
                for cs, d_new in zip(chunks, new):
                    d_near[:, cs] = d_new

        def init():
            for r in (d_near, d_far, d_tail, gw_ref, gb_ref):
                r[...] = jnp.zeros_like(r)

        _lagged(i, ni, multiply, finish, rotate, init)

    def tile(i):
        return jnp.maximum(i - 2, 0)

    cur = pl.BlockSpec((tm, tn), lambda j, i: (tile(i), j))
    prev = pl.BlockSpec((HALO, tn), lambda j, i: (jnp.maximum(tile(i) * per - 1, 0), j))
    nxt = pl.BlockSpec((HALO, tn), lambda j, i: (jnp.minimum((tile(i) + 1) * per, last_halo), j))
    wspec = pl.BlockSpec((3, tn), lambda j, i: (0, j))
    bspec = pl.BlockSpec((1, tn), lambda j, i: (0, j))
    return pl.pallas_call(
        body, name=name, grid=(f // tn, ni + 2),
        in_specs=[pl.BlockSpec((tm, d), lambda j, i: (jnp.minimum(i, ni - 1), 0)),
                  pl.BlockSpec((tn, d), lambda j, i: (j, 0))] + [prev, cur, nxt] * 2 + [cur, wspec],
        out_specs=[cur, cur, wspec, bspec],
        out_shape=[jax.ShapeDtypeStruct((s, f), BF16), jax.ShapeDtypeStruct((s, f), BF16),
                   jax.ShapeDtypeStruct((3, f), F32), jax.ShapeDtypeStruct((1, f), F32)],
        scratch_shapes=[pltpu.VMEM((tm, tn), F32), pltpu.VMEM((tm, tn), F32), pltpu.VMEM((HALO, tn), F32)],
        compiler_params=_params(("parallel", "arbitrary")),
    )(dh2, w_down, gate, gate, gate, slope, slope, slope, silu, conv_w)


def _local_step(x, target, w, late_weights=None, grad_sink=None, first_dep=()):
    s = x.shape[0]
    tables = _rope_tables(s)
    uf, ub = _gate_matrices(w["gf_up"], w["gb_up"])
    if grad_sink is None:
        grad_sink = lambda names, grads: ()

    n1 = _rms_fwd(x, w["norm1_g"], "norm1")
    proj = _matmul([(n1, w["w_in"])], "nn", F32, 1024, 1280, D_MODEL, "in_proj", deps=first_dep)
    qkv = _rope_fwd(proj, tables)
    branches = [_attn_fwd(*qkv[di], d, f"attn_fwd_d{d}") for di, d in enumerate(DILATIONS)]
    o_mix, ao, lse = _attn_combine([b[0] for b in branches], [b[1] for b in branches], w["attn_norm_g"])
    g_f, g_b = _gla_gates(proj, uf, ub, w["gf_b"], w["gb_b"])
    o_f, st_f = _gla_fwd(proj, g_f, False, "gla_fwd_f")
    o_b, st_b = _gla_fwd(proj, g_b, True, "gla_fwd_b")
    cat = _gla_post(o_f, o_b, proj, w["gla_norm_g"], ao)
    if late_weights is not None:
        w = {**w, **late_weights("mixer", cat)}
    h1 = _matmul([(cat, w["w_out"])], "nn", F32, 1024, 1024, D_MODEL, "out_proj", res=x)
    n2 = _rms_fwd(h1, w["norm2_g"], "norm2")
    if late_weights is not None:
        w = {**w, **late_weights("ffn", n2)}
    gate, silu, slope, act = _ffn_in(n2, w["w_gate"], w["w_up"], w["conv_w"], w["conv_b"])
    h2 = _matmul([(act, w["w_down"])], "nn", F32, 1024, 1024, 2816, "ffn_down", res=h1)
    dh2, dh2_b, loss_acc, g_final = _final_loss(h2, target, w["final_norm_g"])

    g_w_down = _matmul([(act, dh2_b)], "tn", BF16, 1408, 1024, 2048, "g_w_down")
    dep = grad_sink(["w_down"], [g_w_down])
    dgate, dup, g_conv_w, g_conv_b = _ffn_mid_bwd(dh2_b, w["w_down"], gate, silu, slope, w["conv_w"])
    g_w_gate = _matmul([(n2, dgate)], "tn", BF16, 2048, 512, 2048, "g_w_gate", deps=dep)
    g_w_up = _matmul([(n2, dup)], "tn", BF16, 2048, 512, 2048, "g_w_up")
    dep = grad_sink(["w_gate", "w_up"], [g_w_gate, g_w_up])
    dn2 = _matmul([(dgate, w["w_gate"])], "nt", F32, 1024, 1024, 2816, "d_n2_gate", deps=dep)
    dn2 = _matmul([(dup, w["w_up"])], "nt", F32, 1024, 1024, 2816, "d_n2_up", res=dn2)
    dh1, dh1_b, g_norm2 = _rms_bwd(dn2, h1, w["norm2_g"], dh2, "norm2_bwd")

    g_w_out = _matmul([(cat, dh1_b)], "tn", BF16, 1024, 1024, 2048, "g_w_out")
    dep = grad_sink(["w_out"], [g_w_out])
    dcat = _matmul([(dh1_b, w["w_out"])], "nt", F32, 1024, 1024, D_MODEL, "d_cat", deps=dep)
    do_attn, delta, g_attn_norm = _attn_prebwd(dcat, o_mix, w["attn_norm_g"])
    grads = [_attn_bwd(*qkv[di], do_attn[di], lse[di], delta[di], d, f"attn_bwd_d{d}")
             for di, d in enumerate(DILATIONS)]
    dproj = _rope_bwd(grads, tables)
    do_gla, dgr, g_gla_norm = _gla_post_bwd(dcat, o_f, o_b, proj, w["gla_norm_g"])
    dq_f, dk_f, dv_f, dg_f = _gla_bwd(proj, g_f, do_gla, st_f, False, "gla_bwd_f")
    dproj, dg_b = _gla_bwd(proj, g_b, do_gla, st_b, True, "gla_bwd_b", merge=(dq_f, dk_f, dv_f, dgr, dproj))
    dproj, g_uf, g_ub, g_gf_b, g_gb_b = _gla_gates_bwd(dg_f, dg_b, proj, uf, ub, w["gf_b"], w["gb_b"], dproj)
    g_w_in = _matmul([(n1, dproj)], "tn", BF16, 1024, 1280, 2048, "g_w_in")
    dep = grad_sink(["w_in"], [g_w_in])
    dn1 = _matmul([(dproj, w["w_in"])], "nt", F32, 1024, 2048, 1280, "d_n1", deps=dep)
    grad_x, g_norm1 = _rms_bwd(dn1, x, w["norm1_g"], dh1, "norm1_bwd", bf16_copy=False)

    g = dict(norm1_g=g_norm1, w_in=g_w_in, gf_up=g_uf[:GLA_RANK], gf_b=g_gf_b,
             gb_up=g_ub[GLA_RANK:2 * GLA_RANK], gb_b=g_gb_b, gla_norm_g=g_gla_norm, attn_norm_g=g_attn_norm,
             w_out=g_w_out, norm2_g=g_norm2, w_gate=g_w_gate, w_up=g_w_up, conv_w=g_conv_w, conv_b=g_conv_b,
             w_down=g_w_down, final_norm_g=g_final)
    return loss_acc, grad_x, g


def _me_and_peers():
    x, y, c = lax.axis_index("x"), lax.axis_index("y"), lax.axis_index("c")
    me = 4 * x + 2 * y + c
    peers = []
    for kbits in range(1, N_DEV):
        px, py, pc = x ^ (kbits >> 2 & 1), y ^ (kbits >> 1 & 1), c ^ (kbits & 1)
        peers.append(((px, py, pc), 4 * px + 2 * py + pc))
    return me, peers


_HBM = pl.BlockSpec(memory_space=pltpu.HBM)
_SEM = pl.BlockSpec(memory_space=pltpu.SEMAPHORE)
_ANY = pl.BlockSpec(memory_space=pl.ANY)
_EFFECT = pltpu.SideEffectType.DATAFLOW_SIDE_EFFECTING


def _exchange_copies(src_refs, land_refs, send_sems, recv_sems, scatter):
    me, peers = _me_and_peers()
    out = []
    for a, (src, land) in enumerate(zip(src_refs, land_refs)):
        for kk, (dev, idx) in enumerate(peers):
            out.append(pltpu.make_async_remote_copy(
                src_ref=src.at[idx] if scatter else src, dst_ref=land.at[me],
                send_sem=send_sems.at[a * (N_DEV - 1) + kk], recv_sem=recv_sems.at[a * (N_DEV - 1) + kk],
                device_id=dev, device_id_type=MESH_ID))
    return out


def _exchange_start(srcs, lands, scatter, name, deps=()):
    n, nd = len(srcs), len(deps)

    def body(*refs):
        src_refs, land_refs = refs[:n], refs[n:2 * n]
        send_sems, recv_sems = refs[2 * n + nd:2 * n + nd + 2]
        token = refs[-1]
        for cp in _exchange_copies(src_refs, land_refs, send_sems, recv_sems, scatter):
            cp.start()
        token[...] = jnp.zeros_like(token)

    outs = pl.pallas_call(
        body, name=name,
        in_specs=[_HBM] * (2 * n) + [_ANY] * nd,
        out_specs=[_SEM, _SEM] + [_HBM] * (2 * n) + [pl.BlockSpec(memory_space=pltpu.VMEM)],
        out_shape=[pltpu.SemaphoreType.DMA((n * (N_DEV - 1),)), pltpu.SemaphoreType.DMA((n * (N_DEV - 1),))]
        + [pltpu.HBM(t.shape, t.dtype) for t in srcs] + [pltpu.HBM(t.shape, t.dtype) for t in lands]
        + [jax.ShapeDtypeStruct((SUBLANES, LANES), F32)],
        input_output_aliases={i: 2 + i for i in range(2 * n)},
        compiler_params=pltpu.CompilerParams(has_side_effects=_EFFECT),
    )(*[pltpu.with_memory_space_constraint(t, pltpu.HBM) for t in list(srcs) + list(lands)], *deps)
    send_sems, recv_sems = outs[0], outs[1]
    return dict(send=send_sems, recv=recv_sems, srcs=outs[2:2 + n], lands=outs[2 + n:2 + 2 * n],
                scatter=scatter, token=outs[-1])


def _exchange_wait(started, name, after):
    n = len(started["srcs"])
    scatter = started["scatter"]

    def body(*refs):
        src_refs, land_refs = refs[:n], refs[n:2 * n]
        send_sems, recv_sems = refs[2 * n], refs[2 * n + 1]
        for cp in _exchange_copies(src_refs, land_refs, send_sems, recv_sems, scatter):
            cp.wait_send()
            cp.wait_recv()

    outs = pl.pallas_call(
        body, name=name,
        in_specs=[_HBM] * (2 * n) + [_SEM, _SEM, _ANY],
        out_specs=[_HBM] * (2 * n),
        out_shape=[pltpu.HBM(t.shape, t.dtype) for t in started["srcs"]]
        + [pltpu.HBM(t.shape, t.dtype) for t in started["lands"]],
        input_output_aliases={i: i for i in range(2 * n)},
        compiler_params=pltpu.CompilerParams(has_side_effects=_EFFECT),
    )(*started["srcs"], *started["lands"], started["send"], started["recv"], after)
    return outs[:n], outs[n:]


def _all_gather_two_level(shard, name):
    def body(x_ref, out_ref, send_sems, recv_sems, local_sem):
        x, y, c = lax.axis_index("x"), lax.axis_index("y"), lax.axis_index("c")
        me, sibling = (x, y, c), (x, y, 1 - c)
        chips = [(1 - x, y), (x, 1 - y), (1 - x, 1 - y)]

        def slot(px, py, pc):
            return out_ref.at[4 * px + 2 * py + pc]

        def copy(k, block, to, src=None):
            return pltpu.make_async_remote_copy(
                src_ref=slot(*block) if src is None else src, dst_ref=slot(*block),
                send_sem=send_sems.at[k], recv_sem=recv_sems.at[k], device_id=to, device_id_type=MESH_ID)

        mine = pltpu.make_async_copy(x_ref, slot(*me), local_sem)
        mine.start()
        first = [copy(0, me, sibling, src=x_ref)]
        first += [copy(1 + j, me, (*chip, c), src=x_ref) for j, chip in enumerate(chips)]
        for cp in first:
            cp.start()
        passed = [copy(4 + j, (*chip, c), sibling) for j, chip in enumerate(chips)]
        for j, chip in enumerate(chips):
            copy(1 + j, (*chip, c), me).wait_recv()
            passed[j].start()
        copy(0, sibling, me).wait_recv()
        for j, chip in enumerate(chips):
            copy(4 + j, (*chip, 1 - c), me).wait_recv()
        for cp in first + passed:
            cp.wait_send()
        mine.wait()

    return pl.pallas_call(
        body, name=name,
        in_specs=[_ANY], out_specs=_ANY,
        out_shape=jax.ShapeDtypeStruct((N_DEV,) + shard.shape, shard.dtype),
        scratch_shapes=[pltpu.SemaphoreType.DMA((N_DEV - 1,)), pltpu.SemaphoreType.DMA((N_DEV - 1,)),
                        pltpu.SemaphoreType.DMA],
    )(shard)


def _all_gather_vmem(vec, name):
    r = vec.shape[0]

    def body(v_ref, o_ref, send_sems, recv_sems):
        me, peers = _me_and_peers()
        o_ref[me] = v_ref[...]
        sends = []
        for kk, (dev, _) in enumerate(peers):
            cp = pltpu.make_async_remote_copy(
                src_ref=v_ref, dst_ref=o_ref.at[me],
                send_sem=send_sems.at[kk], recv_sem=recv_sems.at[kk],
                device_id=dev, device_id_type=MESH_ID)
            cp.start()
            sends.append(cp)
        for kk, (dev, idx) in enumerate(peers):
            pltpu.make_async_remote_copy(
                src_ref=v_ref, dst_ref=o_ref.at[idx],
                send_sem=send_sems.at[kk], recv_sem=recv_sems.at[kk],
                device_id=dev, device_id_type=MESH_ID).wait_recv()
        for cp in sends:
            cp.wait_send()

    return pl.pallas_call(
        body, name=name,
        in_specs=[pl.BlockSpec(memory_space=pltpu.VMEM)],
        out_specs=pl.BlockSpec(memory_space=pltpu.VMEM),
        out_shape=jax.ShapeDtypeStruct((N_DEV, r, LANES), F32),
        scratch_shapes=[pltpu.SemaphoreType.DMA((N_DEV - 1,)), pltpu.SemaphoreType.DMA((N_DEV - 1,))],
        compiler_params=pltpu.CompilerParams(vmem_limit_bytes=VMEM_LIMIT),
    )(vec)


def _adamw_math(w, g, m, v):
    m = ADAM_B1 * m + (1.0 - ADAM_B1) * g
    v = ADAM_B2 * v + (1.0 - ADAM_B2) * (g * g)
    m_hat = m / (1.0 - ADAM_B1 ** ADAM_STEP)
    v_hat = v / (1.0 - ADAM_B2 ** ADAM_STEP)
    delta = -ADAM_LR * (m_hat / (jnp.sqrt(v_hat) + ADAM_EPS) + ADAM_WD * w)
    return delta, m, v


def _adamw_sum(parts, w, m, v, tr, name, own=None, me=None):
    r, c = w.shape

    def body(*refs):
        if own is None:
            p_ref, w_ref, m_ref, v_ref, g_ref, d_ref, nm_ref, nv_ref = refs
            terms = [p_ref[kk] for kk in range(N_DEV)]
        else:
            me_ref, p_ref, own_ref, w_ref, m_ref, v_ref, g_ref, d_ref, nm_ref, nv_ref = refs
            terms = [jnp.where(me_ref[0] == kk, own_ref[0], p_ref[kk]).astype(F32) for kk in range(N_DEV)]
        g = terms[0]
        for t in terms[1:]:
            g = g + t
        g_ref[...] = g
        d_ref[...], nm_ref[...], nv_ref[...] = _adamw_math(w_ref[...], g, m_ref[...], v_ref[...])

    out_shape = [jax.ShapeDtypeStruct((r, c), F32)] * 4
    if own is None:
        blk = pl.BlockSpec((tr, c), lambda i: (i, 0))
        return pl.pallas_call(
            body, name=name, grid=(r // tr,),
            in_specs=[pl.BlockSpec((N_DEV, tr, c), lambda i: (0, i, 0)), blk, blk, blk],
            out_specs=[blk] * 4, out_shape=out_shape,
            compiler_params=_params(("parallel",)),
        )(parts, w, m, v)
    blk = pl.BlockSpec((tr, c), lambda i, me_ref: (i, 0))
    return pl.pallas_call(
        body, name=name,
        grid_spec=pltpu.PrefetchScalarGridSpec(
            num_scalar_prefetch=1, grid=(r // tr,),
            in_specs=[pl.BlockSpec((N_DEV, tr, c), lambda i, me_ref: (0, i, 0)),
                      pl.BlockSpec((1, tr, c), lambda i, me_ref: (me_ref[0], i, 0)), blk, blk, blk],
            out_specs=[blk] * 4),
        out_shape=out_shape,
        compiler_params=_params(("parallel",)),
    )(jnp.reshape(me, (1,)).astype(jnp.int32), parts, own, w, m, v)


def _slabs_to_wide(slabs, width, name):
    n, r, c = slabs.shape

    def body(i_ref, o_ref):
        for k in range(n):
            o_ref[:, c * k:c * (k + 1)] = i_ref[k]
        if width > n * c:
            o_ref[:, n * c:width] = jnp.zeros((ROW_BLOCK, width - n * c), o_ref.dtype)

    return pl.pallas_call(
        body, name=name, grid=(r // ROW_BLOCK,),
        in_specs=[pl.BlockSpec((n, ROW_BLOCK, c), lambda i: (0, i, 0))],
        out_specs=pl.BlockSpec((ROW_BLOCK, width), lambda i: (i, 0)),
        out_shape=jax.ShapeDtypeStruct((r, width), slabs.dtype),
        compiler_params=_params(("parallel",)),
    )(slabs)


def _wide_to_slabs(wide, c, name):
    r, width = wide.shape

    def body(i_ref, o_ref):
        for k in range(N_DEV):
            o_ref[k] = i_ref[:, c * k:c * (k + 1)]

    return pl.pallas_call(
        body, name=name, grid=(r // ROW_BLOCK,),
        in_specs=[pl.BlockSpec((ROW_BLOCK, width), lambda i: (i, 0))],
        out_specs=pl.BlockSpec((N_DEV, ROW_BLOCK, c), lambda i: (0, i, 0)),
        out_shape=jax.ShapeDtypeStruct((N_DEV, r, c), wide.dtype),
        compiler_params=_params(("parallel",)),
    )(wide)


_SMALL = ("norm1_g", "gf_b", "gb_b", "gla_norm_g", "attn_norm_g", "norm2_g", "conv_b", "final_norm_g",
          "gf_up", "gb_up", "conv_w")


def _pack(named):
    flat = jnp.concatenate([jnp.ravel(t).astype(F32) for t in named])
    tile = SUBLANES * LANES
    total = -(-flat.shape[0] // tile) * tile
    return jnp.pad(flat, (0, total - flat.shape[0])).reshape(total // LANES, LANES)


def _unpack(packed, shapes):
    flat = packed.reshape(-1)
    out, off = [], 0
    for shp in shapes:
        size = int(np.prod(shp))
        out.append(flat[off:off + size].reshape(shp))
        off += size
    return out


def kernel(x, norm1_g, w_in, gf_up, gf_b, gb_up, gb_b, gla_norm_g, attn_norm_g, w_out, norm2_g, w_gate, w_up, conv_w, conv_b, w_down, final_norm_g, loss_target, m_norm1_g, m_w_in, m_gf_up, m_gf_b, m_gb_up, m_gb_b, m_gla_norm_g, m_attn_norm_g, m_w_out, m_norm2_g, m_w_gate, m_w_up, m_conv_w, m_conv_b, m_w_down, m_final_norm_g, v_norm1_g, v_w_in, v_gf_up, v_gf_b, v_gb_up, v_gb_b, v_gla_norm_g, v_attn_norm_g, v_w_out, v_norm2_g, v_w_gate, v_w_up, v_conv_w, v_conv_b, v_w_down, v_final_norm_g):
    names = ("norm1_g", "w_in", "gf_up", "gf_b", "gb_up", "gb_b", "gla_norm_g", "attn_norm_g", "w_out", "norm2_g",
             "w_gate", "w_up", "conv_w", "conv_b", "w_down", "final_norm_g")
    ws = dict(zip(names, (norm1_g, w_in, gf_up, gf_b, gb_up, gb_b, gla_norm_g, attn_norm_g, w_out, norm2_g,
                          w_gate, w_up, conv_w, conv_b, w_down, final_norm_g)))
    ms = dict(zip(names, (m_norm1_g, m_w_in, m_gf_up, m_gf_b, m_gb_up, m_gb_b, m_gla_norm_g, m_attn_norm_g, m_w_out,
                          m_norm2_g, m_w_gate, m_w_up, m_conv_w, m_conv_b, m_w_down, m_final_norm_g)))
    vs = dict(zip(names, (v_norm1_g, v_w_in, v_gf_up, v_gf_b, v_gb_up, v_gb_b, v_gla_norm_g, v_attn_norm_g, v_w_out,
                          v_norm2_g, v_w_gate, v_w_up, v_conv_w, v_conv_b, v_w_down, v_final_norm_g)))
    me = 4 * lax.axis_index("x") + 2 * lax.axis_index("y") + lax.axis_index("c")
    big = ("w_in", "w_out", "w_gate", "w_up", "w_down")
    col_sharded = ("w_in", "w_gate", "w_up")

    def gather_start(group, name, deps=()):
        shards = [ws[n][0].astype(BF16) for n in group]
        lands = [lax.empty((N_DEV,) + t.shape, BF16) for t in shards]
        return _exchange_start(shards, lands, False, name, deps)

    def gather_finish(group, started, name, after):
        full = {}
        for n, own, t in zip(group, *_exchange_wait(started, name, after)):
            t = lax.dynamic_update_slice(t, own[None], (me, 0, 0))
            if n in col_sharded:
                full[n] = _slabs_to_wide(t, N_DEV * t.shape[2], "widen_" + n)
            else:
                full[n] = t.reshape(N_DEV * t.shape[1], t.shape[2])
        return full

    w_in_all = _all_gather_two_level(ws["w_in"][0].astype(BF16), "gather_w_in")
    full = {"w_in": _slabs_to_wide(w_in_all, IN_PAD, "widen_w_in")}
    late = {"mixer": ("w_out",), "ffn": ("w_gate", "w_up", "w_down")}
    started_late = {"mixer": gather_start(late["mixer"], "gather_w_out_start", deps=(full["w_in"],))}
    started_late["ffn"] = gather_start(late["ffn"], "gather_ffn_start", deps=(started_late["mixer"]["token"],))

    def late_weights(part, after):
        return gather_finish(late[part], started_late[part], "gather_" + part + "_wait", after)

    small_sharded = ("gf_up", "gb_up", "conv_w")
    sm = _all_gather_vmem(_pack([ws[n][0] for n in small_sharded]), "gather_small")
    shard_shapes = [ws[n][0].shape for n in small_sharded]
    per_dev = [_unpack(sm[d], shard_shapes) for d in range(N_DEV)]
    for i, n in enumerate(small_sharded):
        full[n] = jnp.concatenate([per_dev[d][i] for d in range(N_DEV)], axis=1)
    for n in ("norm1_g", "gf_b", "gb_b", "gla_norm_g", "attn_norm_g", "norm2_g", "conv_b"):
        full[n] = ws[n]
    full["final_norm_g"] = final_norm_g.reshape(1, D_MODEL)

    in_flight = []

    def grad_sink(group, grads):
        partials = []
        for n, t in zip(group, grads):
            t = t.astype(BF16)
            if n in col_sharded:
                t = _wide_to_slabs(t, ws[n].shape[2], "slabs_" + n)
            else:
                t = t.reshape(N_DEV, t.shape[0] // N_DEV, t.shape[1])
            partials.append(t)
        lands = [lax.empty(t.shape, t.dtype) for t in partials]
        started = _exchange_start(partials, lands, True, "exchange_" + "_".join(group) + "_start")
        in_flight.append((group, started))
        return (started["token"],)

    loss_acc, grad_x, g = _local_step(x[0], loss_target[0], full, late_weights, grad_sink,
                                      first_dep=(started_late["ffn"]["token"],))

    out = {}
    for group, started in in_flight:
        sent, landed = _exchange_wait(started, "exchange_" + "_".join(group) + "_wait", grad_x)
        for n, parts, own in zip(group, landed, sent):
            rows = ws[n].shape[1]
            tr = max(t for t in range(HALO, ROW_BLOCK + 1, HALO) if rows % t == 0)
            out[n] = _adamw_sum(parts, ws[n][0], ms[n][0], vs[n][0], tr, "adamw_" + n, own=own, me=me)

    small_full_shapes = [g[n].shape for n in _SMALL]
    gsmall = _pack([g[n] for n in _SMALL] + [loss_acc[0:1, 0:1]])
    gathered_small = _all_gather_vmem(gsmall, "gather_small_grads")

    def full_small(d):
        parts = []
        for n in _SMALL:
            t = d[n].reshape(d[n].shape[-2:]) if d[n].ndim == 3 else d[n].reshape(1, -1)
            if n in small_sharded:
                wide = jnp.zeros((t.shape[0], t.shape[1] * N_DEV), F32)
                t = lax.dynamic_update_slice_in_dim(wide, t, me * t.shape[1], axis=1)
            parts.append(t)
        return _pack(parts + [jnp.zeros((1, 1), F32)])

    rows = gsmall.shape[0]
    res_small = _adamw_sum(gathered_small, full_small(ws), full_small(ms), full_small(vs), rows, "adamw_small")
    loss = res_small[0].reshape(-1)[sum(int(np.prod(sh)) for sh in small_full_shapes)]
    unpacked = [_unpack(t, small_full_shapes) for t in res_small]
    for i, n in enumerate(_SMALL):
        vals = [u[i] for u in unpacked]
        if n in small_sharded:
            width = vals[0].shape[1] // N_DEV
            vals = [lax.dynamic_slice_in_dim(t, me * width, width, axis=1) for t in vals]
        out[n] = vals

    result = [loss, grad_x[None]]
    for kind in range(4):
        for n in names:
            result.append(out[n][kind].reshape(ws[n].shape))
    return tuple(result)
```

```python
import functools

import numpy as np
import jax
import jax.numpy as jnp
from jax import lax
from jax.experimental import pallas as pl
from jax.experimental.pallas import tpu as pltpu

F32 = jnp.float32
BF16 = jnp.bfloat16

D_MODEL = 2048
ATTN_W = 1024
ATTN_HEADS = 8
HEAD_DIM = 128
ROPE_DIM = 32
ROPE_THETA = 500000.0
DILATIONS = (1, 4, 16)
N_SIDE = 64
GLA_KW = 512
GLA_VW = 1024
GLA_HEADS = 4
GLA_DK = 128
GLA_DV = 256
GLA_RANK = 16
GLA_GATE_NORM = 16.0
GLA_CHUNK = 64
IN_WIDTH = 6176
IN_PAD = 6400
D_FF = 5632
EPS = 1e-6
N_DEV = 8

OFF_AQ, OFF_AK, OFF_AV = 0, 1024, 2048
OFF_GQ, OFF_GK, OFF_GV, OFF_GR, OFF_Z = 3072, 3584, 4096, 5120, 6144

ADAM_LR, ADAM_B1, ADAM_B2, ADAM_EPS, ADAM_WD, ADAM_STEP = 0.001, 0.9, 0.999, 1e-08, 0.01, 10

LANES = 128
SUBLANES = 8
VMEM_LIMIT = 56 * 1024 * 1024
ROW_BLOCK = 256
ATTN_BLOCK = 128
GLA_CHUNKS_PER_STEP = 4
NEG = -1e30
MESH_ID = pl.DeviceIdType.MESH


def _params(sem):
    return pltpu.CompilerParams(dimension_semantics=sem, vmem_limit_bytes=VMEM_LIMIT)


def _dot(a, b):
    return lax.dot_general(a, b, (((1,), (0,)), ((), ())), preferred_element_type=F32)


def _dot_nt(a, b):
    return lax.dot_general(a, b, (((1,), (1,)), ((), ())), preferred_element_type=F32)


def _dot_tn(a, b):
    return lax.dot_general(a, b, (((0,), (0,)), ((), ())), preferred_element_type=F32)


def _sigmoid(x):
    return 0.5 * jnp.tanh(0.5 * x) + 0.5


def _matmul(pairs, mode, out_dtype, tm, tn, tk, name, res=None, deps=()):
    a0, b0 = pairs[0]
    if mode == "nn":
        (m, kdim), n = a0.shape, b0.shape[1]
    elif mode == "nt":
        (m, kdim), n = a0.shape, b0.shape[0]
    else:
        (kdim, m), n = a0.shape, b0.shape[1]
    assert m % tm == 0 and n % tn == 0 and kdim % tk == 0, (name, m, n, kdim)
    nk = kdim // tk
    npairs = len(pairs)
    steps = nk * npairs
    dot = {"nn": _dot, "nt": _dot_nt, "tn": _dot_tn}[mode]

    def kidx(p):
        return lambda k: jnp.clip(k - p * nk, 0, nk - 1)

    in_specs, args = [], []
    for p, (a, b) in enumerate(pairs):
        kk = kidx(p)
        if mode == "nn":
            in_specs += [pl.BlockSpec((tm, tk), lambda i, j, k, kk=kk: (i, kk(k))),
                         pl.BlockSpec((tk, tn), lambda i, j, k, kk=kk: (kk(k), j))]
        elif mode == "nt":
            in_specs += [pl.BlockSpec((tm, tk), lambda i, j, k, kk=kk: (i, kk(k))),
                         pl.BlockSpec((tn, tk), lambda i, j, k, kk=kk: (j, kk(k)))]
        else:
            in_specs += [pl.BlockSpec((tk, tm), lambda i, j, k, kk=kk: (kk(k), i)),
                         pl.BlockSpec((tk, tn), lambda i, j, k, kk=kk: (kk(k), j))]
        args += [a, b]
    if res is not None:
        in_specs.append(pl.BlockSpec((tm, tn), lambda i, j, k: (i, j)))
        args.append(res)
    in_specs += [pl.BlockSpec(memory_space=pl.ANY)] * len(deps)
    args += list(deps)

    def body(*refs):
        ab = refs[:2 * npairs]
        res_ref = refs[2 * npairs] if res is not None else None
        o_ref = refs[2 * npairs + (1 if res is not None else 0) + len(deps)]

        def finish(acc):
            if res_ref is not None:
                acc = acc + res_ref[...]
            o_ref[...] = acc.astype(out_dtype)

        if steps == 1:
            finish(dot(ab[0][...], ab[1][...]))
            return
        acc_ref = refs[-1]
        k = pl.program_id(2)

        @pl.when(k == 0)
        def _():
            acc_ref[...] = jnp.zeros_like(acc_ref)

        for p in range(npairs):
            @pl.when((k >= p * nk) & (k < (p + 1) * nk))
            def _(p=p):
                acc_ref[...] += dot(ab[2 * p][...], ab[2 * p + 1][...])

        @pl.when(k == steps - 1)
        def _():
            finish(acc_ref[...])

    return pl.pallas_call(
        body, name=name,
        grid=(m // tm, n // tn, steps),
        in_specs=in_specs,
        out_specs=pl.BlockSpec((tm, tn), lambda i, j, k: (i, j)),
        out_shape=jax.ShapeDtypeStruct((m, n), out_dtype),
        scratch_shapes=[] if steps == 1 else [pltpu.VMEM((tm, tn), F32)],
        compiler_params=_params(("parallel", "parallel", "arbitrary")),
    )(*args)


def _rms_fwd(x, g, name):
    s, d = x.shape

    def body(x_ref, g_ref, o_ref):
        xv = x_ref[...]
        r = lax.rsqrt(jnp.mean(xv * xv, axis=-1, keepdims=True) + EPS)
        o_ref[...] = (xv * r * g_ref[...]).astype(BF16)

    return pl.pallas_call(
        body, name=name, grid=(s // ROW_BLOCK,),
        in_specs=[pl.BlockSpec((ROW_BLOCK, d), lambda i: (i, 0)), pl.BlockSpec((1, d), lambda i: (0, 0))],
        out_specs=pl.BlockSpec((ROW_BLOCK, d), lambda i: (i, 0)),
        out_shape=jax.ShapeDtypeStruct((s, d), BF16),
        compiler_params=_params(("parallel",)),
    )(x, g)


def _rms_bwd(dn, x, g, dres, name, bf16_copy=True):
    s, d = x.shape

    def body(dn_ref, x_ref, g_ref, dres_ref, dx_ref, *rest):
        gg_ref = rest[-1]
        i = pl.program_id(0)
        xv, dnv = x_ref[...], dn_ref[...]
        r = lax.rsqrt(jnp.mean(xv * xv, axis=-1, keepdims=True) + EPS)
        dng = dnv * g_ref[...]
        c = jnp.mean(dng * xv, axis=-1, keepdims=True)
        dx = dres_ref[...] + r * dng - xv * (r * r * r * c)
        dx_ref[...] = dx
        if bf16_copy:
            rest[0][...] = dx.astype(BF16)

        @pl.when(i == 0)
        def _():
            gg_ref[...] = jnp.zeros_like(gg_ref)

        gg_ref[...] += jnp.sum(dnv * xv * r, axis=0, keepdims=True)

    row = pl.BlockSpec((ROW_BLOCK, d), lambda i: (i, 0))
    vec = pl.BlockSpec((1, d), lambda i: (0, 0))
    return pl.pallas_call(
        body, name=name, grid=(s // ROW_BLOCK,),
        in_specs=[row, row, vec, row],
        out_specs=[row] + [row] * bf16_copy + [vec],
        out_shape=[jax.ShapeDtypeStruct((s, d), F32)] + [jax.ShapeDtypeStruct((s, d), BF16)] * bf16_copy
        + [jax.ShapeDtypeStruct((1, d), F32)],
        compiler_params=_params(("arbitrary",)),
    )(dn, x, g, dres)


def _final_loss(h2, target, g, name="final_loss"):
    s, d = h2.shape

    def body(h_ref, t_ref, g_ref, dh_ref, dhb_ref, loss_ref, gg_ref):
        i = pl.program_id(0)
        hv, gv = h_ref[...], g_ref[...]
        r = lax.rsqrt(jnp.mean(hv * hv, axis=-1, keepdims=True) + EPS)
        e = hv * r * gv - t_ref[...]
        dy = e * (1.0 / d)
        dyg = dy * gv
        c = jnp.mean(dyg * hv, axis=-1, keepdims=True)
        dh = r * dyg - hv * (r * r * r * c)
        dh_ref[...] = dh
        dhb_ref[...] = dh.astype(BF16)

        @pl.when(i == 0)
        def _():
            gg_ref[...] = jnp.zeros_like(gg_ref)
            loss_ref[...] = jnp.zeros_like(loss_ref)

        gg_ref[...] += jnp.sum(dy * hv * r, axis=0, keepdims=True)
        loss_ref[...] += jnp.sum(jnp.sum(e * e, axis=-1, keepdims=True), axis=0, keepdims=True) * (0.5 / d)

    row = pl.BlockSpec((ROW_BLOCK, d), lambda i: (i, 0))
    vec = pl.BlockSpec((1, d), lambda i: (0, 0))
    return pl.pallas_call(
        body, name=name, grid=(s // ROW_BLOCK,),
        in_specs=[row, row, vec],
        out_specs=[row, row, pl.BlockSpec((SUBLANES, LANES), lambda i: (0, 0)), vec],
        out_shape=[jax.ShapeDtypeStruct((s, d), F32), jax.ShapeDtypeStruct((s, d), BF16),
                   jax.ShapeDtypeStruct((SUBLANES, LANES), F32), jax.ShapeDtypeStruct((1, d), F32)],
        compiler_params=_params(("arbitrary",)),
    )(h2, target, g)


def _rope_tables(s):
    pos = jnp.arange(s, dtype=F32)
    inv_freq = ROPE_THETA ** (-jnp.arange(0, ROPE_DIM, 2, dtype=F32) / ROPE_DIM)
    ang = pos[:, None] * inv_freq[None, :]
    cos, sin = jnp.cos(ang), jnp.sin(ang)
    half = ROPE_DIM // 2
    rest = HEAD_DIM - ROPE_DIM
    c = jnp.concatenate([cos, cos, jnp.ones((s, rest), F32)], axis=1)
    sm = jnp.concatenate([-sin, jnp.zeros((s, half + rest), F32)], axis=1)
    sp = jnp.concatenate([jnp.zeros((s, half), F32), sin, jnp.zeros((s, rest), F32)], axis=1)
    return c, sm, sp


def _res_shape(s, groups, dil, dtype):
    return jax.ShapeDtypeStruct((s // dil, dil * groups * LANES), dtype)


def _res_spec(groups, dil):
    return pl.BlockSpec((ROW_BLOCK // dil, dil * groups * LANES), lambda i: (i, 0))


def _to_residues(scr, o_ref, dil):
    groups, rows = scr.shape[0], ROW_BLOCK // dil
    for r in range(dil):
        for h in range(groups):
            piece = scr[h] if dil == 1 else scr.at[h][pl.ds(r, rows, stride=dil), :]
            o_ref[:, (r * groups + h) * LANES:(r * groups + h + 1) * LANES] = piece.astype(o_ref.dtype)


def _from_residues(i_ref, scr, dil):
    groups, rows = scr.shape[0], ROW_BLOCK // dil
    for r in range(dil):
        for h in range(groups):
            piece = i_ref[:, (r * groups + h) * LANES:(r * groups + h + 1) * LANES].astype(F32)
            if dil == 1:
                scr[h] = piece
            else:
                scr.at[h][pl.ds(r, rows, stride=dil), :] = piece


def _rope_fwd(proj, tables, name="rope_fwd"):
    s = proj.shape[0]
    half = ROPE_DIM // 2
    nd = len(DILATIONS)

    def body(p_ref, c_ref, sm_ref, sp_ref, *rest):
        outs, scr = rest[:3 * nd], rest[3 * nd]
        c, sm, sp = c_ref[...], sm_ref[...], sp_ref[...]
        for gi, off in enumerate((OFF_AQ, OFF_AK, OFF_AV)):
            for h in range(ATTN_HEADS):
                t = p_ref[:, off + h * HEAD_DIM: off + (h + 1) * HEAD_DIM]
                if off != OFF_AV:
                    t = t * c + pltpu.roll(t, HEAD_DIM - half, 1) * sm + pltpu.roll(t, half, 1) * sp
                scr[h] = t
            for di, dil in enumerate(DILATIONS):
                _to_residues(scr, outs[3 * di + gi], dil)

    tab = pl.BlockSpec((ROW_BLOCK, HEAD_DIM), lambda i: (i, 0))
    outs = pl.pallas_call(
        body, name=name, grid=(s // ROW_BLOCK,),
        in_specs=[pl.BlockSpec((ROW_BLOCK, 3 * ATTN_W), lambda i: (i, 0)), tab, tab, tab],
        out_specs=[_res_spec(ATTN_HEADS, d) for d in DILATIONS for _ in range(3)],
        out_shape=[_res_shape(s, ATTN_HEADS, d, BF16) for d in DILATIONS for _ in range(3)],
        scratch_shapes=[pltpu.VMEM((ATTN_HEADS, ROW_BLOCK, LANES), F32)],
        compiler_params=_params(("parallel",)),
    )(proj, *tables)
    return [tuple(outs[3 * di:3 * di + 3]) for di in range(nd)]


def _rope_bwd(grads, tables, name="rope_bwd"):
    s = grads[0][0].shape[0] * DILATIONS[0]
    half = ROPE_DIM // 2
    nd = len(DILATIONS)

    def body(*refs):
        ins = refs[:3 * nd]
        c_ref, sm_ref, sp_ref, o_ref = refs[3 * nd:3 * nd + 4]
        scrs = refs[3 * nd + 4:]
        c, sm, sp = c_ref[...], sm_ref[...], sp_ref[...]
        for gi, off in enumerate((OFF_AQ, OFF_AK, OFF_AV)):
            for di, dil in enumerate(DILATIONS):
                _from_residues(ins[3 * di + gi], scrs[di], dil)
            for h in range(ATTN_HEADS):
                t = scrs[0][h]
                for scr in scrs[1:]:
                    t = t + scr[h]
                if off != OFF_AV:
                    t = t * c + pltpu.roll(t * sm, half, 1) + pltpu.roll(t * sp, HEAD_DIM - half, 1)
                o_ref[:, off + h * HEAD_DIM: off + (h + 1) * HEAD_DIM] = t.astype(BF16)

    tab = pl.BlockSpec((ROW_BLOCK, HEAD_DIM), lambda i: (i, 0))
    return pl.pallas_call(
        body, name=name, grid=(s // ROW_BLOCK,),
        in_specs=[_res_spec(ATTN_HEADS, d) for d in DILATIONS for _ in range(3)] + [tab, tab, tab],
        out_specs=pl.BlockSpec((ROW_BLOCK, 3 * ATTN_W), lambda i: (i, 0)),
        out_shape=jax.ShapeDtypeStruct((s, IN_PAD), BF16),
        scratch_shapes=[pltpu.VMEM((ATTN_HEADS, ROW_BLOCK, LANES), F32) for _ in DILATIONS],
        compiler_params=_params(("parallel",)),
    )(*[t for g in grads for t in g], *tables)


ATTN_GROUP = 4


def _window_specs(nsteps, width):
    rows, hb = ATTN_GROUP * ATTN_BLOCK, N_SIDE
    per = rows // hb
    cur = pl.BlockSpec((rows, width), lambda r, j: (j, r))
    prev = pl.BlockSpec((hb, width), lambda r, j: (jnp.maximum(per * j - 1, 0), r))
    nxt = pl.BlockSpec((hb, width), lambda r, j: (jnp.minimum(per * (j + 1), per * nsteps - 1), r))
    return prev, cur, nxt


def _block(ref, b, sl):
    return ref[b * ATTN_BLOCK:(b + 1) * ATTN_BLOCK, sl]


def _edge(prev_ref, cur_ref, next_ref, b, sl):
    qb, hb = ATTN_BLOCK, N_SIDE
    before = prev_ref[:, sl] if b == 0 else cur_ref[b * qb - hb:b * qb, sl]
    after = next_ref[:, sl] if b == ATTN_GROUP - 1 else cur_ref[(b + 1) * qb:(b + 1) * qb + hb, sl]
    return jnp.concatenate([before, after], axis=0)


def _band_masks(j, length):
    qb, hb = ATTN_BLOCK, N_SIDE
    row = lax.broadcasted_iota(jnp.int32, (qb, qb), 0)
    col = lax.broadcasted_iota(jnp.int32, (qb, qb), 1)

    def edge_pos(i):
        return j * qb - hb + i + jnp.where(i >= hb, qb, 0)

    def ok(a, b, outside):
        return (jnp.abs(a - b) <= N_SIDE) & (outside >= 0) & (outside < length)

    cur = jnp.abs(row - col) <= N_SIDE
    edge_k = ok(j * qb + row, edge_pos(col), edge_pos(col))
    edge_q = ok(edge_pos(row), j * qb + col, edge_pos(row))
    return cur, edge_k, edge_q


def _attn_fwd(q, k, v, dil, name):
    length = q.shape[0]
    qb = ATTN_BLOCK
    nsteps = length // (ATTN_GROUP * qb)
    scale = HEAD_DIM ** -0.5

    def body(q_ref, kp_ref, kc_ref, kn_ref, vp_ref, vc_ref, vn_ref, o_ref, lse_ref):
        masks = [_band_masks(pl.program_id(1) * ATTN_GROUP + b, length) for b in range(ATTN_GROUP)]
        lane = lax.broadcasted_iota(jnp.int32, (qb, LANES), 1)
        units = [(b, h, slice(h * HEAD_DIM, (h + 1) * HEAD_DIM)) for b in range(ATTN_GROUP)
                 for h in range(ATTN_HEADS)]
        scores = [(_dot_nt(_block(q_ref, b, sl), _block(kc_ref, b, sl)),
                   _dot_nt(_block(q_ref, b, sl), _edge(kp_ref, kc_ref, kn_ref, b, sl))) for b, _, sl in units]
        probs = []
        lse_acc = [jnp.zeros((qb, LANES), F32) for _ in range(ATTN_GROUP)]
        for (b, h, _), (s_c, s_e) in zip(units, scores):
            valid_c, valid_e, _ = masks[b]
            s_c = jnp.where(valid_c, s_c * scale, NEG)
            s_e = jnp.where(valid_e, s_e * scale, NEG)
            m = jnp.max(jnp.maximum(s_c, s_e), axis=-1, keepdims=True)
            p_c, p_e = jnp.exp(s_c - m), jnp.exp(s_e - m)
            den = jnp.sum(p_c + p_e, axis=-1, keepdims=True)
            probs.append((p_c.astype(BF16), p_e.astype(BF16), 1.0 / den))
            lse_acc[b] = jnp.where(lane == h, m + jnp.log(den), lse_acc[b])
        for (b, _, sl), (p_c, p_e, inv) in zip(units, probs):
            o_ref[b * qb:(b + 1) * qb, sl] = (_dot(p_c, _block(vc_ref, b, sl))
                                              + _dot(p_e, _edge(vp_ref, vc_ref, vn_ref, b, sl))) * inv
        for b in range(ATTN_GROUP):
            lse_ref[b * qb:(b + 1) * qb, :] = lse_acc[b]

    prev, cur, nxt = _window_specs(nsteps, ATTN_W)
    return pl.pallas_call(
        body, name=name, grid=(dil, nsteps),
        in_specs=[cur, prev, cur, nxt, prev, cur, nxt],
        out_specs=[cur, pl.BlockSpec((ATTN_GROUP * qb, LANES), lambda r, j: (j, r))],
        out_shape=[jax.ShapeDtypeStruct((length, dil * ATTN_W), F32),
                   jax.ShapeDtypeStruct((length, dil * LANES), F32)],
        compiler_params=_params(("parallel", "parallel")),
    )(q, k, k, k, v, v, v)


def _attn_combine(outs, lses, g, name="attn_combine"):
    s = outs[0].shape[0] * DILATIONS[0]
    nd = len(DILATIONS)

    def body(*refs):
        o_refs, l_refs = refs[:nd], refs[nd:2 * nd]
        g_ref, o_ref, n_ref = refs[2 * nd:2 * nd + 3]
        lse_outs = refs[2 * nd + 3:3 * nd + 3]
        o_scr, l_scr = refs[3 * nd + 3:4 * nd + 3], refs[4 * nd + 3:5 * nd + 3]
        for di, dil in enumerate(DILATIONS):
            _from_residues(o_refs[di], o_scr[di], dil)
            _from_residues(l_refs[di], l_scr[di], dil)
        ls = [scr[0] for scr in l_scr]
        m = ls[0]
        for l in ls[1:]:
            m = jnp.maximum(m, l)
        es = [jnp.exp(l - m) for l in ls]
        z = es[0]
        for e in es[1:]:
            z = z + e
        ws = [e / z for e in es]
        l_scr[0][0] = m + jnp.log(z)
        for di, dil in enumerate(DILATIONS):
            _to_residues(l_scr[0], lse_outs[di], dil)
        ssq = jnp.zeros((ROW_BLOCK, 1), F32)
        for h in range(ATTN_HEADS):
            sl = slice(h * HEAD_DIM, (h + 1) * HEAD_DIM)
            acc = ws[0][:, h:h + 1] * o_scr[0][h]
            for w, scr in zip(ws[1:], o_scr[1:]):
                acc = acc + w[:, h:h + 1] * scr[h]
            o_ref[:, sl] = acc
            ssq = ssq + jnp.sum(acc * acc, axis=-1, keepdims=True)
        r = lax.rsqrt(ssq * (1.0 / ATTN_W) + EPS)
        n_ref[...] = (o_ref[...] * r * g_ref[...]).astype(BF16)

    blk = pl.BlockSpec((ROW_BLOCK, ATTN_W), lambda i: (i, 0))
    outs_ = pl.pallas_call(
        body, name=name, grid=(s // ROW_BLOCK,),
        in_specs=[_res_spec(ATTN_HEADS, d) for d in DILATIONS] + [_res_spec(1, d) for d in DILATIONS]
        + [pl.BlockSpec((1, ATTN_W), lambda i: (0, 0))],
        out_specs=[blk, blk] + [_res_spec(1, d) for d in DILATIONS],
        out_shape=[jax.ShapeDtypeStruct((s, ATTN_W), F32), jax.ShapeDtypeStruct((s, D_MODEL), BF16)]
        + [_res_shape(s, 1, d, F32) for d in DILATIONS],
        scratch_shapes=[pltpu.VMEM((ATTN_HEADS, ROW_BLOCK, LANES), F32) for _ in DILATIONS]
        + [pltpu.VMEM((1, ROW_BLOCK, LANES), F32) for _ in DILATIONS],
        compiler_params=_params(("parallel",)),
    )(*outs, *lses, g)
    return outs_[0], outs_[1], list(outs_[2:])


def _attn_prebwd(dcat, o, g, name="attn_prebwd"):
    s = o.shape[0]
    nd = len(DILATIONS)

    def body(dy_ref, o_ref, g_ref, *rest):
        do_outs, delta_outs, gg_ref = rest[:nd], rest[nd:2 * nd], rest[2 * nd]
        do_scr, delta_scr = rest[2 * nd + 1], rest[2 * nd + 2]
        i = pl.program_id(0)
        dy, ov = dy_ref[...], o_ref[...]
        r = lax.rsqrt(jnp.mean(ov * ov, axis=-1, keepdims=True) + EPS)
        dyg = dy * g_ref[...]
        c = jnp.mean(dyg * ov, axis=-1, keepdims=True)
        do = r * dyg - ov * (r * r * r * c)
        prod = do * ov
        lane = lax.broadcasted_iota(jnp.int32, (ROW_BLOCK, LANES), 1)
        acc = jnp.zeros((ROW_BLOCK, LANES), F32)
        for h in range(ATTN_HEADS):
            sl = slice(h * HEAD_DIM, (h + 1) * HEAD_DIM)
            do_scr[h] = do[:, sl]
            acc = jnp.where(lane == h, jnp.sum(prod[:, sl], axis=-1, keepdims=True), acc)
        delta_scr[0] = acc
        for di, dil in enumerate(DILATIONS):
            _to_residues(do_scr, do_outs[di], dil)
            _to_residues(delta_scr, delta_outs[di], dil)

        @pl.when(i == 0)
        def _():
            gg_ref[...] = jnp.zeros_like(gg_ref)

        gg_ref[...] += jnp.sum(dy * ov * r, axis=0, keepdims=True)

    blk = pl.BlockSpec((ROW_BLOCK, ATTN_W), lambda i: (i, 0))
    vec = pl.BlockSpec((1, ATTN_W), lambda i: (0, 0))
    outs = pl.pallas_call(
        body, name=name, grid=(s // ROW_BLOCK,),
        in_specs=[blk, blk, vec],
        out_specs=[_res_spec(ATTN_HEADS, d) for d in DILATIONS] + [_res_spec(1, d) for d in DILATIONS] + [vec],
        out_shape=[_res_shape(s, ATTN_HEADS, d, BF16) for d in DILATIONS]
        + [_res_shape(s, 1, d, F32) for d in DILATIONS] + [jax.ShapeDtypeStruct((1, ATTN_W), F32)],
        scratch_shapes=[pltpu.VMEM((ATTN_HEADS, ROW_BLOCK, LANES), F32), pltpu.VMEM((1, ROW_BLOCK, LANES), F32)],
        compiler_params=_params(("arbitrary",)),
    )(dcat, o, g)
    return list(outs[:nd]), list(outs[nd:2 * nd]), outs[2 * nd]


def _attn_bwd(q, k, v, do, lse, delta, dil, name):
    length = q.shape[0]
    qb = ATTN_BLOCK
    nsteps = length // (ATTN_GROUP * qb)
    scale = HEAD_DIM ** -0.5

    def body(qp, qc, qn, kp, kc, kn, vp, vc, vn, dop, doc, don, lp, lc, ln, dp, dc, dn, dq_ref, dk_ref, dv_ref):
        masks = [_band_masks(pl.program_id(1) * ATTN_GROUP + b, length) for b in range(ATTN_GROUP)]
        everything = slice(None)
        lse_e = [_edge(lp, lc, ln, b, everything) for b in range(ATTN_GROUP)]
        del_e = [_edge(dp, dc, dn, b, everything) for b in range(ATTN_GROUP)]
        units = [(b, h, slice(h * HEAD_DIM, (h + 1) * HEAD_DIM)) for b in range(ATTN_GROUP)
                 for h in range(ATTN_HEADS)]
        prods = []
        for b, _, sl in units:
            q_c, k_c, v_c, do_c = _block(qc, b, sl), _block(kc, b, sl), _block(vc, b, sl), _block(doc, b, sl)
            q_e, k_e = _edge(qp, qc, qn, b, sl), _edge(kp, kc, kn, b, sl)
            v_e, do_e = _edge(vp, vc, vn, b, sl), _edge(dop, doc, don, b, sl)
            prods.append((_dot_nt(q_c, k_c), _dot_nt(do_c, v_c), _dot_nt(q_c, k_e), _dot_nt(do_c, v_e),
                          _dot_nt(q_e, k_c), _dot_nt(do_e, v_c)))
        parts = []
        for (b, h, _), (s_cc, dp_cc, s_ek, dp_ek, s_eq, dp_eq) in zip(units, prods):
            valid_c, valid_ek, valid_eq = masks[b]
            hc = slice(h, h + 1)
            lse_c, del_c = _block(lc, b, hc), _block(dc, b, hc)
            p_cc = jnp.where(valid_c, jnp.exp(s_cc * scale - lse_c), 0.0)
            ds_cc = (p_cc * (dp_cc - del_c)).astype(BF16)
            p_ek = jnp.where(valid_ek, jnp.exp(s_ek * scale - lse_c), 0.0)
            ds_ek = (p_ek * (dp_ek - del_c)).astype(BF16)
            p_eq = jnp.where(valid_eq, jnp.exp(s_eq * scale - lse_e[b][:, hc]), 0.0)
            ds_eq = (p_eq * (dp_eq - del_e[b][:, hc])).astype(BF16)
            parts.append((p_cc.astype(BF16), ds_cc, ds_ek, p_eq.astype(BF16), ds_eq))
        for (b, _, sl), (p_cc, ds_cc, ds_ek, p_eq, ds_eq) in zip(units, parts):
            rows = slice(b * qb, (b + 1) * qb)
            q_c, k_c, do_c = _block(qc, b, sl), _block(kc, b, sl), _block(doc, b, sl)
            q_e, k_e, do_e = _edge(qp, qc, qn, b, sl), _edge(kp, kc, kn, b, sl), _edge(dop, doc, don, b, sl)
            dq_ref[rows, sl] = ((_dot(ds_cc, k_c) + _dot(ds_ek, k_e)) * scale).astype(BF16)
            dk_ref[rows, sl] = ((_dot_tn(ds_cc, q_c) + _dot_tn(ds_eq, q_e)) * scale).astype(BF16)
            dv_ref[rows, sl] = (_dot_tn(p_cc, do_c) + _dot_tn(p_eq, do_e)).astype(BF16)

    wide, narrow = list(_window_specs(nsteps, ATTN_W)), list(_window_specs(nsteps, LANES))
    return tuple(pl.pallas_call(
        body, name=name, grid=(dil, nsteps),
        in_specs=wide * 4 + narrow * 2,
        out_specs=[wide[1]] * 3,
        out_shape=[jax.ShapeDtypeStruct((length, dil * ATTN_W), BF16)] * 3,
        compiler_params=_params(("parallel", "parallel")),
    )(q, q, q, k, k, k, v, v, v, do, do, do, lse, lse, lse, delta, delta, delta))


def _gate_matrices(gf_up, gb_up):
    pad = LANES - 2 * GLA_RANK
    uf = jnp.concatenate([gf_up, jnp.zeros((GLA_RANK + pad, GLA_KW), gf_up.dtype)], axis=0)
    ub = jnp.concatenate([jnp.zeros((GLA_RANK, GLA_KW), gb_up.dtype), gb_up, jnp.zeros((pad, GLA_KW), gb_up.dtype)], axis=0)
    return uf.astype(BF16), ub.astype(BF16)


def _log_sigmoid(x):
    return jnp.minimum(x, 0.0) - jnp.log(1.0 + jnp.exp(-jnp.abs(x)))


def _gla_gates(proj, uf, ub, gf_b, gb_b, name="gla_gates"):
    s = proj.shape[0]

    def body(z_ref, uf_ref, ub_ref, bf_ref, bb_ref, gf_ref, gb_ref):
        z = z_ref[...].astype(BF16)
        gf_ref[...] = _log_sigmoid(_dot(z, uf_ref[...]) + bf_ref[...]) * (1.0 / GLA_GATE_NORM)
        gb_ref[...] = _log_sigmoid(_dot(z, ub_ref[...]) + bb_ref[...]) * (1.0 / GLA_GATE_NORM)

    mat = pl.BlockSpec((LANES, GLA_KW), lambda i: (0, 0))
    vec = pl.BlockSpec((1, GLA_KW), lambda i: (0, 0))
    out = pl.BlockSpec((ROW_BLOCK, GLA_KW), lambda i: (i, 0))
    return pl.pallas_call(
        body, name=name, grid=(s // ROW_BLOCK,),
        in_specs=[pl.BlockSpec((ROW_BLOCK, LANES), lambda i: (i, OFF_Z // LANES)), mat, mat, vec, vec],
        out_specs=[out, out],
        out_shape=[jax.ShapeDtypeStruct((s, GLA_KW), F32)] * 2,
        compiler_params=_params(("parallel",)),
    )(proj, uf, ub, gf_b, gb_b)


def _gla_gates_bwd(dgf, dgb, proj, uf, ub, gf_b, gb_b, dproj, name="gla_gates_bwd"):
    s = proj.shape[0]
    tail = IN_PAD - OFF_Z

    def body(dgf_ref, dgb_ref, z_ref, uf_ref, ub_ref, bf_ref, bb_ref, _, dz_ref, guf_ref, gub_ref, gbf_ref, gbb_ref):
        i = pl.program_id(0)
        z = z_ref[...].astype(BF16)
        uf_, ub_ = uf_ref[...], ub_ref[...]
        dpf = dgf_ref[...] * (1.0 / GLA_GATE_NORM) * _sigmoid(-(_dot(z, uf_) + bf_ref[...]))
        dpb = dgb_ref[...] * (1.0 / GLA_GATE_NORM) * _sigmoid(-(_dot(z, ub_) + bb_ref[...]))
        dpf_b, dpb_b = dpf.astype(BF16), dpb.astype(BF16)
        dz_ref[:, 0:LANES] = (_dot_nt(dpf_b, uf_) + _dot_nt(dpb_b, ub_)).astype(BF16)
        dz_ref[:, LANES:tail] = jnp.zeros((ROW_BLOCK, tail - LANES), BF16)

        @pl.when(i == 0)
        def _():
            for r in (guf_ref, gub_ref, gbf_ref, gbb_ref):
                r[...] = jnp.zeros_like(r)

        guf_ref[...] += _dot_tn(z, dpf_b)
        gub_ref[...] += _dot_tn(z, dpb_b)
        gbf_ref[...] += jnp.sum(dpf, axis=0, keepdims=True)
        gbb_ref[...] += jnp.sum(dpb, axis=0, keepdims=True)

    mat = pl.BlockSpec((LANES, GLA_KW), lambda i: (0, 0))
    vec = pl.BlockSpec((1, GLA_KW), lambda i: (0, 0))
    blk = pl.BlockSpec((ROW_BLOCK, GLA_KW), lambda i: (i, 0))
    return pl.pallas_call(
        body, name=name, grid=(s // ROW_BLOCK,),
        in_specs=[blk, blk, pl.BlockSpec((ROW_BLOCK, LANES), lambda i: (i, OFF_Z // LANES)), mat, mat, vec, vec,
                  pl.BlockSpec(memory_space=pl.ANY)],
        out_specs=[pl.BlockSpec((ROW_BLOCK, tail), lambda i: (i, OFF_Z // tail)), mat, mat, vec, vec],
        out_shape=[jax.ShapeDtypeStruct(dproj.shape, dproj.dtype), jax.ShapeDtypeStruct((LANES, GLA_KW), F32),
                   jax.ShapeDtypeStruct((LANES, GLA_KW), F32), jax.ShapeDtypeStruct((1, GLA_KW), F32),
                   jax.ShapeDtypeStruct((1, GLA_KW), F32)],
        input_output_aliases={7: 0},
        compiler_params=_params(("arbitrary",)),
    )(dgf, dgb, proj, uf, ub, gf_b, gb_b, dproj)


def _split3(x):
    x1 = x.astype(BF16)
    r1 = x - x1.astype(F32)
    x2 = r1.astype(BF16)
    x3 = (r1 - x2.astype(F32)).astype(BF16)
    return x1, x2, x3


def _dot_exact(mask_bf, x):
    x1, x2, x3 = _split3(x)
    return _dot(mask_bf, x1) + _dot(mask_bf, x2) + _dot(mask_bf, x3)


def _chunk_masks(reverse):
    c = GLA_CHUNK
    row = lax.broadcasted_iota(jnp.int32, (c, c), 0)
    col = lax.broadcasted_iota(jnp.int32, (c, c), 1)
    allowed = (col >= row) if reverse else (col <= row)
    seen_by = (col <= row) if reverse else (col >= row)
    return allowed, seen_by


def _chunk_terms(q_ref, k_ref, g_ref, rs, hs, allowed, reverse):
    c = GLA_CHUNK
    mid, last = (c // 2, 0) if reverse else (c // 2 - 1, c - 1)
    q = q_ref[rs, hs] * (GLA_DK ** -0.5)
    k = k_ref[rs, hs]
    b = _dot_exact(jnp.where(allowed, 1.0, 0.0).astype(BF16), g_ref[rs, hs])
    bref, blast = b[mid:mid + 1, :], b[last:last + 1, :]
    e_q, e_k, e_in, e_st = jnp.exp(b - bref), jnp.exp(bref - b), jnp.exp(b), jnp.exp(blast - b)
    return dict(last=last, e_q=e_q, e_k=e_k, e_in=e_in, e_st=e_st,
                dec=jnp.exp(blast), qe=q * e_q, ke=k * e_k, qin=q * e_in, kst=k * e_st)


def _gla_blockspecs(s, reverse_order):
    cb = GLA_CHUNKS_PER_STEP
    rows = cb * GLA_CHUNK
    nsteps = s // rows

    def rb(n):
        return (nsteps - 1 - n) if reverse_order else n

    qspec = pl.BlockSpec((rows, GLA_KW), lambda n: (rb(n), OFF_GQ // GLA_KW))
    kspec = pl.BlockSpec((rows, GLA_KW), lambda n: (rb(n), OFF_GK // GLA_KW))
    vspec = pl.BlockSpec((rows, GLA_VW), lambda n: (rb(n), OFF_GV // GLA_VW))
    gspec = pl.BlockSpec((rows, GLA_KW), lambda n: (rb(n), 0))
    ospec = pl.BlockSpec((rows, GLA_VW), lambda n: (rb(n), 0))
    sspec = pl.BlockSpec((GLA_HEADS, cb, GLA_DV, GLA_DK), lambda n: (0, rb(n), 0, 0))
    return cb, rows, nsteps, qspec, kspec, vspec, gspec, ospec, sspec


def _gla_units(cb, order_reversed):
    chunks = list(reversed(range(cb))) if order_reversed else list(range(cb))
    return [(c, h, slice(c * GLA_CHUNK, (c + 1) * GLA_CHUNK), slice(h * GLA_DK, (h + 1) * GLA_DK),
             slice(h * GLA_DV, (h + 1) * GLA_DV)) for c in chunks for h in range(GLA_HEADS)]


def _gla_fwd(proj, g, reverse, name):
    s = proj.shape[0]
    cb, rows, nsteps, qspec, kspec, vspec, gspec, ospec, sspec = _gla_blockspecs(s, reverse)

    def body(q_ref, k_ref, v_ref, g_ref, o_ref, st_ref, state):
        @pl.when(pl.program_id(0) == 0)
        def _():
            state[...] = jnp.zeros_like(state)

        allowed, _ = _chunk_masks(reverse)
        units = _gla_units(cb, reverse)
        terms = [_chunk_terms(q_ref, k_ref, g_ref, rs, hs, allowed, reverse) for _, _, rs, hs, _ in units]
        vals = [v_ref[rs, vs].astype(BF16) for _, _, rs, _, vs in units]
        raw = [(_dot_nt(t["qe"].astype(BF16), t["ke"].astype(BF16)), _dot_tn(v, t["kst"].astype(BF16)))
               for t, v in zip(terms, vals)]
        intra = [_dot(jnp.where(allowed, a, 0.0).astype(BF16), v) for (a, _), v in zip(raw, vals)]
        st = [state[h] for h in range(GLA_HEADS)]
        for (c, h, rs, _, vs), t, (_, kv), o_in in zip(units, terms, raw, intra):
            st_b = st[h].astype(BF16)
            st_ref[h, c] = st_b
            o_ref[rs, vs] = o_in + _dot_nt(t["qin"].astype(BF16), st_b)
            st[h] = st[h] * t["dec"] + kv
        for h in range(GLA_HEADS):
            state[h] = st[h]

    return pl.pallas_call(
        body, name=name, grid=(nsteps,),
        in_specs=[qspec, kspec, vspec, gspec],
        out_specs=[ospec, sspec],
        out_shape=[jax.ShapeDtypeStruct((s, GLA_VW), F32),
                   jax.ShapeDtypeStruct((GLA_HEADS, s // GLA_CHUNK, GLA_DV, GLA_DK), BF16)],
        scratch_shapes=[pltpu.VMEM((GLA_HEADS, GLA_DV, GLA_DK), F32)],
        compiler_params=_params(("arbitrary",)),
    )(proj, proj, proj, g)


def _gla_bwd(proj, g, do, states, reverse, name, merge=None):
    s = proj.shape[0]
    cb, rows, nsteps, qspec, kspec, vspec, gspec, ospec, sspec = _gla_blockspecs(s, not reverse)
    gla_cols = OFF_Z - OFF_GQ

    def body(q_ref, k_ref, v_ref, g_ref, do_ref, sp_ref, *rest):
        if merge is None:
            dq_ref, dk_ref, dv_ref, dg_ref, dstate = rest
        else:
            dq_o, dk_o, dv_o, dgr_ref, _, dp_ref, dg_ref, dstate = rest
        @pl.when(pl.program_id(0) == 0)
        def _():
            dstate[...] = jnp.zeros_like(dstate)

        allowed, seen_by = _chunk_masks(reverse)
        units = _gla_units(cb, not reverse)
        terms = [_chunk_terms(q_ref, k_ref, g_ref, rs, hs, allowed, reverse) for _, _, rs, hs, _ in units]
        vals = [v_ref[rs, vs].astype(BF16) for _, _, rs, _, vs in units]
        dos = [do_ref[rs, vs] for _, _, rs, _, vs in units]
        prevs = [sp_ref[h, c] for c, h, _, _, _ in units]
        raw = [(_dot_nt(t["qe"].astype(BF16), t["ke"].astype(BF16)), _dot_nt(do, v),
                _dot(do, sp), _dot_tn(do, t["qin"].astype(BF16)))
               for t, v, do, sp in zip(terms, vals, dos, prevs)]
        inner = []
        for t, do, (a, da, _, _) in zip(terms, dos, raw):
            da = jnp.where(allowed, da, 0.0).astype(BF16)
            inner.append((_dot(da, t["ke"].astype(BF16)), _dot_tn(da, t["qe"].astype(BF16)),
                          _dot_tn(jnp.where(allowed, a, 0.0).astype(BF16), do)))
        ds = [dstate[h] for h in range(GLA_HEADS)]
        outer = []
        for (c, h, _, _, _), t, v, sp, (_, _, _, inc) in zip(units, terms, vals, prevs, raw):
            ds_b = ds[h].astype(BF16)
            outer.append((_dot(v, ds_b), _dot_nt(t["kst"].astype(BF16), ds_b),
                          jnp.sum(sp.astype(F32) * ds[h], axis=0, keepdims=True)))
            ds[h] = ds[h] * t["dec"] + inc
        for h in range(GLA_HEADS):
            dstate[h] = ds[h]
        seen_bf = jnp.where(seen_by, 1.0, 0.0).astype(BF16)
        rowi = lax.broadcasted_iota(jnp.int32, (GLA_CHUNK, GLA_DK), 0)
        for (c, h, rs, hs, vs), t, (_, _, dqin, _), (dqe, dke, dv_in), (dkst, dv_out, ddec) in zip(
                units, terms, raw, inner, outer):
            dq = (dqe * t["e_q"] + dqin * t["e_in"]) * (GLA_DK ** -0.5)
            dk = dke * t["e_k"] + dkst * t["e_st"]
            if merge is None:
                dq_ref[rs, hs], dk_ref[rs, hs], dv_ref[rs, vs] = dq, dk, dv_in + dv_out
            else:
                lo = OFF_GK - OFF_GQ + h * GLA_DK
                dp_ref[rs, hs] = (dq + dq_o[rs, hs]).astype(BF16)
                dp_ref[rs, lo:lo + GLA_DK] = (dk + dk_o[rs, hs]).astype(BF16)
                lo = OFF_GV - OFF_GQ + h * GLA_DV
                dp_ref[rs, lo:lo + GLA_DV] = (dv_in + dv_out + dv_o[rs, vs]).astype(BF16)
            kk = dkst * t["kst"]
            db = dqe * t["qe"] - dke * t["ke"] + dqin * t["qin"] - kk
            extra = jnp.sum(kk, axis=0, keepdims=True) + ddec * t["dec"]
            db = db + jnp.where(rowi == t["last"], extra, 0.0)
            dg_ref[rs, hs] = _dot_exact(seen_bf, db)
        if merge is not None:
            dp_ref[:, OFF_GR - OFF_GQ:gla_cols] = dgr_ref[...]

    scratch = [pltpu.VMEM((GLA_HEADS, GLA_DV, GLA_DK), F32)]
    if merge is None:
        return pl.pallas_call(
            body, name=name, grid=(nsteps,),
            in_specs=[qspec, kspec, vspec, gspec, ospec, sspec],
            out_specs=[gspec, gspec, ospec, gspec],
            out_shape=[jax.ShapeDtypeStruct((s, GLA_KW), F32), jax.ShapeDtypeStruct((s, GLA_KW), F32),
                       jax.ShapeDtypeStruct((s, GLA_VW), F32), jax.ShapeDtypeStruct((s, GLA_KW), F32)],
            scratch_shapes=scratch,
            compiler_params=_params(("arbitrary",)),
        )(proj, proj, proj, g, do, states)
    dproj = merge[4]
    block = gspec.index_map
    return pl.pallas_call(
        body, name=name, grid=(nsteps,),
        in_specs=[qspec, kspec, vspec, gspec, ospec, sspec, gspec, gspec, ospec, ospec, _ANY],
        out_specs=[pl.BlockSpec((rows, gla_cols), lambda n: (block(n)[0], OFF_GQ // gla_cols)), gspec],
        out_shape=[jax.ShapeDtypeStruct(dproj.shape, dproj.dtype), jax.ShapeDtypeStruct((s, GLA_KW), F32)],
        input_output_aliases={10: 0},
        scratch_shapes=scratch,
        compiler_params=_params(("arbitrary",)),
    )(proj, proj, proj, g, do, states, *merge)


def _gla_post(o_f, o_b, proj, g, cat, name="gla_post"):
    s = o_f.shape[0]

    def body(of_ref, ob_ref, gr_ref, g_ref, _, o_ref):
        gv = g_ref[...]
        for h in range(GLA_HEADS):
            sl = slice(h * GLA_DV, (h + 1) * GLA_DV)
            osum = of_ref[:, sl] + ob_ref[:, sl]
            r = lax.rsqrt(jnp.mean(osum * osum, axis=-1, keepdims=True) + EPS)
            gr = gr_ref[:, sl]
            o_ref[:, sl] = (osum * r * gv * (gr * _sigmoid(gr))).astype(BF16)

    blk = pl.BlockSpec((ROW_BLOCK, GLA_VW), lambda i: (i, 0))
    return pl.pallas_call(
        body, name=name, grid=(s // ROW_BLOCK,),
        in_specs=[blk, blk, pl.BlockSpec((ROW_BLOCK, GLA_VW), lambda i: (i, OFF_GR // GLA_VW)),
                  pl.BlockSpec((1, GLA_DV), lambda i: (0, 0)), pl.BlockSpec(memory_space=pl.ANY)],
        out_specs=pl.BlockSpec((ROW_BLOCK, GLA_VW), lambda i: (i, ATTN_W // GLA_VW)),
        out_shape=jax.ShapeDtypeStruct(cat.shape, cat.dtype),
        input_output_aliases={4: 0},
        compiler_params=_params(("parallel",)),
    )(o_f, o_b, proj, g, cat)


def _gla_post_bwd(dcat, o_f, o_b, proj, g, name="gla_post_bwd"):
    s = o_f.shape[0]

    def body(dy_ref, of_ref, ob_ref, gr_ref, g_ref, do_ref, dgr_ref, gg_ref):
        i = pl.program_id(0)
        gv = g_ref[...]
        gg = jnp.zeros((1, GLA_DV), F32)
        for h in range(GLA_HEADS):
            sl = slice(h * GLA_DV, (h + 1) * GLA_DV)
            osum = of_ref[:, sl] + ob_ref[:, sl]
            r = lax.rsqrt(jnp.mean(osum * osum, axis=-1, keepdims=True) + EPS)
            gr, dy = gr_ref[:, sl], dy_ref[:, sl]
            sg = _sigmoid(gr)
            dgr_ref[:, sl] = (dy * (osum * r * gv) * (sg * (1.0 + gr * (1.0 - sg)))).astype(BF16)
            dn = dy * (gr * sg)
            dng = dn * gv
            c = jnp.mean(dng * osum, axis=-1, keepdims=True)
            do_ref[:, sl] = (r * dng - osum * (r * r * r * c)).astype(BF16)
            gg = gg + jnp.sum(dn * osum * r, axis=0, keepdims=True)

        @pl.when(i == 0)
        def _():
            gg_ref[...] = jnp.zeros_like(gg_ref)

        gg_ref[...] += gg

    blk = pl.BlockSpec((ROW_BLOCK, GLA_VW), lambda i: (i, 0))
    vec = pl.BlockSpec((1, GLA_DV), lambda i: (0, 0))
    return pl.pallas_call(
        body, name=name, grid=(s // ROW_BLOCK,),
        in_specs=[pl.BlockSpec((ROW_BLOCK, GLA_VW), lambda i: (i, 1)), blk, blk,
                  pl.BlockSpec((ROW_BLOCK, GLA_VW), lambda i: (i, OFF_GR // GLA_VW)), vec],
        out_specs=[blk, blk, vec],
        out_shape=[jax.ShapeDtypeStruct((s, GLA_VW), BF16), jax.ShapeDtypeStruct((s, GLA_VW), BF16),
                   jax.ShapeDtypeStruct((1, GLA_DV), F32)],
        compiler_params=_params(("arbitrary",)),
    )(dcat, o_f, o_b, proj, g)


HALO = 16


def _extended(prev_ref, cur_ref, next_ref, i, s, tr, cs):
    first, last = i == 0, i == s // tr - 1
    prev = jnp.where(first, 0.0, prev_ref[:, cs].astype(F32))
    nxt = jnp.where(last, 0.0, next_ref[:, cs].astype(F32))
    return jnp.concatenate([prev, cur_ref[:, cs].astype(F32), nxt], axis=0)


FFN_ROWS = 512
FFN_COLS = 512


FFN_CHUNK = 256


def _lagged(i, ni, multiply, finish, rotate, init):
    chunks = [slice(c, c + FFN_CHUNK) for c in range(0, FFN_COLS, FFN_CHUNK)]

    @pl.when(i == 0)
    def _():
        init()

    @pl.when(i < 2)
    def _():
        rotate([multiply(cs) for cs in chunks], chunks)

    @pl.when((i >= 2) & (i < ni))
    def _():
        new = []
        for cs in chunks:
            new.append(multiply(cs))
            finish(cs)
        rotate(new, chunks)

    @pl.when(i >= ni)
    def _():
        for cs in chunks:
            finish(cs)
        rotate(None, chunks)


def _ffn_in(n2, w_gate, w_up, conv_w, conv_b, name="ffn_in"):
    s, d = n2.shape
    f = w_gate.shape[1]
    tm, tn, edge = FFN_ROWS, FFN_COLS, SUBLANES
    ni = s // tm
    ext = tm + 2 * edge

    def body(a_ref, wg_ref, wu_ref, w_ref, b_ref, gate_ref, silu_ref, slope_ref, act_ref, g_tile, u_tile, g_tail):
        i = pl.program_id(1)

        @pl.when(i == 0)
        def _():
            g_tile[...] = jnp.zeros_like(g_tile)
            u_tile[...] = jnp.zeros_like(u_tile)
            g_tail[...] = jnp.zeros_like(g_tail)

        a = a_ref[...]
        g_new = _dot(a, wg_ref[...])
        u_new = _dot(a, wu_ref[...])
        g_old, u_old = g_tile[...], u_tile[...]
        before = jnp.where(i == 1, 0.0, g_tail[...])
        after = jnp.where(i == ni, 0.0, g_new[0:edge])
        ge = jnp.concatenate([before, g_old, after], axis=0)
        w = w_ref[...]
        conv = (w[0:1] * pltpu.roll(ge, 1, 0) + w[1:2] * ge + w[2:3] * pltpu.roll(ge, ext - 1, 0))[edge:edge + tm]
        conv = conv + b_ref[...]
        sg = _sigmoid(conv)
        silu = conv * sg
        u_f = u_old.astype(F32)
        gate_ref[...] = g_old
        silu_ref[...] = silu.astype(BF16)
        slope_ref[...] = (u_f * (sg * (1.0 + conv * (1.0 - sg)))).astype(BF16)
        act_ref[...] = (silu * u_f).astype(BF16)
        g_tail[...] = g_old[tm - edge:tm]
        g_tile[...] = g_new
        u_tile[...] = u_new.astype(BF16)

    lag = pl.BlockSpec((tm, tn), lambda j, i: (jnp.maximum(i - 1, 0), j))
    return pl.pallas_call(
        body, name=name, grid=(f // tn, ni + 1),
        in_specs=[pl.BlockSpec((tm, d), lambda j, i: (jnp.minimum(i, ni - 1), 0)),
                  pl.BlockSpec((d, tn), lambda j, i: (0, j)), pl.BlockSpec((d, tn), lambda j, i: (0, j)),
                  pl.BlockSpec((3, tn), lambda j, i: (0, j)), pl.BlockSpec((1, tn), lambda j, i: (0, j))],
        out_specs=[lag, lag, lag, lag],
        out_shape=[jax.ShapeDtypeStruct((s, f), F32)] + [jax.ShapeDtypeStruct((s, f), BF16)] * 3,
        scratch_shapes=[pltpu.VMEM((tm, tn), F32), pltpu.VMEM((tm, tn), BF16), pltpu.VMEM((edge, tn), F32)],
        compiler_params=_params(("parallel", "arbitrary")),
    )(n2, w_gate, w_up, conv_w, conv_b)


def _ffn_mid_bwd(dh2, w_down, gate, silu, slope, conv_w, name="ffn_mid_bwd"):
    s, d = dh2.shape
    f = gate.shape[1]
    tm, tn = FFN_ROWS, FFN_COLS
    ni = s // tm
    ext = tm + 2 * HALO
    per, last_halo = tm // HALO, s // HALO - 1

    def body(a_ref, wd_ref, gp, gc, gn, sp, sc, sn, silu_ref, w_ref, dg_ref, du_ref, gw_ref, gb_ref,
             d_near, d_far, d_tail):
        i = pl.program_id(1)

        def multiply(cs):
            return _dot_nt(a_ref[...], wd_ref[cs, :])

        def finish(cs):
            before = jnp.where(i == 2, 0.0, d_tail[:, cs])
            after = jnp.where(i == ni + 1, 0.0, d_near[0:HALO, cs])
            d_mid = d_far[:, cs]
            de = jnp.concatenate([before, d_mid, after], axis=0)
            ge = _extended(gp, gc, gn, i - 2, s, tm, cs)
            w = w_ref[:, cs]
            g_prev, g_next = pltpu.roll(ge, 1, 0), pltpu.roll(ge, ext - 1, 0)
            inner = slice(HALO, HALO + tm)
            du_ref[:, cs] = (d_mid * silu_ref[:, cs].astype(F32)).astype(BF16)
            dconv = de * _extended(sp, sc, sn, i - 2, s, tm, cs)
            dgate = w[0:1] * pltpu.roll(dconv, ext - 1, 0) + w[1:2] * dconv + w[2:3] * pltpu.roll(dconv, 1, 0)
            dg_ref[:, cs] = dgate[inner].astype(BF16)
            dci = dconv[inner]
            gw_ref[0:1, cs] += jnp.sum(dci * g_prev[inner], axis=0, keepdims=True)
            gw_ref[1:2, cs] += jnp.sum(dci * ge[inner], axis=0, keepdims=True)
            gw_ref[2:3, cs] += jnp.sum(dci * g_next[inner], axis=0, keepdims=True)
            gb_ref[:, cs] += jnp.sum(dci, axis=0, keepdims=True)

        def rotate(new, chunks):
            d_tail[...] = d_far[tm - HALO:tm]
            d_far[...] = d_near[...]
            if new is not None:
                for cs, d_new in zip(chunks, new):
                    d_near[:, cs] = d_new

        def init():
            for r in (d_near, d_far, d_tail, gw_ref, gb_ref):
                r[...] = jnp.zeros_like(r)

        _lagged(i, ni, multiply, finish, rotate, init)

    def tile(i):
        return jnp.maximum(i - 2, 0)

    cur = pl.BlockSpec((tm, tn), lambda j, i: (tile(i), j))
    prev = pl.BlockSpec((HALO, tn), lambda j, i: (jnp.maximum(tile(i) * per - 1, 0), j))
    nxt = pl.BlockSpec((HALO, tn), lambda j, i: (jnp.minimum((tile(i) + 1) * per, last_halo), j))
    wspec = pl.BlockSpec((3, tn), lambda j, i: (0, j))
    bspec = pl.BlockSpec((1, tn), lambda j, i: (0, j))
    return pl.pallas_call(
        body, name=name, grid=(f // tn, ni + 2),
        in_specs=[pl.BlockSpec((tm, d), lambda j, i: (jnp.minimum(i, ni - 1), 0)),
                  pl.BlockSpec((tn, d), lambda j, i: (j, 0))] + [prev, cur, nxt] * 2 + [cur, wspec],
        out_specs=[cur, cur, wspec, bspec],
        out_shape=[jax.ShapeDtypeStruct((s, f), BF16), jax.ShapeDtypeStruct((s, f), BF16),
                   jax.ShapeDtypeStruct((3, f), F32), jax.ShapeDtypeStruct((1, f), F32)],
        scratch_shapes=[pltpu.VMEM((tm, tn), F32), pltpu.VMEM((tm, tn), F32), pltpu.VMEM((HALO, tn), F32)],
        compiler_params=_params(("parallel", "arbitrary")),
    )(dh2, w_down, gate, gate, gate, slope, slope, slope, silu, conv_w)


def _local_step(x, target, w, late_weights=None, grad_sink=None, first_dep=()):
    s = x.shape[0]
    tables = _rope_tables(s)
    uf, ub = _gate_matrices(w["gf_up"], w["gb_up"])
    if grad_sink is None:
        grad_sink = lambda names, grads: ()

    n1 = _rms_fwd(x, w["norm1_g"], "norm1")
    proj = _matmul([(n1, w["w_in"])], "nn", F32, 1024, 1280, D_MODEL, "in_proj", deps=first_dep)
    qkv = _rope_fwd(proj, tables)
    branches = [_attn_fwd(*qkv[di], d, f"attn_fwd_d{d}") for di, d in enumerate(DILATIONS)]
    o_mix, ao, lse = _attn_combine([b[0] for b in branches], [b[1] for b in branches], w["attn_norm_g"])
    g_f, g_b = _gla_gates(proj, uf, ub, w["gf_b"], w["gb_b"])
    o_f, st_f = _gla_fwd(proj, g_f, False, "gla_fwd_f")
    o_b, st_b = _gla_fwd(proj, g_b, True, "gla_fwd_b")
    cat = _gla_post(o_f, o_b, proj, w["gla_norm_g"], ao)
    if late_weights is not None:
        w = {**w, **late_weights("mixer", cat)}
    h1 = _matmul([(cat, w["w_out"])], "nn", F32, 1024, 1024, D_MODEL, "out_proj", res=x)
    n2 = _rms_fwd(h1, w["norm2_g"], "norm2")
    if late_weights is not None:
        w = {**w, **late_weights("ffn", n2)}
    gate, silu, slope, act = _ffn_in(n2, w["w_gate"], w["w_up"], w["conv_w"], w["conv_b"])
    h2 = _matmul([(act, w["w_down"])], "nn", F32, 1024, 1024, 2816, "ffn_down", res=h1)
    dh2, dh2_b, loss_acc, g_final = _final_loss(h2, target, w["final_norm_g"])

    g_w_down = _matmul([(act, dh2_b)], "tn", BF16, 1408, 1024, 2048, "g_w_down")
    dep = grad_sink(["w_down"], [g_w_down])
    dgate, dup, g_conv_w, g_conv_b = _ffn_mid_bwd(dh2_b, w["w_down"], gate, silu, slope, w["conv_w"])
    g_w_gate = _matmul([(n2, dgate)], "tn", BF16, 2048, 512, 2048, "g_w_gate", deps=dep)
    g_w_up = _matmul([(n2, dup)], "tn", BF16, 2048, 512, 2048, "g_w_up")
    dep = grad_sink(["w_gate", "w_up"], [g_w_gate, g_w_up])
    dn2 = _matmul([(dgate, w["w_gate"])], "nt", F32, 1024, 1024, 2816, "d_n2_gate", deps=dep)
    dn2 = _matmul([(dup, w["w_up"])], "nt", F32, 1024, 1024, 2816, "d_n2_up", res=dn2)
    dh1, dh1_b, g_norm2 = _rms_bwd(dn2, h1, w["norm2_g"], dh2, "norm2_bwd")

    g_w_out = _matmul([(cat, dh1_b)], "tn", BF16, 1024, 1024, 2048, "g_w_out")
    dep = grad_sink(["w_out"], [g_w_out])
    dcat = _matmul([(dh1_b, w["w_out"])], "nt", F32, 1024, 1024, D_MODEL, "d_cat", deps=dep)
    do_attn, delta, g_attn_norm = _attn_prebwd(dcat, o_mix, w["attn_norm_g"])
    grads = [_attn_bwd(*qkv[di], do_attn[di], lse[di], delta[di], d, f"attn_bwd_d{d}")
             for di, d in enumerate(DILATIONS)]
    dproj = _rope_bwd(grads, tables)
    do_gla, dgr, g_gla_norm = _gla_post_bwd(dcat, o_f, o_b, proj, w["gla_norm_g"])
    dq_f, dk_f, dv_f, dg_f = _gla_bwd(proj, g_f, do_gla, st_f, False, "gla_bwd_f")
    dproj, dg_b = _gla_bwd(proj, g_b, do_gla, st_b, True, "gla_bwd_b", merge=(dq_f, dk_f, dv_f, dgr, dproj))
    dproj, g_uf, g_ub, g_gf_b, g_gb_b = _gla_gates_bwd(dg_f, dg_b, proj, uf, ub, w["gf_b"], w["gb_b"], dproj)
    g_w_in = _matmul([(n1, dproj)], "tn", BF16, 1024, 1280, 2048, "g_w_in")
    dep = grad_sink(["w_in"], [g_w_in])
    dn1 = _matmul([(dproj, w["w_in"])], "nt", F32, 1024, 2048, 1280, "d_n1", deps=dep)
    grad_x, g_norm1 = _rms_bwd(dn1, x, w["norm1_g"], dh1, "norm1_bwd", bf16_copy=False)

    g = dict(norm1_g=g_norm1, w_in=g_w_in, gf_up=g_uf[:GLA_RANK], gf_b=g_gf_b,
             gb_up=g_ub[GLA_RANK:2 * GLA_RANK], gb_b=g_gb_b, gla_norm_g=g_gla_norm, attn_norm_g=g_attn_norm,
             w_out=g_w_out, norm2_g=g_norm2, w_gate=g_w_gate, w_up=g_w_up, conv_w=g_conv_w, conv_b=g_conv_b,
             w_down=g_w_down, final_norm_g=g_final)
    return loss_acc, grad_x, g


def _me_and_peers():
    x, y, c = lax.axis_index("x"), lax.axis_index("y"), lax.axis_index("c")
    me = 4 * x + 2 * y + c
    peers = []
    for kbits in range(1, N_DEV):
        px, py, pc = x ^ (kbits >> 2 & 1), y ^ (kbits >> 1 & 1), c ^ (kbits & 1)
        peers.append(((px, py, pc), 4 * px + 2 * py + pc))
    return me, peers


_HBM = pl.BlockSpec(memory_space=pltpu.HBM)
_SEM = pl.BlockSpec(memory_space=pltpu.SEMAPHORE)
_ANY = pl.BlockSpec(memory_space=pl.ANY)
_EFFECT = pltpu.SideEffectType.DATAFLOW_SIDE_EFFECTING


def _exchange_copies(src_refs, land_refs, send_sems, recv_sems, scatter):
    me, peers = _me_and_peers()
    out = []
    for a, (src, land) in enumerate(zip(src_refs, land_refs)):
        for kk, (dev, idx) in enumerate(peers):
            out.append(pltpu.make_async_remote_copy(
                src_ref=src.at[idx] if scatter else src, dst_ref=land.at[me],
                send_sem=send_sems.at[a * (N_DEV - 1) + kk], recv_sem=recv_sems.at[a * (N_DEV - 1) + kk],
                device_id=dev, device_id_type=MESH_ID))
    return out


def _exchange_start(srcs, lands, scatter, name, deps=()):
    n, nd = len(srcs), len(deps)

    def body(*refs):
        src_refs, land_refs = refs[:n], refs[n:2 * n]
        send_sems, recv_sems = refs[2 * n + nd:2 * n + nd + 2]
        token = refs[-1]
        for cp in _exchange_copies(src_refs, land_refs, send_sems, recv_sems, scatter):
            cp.start()
        token[...] = jnp.zeros_like(token)

    outs = pl.pallas_call(
        body, name=name,
        in_specs=[_HBM] * (2 * n) + [_ANY] * nd,
        out_specs=[_SEM, _SEM] + [_HBM] * (2 * n) + [pl.BlockSpec(memory_space=pltpu.VMEM)],
        out_shape=[pltpu.SemaphoreType.DMA((n * (N_DEV - 1),)), pltpu.SemaphoreType.DMA((n * (N_DEV - 1),))]
        + [pltpu.HBM(t.shape, t.dtype) for t in srcs] + [pltpu.HBM(t.shape, t.dtype) for t in lands]
        + [jax.ShapeDtypeStruct((SUBLANES, LANES), F32)],
        input_output_aliases={i: 2 + i for i in range(2 * n)},
        compiler_params=pltpu.CompilerParams(has_side_effects=_EFFECT),
    )(*[pltpu.with_memory_space_constraint(t, pltpu.HBM) for t in list(srcs) + list(lands)], *deps)
    send_sems, recv_sems = outs[0], outs[1]
    return dict(send=send_sems, recv=recv_sems, srcs=outs[2:2 + n], lands=outs[2 + n:2 + 2 * n],
                scatter=scatter, token=outs[-1])


def _exchange_wait(started, name, after):
    n = len(started["srcs"])
    scatter = started["scatter"]

    def body(*refs):
        src_refs, land_refs = refs[:n], refs[n:2 * n]
        send_sems, recv_sems = refs[2 * n], refs[2 * n + 1]
        for cp in _exchange_copies(src_refs, land_refs, send_sems, recv_sems, scatter):
            cp.wait_send()
            cp.wait_recv()

    outs = pl.pallas_call(
        body, name=name,
        in_specs=[_HBM] * (2 * n) + [_SEM, _SEM, _ANY],
        out_specs=[_HBM] * (2 * n),
        out_shape=[pltpu.HBM(t.shape, t.dtype) for t in started["srcs"]]
        + [pltpu.HBM(t.shape, t.dtype) for t in started["lands"]],
        input_output_aliases={i: i for i in range(2 * n)},
        compiler_params=pltpu.CompilerParams(has_side_effects=_EFFECT),
    )(*started["srcs"], *started["lands"], started["send"], started["recv"], after)
    return outs[:n], outs[n:]


def _all_gather_two_level(shard, name):
    def body(x_ref, out_ref, send_sems, recv_sems, local_sem):
        x, y, c = lax.axis_index("x"), lax.axis_index("y"), lax.axis_index("c")
        me, sibling = (x, y, c), (x, y, 1 - c)
        chips = [(1 - x, y), (x, 1 - y), (1 - x, 1 - y)]

        def slot(px, py, pc):
            return out_ref.at[4 * px + 2 * py + pc]

        def copy(k, block, to, src=None):
            return pltpu.make_async_remote_copy(
                src_ref=slot(*block) if src is None else src, dst_ref=slot(*block),
                send_sem=send_sems.at[k], recv_sem=recv_sems.at[k], device_id=to, device_id_type=MESH_ID)

        mine = pltpu.make_async_copy(x_ref, slot(*me), local_sem)
        mine.start()
        first = [copy(0, me, sibling, src=x_ref)]
        first += [copy(1 + j, me, (*chip, c), src=x_ref) for j, chip in enumerate(chips)]
        for cp in first:
            cp.start()
        passed = [copy(4 + j, (*chip, c), sibling) for j, chip in enumerate(chips)]
        for j, chip in enumerate(chips):
            copy(1 + j, (*chip, c), me).wait_recv()
            passed[j].start()
        copy(0, sibling, me).wait_recv()
        for j, chip in enumerate(chips):
            copy(4 + j, (*chip, 1 - c), me).wait_recv()
        for cp in first + passed:
            cp.wait_send()
        mine.wait()

    return pl.pallas_call(
        body, name=name,
        in_specs=[_ANY], out_specs=_ANY,
        out_shape=jax.ShapeDtypeStruct((N_DEV,) + shard.shape, shard.dtype),
        scratch_shapes=[pltpu.SemaphoreType.DMA((N_DEV - 1,)), pltpu.SemaphoreType.DMA((N_DEV - 1,)),
                        pltpu.SemaphoreType.DMA],
    )(shard)


def _all_gather_vmem(vec, name):
    r = vec.shape[0]

    def body(v_ref, o_ref, send_sems, recv_sems):
        me, peers = _me_and_peers()
        o_ref[me] = v_ref[...]
        sends = []
        for kk, (dev, _) in enumerate(peers):
            cp = pltpu.make_async_remote_copy(
                src_ref=v_ref, dst_ref=o_ref.at[me],
                send_sem=send_sems.at[kk], recv_sem=recv_sems.at[kk],
                device_id=dev, device_id_type=MESH_ID)
            cp.start()
            sends.append(cp)
        for kk, (dev, idx) in enumerate(peers):
            pltpu.make_async_remote_copy(
                src_ref=v_ref, dst_ref=o_ref.at[idx],
                send_sem=send_sems.at[kk], recv_sem=recv_sems.at[kk],
                device_id=dev, device_id_type=MESH_ID).wait_recv()
        for cp in sends:
            cp.wait_send()

    return pl.pallas_call(
        body, name=name,
        in_specs=[pl.BlockSpec(memory_space=pltpu.VMEM)],
        out_specs=pl.BlockSpec(memory_space=pltpu.VMEM),
        out_shape=jax.ShapeDtypeStruct((N_DEV, r, LANES), F32),
        scratch_shapes=[pltpu.SemaphoreType.DMA((N_DEV - 1,)), pltpu.SemaphoreType.DMA((N_DEV - 1,))],
        compiler_params=pltpu.CompilerParams(vmem_limit_bytes=VMEM_LIMIT),
    )(vec)


def _adamw_math(w, g, m, v):
    m = ADAM_B1 * m + (1.0 - ADAM_B1) * g
    v = ADAM_B2 * v + (1.0 - ADAM_B2) * (g * g)
    m_hat = m / (1.0 - ADAM_B1 ** ADAM_STEP)
    v_hat = v / (1.0 - ADAM_B2 ** ADAM_STEP)
    delta = -ADAM_LR * (m_hat / (jnp.sqrt(v_hat) + ADAM_EPS) + ADAM_WD * w)
    return delta, m, v


def _adamw_sum(parts, w, m, v, tr, name, own=None, me=None):
    r, c = w.shape[-2:]

    def body(*refs):
        if own is None:
            p_ref, w_ref, m_ref, v_ref, g_ref, d_ref, nm_ref, nv_ref = refs
            terms = [p_ref[kk] for kk in range(N_DEV)]
            g = terms[0]
            for t in terms[1:]:
                g = g + t
            g_ref[...] = g
            d_ref[...], nm_ref[...], nv_ref[...] = _adamw_math(w_ref[...], g, m_ref[...], v_ref[...])
            return
        me_ref, p_ref, own_ref, w_ref, m_ref, v_ref, g_ref, d_ref, nm_ref, nv_ref = refs
        terms = [jnp.where(me_ref[0] == kk, own_ref[0], p_ref[kk]).astype(F32) for kk in range(N_DEV)]
        g = terms[0]
        for t in terms[1:]:
            g = g + t
        g_ref[0] = g
        d_ref[0], nm_ref[0], nv_ref[0] = _adamw_math(w_ref[0], g, m_ref[0], v_ref[0])

    out_shape = [jax.ShapeDtypeStruct(w.shape, F32)] * 4
    if own is None:
        blk = pl.BlockSpec((tr, c), lambda i: (i, 0))
        return pl.pallas_call(
            body, name=name, grid=(r // tr,),
            in_specs=[pl.BlockSpec((N_DEV, tr, c), lambda i: (0, i, 0)), blk, blk, blk],
            out_specs=[blk] * 4, out_shape=out_shape,
            compiler_params=_params(("parallel",)),
        )(parts, w, m, v)
    blk = pl.BlockSpec((1, tr, c), lambda i, me_ref: (0, i, 0))
    return pl.pallas_call(
        body, name=name,
        grid_spec=pltpu.PrefetchScalarGridSpec(
            num_scalar_prefetch=1, grid=(r // tr,),
            in_specs=[pl.BlockSpec((N_DEV, tr, c), lambda i, me_ref: (0, i, 0)),
                      pl.BlockSpec((1, tr, c), lambda i, me_ref: (me_ref[0], i, 0)), blk, blk, blk],
            out_specs=[blk] * 4),
        out_shape=out_shape,
        compiler_params=_params(("parallel",)),
    )(jnp.reshape(me, (1,)).astype(jnp.int32), parts, own, w, m, v)


def _slabs_to_wide(slabs, width, name):
    n, r, c = slabs.shape

    def body(i_ref, o_ref):
        for k in range(n):
            o_ref[:, c * k:c * (k + 1)] = i_ref[k]
        if width > n * c:
            o_ref[:, n * c:width] = jnp.zeros((ROW_BLOCK, width - n * c), o_ref.dtype)

    return pl.pallas_call(
        body, name=name, grid=(r // ROW_BLOCK,),
        in_specs=[pl.BlockSpec((n, ROW_BLOCK, c), lambda i: (0, i, 0))],
        out_specs=pl.BlockSpec((ROW_BLOCK, width), lambda i: (i, 0)),
        out_shape=jax.ShapeDtypeStruct((r, width), slabs.dtype),
        compiler_params=_params(("parallel",)),
    )(slabs)


def _wide_to_slabs(wide, c, name):
    r, width = wide.shape

    def body(i_ref, o_ref):
        for k in range(N_DEV):
            o_ref[k] = i_ref[:, c * k:c * (k + 1)]

    return pl.pallas_call(
        body, name=name, grid=(r // ROW_BLOCK,),
        in_specs=[pl.BlockSpec((ROW_BLOCK, width), lambda i: (i, 0))],
        out_specs=pl.BlockSpec((N_DEV, ROW_BLOCK, c), lambda i: (0, i, 0)),
        out_shape=jax.ShapeDtypeStruct((N_DEV, r, c), wide.dtype),
        compiler_params=_params(("parallel",)),
    )(wide)


_SMALL = ("norm1_g", "gf_b", "gb_b", "gla_norm_g", "attn_norm_g", "norm2_g", "conv_b", "final_norm_g",
          "gf_up", "gb_up", "conv_w")


def _pack(named):
    flat = jnp.concatenate([jnp.ravel(t).astype(F32) for t in named])
    tile = SUBLANES * LANES
    total = -(-flat.shape[0] // tile) * tile
    return jnp.pad(flat, (0, total - flat.shape[0])).reshape(total // LANES, LANES)


def _unpack(packed, shapes):
    flat = packed.reshape(-1)
    out, off = [], 0
    for shp in shapes:
        size = int(np.prod(shp))
        out.append(flat[off:off + size].reshape(shp))
        off += size
    return out


def kernel(x, norm1_g, w_in, gf_up, gf_b, gb_up, gb_b, gla_norm_g, attn_norm_g, w_out, norm2_g, w_gate, w_up, conv_w, conv_b, w_down, final_norm_g, loss_target, m_norm1_g, m_w_in, m_gf_up, m_gf_b, m_gb_up, m_gb_b, m_gla_norm_g, m_attn_norm_g, m_w_out, m_norm2_g, m_w_gate, m_w_up, m_conv_w, m_conv_b, m_w_down, m_final_norm_g, v_norm1_g, v_w_in, v_gf_up, v_gf_b, v_gb_up, v_gb_b, v_gla_norm_g, v_attn_norm_g, v_w_out, v_norm2_g, v_w_gate, v_w_up, v_conv_w, v_conv_b, v_w_down, v_final_norm_g):
    names = ("norm1_g", "w_in", "gf_up", "gf_b", "gb_up", "gb_b", "gla_norm_g", "attn_norm_g", "w_out", "norm2_g",
             "w_gate", "w_up", "conv_w", "conv_b", "w_down", "final_norm_g")
    ws = dict(zip(names, (norm1_g, w_in, gf_up, gf_b, gb_up, gb_b, gla_norm_g, attn_norm_g, w_out, norm2_g,
                          w_gate, w_up, conv_w, conv_b, w_down, final_norm_g)))
    ms = dict(zip(names, (m_norm1_g, m_w_in, m_gf_up, m_gf_b, m_gb_up, m_gb_b, m_gla_norm_g, m_attn_norm_g, m_w_out,
                          m_norm2_g, m_w_gate, m_w_up, m_conv_w, m_conv_b, m_w_down, m_final_norm_g)))
    vs = dict(zip(names, (v_norm1_g, v_w_in, v_gf_up, v_gf_b, v_gb_up, v_gb_b, v_gla_norm_g, v_attn_norm_g, v_w_out,
                          v_norm2_g, v_w_gate, v_w_up, v_conv_w, v_conv_b, v_w_down, v_final_norm_g)))
    me = 4 * lax.axis_index("x") + 2 * lax.axis_index("y") + lax.axis_index("c")
    big = ("w_in", "w_out", "w_gate", "w_up", "w_down")
    col_sharded = ("w_in", "w_gate", "w_up")

    def gather_start(group, name, deps=()):
        shards = [ws[n][0].astype(BF16) for n in group]
        lands = [lax.empty((N_DEV,) + t.shape, BF16) for t in shards]
        return _exchange_start(shards, lands, False, name, deps)

    def gather_finish(group, started, name, after):
        full = {}
        for n, own, t in zip(group, *_exchange_wait(started, name, after)):
            t = lax.dynamic_update_slice(t, own[None], (me, 0, 0))
            if n in col_sharded:
                full[n] = _slabs_to_wide(t, N_DEV * t.shape[2], "widen_" + n)
            else:
                full[n] = t.reshape(N_DEV * t.shape[1], t.shape[2])
        return full

    w_in_all = _all_gather_two_level(ws["w_in"][0].astype(BF16), "gather_w_in")
    full = {"w_in": _slabs_to_wide(w_in_all, IN_PAD, "widen_w_in")}
    late = {"mixer": ("w_out",), "ffn": ("w_gate", "w_up", "w_down")}
    started_late = {"mixer": gather_start(late["mixer"], "gather_w_out_start", deps=(full["w_in"],))}
    started_late["ffn"] = gather_start(late["ffn"], "gather_ffn_start", deps=(started_late["mixer"]["token"],))

    def late_weights(part, after):
        return gather_finish(late[part], started_late[part], "gather_" + part + "_wait", after)

    small_sharded = ("gf_up", "gb_up", "conv_w")
    sm = _all_gather_vmem(_pack([ws[n][0] for n in small_sharded]), "gather_small")
    shard_shapes = [ws[n][0].shape for n in small_sharded]
    per_dev = [_unpack(sm[d], shard_shapes) for d in range(N_DEV)]
    for i, n in enumerate(small_sharded):
        full[n] = jnp.concatenate([per_dev[d][i] for d in range(N_DEV)], axis=1)
    for n in ("norm1_g", "gf_b", "gb_b", "gla_norm_g", "attn_norm_g", "norm2_g", "conv_b"):
        full[n] = ws[n]
    full["final_norm_g"] = final_norm_g.reshape(1, D_MODEL)

    in_flight = []

    def grad_sink(group, grads):
        partials = []
        for n, t in zip(group, grads):
            t = t.astype(BF16)
            if n in col_sharded:
                t = _wide_to_slabs(t, ws[n].shape[2], "slabs_" + n)
            else:
                t = t.reshape(N_DEV, t.shape[0] // N_DEV, t.shape[1])
            partials.append(t)
        lands = [lax.empty(t.shape, t.dtype) for t in partials]
        started = _exchange_start(partials, lands, True, "exchange_" + "_".join(group) + "_start")
        in_flight.append((group, started))
        return (started["token"],)

    loss_acc, grad_x, g = _local_step(x[0], loss_target[0], full, late_weights, grad_sink,
                                      first_dep=(started_late["ffn"]["token"],))

    out = {}
    for group, started in in_flight:
        sent, landed = _exchange_wait(started, "exchange_" + "_".join(group) + "_wait", grad_x)
        for n, parts, own in zip(group, landed, sent):
            rows = ws[n].shape[1]
            tr = max(t for t in range(HALO, ROW_BLOCK + 1, HALO) if rows % t == 0)
            out[n] = _adamw_sum(parts, ws[n], ms[n], vs[n], tr, "adamw_" + n, own=own, me=me)

    small_full_shapes = [g[n].shape for n in _SMALL]
    gsmall = _pack([g[n] for n in _SMALL] + [loss_acc[0:1, 0:1]])
    gathered_small = _all_gather_vmem(gsmall, "gather_small_grads")

    def full_small(d):
        parts = []
        for n in _SMALL:
            t = d[n].reshape(d[n].shape[-2:]) if d[n].ndim == 3 else d[n].reshape(1, -1)
            if n in small_sharded:
                wide = jnp.zeros((t.shape[0], t.shape[1] * N_DEV), F32)
                t = lax.dynamic_update_slice_in_dim(wide, t, me * t.shape[1], axis=1)
            parts.append(t)
        return _pack(parts + [jnp.zeros((1, 1), F32)])

    rows = gsmall.shape[0]
    res_small = _adamw_sum(gathered_small, full_small(ws), full_small(ms), full_small(vs), rows, "adamw_small")
    loss = res_small[0].reshape(-1)[sum(int(np.prod(sh)) for sh in small_full_shapes)]
    unpacked = [_unpack(t, small_full_shapes) for t in res_small]
    for i, n in enumerate(_SMALL):
        vals = [u[i] for u in unpacked]
        if n in small_sharded:
            width = vals[0].shape[1] // N_DEV
            vals = [lax.dynamic_slice_in_dim(t, me * width, width, axis=1) for t in vals]
        out[n] = vals

    result = [loss, grad_x[None]]
    for kind in range(4):
        for n in names:
            result.append(out[n][kind].reshape(ws[n].shape))
    return tuple(result)
```

```python
import functools

import numpy as np
import jax
import jax.numpy as jnp
from jax import lax
from jax.experimental import pallas as pl
from jax.experimental.pallas import tpu as pltpu

F32 = jnp.float32
BF16 = jnp.bfloat16

D_MODEL = 2048
ATTN_W = 1024
ATTN_HEADS = 8
HEAD_DIM = 128
ROPE_DIM = 32
ROPE_THETA = 500000.0
DILATIONS = (1, 4, 16)
N_SIDE = 64
GLA_KW = 512
GLA_VW = 1024
GLA_HEADS = 4
GLA_DK = 128
GLA_DV = 256
GLA_RANK = 16
GLA_GATE_NORM = 16.0
GLA_CHUNK = 64
IN_WIDTH = 6176
IN_PAD = 6400
D_FF = 5632
EPS = 1e-6
N_DEV = 8

OFF_AQ, OFF_AK, OFF_AV = 0, 1024, 2048
OFF_GQ, OFF_GK, OFF_GV, OFF_GR, OFF_Z = 3072, 3584, 4096, 5120, 6144

ADAM_LR, ADAM_B1, ADAM_B2, ADAM_EPS, ADAM_WD, ADAM_STEP = 0.001, 0.9, 0.999, 1e-08, 0.01, 10

LANES = 128
SUBLANES = 8
VMEM_LIMIT = 56 * 1024 * 1024
ROW_BLOCK = 256
ATTN_BLOCK = 128
GLA_CHUNKS_PER_STEP = 4
NEG = -1e30
MESH_ID = pl.DeviceIdType.MESH


def _params(sem):
    return pltpu.CompilerParams(dimension_semantics=sem, vmem_limit_bytes=VMEM_LIMIT)


def _dot(a, b):
    return lax.dot_general(a, b, (((1,), (0,)), ((), ())), preferred_element_type=F32)


def _dot_nt(a, b):
    return lax.dot_general(a, b, (((1,), (1,)), ((), ())), preferred_element_type=F32)


def _dot_tn(a, b):
    return lax.dot_general(a, b, (((0,), (0,)), ((), ())), preferred_element_type=F32)


def _sigmoid(x):
    return 0.5 * jnp.tanh(0.5 * x) + 0.5


def _matmul(pairs, mode, out_dtype, tm, tn, tk, name, res=None, deps=()):
    a0, b0 = pairs[0]
    if mode == "nn":
        (m, kdim), n = a0.shape, b0.shape[1]
    elif mode == "nt":
        (m, kdim), n = a0.shape, b0.shape[0]
    else:
        (kdim, m), n = a0.shape, b0.shape[1]
    assert m % tm == 0 and n % tn == 0 and kdim % tk == 0, (name, m, n, kdim)
    nk = kdim // tk
    npairs = len(pairs)
    steps = nk * npairs
    dot = {"nn": _dot, "nt": _dot_nt, "tn": _dot_tn}[mode]

    def kidx(p):
        return lambda k: jnp.clip(k - p * nk, 0, nk - 1)

    in_specs, args = [], []
    for p, (a, b) in enumerate(pairs):
        kk = kidx(p)
        if mode == "nn":
            in_specs += [pl.BlockSpec((tm, tk), lambda i, j, k, kk=kk: (i, kk(k))),
                         pl.BlockSpec((tk, tn), lambda i, j, k, kk=kk: (kk(k), j))]
        elif mode == "nt":
            in_specs += [pl.BlockSpec((tm, tk), lambda i, j, k, kk=kk: (i, kk(k))),
                         pl.BlockSpec((tn, tk), lambda i, j, k, kk=kk: (j, kk(k)))]
        else:
            in_specs += [pl.BlockSpec((tk, tm), lambda i, j, k, kk=kk: (kk(k), i)),
                         pl.BlockSpec((tk, tn), lambda i, j, k, kk=kk: (kk(k), j))]
        args += [a, b]
    if res is not None:
        in_specs.append(pl.BlockSpec((tm, tn), lambda i, j, k: (i, j)))
        args.append(res)
    in_specs += [pl.BlockSpec(memory_space=pl.ANY)] * len(deps)
    args += list(deps)

    def body(*refs):
        ab = refs[:2 * npairs]
        res_ref = refs[2 * npairs] if res is not None else None
        o_ref = refs[2 * npairs + (1 if res is not None else 0) + len(deps)]

        def finish(acc):
            if res_ref is not None:
                acc = acc + res_ref[...]
            o_ref[...] = acc.astype(out_dtype)

        if steps == 1:
            finish(dot(ab[0][...], ab[1][...]))
            return
        acc_ref = refs[-1]
        k = pl.program_id(2)

        @pl.when(k == 0)
        def _():
            acc_ref[...] = jnp.zeros_like(acc_ref)

        for p in range(npairs):
            @pl.when((k >= p * nk) & (k < (p + 1) * nk))
            def _(p=p):
                acc_ref[...] += dot(ab[2 * p][...], ab[2 * p + 1][...])

        @pl.when(k == steps - 1)
        def _():
            finish(acc_ref[...])

    return pl.pallas_call(
        body, name=name,
        grid=(m // tm, n // tn, steps),
        in_specs=in_specs,
        out_specs=pl.BlockSpec((tm, tn), lambda i, j, k: (i, j)),
        out_shape=jax.ShapeDtypeStruct((m, n), out_dtype),
        scratch_shapes=[] if steps == 1 else [pltpu.VMEM((tm, tn), F32)],
        compiler_params=_params(("parallel", "parallel", "arbitrary")),
    )(*args)


def _rms_fwd(x, g, name):
    s, d = x.shape

    def body(x_ref, g_ref, o_ref):
        xv = x_ref[...]
        r = lax.rsqrt(jnp.mean(xv * xv, axis=-1, keepdims=True) + EPS)
        o_ref[...] = (xv * r * g_ref[...]).astype(BF16)

    return pl.pallas_call(
        body, name=name, grid=(s // ROW_BLOCK,),
        in_specs=[pl.BlockSpec((ROW_BLOCK, d), lambda i: (i, 0)), pl.BlockSpec((1, d), lambda i: (0, 0))],
        out_specs=pl.BlockSpec((ROW_BLOCK, d), lambda i: (i, 0)),
        out_shape=jax.ShapeDtypeStruct((s, d), BF16),
        compiler_params=_params(("parallel",)),
    )(x, g)


def _rms_bwd(dn, x, g, dres, name, bf16_copy=True):
    s, d = x.shape

    def body(dn_ref, x_ref, g_ref, dres_ref, dx_ref, *rest):
        gg_ref = rest[-1]
        i = pl.program_id(0)
        xv, dnv = x_ref[...], dn_ref[...]
        r = lax.rsqrt(jnp.mean(xv * xv, axis=-1, keepdims=True) + EPS)
        dng = dnv * g_ref[...]
        c = jnp.mean(dng * xv, axis=-1, keepdims=True)
        dx = dres_ref[...] + r * dng - xv * (r * r * r * c)
        dx_ref[...] = dx
        if bf16_copy:
            rest[0][...] = dx.astype(BF16)

        @pl.when(i == 0)
        def _():
            gg_ref[...] = jnp.zeros_like(gg_ref)

        gg_ref[...] += jnp.sum(dnv * xv * r, axis=0, keepdims=True)

    row = pl.BlockSpec((ROW_BLOCK, d), lambda i: (i, 0))
    vec = pl.BlockSpec((1, d), lambda i: (0, 0))
    return pl.pallas_call(
        body, name=name, grid=(s // ROW_BLOCK,),
        in_specs=[row, row, vec, row],
        out_specs=[row] + [row] * bf16_copy + [vec],
        out_shape=[jax.ShapeDtypeStruct((s, d), F32)] + [jax.ShapeDtypeStruct((s, d), BF16)] * bf16_copy
        + [jax.ShapeDtypeStruct((1, d), F32)],
        compiler_params=_params(("arbitrary",)),
    )(dn, x, g, dres)


def _final_loss(h2, target, g, name="final_loss"):
    s, d = h2.shape

    def body(h_ref, t_ref, g_ref, dh_ref, dhb_ref, loss_ref, gg_ref):
        i = pl.program_id(0)
        hv, gv = h_ref[...], g_ref[...]
        r = lax.rsqrt(jnp.mean(hv * hv, axis=-1, keepdims=True) + EPS)
        e = hv * r * gv - t_ref[...]
        dy = e * (1.0 / d)
        dyg = dy * gv
        c = jnp.mean(dyg * hv, axis=-1, keepdims=True)
        dh = r * dyg - hv * (r * r * r * c)
        dh_ref[...] = dh
        dhb_ref[...] = dh.astype(BF16)

        @pl.when(i == 0)
        def _():
            gg_ref[...] = jnp.zeros_like(gg_ref)
            loss_ref[...] = jnp.zeros_like(loss_ref)

        gg_ref[...] += jnp.sum(dy * hv * r, axis=0, keepdims=True)
        loss_ref[...] += jnp.sum(jnp.sum(e * e, axis=-1, keepdims=True), axis=0, keepdims=True) * (0.5 / d)

    row = pl.BlockSpec((ROW_BLOCK, d), lambda i: (i, 0))
    vec = pl.BlockSpec((1, d), lambda i: (0, 0))
    return pl.pallas_call(
        body, name=name, grid=(s // ROW_BLOCK,),
        in_specs=[row, row, vec],
        out_specs=[row, row, pl.BlockSpec((SUBLANES, LANES), lambda i: (0, 0)), vec],
        out_shape=[jax.ShapeDtypeStruct((s, d), F32), jax.ShapeDtypeStruct((s, d), BF16),
                   jax.ShapeDtypeStruct((SUBLANES, LANES), F32), jax.ShapeDtypeStruct((1, d), F32)],
        compiler_params=_params(("arbitrary",)),
    )(h2, target, g)


def _rope_tables(s):
    pos = jnp.arange(s, dtype=F32)
    inv_freq = ROPE_THETA ** (-jnp.arange(0, ROPE_DIM, 2, dtype=F32) / ROPE_DIM)
    ang = pos[:, None] * inv_freq[None, :]
    cos, sin = jnp.cos(ang), jnp.sin(ang)
    half = ROPE_DIM // 2
    rest = HEAD_DIM - ROPE_DIM
    c = jnp.concatenate([cos, cos, jnp.ones((s, rest), F32)], axis=1)
    sm = jnp.concatenate([-sin, jnp.zeros((s, half + rest), F32)], axis=1)
    sp = jnp.concatenate([jnp.zeros((s, half), F32), sin, jnp.zeros((s, rest), F32)], axis=1)
    return c, sm, sp


def _res_shape(s, groups, dil, dtype):
    return jax.ShapeDtypeStruct((s // dil, dil * groups * LANES), dtype)


def _res_spec(groups, dil):
    return pl.BlockSpec((ROW_BLOCK // dil, dil * groups * LANES), lambda i: (i, 0))


def _to_residues(scr, o_ref, dil):
    groups, rows = scr.shape[0], ROW_BLOCK // dil
    for r in range(dil):
        for h in range(groups):
            piece = scr[h] if dil == 1 else scr.at[h][pl.ds(r, rows, stride=dil), :]
            o_ref[:, (r * groups + h) * LANES:(r * groups + h + 1) * LANES] = piece.astype(o_ref.dtype)


def _from_residues(i_ref, scr, dil):
    groups, rows = scr.shape[0], ROW_BLOCK // dil
    for r in range(dil):
        for h in range(groups):
            piece = i_ref[:, (r * groups + h) * LANES:(r * groups + h + 1) * LANES].astype(F32)
            if dil == 1:
                scr[h] = piece
            else:
                scr.at[h][pl.ds(r, rows, stride=dil), :] = piece


def _rope_fwd(proj, tables, name="rope_fwd"):
    s = proj.shape[0]
    half = ROPE_DIM // 2
    nd = len(DILATIONS)

    def body(p_ref, c_ref, sm_ref, sp_ref, *rest):
        outs, scr = rest[:3 * nd], rest[3 * nd]
        c, sm, sp = c_ref[...], sm_ref[...], sp_ref[...]
        for gi, off in enumerate((OFF_AQ, OFF_AK, OFF_AV)):
            for h in range(ATTN_HEADS):
                t = p_ref[:, off + h * HEAD_DIM: off + (h + 1) * HEAD_DIM]
                if off != OFF_AV:
                    t = t * c + pltpu.roll(t, HEAD_DIM - half, 1) * sm + pltpu.roll(t, half, 1) * sp
                scr[h] = t
            for di, dil in enumerate(DILATIONS):
                _to_residues(scr, outs[3 * di + gi], dil)

    tab = pl.BlockSpec((ROW_BLOCK, HEAD_DIM), lambda i: (i, 0))
    outs = pl.pallas_call(
        body, name=name, grid=(s // ROW_BLOCK,),
        in_specs=[pl.BlockSpec((ROW_BLOCK, 3 * ATTN_W), lambda i: (i, 0)), tab, tab, tab],
        out_specs=[_res_spec(ATTN_HEADS, d) for d in DILATIONS for _ in range(3)],
        out_shape=[_res_shape(s, ATTN_HEADS, d, BF16) for d in DILATIONS for _ in range(3)],
        scratch_shapes=[pltpu.VMEM((ATTN_HEADS, ROW_BLOCK, LANES), F32)],
        compiler_params=_params(("parallel",)),
    )(proj, *tables)
    return [tuple(outs[3 * di:3 * di + 3]) for di in range(nd)]


def _rope_bwd(grads, tables, name="rope_bwd"):
    s = grads[0][0].shape[0] * DILATIONS[0]
    half = ROPE_DIM // 2
    nd = len(DILATIONS)

    def body(*refs):
        ins = refs[:3 * nd]
        c_ref, sm_ref, sp_ref, o_ref = refs[3 * nd:3 * nd + 4]
        scrs = refs[3 * nd + 4:]
        c, sm, sp = c_ref[...], sm_ref[...], sp_ref[...]
        for gi, off in enumerate((OFF_AQ, OFF_AK, OFF_AV)):
            for di, dil in enumerate(DILATIONS):
                _from_residues(ins[3 * di + gi], scrs[di], dil)
            for h in range(ATTN_HEADS):
                t = scrs[0][h]
                for scr in scrs[1:]:
                    t = t + scr[h]
                if off != OFF_AV:
                    t = t * c + pltpu.roll(t * sm, half, 1) + pltpu.roll(t * sp, HEAD_DIM - half, 1)
                o_ref[:, off + h * HEAD_DIM: off + (h + 1) * HEAD_DIM] = t.astype(BF16)

    tab = pl.BlockSpec((ROW_BLOCK, HEAD_DIM), lambda i: (i, 0))
    return pl.pallas_call(
        body, name=name, grid=(s // ROW_BLOCK,),
        in_specs=[_res_spec(ATTN_HEADS, d) for d in DILATIONS for _ in range(3)] + [tab, tab, tab],
        out_specs=pl.BlockSpec((ROW_BLOCK, 3 * ATTN_W), lambda i: (i, 0)),
        out_shape=jax.ShapeDtypeStruct((s, IN_PAD), BF16),
        scratch_shapes=[pltpu.VMEM((ATTN_HEADS, ROW_BLOCK, LANES), F32) for _ in DILATIONS],
        compiler_params=_params(("parallel",)),
    )(*[t for g in grads for t in g], *tables)


ATTN_GROUP = 4


def _window_specs(nsteps, width):
    rows, hb = ATTN_GROUP * ATTN_BLOCK, N_SIDE
    per = rows // hb
    cur = pl.BlockSpec((rows, width), lambda r, j: (j, r))
    prev = pl.BlockSpec((hb, width), lambda r, j: (jnp.maximum(per * j - 1, 0), r))
    nxt = pl.BlockSpec((hb, width), lambda r, j: (jnp.minimum(per * (j + 1), per * nsteps - 1), r))
    return prev, cur, nxt


def _block(ref, b, sl):
    return ref[b * ATTN_BLOCK:(b + 1) * ATTN_BLOCK, sl]


def _edge(prev_ref, cur_ref, next_ref, b, sl):
    qb, hb = ATTN_BLOCK, N_SIDE
    before = prev_ref[:, sl] if b == 0 else cur_ref[b * qb - hb:b * qb, sl]
    after = next_ref[:, sl] if b == ATTN_GROUP - 1 else cur_ref[(b + 1) * qb:(b + 1) * qb + hb, sl]
    return jnp.concatenate([before, after], axis=0)


def _band_masks(j, length):
    qb, hb = ATTN_BLOCK, N_SIDE
    row = lax.broadcasted_iota(jnp.int32, (qb, qb), 0)
    col = lax.broadcasted_iota(jnp.int32, (qb, qb), 1)

    def edge_pos(i):
        return j * qb - hb + i + jnp.where(i >= hb, qb, 0)

    def ok(a, b, outside):
        return (jnp.abs(a - b) <= N_SIDE) & (outside >= 0) & (outside < length)

    cur = jnp.abs(row - col) <= N_SIDE
    edge_k = ok(j * qb + row, edge_pos(col), edge_pos(col))
    edge_q = ok(edge_pos(row), j * qb + col, edge_pos(row))
    return cur, edge_k, edge_q


def _attn_fwd(q, k, v, dil, name):
    length = q.shape[0]
    qb = ATTN_BLOCK
    nsteps = length // (ATTN_GROUP * qb)
    scale = HEAD_DIM ** -0.5

    def body(q_ref, kp_ref, kc_ref, kn_ref, vp_ref, vc_ref, vn_ref, o_ref, lse_ref):
        masks = [_band_masks(pl.program_id(1) * ATTN_GROUP + b, length) for b in range(ATTN_GROUP)]
        lane = lax.broadcasted_iota(jnp.int32, (qb, LANES), 1)
        units = [(b, h, slice(h * HEAD_DIM, (h + 1) * HEAD_DIM)) for b in range(ATTN_GROUP)
                 for h in range(ATTN_HEADS)]
        scores = [(_dot_nt(_block(q_ref, b, sl), _block(kc_ref, b, sl)),
                   _dot_nt(_block(q_ref, b, sl), _edge(kp_ref, kc_ref, kn_ref, b, sl))) for b, _, sl in units]
        probs = []
        lse_acc = [jnp.zeros((qb, LANES), F32) for _ in range(ATTN_GROUP)]
        for (b, h, _), (s_c, s_e) in zip(units, scores):
            valid_c, valid_e, _ = masks[b]
            s_c = jnp.where(valid_c, s_c * scale, NEG)
            s_e = jnp.where(valid_e, s_e * scale, NEG)
            m = jnp.max(jnp.maximum(s_c, s_e), axis=-1, keepdims=True)
            p_c, p_e = jnp.exp(s_c - m), jnp.exp(s_e - m)
            den = jnp.sum(p_c + p_e, axis=-1, keepdims=True)
            probs.append((p_c.astype(BF16), p_e.astype(BF16), 1.0 / den))
            lse_acc[b] = jnp.where(lane == h, m + jnp.log(den), lse_acc[b])
        for (b, _, sl), (p_c, p_e, inv) in zip(units, probs):
            o_ref[b * qb:(b + 1) * qb, sl] = (_dot(p_c, _block(vc_ref, b, sl))
                                              + _dot(p_e, _edge(vp_ref, vc_ref, vn_ref, b, sl))) * inv
        for b in range(ATTN_GROUP):
            lse_ref[b * qb:(b + 1) * qb, :] = lse_acc[b]

    prev, cur, nxt = _window_specs(nsteps, ATTN_W)
    return pl.pallas_call(
        body, name=name, grid=(dil, nsteps),
        in_specs=[cur, prev, cur, nxt, prev, cur, nxt],
        out_specs=[cur, pl.BlockSpec((ATTN_GROUP * qb, LANES), lambda r, j: (j, r))],
        out_shape=[jax.ShapeDtypeStruct((length, dil * ATTN_W), F32),
                   jax.ShapeDtypeStruct((length, dil * LANES), F32)],
        compiler_params=_params(("parallel", "parallel")),
    )(q, k, k, k, v, v, v)


def _attn_combine(outs, lses, g, name="attn_combine"):
    s = outs[0].shape[0] * DILATIONS[0]
    nd = len(DILATIONS)

    def body(*refs):
        o_refs, l_refs = refs[:nd], refs[nd:2 * nd]
        g_ref, o_ref, n_ref = refs[2 * nd:2 * nd + 3]
        lse_outs = refs[2 * nd + 3:3 * nd + 3]
        o_scr, l_scr = refs[3 * nd + 3:4 * nd + 3], refs[4 * nd + 3:5 * nd + 3]
        for di, dil in enumerate(DILATIONS):
            _from_residues(o_refs[di], o_scr[di], dil)
            _from_residues(l_refs[di], l_scr[di], dil)
        ls = [scr[0] for scr in l_scr]
        m = ls[0]
        for l in ls[1:]:
            m = jnp.maximum(m, l)
        es = [jnp.exp(l - m) for l in ls]
        z = es[0]
        for e in es[1:]:
            z = z + e
        ws = [e / z for e in es]
        l_scr[0][0] = m + jnp.log(z)
        for di, dil in enumerate(DILATIONS):
            _to_residues(l_scr[0], lse_outs[di], dil)
        ssq = jnp.zeros((ROW_BLOCK, 1), F32)
        for h in range(ATTN_HEADS):
            sl = slice(h * HEAD_DIM, (h + 1) * HEAD_DIM)
            acc = ws[0][:, h:h + 1] * o_scr[0][h]
            for w, scr in zip(ws[1:], o_scr[1:]):
                acc = acc + w[:, h:h + 1] * scr[h]
            o_ref[:, sl] = acc
            ssq = ssq + jnp.sum(acc * acc, axis=-1, keepdims=True)
        r = lax.rsqrt(ssq * (1.0 / ATTN_W) + EPS)
        n_ref[...] = (o_ref[...] * r * g_ref[...]).astype(BF16)

    blk = pl.BlockSpec((ROW_BLOCK, ATTN_W), lambda i: (i, 0))
    outs_ = pl.pallas_call(
        body, name=name, grid=(s // ROW_BLOCK,),
        in_specs=[_res_spec(ATTN_HEADS, d) for d in DILATIONS] + [_res_spec(1, d) for d in DILATIONS]
        + [pl.BlockSpec((1, ATTN_W), lambda i: (0, 0))],
        out_specs=[blk, blk] + [_res_spec(1, d) for d in DILATIONS],
        out_shape=[jax.ShapeDtypeStruct((s, ATTN_W), F32), jax.ShapeDtypeStruct((s, D_MODEL), BF16)]
        + [_res_shape(s, 1, d, F32) for d in DILATIONS],
        scratch_shapes=[pltpu.VMEM((ATTN_HEADS, ROW_BLOCK, LANES), F32) for _ in DILATIONS]
        + [pltpu.VMEM((1, ROW_BLOCK, LANES), F32) for _ in DILATIONS],
        compiler_params=_params(("parallel",)),
    )(*outs, *lses, g)
    return outs_[0], outs_[1], list(outs_[2:])


def _attn_prebwd(dcat, o, g, name="attn_prebwd"):
    s = o.shape[0]
    nd = len(DILATIONS)

    def body(dy_ref, o_ref, g_ref, *rest):
        do_outs, delta_outs, gg_ref = rest[:nd], rest[nd:2 * nd], rest[2 * nd]
        do_scr, delta_scr = rest[2 * nd + 1], rest[2 * nd + 2]
        i = pl.program_id(0)
        dy, ov = dy_ref[...], o_ref[...]
        r = lax.rsqrt(jnp.mean(ov * ov, axis=-1, keepdims=True) + EPS)
        dyg = dy * g_ref[...]
        c = jnp.mean(dyg * ov, axis=-1, keepdims=True)
        do = r * dyg - ov * (r * r * r * c)
        prod = do * ov
        lane = lax.broadcasted_iota(jnp.int32, (ROW_BLOCK, LANES), 1)
        acc = jnp.zeros((ROW_BLOCK, LANES), F32)
        for h in range(ATTN_HEADS):
            sl = slice(h * HEAD_DIM, (h + 1) * HEAD_DIM)
            do_scr[h] = do[:, sl]
            acc = jnp.where(lane == h, jnp.sum(prod[:, sl], axis=-1, keepdims=True), acc)
        delta_scr[0] = acc
        for di, dil in enumerate(DILATIONS):
            _to_residues(do_scr, do_outs[di], dil)
            _to_residues(delta_scr, delta_outs[di], dil)

        @pl.when(i == 0)
        def _():
            gg_ref[...] = jnp.zeros_like(gg_ref)

        gg_ref[...] += jnp.sum(dy * ov * r, axis=0, keepdims=True)

    blk = pl.BlockSpec((ROW_BLOCK, ATTN_W), lambda i: (i, 0))
    vec = pl.BlockSpec((1, ATTN_W), lambda i: (0, 0))
    outs = pl.pallas_call(
        body, name=name, grid=(s // ROW_BLOCK,),
        in_specs=[blk, blk, vec],
        out_specs=[_res_spec(ATTN_HEADS, d) for d in DILATIONS] + [_res_spec(1, d) for d in DILATIONS] + [vec],
        out_shape=[_res_shape(s, ATTN_HEADS, d, BF16) for d in DILATIONS]
        + [_res_shape(s, 1, d, F32) for d in DILATIONS] + [jax.ShapeDtypeStruct((1, ATTN_W), F32)],
        scratch_shapes=[pltpu.VMEM((ATTN_HEADS, ROW_BLOCK, LANES), F32), pltpu.VMEM((1, ROW_BLOCK, LANES), F32)],
        compiler_params=_params(("arbitrary",)),
    )(dcat, o, g)
    return list(outs[:nd]), list(outs[nd:2 * nd]), outs[2 * nd]


def _attn_bwd(q, k, v, do, lse, delta, dil, name):
    length = q.shape[0]
    qb = ATTN_BLOCK
    nsteps = length // (ATTN_GROUP * qb)
    scale = HEAD_DIM ** -0.5

    def body(qp, qc, qn, kp, kc, kn, vp, vc, vn, dop, doc, don, lp, lc, ln, dp, dc, dn, dq_ref, dk_ref, dv_ref):
        masks = [_band_masks(pl.program_id(1) * ATTN_GROUP + b, length) for b in range(ATTN_GROUP)]
        everything = slice(None)
        lse_e = [_edge(lp, lc, ln, b, everything) for b in range(ATTN_GROUP)]
        del_e = [_edge(dp, dc, dn, b, everything) for b in range(ATTN_GROUP)]
        units = [(b, h, slice(h * HEAD_DIM, (h + 1) * HEAD_DIM)) for b in range(ATTN_GROUP)
                 for h in range(ATTN_HEADS)]
        prods = []
        for b, _, sl in units:
            q_c, k_c, v_c, do_c = _block(qc, b, sl), _block(kc, b, sl), _block(vc, b, sl), _block(doc, b, sl)
            q_e, k_e = _edge(qp, qc, qn, b, sl), _edge(kp, kc, kn, b, sl)
            v_e, do_e = _edge(vp, vc, vn, b, sl), _edge(dop, doc, don, b, sl)
            prods.append((_dot_nt(q_c, k_c), _dot_nt(do_c, v_c), _dot_nt(q_c, k_e), _dot_nt(do_c, v_e),
                          _dot_nt(q_e, k_c), _dot_nt(do_e, v_c)))
        parts = []
        for (b, h, _), (s_cc, dp_cc, s_ek, dp_ek, s_eq, dp_eq) in zip(units, prods):
            valid_c, valid_ek, valid_eq = masks[b]
            hc = slice(h, h + 1)
            lse_c, del_c = _block(lc, b, hc), _block(dc, b, hc)
            p_cc = jnp.where(valid_c, jnp.exp(s_cc * scale - lse_c), 0.0)
            ds_cc = (p_cc * (dp_cc - del_c)).astype(BF16)
            p_ek = jnp.where(valid_ek, jnp.exp(s_ek * scale - lse_c), 0.0)
            ds_ek = (p_ek * (dp_ek - del_c)).astype(BF16)
            p_eq = jnp.where(valid_eq, jnp.exp(s_eq * scale - lse_e[b][:, hc]), 0.0)
            ds_eq = (p_eq * (dp_eq - del_e[b][:, hc])).astype(BF16)
            parts.append((p_cc.astype(BF16), ds_cc, ds_ek, p_eq.astype(BF16), ds_eq))
        for (b, _, sl), (p_cc, ds_cc, ds_ek, p_eq, ds_eq) in zip(units, parts):
            rows = slice(b * qb, (b + 1) * qb)
            q_c, k_c, do_c = _block(qc, b, sl), _block(kc, b, sl), _block(doc, b, sl)
            q_e, k_e, do_e = _edge(qp, qc, qn, b, sl), _edge(kp, kc, kn, b, sl), _edge(dop, doc, don, b, sl)
            dq_ref[rows, sl] = ((_dot(ds_cc, k_c) + _dot(ds_ek, k_e)) * scale).astype(BF16)
            dk_ref[rows, sl] = ((_dot_tn(ds_cc, q_c) + _dot_tn(ds_eq, q_e)) * scale).astype(BF16)
            dv_ref[rows, sl] = (_dot_tn(p_cc, do_c) + _dot_tn(p_eq, do_e)).astype(BF16)

    wide, narrow = list(_window_specs(nsteps, ATTN_W)), list(_window_specs(nsteps, LANES))
    return tuple(pl.pallas_call(
        body, name=name, grid=(dil, nsteps),
        in_specs=wide * 4 + narrow * 2,
        out_specs=[wide[1]] * 3,
        out_shape=[jax.ShapeDtypeStruct((length, dil * ATTN_W), BF16)] * 3,
        compiler_params=_params(("parallel", "parallel")),
    )(q, q, q, k, k, k, v, v, v, do, do, do, lse, lse, lse, delta, delta, delta))


def _gate_matrices(gf_up, gb_up):
    pad = LANES - 2 * GLA_RANK
    uf = jnp.concatenate([gf_up, jnp.zeros((GLA_RANK + pad, GLA_KW), gf_up.dtype)], axis=0)
    ub = jnp.concatenate([jnp.zeros((GLA_RANK, GLA_KW), gb_up.dtype), gb_up, jnp.zeros((pad, GLA_KW), gb_up.dtype)], axis=0)
    return uf.astype(BF16), ub.astype(BF16)


def _log_sigmoid(x):
    return jnp.minimum(x, 0.0) - jnp.log(1.0 + jnp.exp(-jnp.abs(x)))


def _gla_gates(proj, uf, ub, gf_b, gb_b, name="gla_gates"):
    s = proj.shape[0]

    def body(z_ref, uf_ref, ub_ref, bf_ref, bb_ref, gf_ref, gb_ref):
        z = z_ref[...].astype(BF16)
        gf_ref[...] = _log_sigmoid(_dot(z, uf_ref[...]) + bf_ref[...]) * (1.0 / GLA_GATE_NORM)
        gb_ref[...] = _log_sigmoid(_dot(z, ub_ref[...]) + bb_ref[...]) * (1.0 / GLA_GATE_NORM)

    mat = pl.BlockSpec((LANES, GLA_KW), lambda i: (0, 0))
    vec = pl.BlockSpec((1, GLA_KW), lambda i: (0, 0))
    out = pl.BlockSpec((ROW_BLOCK, GLA_KW), lambda i: (i, 0))
    return pl.pallas_call(
        body, name=name, grid=(s // ROW_BLOCK,),
        in_specs=[pl.BlockSpec((ROW_BLOCK, LANES), lambda i: (i, OFF_Z // LANES)), mat, mat, vec, vec],
        out_specs=[out, out],
        out_shape=[jax.ShapeDtypeStruct((s, GLA_KW), F32)] * 2,
        compiler_params=_params(("parallel",)),
    )(proj, uf, ub, gf_b, gb_b)


def _gla_gates_bwd(dgf, dgb, proj, uf, ub, gf_b, gb_b, dproj, name="gla_gates_bwd"):
    s = proj.shape[0]
    tail = IN_PAD - OFF_Z

    def body(dgf_ref, dgb_ref, z_ref, uf_ref, ub_ref, bf_ref, bb_ref, _, dz_ref, guf_ref, gub_ref, gbf_ref, gbb_ref):
        i = pl.program_id(0)
        z = z_ref[...].astype(BF16)
        uf_, ub_ = uf_ref[...], ub_ref[...]
        dpf = dgf_ref[...] * (1.0 / GLA_GATE_NORM) * _sigmoid(-(_dot(z, uf_) + bf_ref[...]))
        dpb = dgb_ref[...] * (1.0 / GLA_GATE_NORM) * _sigmoid(-(_dot(z, ub_) + bb_ref[...]))
        dpf_b, dpb_b = dpf.astype(BF16), dpb.astype(BF16)
        dz_ref[:, 0:LANES] = (_dot_nt(dpf_b, uf_) + _dot_nt(dpb_b, ub_)).astype(BF16)
        dz_ref[:, LANES:tail] = jnp.zeros((ROW_BLOCK, tail - LANES), BF16)

        @pl.when(i == 0)
        def _():
            for r in (guf_ref, gub_ref, gbf_ref, gbb_ref):
                r[...] = jnp.zeros_like(r)

        guf_ref[...] += _dot_tn(z, dpf_b)
        gub_ref[...] += _dot_tn(z, dpb_b)
        gbf_ref[...] += jnp.sum(dpf, axis=0, keepdims=True)
        gbb_ref[...] += jnp.sum(dpb, axis=0, keepdims=True)

    mat = pl.BlockSpec((LANES, GLA_KW), lambda i: (0, 0))
    vec = pl.BlockSpec((1, GLA_KW), lambda i: (0, 0))
    blk = pl.BlockSpec((ROW_BLOCK, GLA_KW), lambda i: (i, 0))
    return pl.pallas_call(
        body, name=name, grid=(s // ROW_BLOCK,),
        in_specs=[blk, blk, pl.BlockSpec((ROW_BLOCK, LANES), lambda i: (i, OFF_Z // LANES)), mat, mat, vec, vec,
                  pl.BlockSpec(memory_space=pl.ANY)],
        out_specs=[pl.BlockSpec((ROW_BLOCK, tail), lambda i: (i, OFF_Z // tail)), mat, mat, vec, vec],
        out_shape=[jax.ShapeDtypeStruct(dproj.shape, dproj.dtype), jax.ShapeDtypeStruct((LANES, GLA_KW), F32),
                   jax.ShapeDtypeStruct((LANES, GLA_KW), F32), jax.ShapeDtypeStruct((1, GLA_KW), F32),
                   jax.ShapeDtypeStruct((1, GLA_KW), F32)],
        input_output_aliases={7: 0},
        compiler_params=_params(("arbitrary",)),
    )(dgf, dgb, proj, uf, ub, gf_b, gb_b, dproj)


def _split3(x):
    x1 = x.astype(BF16)
    r1 = x - x1.astype(F32)
    x2 = r1.astype(BF16)
    x3 = (r1 - x2.astype(F32)).astype(BF16)
    return x1, x2, x3


def _dot_exact(mask_bf, x):
    x1, x2, x3 = _split3(x)
    return _dot(mask_bf, x1) + _dot(mask_bf, x2) + _dot(mask_bf, x3)


def _chunk_masks(reverse):
    c = GLA_CHUNK
    row = lax.broadcasted_iota(jnp.int32, (c, c), 0)
    col = lax.broadcasted_iota(jnp.int32, (c, c), 1)
    allowed = (col >= row) if reverse else (col <= row)
    seen_by = (col <= row) if reverse else (col >= row)
    return allowed, seen_by


def _chunk_terms(q_ref, k_ref, g_ref, rs, hs, allowed, reverse):
    c = GLA_CHUNK
    mid, last = (c // 2, 0) if reverse else (c // 2 - 1, c - 1)
    q = q_ref[rs, hs] * (GLA_DK ** -0.5)
    k = k_ref[rs, hs]
    b = _dot_exact(jnp.where(allowed, 1.0, 0.0).astype(BF16), g_ref[rs, hs])
    bref, blast = b[mid:mid + 1, :], b[last:last + 1, :]
    e_q, e_k, e_in, e_st = jnp.exp(b - bref), jnp.exp(bref - b), jnp.exp(b), jnp.exp(blast - b)
    return dict(last=last, e_q=e_q, e_k=e_k, e_in=e_in, e_st=e_st,
                dec=jnp.exp(blast), qe=q * e_q, ke=k * e_k, qin=q * e_in, kst=k * e_st)


def _gla_blockspecs(s, reverse_order):
    cb = GLA_CHUNKS_PER_STEP
    rows = cb * GLA_CHUNK
    nsteps = s // rows

    def rb(n):
        return (nsteps - 1 - n) if reverse_order else n

    qspec = pl.BlockSpec((rows, GLA_KW), lambda n: (rb(n), OFF_GQ // GLA_KW))
    kspec = pl.BlockSpec((rows, GLA_KW), lambda n: (rb(n), OFF_GK // GLA_KW))
    vspec = pl.BlockSpec((rows, GLA_VW), lambda n: (rb(n), OFF_GV // GLA_VW))
    gspec = pl.BlockSpec((rows, GLA_KW), lambda n: (rb(n), 0))
    ospec = pl.BlockSpec((rows, GLA_VW), lambda n: (rb(n), 0))
    sspec = pl.BlockSpec((GLA_HEADS, cb, GLA_DV, GLA_DK), lambda n: (0, rb(n), 0, 0))
    return cb, rows, nsteps, qspec, kspec, vspec, gspec, ospec, sspec


def _gla_units(cb, order_reversed):
    chunks = list(reversed(range(cb))) if order_reversed else list(range(cb))
    return [(c, h, slice(c * GLA_CHUNK, (c + 1) * GLA_CHUNK), slice(h * GLA_DK, (h + 1) * GLA_DK),
             slice(h * GLA_DV, (h + 1) * GLA_DV)) for c in chunks for h in range(GLA_HEADS)]


def _gla_fwd(proj, g, reverse, name):
    s = proj.shape[0]
    cb, rows, nsteps, qspec, kspec, vspec, gspec, ospec, sspec = _gla_blockspecs(s, reverse)

    def body(q_ref, k_ref, v_ref, g_ref, o_ref, st_ref, state):
        @pl.when(pl.program_id(0) == 0)
        def _():
            state[...] = jnp.zeros_like(state)

        allowed, _ = _chunk_masks(reverse)
        units = _gla_units(cb, reverse)
        terms = [_chunk_terms(q_ref, k_ref, g_ref, rs, hs, allowed, reverse) for _, _, rs, hs, _ in units]
        vals = [v_ref[rs, vs].astype(BF16) for _, _, rs, _, vs in units]
        raw = [(_dot_nt(t["qe"].astype(BF16), t["ke"].astype(BF16)), _dot_tn(v, t["kst"].astype(BF16)))
               for t, v in zip(terms, vals)]
        intra = [_dot(jnp.where(allowed, a, 0.0).astype(BF16), v) for (a, _), v in zip(raw, vals)]
        st = [state[h] for h in range(GLA_HEADS)]
        for (c, h, rs, _, vs), t, (_, kv), o_in in zip(units, terms, raw, intra):
            st_b = st[h].astype(BF16)
            st_ref[h, c] = st_b
            o_ref[rs, vs] = o_in + _dot_nt(t["qin"].astype(BF16), st_b)
            st[h] = st[h] * t["dec"] + kv
        for h in range(GLA_HEADS):
            state[h] = st[h]

    return pl.pallas_call(
        body, name=name, grid=(nsteps,),
        in_specs=[qspec, kspec, vspec, gspec],
        out_specs=[ospec, sspec],
        out_shape=[jax.ShapeDtypeStruct((s, GLA_VW), F32),
                   jax.ShapeDtypeStruct((GLA_HEADS, s // GLA_CHUNK, GLA_DV, GLA_DK), BF16)],
        scratch_shapes=[pltpu.VMEM((GLA_HEADS, GLA_DV, GLA_DK), F32)],
        compiler_params=_params(("arbitrary",)),
    )(proj, proj, proj, g)


def _gla_bwd(proj, g, do, states, reverse, name, merge=None):
    s = proj.shape[0]
    cb, rows, nsteps, qspec, kspec, vspec, gspec, ospec, sspec = _gla_blockspecs(s, not reverse)
    gla_cols = OFF_Z - OFF_GQ

    def body(q_ref, k_ref, v_ref, g_ref, do_ref, sp_ref, *rest):
        if merge is None:
            dq_ref, dk_ref, dv_ref, dg_ref, dstate = rest
        else:
            dq_o, dk_o, dv_o, dgr_ref, _, dp_ref, dg_ref, dstate = rest
        @pl.when(pl.program_id(0) == 0)
        def _():
            dstate[...] = jnp.zeros_like(dstate)

        allowed, seen_by = _chunk_masks(reverse)
        units = _gla_units(cb, not reverse)
        terms = [_chunk_terms(q_ref, k_ref, g_ref, rs, hs, allowed, reverse) for _, _, rs, hs, _ in units]
        vals = [v_ref[rs, vs].astype(BF16) for _, _, rs, _, vs in units]
        dos = [do_ref[rs, vs] for _, _, rs, _, vs in units]
        prevs = [sp_ref[h, c] for c, h, _, _, _ in units]
        raw = [(_dot_nt(t["qe"].astype(BF16), t["ke"].astype(BF16)), _dot_nt(do, v),
                _dot(do, sp), _dot_tn(do, t["qin"].astype(BF16)))
               for t, v, do, sp in zip(terms, vals, dos, prevs)]
        inner = []
        for t, do, (a, da, _, _) in zip(terms, dos, raw):
            da = jnp.where(allowed, da, 0.0).astype(BF16)
            inner.append((_dot(da, t["ke"].astype(BF16)), _dot_tn(da, t["qe"].astype(BF16)),
                          _dot_tn(jnp.where(allowed, a, 0.0).astype(BF16), do)))
        ds = [dstate[h] for h in range(GLA_HEADS)]
        outer = []
        for (c, h, _, _, _), t, v, sp, (_, _, _, inc) in zip(units, terms, vals, prevs, raw):
            ds_b = ds[h].astype(BF16)
            outer.append((_dot(v, ds_b), _dot_nt(t["kst"].astype(BF16), ds_b),
                          jnp.sum(sp.astype(F32) * ds[h], axis=0, keepdims=True)))
            ds[h] = ds[h] * t["dec"] + inc
        for h in range(GLA_HEADS):
            dstate[h] = ds[h]
        seen_bf = jnp.where(seen_by, 1.0, 0.0).astype(BF16)
        rowi = lax.broadcasted_iota(jnp.int32, (GLA_CHUNK, GLA_DK), 0)
        for (c, h, rs, hs, vs), t, (_, _, dqin, _), (dqe, dke, dv_in), (dkst, dv_out, ddec) in zip(
                units, terms, raw, inner, outer):
            dq = (dqe * t["e_q"] + dqin * t["e_in"]) * (GLA_DK ** -0.5)
            dk = dke * t["e_k"] + dkst * t["e_st"]
            if merge is None:
                dq_ref[rs, hs], dk_ref[rs, hs], dv_ref[rs, vs] = dq, dk, dv_in + dv_out
            else:
                lo = OFF_GK - OFF_GQ + h * GLA_DK
                dp_ref[rs, hs] = (dq + dq_o[rs, hs]).astype(BF16)
                dp_ref[rs, lo:lo + GLA_DK] = (dk + dk_o[rs, hs]).astype(BF16)
                lo = OFF_GV - OFF_GQ + h * GLA_DV
                dp_ref[rs, lo:lo + GLA_DV] = (dv_in + dv_out + dv_o[rs, vs]).astype(BF16)
            kk = dkst * t["kst"]
            db = dqe * t["qe"] - dke * t["ke"] + dqin * t["qin"] - kk
            extra = jnp.sum(kk, axis=0, keepdims=True) + ddec * t["dec"]
            db = db + jnp.where(rowi == t["last"], extra, 0.0)
            dg_ref[rs, hs] = _dot_exact(seen_bf, db)
        if merge is not None:
            dp_ref[:, OFF_GR - OFF_GQ:gla_cols] = dgr_ref[...]

    scratch = [pltpu.VMEM((GLA_HEADS, GLA_DV, GLA_DK), F32)]
    if merge is None:
        return pl.pallas_call(
            body, name=name, grid=(nsteps,),
            in_specs=[qspec, kspec, vspec, gspec, ospec, sspec],
            out_specs=[gspec, gspec, ospec, gspec],
            out_shape=[jax.ShapeDtypeStruct((s, GLA_KW), F32), jax.ShapeDtypeStruct((s, GLA_KW), F32),
                       jax.ShapeDtypeStruct((s, GLA_VW), F32), jax.ShapeDtypeStruct((s, GLA_KW), F32)],
            scratch_shapes=scratch,
            compiler_params=_params(("arbitrary",)),
        )(proj, proj, proj, g, do, states)
    dproj = merge[4]
    block = gspec.index_map
    return pl.pallas_call(
        body, name=name, grid=(nsteps,),
        in_specs=[qspec, kspec, vspec, gspec, ospec, sspec, gspec, gspec, ospec, ospec, _ANY],
        out_specs=[pl.BlockSpec((rows, gla_cols), lambda n: (block(n)[0], OFF_GQ // gla_cols)), gspec],
        out_shape=[jax.ShapeDtypeStruct(dproj.shape, dproj.dtype), jax.ShapeDtypeStruct((s, GLA_KW), F32)],
        input_output_aliases={10: 0},
        scratch_shapes=scratch,
        compiler_params=_params(("arbitrary",)),
    )(proj, proj, proj, g, do, states, *merge)


def _gla_post(o_f, o_b, proj, g, cat, name="gla_post"):
    s = o_f.shape[0]

    def body(of_ref, ob_ref, gr_ref, g_ref, _, o_ref):
        gv = g_ref[...]
        for h in range(GLA_HEADS):
            sl = slice(h * GLA_DV, (h + 1) * GLA_DV)
            osum = of_ref[:, sl] + ob_ref[:, sl]
            r = lax.rsqrt(jnp.mean(osum * osum, axis=-1, keepdims=True) + EPS)
            gr = gr_ref[:, sl]
            o_ref[:, sl] = (osum * r * gv * (gr * _sigmoid(gr))).astype(BF16)

    blk = pl.BlockSpec((ROW_BLOCK, GLA_VW), lambda i: (i, 0))
    return pl.pallas_call(
        body, name=name, grid=(s // ROW_BLOCK,),
        in_specs=[blk, blk, pl.BlockSpec((ROW_BLOCK, GLA_VW), lambda i: (i, OFF_GR // GLA_VW)),
                  pl.BlockSpec((1, GLA_DV), lambda i: (0, 0)), pl.BlockSpec(memory_space=pl.ANY)],
        out_specs=pl.BlockSpec((ROW_BLOCK, GLA_VW), lambda i: (i, ATTN_W // GLA_VW)),
        out_shape=jax.ShapeDtypeStruct(cat.shape, cat.dtype),
        input_output_aliases={4: 0},
        compiler_params=_params(("parallel",)),
    )(o_f, o_b, proj, g, cat)


def _gla_post_bwd(dcat, o_f, o_b, proj, g, name="gla_post_bwd"):
    s = o_f.shape[0]

    def body(dy_ref, of_ref, ob_ref, gr_ref, g_ref, do_ref, dgr_ref, gg_ref):
        i = pl.program_id(0)
        gv = g_ref[...]
        gg = jnp.zeros((1, GLA_DV), F32)
        for h in range(GLA_HEADS):
            sl = slice(h * GLA_DV, (h + 1) * GLA_DV)
            osum = of_ref[:, sl] + ob_ref[:, sl]
            r = lax.rsqrt(jnp.mean(osum * osum, axis=-1, keepdims=True) + EPS)
            gr, dy = gr_ref[:, sl], dy_ref[:, sl]
            sg = _sigmoid(gr)
            dgr_ref[:, sl] = (dy * (osum * r * gv) * (sg * (1.0 + gr * (1.0 - sg)))).astype(BF16)
            dn = dy * (gr * sg)
            dng = dn * gv
            c = jnp.mean(dng * osum, axis=-1, keepdims=True)
            do_ref[:, sl] = (r * dng - osum * (r * r * r * c)).astype(BF16)
            gg = gg + jnp.sum(dn * osum * r, axis=0, keepdims=True)

        @pl.when(i == 0)
        def _():
            gg_ref[...] = jnp.zeros_like(gg_ref)

        gg_ref[...] += gg

    blk = pl.BlockSpec((ROW_BLOCK, GLA_VW), lambda i: (i, 0))
    vec = pl.BlockSpec((1, GLA_DV), lambda i: (0, 0))
    return pl.pallas_call(
        body, name=name, grid=(s // ROW_BLOCK,),
        in_specs=[pl.BlockSpec((ROW_BLOCK, GLA_VW), lambda i: (i, 1)), blk, blk,
                  pl.BlockSpec((ROW_BLOCK, GLA_VW), lambda i: (i, OFF_GR // GLA_VW)), vec],
        out_specs=[blk, blk, vec],
        out_shape=[jax.ShapeDtypeStruct((s, GLA_VW), BF16), jax.ShapeDtypeStruct((s, GLA_VW), BF16),
                   jax.ShapeDtypeStruct((1, GLA_DV), F32)],
        compiler_params=_params(("arbitrary",)),
    )(dcat, o_f, o_b, proj, g)


HALO = 16


def _extended(prev_ref, cur_ref, next_ref, i, s, tr, cs):
    first, last = i == 0, i == s // tr - 1
    prev = jnp.where(first, 0.0, prev_ref[:, cs].astype(F32))
    nxt = jnp.where(last, 0.0, next_ref[:, cs].astype(F32))
    return jnp.concatenate([prev, cur_ref[:, cs].astype(F32), nxt], axis=0)


FFN_ROWS = 512
FFN_COLS = 512


FFN_CHUNK = 256


def _lagged(i, ni, multiply, finish, rotate, init):
    chunks = [slice(c, c + FFN_CHUNK) for c in range(0, FFN_COLS, FFN_CHUNK)]

    @pl.when(i == 0)
    def _():
        init()

    @pl.when(i < 2)
    def _():
        rotate([multiply(cs) for cs in chunks], chunks)

    @pl.when((i >= 2) & (i < ni))
    def _():
        new = []
        for cs in chunks:
            new.append(multiply(cs))
            finish(cs)
        rotate(new, chunks)

    @pl.when(i >= ni)
    def _():
        for cs in chunks:
            finish(cs)
        rotate(None, chunks)


def _ffn_in(n2, w_gate, w_up, conv_w, conv_b, name="ffn_in"):
    s, d = n2.shape
    f = w_gate.shape[1]
    tm, tn, edge = FFN_ROWS, FFN_COLS, SUBLANES
    ni = s // tm
    ext = tm + 2 * edge

    def body(a_ref, wg_ref, wu_ref, w_ref, b_ref, gate_ref, silu_ref, slope_ref, act_ref, g_tile, u_tile, g_tail):
        i = pl.program_id(1)

        @pl.when(i == 0)
        def _():
            g_tile[...] = jnp.zeros_like(g_tile)
            u_tile[...] = jnp.zeros_like(u_tile)
            g_tail[...] = jnp.zeros_like(g_tail)

        def finish(after):
            g_old, u_old = g_tile[...], u_tile[...]
            before = jnp.where(i == 1, 0.0, g_tail[...])
            ge = jnp.concatenate([before, g_old, after], axis=0)
            w = w_ref[...]
            conv = (w[0:1] * pltpu.roll(ge, 1, 0) + w[1:2] * ge + w[2:3] * pltpu.roll(ge, ext - 1, 0))[edge:edge + tm]
            conv = conv + b_ref[...]
            sg = _sigmoid(conv)
            silu = conv * sg
            u_f = u_old.astype(F32)
            gate_ref[...] = g_old
            silu_ref[...] = silu.astype(BF16)
            slope_ref[...] = (u_f * (sg * (1.0 + conv * (1.0 - sg)))).astype(BF16)
            act_ref[...] = (silu * u_f).astype(BF16)
            g_tail[...] = g_old[tm - edge:tm]

        @pl.when(i < ni)
        def _():
            a = a_ref[...]
            g_new = _dot(a, wg_ref[...])
            u_new = _dot(a, wu_ref[...])
            finish(g_new[0:edge])
            g_tile[...] = g_new
            u_tile[...] = u_new.astype(BF16)

        @pl.when(i == ni)
        def _():
            finish(jnp.zeros((edge, tn), F32))

    lag = pl.BlockSpec((tm, tn), lambda j, i: (jnp.maximum(i - 1, 0), j))
    return pl.pallas_call(
        body, name=name, grid=(f // tn, ni + 1),
        in_specs=[pl.BlockSpec((tm, d), lambda j, i: (jnp.minimum(i, ni - 1), 0)),
                  pl.BlockSpec((d, tn), lambda j, i: (0, j)), pl.BlockSpec((d, tn), lambda j, i: (0, j)),
                  pl.BlockSpec((3, tn), lambda j, i: (0, j)), pl.BlockSpec((1, tn), lambda j, i: (0, j))],
        out_specs=[lag, lag, lag, lag],
        out_shape=[jax.ShapeDtypeStruct((s, f), F32)] + [jax.ShapeDtypeStruct((s, f), BF16)] * 3,
        scratch_shapes=[pltpu.VMEM((tm, tn), F32), pltpu.VMEM((tm, tn), BF16), pltpu.VMEM((edge, tn), F32)],
        compiler_params=_params(("parallel", "arbitrary")),
    )(n2, w_gate, w_up, conv_w, conv_b)


def _ffn_mid_bwd(dh2, w_down, gate, silu, slope, conv_w, name="ffn_mid_bwd"):
    s, d = dh2.shape
    f = gate.shape[1]
    tm, tn = FFN_ROWS, FFN_COLS
    ni = s // tm
    ext = tm + 2 * HALO
    per, last_halo = tm // HALO, s // HALO - 1

    def body(a_ref, wd_ref, gp, gc, gn, sp, sc, sn, silu_ref, w_ref, dg_ref, du_ref, gw_ref, gb_ref,
             d_near, d_far, d_tail):
        i = pl.program_id(1)

        def multiply(cs):
            return _dot_nt(a_ref[...], wd_ref[cs, :])

        def finish(cs):
            before = jnp.where(i == 2, 0.0, d_tail[:, cs])
            after = jnp.where(i == ni + 1, 0.0, d_near[0:HALO, cs])
            d_mid = d_far[:, cs]
            de = jnp.concatenate([before, d_mid, after], axis=0)
            ge = _extended(gp, gc, gn, i - 2, s, tm, cs)
            w = w_ref[:, cs]
            g_prev, g_next = pltpu.roll(ge, 1, 0), pltpu.roll(ge, ext - 1, 0)
            inner = slice(HALO, HALO + tm)
            du_ref[:, cs] = (d_mid * silu_ref[:, cs].astype(F32)).astype(BF16)
            dconv = de * _extended(sp, sc, sn, i - 2, s, tm, cs)
            dgate = w[0:1] * pltpu.roll(dconv, ext - 1, 0) + w[1:2] * dconv + w[2:3] * pltpu.roll(dconv, 1, 0)
            dg_ref[:, cs] = dgate[inner].astype(BF16)
            dci = dconv[inner]
            gw_ref[0:1, cs] += jnp.sum(dci * g_prev[inner], axis=0, keepdims=True)
            gw_ref[1:2, cs] += jnp.sum(dci * ge[inner], axis=0, keepdims=True)
            gw_ref[2:3, cs] += jnp.sum(dci * g_next[inner], axis=0, keepdims=True)
            gb_ref[:, cs] += jnp.sum(dci, axis=0, keepdims=True)

        def rotate(new, chunks):
            d_tail[...] = d_far[tm - HALO:tm]
            d_far[...] = d_near[...]
            if new is not None:
                for cs, d_new in zip(chunks, new):
                    d_near[:, cs] = d_new

        def init():
            for r in (d_near, d_far, d_tail, gw_ref, gb_ref):
                r[...] = jnp.zeros_like(r)

        _lagged(i, ni, multiply, finish, rotate, init)

    def tile(i):
        return jnp.maximum(i - 2, 0)

    cur = pl.BlockSpec((tm, tn), lambda j, i: (tile(i), j))
    prev = pl.BlockSpec((HALO, tn), lambda j, i: (jnp.maximum(tile(i) * per - 1, 0), j))
    nxt = pl.BlockSpec((HALO, tn), lambda j, i: (jnp.minimum((tile(i) + 1) * per, last_halo), j))
    wspec = pl.BlockSpec((3, tn), lambda j, i: (0, j))
    bspec = pl.BlockSpec((1, tn), lambda j, i: (0, j))
    return pl.pallas_call(
        body, name=name, grid=(f // tn, ni + 2),
        in_specs=[pl.BlockSpec((tm, d), lambda j, i: (jnp.minimum(i, ni - 1), 0)),
                  pl.BlockSpec((tn, d), lambda j, i: (j, 0))] + [prev, cur, nxt] * 2 + [cur, wspec],
        out_specs=[cur, cur, wspec, bspec],
        out_shape=[jax.ShapeDtypeStruct((s, f), BF16), jax.ShapeDtypeStruct((s, f), BF16),
                   jax.ShapeDtypeStruct((3, f), F32), jax.ShapeDtypeStruct((1, f), F32)],
        scratch_shapes=[pltpu.VMEM((tm, tn), F32), pltpu.VMEM((tm, tn), F32), pltpu.VMEM((HALO, tn), F32)],
        compiler_params=_params(("parallel", "arbitrary")),
    )(dh2, w_down, gate, gate, gate, slope, slope, slope, silu, conv_w)


def _local_step(x, target, w, late_weights=None, grad_sink=None, first_dep=()):
    s = x.shape[0]
    tables = _rope_tables(s)
    uf, ub = _gate_matrices(w["gf_up"], w["gb_up"])
    if grad_sink is None:
        grad_sink = lambda names, grads: ()

    n1 = _rms_fwd(x, w["norm1_g"], "norm1")
    proj = _matmul([(n1, w["w_in"])], "nn", F32, 1024, 1280, D_MODEL, "in_proj", deps=first_dep)
    qkv = _rope_fwd(proj, tables)
    branches = [_attn_fwd(*qkv[di], d, f"attn_fwd_d{d}") for di, d in enumerate(DILATIONS)]
    o_mix, ao, lse = _attn_combine([b[0] for b in branches], [b[1] for b in branches], w["attn_norm_g"])
    g_f, g_b = _gla_gates(proj, uf, ub, w["gf_b"], w["gb_b"])
    o_f, st_f = _gla_fwd(proj, g_f, False, "gla_fwd_f")
    o_b, st_b = _gla_fwd(proj, g_b, True, "gla_fwd_b")
    cat = _gla_post(o_f, o_b, proj, w["gla_norm_g"], ao)
    if late_weights is not None:
        w = {**w, **late_weights("mixer", cat)}
    h1 = _matmul([(cat, w["w_out"])], "nn", F32, 1024, 1024, D_MODEL, "out_proj", res=x)
    n2 = _rms_fwd(h1, w["norm2_g"], "norm2")
    if late_weights is not None:
        w = {**w, **late_weights("ffn", n2)}
    gate, silu, slope, act = _ffn_in(n2, w["w_gate"], w["w_up"], w["conv_w"], w["conv_b"])
    h2 = _matmul([(act, w["w_down"])], "nn", F32, 1024, 1024, 2816, "ffn_down", res=h1)
    dh2, dh2_b, loss_acc, g_final = _final_loss(h2, target, w["final_norm_g"])

    g_w_down = _matmul([(act, dh2_b)], "tn", BF16, 1408, 1024, 2048, "g_w_down")
    dep = grad_sink(["w_down"], [g_w_down])
    dgate, dup, g_conv_w, g_conv_b = _ffn_mid_bwd(dh2_b, w["w_down"], gate, silu, slope, w["conv_w"])
    g_w_gate = _matmul([(n2, dgate)], "tn", BF16, 2048, 512, 2048, "g_w_gate", deps=dep)
    g_w_up = _matmul([(n2, dup)], "tn", BF16, 2048, 512, 2048, "g_w_up")
    dep = grad_sink(["w_gate", "w_up"], [g_w_gate, g_w_up])
    dn2 = _matmul([(dgate, w["w_gate"])], "nt", F32, 1024, 1024, 2816, "d_n2_gate", deps=dep)
    dn2 = _matmul([(dup, w["w_up"])], "nt", F32, 1024, 1024, 2816, "d_n2_up", res=dn2)
    dh1, dh1_b, g_norm2 = _rms_bwd(dn2, h1, w["norm2_g"], dh2, "norm2_bwd")

    g_w_out = _matmul([(cat, dh1_b)], "tn", BF16, 1024, 1024, 2048, "g_w_out")
    dep = grad_sink(["w_out"], [g_w_out])
    dcat = _matmul([(dh1_b, w["w_out"])], "nt", F32, 1024, 1024, D_MODEL, "d_cat", deps=dep)
    do_attn, delta, g_attn_norm = _attn_prebwd(dcat, o_mix, w["attn_norm_g"])
    grads = [_attn_bwd(*qkv[di], do_attn[di], lse[di], delta[di], d, f"attn_bwd_d{d}")
             for di, d in enumerate(DILATIONS)]
    dproj = _rope_bwd(grads, tables)
    do_gla, dgr, g_gla_norm = _gla_post_bwd(dcat, o_f, o_b, proj, w["gla_norm_g"])
    dq_f, dk_f, dv_f, dg_f = _gla_bwd(proj, g_f, do_gla, st_f, False, "gla_bwd_f")
    dproj, dg_b = _gla_bwd(proj, g_b, do_gla, st_b, True, "gla_bwd_b", merge=(dq_f, dk_f, dv_f, dgr, dproj))
    dproj, g_uf, g_ub, g_gf_b, g_gb_b = _gla_gates_bwd(dg_f, dg_b, proj, uf, ub, w["gf_b"], w["gb_b"], dproj)
    g_w_in = _matmul([(n1, dproj)], "tn", BF16, 1024, 1280, 2048, "g_w_in")
    dep = grad_sink(["w_in"], [g_w_in])
    dn1 = _matmul([(dproj, w["w_in"])], "nt", F32, 1024, 2048, 1280, "d_n1", deps=dep)
    grad_x, g_norm1 = _rms_bwd(dn1, x, w["norm1_g"], dh1, "norm1_bwd", bf16_copy=False)

    g = dict(norm1_g=g_norm1, w_in=g_w_in, gf_up=g_uf[:GLA_RANK], gf_b=g_gf_b,
             gb_up=g_ub[GLA_RANK:2 * GLA_RANK], gb_b=g_gb_b, gla_norm_g=g_gla_norm, attn_norm_g=g_attn_norm,
             w_out=g_w_out, norm2_g=g_norm2, w_gate=g_w_gate, w_up=g_w_up, conv_w=g_conv_w, conv_b=g_conv_b,
             w_down=g_w_down, final_norm_g=g_final)
    return loss_acc, grad_x, g


def _me_and_peers():
    x, y, c = lax.axis_index("x"), lax.axis_index("y"), lax.axis_index("c")
    me = 4 * x + 2 * y + c
    peers = []
    for kbits in range(1, N_DEV):
        px, py, pc = x ^ (kbits >> 2 & 1), y ^ (kbits >> 1 & 1), c ^ (kbits & 1)
        peers.append(((px, py, pc), 4 * px + 2 * py + pc))
    return me, peers


_HBM = pl.BlockSpec(memory_space=pltpu.HBM)
_SEM = pl.BlockSpec(memory_space=pltpu.SEMAPHORE)
_ANY = pl.BlockSpec(memory_space=pl.ANY)
_EFFECT = pltpu.SideEffectType.DATAFLOW_SIDE_EFFECTING


def _exchange_copies(src_refs, land_refs, send_sems, recv_sems, scatter):
    me, peers = _me_and_peers()
    out = []
    for a, (src, land) in enumerate(zip(src_refs, land_refs)):
        for kk, (dev, idx) in enumerate(peers):
            out.append(pltpu.make_async_remote_copy(
                src_ref=src.at[idx] if scatter else src, dst_ref=land.at[me],
                send_sem=send_sems.at[a * (N_DEV - 1) + kk], recv_sem=recv_sems.at[a * (N_DEV - 1) + kk],
                device_id=dev, device_id_type=MESH_ID))
    return out


def _exchange_start(srcs, lands, scatter, name, deps=()):
    n, nd = len(srcs), len(deps)

    def body(*refs):
        src_refs, land_refs = refs[:n], refs[n:2 * n]
        send_sems, recv_sems = refs[2 * n + nd:2 * n + nd + 2]
        token = refs[-1]
        for cp in _exchange_copies(src_refs, land_refs, send_sems, recv_sems, scatter):
            cp.start()
        token[...] = jnp.zeros_like(token)

    outs = pl.pallas_call(
        body, name=name,
        in_specs=[_HBM] * (2 * n) + [_ANY] * nd,
        out_specs=[_SEM, _SEM] + [_HBM] * (2 * n) + [pl.BlockSpec(memory_space=pltpu.VMEM)],
        out_shape=[pltpu.SemaphoreType.DMA((n * (N_DEV - 1),)), pltpu.SemaphoreType.DMA((n * (N_DEV - 1),))]
        + [pltpu.HBM(t.shape, t.dtype) for t in srcs] + [pltpu.HBM(t.shape, t.dtype) for t in lands]
        + [jax.ShapeDtypeStruct((SUBLANES, LANES), F32)],
        input_output_aliases={i: 2 + i for i in range(2 * n)},
        compiler_params=pltpu.CompilerParams(has_side_effects=_EFFECT),
    )(*[pltpu.with_memory_space_constraint(t, pltpu.HBM) for t in list(srcs) + list(lands)], *deps)
    send_sems, recv_sems = outs[0], outs[1]
    return dict(send=send_sems, recv=recv_sems, srcs=outs[2:2 + n], lands=outs[2 + n:2 + 2 * n],
                scatter=scatter, token=outs[-1])


def _exchange_wait(started, name, after):
    n = len(started["srcs"])
    scatter = started["scatter"]

    def body(*refs):
        src_refs, land_refs = refs[:n], refs[n:2 * n]
        send_sems, recv_sems = refs[2 * n], refs[2 * n + 1]
        for cp in _exchange_copies(src_refs, land_refs, send_sems, recv_sems, scatter):
            cp.wait_send()
            cp.wait_recv()

    outs = pl.pallas_call(
        body, name=name,
        in_specs=[_HBM] * (2 * n) + [_SEM, _SEM, _ANY],
        out_specs=[_HBM] * (2 * n),
        out_shape=[pltpu.HBM(t.shape, t.dtype) for t in started["srcs"]]
        + [pltpu.HBM(t.shape, t.dtype) for t in started["lands"]],
        input_output_aliases={i: i for i in range(2 * n)},
        compiler_params=pltpu.CompilerParams(has_side_effects=_EFFECT),
    )(*started["srcs"], *started["lands"], started["send"], started["recv"], after)
    return outs[:n], outs[n:]


def _all_gather_two_level(shard, name):
    def body(x_ref, out_ref, send_sems, recv_sems, local_sem):
        x, y, c = lax.axis_index("x"), lax.axis_index("y"), lax.axis_index("c")
        me, sibling = (x, y, c), (x, y, 1 - c)
        chips = [(1 - x, y), (x, 1 - y), (1 - x, 1 - y)]

        def slot(px, py, pc):
            return out_ref.at[4 * px + 2 * py + pc]

        def copy(k, block, to, src=None):
            return pltpu.make_async_remote_copy(
                src_ref=slot(*block) if src is None else src, dst_ref=slot(*block),
                send_sem=send_sems.at[k], recv_sem=recv_sems.at[k], device_id=to, device_id_type=MESH_ID)

        mine = pltpu.make_async_copy(x_ref, slot(*me), local_sem)
        mine.start()
        first = [copy(0, me, sibling, src=x_ref)]
        first += [copy(1 + j, me, (*chip, c), src=x_ref) for j, chip in enumerate(chips)]
        for cp in first:
            cp.start()
        passed = [copy(4 + j, (*chip, c), sibling) for j, chip in enumerate(chips)]
        for j, chip in enumerate(chips):
            copy(1 + j, (*chip, c), me).wait_recv()
            passed[j].start()
        copy(0, sibling, me).wait_recv()
        for j, chip in enumerate(chips):
            copy(4 + j, (*chip, 1 - c), me).wait_recv()
        for cp in first + passed:
            cp.wait_send()
        mine.wait()

    return pl.pallas_call(
        body, name=name,
        in_specs=[_ANY], out_specs=_ANY,
        out_shape=jax.ShapeDtypeStruct((N_DEV,) + shard.shape, shard.dtype),
        scratch_shapes=[pltpu.SemaphoreType.DMA((N_DEV - 1,)), pltpu.SemaphoreType.DMA((N_DEV - 1,)),
                        pltpu.SemaphoreType.DMA],
    )(shard)


def _all_gather_vmem(vec, name):
    r = vec.shape[0]

    def body(v_ref, o_ref, send_sems, recv_sems):
        me, peers = _me_and_peers()
        o_ref[me] = v_ref[...]
        sends = []
        for kk, (dev, _) in enumerate(peers):
            cp = pltpu.make_async_remote_copy(
                src_ref=v_ref, dst_ref=o_ref.at[me],
                send_sem=send_sems.at[kk], recv_sem=recv_sems.at[kk],
                device_id=dev, device_id_type=MESH_ID)
            cp.start()
            sends.append(cp)
        for kk, (dev, idx) in enumerate(peers):
            pltpu.make_async_remote_copy(
                src_ref=v_ref, dst_ref=o_ref.at[idx],
                send_sem=send_sems.at[kk], recv_sem=recv_sems.at[kk],
                device_id=dev, device_id_type=MESH_ID).wait_recv()
        for cp in sends:
            cp.wait_send()

    return pl.pallas_call(
        body, name=name,
        in_specs=[pl.BlockSpec(memory_space=pltpu.VMEM)],
        out_specs=pl.BlockSpec(memory_space=pltpu.VMEM),
        out_shape=jax.ShapeDtypeStruct((N_DEV, r, LANES), F32),
        scratch_shapes=[pltpu.SemaphoreType.DMA((N_DEV - 1,)), pltpu.SemaphoreType.DMA((N_DEV - 1,))],
        compiler_params=pltpu.CompilerParams(vmem_limit_bytes=VMEM_LIMIT),
    )(vec)


def _adamw_math(w, g, m, v):
    m = ADAM_B1 * m + (1.0 - ADAM_B1) * g
    v = ADAM_B2 * v + (1.0 - ADAM_B2) * (g * g)
    m_hat = m / (1.0 - ADAM_B1 ** ADAM_STEP)
    v_hat = v / (1.0 - ADAM_B2 ** ADAM_STEP)
    delta = -ADAM_LR * (m_hat / (jnp.sqrt(v_hat) + ADAM_EPS) + ADAM_WD * w)
    return delta, m, v


def _adamw_sum(parts, w, m, v, tr, name, own=None, me=None):
    r, c = w.shape

    def body(*refs):
        if own is None:
            p_ref, w_ref, m_ref, v_ref, g_ref, d_ref, nm_ref, nv_ref = refs
            terms = [p_ref[kk] for kk in range(N_DEV)]
        else:
            me_ref, p_ref, own_ref, w_ref, m_ref, v_ref, g_ref, d_ref, nm_ref, nv_ref = refs
            terms = [jnp.where(me_ref[0] == kk, own_ref[0], p_ref[kk]).astype(F32) for kk in range(N_DEV)]
        g = terms[0]
        for t in terms[1:]:
            g = g + t
        g_ref[...] = g
        d_ref[...], nm_ref[...], nv_ref[...] = _adamw_math(w_ref[...], g, m_ref[...], v_ref[...])

    out_shape = [jax.ShapeDtypeStruct((r, c), F32)] * 4
    if own is None:
        blk = pl.BlockSpec((tr, c), lambda i: (i, 0))
        return pl.pallas_call(
            body, name=name, grid=(r // tr,),
            in_specs=[pl.BlockSpec((N_DEV, tr, c), lambda i: (0, i, 0)), blk, blk, blk],
            out_specs=[blk] * 4, out_shape=out_shape,
            compiler_params=_params(("parallel",)),
        )(parts, w, m, v)
    blk = pl.BlockSpec((tr, c), lambda i, me_ref: (i, 0))
    return pl.pallas_call(
        body, name=name,
        grid_spec=pltpu.PrefetchScalarGridSpec(
            num_scalar_prefetch=1, grid=(r // tr,),
            in_specs=[pl.BlockSpec((N_DEV, tr, c), lambda i, me_ref: (0, i, 0)),
                      pl.BlockSpec((1, tr, c), lambda i, me_ref: (me_ref[0], i, 0)), blk, blk, blk],
            out_specs=[blk] * 4),
        out_shape=out_shape,
        compiler_params=_params(("parallel",)),
    )(jnp.reshape(me, (1,)).astype(jnp.int32), parts, own, w, m, v)


def _slabs_to_wide(slabs, width, name):
    n, r, c = slabs.shape

    def body(i_ref, o_ref):
        for k in range(n):
            o_ref[:, c * k:c * (k + 1)] = i_ref[k]
        if width > n * c:
            o_ref[:, n * c:width] = jnp.zeros((ROW_BLOCK, width - n * c), o_ref.dtype)

    return pl.pallas_call(
        body, name=name, grid=(r // ROW_BLOCK,),
        in_specs=[pl.BlockSpec((n, ROW_BLOCK, c), lambda i: (0, i, 0))],
        out_specs=pl.BlockSpec((ROW_BLOCK, width), lambda i: (i, 0)),
        out_shape=jax.ShapeDtypeStruct((r, width), slabs.dtype),
        compiler_params=_params(("parallel",)),
    )(slabs)


def _wide_to_slabs(wide, c, name):
    r, width = wide.shape

    def body(i_ref, o_ref):
        for k in range(N_DEV):
            o_ref[k] = i_ref[:, c * k:c * (k + 1)]

    return pl.pallas_call(
        body, name=name, grid=(r // ROW_BLOCK,),
        in_specs=[pl.BlockSpec((ROW_BLOCK, width), lambda i: (i, 0))],
        out_specs=pl.BlockSpec((N_DEV, ROW_BLOCK, c), lambda i: (0, i, 0)),
        out_shape=jax.ShapeDtypeStruct((N_DEV, r, c), wide.dtype),
        compiler_params=_params(("parallel",)),
    )(wide)


_SMALL = ("norm1_g", "gf_b", "gb_b", "gla_norm_g", "attn_norm_g", "norm2_g", "conv_b", "final_norm_g",
          "gf_up", "gb_up", "conv_w")


def _pack(named):
    flat = jnp.concatenate([jnp.ravel(t).astype(F32) for t in named])
    tile = SUBLANES * LANES
    total = -(-flat.shape[0] // tile) * tile
    return jnp.pad(flat, (0, total - flat.shape[0])).reshape(total // LANES, LANES)


def _unpack(packed, shapes):
    flat = packed.reshape(-1)
    out, off = [], 0
    for shp in shapes:
        size = int(np.prod(shp))
        out.append(flat[off:off + size].reshape(shp))
        off += size
    return out


def kernel(x, norm1_g, w_in, gf_up, gf_b, gb_up, gb_b, gla_norm_g, attn_norm_g, w_out, norm2_g, w_gate, w_up, conv_w, conv_b, w_down, final_norm_g, loss_target, m_norm1_g, m_w_in, m_gf_up, m_gf_b, m_gb_up, m_gb_b, m_gla_norm_g, m_attn_norm_g, m_w_out, m_norm2_g, m_w_gate, m_w_up, m_conv_w, m_conv_b, m_w_down, m_final_norm_g, v_norm1_g, v_w_in, v_gf_up, v_gf_b, v_gb_up, v_gb_b, v_gla_norm_g, v_attn_norm_g, v_w_out, v_norm2_g, v_w_gate, v_w_up, v_conv_w, v_conv_b, v_w_down, v_final_norm_g):
    names = ("norm1_g", "w_in", "gf_up", "gf_b", "gb_up", "gb_b", "gla_norm_g", "attn_norm_g", "w_out", "norm2_g",
             "w_gate", "w_up", "conv_w", "conv_b", "w_down", "final_norm_g")
    ws = dict(zip(names, (norm1_g, w_in, gf_up, gf_b, gb_up, gb_b, gla_norm_g, attn_norm_g, w_out, norm2_g,
                          w_gate, w_up, conv_w, conv_b, w_down, final_norm_g)))
    ms = dict(zip(names, (m_norm1_g, m_w_in, m_gf_up, m_gf_b, m_gb_up, m_gb_b, m_gla_norm_g, m_attn_norm_g, m_w_out,
                          m_norm2_g, m_w_gate, m_w_up, m_conv_w, m_conv_b, m_w_down, m_final_norm_g)))
    vs = dict(zip(names, (v_norm1_g, v_w_in, v_gf_up, v_gf_b, v_gb_up, v_gb_b, v_gla_norm_g, v_attn_norm_g, v_w_out,
                          v_norm2_g, v_w_gate, v_w_up, v_conv_w, v_conv_b, v_w_down, v_final_norm_g)))
    me = 4 * lax.axis_index("x") + 2 * lax.axis_index("y") + lax.axis_index("c")
    big = ("w_in", "w_out", "w_gate", "w_up", "w_down")
    col_sharded = ("w_in", "w_gate", "w_up")

    def gather_start(group, name, deps=()):
        shards = [ws[n][0].astype(BF16) for n in group]
        lands = [lax.empty((N_DEV,) + t.shape, BF16) for t in shards]
        return _exchange_start(shards, lands, False, name, deps)

    def gather_finish(group, started, name, after):
        full = {}
        for n, own, t in zip(group, *_exchange_wait(started, name, after)):
            t = lax.dynamic_update_slice(t, own[None], (me, 0, 0))
            if n in col_sharded:
                full[n] = _slabs_to_wide(t, N_DEV * t.shape[2], "widen_" + n)
            else:
                full[n] = t.reshape(N_DEV * t.shape[1], t.shape[2])
        return full

    w_in_all = _all_gather_two_level(ws["w_in"][0].astype(BF16), "gather_w_in")
    full = {"w_in": _slabs_to_wide(w_in_all, IN_PAD, "widen_w_in")}
    late = {"mixer": ("w_out",), "ffn": ("w_gate", "w_up", "w_down")}
    started_late = {"mixer": gather_start(late["mixer"], "gather_w_out_start", deps=(full["w_in"],))}
    started_late["ffn"] = gather_start(late["ffn"], "gather_ffn_start", deps=(started_late["mixer"]["token"],))

    def late_weights(part, after):
        return gather_finish(late[part], started_late[part], "gather_" + part + "_wait", after)

    small_sharded = ("gf_up", "gb_up", "conv_w")
    sm = _all_gather_vmem(_pack([ws[n][0] for n in small_sharded]), "gather_small")
    shard_shapes = [ws[n][0].shape for n in small_sharded]
    per_dev = [_unpack(sm[d], shard_shapes) for d in range(N_DEV)]
    for i, n in enumerate(small_sharded):
        full[n] = jnp.concatenate([per_dev[d][i] for d in range(N_DEV)], axis=1)
    for n in ("norm1_g", "gf_b", "gb_b", "gla_norm_g", "attn_norm_g", "norm2_g", "conv_b"):
        full[n] = ws[n]
    full["final_norm_g"] = final_norm_g.reshape(1, D_MODEL)

    in_flight = []

    def grad_sink(group, grads):
        partials = []
        for n, t in zip(group, grads):
            t = t.astype(BF16)
            if n in col_sharded:
                t = _wide_to_slabs(t, ws[n].shape[2], "slabs_" + n)
            else:
                t = t.reshape(N_DEV, t.shape[0] // N_DEV, t.shape[1])
            partials.append(t)
        lands = [lax.empty(t.shape, t.dtype) for t in partials]
        started = _exchange_start(partials, lands, True, "exchange_" + "_".join(group) + "_start")
        in_flight.append((group, started))
        return (started["token"],)

    loss_acc, grad_x, g = _local_step(x[0], loss_target[0], full, late_weights, grad_sink,
                                      first_dep=(started_late["ffn"]["token"],))

    out = {}
    for group, started in in_flight:
        sent, landed = _exchange_wait(started, "exchange_" + "_".join(group) + "_wait", grad_x)
        for n, parts, own in zip(group, landed, sent):
            rows = ws[n].shape[1]
            tr = max(t for t in range(HALO, ROW_BLOCK + 1, HALO) if rows % t == 0)
            out[n] = _adamw_sum(parts, ws[n][0], ms[n][0], vs[n][0], tr, "adamw_" + n, own=own, me=me)

    small_full_shapes = [g[n].shape for n in _SMALL]
    gsmall = _pack([g[n] for n in _SMALL] + [loss_acc[0:1, 0:1]])
    gathered_small = _all_gather_vmem(gsmall, "gather_small_grads")

    def full_small(d):
        parts = []
        for n in _SMALL:
            t = d[n].reshape(d[n].shape[-2:]) if d[n].ndim == 3 else d[n].reshape(1, -1)
            if n in small_sharded:
                wide = jnp.zeros((t.shape[0], t.shape[1] * N_DEV), F32)
                t = lax.dynamic_update_slice_in_dim(wide, t, me * t.shape[1], axis=1)
            parts.append(t)
        return _pack(parts + [jnp.zeros((1, 1), F32)])

    rows = gsmall.shape[0]
    res_small = _adamw_sum(gathered_small, full_small(ws), full_small(ms), full_small(vs), rows, "adamw_small")
    loss = res_small[0].reshape(-1)[sum(int(np.prod(sh)) for sh in small_full_shapes)]
    unpacked = [_unpack(t, small_full_shapes) for t in res_small]
    for i, n in enumerate(_SMALL):
        vals = [u[i] for u in unpacked]
        if n in small_sharded:
            width = vals[0].shape[1] // N_DEV
            vals = [lax.dynamic_slice_in_dim(t, me * width, width, axis=1) for t in vals]
        out[n] = vals

    result = [loss, grad_x[None]]
    for kind in range(4):
        for n in names:
            result.append(out[n][kind].reshape(ws[n].shape))
    return tuple(result)
```

```python
import functools

import numpy as np
import jax
import jax.numpy as jnp
from jax import lax
from jax.experimental import pallas as pl
from jax.experimental.pallas import tpu as pltpu

F32 = jnp.float32
BF16 = jnp.bfloat16

D_MODEL = 2048
ATTN_W = 1024
ATTN_HEADS = 8
HEAD_DIM = 128
ROPE_DIM = 32
ROPE_THETA = 500000.0
DILATIONS = (1, 4, 16)
N_SIDE = 64
GLA_KW = 512
GLA_VW = 1024
GLA_HEADS = 4
GLA_DK = 128
GLA_DV = 256
GLA_RANK = 16
GLA_GATE_NORM = 16.0
GLA_CHUNK = 64
IN_WIDTH = 6176
IN_PAD = 6400
D_FF = 5632
EPS = 1e-6
N_DEV = 8

OFF_AQ, OFF_AK, OFF_AV = 0, 1024, 2048
OFF_GQ, OFF_GK, OFF_GV, OFF_GR, OFF_Z = 3072, 3584, 4096, 5120, 6144

ADAM_LR, ADAM_B1, ADAM_B2, ADAM_EPS, ADAM_WD, ADAM_STEP = 0.001, 0.9, 0.999, 1e-08, 0.01, 10

LANES = 128
SUBLANES = 8
VMEM_LIMIT = 56 * 1024 * 1024
ROW_BLOCK = 256
ATTN_BLOCK = 128
GLA_CHUNKS_PER_STEP = 8
NEG = -1e30
MESH_ID = pl.DeviceIdType.MESH


def _params(sem):
    return pltpu.CompilerParams(dimension_semantics=sem, vmem_limit_bytes=VMEM_LIMIT)


def _dot(a, b):
    return lax.dot_general(a, b, (((1,), (0,)), ((), ())), preferred_element_type=F32)


def _dot_nt(a, b):
    return lax.dot_general(a, b, (((1,), (1,)), ((), ())), preferred_element_type=F32)


def _dot_tn(a, b):
    return lax.dot_general(a, b, (((0,), (0,)), ((), ())), preferred_element_type=F32)


def _sigmoid(x):
    return 0.5 * jnp.tanh(0.5 * x) + 0.5


def _matmul(pairs, mode, out_dtype, tm, tn, tk, name, res=None, deps=()):
    a0, b0 = pairs[0]
    if mode == "nn":
        (m, kdim), n = a0.shape, b0.shape[1]
    elif mode == "nt":
        (m, kdim), n = a0.shape, b0.shape[0]
    else:
        (kdim, m), n = a0.shape, b0.shape[1]
    assert m % tm == 0 and n % tn == 0 and kdim % tk == 0, (name, m, n, kdim)
    nk = kdim // tk
    npairs = len(pairs)
    steps = nk * npairs
    dot = {"nn": _dot, "nt": _dot_nt, "tn": _dot_tn}[mode]

    def kidx(p):
        return lambda k: jnp.clip(k - p * nk, 0, nk - 1)

    in_specs, args = [], []
    for p, (a, b) in enumerate(pairs):
        kk = kidx(p)
        if mode == "nn":
            in_specs += [pl.BlockSpec((tm, tk), lambda i, j, k, kk=kk: (i, kk(k))),
                         pl.BlockSpec((tk, tn), lambda i, j, k, kk=kk: (kk(k), j))]
        elif mode == "nt":
            in_specs += [pl.BlockSpec((tm, tk), lambda i, j, k, kk=kk: (i, kk(k))),
                         pl.BlockSpec((tn, tk), lambda i, j, k, kk=kk: (j, kk(k)))]
        else:
            in_specs += [pl.BlockSpec((tk, tm), lambda i, j, k, kk=kk: (kk(k), i)),
                         pl.BlockSpec((tk, tn), lambda i, j, k, kk=kk: (kk(k), j))]
        args += [a, b]
    if res is not None:
        in_specs.append(pl.BlockSpec((tm, tn), lambda i, j, k: (i, j)))
        args.append(res)
    in_specs += [pl.BlockSpec(memory_space=pl.ANY)] * len(deps)
    args += list(deps)

    def body(*refs):
        ab = refs[:2 * npairs]
        res_ref = refs[2 * npairs] if res is not None else None
        o_ref = refs[2 * npairs + (1 if res is not None else 0) + len(deps)]

        def finish(acc):
            if res_ref is not None:
                acc = acc + res_ref[...]
            o_ref[...] = acc.astype(out_dtype)

        if steps == 1:
            finish(dot(ab[0][...], ab[1][...]))
            return
        acc_ref = refs[-1]
        k = pl.program_id(2)

        @pl.when(k == 0)
        def _():
            acc_ref[...] = jnp.zeros_like(acc_ref)

        for p in range(npairs):
            @pl.when((k >= p * nk) & (k < (p + 1) * nk))
            def _(p=p):
                acc_ref[...] += dot(ab[2 * p][...], ab[2 * p + 1][...])

        @pl.when(k == steps - 1)
        def _():
            finish(acc_ref[...])

    return pl.pallas_call(
        body, name=name,
        grid=(m // tm, n // tn, steps),
        in_specs=in_specs,
        out_specs=pl.BlockSpec((tm, tn), lambda i, j, k: (i, j)),
        out_shape=jax.ShapeDtypeStruct((m, n), out_dtype),
        scratch_shapes=[] if steps == 1 else [pltpu.VMEM((tm, tn), F32)],
        compiler_params=_params(("parallel", "parallel", "arbitrary")),
    )(*args)


def _rms_fwd(x, g, name):
    s, d = x.shape

    def body(x_ref, g_ref, o_ref):
        xv = x_ref[...]
        r = lax.rsqrt(jnp.mean(xv * xv, axis=-1, keepdims=True) + EPS)
        o_ref[...] = (xv * r * g_ref[...]).astype(BF16)

    return pl.pallas_call(
        body, name=name, grid=(s // ROW_BLOCK,),
        in_specs=[pl.BlockSpec((ROW_BLOCK, d), lambda i: (i, 0)), pl.BlockSpec((1, d), lambda i: (0, 0))],
        out_specs=pl.BlockSpec((ROW_BLOCK, d), lambda i: (i, 0)),
        out_shape=jax.ShapeDtypeStruct((s, d), BF16),
        compiler_params=_params(("parallel",)),
    )(x, g)


def _rms_bwd(dn, x, g, dres, name, bf16_copy=True):
    s, d = x.shape

    def body(dn_ref, x_ref, g_ref, dres_ref, dx_ref, *rest):
        gg_ref = rest[-1]
        i = pl.program_id(0)
        xv, dnv = x_ref[...], dn_ref[...]
        r = lax.rsqrt(jnp.mean(xv * xv, axis=-1, keepdims=True) + EPS)
        dng = dnv * g_ref[...]
        c = jnp.mean(dng * xv, axis=-1, keepdims=True)
        dx = dres_ref[...] + r * dng - xv * (r * r * r * c)
        dx_ref[...] = dx
        if bf16_copy:
            rest[0][...] = dx.astype(BF16)

        @pl.when(i == 0)
        def _():
            gg_ref[...] = jnp.zeros_like(gg_ref)

        gg_ref[...] += jnp.sum(dnv * xv * r, axis=0, keepdims=True)

    row = pl.BlockSpec((ROW_BLOCK, d), lambda i: (i, 0))
    vec = pl.BlockSpec((1, d), lambda i: (0, 0))
    return pl.pallas_call(
        body, name=name, grid=(s // ROW_BLOCK,),
        in_specs=[row, row, vec, row],
        out_specs=[row] + [row] * bf16_copy + [vec],
        out_shape=[jax.ShapeDtypeStruct((s, d), F32)] + [jax.ShapeDtypeStruct((s, d), BF16)] * bf16_copy
        + [jax.ShapeDtypeStruct((1, d), F32)],
        compiler_params=_params(("arbitrary",)),
    )(dn, x, g, dres)


def _final_loss(h2, target, g, name="final_loss"):
    s, d = h2.shape

    def body(h_ref, t_ref, g_ref, dh_ref, dhb_ref, loss_ref, gg_ref):
        i = pl.program_id(0)
        hv, gv = h_ref[...], g_ref[...]
        r = lax.rsqrt(jnp.mean(hv * hv, axis=-1, keepdims=True) + EPS)
        e = hv * r * gv - t_ref[...]
        dy = e * (1.0 / d)
        dyg = dy * gv
        c = jnp.mean(dyg * hv, axis=-1, keepdims=True)
        dh = r * dyg - hv * (r * r * r * c)
        dh_ref[...] = dh
        dhb_ref[...] = dh.astype(BF16)

        @pl.when(i == 0)
        def _():
            gg_ref[...] = jnp.zeros_like(gg_ref)
            loss_ref[...] = jnp.zeros_like(loss_ref)

        gg_ref[...] += jnp.sum(dy * hv * r, axis=0, keepdims=True)
        loss_ref[...] += jnp.sum(jnp.sum(e * e, axis=-1, keepdims=True), axis=0, keepdims=True) * (0.5 / d)

    row = pl.BlockSpec((ROW_BLOCK, d), lambda i: (i, 0))
    vec = pl.BlockSpec((1, d), lambda i: (0, 0))
    return pl.pallas_call(
        body, name=name, grid=(s // ROW_BLOCK,),
        in_specs=[row, row, vec],
        out_specs=[row, row, pl.BlockSpec((SUBLANES, LANES), lambda i: (0, 0)), vec],
        out_shape=[jax.ShapeDtypeStruct((s, d), F32), jax.ShapeDtypeStruct((s, d), BF16),
                   jax.ShapeDtypeStruct((SUBLANES, LANES), F32), jax.ShapeDtypeStruct((1, d), F32)],
        compiler_params=_params(("arbitrary",)),
    )(h2, target, g)


def _rope_tables(s):
    pos = jnp.arange(s, dtype=F32)
    inv_freq = ROPE_THETA ** (-jnp.arange(0, ROPE_DIM, 2, dtype=F32) / ROPE_DIM)
    ang = pos[:, None] * inv_freq[None, :]
    cos, sin = jnp.cos(ang), jnp.sin(ang)
    half = ROPE_DIM // 2
    rest = HEAD_DIM - ROPE_DIM
    c = jnp.concatenate([cos, cos, jnp.ones((s, rest), F32)], axis=1)
    sm = jnp.concatenate([-sin, jnp.zeros((s, half + rest), F32)], axis=1)
    sp = jnp.concatenate([jnp.zeros((s, half), F32), sin, jnp.zeros((s, rest), F32)], axis=1)
    return c, sm, sp


def _res_shape(s, groups, dil, dtype):
    return jax.ShapeDtypeStruct((s // dil, dil * groups * LANES), dtype)


def _res_spec(groups, dil):
    return pl.BlockSpec((ROW_BLOCK // dil, dil * groups * LANES), lambda i: (i, 0))


def _to_residues(scr, o_ref, dil):
    groups, rows = scr.shape[0], ROW_BLOCK // dil
    for r in range(dil):
        for h in range(groups):
            piece = scr[h] if dil == 1 else scr.at[h][pl.ds(r, rows, stride=dil), :]
            o_ref[:, (r * groups + h) * LANES:(r * groups + h + 1) * LANES] = piece.astype(o_ref.dtype)


def _from_residues(i_ref, scr, dil):
    groups, rows = scr.shape[0], ROW_BLOCK // dil
    for r in range(dil):
        for h in range(groups):
            piece = i_ref[:, (r * groups + h) * LANES:(r * groups + h + 1) * LANES].astype(F32)
            if dil == 1:
                scr[h] = piece
            else:
                scr.at[h][pl.ds(r, rows, stride=dil), :] = piece


def _rope_fwd(proj, tables, name="rope_fwd"):
    s = proj.shape[0]
    half = ROPE_DIM // 2
    nd = len(DILATIONS)

    def body(p_ref, c_ref, sm_ref, sp_ref, *rest):
        outs, scr = rest[:3 * nd], rest[3 * nd]
        c, sm, sp = c_ref[...], sm_ref[...], sp_ref[...]
        for gi, off in enumerate((OFF_AQ, OFF_AK, OFF_AV)):
            for h in range(ATTN_HEADS):
                t = p_ref[:, off + h * HEAD_DIM: off + (h + 1) * HEAD_DIM]
                if off != OFF_AV:
                    t = t * c + pltpu.roll(t, HEAD_DIM - half, 1) * sm + pltpu.roll(t, half, 1) * sp
                scr[h] = t
            for di, dil in enumerate(DILATIONS):
                _to_residues(scr, outs[3 * di + gi], dil)

    tab = pl.BlockSpec((ROW_BLOCK, HEAD_DIM), lambda i: (i, 0))
    outs = pl.pallas_call(
        body, name=name, grid=(s // ROW_BLOCK,),
        in_specs=[pl.BlockSpec((ROW_BLOCK, 3 * ATTN_W), lambda i: (i, 0)), tab, tab, tab],
        out_specs=[_res_spec(ATTN_HEADS, d) for d in DILATIONS for _ in range(3)],
        out_shape=[_res_shape(s, ATTN_HEADS, d, BF16) for d in DILATIONS for _ in range(3)],
        scratch_shapes=[pltpu.VMEM((ATTN_HEADS, ROW_BLOCK, LANES), F32)],
        compiler_params=_params(("parallel",)),
    )(proj, *tables)
    return [tuple(outs[3 * di:3 * di + 3]) for di in range(nd)]


def _rope_bwd(grads, tables, name="rope_bwd"):
    s = grads[0][0].shape[0] * DILATIONS[0]
    half = ROPE_DIM // 2
    nd = len(DILATIONS)

    def body(*refs):
        ins = refs[:3 * nd]
        c_ref, sm_ref, sp_ref, o_ref = refs[3 * nd:3 * nd + 4]
        scrs = refs[3 * nd + 4:]
        c, sm, sp = c_ref[...], sm_ref[...], sp_ref[...]
        for gi, off in enumerate((OFF_AQ, OFF_AK, OFF_AV)):
            for di, dil in enumerate(DILATIONS):
                _from_residues(ins[3 * di + gi], scrs[di], dil)
            for h in range(ATTN_HEADS):
                t = scrs[0][h]
                for scr in scrs[1:]:
                    t = t + scr[h]
                if off != OFF_AV:
                    t = t * c + pltpu.roll(t * sm, half, 1) + pltpu.roll(t * sp, HEAD_DIM - half, 1)
                o_ref[:, off + h * HEAD_DIM: off + (h + 1) * HEAD_DIM] = t.astype(BF16)

    tab = pl.BlockSpec((ROW_BLOCK, HEAD_DIM), lambda i: (i, 0))
    return pl.pallas_call(
        body, name=name, grid=(s // ROW_BLOCK,),
        in_specs=[_res_spec(ATTN_HEADS, d) for d in DILATIONS for _ in range(3)] + [tab, tab, tab],
        out_specs=pl.BlockSpec((ROW_BLOCK, 3 * ATTN_W), lambda i: (i, 0)),
        out_shape=jax.ShapeDtypeStruct((s, IN_PAD), BF16),
        scratch_shapes=[pltpu.VMEM((ATTN_HEADS, ROW_BLOCK, LANES), F32) for _ in DILATIONS],
        compiler_params=_params(("parallel",)),
    )(*[t for g in grads for t in g], *tables)


ATTN_GROUP = 4


def _window_specs(nsteps, width):
    rows, hb = ATTN_GROUP * ATTN_BLOCK, N_SIDE
    per = rows // hb
    cur = pl.BlockSpec((rows, width), lambda r, j: (j, r))
    prev = pl.BlockSpec((hb, width), lambda r, j: (jnp.maximum(per * j - 1, 0), r))
    nxt = pl.BlockSpec((hb, width), lambda r, j: (jnp.minimum(per * (j + 1), per * nsteps - 1), r))
    return prev, cur, nxt


def _block(ref, b, sl):
    return ref[b * ATTN_BLOCK:(b + 1) * ATTN_BLOCK, sl]


def _edge(prev_ref, cur_ref, next_ref, b, sl):
    qb, hb = ATTN_BLOCK, N_SIDE
    before = prev_ref[:, sl] if b == 0 else cur_ref[b * qb - hb:b * qb, sl]
    after = next_ref[:, sl] if b == ATTN_GROUP - 1 else cur_ref[(b + 1) * qb:(b + 1) * qb + hb, sl]
    return jnp.concatenate([before, after], axis=0)


def _band_masks(j, length):
    qb, hb = ATTN_BLOCK, N_SIDE
    row = lax.broadcasted_iota(jnp.int32, (qb, qb), 0)
    col = lax.broadcasted_iota(jnp.int32, (qb, qb), 1)

    def edge_pos(i):
        return j * qb - hb + i + jnp.where(i >= hb, qb, 0)

    def ok(a, b, outside):
        return (jnp.abs(a - b) <= N_SIDE) & (outside >= 0) & (outside < length)

    cur = jnp.abs(row - col) <= N_SIDE
    edge_k = ok(j * qb + row, edge_pos(col), edge_pos(col))
    edge_q = ok(edge_pos(row), j * qb + col, edge_pos(row))
    return cur, edge_k, edge_q


def _attn_fwd(q, k, v, dil, name):
    length = q.shape[0]
    qb = ATTN_BLOCK
    nsteps = length // (ATTN_GROUP * qb)
    scale = HEAD_DIM ** -0.5

    def body(q_ref, kp_ref, kc_ref, kn_ref, vp_ref, vc_ref, vn_ref, o_ref, lse_ref):
        masks = [_band_masks(pl.program_id(1) * ATTN_GROUP + b, length) for b in range(ATTN_GROUP)]
        lane = lax.broadcasted_iota(jnp.int32, (qb, LANES), 1)
        units = [(b, h, slice(h * HEAD_DIM, (h + 1) * HEAD_DIM)) for b in range(ATTN_GROUP)
                 for h in range(ATTN_HEADS)]
        scores = [(_dot_nt(_block(q_ref, b, sl), _block(kc_ref, b, sl)),
                   _dot_nt(_block(q_ref, b, sl), _edge(kp_ref, kc_ref, kn_ref, b, sl))) for b, _, sl in units]
        probs = []
        lse_acc = [jnp.zeros((qb, LANES), F32) for _ in range(ATTN_GROUP)]
        for (b, h, _), (s_c, s_e) in zip(units, scores):
            valid_c, valid_e, _ = masks[b]
            s_c = jnp.where(valid_c, s_c * scale, NEG)
            s_e = jnp.where(valid_e, s_e * scale, NEG)
            m = jnp.max(jnp.maximum(s_c, s_e), axis=-1, keepdims=True)
            p_c, p_e = jnp.exp(s_c - m), jnp.exp(s_e - m)
            den = jnp.sum(p_c + p_e, axis=-1, keepdims=True)
            probs.append((p_c.astype(BF16), p_e.astype(BF16), 1.0 / den))
            lse_acc[b] = jnp.where(lane == h, m + jnp.log(den), lse_acc[b])
        for (b, _, sl), (p_c, p_e, inv) in zip(units, probs):
            o_ref[b * qb:(b + 1) * qb, sl] = (_dot(p_c, _block(vc_ref, b, sl))
                                              + _dot(p_e, _edge(vp_ref, vc_ref, vn_ref, b, sl))) * inv
        for b in range(ATTN_GROUP):
            lse_ref[b * qb:(b + 1) * qb, :] = lse_acc[b]

    prev, cur, nxt = _window_specs(nsteps, ATTN_W)
    return pl.pallas_call(
        body, name=name, grid=(dil, nsteps),
        in_specs=[cur, prev, cur, nxt, prev, cur, nxt],
        out_specs=[cur, pl.BlockSpec((ATTN_GROUP * qb, LANES), lambda r, j: (j, r))],
        out_shape=[jax.ShapeDtypeStruct((length, dil * ATTN_W), F32),
                   jax.ShapeDtypeStruct((length, dil * LANES), F32)],
        compiler_params=_params(("parallel", "parallel")),
    )(q, k, k, k, v, v, v)


def _attn_combine(outs, lses, g, name="attn_combine"):
    s = outs[0].shape[0] * DILATIONS[0]
    nd = len(DILATIONS)

    def body(*refs):
        o_refs, l_refs = refs[:nd], refs[nd:2 * nd]
        g_ref, o_ref, n_ref = refs[2 * nd:2 * nd + 3]
        lse_outs = refs[2 * nd + 3:3 * nd + 3]
        o_scr, l_scr = refs[3 * nd + 3:4 * nd + 3], refs[4 * nd + 3:5 * nd + 3]
        for di, dil in enumerate(DILATIONS):
            _from_residues(o_refs[di], o_scr[di], dil)
            _from_residues(l_refs[di], l_scr[di], dil)
        ls = [scr[0] for scr in l_scr]
        m = ls[0]
        for l in ls[1:]:
            m = jnp.maximum(m, l)
        es = [jnp.exp(l - m) for l in ls]
        z = es[0]
        for e in es[1:]:
            z = z + e
        ws = [e / z for e in es]
        l_scr[0][0] = m + jnp.log(z)
        for di, dil in enumerate(DILATIONS):
            _to_residues(l_scr[0], lse_outs[di], dil)
        ssq = jnp.zeros((ROW_BLOCK, 1), F32)
        for h in range(ATTN_HEADS):
            sl = slice(h * HEAD_DIM, (h + 1) * HEAD_DIM)
            acc = ws[0][:, h:h + 1] * o_scr[0][h]
            for w, scr in zip(ws[1:], o_scr[1:]):
                acc = acc + w[:, h:h + 1] * scr[h]
            o_ref[:, sl] = acc
            ssq = ssq + jnp.sum(acc * acc, axis=-1, keepdims=True)
        r = lax.rsqrt(ssq * (1.0 / ATTN_W) + EPS)
        n_ref[...] = (o_ref[...] * r * g_ref[...]).astype(BF16)

    blk = pl.BlockSpec((ROW_BLOCK, ATTN_W), lambda i: (i, 0))
    outs_ = pl.pallas_call(
        body, name=name, grid=(s // ROW_BLOCK,),
        in_specs=[_res_spec(ATTN_HEADS, d) for d in DILATIONS] + [_res_spec(1, d) for d in DILATIONS]
        + [pl.BlockSpec((1, ATTN_W), lambda i: (0, 0))],
        out_specs=[blk, blk] + [_res_spec(1, d) for d in DILATIONS],
        out_shape=[jax.ShapeDtypeStruct((s, ATTN_W), F32), jax.ShapeDtypeStruct((s, D_MODEL), BF16)]
        + [_res_shape(s, 1, d, F32) for d in DILATIONS],
        scratch_shapes=[pltpu.VMEM((ATTN_HEADS, ROW_BLOCK, LANES), F32) for _ in DILATIONS]
        + [pltpu.VMEM((1, ROW_BLOCK, LANES), F32) for _ in DILATIONS],
        compiler_params=_params(("parallel",)),
    )(*outs, *lses, g)
    return outs_[0], outs_[1], list(outs_[2:])


def _attn_prebwd(dcat, o, g, name="attn_prebwd"):
    s = o.shape[0]
    nd = len(DILATIONS)

    def body(dy_ref, o_ref, g_ref, *rest):
        do_outs, delta_outs, gg_ref = rest[:nd], rest[nd:2 * nd], rest[2 * nd]
        do_scr, delta_scr = rest[2 * nd + 1], rest[2 * nd + 2]
        i = pl.program_id(0)
        dy, ov = dy_ref[...], o_ref[...]
        r = lax.rsqrt(jnp.mean(ov * ov, axis=-1, keepdims=True) + EPS)
        dyg = dy * g_ref[...]
        c = jnp.mean(dyg * ov, axis=-1, keepdims=True)
        do = r * dyg - ov * (r * r * r * c)
        prod = do * ov
        lane = lax.broadcasted_iota(jnp.int32, (ROW_BLOCK, LANES), 1)
        acc = jnp.zeros((ROW_BLOCK, LANES), F32)
        for h in range(ATTN_HEADS):
            sl = slice(h * HEAD_DIM, (h + 1) * HEAD_DIM)
            do_scr[h] = do[:, sl]
            acc = jnp.where(lane == h, jnp.sum(prod[:, sl], axis=-1, keepdims=True), acc)
        delta_scr[0] = acc
        for di, dil in enumerate(DILATIONS):
            _to_residues(do_scr, do_outs[di], dil)
            _to_residues(delta_scr, delta_outs[di], dil)

        @pl.when(i == 0)
        def _():
            gg_ref[...] = jnp.zeros_like(gg_ref)

        gg_ref[...] += jnp.sum(dy * ov * r, axis=0, keepdims=True)

    blk = pl.BlockSpec((ROW_BLOCK, ATTN_W), lambda i: (i, 0))
    vec = pl.BlockSpec((1, ATTN_W), lambda i: (0, 0))
    outs = pl.pallas_call(
        body, name=name, grid=(s // ROW_BLOCK,),
        in_specs=[blk, blk, vec],
        out_specs=[_res_spec(ATTN_HEADS, d) for d in DILATIONS] + [_res_spec(1, d) for d in DILATIONS] + [vec],
        out_shape=[_res_shape(s, ATTN_HEADS, d, BF16) for d in DILATIONS]
        + [_res_shape(s, 1, d, F32) for d in DILATIONS] + [jax.ShapeDtypeStruct((1, ATTN_W), F32)],
        scratch_shapes=[pltpu.VMEM((ATTN_HEADS, ROW_BLOCK, LANES), F32), pltpu.VMEM((1, ROW_BLOCK, LANES), F32)],
        compiler_params=_params(("arbitrary",)),
    )(dcat, o, g)
    return list(outs[:nd]), list(outs[nd:2 * nd]), outs[2 * nd]


def _attn_bwd(q, k, v, do, lse, delta, dil, name):
    length = q.shape[0]
    qb = ATTN_BLOCK
    nsteps = length // (ATTN_GROUP * qb)
    scale = HEAD_DIM ** -0.5

    def body(qp, qc, qn, kp, kc, kn, vp, vc, vn, dop, doc, don, lp, lc, ln, dp, dc, dn, dq_ref, dk_ref, dv_ref):
        masks = [_band_masks(pl.program_id(1) * ATTN_GROUP + b, length) for b in range(ATTN_GROUP)]
        everything = slice(None)
        lse_e = [_edge(lp, lc, ln, b, everything) for b in range(ATTN_GROUP)]
        del_e = [_edge(dp, dc, dn, b, everything) for b in range(ATTN_GROUP)]
        units = [(b, h, slice(h * HEAD_DIM, (h + 1) * HEAD_DIM)) for b in range(ATTN_GROUP)
                 for h in range(ATTN_HEADS)]
        prods = []
        for b, _, sl in units:
            q_c, k_c, v_c, do_c = _block(qc, b, sl), _block(kc, b, sl), _block(vc, b, sl), _block(doc, b, sl)
            q_e, k_e = _edge(qp, qc, qn, b, sl), _edge(kp, kc, kn, b, sl)
            v_e, do_e = _edge(vp, vc, vn, b, sl), _edge(dop, doc, don, b, sl)
            prods.append((_dot_nt(q_c, k_c), _dot_nt(do_c, v_c), _dot_nt(q_c, k_e), _dot_nt(do_c, v_e),
                          _dot_nt(q_e, k_c), _dot_nt(do_e, v_c)))
        parts = []
        for (b, h, _), (s_cc, dp_cc, s_ek, dp_ek, s_eq, dp_eq) in zip(units, prods):
            valid_c, valid_ek, valid_eq = masks[b]
            hc = slice(h, h + 1)
            lse_c, del_c = _block(lc, b, hc), _block(dc, b, hc)
            p_cc = jnp.where(valid_c, jnp.exp(s_cc * scale - lse_c), 0.0)
            ds_cc = (p_cc * (dp_cc - del_c)).astype(BF16)
            p_ek = jnp.where(valid_ek, jnp.exp(s_ek * scale - lse_c), 0.0)
            ds_ek = (p_ek * (dp_ek - del_c)).astype(BF16)
            p_eq = jnp.where(valid_eq, jnp.exp(s_eq * scale - lse_e[b][:, hc]), 0.0)
            ds_eq = (p_eq * (dp_eq - del_e[b][:, hc])).astype(BF16)
            parts.append((p_cc.astype(BF16), ds_cc, ds_ek, p_eq.astype(BF16), ds_eq))
        for (b, _, sl), (p_cc, ds_cc, ds_ek, p_eq, ds_eq) in zip(units, parts):
            rows = slice(b * qb, (b + 1) * qb)
            q_c, k_c, do_c = _block(qc, b, sl), _block(kc, b, sl), _block(doc, b, sl)
            q_e, k_e, do_e = _edge(qp, qc, qn, b, sl), _edge(kp, kc, kn, b, sl), _edge(dop, doc, don, b, sl)
            dq_ref[rows, sl] = ((_dot(ds_cc, k_c) + _dot(ds_ek, k_e)) * scale).astype(BF16)
            dk_ref[rows, sl] = ((_dot_tn(ds_cc, q_c) + _dot_tn(ds_eq, q_e)) * scale).astype(BF16)
            dv_ref[rows, sl] = (_dot_tn(p_cc, do_c) + _dot_tn(p_eq, do_e)).astype(BF16)

    wide, narrow = list(_window_specs(nsteps, ATTN_W)), list(_window_specs(nsteps, LANES))
    return tuple(pl.pallas_call(
        body, name=name, grid=(dil, nsteps),
        in_specs=wide * 4 + narrow * 2,
        out_specs=[wide[1]] * 3,
        out_shape=[jax.ShapeDtypeStruct((length, dil * ATTN_W), BF16)] * 3,
        compiler_params=_params(("parallel", "parallel")),
    )(q, q, q, k, k, k, v, v, v, do, do, do, lse, lse, lse, delta, delta, delta))


def _gate_matrices(gf_up, gb_up):
    pad = LANES - 2 * GLA_RANK
    uf = jnp.concatenate([gf_up, jnp.zeros((GLA_RANK + pad, GLA_KW), gf_up.dtype)], axis=0)
    ub = jnp.concatenate([jnp.zeros((GLA_RANK, GLA_KW), gb_up.dtype), gb_up, jnp.zeros((pad, GLA_KW), gb_up.dtype)], axis=0)
    return uf.astype(BF16), ub.astype(BF16)


def _log_sigmoid(x):
    return jnp.minimum(x, 0.0) - jnp.log(1.0 + jnp.exp(-jnp.abs(x)))


def _gla_gates(proj, uf, ub, gf_b, gb_b, name="gla_gates"):
    s = proj.shape[0]

    def body(z_ref, uf_ref, ub_ref, bf_ref, bb_ref, gf_ref, gb_ref):
        z = z_ref[...].astype(BF16)
        gf_ref[...] = _log_sigmoid(_dot(z, uf_ref[...]) + bf_ref[...]) * (1.0 / GLA_GATE_NORM)
        gb_ref[...] = _log_sigmoid(_dot(z, ub_ref[...]) + bb_ref[...]) * (1.0 / GLA_GATE_NORM)

    mat = pl.BlockSpec((LANES, GLA_KW), lambda i: (0, 0))
    vec = pl.BlockSpec((1, GLA_KW), lambda i: (0, 0))
    out = pl.BlockSpec((ROW_BLOCK, GLA_KW), lambda i: (i, 0))
    return pl.pallas_call(
        body, name=name, grid=(s // ROW_BLOCK,),
        in_specs=[pl.BlockSpec((ROW_BLOCK, LANES), lambda i: (i, OFF_Z // LANES)), mat, mat, vec, vec],
        out_specs=[out, out],
        out_shape=[jax.ShapeDtypeStruct((s, GLA_KW), F32)] * 2,
        compiler_params=_params(("parallel",)),
    )(proj, uf, ub, gf_b, gb_b)


def _gla_gates_bwd(dgf, dgb, proj, uf, ub, gf_b, gb_b, dproj, name="gla_gates_bwd"):
    s = proj.shape[0]
    tail = IN_PAD - OFF_Z

    def body(dgf_ref, dgb_ref, z_ref, uf_ref, ub_ref, bf_ref, bb_ref, _, dz_ref, guf_ref, gub_ref, gbf_ref, gbb_ref):
        i = pl.program_id(0)
        z = z_ref[...].astype(BF16)
        uf_, ub_ = uf_ref[...], ub_ref[...]
        dpf = dgf_ref[...] * (1.0 / GLA_GATE_NORM) * _sigmoid(-(_dot(z, uf_) + bf_ref[...]))
        dpb = dgb_ref[...] * (1.0 / GLA_GATE_NORM) * _sigmoid(-(_dot(z, ub_) + bb_ref[...]))
        dpf_b, dpb_b = dpf.astype(BF16), dpb.astype(BF16)
        dz_ref[:, 0:LANES] = (_dot_nt(dpf_b, uf_) + _dot_nt(dpb_b, ub_)).astype(BF16)
        dz_ref[:, LANES:tail] = jnp.zeros((ROW_BLOCK, tail - LANES), BF16)

        @pl.when(i == 0)
        def _():
            for r in (guf_ref, gub_ref, gbf_ref, gbb_ref):
                r[...] = jnp.zeros_like(r)

        guf_ref[...] += _dot_tn(z, dpf_b)
        gub_ref[...] += _dot_tn(z, dpb_b)
        gbf_ref[...] += jnp.sum(dpf, axis=0, keepdims=True)
        gbb_ref[...] += jnp.sum(dpb, axis=0, keepdims=True)

    mat = pl.BlockSpec((LANES, GLA_KW), lambda i: (0, 0))
    vec = pl.BlockSpec((1, GLA_KW), lambda i: (0, 0))
    blk = pl.BlockSpec((ROW_BLOCK, GLA_KW), lambda i: (i, 0))
    return pl.pallas_call(
        body, name=name, grid=(s // ROW_BLOCK,),
        in_specs=[blk, blk, pl.BlockSpec((ROW_BLOCK, LANES), lambda i: (i, OFF_Z // LANES)), mat, mat, vec, vec,
                  pl.BlockSpec(memory_space=pl.ANY)],
        out_specs=[pl.BlockSpec((ROW_BLOCK, tail), lambda i: (i, OFF_Z // tail)), mat, mat, vec, vec],
        out_shape=[jax.ShapeDtypeStruct(dproj.shape, dproj.dtype), jax.ShapeDtypeStruct((LANES, GLA_KW), F32),
                   jax.ShapeDtypeStruct((LANES, GLA_KW), F32), jax.ShapeDtypeStruct((1, GLA_KW), F32),
                   jax.ShapeDtypeStruct((1, GLA_KW), F32)],
        input_output_aliases={7: 0},
        compiler_params=_params(("arbitrary",)),
    )(dgf, dgb, proj, uf, ub, gf_b, gb_b, dproj)


def _split3(x):
    x1 = x.astype(BF16)
    r1 = x - x1.astype(F32)
    x2 = r1.astype(BF16)
    x3 = (r1 - x2.astype(F32)).astype(BF16)
    return x1, x2, x3


def _dot_exact(mask_bf, x):
    x1, x2, x3 = _split3(x)
    return _dot(mask_bf, x1) + _dot(mask_bf, x2) + _dot(mask_bf, x3)


def _chunk_masks(reverse):
    c = GLA_CHUNK
    row = lax.broadcasted_iota(jnp.int32, (c, c), 0)
    col = lax.broadcasted_iota(jnp.int32, (c, c), 1)
    allowed = (col >= row) if reverse else (col <= row)
    seen_by = (col <= row) if reverse else (col >= row)
    return allowed, seen_by


def _chunk_terms(q_ref, k_ref, g_ref, rs, hs, allowed, reverse):
    c = GLA_CHUNK
    mid, last = (c // 2, 0) if reverse else (c // 2 - 1, c - 1)
    q = q_ref[rs, hs] * (GLA_DK ** -0.5)
    k = k_ref[rs, hs]
    b = _dot_exact(jnp.where(allowed, 1.0, 0.0).astype(BF16), g_ref[rs, hs])
    bref, blast = b[mid:mid + 1, :], b[last:last + 1, :]
    e_q, e_k, e_in, e_st = jnp.exp(b - bref), jnp.exp(bref - b), jnp.exp(b), jnp.exp(blast - b)
    return dict(last=last, e_q=e_q, e_k=e_k, e_in=e_in, e_st=e_st,
                dec=jnp.exp(blast), qe=q * e_q, ke=k * e_k, qin=q * e_in, kst=k * e_st)


def _gla_blockspecs(s, reverse_order):
    cb = GLA_CHUNKS_PER_STEP
    rows = cb * GLA_CHUNK
    nsteps = s // rows

    def rb(n):
        return (nsteps - 1 - n) if reverse_order else n

    qspec = pl.BlockSpec((rows, GLA_KW), lambda n: (rb(n), OFF_GQ // GLA_KW))
    kspec = pl.BlockSpec((rows, GLA_KW), lambda n: (rb(n), OFF_GK // GLA_KW))
    vspec = pl.BlockSpec((rows, GLA_VW), lambda n: (rb(n), OFF_GV // GLA_VW))
    gspec = pl.BlockSpec((rows, GLA_KW), lambda n: (rb(n), 0))
    ospec = pl.BlockSpec((rows, GLA_VW), lambda n: (rb(n), 0))
    sspec = pl.BlockSpec((GLA_HEADS, cb, GLA_DV, GLA_DK), lambda n: (0, rb(n), 0, 0))
    return cb, rows, nsteps, qspec, kspec, vspec, gspec, ospec, sspec


def _gla_units(cb, order_reversed):
    chunks = list(reversed(range(cb))) if order_reversed else list(range(cb))
    return [(c, h, slice(c * GLA_CHUNK, (c + 1) * GLA_CHUNK), slice(h * GLA_DK, (h + 1) * GLA_DK),
             slice(h * GLA_DV, (h + 1) * GLA_DV)) for c in chunks for h in range(GLA_HEADS)]


def _gla_fwd(proj, g, reverse, name):
    s = proj.shape[0]
    cb, rows, nsteps, qspec, kspec, vspec, gspec, ospec, sspec = _gla_blockspecs(s, reverse)

    def body(q_ref, k_ref, v_ref, g_ref, o_ref, st_ref, state):
        @pl.when(pl.program_id(0) == 0)
        def _():
            state[...] = jnp.zeros_like(state)

        allowed, _ = _chunk_masks(reverse)
        units = _gla_units(cb, reverse)
        terms = [_chunk_terms(q_ref, k_ref, g_ref, rs, hs, allowed, reverse) for _, _, rs, hs, _ in units]
        vals = [v_ref[rs, vs].astype(BF16) for _, _, rs, _, vs in units]
        raw = [(_dot_nt(t["qe"].astype(BF16), t["ke"].astype(BF16)), _dot_tn(v, t["kst"].astype(BF16)))
               for t, v in zip(terms, vals)]
        intra = [_dot(jnp.where(allowed, a, 0.0).astype(BF16), v) for (a, _), v in zip(raw, vals)]
        st = [state[h] for h in range(GLA_HEADS)]
        for (c, h, rs, _, vs), t, (_, kv), o_in in zip(units, terms, raw, intra):
            st_b = st[h].astype(BF16)
            st_ref[h, c] = st_b
            o_ref[rs, vs] = o_in + _dot_nt(t["qin"].astype(BF16), st_b)
            st[h] = st[h] * t["dec"] + kv
        for h in range(GLA_HEADS):
            state[h] = st[h]

    return pl.pallas_call(
        body, name=name, grid=(nsteps,),
        in_specs=[qspec, kspec, vspec, gspec],
        out_specs=[ospec, sspec],
        out_shape=[jax.ShapeDtypeStruct((s, GLA_VW), F32),
                   jax.ShapeDtypeStruct((GLA_HEADS, s // GLA_CHUNK, GLA_DV, GLA_DK), BF16)],
        scratch_shapes=[pltpu.VMEM((GLA_HEADS, GLA_DV, GLA_DK), F32)],
        compiler_params=_params(("arbitrary",)),
    )(proj, proj, proj, g)


def _gla_bwd(proj, g, do, states, reverse, name, merge=None):
    s = proj.shape[0]
    cb, rows, nsteps, qspec, kspec, vspec, gspec, ospec, sspec = _gla_blockspecs(s, not reverse)
    gla_cols = OFF_Z - OFF_GQ

    def body(q_ref, k_ref, v_ref, g_ref, do_ref, sp_ref, *rest):
        if merge is None:
            dq_ref, dk_ref, dv_ref, dg_ref, dstate = rest
        else:
            dq_o, dk_o, dv_o, dgr_ref, _, dp_ref, dg_ref, dstate = rest
        @pl.when(pl.program_id(0) == 0)
        def _():
            dstate[...] = jnp.zeros_like(dstate)

        allowed, seen_by = _chunk_masks(reverse)
        units = _gla_units(cb, not reverse)
        terms = [_chunk_terms(q_ref, k_ref, g_ref, rs, hs, allowed, reverse) for _, _, rs, hs, _ in units]
        vals = [v_ref[rs, vs].astype(BF16) for _, _, rs, _, vs in units]
        dos = [do_ref[rs, vs] for _, _, rs, _, vs in units]
        prevs = [sp_ref[h, c] for c, h, _, _, _ in units]
        raw = [(_dot_nt(t["qe"].astype(BF16), t["ke"].astype(BF16)), _dot_nt(do, v),
                _dot(do, sp), _dot_tn(do, t["qin"].astype(BF16)))
               for t, v, do, sp in zip(terms, vals, dos, prevs)]
        inner = []
        for t, do, (a, da, _, _) in zip(terms, dos, raw):
            da = jnp.where(allowed, da, 0.0).astype(BF16)
            inner.append((_dot(da, t["ke"].astype(BF16)), _dot_tn(da, t["qe"].astype(BF16)),
                          _dot_tn(jnp.where(allowed, a, 0.0).astype(BF16), do)))
        ds = [dstate[h] for h in range(GLA_HEADS)]
        outer = []
        for (c, h, _, _, _), t, v, sp, (_, _, _, inc) in zip(units, terms, vals, prevs, raw):
            ds_b = ds[h].astype(BF16)
            outer.append((_dot(v, ds_b), _dot_nt(t["kst"].astype(BF16), ds_b),
                          jnp.sum(sp.astype(F32) * ds[h], axis=0, keepdims=True)))
            ds[h] = ds[h] * t["dec"] + inc
        for h in range(GLA_HEADS):
            dstate[h] = ds[h]
        seen_bf = jnp.where(seen_by, 1.0, 0.0).astype(BF16)
        rowi = lax.broadcasted_iota(jnp.int32, (GLA_CHUNK, GLA_DK), 0)
        for (c, h, rs, hs, vs), t, (_, _, dqin, _), (dqe, dke, dv_in), (dkst, dv_out, ddec) in zip(
                units, terms, raw, inner, outer):
            dq = (dqe * t["e_q"] + dqin * t["e_in"]) * (GLA_DK ** -0.5)
            dk = dke * t["e_k"] + dkst * t["e_st"]
            if merge is None:
                dq_ref[rs, hs], dk_ref[rs, hs], dv_ref[rs, vs] = dq, dk, dv_in + dv_out
            else:
                lo = OFF_GK - OFF_GQ + h * GLA_DK
                dp_ref[rs, hs] = (dq + dq_o[rs, hs]).astype(BF16)
                dp_ref[rs, lo:lo + GLA_DK] = (dk + dk_o[rs, hs]).astype(BF16)
                lo = OFF_GV - OFF_GQ + h * GLA_DV
                dp_ref[rs, lo:lo + GLA_DV] = (dv_in + dv_out + dv_o[rs, vs]).astype(BF16)
            kk = dkst * t["kst"]
            db = dqe * t["qe"] - dke * t["ke"] + dqin * t["qin"] - kk
            extra = jnp.sum(kk, axis=0, keepdims=True) + ddec * t["dec"]
            db = db + jnp.where(rowi == t["last"], extra, 0.0)
            dg_ref[rs, hs] = _dot_exact(seen_bf, db)
        if merge is not None:
            dp_ref[:, OFF_GR - OFF_GQ:gla_cols] = dgr_ref[...]

    scratch = [pltpu.VMEM((GLA_HEADS, GLA_DV, GLA_DK), F32)]
    if merge is None:
        return pl.pallas_call(
            body, name=name, grid=(nsteps,),
            in_specs=[qspec, kspec, vspec, gspec, ospec, sspec],
            out_specs=[gspec, gspec, ospec, gspec],
            out_shape=[jax.ShapeDtypeStruct((s, GLA_KW), F32), jax.ShapeDtypeStruct((s, GLA_KW), F32),
                       jax.ShapeDtypeStruct((s, GLA_VW), F32), jax.ShapeDtypeStruct((s, GLA_KW), F32)],
            scratch_shapes=scratch,
            compiler_params=_params(("arbitrary",)),
        )(proj, proj, proj, g, do, states)
    dproj = merge[4]
    block = gspec.index_map
    return pl.pallas_call(
        body, name=name, grid=(nsteps,),
        in_specs=[qspec, kspec, vspec, gspec, ospec, sspec, gspec, gspec, ospec, ospec, _ANY],
        out_specs=[pl.BlockSpec((rows, gla_cols), lambda n: (block(n)[0], OFF_GQ // gla_cols)), gspec],
        out_shape=[jax.ShapeDtypeStruct(dproj.shape, dproj.dtype), jax.ShapeDtypeStruct((s, GLA_KW), F32)],
        input_output_aliases={10: 0},
        scratch_shapes=scratch,
        compiler_params=_params(("arbitrary",)),
    )(proj, proj, proj, g, do, states, *merge)


def _gla_post(o_f, o_b, proj, g, cat, name="gla_post"):
    s = o_f.shape[0]

    def body(of_ref, ob_ref, gr_ref, g_ref, _, o_ref):
        gv = g_ref[...]
        for h in range(GLA_HEADS):
            sl = slice(h * GLA_DV, (h + 1) * GLA_DV)
            osum = of_ref[:, sl] + ob_ref[:, sl]
            r = lax.rsqrt(jnp.mean(osum * osum, axis=-1, keepdims=True) + EPS)
            gr = gr_ref[:, sl]
            o_ref[:, sl] = (osum * r * gv * (gr * _sigmoid(gr))).astype(BF16)

    blk = pl.BlockSpec((ROW_BLOCK, GLA_VW), lambda i: (i, 0))
    return pl.pallas_call(
        body, name=name, grid=(s // ROW_BLOCK,),
        in_specs=[blk, blk, pl.BlockSpec((ROW_BLOCK, GLA_VW), lambda i: (i, OFF_GR // GLA_VW)),
                  pl.BlockSpec((1, GLA_DV), lambda i: (0, 0)), pl.BlockSpec(memory_space=pl.ANY)],
        out_specs=pl.BlockSpec((ROW_BLOCK, GLA_VW), lambda i: (i, ATTN_W // GLA_VW)),
        out_shape=jax.ShapeDtypeStruct(cat.shape, cat.dtype),
        input_output_aliases={4: 0},
        compiler_params=_params(("parallel",)),
    )(o_f, o_b, proj, g, cat)


def _gla_post_bwd(dcat, o_f, o_b, proj, g, name="gla_post_bwd"):
    s = o_f.shape[0]

    def body(dy_ref, of_ref, ob_ref, gr_ref, g_ref, do_ref, dgr_ref, gg_ref):
        i = pl.program_id(0)
        gv = g_ref[...]
        gg = jnp.zeros((1, GLA_DV), F32)
        for h in range(GLA_HEADS):
            sl = slice(h * GLA_DV, (h + 1) * GLA_DV)
            osum = of_ref[:, sl] + ob_ref[:, sl]
            r = lax.rsqrt(jnp.mean(osum * osum, axis=-1, keepdims=True) + EPS)
            gr, dy = gr_ref[:, sl], dy_ref[:, sl]
            sg = _sigmoid(gr)
            dgr_ref[:, sl] = (dy * (osum * r * gv) * (sg * (1.0 + gr * (1.0 - sg)))).astype(BF16)
            dn = dy * (gr * sg)
            dng = dn * gv
            c = jnp.mean(dng * osum, axis=-1, keepdims=True)
            do_ref[:, sl] = (r * dng - osum * (r * r * r * c)).astype(BF16)
            gg = gg + jnp.sum(dn * osum * r, axis=0, keepdims=True)

        @pl.when(i == 0)
        def _():
            gg_ref[...] = jnp.zeros_like(gg_ref)

        gg_ref[...] += gg

    blk = pl.BlockSpec((ROW_BLOCK, GLA_VW), lambda i: (i, 0))
    vec = pl.BlockSpec((1, GLA_DV), lambda i: (0, 0))
    return pl.pallas_call(
        body, name=name, grid=(s // ROW_BLOCK,),
        in_specs=[pl.BlockSpec((ROW_BLOCK, GLA_VW), lambda i: (i, 1)), blk, blk,
                  pl.BlockSpec((ROW_BLOCK, GLA_VW), lambda i: (i, OFF_GR // GLA_VW)), vec],
        out_specs=[blk, blk, vec],
        out_shape=[jax.ShapeDtypeStruct((s, GLA_VW), BF16), jax.ShapeDtypeStruct((s, GLA_VW), BF16),
                   jax.ShapeDtypeStruct((1, GLA_DV), F32)],
        compiler_params=_params(("arbitrary",)),
    )(dcat, o_f, o_b, proj, g)


HALO = 16


def _extended(prev_ref, cur_ref, next_ref, i, s, tr, cs):
    first, last = i == 0, i == s // tr - 1
    prev = jnp.where(first, 0.0, prev_ref[:, cs].astype(F32))
    nxt = jnp.where(last, 0.0, next_ref[:, cs].astype(F32))
    return jnp.concatenate([prev, cur_ref[:, cs].astype(F32), nxt], axis=0)


FFN_ROWS = 512
FFN_COLS = 512


FFN_CHUNK = 256


def _lagged(i, ni, multiply, finish, rotate, init):
    chunks = [slice(c, c + FFN_CHUNK) for c in range(0, FFN_COLS, FFN_CHUNK)]

    @pl.when(i == 0)
    def _():
        init()

    @pl.when(i < 2)
    def _():
        rotate([multiply(cs) for cs in chunks], chunks)

    @pl.when((i >= 2) & (i < ni))
    def _():
        new = []
        for cs in chunks:
            new.append(multiply(cs))
            finish(cs)
        rotate(new, chunks)

    @pl.when(i >= ni)
    def _():
        for cs in chunks:
            finish(cs)
        rotate(None, chunks)


def _ffn_in(n2, w_gate, w_up, conv_w, conv_b, name="ffn_in"):
    s, d = n2.shape
    f = w_gate.shape[1]
    tm, tn, edge = FFN_ROWS, FFN_COLS, SUBLANES
    ni = s // tm
    ext = tm + 2 * edge

    def body(a_ref, wg_ref, wu_ref, w_ref, b_ref, gate_ref, silu_ref, slope_ref, act_ref, g_tile, u_tile, g_tail):
        i = pl.program_id(1)

        @pl.when(i == 0)
        def _():
            g_tile[...] = jnp.zeros_like(g_tile)
            u_tile[...] = jnp.zeros_like(u_tile)
            g_tail[...] = jnp.zeros_like(g_tail)

        a = a_ref[...]
        g_new = _dot(a, wg_ref[...])
        u_new = _dot(a, wu_ref[...])
        g_old, u_old = g_tile[...], u_tile[...]
        before = jnp.where(i == 1, 0.0, g_tail[...])
        after = jnp.where(i == ni, 0.0, g_new[0:edge])
        ge = jnp.concatenate([before, g_old, after], axis=0)
        w = w_ref[...]
        conv = (w[0:1] * pltpu.roll(ge, 1, 0) + w[1:2] * ge + w[2:3] * pltpu.roll(ge, ext - 1, 0))[edge:edge + tm]
        conv = conv + b_ref[...]
        sg = _sigmoid(conv)
        silu = conv * sg
        u_f = u_old.astype(F32)
        gate_ref[...] = g_old
        silu_ref[...] = silu.astype(BF16)
        slope_ref[...] = (u_f * (sg * (1.0 + conv * (1.0 - sg)))).astype(BF16)
        act_ref[...] = (silu * u_f).astype(BF16)
        g_tail[...] = g_old[tm - edge:tm]
        g_tile[...] = g_new
        u_tile[...] = u_new.astype(BF16)

    lag = pl.BlockSpec((tm, tn), lambda j, i: (jnp.maximum(i - 1, 0), j))
    return pl.pallas_call(
        body, name=name, grid=(f // tn, ni + 1),
        in_specs=[pl.BlockSpec((tm, d), lambda j, i: (jnp.minimum(i, ni - 1), 0)),
                  pl.BlockSpec((d, tn), lambda j, i: (0, j)), pl.BlockSpec((d, tn), lambda j, i: (0, j)),
                  pl.BlockSpec((3, tn), lambda j, i: (0, j)), pl.BlockSpec((1, tn), lambda j, i: (0, j))],
        out_specs=[lag, lag, lag, lag],
        out_shape=[jax.ShapeDtypeStruct((s, f), F32)] + [jax.ShapeDtypeStruct((s, f), BF16)] * 3,
        scratch_shapes=[pltpu.VMEM((tm, tn), F32), pltpu.VMEM((tm, tn), BF16), pltpu.VMEM((edge, tn), F32)],
        compiler_params=_params(("parallel", "arbitrary")),
    )(n2, w_gate, w_up, conv_w, conv_b)


def _ffn_mid_bwd(dh2, w_down, gate, silu, slope, conv_w, name="ffn_mid_bwd"):
    s, d = dh2.shape
    f = gate.shape[1]
    tm, tn = FFN_ROWS, FFN_COLS
    ni = s // tm
    ext = tm + 2 * HALO
    per, last_halo = tm // HALO, s // HALO - 1

    def body(a_ref, wd_ref, gp, gc, gn, sp, sc, sn, silu_ref, w_ref, dg_ref, du_ref, gw_ref, gb_ref,
             d_near, d_far, d_tail):
        i = pl.program_id(1)

        def multiply(cs):
            return _dot_nt(a_ref[...], wd_ref[cs, :])

        def finish(cs):
            before = jnp.where(i == 2, 0.0, d_tail[:, cs])
            after = jnp.where(i == ni + 1, 0.0, d_near[0:HALO, cs])
            d_mid = d_far[:, cs]
            de = jnp.concatenate([before, d_mid, after], axis=0)
            ge = _extended(gp, gc, gn, i - 2, s, tm, cs)
            w = w_ref[:, cs]
            g_prev, g_next = pltpu.roll(ge, 1, 0), pltpu.roll(ge, ext - 1, 0)
            inner = slice(HALO, HALO + tm)
            du_ref[:, cs] = (d_mid * silu_ref[:, cs].astype(F32)).astype(BF16)
            dconv = de * _extended(sp, sc, sn, i - 2, s, tm, cs)
            dgate = w[0:1] * pltpu.roll(dconv, ext - 1, 0) + w[1:2] * dconv + w[2:3] * pltpu.roll(dconv, 1, 0)
            dg_ref[:, cs] = dgate[inner].astype(BF16)
            dci = dconv[inner]
            gw_ref[0:1, cs] += jnp.sum(dci * g_prev[inner], axis=0, keepdims=True)
            gw_ref[1:2, cs] += jnp.sum(dci * ge[inner], axis=0, keepdims=True)
            gw_ref[2:3, cs] += jnp.sum(dci * g_next[inner], axis=0, keepdims=True)
            gb_ref[:, cs] += jnp.sum(dci, axis=0, keepdims=True)

        def rotate(new, chunks):
            d_tail[...] = d_far[tm - HALO:tm]
            d_far[...] = d_near[...]
            if new is not None:
                for cs, d_new in zip(chunks, new):
                    d_near[:, cs] = d_new

        def init():
            for r in (d_near, d_far, d_tail, gw_ref, gb_ref):
                r[...] = jnp.zeros_like(r)

        _lagged(i, ni, multiply, finish, rotate, init)

    def tile(i):
        return jnp.maximum(i - 2, 0)

    cur = pl.BlockSpec((tm, tn), lambda j, i: (tile(i), j))
    prev = pl.BlockSpec((HALO, tn), lambda j, i: (jnp.maximum(tile(i) * per - 1, 0), j))
    nxt = pl.BlockSpec((HALO, tn), lambda j, i: (jnp.minimum((tile(i) + 1) * per, last_halo), j))
    wspec = pl.BlockSpec((3, tn), lambda j, i: (0, j))
    bspec = pl.BlockSpec((1, tn), lambda j, i: (0, j))
    return pl.pallas_call(
        body, name=name, grid=(f // tn, ni + 2),
        in_specs=[pl.BlockSpec((tm, d), lambda j, i: (jnp.minimum(i, ni - 1), 0)),
                  pl.BlockSpec((tn, d), lambda j, i: (j, 0))] + [prev, cur, nxt] * 2 + [cur, wspec],
        out_specs=[cur, cur, wspec, bspec],
        out_shape=[jax.ShapeDtypeStruct((s, f), BF16), jax.ShapeDtypeStruct((s, f), BF16),
                   jax.ShapeDtypeStruct((3, f), F32), jax.ShapeDtypeStruct((1, f), F32)],
        scratch_shapes=[pltpu.VMEM((tm, tn), F32), pltpu.VMEM((tm, tn), F32), pltpu.VMEM((HALO, tn), F32)],
        compiler_params=_params(("parallel", "arbitrary")),
    )(dh2, w_down, gate, gate, gate, slope, slope, slope, silu, conv_w)


def _local_step(x, target, w, late_weights=None, grad_sink=None, first_dep=()):
    s = x.shape[0]
    tables = _rope_tables(s)
    uf, ub = _gate_matrices(w["gf_up"], w["gb_up"])
    if grad_sink is None:
        grad_sink = lambda names, grads: ()

    n1 = _rms_fwd(x, w["norm1_g"], "norm1")
    proj = _matmul([(n1, w["w_in"])], "nn", F32, 1024, 1280, D_MODEL, "in_proj", deps=first_dep)
    qkv = _rope_fwd(proj, tables)
    branches = [_attn_fwd(*qkv[di], d, f"attn_fwd_d{d}") for di, d in enumerate(DILATIONS)]
    o_mix, ao, lse = _attn_combine([b[0] for b in branches], [b[1] for b in branches], w["attn_norm_g"])
    g_f, g_b = _gla_gates(proj, uf, ub, w["gf_b"], w["gb_b"])
    o_f, st_f = _gla_fwd(proj, g_f, False, "gla_fwd_f")
    o_b, st_b = _gla_fwd(proj, g_b, True, "gla_fwd_b")
    cat = _gla_post(o_f, o_b, proj, w["gla_norm_g"], ao)
    if late_weights is not None:
        w = {**w, **late_weights("mixer", cat)}
    h1 = _matmul([(cat, w["w_out"])], "nn", F32, 1024, 1024, D_MODEL, "out_proj", res=x)
    n2 = _rms_fwd(h1, w["norm2_g"], "norm2")
    if late_weights is not None:
        w = {**w, **late_weights("ffn", n2)}
    gate, silu, slope, act = _ffn_in(n2, w["w_gate"], w["w_up"], w["conv_w"], w["conv_b"])
    h2 = _matmul([(act, w["w_down"])], "nn", F32, 1024, 1024, 2816, "ffn_down", res=h1)
    dh2, dh2_b, loss_acc, g_final = _final_loss(h2, target, w["final_norm_g"])

    g_w_down = _matmul([(act, dh2_b)], "tn", BF16, 1408, 1024, 2048, "g_w_down")
    dep = grad_sink(["w_down"], [g_w_down])
    dgate, dup, g_conv_w, g_conv_b = _ffn_mid_bwd(dh2_b, w["w_down"], gate, silu, slope, w["conv_w"])
    g_w_gate = _matmul([(n2, dgate)], "tn", BF16, 2048, 512, 2048, "g_w_gate", deps=dep)
    g_w_up = _matmul([(n2, dup)], "tn", BF16, 2048, 512, 2048, "g_w_up")
    dep = grad_sink(["w_gate", "w_up"], [g_w_gate, g_w_up])
    dn2 = _matmul([(dgate, w["w_gate"])], "nt", F32, 1024, 1024, 2816, "d_n2_gate", deps=dep)
    dn2 = _matmul([(dup, w["w_up"])], "nt", F32, 1024, 1024, 2816, "d_n2_up", res=dn2)
    dh1, dh1_b, g_norm2 = _rms_bwd(dn2, h1, w["norm2_g"], dh2, "norm2_bwd")

    g_w_out = _matmul([(cat, dh1_b)], "tn", BF16, 2048, 1024, 2048, "g_w_out")
    dep = grad_sink(["w_out"], [g_w_out])
    dcat = _matmul([(dh1_b, w["w_out"])], "nt", F32, 1024, 1024, D_MODEL, "d_cat", deps=dep)
    do_attn, delta, g_attn_norm = _attn_prebwd(dcat, o_mix, w["attn_norm_g"])
    grads = [_attn_bwd(*qkv[di], do_attn[di], lse[di], delta[di], d, f"attn_bwd_d{d}")
             for di, d in enumerate(DILATIONS)]
    dproj = _rope_bwd(grads, tables)
    do_gla, dgr, g_gla_norm = _gla_post_bwd(dcat, o_f, o_b, proj, w["gla_norm_g"])
    dq_f, dk_f, dv_f, dg_f = _gla_bwd(proj, g_f, do_gla, st_f, False, "gla_bwd_f")
    dproj, dg_b = _gla_bwd(proj, g_b, do_gla, st_b, True, "gla_bwd_b", merge=(dq_f, dk_f, dv_f, dgr, dproj))
    dproj, g_uf, g_ub, g_gf_b, g_gb_b = _gla_gates_bwd(dg_f, dg_b, proj, uf, ub, w["gf_b"], w["gb_b"], dproj)
    g_w_in = _matmul([(n1, dproj)], "tn", BF16, 1024, 1280, 2048, "g_w_in")
    dep = grad_sink(["w_in"], [g_w_in])
    dn1 = _matmul([(dproj, w["w_in"])], "nt", F32, 1024, 2048, 1280, "d_n1", deps=dep)
    grad_x, g_norm1 = _rms_bwd(dn1, x, w["norm1_g"], dh1, "norm1_bwd", bf16_copy=False)

    g = dict(norm1_g=g_norm1, w_in=g_w_in, gf_up=g_uf[:GLA_RANK], gf_b=g_gf_b,
             gb_up=g_ub[GLA_RANK:2 * GLA_RANK], gb_b=g_gb_b, gla_norm_g=g_gla_norm, attn_norm_g=g_attn_norm,
             w_out=g_w_out, norm2_g=g_norm2, w_gate=g_w_gate, w_up=g_w_up, conv_w=g_conv_w, conv_b=g_conv_b,
             w_down=g_w_down, final_norm_g=g_final)
    return loss_acc, grad_x, g


def _me_and_peers():
    x, y, c = lax.axis_index("x"), lax.axis_index("y"), lax.axis_index("c")
    me = 4 * x + 2 * y + c
    peers = []
    for kbits in range(1, N_DEV):
        px, py, pc = x ^ (kbits >> 2 & 1), y ^ (kbits >> 1 & 1), c ^ (kbits & 1)
        peers.append(((px, py, pc), 4 * px + 2 * py + pc))
    return me, peers


_HBM = pl.BlockSpec(memory_space=pltpu.HBM)
_SEM = pl.BlockSpec(memory_space=pltpu.SEMAPHORE)
_ANY = pl.BlockSpec(memory_space=pl.ANY)
_EFFECT = pltpu.SideEffectType.DATAFLOW_SIDE_EFFECTING


def _exchange_copies(src_refs, land_refs, send_sems, recv_sems, scatter):
    me, peers = _me_and_peers()
    out = []
    for a, (src, land) in enumerate(zip(src_refs, land_refs)):
        for kk, (dev, idx) in enumerate(peers):
            out.append(pltpu.make_async_remote_copy(
                src_ref=src.at[idx] if scatter else src, dst_ref=land.at[me],
                send_sem=send_sems.at[a * (N_DEV - 1) + kk], recv_sem=recv_sems.at[a * (N_DEV - 1) + kk],
                device_id=dev, device_id_type=MESH_ID))
    return out


def _exchange_start(srcs, lands, scatter, name, deps=()):
    n, nd = len(srcs), len(deps)

    def body(*refs):
        src_refs, land_refs = refs[:n], refs[n:2 * n]
        send_sems, recv_sems = refs[2 * n + nd:2 * n + nd + 2]
        token = refs[-1]
        for cp in _exchange_copies(src_refs, land_refs, send_sems, recv_sems, scatter):
            cp.start()
        token[...] = jnp.zeros_like(token)

    outs = pl.pallas_call(
        body, name=name,
        in_specs=[_HBM] * (2 * n) + [_ANY] * nd,
        out_specs=[_SEM, _SEM] + [_HBM] * (2 * n) + [pl.BlockSpec(memory_space=pltpu.VMEM)],
        out_shape=[pltpu.SemaphoreType.DMA((n * (N_DEV - 1),)), pltpu.SemaphoreType.DMA((n * (N_DEV - 1),))]
        + [pltpu.HBM(t.shape, t.dtype) for t in srcs] + [pltpu.HBM(t.shape, t.dtype) for t in lands]
        + [jax.ShapeDtypeStruct((SUBLANES, LANES), F32)],
        input_output_aliases={i: 2 + i for i in range(2 * n)},
        compiler_params=pltpu.CompilerParams(has_side_effects=_EFFECT),
    )(*[pltpu.with_memory_space_constraint(t, pltpu.HBM) for t in list(srcs) + list(lands)], *deps)
    send_sems, recv_sems = outs[0], outs[1]
    return dict(send=send_sems, recv=recv_sems, srcs=outs[2:2 + n], lands=outs[2 + n:2 + 2 * n],
                scatter=scatter, token=outs[-1])


def _exchange_wait(started, name, after):
    n = len(started["srcs"])
    scatter = started["scatter"]

    def body(*refs):
        src_refs, land_refs = refs[:n], refs[n:2 * n]
        send_sems, recv_sems = refs[2 * n], refs[2 * n + 1]
        for cp in _exchange_copies(src_refs, land_refs, send_sems, recv_sems, scatter):
            cp.wait_send()
            cp.wait_recv()

    outs = pl.pallas_call(
        body, name=name,
        in_specs=[_HBM] * (2 * n) + [_SEM, _SEM, _ANY],
        out_specs=[_HBM] * (2 * n),
        out_shape=[pltpu.HBM(t.shape, t.dtype) for t in started["srcs"]]
        + [pltpu.HBM(t.shape, t.dtype) for t in started["lands"]],
        input_output_aliases={i: i for i in range(2 * n)},
        compiler_params=pltpu.CompilerParams(has_side_effects=_EFFECT),
    )(*started["srcs"], *started["lands"], started["send"], started["recv"], after)
    return outs[:n], outs[n:]


def _all_gather_two_level(shard, name):
    def body(x_ref, out_ref, send_sems, recv_sems, local_sem):
        x, y, c = lax.axis_index("x"), lax.axis_index("y"), lax.axis_index("c")
        me, sibling = (x, y, c), (x, y, 1 - c)
        chips = [(1 - x, y), (x, 1 - y), (1 - x, 1 - y)]

        def slot(px, py, pc):
            return out_ref.at[4 * px + 2 * py + pc]

        def copy(k, block, to, src=None):
            return pltpu.make_async_remote_copy(
                src_ref=slot(*block) if src is None else src, dst_ref=slot(*block),
                send_sem=send_sems.at[k], recv_sem=recv_sems.at[k], device_id=to, device_id_type=MESH_ID)

        mine = pltpu.make_async_copy(x_ref, slot(*me), local_sem)
        mine.start()
        first = [copy(0, me, sibling, src=x_ref)]
        first += [copy(1 + j, me, (*chip, c), src=x_ref) for j, chip in enumerate(chips)]
        for cp in first:
            cp.start()
        passed = [copy(4 + j, (*chip, c), sibling) for j, chip in enumerate(chips)]
        for j, chip in enumerate(chips):
            copy(1 + j, (*chip, c), me).wait_recv()
            passed[j].start()
        copy(0, sibling, me).wait_recv()
        for j, chip in enumerate(chips):
            copy(4 + j, (*chip, 1 - c), me).wait_recv()
        for cp in first + passed:
            cp.wait_send()
        mine.wait()

    return pl.pallas_call(
        body, name=name,
        in_specs=[_ANY], out_specs=_ANY,
        out_shape=jax.ShapeDtypeStruct((N_DEV,) + shard.shape, shard.dtype),
        scratch_shapes=[pltpu.SemaphoreType.DMA((N_DEV - 1,)), pltpu.SemaphoreType.DMA((N_DEV - 1,)),
                        pltpu.SemaphoreType.DMA],
    )(shard)


def _all_gather_vmem(vec, name):
    r = vec.shape[0]

    def body(v_ref, o_ref, send_sems, recv_sems):
        me, peers = _me_and_peers()
        o_ref[me] = v_ref[...]
        sends = []
        for kk, (dev, _) in enumerate(peers):
            cp = pltpu.make_async_remote_copy(
                src_ref=v_ref, dst_ref=o_ref.at[me],
                send_sem=send_sems.at[kk], recv_sem=recv_sems.at[kk],
                device_id=dev, device_id_type=MESH_ID)
            cp.start()
            sends.append(cp)
        for kk, (dev, idx) in enumerate(peers):
            pltpu.make_async_remote_copy(
                src_ref=v_ref, dst_ref=o_ref.at[idx],
                send_sem=send_sems.at[kk], recv_sem=recv_sems.at[kk],
                device_id=dev, device_id_type=MESH_ID).wait_recv()
        for cp in sends:
            cp.wait_send()

    return pl.pallas_call(
        body, name=name,
        in_specs=[pl.BlockSpec(memory_space=pltpu.VMEM)],
        out_specs=pl.BlockSpec(memory_space=pltpu.VMEM),
        out_shape=jax.ShapeDtypeStruct((N_DEV, r, LANES), F32),
        scratch_shapes=[pltpu.SemaphoreType.DMA((N_DEV - 1,)), pltpu.SemaphoreType.DMA((N_DEV - 1,))],
        compiler_params=pltpu.CompilerParams(vmem_limit_bytes=VMEM_LIMIT),
    )(vec)


def _adamw_math(w, g, m, v):
    m = ADAM_B1 * m + (1.0 - ADAM_B1) * g
    v = ADAM_B2 * v + (1.0 - ADAM_B2) * (g * g)
    m_hat = m / (1.0 - ADAM_B1 ** ADAM_STEP)
    v_hat = v / (1.0 - ADAM_B2 ** ADAM_STEP)
    delta = -ADAM_LR * (m_hat / (jnp.sqrt(v_hat) + ADAM_EPS) + ADAM_WD * w)
    return delta, m, v


def _adamw_sum(parts, w, m, v, tr, name, own=None, me=None):
    r, c = w.shape

    def body(*refs):
        if own is None:
            p_ref, w_ref, m_ref, v_ref, g_ref, d_ref, nm_ref, nv_ref = refs
            terms = [p_ref[kk] for kk in range(N_DEV)]
        else:
            me_ref, p_ref, own_ref, w_ref, m_ref, v_ref, g_ref, d_ref, nm_ref, nv_ref = refs
            terms = [jnp.where(me_ref[0] == kk, own_ref[0], p_ref[kk]).astype(F32) for kk in range(N_DEV)]
        g = terms[0]
        for t in terms[1:]:
            g = g + t
        g_ref[...] = g
        d_ref[...], nm_ref[...], nv_ref[...] = _adamw_math(w_ref[...], g, m_ref[...], v_ref[...])

    out_shape = [jax.ShapeDtypeStruct((r, c), F32)] * 4
    if own is None:
        blk = pl.BlockSpec((tr, c), lambda i: (i, 0))
        return pl.pallas_call(
            body, name=name, grid=(r // tr,),
            in_specs=[pl.BlockSpec((N_DEV, tr, c), lambda i: (0, i, 0)), blk, blk, blk],
            out_specs=[blk] * 4, out_shape=out_shape,
            compiler_params=_params(("parallel",)),
        )(parts, w, m, v)
    blk = pl.BlockSpec((tr, c), lambda i, me_ref: (i, 0))
    return pl.pallas_call(
        body, name=name,
        grid_spec=pltpu.PrefetchScalarGridSpec(
            num_scalar_prefetch=1, grid=(r // tr,),
            in_specs=[pl.BlockSpec((N_DEV, tr, c), lambda i, me_ref: (0, i, 0)),
                      pl.BlockSpec((1, tr, c), lambda i, me_ref: (me_ref[0], i, 0)), blk, blk, blk],
            out_specs=[blk] * 4),
        out_shape=out_shape,
        compiler_params=_params(("parallel",)),
    )(jnp.reshape(me, (1,)).astype(jnp.int32), parts, own, w, m, v)


def _slabs_to_wide(slabs, width, name):
    n, r, c = slabs.shape

    def body(i_ref, o_ref):
        for k in range(n):
            o_ref[:, c * k:c * (k + 1)] = i_ref[k]
        if width > n * c:
            o_ref[:, n * c:width] = jnp.zeros((ROW_BLOCK, width - n * c), o_ref.dtype)

    return pl.pallas_call(
        body, name=name, grid=(r // ROW_BLOCK,),
        in_specs=[pl.BlockSpec((n, ROW_BLOCK, c), lambda i: (0, i, 0))],
        out_specs=pl.BlockSpec((ROW_BLOCK, width), lambda i: (i, 0)),
        out_shape=jax.ShapeDtypeStruct((r, width), slabs.dtype),
        compiler_params=_params(("parallel",)),
    )(slabs)


def _wide_to_slabs(wide, c, name):
    r, width = wide.shape

    def body(i_ref, o_ref):
        for k in range(N_DEV):
            o_ref[k] = i_ref[:, c * k:c * (k + 1)]

    return pl.pallas_call(
        body, name=name, grid=(r // ROW_BLOCK,),
        in_specs=[pl.BlockSpec((ROW_BLOCK, width), lambda i: (i, 0))],
        out_specs=pl.BlockSpec((N_DEV, ROW_BLOCK, c), lambda i: (0, i, 0)),
        out_shape=jax.ShapeDtypeStruct((N_DEV, r, c), wide.dtype),
        compiler_params=_params(("parallel",)),
    )(wide)


_SMALL = ("norm1_g", "gf_b", "gb_b", "gla_norm_g", "attn_norm_g", "norm2_g", "conv_b", "final_norm_g",
          "gf_up", "gb_up", "conv_w")


def _pack(named):
    flat = jnp.concatenate([jnp.ravel(t).astype(F32) for t in named])
    tile = SUBLANES * LANES
    total = -(-flat.shape[0] // tile) * tile
    return jnp.pad(flat, (0, total - flat.shape[0])).reshape(total // LANES, LANES)


def _unpack(packed, shapes):
    flat = packed.reshape(-1)
    out, off = [], 0
    for shp in shapes:
        size = int(np.prod(shp))
        out.append(flat[off:off + size].reshape(shp))
        off += size
    return out


def kernel(x, norm1_g, w_in, gf_up, gf_b, gb_up, gb_b, gla_norm_g, attn_norm_g, w_out, norm2_g, w_gate, w_up, conv_w, conv_b, w_down, final_norm_g, loss_target, m_norm1_g, m_w_in, m_gf_up, m_gf_b, m_gb_up, m_gb_b, m_gla_norm_g, m_attn_norm_g, m_w_out, m_norm2_g, m_w_gate, m_w_up, m_conv_w, m_conv_b, m_w_down, m_final_norm_g, v_norm1_g, v_w_in, v_gf_up, v_gf_b, v_gb_up, v_gb_b, v_gla_norm_g, v_attn_norm_g, v_w_out, v_norm2_g, v_w_gate, v_w_up, v_conv_w, v_conv_b, v_w_down, v_final_norm_g):
    names = ("norm1_g", "w_in", "gf_up", "gf_b", "gb_up", "gb_b", "gla_norm_g", "attn_norm_g", "w_out", "norm2_g",
             "w_gate", "w_up", "conv_w", "conv_b", "w_down", "final_norm_g")
    ws = dict(zip(names, (norm1_g, w_in, gf_up, gf_b, gb_up, gb_b, gla_norm_g, attn_norm_g, w_out, norm2_g,
                          w_gate, w_up, conv_w, conv_b, w_down, final_norm_g)))
    ms = dict(zip(names, (m_norm1_g, m_w_in, m_gf_up, m_gf_b, m_gb_up, m_gb_b, m_gla_norm_g, m_attn_norm_g, m_w_out,
                          m_norm2_g, m_w_gate, m_w_up, m_conv_w, m_conv_b, m_w_down, m_final_norm_g)))
    vs = dict(zip(names, (v_norm1_g, v_w_in, v_gf_up, v_gf_b, v_gb_up, v_gb_b, v_gla_norm_g, v_attn_norm_g, v_w_out,
                          v_norm2_g, v_w_gate, v_w_up, v_conv_w, v_conv_b, v_w_down, v_final_norm_g)))
    me = 4 * lax.axis_index("x") + 2 * lax.axis_index("y") + lax.axis_index("c")
    big = ("w_in", "w_out", "w_gate", "w_up", "w_down")
    col_sharded = ("w_in", "w_gate", "w_up")

    def gather_start(group, name, deps=()):
        shards = [ws[n][0].astype(BF16) for n in group]
        lands = [lax.empty((N_DEV,) + t.shape, BF16) for t in shards]
        return _exchange_start(shards, lands, False, name, deps)

    def gather_finish(group, started, name, after):
        full = {}
        for n, own, t in zip(group, *_exchange_wait(started, name, after)):
            t = lax.dynamic_update_slice(t, own[None], (me, 0, 0))
            if n in col_sharded:
                full[n] = _slabs_to_wide(t, N_DEV * t.shape[2], "widen_" + n)
            else:
                full[n] = t.reshape(N_DEV * t.shape[1], t.shape[2])
        return full

    w_in_all = _all_gather_two_level(ws["w_in"][0].astype(BF16), "gather_w_in")
    full = {"w_in": _slabs_to_wide(w_in_all, IN_PAD, "widen_w_in")}
    late = {"mixer": ("w_out",), "ffn": ("w_gate", "w_up", "w_down")}
    started_late = {"mixer": gather_start(late["mixer"], "gather_w_out_start", deps=(full["w_in"],))}
    started_late["ffn"] = gather_start(late["ffn"], "gather_ffn_start", deps=(started_late["mixer"]["token"],))

    def late_weights(part, after):
        return gather_finish(late[part], started_late[part], "gather_" + part + "_wait", after)

    small_sharded = ("gf_up", "gb_up", "conv_w")
    sm = _all_gather_vmem(_pack([ws[n][0] for n in small_sharded]), "gather_small")
    shard_shapes = [ws[n][0].shape for n in small_sharded]
    per_dev = [_unpack(sm[d], shard_shapes) for d in range(N_DEV)]
    for i, n in enumerate(small_sharded):
        full[n] = jnp.concatenate([per_dev[d][i] for d in range(N_DEV)], axis=1)
    for n in ("norm1_g", "gf_b", "gb_b", "gla_norm_g", "attn_norm_g", "norm2_g", "conv_b"):
        full[n] = ws[n]
    full["final_norm_g"] = final_norm_g.reshape(1, D_MODEL)

    in_flight = []

    def grad_sink(group, grads):
        partials = []
        for n, t in zip(group, grads):
            t = t.astype(BF16)
            if n in col_sharded:
                t = _wide_to_slabs(t, ws[n].shape[2], "slabs_" + n)
            else:
                t = t.reshape(N_DEV, t.shape[0] // N_DEV, t.shape[1])
            partials.append(t)
        lands = [lax.empty(t.shape, t.dtype) for t in partials]
        started = _exchange_start(partials, lands, True, "exchange_" + "_".join(group) + "_start")
        in_flight.append((group, started))
        return (started["token"],)

    loss_acc, grad_x, g = _local_step(x[0], loss_target[0], full, late_weights, grad_sink,
                                      first_dep=(started_late["ffn"]["token"],))

    out = {}
    for group, started in in_flight:
        sent, landed = _exchange_wait(started, "exchange_" + "_".join(group) + "_wait", grad_x)
        for n, parts, own in zip(group, landed, sent):
            rows = ws[n].shape[1]
            tr = max(t for t in range(HALO, ROW_BLOCK + 1, HALO) if rows % t == 0)
            out[n] = _adamw_sum(parts, ws[n][0], ms[n][0], vs[n][0], tr, "adamw_" + n, own=own, me=me)

    small_full_shapes = [g[n].shape for n in _SMALL]
    gsmall = _pack([g[n] for n in _SMALL] + [loss_acc[0:1, 0:1]])
    gathered_small = _all_gather_vmem(gsmall, "gather_small_grads")

    def full_small(d):
        parts = []
        for n in _SMALL:
            t = d[n].reshape(d[n].shape[-2:]) if d[n].ndim == 3 else d[n].reshape(1, -1)
            if n in small_sharded:
                wide = jnp.zeros((t.shape[0], t.shape[1] * N_DEV), F32)
                t = lax.dynamic_update_slice_in_dim(wide, t, me * t.shape[1], axis=1)
            parts.append(t)
        return _pack(parts + [jnp.zeros((1, 1), F32)])

    rows = gsmall.shape[0]
    res_small = _adamw_sum(gathered_small, full_small(ws), full_small(ms), full_small(vs), rows, "adamw_small")
    loss = res_small[0].reshape(-1)[sum(int(np.prod(sh)) for sh in small_full_shapes)]
    unpacked = [_unpack(t, small_full_shapes) for t in res_small]
    for i, n in enumerate(_SMALL):
        vals = [u[i] for u in unpacked]
        if n in small_sharded:
            width = vals[0].shape[1] // N_DEV
            vals = [lax.dynamic_slice_in_dim(t, me * width, width, axis=1) for t in vals]
        out[n] = vals

    result = [loss, grad_x[None]]
    for kind in range(4):
        for n in names:
            result.append(out[n][kind].reshape(ws[n].shape))
    return tuple(result)
```
